```python
import math
import jax
import jax.numpy as jnp
from jax import lax
import numpy as np

D_MODEL = 2048
BATCH = 2
SEQ = 4096
DEPTH = 4
DEC_BATCH = 8
DEC_SEQ = 8
PAST_LEN = 16384
PAGE_SIZE = 128

DN_HEADS = D_MODEL // 256
DN_DK = 128
DN_DV = 128
DN_WIDTH = DN_HEADS * DN_DV
DN_CONV_CH = 2 * DN_HEADS * DN_DK + DN_WIDTH
CONV_W = 4
DN_CHUNK = 64
NSA_HEADS = D_MODEL // 256
NSA_KV_HEADS = 2
NSA_GROUP = NSA_HEADS // NSA_KV_HEADS
NSA_HD = 128
NSA_WIDTH = NSA_HEADS * NSA_HD
NSA_KV_W = NSA_KV_HEADS * NSA_HD
NSA_BLOCK = 64
NSA_TOPN = 16
WINDOW = 512
WIN_QBLOCK = 128
SLC_QBLOCK = 64
SCALE = NSA_HD ** -0.5
MIX_WIDTH = DN_WIDTH + NSA_WIDTH
NUM_BUCKETS = 32
REL_MAX_DIST = 1024
PLE_DIM = 256
EPS = 1e-6
NEG = -1e30

_SPLIT_WIDTHS = (DN_CONV_CH, DN_WIDTH, DN_HEADS, DN_HEADS, NSA_WIDTH, NSA_WIDTH,
                 2 * NSA_KV_W, 2 * NSA_KV_W, 2 * NSA_KV_W, 3 * NSA_HEADS)
N_IN = sum(_SPLIT_WIDTHS)
_OFFSETS = tuple(int(v) for v in np.cumsum(_SPLIT_WIDTHS)[:-1])

kernel_name = 'hybrid_gdn_nsa_decoder_step'


def _rmsnorm(x, g):
    xf = x.astype(jnp.float32)
    y = xf * lax.rsqrt(jnp.mean(xf * xf, axis=-1, keepdims=True) + EPS)
    return (y * g.astype(jnp.float32)).astype(x.dtype)


def _l2norm(x):
    xf = x.astype(jnp.float32)
    return xf * lax.rsqrt(jnp.sum(xf * xf, axis=-1, keepdims=True) + EPS)


def _masked_softmax(logits, mask):
    logits = jnp.where(mask, logits.astype(jnp.float32), NEG)
    m = jnp.max(logits, axis=-1, keepdims=True)
    p = jnp.where(mask, jnp.exp(logits - m), 0.0)
    return p / jnp.maximum(jnp.sum(p, axis=-1, keepdims=True), 1e-30)


def _rel_bucket(dist):
    dist = jnp.maximum(dist, 0)
    exact = NUM_BUCKETS // 2
    scaled = jnp.log(jnp.maximum(dist, 1).astype(jnp.float32) / exact) / math.log(REL_MAX_DIST / exact)
    large = jnp.minimum(exact + (scaled * (NUM_BUCKETS - exact)).astype(jnp.int32), NUM_BUCKETS - 1)
    return jnp.where(dist < exact, dist, large)


def _gated_delta_chunked(q, k, v, g, beta, S0):
    B, L, H, DK = q.shape
    DV = v.shape[-1]
    C = DN_CHUNK
    n = -(-L // C)
    pad = n * C - L

    def prep(a):
        a = jnp.pad(a, [(0, 0), (0, pad)] + [(0, 0)] * (a.ndim - 2))
        a = a.reshape((B, n, C) + a.shape[2:])
        return jnp.moveaxis(a, 3, 1)

    q, k, v, g, beta = (prep(a) for a in (q, k, v, g, beta))
    gc = jnp.cumsum(g, axis=-1)
    idx = jnp.arange(C)
    lower = idx[:, None] >= idx[None, :]
    strict = idx[:, None] > idx[None, :]
    decay = jnp.exp(jnp.where(lower, gc[..., :, None] - gc[..., None, :], NEG))
    kb = k * beta[..., None]
    A = jnp.where(strict, jnp.einsum('bhncd,bhnsd->bhncs', kb, k) * decay, 0.0)
    rhs = jnp.concatenate([v * beta[..., None], kb * jnp.exp(gc)[..., None]], axis=-1)
    sol = lax.linalg.triangular_solve(A + jnp.eye(C, dtype=jnp.float32), rhs,
                                      left_side=True, lower=True, unit_diagonal=True)
    u, w = sol[..., :DV], sol[..., DV:]
    qk = jnp.einsum('bhncd,bhnsd->bhncs', q, k) * decay
    g_last = gc[..., -1]
    k_dec = k * jnp.exp(g_last[..., None] - gc)[..., None]
    q_dec = q * jnp.exp(gc)[..., None]

    def step(S, xs):
        q_i, k_i, u_i, w_i, qk_i, gl_i = xs
        v_new = u_i - jnp.einsum('bhcd,bhde->bhce', w_i, S)
        o = jnp.einsum('bhcd,bhde->bhce', q_i, S) + jnp.einsum('bhcs,bhse->bhce', qk_i, v_new)
        S = S * jnp.exp(gl_i)[..., None, None] + jnp.einsum('bhcd,bhce->bhde', k_i, v_new)
        return S, o

    xs = tuple(jnp.moveaxis(a, 2, 0) for a in (q_dec, k_dec, u, w, qk, g_last))
    S, o = lax.scan(step, S0, xs)
    o = jnp.moveaxis(o, 0, 2).reshape(B, H, n * C, DV)[:, :, :L]
    return jnp.transpose(o, (0, 2, 1, 3)), S


def _deltanet(qkv, z, beta_raw, a_raw, conv0, S0, conv_w, A_log, dt_bias, norm_w):
    B, L, _ = qkv.shape
    xc = jnp.concatenate([conv0.astype(qkv.dtype), qkv], axis=1)
    y = xc[:, 0:L] * conv_w[0]
    for j in range(1, CONV_W):
        y = y + xc[:, j:j + L] * conv_w[j]
    y = jax.nn.silu(y)
    conv_new = xc[:, L:]
    qk_w = DN_HEADS * DN_DK
    q = _l2norm(y[..., :qk_w].reshape(B, L, DN_HEADS, DN_DK)) * (DN_DK ** -0.5)
    k = _l2norm(y[..., qk_w:2 * qk_w].reshape(B, L, DN_HEADS, DN_DK))
    v = y[..., 2 * qk_w:].reshape(B, L, DN_HEADS, DN_DV).astype(jnp.float32)
    beta = jax.nn.sigmoid(beta_raw.astype(jnp.float32))
    g = -jnp.exp(A_log.astype(jnp.float32)) * jax.nn.softplus(a_raw.astype(jnp.float32) + dt_bias.astype(jnp.float32))
    o, S = _gated_delta_chunked(q, k, v, g, beta, S0.astype(jnp.float32))
    gate = jax.nn.silu(z.astype(jnp.float32)).reshape(B, L, DN_HEADS, DN_DV)
    o = _rmsnorm(o, norm_w) * gate
    return o.reshape(B, L, DN_WIDTH).astype(qkv.dtype), conv_new, S.astype(S0.dtype)


def _compress(kv, pos_w, w1, w2):
    B, L = kv.shape[:2]
    nb = L // NSA_BLOCK
    blocks = kv[:, :nb * NSA_BLOCK].reshape(B, nb, NSA_BLOCK, NSA_KV_HEADS, NSA_HD)
    pooled = jnp.einsum('bnjgd,j->bngd', blocks, pos_w)
    return jax.nn.silu(pooled @ w1) @ w2


def _cmp_branch(q, q_pos, ck, cv, table):
    Lq = q.shape[1]
    nb = ck.shape[1]
    blk_end = jnp.arange(nb) * NSA_BLOCK + NSA_BLOCK - 1
    dist = q_pos[:, None] - blk_end[None, :]
    bias = table[_rel_bucket(dist)].reshape(Lq, nb, NSA_KV_HEADS, NSA_GROUP).transpose(0, 2, 3, 1)
    logits = jnp.einsum('bqgrd,bngd->bqgrn', q, ck).astype(jnp.float32) * SCALE + bias
    p = _masked_softmax(logits, (dist >= 0)[:, None, None, :])
    o = jnp.einsum('bqgrn,bngd->bqgrd', p.astype(cv.dtype), cv)
    return o, p


def _select(p_cmp, q_pos, nbs):
    imp = jnp.sum(p_cmp, axis=3)
    imp = jnp.pad(imp, [(0, 0), (0, 0), (0, 0), (0, nbs - imp.shape[-1])])
    b = jnp.arange(nbs)[None, :]
    cur = (q_pos // NSA_BLOCK)[:, None]
    forced = ((b == 0) | (b == cur) | (b == cur - 1))[None, :, None, :]
    causal = (b <= cur)[None, :, None, :]
    score = jnp.where(causal, jnp.where(forced, NSA_GROUP + 1.0, imp), -1.0)
    vals, idx = lax.top_k(score, min(NSA_TOPN, nbs))
    return idx, vals >= 0.0


def _slc_branch(q, q_pos, k, v, idx, valid, table):
    B, L = k.shape[:2]
    Lq = q.shape[1]
    nbs = -(-L // NSA_BLOCK)
    pad = nbs * NSA_BLOCK - L

    def blockify(a):
        a = jnp.pad(a, [(0, 0), (0, pad), (0, 0), (0, 0)])
        return a.reshape(B, nbs, NSA_BLOCK, NSA_KV_HEADS, NSA_HD).transpose(0, 3, 1, 2, 4)

    kb, vb = blockify(k), blockify(v)
    table_gr = table.reshape(NUM_BUCKETS, NSA_KV_HEADS, NSA_GROUP)
    bi = jnp.arange(B)[:, None, None, None]
    gi = jnp.arange(NSA_KV_HEADS)[None, :, None, None]

    def attend(args):
        q_c, t_c, idx_c, val_c = args
        qc = q_c.shape[1]
        idx_c = jnp.transpose(idx_c, (0, 2, 1, 3))
        val_c = jnp.transpose(val_c, (0, 2, 1, 3))
        ks = kb[bi, gi, idx_c]
        vs = vb[bi, gi, idx_c]
        kpos = idx_c[..., None] * NSA_BLOCK + jnp.arange(NSA_BLOCK)
        dist = t_c[None, None, :, None, None] - kpos
        mask = (dist >= 0) & val_c[..., None]
        bias = table_gr[_rel_bucket(dist), gi[..., None]]
        logits = jnp.einsum('bqgrd,bgqnkd->bgqrnk', q_c, ks).astype(jnp.float32) * SCALE + jnp.moveaxis(bias, -1, 3)
        p = _masked_softmax(logits.reshape(logits.shape[:4] + (-1,)), mask.reshape(B, NSA_KV_HEADS, qc, 1, -1))
        return jnp.einsum('bgqrm,bgqmd->bqgrd', p.astype(vs.dtype), vs.reshape(B, NSA_KV_HEADS, qc, -1, NSA_HD))

    qc = SLC_QBLOCK if (Lq % SLC_QBLOCK == 0 and Lq > SLC_QBLOCK) else Lq
    nc = Lq // qc

    def chunks(a):
        return jnp.moveaxis(a.reshape((B, nc, qc) + a.shape[2:]), 1, 0)

    o = lax.map(attend, (chunks(q), q_pos.reshape(nc, qc), chunks(idx), chunks(valid)))
    return jnp.moveaxis(o, 0, 1).reshape(B, Lq, NSA_KV_HEADS, NSA_GROUP, NSA_HD)


def _win_attend(q, q_pos, k, v, k_pos, table):
    dist = q_pos[:, None] - k_pos[None, :]
    mask = (dist >= 0) & (dist <= WINDOW) & (k_pos[None, :] >= 0)
    Q, K = dist.shape
    bias = table[_rel_bucket(dist)].reshape(Q, K, NSA_KV_HEADS, NSA_GROUP).transpose(2, 3, 0, 1)
    logits = jnp.einsum('bqgrd,bkgd->bgrqk', q, k).astype(jnp.float32) * SCALE + bias
    p = _masked_softmax(logits, mask)
    return jnp.einsum('bgrqk,bkgd->bqgrd', p.astype(v.dtype), v)


def _win_banded(q, k, v, table):
    B, L = q.shape[:2]
    nblk = L // WIN_QBLOCK
    kp = jnp.pad(k, [(0, 0), (WINDOW, 0), (0, 0), (0, 0)])
    vp = jnp.pad(v, [(0, 0), (WINDOW, 0), (0, 0), (0, 0)])
    qb = jnp.moveaxis(q.reshape(B, nblk, WIN_QBLOCK, NSA_KV_HEADS, NSA_GROUP, NSA_HD), 1, 0)

    def blk(args):
        q_c, c = args
        start = c * WIN_QBLOCK
        k_c = lax.dynamic_slice_in_dim(kp, start, WINDOW + WIN_QBLOCK, axis=1)
        v_c = lax.dynamic_slice_in_dim(vp, start, WINDOW + WIN_QBLOCK, axis=1)
        q_pos = start + jnp.arange(WIN_QBLOCK)
        k_pos = start - WINDOW + jnp.arange(WINDOW + WIN_QBLOCK)
        return _win_attend(q_c, q_pos, k_c, v_c, k_pos, table)

    o = lax.map(blk, (qb, jnp.arange(nblk)))
    return jnp.moveaxis(o, 0, 1).reshape(B, L, NSA_KV_HEADS, NSA_GROUP, NSA_HD)


def _nsa(q_raw, z, gate_raw, cmp_new, slc_new, win_new, cmp_past, slc_past, win_past, pos0, table, pos_w, w1, w2):
    B, L = q_raw.shape[:2]
    q = q_raw.reshape(B, L, NSA_KV_HEADS, NSA_GROUP, NSA_HD)
    q_pos = pos0 + jnp.arange(L)
    cmp_all = cmp_new if cmp_past is None else jnp.concatenate([cmp_past, cmp_new], axis=1)
    slc_all = slc_new if slc_past is None else jnp.concatenate([slc_past, slc_new], axis=1)
    ck = _compress(cmp_all[:, :, 0], pos_w[0], w1[0], w2[0])
    cv = _compress(cmp_all[:, :, 1], pos_w[1], w1[1], w2[1])
    o_cmp, p_cmp = _cmp_branch(q, q_pos, ck, cv, table)
    nbs = -(-slc_all.shape[1] // NSA_BLOCK)
    idx, valid = _select(p_cmp, q_pos, nbs)
    o_slc = _slc_branch(q, q_pos, slc_all[:, :, 0], slc_all[:, :, 1], idx, valid, table)
    if win_past is None:
        win_all = win_new
        o_win = _win_banded(q, win_new[:, :, 0], win_new[:, :, 1], table)
    else:
        win_all = jnp.concatenate([win_past, win_new], axis=1)
        k_pos = pos0 - win_past.shape[1] + jnp.arange(win_all.shape[1])
        o_win = _win_attend(q, q_pos, win_all[:, :, 0], win_all[:, :, 1], k_pos, table)
    win_state = win_all[:, -min(WINDOW, pos0 + L):]
    gates = jax.nn.sigmoid(gate_raw.astype(jnp.float32)).reshape(B, L, NSA_KV_HEADS, NSA_GROUP, 3)
    o = gates[..., 0:1] * o_cmp + gates[..., 1:2] * o_slc + gates[..., 2:3] * o_win
    o = o.reshape(B, L, NSA_WIDTH) * jax.nn.silu(z.astype(jnp.float32))
    return o.astype(q_raw.dtype), win_state


def _layer(x, p, pos0, conv0, S0, win_past, cmp_past, slc_past, lw, table):
    B, L, _ = x.shape
    h = _rmsnorm(x, lw['g_pre'])
    proj = h @ lw['w_in']
    (dn_qkv, dn_z, dn_beta, dn_a, nsa_q, nsa_z, cmp_kv, slc_kv, win_kv, nsa_gate) = jnp.split(proj, _OFFSETS, axis=-1)
    dn_out, conv_new, S_new = _deltanet(dn_qkv, dn_z, dn_beta, dn_a, conv0, S0, lw['dn_conv_w'],
                                        lw['dn_A_log'], lw['dn_dt_bias'], lw['dn_norm_w'])
    kv_shape = (B, L, 2, NSA_KV_HEADS, NSA_HD)
    cmp_new = cmp_kv.reshape(kv_shape)
    slc_new = slc_kv.reshape(kv_shape)
    win_new = win_kv.reshape(kv_shape)
    nsa_out, win_state = _nsa(nsa_q, nsa_z, nsa_gate, cmp_new, slc_new, win_new, cmp_past, slc_past, win_past,
                              pos0, table, lw['cmp_pos_w'], lw['cmp_w1'], lw['cmp_w2'])
    mixed = jnp.concatenate([dn_out, nsa_out], axis=-1) @ lw['w_out']
    x = x + _rmsnorm(mixed, lw['g_post'])
    e = _rmsnorm(p.astype(x.dtype) @ lw['w_ple_proj'], lw['g_ple'])
    x = x + jax.nn.sigmoid((x @ lw['w_ple_gate']).astype(jnp.float32)).astype(x.dtype) * e
    return x, (S_new, conv_new, win_state, cmp_new, slc_new)


def setup_inputs(seed: int = 0) -> dict:
    key = jax.random.key(seed)
    ks = iter(jax.random.split(key, 32))
    f32 = jnp.float32

    def nrm(shape, scale):
        return jax.random.normal(next(ks), shape, f32) * scale

    n_pages = PAST_LEN // PAGE_SIZE
    n_pool = (DEC_BATCH * n_pages * 5) // 4
    win_buf = min(WINDOW, PAST_LEN)
    perm = jax.random.permutation(next(ks), n_pool)
    page_table = perm[:DEC_BATCH * n_pages].reshape(DEC_BATCH, n_pages).astype(jnp.int32)
    kv_row = (2, NSA_KV_HEADS, NSA_HD)
    dt = jnp.exp(jax.random.uniform(next(ks), (DEPTH, DN_HEADS), f32, math.log(1e-3), math.log(1e-1)))
    return {
        'x_prompt': nrm((BATCH, SEQ, D_MODEL), 1.0),
        'x_sample': nrm((DEC_BATCH, DEC_SEQ, D_MODEL), 1.0),
        'state_dn_S': nrm((DEPTH, DEC_BATCH, DN_HEADS, DN_DK, DN_DV), DN_DK ** -0.5),
        'state_dn_conv': nrm((DEPTH, DEC_BATCH, CONV_W - 1, DN_CONV_CH), 1.0),
        'cache_win_kv': nrm((DEPTH, DEC_BATCH, win_buf) + kv_row, 1.0),
        'cache_cmp_kv': nrm((DEPTH, n_pool, PAGE_SIZE) + kv_row, 1.0),
        'cache_slc_kv': nrm((DEPTH, n_pool, PAGE_SIZE) + kv_row, 1.0),
        'page_table': page_table,
        'p_prompt': nrm((DEPTH, BATCH, SEQ, PLE_DIM), 1.0),
        'p_sample': nrm((DEPTH, DEC_BATCH, DEC_SEQ, PLE_DIM), 1.0),
        'rel_bias_table': nrm((NUM_BUCKETS, NSA_HEADS), 0.5),
        'w_in': nrm((DEPTH, D_MODEL, N_IN), D_MODEL ** -0.5),
        'w_out': nrm((DEPTH, MIX_WIDTH, D_MODEL), MIX_WIDTH ** -0.5),
        'g_pre': 1.0 + nrm((DEPTH, D_MODEL), 0.05),
        'g_post': 1.0 + nrm((DEPTH, D_MODEL), 0.05),
        'dn_conv_w': nrm((DEPTH, CONV_W, DN_CONV_CH), CONV_W ** -0.5),
        'dn_A_log': jnp.log(jax.random.uniform(next(ks), (DEPTH, DN_HEADS), f32, 1.0, 16.0)),
        'dn_dt_bias': jnp.log(jnp.expm1(dt)),
        'dn_norm_w': 1.0 + nrm((DEPTH, DN_DV), 0.05),
        'cmp_pos_w': (1.0 + nrm((DEPTH, 2, NSA_BLOCK), 0.1)) * NSA_BLOCK ** -0.5,
        'cmp_w1': nrm((DEPTH, 2, NSA_HD, NSA_HD), NSA_HD ** -0.5),
        'cmp_w2': nrm((DEPTH, 2, NSA_HD, NSA_HD), NSA_HD ** -0.5),
        'w_ple_proj': nrm((DEPTH, PLE_DIM, D_MODEL), PLE_DIM ** -0.5),
        'g_ple': 1.0 + nrm((DEPTH, D_MODEL), 0.05),
        'w_ple_gate': nrm((DEPTH, D_MODEL, D_MODEL), D_MODEL ** -0.5),
    }


def reference(x_prompt, x_sample, state_dn_S, state_dn_conv, cache_win_kv, cache_cmp_kv, cache_slc_kv,
              page_table, p_prompt, p_sample, rel_bias_table, w_in, w_out, g_pre, g_post, dn_conv_w,
              dn_A_log, dn_dt_bias, dn_norm_w, cmp_pos_w, cmp_w1, cmp_w2, w_ple_proj, g_ple, w_ple_gate):
    B = x_prompt.shape[0]
    Bd = x_sample.shape[0]
    past = page_table.shape[1] * PAGE_SIZE
    kv_row = (2, NSA_KV_HEADS, NSA_HD)
    conv_zero = jnp.zeros((B, CONV_W - 1, DN_CONV_CH), x_prompt.dtype)
    S_zero = jnp.zeros((B, DN_HEADS, DN_DK, DN_DV), x_prompt.dtype)
    xp, xs = x_prompt, x_sample
    st_p, st_s = [], []
    for i in range(DEPTH):
        lw = {'g_pre': g_pre[i], 'w_in': w_in[i], 'w_out': w_out[i], 'g_post': g_post[i],
              'dn_conv_w': dn_conv_w[i], 'dn_A_log': dn_A_log[i], 'dn_dt_bias': dn_dt_bias[i],
              'dn_norm_w': dn_norm_w[i], 'cmp_pos_w': cmp_pos_w[i], 'cmp_w1': cmp_w1[i], 'cmp_w2': cmp_w2[i],
              'w_ple_proj': w_ple_proj[i], 'g_ple': g_ple[i], 'w_ple_gate': w_ple_gate[i]}
        xp, sp = _layer(xp, p_prompt[i], 0, conv_zero, S_zero, None, None, None, lw, rel_bias_table)
        cmp_past = cache_cmp_kv[i][page_table].reshape((Bd, past) + kv_row)
        slc_past = cache_slc_kv[i][page_table].reshape((Bd, past) + kv_row)
        xs, ss = _layer(xs, p_sample[i], past, state_dn_conv[i], state_dn_S[i], cache_win_kv[i],
                        cmp_past, slc_past, lw, rel_bias_table)
        st_p.append(sp)
        st_s.append(ss)
    p_S = jnp.stack([s[0] for s in st_p])
    p_conv = jnp.stack([s[1] for s in st_p])
    p_win = jnp.stack([s[2] for s in st_p])
    p_cmp = jnp.stack([s[3] for s in st_p])
    p_slc = jnp.stack([s[4] for s in st_p])
    s_S = jnp.stack([s[0] for s in st_s])
    s_conv = jnp.stack([s[1] for s in st_s])
    s_win = jnp.stack([s[2] for s in st_s])
    s_cmp = jnp.stack([s[3] for s in st_s])
    s_slc = jnp.stack([s[4] for s in st_s])
    return (xp, xs, p_S, p_conv, p_win, p_cmp, p_slc, s_S, s_conv, s_win, s_cmp, s_slc)
```

```python
import functools
import math

import numpy as np
import jax
import jax.numpy as jnp
from jax import lax
from jax.experimental import pallas as pl
from jax.experimental.pallas import tpu as pltpu

F32 = jnp.float32
BF16 = jnp.bfloat16

D_MODEL = 2048
DN_HEADS = 8
DN_DK = 128
DN_DV = 128
DN_QK_W = DN_HEADS * DN_DK
DN_CONV_CH = 2 * DN_QK_W + DN_HEADS * DN_DV
CONV_W = 4
DN_CHUNK = 64
NSA_HEADS = 8
NSA_KV_HEADS = 2
NSA_GROUP = 4
NSA_HD = 128
NSA_WIDTH = NSA_HEADS * NSA_HD
NSA_KV_W = NSA_KV_HEADS * NSA_HD
NSA_BLOCK = 64
NSA_TOPN = 16
WINDOW = 512
SCALE = NSA_HD ** -0.5
NUM_BUCKETS = 32
REL_MAX_DIST = 1024
PLE_DIM = 256
PAGE_SIZE = 128
EPS = 1e-6
NEG = -1e30

VMEM_LIMIT_BYTES = 56 * 1024 * 1024
LANES = 128
SUBLANES = 8

C_QKV = 0
C_Z = 3072
C_NQ = 4096
C_NZ = 5120
C_CMP = 6144
C_SLC = 6656
C_WIN = 7168
C_SM = 7680
C_G0 = 7808
C_G1 = 7936
N_PAD = 8192
_O_BETA = DN_CONV_CH + DN_HEADS * DN_DV
_O_NQ = _O_BETA + 2 * DN_HEADS
_O_GATE = _O_NQ + 2 * NSA_WIDTH + 3 * 2 * NSA_KV_W
N_BIAS_TILES = 9


def _bucket_np(dist):
    dist = np.maximum(np.asarray(dist, np.int64), 0)
    exact = NUM_BUCKETS // 2
    scaled = np.log(np.maximum(dist, 1).astype(np.float64) / exact) / math.log(REL_MAX_DIST / exact)
    large = np.minimum(exact + (scaled * (NUM_BUCKETS - exact)).astype(np.int64), NUM_BUCKETS - 1)
    return np.where(dist < exact, dist, large).astype(np.int32)


_BUCKET_THR = tuple(int(np.argmax(_bucket_np(np.arange(2 * REL_MAX_DIST)) >= b)) for b in range(1, NUM_BUCKETS))


def _dot(a, b):
    return jnp.dot(a.astype(BF16), b.astype(BF16), preferred_element_type=F32)


def _dot_nt(a, b):
    return lax.dot_general(a.astype(BF16), b.astype(BF16), (((1,), (1,)), ((), ())), preferred_element_type=F32)


def _dot_tn(a, b):
    return lax.dot_general(a.astype(BF16), b.astype(BF16), (((0,), (0,)), ((), ())), preferred_element_type=F32)


def _dot_hi(a, b):
    return jnp.dot(a, b, preferred_element_type=F32, precision=lax.Precision.HIGHEST)


def _dot_hi_nt(a, b):
    return lax.dot_general(a, b, (((1,), (1,)), ((), ())), preferred_element_type=F32,
                           precision=lax.Precision.HIGHEST)


def _silu(x):
    return x * jax.nn.sigmoid(x)


def _params(*sem):
    return pltpu.CompilerParams(dimension_semantics=sem, vmem_limit_bytes=VMEM_LIMIT_BYTES)


def _in_proj_kernel(x_ref, g_ref, w_ref, o_ref, xn_ref):
    @pl.when(pl.program_id(1) == 0)
    def _():
        x = x_ref[...]
        y = x * lax.rsqrt(jnp.mean(x * x, axis=-1, keepdims=True) + EPS)
        xn_ref[...] = (y * g_ref[...]).astype(BF16)

    o_ref[...] = jnp.dot(xn_ref[...], w_ref[...], preferred_element_type=F32)


def _in_proj(x, g_all, w_all, layer, tm):
    m = x.shape[0]
    tn = 512
    return pl.pallas_call(
        _in_proj_kernel,
        grid=(m // tm, N_PAD // tn),
        in_specs=[pl.BlockSpec((tm, D_MODEL), lambda i, j: (i, 0)),
                  pl.BlockSpec((None, 1, D_MODEL), lambda i, j: (layer, 0, 0)),
                  pl.BlockSpec((None, D_MODEL, tn), lambda i, j: (layer, 0, j))],
        out_specs=pl.BlockSpec((tm, tn), lambda i, j: (i, j)),
        out_shape=jax.ShapeDtypeStruct((m, N_PAD), F32),
        scratch_shapes=[pltpu.VMEM((tm, D_MODEL), BF16)],
        compiler_params=_params("parallel", "arbitrary"),
        name="in_proj",
    )(x, g_all, w_all)


def _dn_kernel(qkv_ref, z_ref, sm_ref, conv0_ref, s0_ref, cw_ref, par_ref, nw_ref,
               o_ref, convn_ref, sout_ref, xs_ref, s_ref, *, chunk):
    c = pl.program_id(1)
    halo = SUBLANES

    @pl.when(c == 0)
    def _():
        xs_ref[0:halo, :] = conv0_ref[...]
        s_ref[...] = s0_ref[...]

    xs_ref[halo:halo + chunk, :] = qkv_ref[...]
    base = halo - (CONV_W - 1)
    y = xs_ref[base:base + chunk, :] * cw_ref[0:1, :]
    for j in range(1, CONV_W):
        y = y + xs_ref[base + j:base + j + chunk, :] * cw_ref[j:j + 1, :]
    y = _silu(y)
    tail = xs_ref[chunk:chunk + halo, :]
    xs_ref[0:halo, :] = tail
    convn_ref[...] = tail

    sm = sm_ref[...]
    beta_all = jax.nn.sigmoid(sm)
    xg = sm + par_ref[1:2, :]
    softplus = jnp.maximum(xg, 0.0) + jnp.log1p(jnp.exp(-jnp.abs(xg)))
    g_all = -jnp.exp(par_ref[0:1, :]) * softplus
    row = lax.broadcasted_iota(jnp.int32, (chunk, chunk), 0)
    col = lax.broadcasted_iota(jnp.int32, (chunk, chunk), 1)
    lower = row >= col
    strict = row > col
    gc_all = _dot_hi(lower.astype(F32), g_all)
    pick = (lax.broadcasted_iota(jnp.int32, (DN_HEADS, LANES), 1)
            == lax.broadcasted_iota(jnp.int32, (DN_HEADS, LANES), 0) + DN_HEADS).astype(F32)
    gc_rows = _dot_hi_nt(pick, gc_all)

    n_factors = int(math.log2(chunk))
    for h in range(DN_HEADS):
        qh = y[:, h * DN_DK:(h + 1) * DN_DK]
        kh = y[:, DN_QK_W + h * DN_DK:DN_QK_W + (h + 1) * DN_DK]
        vh = y[:, 2 * DN_QK_W + h * DN_DV:2 * DN_QK_W + (h + 1) * DN_DV]
        qn = qh * lax.rsqrt(jnp.sum(qh * qh, axis=-1, keepdims=True) + EPS) * (DN_DK ** -0.5)
        kn = kh * lax.rsqrt(jnp.sum(kh * kh, axis=-1, keepdims=True) + EPS)
        beta = beta_all[:, h:h + 1]
        gc = gc_all[:, DN_HEADS + h:DN_HEADS + h + 1]
        gr = gc_rows[h:h + 1, :]
        decay = jnp.exp(jnp.where(lower, gc - gr, NEG))
        kb = kn * beta
        a = jnp.where(strict, _dot_nt(kb, kn) * decay, 0.0)
        x = jnp.concatenate([vh * beta, kb * jnp.exp(gc)], axis=-1)
        x = x - _dot_hi(a, x)
        pw = a
        for _ in range(n_factors - 1):
            pw = _dot_hi(pw, pw)
            x = x + _dot_hi(pw, x)
        u = x[:, :DN_DV]
        w = x[:, DN_DV:]
        qk = _dot_nt(qn, kn) * decay
        g_last = gc[chunk - 1:chunk, :]
        k_dec = kn * jnp.exp(g_last - gc)
        q_dec = qn * jnp.exp(gc)
        s = s_ref[h]
        v_new = u - _dot(w, s)
        o = _dot(q_dec, s) + _dot(qk, v_new)
        s_ref[h] = s * jnp.exp(g_last) + _dot_tn(k_dec, v_new)
        on = o * lax.rsqrt(jnp.mean(o * o, axis=-1, keepdims=True) + EPS) * nw_ref[...]
        o_ref[:, h * DN_DV:(h + 1) * DN_DV] = on * _silu(z_ref[:, h * DN_DV:(h + 1) * DN_DV])

    @pl.when(c == pl.num_programs(1) - 1)
    def _():
        sout_ref[...] = s_ref[...]


def _deltanet(proj, nb, seq, chunk, conv0_all, s0_all, state_layer, cw_all, par_all, nw_all, layer):
    nc = seq // chunk
    m = nb * seq
    hs = (DN_HEADS, DN_DK, DN_DV)
    return pl.pallas_call(
        functools.partial(_dn_kernel, chunk=chunk),
        grid=(nb, nc),
        in_specs=[pl.BlockSpec((chunk, DN_CONV_CH), lambda b, c: (b * nc + c, C_QKV // DN_CONV_CH)),
                  pl.BlockSpec((chunk, DN_HEADS * DN_DV), lambda b, c: (b * nc + c, C_Z // (DN_HEADS * DN_DV))),
                  pl.BlockSpec((chunk, LANES), lambda b, c: (b * nc + c, C_SM // LANES)),
                  pl.BlockSpec((None, None, SUBLANES, DN_CONV_CH), lambda b, c: (state_layer, b, 0, 0)),
                  pl.BlockSpec((None, None) + hs, lambda b, c: (state_layer, b, 0, 0, 0)),
                  pl.BlockSpec((None, CONV_W, DN_CONV_CH), lambda b, c: (layer, 0, 0)),
                  pl.BlockSpec((None, 2, LANES), lambda b, c: (layer, 0, 0)),
                  pl.BlockSpec((None, 1, DN_DV), lambda b, c: (layer, 0, 0))],
        out_specs=[pl.BlockSpec((chunk, DN_HEADS * DN_DV), lambda b, c: (b * nc + c, 0)),
                   pl.BlockSpec((None, SUBLANES, DN_CONV_CH), lambda b, c: (b, 0, 0)),
                   pl.BlockSpec((None,) + hs, lambda b, c: (b, 0, 0, 0))],
        out_shape=[jax.ShapeDtypeStruct((m, DN_HEADS * DN_DV), F32),
                   jax.ShapeDtypeStruct((nb, SUBLANES, DN_CONV_CH), F32),
                   jax.ShapeDtypeStruct((nb,) + hs, F32)],
        scratch_shapes=[pltpu.VMEM((chunk + SUBLANES, DN_CONV_CH), F32), pltpu.VMEM(hs, F32)],
        compiler_params=_params("parallel", "arbitrary"),
        name="deltanet",
    )(proj, proj, proj, conv0_all, s0_all, cw_all, par_all, nw_all)


def _pool_kernel(x_ref, pw_ref, o_ref):
    rows = x_ref.shape[0]
    x = x_ref[...].reshape(rows // NSA_BLOCK, NSA_BLOCK, 2 * NSA_KV_W)
    o_ref[...] = jnp.sum(x * pw_ref[...][None], axis=1)


def _pool_prompt(proj, nb, seq, pw_all, layer):
    rows = min(seq, 1024)
    nr = seq // rows
    width = 2 * NSA_KV_W
    return pl.pallas_call(
        _pool_kernel,
        grid=(nb, nr),
        in_specs=[pl.BlockSpec((rows, width), lambda b, r: (b * nr + r, C_CMP // width)),
                  pl.BlockSpec((None, NSA_BLOCK, width), lambda b, r: (layer, 0, 0))],
        out_specs=pl.BlockSpec((None, rows // NSA_BLOCK, width), lambda b, r: (b, r, 0)),
        out_shape=jax.ShapeDtypeStruct((nb, seq // NSA_BLOCK, width), F32),
        compiler_params=_params("parallel", "parallel"),
        name="pool_prompt",
    )(proj, pw_all)


def _paged_pool_kernel(pt_ref, *refs, pages):
    del pt_ref
    page_refs = refs[:pages]
    pw_ref = refs[pages]
    o_ref = refs[pages + 1]
    rows = PAGE_SIZE * 4
    per_block = rows // 2 // SUBLANES
    for k in range(pages):
        prod = page_refs[k][...] * pw_ref[...]
        y = jnp.sum(prod.reshape(2, per_block, SUBLANES, NSA_HD), axis=1)
        o_ref[2 * k:2 * k + 2] = y[:, 0:4, :] + y[:, 4:8, :]


def _pool_paged(cache4, page_table, pw_all, layer, pages):
    nb, n_pages = page_table.shape
    steps = n_pages // pages
    rows = PAGE_SIZE * 4

    def page_spec(k):
        return pl.BlockSpec((None, None, rows, NSA_HD),
                            lambda b, s, pt: (layer, pt[b, s * pages + k], 0, 0))

    grid_spec = pltpu.PrefetchScalarGridSpec(
        num_scalar_prefetch=1,
        grid=(nb, steps),
        in_specs=[page_spec(k) for k in range(pages)]
        + [pl.BlockSpec((None, rows, NSA_HD), lambda b, s, pt: (layer, 0, 0))],
        out_specs=pl.BlockSpec((None, 2 * pages, 4, NSA_HD), lambda b, s, pt: (b, s, 0, 0)),
    )
    return pl.pallas_call(
        functools.partial(_paged_pool_kernel, pages=pages),
        grid_spec=grid_spec,
        out_shape=jax.ShapeDtypeStruct((nb, 2 * n_pages, 4, NSA_HD), F32),
        compiler_params=_params("parallel", "arbitrary"),
        name="pool_paged",
    )(page_table, *([cache4] * pages), pw_all)


def _cmp_mlp_kernel(p_ref, w1_ref, w2_ref, o_ref):
    for s in range(4):
        kv = s // NSA_KV_HEADS
        h = _silu(_dot_hi(p_ref[:, s * NSA_HD:(s + 1) * NSA_HD], w1_ref[kv]))
        o_ref[:, s * NSA_HD:(s + 1) * NSA_HD] = _dot_hi(h, w2_ref[kv])


def _cmp_mlp(pooled, w1_all, w2_all, layer):
    nb, n, width = pooled.shape
    wspec = pl.BlockSpec((None, 2, NSA_HD, NSA_HD), lambda b: (layer, 0, 0, 0))
    return pl.pallas_call(
        _cmp_mlp_kernel,
        grid=(nb,),
        in_specs=[pl.BlockSpec((None, n, width), lambda b: (b, 0, 0)), wspec, wspec],
        out_specs=pl.BlockSpec((None, n, width), lambda b: (b, 0, 0)),
        out_shape=jax.ShapeDtypeStruct((nb, n, width), F32),
        compiler_params=_params("parallel"),
        name="cmp_mlp",
    )(pooled, w1_all, w2_all)


def _cmp_attn_kernel(q_ref, ckv_ref, bias_ref, o_ref, sc_ref, *, pos0, tq, nb, nbs, nbp):
    qi = pl.program_id(1)
    t = pos0 + qi * tq + lax.broadcasted_iota(jnp.int32, (tq, 1), 0)
    blk = lax.broadcasted_iota(jnp.int32, (1, nb), 1)
    valid = t >= blk * NSA_BLOCK + (NSA_BLOCK - 1)
    bi = lax.broadcasted_iota(jnp.int32, (1, nbp), 1)
    cur = t // NSA_BLOCK
    forced = (bi == 0) | (bi == cur) | (bi == cur - 1)
    causal = bi <= cur
    for g in range(NSA_KV_HEADS):
        ck = ckv_ref[:, g * NSA_HD:(g + 1) * NSA_HD]
        cv = ckv_ref[:, (NSA_KV_HEADS + g) * NSA_HD:(NSA_KV_HEADS + g + 1) * NSA_HD]
        imp = jnp.zeros((tq, nb), F32)
        for r in range(NSA_GROUP):
            h = g * NSA_GROUP + r
            logits = _dot_hi_nt(q_ref[:, h * NSA_HD:(h + 1) * NSA_HD], ck) * SCALE + bias_ref[h]
            logits = jnp.where(valid, logits, NEG)
            mx = jnp.max(logits, axis=-1, keepdims=True)
            p = jnp.where(valid, jnp.exp(logits - mx), 0.0)
            p = p / jnp.maximum(jnp.sum(p, axis=-1, keepdims=True), 1e-30)
            o_ref[:, h * NSA_HD:(h + 1) * NSA_HD] = _dot_hi(p, cv)
            imp = imp + p
        if nbp > nb:
            imp = jnp.concatenate([imp, jnp.zeros((tq, nbp - nb), F32)], axis=-1)
        score = jnp.where(causal, jnp.where(forced, NSA_GROUP + 1.0, imp), -1.0)
        sc_ref[:, g * nbp:(g + 1) * nbp] = jnp.where(bi < nbs, score, -2.0)


def _cmp_attn(proj, ckv, bias_cmp, nb_batch, seq, pos0, tq, nbs, nbp):
    nq = seq // tq
    nb = ckv.shape[1]
    m = nb_batch * seq
    bias_tq = bias_cmp.shape[1] // nq
    return pl.pallas_call(
        functools.partial(_cmp_attn_kernel, pos0=pos0, tq=tq, nb=nb, nbs=nbs, nbp=nbp),
        grid=(nb_batch, nq),
        in_specs=[pl.BlockSpec((tq, NSA_WIDTH), lambda b, i: (b * nq + i, C_NQ // NSA_WIDTH)),
                  pl.BlockSpec((None, nb, 2 * NSA_KV_W), lambda b, i: (b, 0, 0)),
                  pl.BlockSpec((NSA_HEADS, bias_tq, nb), lambda b, i: (0, i, 0))],
        out_specs=[pl.BlockSpec((tq, NSA_WIDTH), lambda b, i: (b * nq + i, 0)),
                   pl.BlockSpec((tq, NSA_KV_HEADS * nbp), lambda b, i: (b * nq + i, 0))],
        out_shape=[jax.ShapeDtypeStruct((m, NSA_WIDTH), F32),
                   jax.ShapeDtypeStruct((m, NSA_KV_HEADS * nbp), F32)],
        compiler_params=_params("parallel", "parallel"),
        name="cmp_attn",
    )(proj, ckv, bias_cmp)


def _topk_kernel(s_ref, sel_ref, idx_ref, val_ref, *, nbs):
    s = s_ref[...]
    bi = lax.broadcasted_iota(jnp.int32, s.shape, 0)

    def body(j, rank):
        other = s_ref[pl.ds(j, 1), :]
        beats = (other > s) | ((other == s) & (j < bi))
        return rank + beats.astype(jnp.int32)

    rank = lax.fori_loop(0, nbs, body, jnp.zeros(s.shape, jnp.int32))
    ok = s >= 0.0
    sel_ref[...] = jnp.where((rank < NSA_TOPN) & ok, 1.0, 0.0)
    for r in range(NSA_TOPN):
        hit = rank == r
        idx_ref[r:r + 1, :] = jnp.sum(jnp.where(hit, bi, 0), axis=0, keepdims=True)
        val_ref[r:r + 1, :] = jnp.sum(jnp.where(hit & ok, 1, 0), axis=0, keepdims=True)


def _topk(score_t, nbs):
    rows, cols = score_t.shape
    tc = min(cols, 512)
    return pl.pallas_call(
        functools.partial(_topk_kernel, nbs=nbs),
        grid=(cols // tc,),
        in_specs=[pl.BlockSpec((rows, tc), lambda i: (0, i))],
        out_specs=[pl.BlockSpec((rows, tc), lambda i: (0, i)),
                   pl.BlockSpec((NSA_TOPN, tc), lambda i: (0, i)),
                   pl.BlockSpec((NSA_TOPN, tc), lambda i: (0, i))],
        out_shape=[jax.ShapeDtypeStruct((rows, cols), F32),
                   jax.ShapeDtypeStruct((NSA_TOPN, cols), jnp.int32),
                   jax.ShapeDtypeStruct((NSA_TOPN, cols), jnp.int32)],
        compiler_params=_params("parallel"),
        name="topk",
    )(score_t)


PQ = 128
PK = 512


def _softmax_rows(s, ok):
    mx = jnp.max(s, axis=-1, keepdims=True)
    p = jnp.where(ok, jnp.exp(s - mx), 0.0)
    return p, jnp.sum(p, axis=-1, keepdims=True)


def _nsa_prompt_kernel(q_ref, ks_ref, vs_ref, kw_ref, vw_ref, sel_ref, bt_ref, ocmp_ref, z_ref, gate_ref,
                       o_ref, s_ref, sw_ref, m_ref, l_ref, acc_ref, *, nq, wt):
    qi = pl.program_id(2)
    rows = NSA_GROUP * PQ
    q4 = jnp.concatenate([q_ref[:, r * NSA_HD:(r + 1) * NSA_HD] for r in range(NSA_GROUP)], axis=0).astype(BF16)
    t = qi * PQ + lax.broadcasted_iota(jnp.int32, (PQ, 1), 0)
    selq = sel_ref[...].astype(BF16)
    nblk = sel_ref.shape[1]
    sub = PK // PQ

    m_ref[...] = jnp.full((rows, 1), NEG, F32)
    l_ref[...] = jnp.zeros((rows, 1), F32)
    acc_ref[...] = jnp.zeros((rows, NSA_HD), F32)

    def body(kt, carry):
        k0 = pl.multiple_of(kt * PK, PK)
        k = ks_ref[pl.ds(k0, PK), :]
        v = vs_ref[pl.ds(k0, PK), :]
        s = _dot_nt(q4, k) * SCALE
        expand = (lax.broadcasted_iota(jnp.int32, (nblk, PK), 0)
                  == kt * (PK // NSA_BLOCK) + lax.broadcasted_iota(jnp.int32, (nblk, PK), 1) // NSA_BLOCK)
        picked = jnp.dot(selq, jnp.where(expand, 1.0, 0.0).astype(BF16), preferred_element_type=F32)
        kpos = k0 + lax.broadcasted_iota(jnp.int32, (1, PK), 1)
        ok = (picked > 0.5) & (kpos <= t)
        for r in range(NSA_GROUP):
            for c in range(sub):
                d = jnp.clip(qi - (kt * sub + c), 0, N_BIAS_TILES - 1)
                blk = s[r * PQ:(r + 1) * PQ, c * PQ:(c + 1) * PQ] + bt_ref[r, d]
                s_ref[r * PQ:(r + 1) * PQ, c * PQ:(c + 1) * PQ] = jnp.where(ok[:, c * PQ:(c + 1) * PQ], blk, NEG)
        sm = s_ref[...]
        ok4 = jnp.concatenate([ok] * NSA_GROUP, axis=0)
        m_old = m_ref[...]
        m_new = jnp.maximum(m_old, jnp.max(sm, axis=-1, keepdims=True))
        p = jnp.where(ok4, jnp.exp(sm - m_new), 0.0)
        alpha = jnp.exp(m_old - m_new)
        l_ref[...] = alpha * l_ref[...] + jnp.sum(p, axis=-1, keepdims=True)
        acc_ref[...] = alpha * acc_ref[...] + _dot(p, v)
        m_ref[...] = m_new
        return carry

    lax.fori_loop(0, (qi * PQ + PQ - 1) // PK + 1, body, 0)
    o_slc = acc_ref[...] / jnp.maximum(l_ref[...], 1e-30)

    wk = wt * PQ
    kst = jnp.clip(qi - (wt - 1), 0, nq - wt)
    k0 = pl.multiple_of(kst * PQ, PQ)
    s = _dot_nt(q4, kw_ref[pl.ds(k0, wk), :]) * SCALE
    dist = t - (k0 + lax.broadcasted_iota(jnp.int32, (1, wk), 1))
    ok = (dist >= 0) & (dist <= WINDOW)
    for r in range(NSA_GROUP):
        for c in range(wt):
            d = jnp.clip(qi - (kst + c), 0, N_BIAS_TILES - 1)
            blk = s[r * PQ:(r + 1) * PQ, c * PQ:(c + 1) * PQ] + bt_ref[r, d]
            sw_ref[r * PQ:(r + 1) * PQ, c * PQ:(c + 1) * PQ] = jnp.where(ok[:, c * PQ:(c + 1) * PQ], blk, NEG)
    p, den = _softmax_rows(sw_ref[...], jnp.concatenate([ok] * NSA_GROUP, axis=0))
    o_win = _dot(p, vw_ref[pl.ds(k0, wk), :]) / jnp.maximum(den, 1e-30)

    gates = jax.nn.sigmoid(gate_ref[...])
    for r in range(NSA_GROUP):
        sl = slice(r * NSA_HD, (r + 1) * NSA_HD)
        rs = slice(r * PQ, (r + 1) * PQ)
        o = (gates[:, 3 * r:3 * r + 1] * ocmp_ref[:, sl] + gates[:, 3 * r + 1:3 * r + 2] * o_slc[rs]
             + gates[:, 3 * r + 2:3 * r + 3] * o_win[rs])
        o_ref[:, sl] = o * _silu(z_ref[:, sl])


def _nsa_prompt(proj, sel, bias_tiles, o_cmp, nb_batch, seq):
    nq = seq // PQ
    wt = min(WINDOW // PQ + 1, nq)
    gw = NSA_GROUP * NSA_HD
    rows = NSA_GROUP * PQ
    nblk = sel.shape[-1]

    def kv_spec(col):
        return pl.BlockSpec((seq, NSA_HD), lambda b, g, i: (b, col // NSA_HD + g))

    return pl.pallas_call(
        functools.partial(_nsa_prompt_kernel, nq=nq, wt=wt),
        grid=(nb_batch, NSA_KV_HEADS, nq),
        in_specs=[pl.BlockSpec((PQ, gw), lambda b, g, i: (b * nq + i, C_NQ // gw + g)),
                  kv_spec(C_SLC), kv_spec(C_SLC + NSA_KV_W), kv_spec(C_WIN), kv_spec(C_WIN + NSA_KV_W),
                  pl.BlockSpec((None, PQ, nblk), lambda b, g, i: (g, b * nq + i, 0)),
                  pl.BlockSpec((NSA_GROUP, N_BIAS_TILES, PQ, PQ), lambda b, g, i: (g, 0, 0, 0)),
                  pl.BlockSpec((PQ, gw), lambda b, g, i: (b * nq + i, g)),
                  pl.BlockSpec((PQ, gw), lambda b, g, i: (b * nq + i, C_NZ // gw + g)),
                  pl.BlockSpec((PQ, LANES), lambda b, g, i: (b * nq + i, C_G0 // LANES + g))],
        out_specs=pl.BlockSpec((PQ, gw), lambda b, g, i: (b * nq + i, g)),
        out_shape=jax.ShapeDtypeStruct((nb_batch * seq, NSA_WIDTH), F32),
        scratch_shapes=[pltpu.VMEM((rows, PK), F32), pltpu.VMEM((rows, wt * PQ), F32),
                        pltpu.VMEM((rows, 1), F32), pltpu.VMEM((rows, 1), F32), pltpu.VMEM((rows, NSA_HD), F32)],
        compiler_params=_params("parallel", "parallel", "arbitrary"),
        name="nsa_prompt",
    )(proj, proj, proj, proj, proj, sel, bias_tiles, o_cmp, proj, proj)


def _bias_by_threshold(dist, tcols):
    bias = jnp.broadcast_to(tcols[:, 0:1], (tcols.shape[0], dist.shape[-1]))
    for b in range(1, NUM_BUCKETS):
        bias = jnp.where(dist >= _BUCKET_THR[b - 1], tcols[:, b:b + 1], bias)
    return bias


SLOT_CHUNKS = NSA_TOPN + 2


def _slc_sample_kernel(pt_ref, idx_ref, val_ref, *refs, past, lq):
    del pt_ref
    n_in = NSA_KV_HEADS * NSA_TOPN
    page_refs = refs[:n_in]
    q_ref, knew_ref, tcol_ref, o_ref, kc_ref, vc_ref = refs[n_in:]
    b = pl.program_id(0)
    qi = pl.program_id(1)
    t = past + qi
    n_past_blocks = past // NSA_BLOCK
    nk = SLOT_CHUNKS * NSA_BLOCK
    lane = lax.broadcasted_iota(jnp.int32, (1, nk), 1)
    slot = lane // NSA_BLOCK
    this_q = lax.broadcasted_iota(jnp.int32, (lq, 1), 0) == qi
    for g in range(NSA_KV_HEADS):
        base = ((b * lq + qi) * NSA_KV_HEADS + g) * NSA_TOPN
        blk_of_lane = jnp.where(slot == NSA_TOPN, n_past_blocks, n_past_blocks + 1)
        use_of_lane = jnp.where(slot == NSA_TOPN, 1, 0)
        for j in range(NSA_TOPN):
            n_j = idx_ref[base + j]
            use = jnp.where((val_ref[base + j] > 0) & (n_j < n_past_blocks), 1, 0)
            blk_of_lane = jnp.where(slot == j, n_j, blk_of_lane)
            use_of_lane = jnp.where(slot == j, use, use_of_lane)
            page = page_refs[g * NSA_TOPN + j]
            kc_ref[j * NSA_BLOCK:(j + 1) * NSA_BLOCK, :] = page[pl.ds(g, NSA_BLOCK, stride=4), :]
            vc_ref[j * NSA_BLOCK:(j + 1) * NSA_BLOCK, :] = page[pl.ds(NSA_KV_HEADS + g, NSA_BLOCK, stride=4), :]
        new0 = NSA_TOPN * NSA_BLOCK
        kc_ref[new0:, :] = jnp.zeros((nk - new0, NSA_HD), F32)
        vc_ref[new0:, :] = jnp.zeros((nk - new0, NSA_HD), F32)
        kc_ref[new0:new0 + lq, :] = knew_ref[:, g * NSA_HD:(g + 1) * NSA_HD]
        vc_ref[new0:new0 + lq, :] = knew_ref[:, (NSA_KV_HEADS + g) * NSA_HD:(NSA_KV_HEADS + g + 1) * NSA_HD]
        q4 = jnp.concatenate(
            [jnp.sum(jnp.where(this_q, q_ref[:, (g * NSA_GROUP + r) * NSA_HD:(g * NSA_GROUP + r + 1) * NSA_HD], 0.0),
                     axis=0, keepdims=True)
             for r in range(NSA_GROUP)] + [jnp.zeros((SUBLANES - NSA_GROUP, NSA_HD), F32)], axis=0)
        dist = t - (blk_of_lane * NSA_BLOCK + lane % NSA_BLOCK)
        ok = (use_of_lane > 0) & (dist >= 0)
        s = _dot_nt(q4, kc_ref[...]) * SCALE + _bias_by_threshold(dist, tcol_ref[g])
        s = jnp.where(ok, s, NEG)
        p, den = _softmax_rows(s, ok)
        o = _dot(p, vc_ref[...]) / jnp.maximum(den, 1e-30)
        for r in range(NSA_GROUP):
            h = g * NSA_GROUP + r
            o_ref[:, h * NSA_HD:(h + 1) * NSA_HD] = o[r:r + 1, :]


def _slc_sample(proj, cache4, page_table, idx_flat, val_flat, tcols, layer, nb_batch, lq, past):
    rows = NSA_BLOCK * 4
    n_past_blocks = past // NSA_BLOCK

    def page_spec(g, j):
        def index_map(b, i, pt, idx, val):
            n = jnp.minimum(idx[((b * lq + i) * NSA_KV_HEADS + g) * NSA_TOPN + j], n_past_blocks - 1)
            return (layer, pt[b, n // 2], n % 2, 0)
        return pl.BlockSpec((None, None, rows, NSA_HD), index_map)

    grid_spec = pltpu.PrefetchScalarGridSpec(
        num_scalar_prefetch=3,
        grid=(nb_batch, lq),
        in_specs=[page_spec(g, j) for g in range(NSA_KV_HEADS) for j in range(NSA_TOPN)]
        + [pl.BlockSpec((lq, NSA_WIDTH), lambda b, i, *_: (b, C_NQ // NSA_WIDTH)),
           pl.BlockSpec((lq, 2 * NSA_KV_W), lambda b, i, *_: (b, C_SLC // (2 * NSA_KV_W))),
           pl.BlockSpec((NSA_KV_HEADS, SUBLANES, NUM_BUCKETS), lambda b, i, *_: (0, 0, 0))],
        out_specs=pl.BlockSpec((None, 1, NSA_WIDTH), lambda b, i, *_: (b * lq + i, 0, 0)),
        scratch_shapes=[pltpu.VMEM((SLOT_CHUNKS * NSA_BLOCK, NSA_HD), F32),
                        pltpu.VMEM((SLOT_CHUNKS * NSA_BLOCK, NSA_HD), F32)],
    )
    out = pl.pallas_call(
        functools.partial(_slc_sample_kernel, past=past, lq=lq),
        grid_spec=grid_spec,
        out_shape=jax.ShapeDtypeStruct((nb_batch * lq, 1, NSA_WIDTH), F32),
        compiler_params=_params("parallel", "arbitrary"),
        name="slc_sample",
    )(page_table, idx_flat, val_flat, *([cache4] * (NSA_KV_HEADS * NSA_TOPN)), proj, proj, tcols)
    return out.reshape(nb_batch * lq, NSA_WIDTH)


def _win_sample_kernel(q_ref, wpast_ref, wnew_ref, tcol_ref, ocmp_ref, oslc_ref, z_ref, g0_ref, g1_ref,
                       o_ref, kc_ref, vc_ref, *, past, lq, wlen):
    rows = NSA_GROUP * lq
    nk = kc_ref.shape[0]
    lane = lax.broadcasted_iota(jnp.int32, (1, nk), 1)
    kpos = past - wlen + lane
    t = past + lax.broadcasted_iota(jnp.int32, (rows, 1), 0) % lq
    dist = t - kpos
    ok = (dist >= 0) & (dist <= WINDOW) & (kpos >= 0) & (lane < wlen + lq)
    gate_refs = (g0_ref, g1_ref)
    for g in range(NSA_KV_HEADS):
        kc_ref[0:wlen, :] = wpast_ref[pl.ds(g, wlen, stride=4), :]
        vc_ref[0:wlen, :] = wpast_ref[pl.ds(NSA_KV_HEADS + g, wlen, stride=4), :]
        kc_ref[wlen:, :] = jnp.zeros((nk - wlen, NSA_HD), F32)
        vc_ref[wlen:, :] = jnp.zeros((nk - wlen, NSA_HD), F32)
        kc_ref[wlen:wlen + lq, :] = wnew_ref[:, g * NSA_HD:(g + 1) * NSA_HD]
        vc_ref[wlen:wlen + lq, :] = wnew_ref[:, (NSA_KV_HEADS + g) * NSA_HD:(NSA_KV_HEADS + g + 1) * NSA_HD]
        q4 = jnp.concatenate([q_ref[:, (g * NSA_GROUP + r) * NSA_HD:(g * NSA_GROUP + r + 1) * NSA_HD]
                              for r in range(NSA_GROUP)], axis=0)
        s = _dot_nt(q4, kc_ref[...]) * SCALE + _bias_by_threshold(dist, tcol_ref[g])
        s = jnp.where(ok, s, NEG)
        p, den = _softmax_rows(s, ok)
        o_win = _dot(p, vc_ref[...]) / jnp.maximum(den, 1e-30)
        gates = jax.nn.sigmoid(gate_refs[g][...])
        for r in range(NSA_GROUP):
            h = g * NSA_GROUP + r
            sl = slice(h * NSA_HD, (h + 1) * NSA_HD)
            o = (gates[:, 3 * r:3 * r + 1] * ocmp_ref[:, sl] + gates[:, 3 * r + 1:3 * r + 2] * oslc_ref[:, sl]
                 + gates[:, 3 * r + 2:3 * r + 3] * o_win[r * lq:(r + 1) * lq])
            o_ref[:, sl] = o * _silu(z_ref[:, sl])


def _win_sample(proj, win4, tcols_rows, o_cmp, o_slc, layer, nb_batch, lq, past):
    wlen = win4.shape[2] // 4
    nk = -(-(wlen + lq) // LANES) * LANES
    row_spec = pl.BlockSpec((lq, NSA_WIDTH), lambda b: (b, 0))
    return pl.pallas_call(
        functools.partial(_win_sample_kernel, past=past, lq=lq, wlen=wlen),
        grid=(nb_batch,),
        in_specs=[pl.BlockSpec((lq, NSA_WIDTH), lambda b: (b, C_NQ // NSA_WIDTH)),
                  pl.BlockSpec((None, None, wlen * 4, NSA_HD), lambda b: (layer, b, 0, 0)),
                  pl.BlockSpec((lq, 2 * NSA_KV_W), lambda b: (b, C_WIN // (2 * NSA_KV_W))),
                  pl.BlockSpec((NSA_KV_HEADS, NSA_GROUP * lq, NUM_BUCKETS), lambda b: (0, 0, 0)),
                  row_spec, row_spec,
                  pl.BlockSpec((lq, NSA_WIDTH), lambda b: (b, C_NZ // NSA_WIDTH)),
                  pl.BlockSpec((lq, LANES), lambda b: (b, C_G0 // LANES)),
                  pl.BlockSpec((lq, LANES), lambda b: (b, C_G1 // LANES))],
        out_specs=row_spec,
        out_shape=jax.ShapeDtypeStruct((nb_batch * lq, NSA_WIDTH), F32),
        scratch_shapes=[pltpu.VMEM((nk, NSA_HD), F32), pltpu.VMEM((nk, NSA_HD), F32)],
        compiler_params=_params("parallel"),
        name="win_sample",
    )(proj, win4, proj, tcols_rows, o_cmp, o_slc, proj, proj, proj)


def _out_kernel(dn_ref, nsa_ref, x_ref, p_ref, wo_ref, gpost_ref, wpp_ref, gple_ref, wpg_ref, o_ref):
    half = dn_ref.shape[1]
    mixed = (jnp.dot(dn_ref[...].astype(BF16), wo_ref[0:half, :], preferred_element_type=F32)
             + jnp.dot(nsa_ref[...].astype(BF16), wo_ref[half:, :], preferred_element_type=F32))
    y = mixed * lax.rsqrt(jnp.mean(mixed * mixed, axis=-1, keepdims=True) + EPS) * gpost_ref[...]
    x1 = x_ref[...] + y
    e = jnp.dot(p_ref[...].astype(BF16), wpp_ref[...], preferred_element_type=F32)
    e = e * lax.rsqrt(jnp.mean(e * e, axis=-1, keepdims=True) + EPS) * gple_ref[...]
    gate = jax.nn.sigmoid(jnp.dot(x1.astype(BF16), wpg_ref[...], preferred_element_type=F32))
    o_ref[...] = x1 + gate * e


def _out_proj(dn, nsa, x, p_all, wo_all, gpost_all, wpp_all, gple_all, wpg_all, layer, tm):
    m = x.shape[0]
    once = pl.Buffered(1)

    def wspec(k):
        return pl.BlockSpec((None, k, D_MODEL), lambda i: (layer, 0, 0), pipeline_mode=once)

    return pl.pallas_call(
        _out_kernel,
        grid=(m // tm,),
        in_specs=[pl.BlockSpec((tm, dn.shape[1]), lambda i: (i, 0)),
                  pl.BlockSpec((tm, nsa.shape[1]), lambda i: (i, 0)),
                  pl.BlockSpec((tm, D_MODEL), lambda i: (i, 0)),
                  pl.BlockSpec((None, tm, PLE_DIM), lambda i: (layer, i, 0)),
                  wspec(D_MODEL), wspec(1), wspec(PLE_DIM), wspec(1), wspec(D_MODEL)],
        out_specs=pl.BlockSpec((tm, D_MODEL), lambda i: (i, 0)),
        out_shape=jax.ShapeDtypeStruct((m, D_MODEL), F32),
        compiler_params=_params("parallel"),
        name="out_proj",
    )(dn, nsa, x, p_all, wo_all, gpost_all, wpp_all, gple_all, wpg_all)


def _relayout_w_in(w_in):
    depth = w_in.shape[0]

    def zeros(n):
        return jnp.zeros((depth, D_MODEL, n), w_in.dtype)

    half = 3 * NSA_GROUP
    parts = [w_in[:, :, 0:_O_BETA], w_in[:, :, _O_NQ:_O_GATE],
             w_in[:, :, _O_BETA:_O_NQ], zeros(LANES - 2 * DN_HEADS),
             w_in[:, :, _O_GATE:_O_GATE + half], zeros(LANES - half),
             w_in[:, :, _O_GATE + half:_O_GATE + 2 * half], zeros(LANES - half),
             zeros(N_PAD - C_G1 - LANES)]
    return jnp.concatenate(parts, axis=-1).astype(BF16)


def _bias_tiles(table):
    i = np.arange(PQ)
    d = np.arange(N_BIAS_TILES)[:, None, None] * PQ + i[None, :, None] - i[None, None, :]
    tiles = table[_bucket_np(d)]
    return jnp.transpose(tiles, (3, 0, 1, 2))


def _bias_cmp(table, pos0, lq, nb):
    dist = (pos0 + np.arange(lq))[:, None] - (np.arange(nb) * NSA_BLOCK + NSA_BLOCK - 1)[None, :]
    return jnp.transpose(table[_bucket_np(dist)], (2, 0, 1))


def kernel(x_prompt, x_sample, state_dn_S, state_dn_conv, cache_win_kv, cache_cmp_kv, cache_slc_kv, page_table,
           p_prompt, p_sample, rel_bias_table, w_in, w_out, g_pre, g_post, dn_conv_w, dn_A_log, dn_dt_bias,
           dn_norm_w, cmp_pos_w, cmp_w1, cmp_w2, w_ple_proj, g_ple, w_ple_gate):
    depth = w_in.shape[0]
    bp, seq, _ = x_prompt.shape
    bs, lq, _ = x_sample.shape
    n_pages = page_table.shape[1]
    past = n_pages * PAGE_SIZE
    n_pool = cache_cmp_kv.shape[1]
    wlen = cache_win_kv.shape[2]
    mp, ms = bp * seq, bs * lq
    assert seq % PK == 0 and seq % DN_CHUNK == 0 and lq == SUBLANES and past % NSA_BLOCK == 0
    kv_row = (2, NSA_KV_HEADS, NSA_HD)

    w_in_r = _relayout_w_in(w_in)
    w_out_b = w_out.astype(BF16)
    wpp_b = w_ple_proj.astype(BF16)
    wpg_b = w_ple_gate.astype(BF16)
    g_pre3 = g_pre[:, None, :]
    g_post3 = g_post[:, None, :]
    g_ple3 = g_ple[:, None, :]
    nw3 = dn_norm_w[:, None, :]
    lane_pad = ((0, 0), (DN_HEADS, LANES - 2 * DN_HEADS))
    dn_par = jnp.stack([jnp.pad(dn_A_log, lane_pad), jnp.pad(dn_dt_bias, lane_pad)], axis=1)
    pw_prompt = jnp.broadcast_to(cmp_pos_w[:, :, None, :, None], (depth, 2, NSA_KV_HEADS, NSA_BLOCK, NSA_HD))
    pw_prompt = jnp.transpose(pw_prompt, (0, 3, 1, 2, 4)).reshape(depth, NSA_BLOCK, 2 * NSA_KV_W)
    pw_page = jnp.broadcast_to(cmp_pos_w[:, None, None, :, :, None],
                               (depth, PAGE_SIZE // NSA_BLOCK, NSA_KV_HEADS, 2, NSA_BLOCK, NSA_HD))
    pw_page = jnp.transpose(pw_page, (0, 1, 4, 3, 2, 5)).reshape(depth, PAGE_SIZE * 4, NSA_HD)
    table = rel_bias_table.astype(F32)
    bias_tiles = _bias_tiles(table)
    nb_p = seq // NSA_BLOCK
    nb_s = past // NSA_BLOCK
    nbs_s = -(-(past + lq) // NSA_BLOCK)
    nbp_s = -(-nbs_s // LANES) * LANES
    bias_cmp_p = _bias_cmp(table, 0, seq, nb_p)
    bias_cmp_s = _bias_cmp(table, past, lq, nb_s)
    tcols = jnp.pad(table.T.reshape(NSA_KV_HEADS, NSA_GROUP, NUM_BUCKETS),
                    ((0, 0), (0, SUBLANES - NSA_GROUP), (0, 0)))
    tcols_rows = jnp.repeat(table.T.reshape(NSA_KV_HEADS, NSA_GROUP, NUM_BUCKETS), lq, axis=1)

    conv0_s = jnp.pad(state_dn_conv, ((0, 0), (0, 0), (SUBLANES - (CONV_W - 1), 0), (0, 0)))
    conv0_p = jnp.zeros((1, bp, SUBLANES, DN_CONV_CH), F32)
    s0_p = jnp.zeros((1, bp, DN_HEADS, DN_DK, DN_DV), F32)
    cmp4 = cache_cmp_kv.reshape(depth, n_pool, PAGE_SIZE * 4, NSA_HD)
    slc4 = cache_slc_kv.reshape(depth, n_pool, PAGE_SIZE * 4, NSA_HD)
    win4 = cache_win_kv.reshape(depth, bs, wlen * 4, NSA_HD)
    p_prompt3 = p_prompt.reshape(depth, mp, PLE_DIM)
    p_sample3 = p_sample.reshape(depth, ms, PLE_DIM)

    xp = x_prompt.reshape(mp, D_MODEL)
    xs = x_sample.reshape(ms, D_MODEL)
    outs_p = [[] for _ in range(5)]
    outs_s = [[] for _ in range(5)]
    tq_p = min(256, seq)
    for i in range(depth):
        proj = _in_proj(xp, g_pre3, w_in_r, i, min(1024, mp))
        dn_o, conv_n, s_n = _deltanet(proj, bp, seq, DN_CHUNK, conv0_p, s0_p, 0, dn_conv_w, dn_par, nw3, i)
        ckv = _cmp_mlp(_pool_prompt(proj, bp, seq, pw_prompt, i), cmp_w1, cmp_w2, i)
        o_cmp, score = _cmp_attn(proj, ckv, bias_cmp_p, bp, seq, 0, tq_p, nb_p, nb_p)
        score_t = jnp.transpose(score.reshape(mp, NSA_KV_HEADS, nb_p), (2, 1, 0)).reshape(nb_p, NSA_KV_HEADS * mp)
        sel_t, _, _ = _topk(score_t, nb_p)
        sel = jnp.transpose(sel_t.reshape(nb_p, NSA_KV_HEADS, mp), (1, 2, 0))
        nsa_o = _nsa_prompt(proj, sel, bias_tiles, o_cmp, bp, seq)
        xp = _out_proj(dn_o, nsa_o, xp, p_prompt3, w_out_b, g_post3, wpp_b, g_ple3, wpg_b, i, min(256, mp))
        kv_all = proj[:, C_CMP:C_WIN + 2 * NSA_KV_W].reshape(bp, seq, 3, *kv_row)
        outs_p[0].append(s_n)
        outs_p[1].append(conv_n[:, SUBLANES - (CONV_W - 1):])
        outs_p[2].append(kv_all[:, seq - min(WINDOW, seq):, 2])
        outs_p[3].append(kv_all[:, :, 0])
        outs_p[4].append(kv_all[:, :, 1])

        proj = _in_proj(xs, g_pre3, w_in_r, i, ms)
        dn_o, conv_n, s_n = _deltanet(proj, bs, lq, lq, conv0_s, state_dn_S, i, dn_conv_w, dn_par, nw3, i)
        pooled = _pool_paged(cmp4, page_table, pw_page, i, min(16, n_pages))
        ckv = _cmp_mlp(pooled.reshape(bs, nb_s, 2 * NSA_KV_W), cmp_w1, cmp_w2, i)
        o_cmp, score = _cmp_attn(proj, ckv, bias_cmp_s, bs, lq, past, lq, nbs_s, nbp_s)
        score_t = jnp.transpose(score.reshape(ms * NSA_KV_HEADS, nbp_s))
        _, idx_t, val_t = _topk(score_t, nbs_s)
        o_slc = _slc_sample(proj, slc4, page_table, idx_t.T.reshape(-1), val_t.T.reshape(-1), tcols, i,
                            bs, lq, past)
        nsa_o = _win_sample(proj, win4, tcols_rows, o_cmp, o_slc, i, bs, lq, past)
        xs = _out_proj(dn_o, nsa_o, xs, p_sample3, w_out_b, g_post3, wpp_b, g_ple3, wpg_b, i, ms)
        kv_all = proj[:, C_CMP:C_WIN + 2 * NSA_KV_W].reshape(bs, lq, 3, *kv_row)
        win_all = jnp.concatenate([cache_win_kv[i], kv_all[:, :, 2]], axis=1)
        outs_s[0].append(s_n)
        outs_s[1].append(conv_n[:, SUBLANES - (CONV_W - 1):])
        outs_s[2].append(win_all[:, win_all.shape[1] - min(WINDOW, past + lq):])
        outs_s[3].append(kv_all[:, :, 0])
        outs_s[4].append(kv_all[:, :, 1])

    return ((xp.reshape(bp, seq, D_MODEL), xs.reshape(bs, lq, D_MODEL))
            + tuple(jnp.stack(o) for o in outs_p) + tuple(jnp.stack(o) for o in outs_s))
```

```python
import functools
import math

import numpy as np
import jax
import jax.numpy as jnp
from jax import lax
from jax.experimental import pallas as pl
from jax.experimental.pallas import tpu as pltpu

F32 = jnp.float32
BF16 = jnp.bfloat16

D_MODEL = 2048
DN_HEADS = 8
DN_DK = 128
DN_DV = 128
DN_QK_W = DN_HEADS * DN_DK
DN_CONV_CH = 2 * DN_QK_W + DN_HEADS * DN_DV
CONV_W = 4
DN_CHUNK = 64
NSA_HEADS = 8
NSA_KV_HEADS = 2
NSA_GROUP = 4
NSA_HD = 128
NSA_WIDTH = NSA_HEADS * NSA_HD
NSA_KV_W = NSA_KV_HEADS * NSA_HD
NSA_BLOCK = 64
NSA_TOPN = 16
WINDOW = 512
SCALE = NSA_HD ** -0.5
NUM_BUCKETS = 32
REL_MAX_DIST = 1024
PLE_DIM = 256
PAGE_SIZE = 128
EPS = 1e-6
NEG = -1e30

VMEM_LIMIT_BYTES = 56 * 1024 * 1024
LANES = 128
SUBLANES = 8

C_QKV = 0
C_Z = 3072
C_NQ = 4096
C_NZ = 5120
C_CMP = 6144
C_SLC = 6656
C_WIN = 7168
C_SM = 7680
C_G0 = 7808
C_G1 = 7936
N_PAD = 8192
_O_BETA = DN_CONV_CH + DN_HEADS * DN_DV
_O_NQ = _O_BETA + 2 * DN_HEADS
_O_GATE = _O_NQ + 2 * NSA_WIDTH + 3 * 2 * NSA_KV_W
N_BIAS_TILES = 9


def _bucket_np(dist):
    dist = np.maximum(np.asarray(dist, np.int64), 0)
    exact = NUM_BUCKETS // 2
    scaled = np.log(np.maximum(dist, 1).astype(np.float64) / exact) / math.log(REL_MAX_DIST / exact)
    large = np.minimum(exact + (scaled * (NUM_BUCKETS - exact)).astype(np.int64), NUM_BUCKETS - 1)
    return np.where(dist < exact, dist, large).astype(np.int32)


_BUCKET_THR = tuple(int(np.argmax(_bucket_np(np.arange(2 * REL_MAX_DIST)) >= b)) for b in range(1, NUM_BUCKETS))


def _dot(a, b):
    return jnp.dot(a.astype(BF16), b.astype(BF16), preferred_element_type=F32)


def _dot_nt(a, b):
    return lax.dot_general(a.astype(BF16), b.astype(BF16), (((1,), (1,)), ((), ())), preferred_element_type=F32)


def _dot_tn(a, b):
    return lax.dot_general(a.astype(BF16), b.astype(BF16), (((0,), (0,)), ((), ())), preferred_element_type=F32)


def _dot_hi(a, b):
    return jnp.dot(a, b, preferred_element_type=F32, precision=lax.Precision.HIGHEST)


def _dot_hi_nt(a, b):
    return lax.dot_general(a, b, (((1,), (1,)), ((), ())), preferred_element_type=F32,
                           precision=lax.Precision.HIGHEST)


def _silu(x):
    return x * jax.nn.sigmoid(x)


def _params(*sem):
    return pltpu.CompilerParams(dimension_semantics=sem, vmem_limit_bytes=VMEM_LIMIT_BYTES)


def _in_proj_kernel(x_ref, g_ref, w_ref, o_ref, xn_ref):
    @pl.when(pl.program_id(1) == 0)
    def _():
        x = x_ref[...]
        y = x * lax.rsqrt(jnp.mean(x * x, axis=-1, keepdims=True) + EPS)
        xn_ref[...] = (y * g_ref[...]).astype(BF16)

    o_ref[...] = jnp.dot(xn_ref[...], w_ref[...], preferred_element_type=F32)


def _in_proj(x, g_all, w_all, layer, tm):
    m = x.shape[0]
    tn = 512
    return pl.pallas_call(
        _in_proj_kernel,
        grid=(m // tm, N_PAD // tn),
        in_specs=[pl.BlockSpec((tm, D_MODEL), lambda i, j: (i, 0)),
                  pl.BlockSpec((None, 1, D_MODEL), lambda i, j: (layer, 0, 0)),
                  pl.BlockSpec((None, D_MODEL, tn), lambda i, j: (layer, 0, j))],
        out_specs=pl.BlockSpec((tm, tn), lambda i, j: (i, j)),
        out_shape=jax.ShapeDtypeStruct((m, N_PAD), F32),
        scratch_shapes=[pltpu.VMEM((tm, D_MODEL), BF16)],
        compiler_params=_params("parallel", "arbitrary"),
        name="in_proj",
    )(x, g_all, w_all)


def _dn_prep_kernel(qkv_ref, sm_ref, conv0_ref, cw_ref, par_ref,
                    u_ref, w_ref, qd_ref, kd_ref, qk_ref, gl_ref, convn_ref, xs_ref, *, chunk, cs):
    tb = chunk * cs
    halo = SUBLANES

    @pl.when(pl.program_id(1) == 0)
    def _():
        xs_ref[0:halo, :] = conv0_ref[...]

    xs_ref[halo:halo + tb, :] = qkv_ref[...]
    base = halo - (CONV_W - 1)
    y = xs_ref[base:base + tb, :] * cw_ref[0:1, :]
    for j in range(1, CONV_W):
        y = y + xs_ref[base + j:base + j + tb, :] * cw_ref[j:j + 1, :]
    y = _silu(y)
    tail = xs_ref[tb:tb + halo, :]
    xs_ref[0:halo, :] = tail
    convn_ref[...] = tail

    sm = sm_ref[...]
    beta_all = jax.nn.sigmoid(sm)
    xg = sm + par_ref[1:2, :]
    softplus = jnp.maximum(xg, 0.0) + jnp.log1p(jnp.exp(-jnp.abs(xg)))
    g_all = -jnp.exp(par_ref[0:1, :]) * softplus
    row = lax.broadcasted_iota(jnp.int32, (tb, tb), 0)
    col = lax.broadcasted_iota(jnp.int32, (tb, tb), 1)
    in_chunk_lower = (row >= col) & (row // chunk == col // chunk)
    gc_all = _dot_hi(jnp.where(in_chunk_lower, 1.0, 0.0), g_all)
    pick = (lax.broadcasted_iota(jnp.int32, (DN_HEADS, LANES), 1)
            == lax.broadcasted_iota(jnp.int32, (DN_HEADS, LANES), 0) + DN_HEADS).astype(F32)
    gc_rows = _dot_hi_nt(pick, gc_all)
    for c in range(cs):
        gl_ref[c] = gc_all[(c + 1) * chunk - 1:(c + 1) * chunk, :]
    r64 = lax.broadcasted_iota(jnp.int32, (chunk, chunk), 0)
    c64 = lax.broadcasted_iota(jnp.int32, (chunk, chunk), 1)
    lower = r64 >= c64
    strict = r64 > c64

    lhs, kns, rhs, decays = [], [], [], []
    for h in range(DN_HEADS):
        hc = slice(h * DN_DV, (h + 1) * DN_DV)
        qh = y[:, h * DN_DK:(h + 1) * DN_DK]
        kh = y[:, DN_QK_W + h * DN_DK:DN_QK_W + (h + 1) * DN_DK]
        vh = y[:, 2 * DN_QK_W + h * DN_DV:2 * DN_QK_W + (h + 1) * DN_DV]
        qn = qh * lax.rsqrt(jnp.sum(qh * qh, axis=-1, keepdims=True) + EPS) * (DN_DK ** -0.5)
        kn = kh * lax.rsqrt(jnp.sum(kh * kh, axis=-1, keepdims=True) + EPS)
        beta = beta_all[:, h:h + 1]
        gc = gc_all[:, DN_HEADS + h:DN_HEADS + h + 1]
        egc = jnp.exp(gc)
        kb = kn * beta
        rhs_h = jnp.concatenate([vh * beta, kb * egc], axis=-1)
        qd_ref[:, hc] = (qn * egc).astype(qd_ref.dtype)
        for c in range(cs):
            rs = slice(c * chunk, (c + 1) * chunk)
            gcc = gc[rs]
            decays.append(jnp.exp(jnp.where(lower, gcc - gc_rows[h:h + 1, rs], NEG)))
            lhs.append(jnp.concatenate([kb[rs], qn[rs]], axis=0))
            kns.append(kn[rs])
            rhs.append(rhs_h[rs])
            kd_ref[rs, hc] = (kn[rs] * jnp.exp(gcc[chunk - 1:chunk, :] - gcc)).astype(kd_ref.dtype)
    decay = jnp.stack(decays)
    both = jnp.einsum('nid,njd->nij', jnp.stack(lhs).astype(BF16), jnp.stack(kns).astype(BF16),
                      preferred_element_type=F32)
    a = jnp.where(strict[None], both[:, :chunk] * decay, 0.0)
    qk = both[:, chunk:] * decay

    def bmm(p, q):
        return jnp.einsum('nij,njk->nik', p, q, preferred_element_type=F32)

    def split(v):
        hi = v.astype(BF16)
        return hi, (v - hi.astype(F32)).astype(BF16)

    def bmm3(p, q):
        (ph, pl_), (qh, ql) = p, q
        return bmm(ph, qh) + (bmm(ph, ql) + bmm(pl_, qh))

    x = jnp.stack(rhs)
    a_s = split(a)
    x = x - bmm3(a_s, split(x))
    pw = bmm3(a_s, a_s)
    x = x + bmm3(split(pw), split(x))
    for _ in range(int(math.log2(chunk)) - 2):
        pw = bmm(pw.astype(BF16), pw.astype(BF16))
        x = x + bmm(pw.astype(BF16), x.astype(BF16))
    for h in range(DN_HEADS):
        hc = slice(h * DN_DV, (h + 1) * DN_DV)
        for c in range(cs):
            rs = slice(c * chunk, (c + 1) * chunk)
            n = h * cs + c
            u_ref[rs, hc] = x[n, :, :DN_DV]
            w_ref[rs, hc] = x[n, :, DN_DV:].astype(w_ref.dtype)
            qk_ref[rs, h * chunk:(h + 1) * chunk] = qk[n]


def _dn_scan_kernel(u_ref, w_ref, qd_ref, kd_ref, qk_ref, gl_ref, z_ref, s0_ref, nw_ref,
                    o_ref, sout_ref, s_ref, *, chunk, cs):
    @pl.when(pl.program_id(1) == 0)
    def _():
        s_ref[...] = s0_ref[...]

    def heads(ref, rs, width):
        return jnp.stack([ref[rs, h * width:(h + 1) * width] for h in range(DN_HEADS)])

    def bdot(spec, p, q):
        return jnp.einsum(spec, p.astype(BF16), q.astype(BF16), preferred_element_type=F32)

    for c in range(cs):
        rs = slice(c * chunk, (c + 1) * chunk)
        decay_last = jnp.exp(gl_ref[c])
        decay_last = jnp.stack([decay_last[:, DN_HEADS + h:DN_HEADS + h + 1] for h in range(DN_HEADS)])
        s = s_ref[...]
        sb = s.astype(BF16)
        v_new = heads(u_ref, rs, DN_DV) - bdot('hcd,hde->hce', heads(w_ref, rs, DN_DK), sb)
        o = (bdot('hcd,hde->hce', heads(qd_ref, rs, DN_DK), sb)
             + bdot('hcs,hse->hce', heads(qk_ref, rs, chunk), v_new))
        s_ref[...] = s * decay_last + bdot('hcd,hce->hde', heads(kd_ref, rs, DN_DK), v_new)
        on = o * lax.rsqrt(jnp.mean(o * o, axis=-1, keepdims=True) + EPS) * nw_ref[...]
        for h in range(DN_HEADS):
            hc = slice(h * DN_DV, (h + 1) * DN_DV)
            o_ref[rs, hc] = on[h] * _silu(z_ref[rs, hc])

    @pl.when(pl.program_id(1) == pl.num_programs(1) - 1)
    def _():
        sout_ref[...] = s_ref[...]


def _deltanet(proj, nb, seq, chunk, conv0_all, s0_all, state_layer, cw_all, par_all, nw_all, layer):
    m = nb * seq
    nchunks = seq // chunk
    width = DN_HEADS * DN_DV
    hs = (DN_HEADS, DN_DK, DN_DV)
    cs_a = 2 if nchunks % 2 == 0 else 1
    na = nchunks // cs_a
    ta = cs_a * chunk
    op_dtype = BF16 if chunk % (2 * SUBLANES) == 0 else F32
    u, w, qd, kd, qk, gl, conv_n = pl.pallas_call(
        functools.partial(_dn_prep_kernel, chunk=chunk, cs=cs_a),
        grid=(nb, na),
        in_specs=[pl.BlockSpec((ta, DN_CONV_CH), lambda b, c: (b * na + c, C_QKV // DN_CONV_CH)),
                  pl.BlockSpec((ta, LANES), lambda b, c: (b * na + c, C_SM // LANES)),
                  pl.BlockSpec((None, None, SUBLANES, DN_CONV_CH), lambda b, c: (state_layer, b, 0, 0)),
                  pl.BlockSpec((None, CONV_W, DN_CONV_CH), lambda b, c: (layer, 0, 0)),
                  pl.BlockSpec((None, 2, LANES), lambda b, c: (layer, 0, 0))],
        out_specs=[pl.BlockSpec((ta, width), lambda b, c: (b * na + c, 0)),
                   pl.BlockSpec((ta, width), lambda b, c: (b * na + c, 0)),
                   pl.BlockSpec((ta, width), lambda b, c: (b * na + c, 0)),
                   pl.BlockSpec((ta, width), lambda b, c: (b * na + c, 0)),
                   pl.BlockSpec((ta, DN_HEADS * chunk), lambda b, c: (b * na + c, 0)),
                   pl.BlockSpec((cs_a, 1, LANES), lambda b, c: (b * na + c, 0, 0)),
                   pl.BlockSpec((None, SUBLANES, DN_CONV_CH), lambda b, c: (b, 0, 0))],
        out_shape=[jax.ShapeDtypeStruct((m, width), F32),
                   jax.ShapeDtypeStruct((m, width), op_dtype),
                   jax.ShapeDtypeStruct((m, width), op_dtype),
                   jax.ShapeDtypeStruct((m, width), op_dtype),
                   jax.ShapeDtypeStruct((m, DN_HEADS * chunk), F32),
                   jax.ShapeDtypeStruct((nb * nchunks, 1, LANES), F32),
                   jax.ShapeDtypeStruct((nb, SUBLANES, DN_CONV_CH), F32)],
        scratch_shapes=[pltpu.VMEM((ta + SUBLANES, DN_CONV_CH), F32)],
        compiler_params=_params("parallel", "arbitrary"),
        name="dn_prep",
    )(proj, proj, conv0_all, cw_all, par_all)

    cs_b = 4 if nchunks % 4 == 0 else 1
    nbk = nchunks // cs_b
    tbk = cs_b * chunk
    row_spec = pl.BlockSpec((tbk, width), lambda b, c: (b * nbk + c, 0))
    o, s_n = pl.pallas_call(
        functools.partial(_dn_scan_kernel, chunk=chunk, cs=cs_b),
        grid=(nb, nbk),
        in_specs=[row_spec, row_spec, row_spec, row_spec,
                  pl.BlockSpec((tbk, DN_HEADS * chunk), lambda b, c: (b * nbk + c, 0)),
                  pl.BlockSpec((cs_b, 1, LANES), lambda b, c: (b * nbk + c, 0, 0)),
                  pl.BlockSpec((tbk, width), lambda b, c: (b * nbk + c, C_Z // width)),
                  pl.BlockSpec((None, None) + hs, lambda b, c: (state_layer, b, 0, 0, 0)),
                  pl.BlockSpec((None, 1, DN_DV), lambda b, c: (layer, 0, 0))],
        out_specs=[row_spec, pl.BlockSpec((None,) + hs, lambda b, c: (b, 0, 0, 0))],
        out_shape=[jax.ShapeDtypeStruct((m, width), F32), jax.ShapeDtypeStruct((nb,) + hs, F32)],
        scratch_shapes=[pltpu.VMEM(hs, F32)],
        compiler_params=_params("parallel", "arbitrary"),
        name="dn_scan",
    )(u, w, qd, kd, qk, gl, proj, s0_all, nw_all)
    return o, conv_n, s_n


def _pool_kernel(x_ref, pw_ref, o_ref):
    rows = x_ref.shape[0]
    x = x_ref[...].reshape(rows // NSA_BLOCK, NSA_BLOCK, 2 * NSA_KV_W)
    o_ref[...] = jnp.sum(x * pw_ref[...][None], axis=1)


def _pool_prompt(proj, nb, seq, pw_all, layer):
    rows = min(seq, 1024)
    nr = seq // rows
    width = 2 * NSA_KV_W
    return pl.pallas_call(
        _pool_kernel,
        grid=(nb, nr),
        in_specs=[pl.BlockSpec((rows, width), lambda b, r: (b * nr + r, C_CMP // width)),
                  pl.BlockSpec((None, NSA_BLOCK, width), lambda b, r: (layer, 0, 0))],
        out_specs=pl.BlockSpec((None, rows // NSA_BLOCK, width), lambda b, r: (b, r, 0)),
        out_shape=jax.ShapeDtypeStruct((nb, seq // NSA_BLOCK, width), F32),
        compiler_params=_params("parallel", "parallel"),
        name="pool_prompt",
    )(proj, pw_all)


def _paged_pool_kernel(pt_ref, *refs, pages):
    del pt_ref
    page_refs = refs[:pages]
    pw_ref = refs[pages]
    o_ref = refs[pages + 1]
    rows = PAGE_SIZE * 4
    per_block = rows // 2 // SUBLANES
    for k in range(pages):
        prod = page_refs[k][...] * pw_ref[...]
        y = jnp.sum(prod.reshape(2, per_block, SUBLANES, NSA_HD), axis=1)
        o_ref[2 * k:2 * k + 2] = y[:, 0:4, :] + y[:, 4:8, :]


def _pool_paged(cache4, page_table, pw_all, layer, pages):
    nb, n_pages = page_table.shape
    steps = n_pages // pages
    rows = PAGE_SIZE * 4

    def page_spec(k):
        return pl.BlockSpec((None, None, rows, NSA_HD),
                            lambda b, s, pt: (layer, pt[b, s * pages + k], 0, 0))

    grid_spec = pltpu.PrefetchScalarGridSpec(
        num_scalar_prefetch=1,
        grid=(nb, steps),
        in_specs=[page_spec(k) for k in range(pages)]
        + [pl.BlockSpec((None, rows, NSA_HD), lambda b, s, pt: (layer, 0, 0))],
        out_specs=pl.BlockSpec((None, 2 * pages, 4, NSA_HD), lambda b, s, pt: (b, s, 0, 0)),
    )
    return pl.pallas_call(
        functools.partial(_paged_pool_kernel, pages=pages),
        grid_spec=grid_spec,
        out_shape=jax.ShapeDtypeStruct((nb, 2 * n_pages, 4, NSA_HD), F32),
        compiler_params=_params("parallel", "arbitrary"),
        name="pool_paged",
    )(page_table, *([cache4] * pages), pw_all)


def _cmp_mlp_kernel(p_ref, w1_ref, w2_ref, o_ref):
    for s in range(4):
        kv = s // NSA_KV_HEADS
        h = _silu(_dot_hi(p_ref[:, s * NSA_HD:(s + 1) * NSA_HD], w1_ref[kv]))
        o_ref[:, s * NSA_HD:(s + 1) * NSA_HD] = _dot_hi(h, w2_ref[kv])


def _cmp_mlp(pooled, w1_all, w2_all, layer):
    nb, n, width = pooled.shape
    wspec = pl.BlockSpec((None, 2, NSA_HD, NSA_HD), lambda b: (layer, 0, 0, 0))
    return pl.pallas_call(
        _cmp_mlp_kernel,
        grid=(nb,),
        in_specs=[pl.BlockSpec((None, n, width), lambda b: (b, 0, 0)), wspec, wspec],
        out_specs=pl.BlockSpec((None, n, width), lambda b: (b, 0, 0)),
        out_shape=jax.ShapeDtypeStruct((nb, n, width), F32),
        compiler_params=_params("parallel"),
        name="cmp_mlp",
    )(pooled, w1_all, w2_all)


def _select_blocks(score, st_ref, nbp):
    st_ref[...] = score.T
    halves = [st_ref[g * nbp:(g + 1) * nbp, :] for g in range(NSA_KV_HEADS)]
    bi = lax.broadcasted_iota(jnp.int32, halves[0].shape, 0)

    def body(j, ranks):
        out = []
        for g in range(NSA_KV_HEADS):
            other = st_ref[pl.ds(g * nbp + j, 1), :]
            beats = (other > halves[g]) | ((other == halves[g]) & (j < bi))
            out.append(ranks[g] + jnp.where(beats, 1, 0))
        return tuple(out)

    ranks = lax.fori_loop(0, nbp, body, tuple(jnp.zeros(bi.shape, jnp.int32) for _ in halves))
    keep = [jnp.where((ranks[g] < NSA_TOPN) & (halves[g] >= 0.0), 1.0, 0.0) for g in range(NSA_KV_HEADS)]
    return jnp.concatenate(keep, axis=0).T


def _cmp_attn_kernel(q_ref, ckv_ref, bias_ref, o_ref, sc_ref, *scratch, pos0, tq, nb, nbs, nbp, select):
    qi = pl.program_id(1)
    t = pos0 + qi * tq + lax.broadcasted_iota(jnp.int32, (tq, 1), 0)
    blk = lax.broadcasted_iota(jnp.int32, (1, nb), 1)
    valid = t >= blk * NSA_BLOCK + (NSA_BLOCK - 1)
    bi = lax.broadcasted_iota(jnp.int32, (1, nbp), 1)
    cur = t // NSA_BLOCK
    forced = (bi == 0) | (bi == cur) | (bi == cur - 1)
    causal = bi <= cur
    scores = []
    for g in range(NSA_KV_HEADS):
        ck = ckv_ref[:, g * NSA_HD:(g + 1) * NSA_HD]
        cv = ckv_ref[:, (NSA_KV_HEADS + g) * NSA_HD:(NSA_KV_HEADS + g + 1) * NSA_HD]
        imp = jnp.zeros((tq, nb), F32)
        for r in range(NSA_GROUP):
            h = g * NSA_GROUP + r
            logits = _dot_hi_nt(q_ref[:, h * NSA_HD:(h + 1) * NSA_HD], ck) * SCALE + bias_ref[h]
            logits = jnp.where(valid, logits, NEG)
            mx = jnp.max(logits, axis=-1, keepdims=True)
            p = jnp.where(valid, jnp.exp(logits - mx), 0.0)
            p = p / jnp.maximum(jnp.sum(p, axis=-1, keepdims=True), 1e-30)
            o_ref[:, h * NSA_HD:(h + 1) * NSA_HD] = _dot_hi(p, cv)
            imp = imp + p
        if nbp > nb:
            imp = jnp.concatenate([imp, jnp.zeros((tq, nbp - nb), F32)], axis=-1)
        score = jnp.where(causal, jnp.where(forced, NSA_GROUP + 1.0, imp), -1.0)
        scores.append(jnp.where(bi < nbs, score, -2.0))
    score = jnp.concatenate(scores, axis=-1)
    sc_ref[...] = _select_blocks(score, scratch[0], nbp) if select else score


def _cmp_attn(proj, ckv, bias_cmp, nb_batch, seq, pos0, tq, nbs, nbp, select):
    nq = seq // tq
    nb = ckv.shape[1]
    m = nb_batch * seq
    bias_tq = bias_cmp.shape[1] // nq
    return pl.pallas_call(
        functools.partial(_cmp_attn_kernel, pos0=pos0, tq=tq, nb=nb, nbs=nbs, nbp=nbp, select=select),
        scratch_shapes=[pltpu.VMEM((NSA_KV_HEADS * nbp, tq), F32)] if select else [],
        grid=(nb_batch, nq),
        in_specs=[pl.BlockSpec((tq, NSA_WIDTH), lambda b, i: (b * nq + i, C_NQ // NSA_WIDTH)),
                  pl.BlockSpec((None, nb, 2 * NSA_KV_W), lambda b, i: (b, 0, 0)),
                  pl.BlockSpec((NSA_HEADS, bias_tq, nb), lambda b, i: (0, i, 0))],
        out_specs=[pl.BlockSpec((tq, NSA_WIDTH), lambda b, i: (b * nq + i, 0)),
                   pl.BlockSpec((tq, NSA_KV_HEADS * nbp), lambda b, i: (b * nq + i, 0))],
        out_shape=[jax.ShapeDtypeStruct((m, NSA_WIDTH), F32),
                   jax.ShapeDtypeStruct((m, NSA_KV_HEADS * nbp), F32)],
        compiler_params=_params("parallel", "parallel"),
        name="cmp_attn",
    )(proj, ckv, bias_cmp)


def _topk_kernel(s_ref, sel_ref, idx_ref, val_ref, *, nbs):
    s = s_ref[...]
    bi = lax.broadcasted_iota(jnp.int32, s.shape, 0)

    def body(j, rank):
        other = s_ref[pl.ds(j, 1), :]
        beats = (other > s) | ((other == s) & (j < bi))
        return rank + beats.astype(jnp.int32)

    rank = lax.fori_loop(0, nbs, body, jnp.zeros(s.shape, jnp.int32))
    ok = s >= 0.0
    sel_ref[...] = jnp.where((rank < NSA_TOPN) & ok, 1.0, 0.0)
    for r in range(NSA_TOPN):
        hit = rank == r
        idx_ref[r:r + 1, :] = jnp.sum(jnp.where(hit, bi, 0), axis=0, keepdims=True)
        val_ref[r:r + 1, :] = jnp.sum(jnp.where(hit & ok, 1, 0), axis=0, keepdims=True)


def _topk(score_t, nbs):
    rows, cols = score_t.shape
    tc = min(cols, 512)
    return pl.pallas_call(
        functools.partial(_topk_kernel, nbs=nbs),
        grid=(cols // tc,),
        in_specs=[pl.BlockSpec((rows, tc), lambda i: (0, i))],
        out_specs=[pl.BlockSpec((rows, tc), lambda i: (0, i)),
                   pl.BlockSpec((NSA_TOPN, tc), lambda i: (0, i)),
                   pl.BlockSpec((NSA_TOPN, tc), lambda i: (0, i))],
        out_shape=[jax.ShapeDtypeStruct((rows, cols), F32),
                   jax.ShapeDtypeStruct((NSA_TOPN, cols), jnp.int32),
                   jax.ShapeDtypeStruct((NSA_TOPN, cols), jnp.int32)],
        compiler_params=_params("parallel"),
        name="topk",
    )(score_t)


PQ = 128
PK = 512


def _softmax_rows(s, ok):
    mx = jnp.max(s, axis=-1, keepdims=True)
    p = jnp.where(ok, jnp.exp(s - mx), 0.0)
    return p, jnp.sum(p, axis=-1, keepdims=True)


def _nsa_prompt_kernel(q_ref, ks_ref, vs_ref, kw_ref, vw_ref, sel_ref, bt_ref, ocmp_ref, z_ref, gate_ref,
                       o_ref, s_ref, sw_ref, m_ref, l_ref, acc_ref, *, nq, wt):
    qi = pl.program_id(2)
    rows = NSA_GROUP * PQ
    q4 = jnp.concatenate([q_ref[:, r * NSA_HD:(r + 1) * NSA_HD] for r in range(NSA_GROUP)], axis=0).astype(BF16)
    t = qi * PQ + lax.broadcasted_iota(jnp.int32, (PQ, 1), 0)
    selq = sel_ref[...].astype(BF16)
    nsel = sel_ref.shape[1]
    blk0 = pl.program_id(1) * (nsel // NSA_KV_HEADS)
    sub = PK // PQ

    m_ref[...] = jnp.full((rows, 1), NEG, F32)
    l_ref[...] = jnp.zeros((rows, 1), F32)
    acc_ref[...] = jnp.zeros((rows, NSA_HD), F32)

    def body(kt, carry):
        k0 = pl.multiple_of(kt * PK, PK)
        k = ks_ref[pl.ds(k0, PK), :]
        v = vs_ref[pl.ds(k0, PK), :]
        s = _dot_nt(q4, k) * SCALE
        expand = (lax.broadcasted_iota(jnp.int32, (nsel, PK), 0)
                  == blk0 + kt * (PK // NSA_BLOCK) + lax.broadcasted_iota(jnp.int32, (nsel, PK), 1) // NSA_BLOCK)
        picked = jnp.dot(selq, jnp.where(expand, 1.0, 0.0).astype(BF16), preferred_element_type=F32)
        kpos = k0 + lax.broadcasted_iota(jnp.int32, (1, PK), 1)
        ok = (picked > 0.5) & (kpos <= t)
        for r in range(NSA_GROUP):
            for c in range(sub):
                d = jnp.clip(qi - (kt * sub + c), 0, N_BIAS_TILES - 1)
                blk = s[r * PQ:(r + 1) * PQ, c * PQ:(c + 1) * PQ] + bt_ref[r, d]
                s_ref[r * PQ:(r + 1) * PQ, c * PQ:(c + 1) * PQ] = jnp.where(ok[:, c * PQ:(c + 1) * PQ], blk, NEG)
        sm = s_ref[...]
        ok4 = jnp.concatenate([ok] * NSA_GROUP, axis=0)
        m_old = m_ref[...]
        m_new = jnp.maximum(m_old, jnp.max(sm, axis=-1, keepdims=True))
        p = jnp.where(ok4, jnp.exp(sm - m_new), 0.0)
        alpha = jnp.exp(m_old - m_new)
        l_ref[...] = alpha * l_ref[...] + jnp.sum(p, axis=-1, keepdims=True)
        acc_ref[...] = alpha * acc_ref[...] + _dot(p, v)
        m_ref[...] = m_new
        return carry

    lax.fori_loop(0, (qi * PQ + PQ - 1) // PK + 1, body, 0)
    o_slc = acc_ref[...] / jnp.maximum(l_ref[...], 1e-30)

    wk = wt * PQ
    kst = jnp.clip(qi - (wt - 1), 0, nq - wt)
    k0 = pl.multiple_of(kst * PQ, PQ)
    s = _dot_nt(q4, kw_ref[pl.ds(k0, wk), :]) * SCALE
    dist = t - (k0 + lax.broadcasted_iota(jnp.int32, (1, wk), 1))
    ok = (dist >= 0) & (dist <= WINDOW)
    for r in range(NSA_GROUP):
        for c in range(wt):
            d = jnp.clip(qi - (kst + c), 0, N_BIAS_TILES - 1)
            blk = s[r * PQ:(r + 1) * PQ, c * PQ:(c + 1) * PQ] + bt_ref[r, d]
            sw_ref[r * PQ:(r + 1) * PQ, c * PQ:(c + 1) * PQ] = jnp.where(ok[:, c * PQ:(c + 1) * PQ], blk, NEG)
    p, den = _softmax_rows(sw_ref[...], jnp.concatenate([ok] * NSA_GROUP, axis=0))
    o_win = _dot(p, vw_ref[pl.ds(k0, wk), :]) / jnp.maximum(den, 1e-30)

    gates = jax.nn.sigmoid(gate_ref[...])
    for r in range(NSA_GROUP):
        sl = slice(r * NSA_HD, (r + 1) * NSA_HD)
        rs = slice(r * PQ, (r + 1) * PQ)
        o = (gates[:, 3 * r:3 * r + 1] * ocmp_ref[:, sl] + gates[:, 3 * r + 1:3 * r + 2] * o_slc[rs]
             + gates[:, 3 * r + 2:3 * r + 3] * o_win[rs])
        o_ref[:, sl] = o * _silu(z_ref[:, sl])


def _nsa_prompt(proj, sel, bias_tiles, o_cmp, nb_batch, seq):
    nq = seq // PQ
    wt = min(WINDOW // PQ + 1, nq)
    gw = NSA_GROUP * NSA_HD
    rows = NSA_GROUP * PQ
    nsel = sel.shape[-1]

    def kv_spec(col):
        return pl.BlockSpec((seq, NSA_HD), lambda b, g, i: (b, col // NSA_HD + g))

    return pl.pallas_call(
        functools.partial(_nsa_prompt_kernel, nq=nq, wt=wt),
        grid=(nb_batch, NSA_KV_HEADS, nq),
        in_specs=[pl.BlockSpec((PQ, gw), lambda b, g, i: (b * nq + i, C_NQ // gw + g)),
                  kv_spec(C_SLC), kv_spec(C_SLC + NSA_KV_W), kv_spec(C_WIN), kv_spec(C_WIN + NSA_KV_W),
                  pl.BlockSpec((PQ, nsel), lambda b, g, i: (b * nq + i, 0)),
                  pl.BlockSpec((NSA_GROUP, N_BIAS_TILES, PQ, PQ), lambda b, g, i: (g, 0, 0, 0)),
                  pl.BlockSpec((PQ, gw), lambda b, g, i: (b * nq + i, g)),
                  pl.BlockSpec((PQ, gw), lambda b, g, i: (b * nq + i, C_NZ // gw + g)),
                  pl.BlockSpec((PQ, LANES), lambda b, g, i: (b * nq + i, C_G0 // LANES + g))],
        out_specs=pl.BlockSpec((PQ, gw), lambda b, g, i: (b * nq + i, g)),
        out_shape=jax.ShapeDtypeStruct((nb_batch * seq, NSA_WIDTH), F32),
        scratch_shapes=[pltpu.VMEM((rows, PK), F32), pltpu.VMEM((rows, wt * PQ), F32),
                        pltpu.VMEM((rows, 1), F32), pltpu.VMEM((rows, 1), F32), pltpu.VMEM((rows, NSA_HD), F32)],
        compiler_params=_params("parallel", "parallel", "arbitrary"),
        name="nsa_prompt",
    )(proj, proj, proj, proj, proj, sel, bias_tiles, o_cmp, proj, proj)


def _bias_by_threshold(dist, tcols):
    bias = jnp.broadcast_to(tcols[:, 0:1], (tcols.shape[0], dist.shape[-1]))
    for b in range(1, NUM_BUCKETS):
        bias = jnp.where(dist >= _BUCKET_THR[b - 1], tcols[:, b:b + 1], bias)
    return bias


SLOT_CHUNKS = NSA_TOPN + 2


def _slc_sample_kernel(pt_ref, idx_ref, val_ref, *refs, past, lq):
    del pt_ref
    n_in = NSA_KV_HEADS * NSA_TOPN
    page_refs = refs[:n_in]
    q_ref, knew_ref, tcol_ref, o_ref, kc_ref, vc_ref = refs[n_in:]
    b = pl.program_id(0)
    qi = pl.program_id(1)
    t = past + qi
    n_past_blocks = past // NSA_BLOCK
    nk = SLOT_CHUNKS * NSA_BLOCK
    lane = lax.broadcasted_iota(jnp.int32, (1, nk), 1)
    slot = lane // NSA_BLOCK
    this_q = lax.broadcasted_iota(jnp.int32, (lq, 1), 0) == qi
    for g in range(NSA_KV_HEADS):
        base = ((b * lq + qi) * NSA_KV_HEADS + g) * NSA_TOPN
        blk_of_lane = jnp.where(slot == NSA_TOPN, n_past_blocks, n_past_blocks + 1)
        use_of_lane = jnp.where(slot == NSA_TOPN, 1, 0)
        for j in range(NSA_TOPN):
            n_j = idx_ref[base + j]
            use = jnp.where((val_ref[base + j] > 0) & (n_j < n_past_blocks), 1, 0)
            blk_of_lane = jnp.where(slot == j, n_j, blk_of_lane)
            use_of_lane = jnp.where(slot == j, use, use_of_lane)
            page = page_refs[g * NSA_TOPN + j]
            kc_ref[j * NSA_BLOCK:(j + 1) * NSA_BLOCK, :] = page[pl.ds(g, NSA_BLOCK, stride=4), :]
            vc_ref[j * NSA_BLOCK:(j + 1) * NSA_BLOCK, :] = page[pl.ds(NSA_KV_HEADS + g, NSA_BLOCK, stride=4), :]
        new0 = NSA_TOPN * NSA_BLOCK
        kc_ref[new0:, :] = jnp.zeros((nk - new0, NSA_HD), F32)
        vc_ref[new0:, :] = jnp.zeros((nk - new0, NSA_HD), F32)
        kc_ref[new0:new0 + lq, :] = knew_ref[:, g * NSA_HD:(g + 1) * NSA_HD]
        vc_ref[new0:new0 + lq, :] = knew_ref[:, (NSA_KV_HEADS + g) * NSA_HD:(NSA_KV_HEADS + g + 1) * NSA_HD]
        q4 = jnp.concatenate(
            [jnp.sum(jnp.where(this_q, q_ref[:, (g * NSA_GROUP + r) * NSA_HD:(g * NSA_GROUP + r + 1) * NSA_HD], 0.0),
                     axis=0, keepdims=True)
             for r in range(NSA_GROUP)] + [jnp.zeros((SUBLANES - NSA_GROUP, NSA_HD), F32)], axis=0)
        dist = t - (blk_of_lane * NSA_BLOCK + lane % NSA_BLOCK)
        ok = (use_of_lane > 0) & (dist >= 0)
        s = _dot_nt(q4, kc_ref[...]) * SCALE + _bias_by_threshold(dist, tcol_ref[g])
        s = jnp.where(ok, s, NEG)
        p, den = _softmax_rows(s, ok)
        o = _dot(p, vc_ref[...]) / jnp.maximum(den, 1e-30)
        for r in range(NSA_GROUP):
            h = g * NSA_GROUP + r
            o_ref[:, h * NSA_HD:(h + 1) * NSA_HD] = o[r:r + 1, :]


def _slc_sample(proj, cache4, page_table, idx_flat, val_flat, tcols, layer, nb_batch, lq, past):
    rows = NSA_BLOCK * 4
    n_past_blocks = past // NSA_BLOCK

    def page_spec(g, j):
        def index_map(b, i, pt, idx, val):
            n = jnp.minimum(idx[((b * lq + i) * NSA_KV_HEADS + g) * NSA_TOPN + j], n_past_blocks - 1)
            return (layer, pt[b, n // 2], n % 2, 0)
        return pl.BlockSpec((None, None, rows, NSA_HD), index_map)

    grid_spec = pltpu.PrefetchScalarGridSpec(
        num_scalar_prefetch=3,
        grid=(nb_batch, lq),
        in_specs=[page_spec(g, j) for g in range(NSA_KV_HEADS) for j in range(NSA_TOPN)]
        + [pl.BlockSpec((lq, NSA_WIDTH), lambda b, i, *_: (b, C_NQ // NSA_WIDTH)),
           pl.BlockSpec((lq, 2 * NSA_KV_W), lambda b, i, *_: (b, C_SLC // (2 * NSA_KV_W))),
           pl.BlockSpec((NSA_KV_HEADS, SUBLANES, NUM_BUCKETS), lambda b, i, *_: (0, 0, 0))],
        out_specs=pl.BlockSpec((None, 1, NSA_WIDTH), lambda b, i, *_: (b * lq + i, 0, 0)),
        scratch_shapes=[pltpu.VMEM((SLOT_CHUNKS * NSA_BLOCK, NSA_HD), F32),
                        pltpu.VMEM((SLOT_CHUNKS * NSA_BLOCK, NSA_HD), F32)],
    )
    out = pl.pallas_call(
        functools.partial(_slc_sample_kernel, past=past, lq=lq),
        grid_spec=grid_spec,
        out_shape=jax.ShapeDtypeStruct((nb_batch * lq, 1, NSA_WIDTH), F32),
        compiler_params=_params("parallel", "arbitrary"),
        name="slc_sample",
    )(page_table, idx_flat, val_flat, *([cache4] * (NSA_KV_HEADS * NSA_TOPN)), proj, proj, tcols)
    return out.reshape(nb_batch * lq, NSA_WIDTH)


def _win_sample_kernel(q_ref, wpast_ref, wnew_ref, tcol_ref, ocmp_ref, oslc_ref, z_ref, g0_ref, g1_ref,
                       o_ref, kc_ref, vc_ref, *, past, lq, wlen):
    rows = NSA_GROUP * lq
    nk = kc_ref.shape[0]
    lane = lax.broadcasted_iota(jnp.int32, (1, nk), 1)
    kpos = past - wlen + lane
    t = past + lax.broadcasted_iota(jnp.int32, (rows, 1), 0) % lq
    dist = t - kpos
    ok = (dist >= 0) & (dist <= WINDOW) & (kpos >= 0) & (lane < wlen + lq)
    gate_refs = (g0_ref, g1_ref)
    for g in range(NSA_KV_HEADS):
        kc_ref[0:wlen, :] = wpast_ref[pl.ds(g, wlen, stride=4), :]
        vc_ref[0:wlen, :] = wpast_ref[pl.ds(NSA_KV_HEADS + g, wlen, stride=4), :]
        kc_ref[wlen:, :] = jnp.zeros((nk - wlen, NSA_HD), F32)
        vc_ref[wlen:, :] = jnp.zeros((nk - wlen, NSA_HD), F32)
        kc_ref[wlen:wlen + lq, :] = wnew_ref[:, g * NSA_HD:(g + 1) * NSA_HD]
        vc_ref[wlen:wlen + lq, :] = wnew_ref[:, (NSA_KV_HEADS + g) * NSA_HD:(NSA_KV_HEADS + g + 1) * NSA_HD]
        q4 = jnp.concatenate([q_ref[:, (g * NSA_GROUP + r) * NSA_HD:(g * NSA_GROUP + r + 1) * NSA_HD]
                              for r in range(NSA_GROUP)], axis=0)
        s = _dot_nt(q4, kc_ref[...]) * SCALE + _bias_by_threshold(dist, tcol_ref[g])
        s = jnp.where(ok, s, NEG)
        p, den = _softmax_rows(s, ok)
        o_win = _dot(p, vc_ref[...]) / jnp.maximum(den, 1e-30)
        gates = jax.nn.sigmoid(gate_refs[g][...])
        for r in range(NSA_GROUP):
            h = g * NSA_GROUP + r
            sl = slice(h * NSA_HD, (h + 1) * NSA_HD)
            o = (gates[:, 3 * r:3 * r + 1] * ocmp_ref[:, sl] + gates[:, 3 * r + 1:3 * r + 2] * oslc_ref[:, sl]
                 + gates[:, 3 * r + 2:3 * r + 3] * o_win[r * lq:(r + 1) * lq])
            o_ref[:, sl] = o * _silu(z_ref[:, sl])


def _win_sample(proj, win4, tcols_rows, o_cmp, o_slc, layer, nb_batch, lq, past):
    wlen = win4.shape[2] // 4
    nk = -(-(wlen + lq) // LANES) * LANES
    row_spec = pl.BlockSpec((lq, NSA_WIDTH), lambda b: (b, 0))
    return pl.pallas_call(
        functools.partial(_win_sample_kernel, past=past, lq=lq, wlen=wlen),
        grid=(nb_batch,),
        in_specs=[pl.BlockSpec((lq, NSA_WIDTH), lambda b: (b, C_NQ // NSA_WIDTH)),
                  pl.BlockSpec((None, None, wlen * 4, NSA_HD), lambda b: (layer, b, 0, 0)),
                  pl.BlockSpec((lq, 2 * NSA_KV_W), lambda b: (b, C_WIN // (2 * NSA_KV_W))),
                  pl.BlockSpec((NSA_KV_HEADS, NSA_GROUP * lq, NUM_BUCKETS), lambda b: (0, 0, 0)),
                  row_spec, row_spec,
                  pl.BlockSpec((lq, NSA_WIDTH), lambda b: (b, C_NZ // NSA_WIDTH)),
                  pl.BlockSpec((lq, LANES), lambda b: (b, C_G0 // LANES)),
                  pl.BlockSpec((lq, LANES), lambda b: (b, C_G1 // LANES))],
        out_specs=row_spec,
        out_shape=jax.ShapeDtypeStruct((nb_batch * lq, NSA_WIDTH), F32),
        scratch_shapes=[pltpu.VMEM((nk, NSA_HD), F32), pltpu.VMEM((nk, NSA_HD), F32)],
        compiler_params=_params("parallel"),
        name="win_sample",
    )(proj, win4, proj, tcols_rows, o_cmp, o_slc, proj, proj, proj)


def _out_kernel(dn_ref, nsa_ref, x_ref, p_ref, wo_ref, gpost_ref, wpp_ref, gple_ref, wpg_ref, o_ref):
    half = dn_ref.shape[1]
    mixed = (jnp.dot(dn_ref[...].astype(BF16), wo_ref[0:half, :], preferred_element_type=F32)
             + jnp.dot(nsa_ref[...].astype(BF16), wo_ref[half:, :], preferred_element_type=F32))
    y = mixed * lax.rsqrt(jnp.mean(mixed * mixed, axis=-1, keepdims=True) + EPS) * gpost_ref[...]
    x1 = x_ref[...] + y
    e = jnp.dot(p_ref[...].astype(BF16), wpp_ref[...], preferred_element_type=F32)
    e = e * lax.rsqrt(jnp.mean(e * e, axis=-1, keepdims=True) + EPS) * gple_ref[...]
    gate = jax.nn.sigmoid(jnp.dot(x1.astype(BF16), wpg_ref[...], preferred_element_type=F32))
    o_ref[...] = x1 + gate * e


def _out_proj(dn, nsa, x, p_all, wo_all, gpost_all, wpp_all, gple_all, wpg_all, layer, tm):
    m = x.shape[0]
    once = pl.Buffered(1)

    def wspec(k):
        return pl.BlockSpec((None, k, D_MODEL), lambda i: (layer, 0, 0), pipeline_mode=once)

    return pl.pallas_call(
        _out_kernel,
        grid=(m // tm,),
        in_specs=[pl.BlockSpec((tm, dn.shape[1]), lambda i: (i, 0)),
                  pl.BlockSpec((tm, nsa.shape[1]), lambda i: (i, 0)),
                  pl.BlockSpec((tm, D_MODEL), lambda i: (i, 0)),
                  pl.BlockSpec((None, tm, PLE_DIM), lambda i: (layer, i, 0)),
                  wspec(D_MODEL), wspec(1), wspec(PLE_DIM), wspec(1), wspec(D_MODEL)],
        out_specs=pl.BlockSpec((tm, D_MODEL), lambda i: (i, 0)),
        out_shape=jax.ShapeDtypeStruct((m, D_MODEL), F32),
        compiler_params=_params("parallel"),
        name="out_proj",
    )(dn, nsa, x, p_all, wo_all, gpost_all, wpp_all, gple_all, wpg_all)


def _relayout_w_in(w_in):
    depth = w_in.shape[0]

    def zeros(n):
        return jnp.zeros((depth, D_MODEL, n), w_in.dtype)

    half = 3 * NSA_GROUP
    parts = [w_in[:, :, 0:_O_BETA], w_in[:, :, _O_NQ:_O_GATE],
             w_in[:, :, _O_BETA:_O_NQ], zeros(LANES - 2 * DN_HEADS),
             w_in[:, :, _O_GATE:_O_GATE + half], zeros(LANES - half),
             w_in[:, :, _O_GATE + half:_O_GATE + 2 * half], zeros(LANES - half),
             zeros(N_PAD - C_G1 - LANES)]
    return jnp.concatenate(parts, axis=-1).astype(BF16)


def _bias_expand_kernel(t_ref, bucket_ref, o_ref):
    bucket = bucket_ref[...]
    for h in range(NSA_HEADS):
        acc = jnp.full(bucket.shape, t_ref[0, h], F32)
        for b in range(1, NUM_BUCKETS):
            acc = jnp.where(bucket >= b, t_ref[b, h], acc)
        o_ref[h] = acc


def _bias_expand(table, bucket):
    rows, cols = bucket.shape
    tr = max(t for t in range(SUBLANES, min(rows, 512) + 1, SUBLANES) if rows % t == 0)
    return pl.pallas_call(
        _bias_expand_kernel,
        grid=(rows // tr,),
        in_specs=[pl.BlockSpec(memory_space=pltpu.SMEM), pl.BlockSpec((tr, cols), lambda i: (i, 0))],
        out_specs=pl.BlockSpec((NSA_HEADS, tr, cols), lambda i: (0, i, 0)),
        out_shape=jax.ShapeDtypeStruct((NSA_HEADS, rows, cols), F32),
        compiler_params=_params("parallel"),
        name="bias_expand",
    )(table, jnp.asarray(bucket))


def _bias_tiles(table):
    i = np.arange(PQ)
    d = np.arange(N_BIAS_TILES)[:, None, None] * PQ + i[None, :, None] - i[None, None, :]
    tiles = _bias_expand(table, _bucket_np(d).reshape(N_BIAS_TILES * PQ, PQ))
    return tiles.reshape(NSA_HEADS, N_BIAS_TILES, PQ, PQ)


def _bias_cmp(table, pos0, lq, nb):
    dist = (pos0 + np.arange(lq))[:, None] - (np.arange(nb) * NSA_BLOCK + NSA_BLOCK - 1)[None, :]
    return _bias_expand(table, _bucket_np(dist))


def kernel(x_prompt, x_sample, state_dn_S, state_dn_conv, cache_win_kv, cache_cmp_kv, cache_slc_kv, page_table,
           p_prompt, p_sample, rel_bias_table, w_in, w_out, g_pre, g_post, dn_conv_w, dn_A_log, dn_dt_bias,
           dn_norm_w, cmp_pos_w, cmp_w1, cmp_w2, w_ple_proj, g_ple, w_ple_gate):
    depth = w_in.shape[0]
    bp, seq, _ = x_prompt.shape
    bs, lq, _ = x_sample.shape
    n_pages = page_table.shape[1]
    past = n_pages * PAGE_SIZE
    n_pool = cache_cmp_kv.shape[1]
    wlen = cache_win_kv.shape[2]
    mp, ms = bp * seq, bs * lq
    assert seq % PK == 0 and seq % DN_CHUNK == 0 and lq == SUBLANES and past % NSA_BLOCK == 0
    kv_row = (2, NSA_KV_HEADS, NSA_HD)

    w_in_r = _relayout_w_in(w_in)
    w_out_b = w_out.astype(BF16)
    wpp_b = w_ple_proj.astype(BF16)
    wpg_b = w_ple_gate.astype(BF16)
    g_pre3 = g_pre[:, None, :]
    g_post3 = g_post[:, None, :]
    g_ple3 = g_ple[:, None, :]
    nw3 = dn_norm_w[:, None, :]
    lane_pad = ((0, 0), (DN_HEADS, LANES - 2 * DN_HEADS))
    dn_par = jnp.stack([jnp.pad(dn_A_log, lane_pad), jnp.pad(dn_dt_bias, lane_pad)], axis=1)
    pw_prompt = jnp.broadcast_to(cmp_pos_w[:, :, None, :, None], (depth, 2, NSA_KV_HEADS, NSA_BLOCK, NSA_HD))
    pw_prompt = jnp.transpose(pw_prompt, (0, 3, 1, 2, 4)).reshape(depth, NSA_BLOCK, 2 * NSA_KV_W)
    pw_page = jnp.broadcast_to(cmp_pos_w[:, None, None, :, :, None],
                               (depth, PAGE_SIZE // NSA_BLOCK, NSA_KV_HEADS, 2, NSA_BLOCK, NSA_HD))
    pw_page = jnp.transpose(pw_page, (0, 1, 4, 3, 2, 5)).reshape(depth, PAGE_SIZE * 4, NSA_HD)
    table = rel_bias_table.astype(F32)
    bias_tiles = _bias_tiles(table)
    nb_p = seq // NSA_BLOCK
    nb_s = past // NSA_BLOCK
    nbs_s = -(-(past + lq) // NSA_BLOCK)
    nbp_s = -(-nbs_s // LANES) * LANES
    bias_cmp_p = _bias_cmp(table, 0, seq, nb_p)
    bias_cmp_s = _bias_cmp(table, past, lq, nb_s)
    tcols = jnp.pad(table.T.reshape(NSA_KV_HEADS, NSA_GROUP, NUM_BUCKETS),
                    ((0, 0), (0, SUBLANES - NSA_GROUP), (0, 0)))
    tcols_rows = jnp.repeat(table.T.reshape(NSA_KV_HEADS, NSA_GROUP, NUM_BUCKETS), lq, axis=1)

    conv0_s = jnp.pad(state_dn_conv, ((0, 0), (0, 0), (SUBLANES - (CONV_W - 1), 0), (0, 0)))
    conv0_p = jnp.zeros((1, bp, SUBLANES, DN_CONV_CH), F32)
    s0_p = jnp.zeros((1, bp, DN_HEADS, DN_DK, DN_DV), F32)
    cmp4 = cache_cmp_kv.reshape(depth, n_pool, PAGE_SIZE * 4, NSA_HD)
    slc4 = cache_slc_kv.reshape(depth, n_pool, PAGE_SIZE * 4, NSA_HD)
    win4 = cache_win_kv.reshape(depth, bs, wlen * 4, NSA_HD)
    p_prompt3 = p_prompt.reshape(depth, mp, PLE_DIM)
    p_sample3 = p_sample.reshape(depth, ms, PLE_DIM)

    xp = x_prompt.reshape(mp, D_MODEL)
    xs = x_sample.reshape(ms, D_MODEL)
    outs_p = [[] for _ in range(5)]
    outs_s = [[] for _ in range(5)]
    tq_p = min(256, seq)
    for i in range(depth):
        proj = _in_proj(xp, g_pre3, w_in_r, i, min(1024, mp))
        dn_o, conv_n, s_n = _deltanet(proj, bp, seq, DN_CHUNK, conv0_p, s0_p, 0, dn_conv_w, dn_par, nw3, i)
        ckv = _cmp_mlp(_pool_prompt(proj, bp, seq, pw_prompt, i), cmp_w1, cmp_w2, i)
        o_cmp, sel = _cmp_attn(proj, ckv, bias_cmp_p, bp, seq, 0, tq_p, nb_p, nb_p, True)
        nsa_o = _nsa_prompt(proj, sel, bias_tiles, o_cmp, bp, seq)
        xp = _out_proj(dn_o, nsa_o, xp, p_prompt3, w_out_b, g_post3, wpp_b, g_ple3, wpg_b, i, min(256, mp))
        kv_all = proj[:, C_CMP:C_WIN + 2 * NSA_KV_W].reshape(bp, seq, 3, *kv_row)
        outs_p[0].append(s_n)
        outs_p[1].append(conv_n[:, SUBLANES - (CONV_W - 1):])
        outs_p[2].append(kv_all[:, seq - min(WINDOW, seq):, 2])
        outs_p[3].append(kv_all[:, :, 0])
        outs_p[4].append(kv_all[:, :, 1])

        proj = _in_proj(xs, g_pre3, w_in_r, i, ms)
        dn_o, conv_n, s_n = _deltanet(proj, bs, lq, lq, conv0_s, state_dn_S, i, dn_conv_w, dn_par, nw3, i)
        pooled = _pool_paged(cmp4, page_table, pw_page, i, min(16, n_pages))
        ckv = _cmp_mlp(pooled.reshape(bs, nb_s, 2 * NSA_KV_W), cmp_w1, cmp_w2, i)
        o_cmp, score = _cmp_attn(proj, ckv, bias_cmp_s, bs, lq, past, lq, nbs_s, nbp_s, False)
        score_t = jnp.transpose(score.reshape(ms * NSA_KV_HEADS, nbp_s))
        _, idx_t, val_t = _topk(score_t, nbs_s)
        o_slc = _slc_sample(proj, slc4, page_table, idx_t.T.reshape(-1), val_t.T.reshape(-1), tcols, i,
                            bs, lq, past)
        nsa_o = _win_sample(proj, win4, tcols_rows, o_cmp, o_slc, i, bs, lq, past)
        xs = _out_proj(dn_o, nsa_o, xs, p_sample3, w_out_b, g_post3, wpp_b, g_ple3, wpg_b, i, ms)
        kv_all = proj[:, C_CMP:C_WIN + 2 * NSA_KV_W].reshape(bs, lq, 3, *kv_row)
        win_all = jnp.concatenate([cache_win_kv[i], kv_all[:, :, 2]], axis=1)
        outs_s[0].append(s_n)
        outs_s[1].append(conv_n[:, SUBLANES - (CONV_W - 1):])
        outs_s[2].append(win_all[:, win_all.shape[1] - min(WINDOW, past + lq):])
        outs_s[3].append(kv_all[:, :, 0])
        outs_s[4].append(kv_all[:, :, 1])

    return ((xp.reshape(bp, seq, D_MODEL), xs.reshape(bs, lq, D_MODEL))
            + tuple(jnp.stack(o) for o in outs_p) + tuple(jnp.stack(o) for o in outs_s))
```

```python
import functools
import math

import numpy as np
import jax
import jax.numpy as jnp
from jax import lax
from jax.experimental import pallas as pl
from jax.experimental.pallas import tpu as pltpu

F32 = jnp.float32
BF16 = jnp.bfloat16

D_MODEL = 2048
DN_HEADS = 8
DN_DK = 128
DN_DV = 128
DN_QK_W = DN_HEADS * DN_DK
DN_CONV_CH = 2 * DN_QK_W + DN_HEADS * DN_DV
CONV_W = 4
DN_CHUNK = 64
NSA_HEADS = 8
NSA_KV_HEADS = 2
NSA_GROUP = 4
NSA_HD = 128
NSA_WIDTH = NSA_HEADS * NSA_HD
NSA_KV_W = NSA_KV_HEADS * NSA_HD
NSA_BLOCK = 64
NSA_TOPN = 16
WINDOW = 512
SCALE = NSA_HD ** -0.5
NUM_BUCKETS = 32
REL_MAX_DIST = 1024
PLE_DIM = 256
PAGE_SIZE = 128
EPS = 1e-6
NEG = -1e30
LOG2E = math.log2(math.e)

VMEM_LIMIT_BYTES = 56 * 1024 * 1024
LANES = 128
SUBLANES = 8

C_QKV = 0
C_Z = 3072
C_NQ = 4096
C_NZ = 5120
C_CMP = 6144
C_SLC = 6656
C_WIN = 7168
C_SM = 7680
C_G0 = 7808
C_G1 = 7936
N_PAD = 8192
_O_BETA = DN_CONV_CH + DN_HEADS * DN_DV
_O_NQ = _O_BETA + 2 * DN_HEADS
_O_GATE = _O_NQ + 2 * NSA_WIDTH + 3 * 2 * NSA_KV_W
N_BIAS_TILES = 9


def _bucket_np(dist):
    dist = np.maximum(np.asarray(dist, np.int64), 0)
    exact = NUM_BUCKETS // 2
    scaled = np.log(np.maximum(dist, 1).astype(np.float64) / exact) / math.log(REL_MAX_DIST / exact)
    large = np.minimum(exact + (scaled * (NUM_BUCKETS - exact)).astype(np.int64), NUM_BUCKETS - 1)
    return np.where(dist < exact, dist, large).astype(np.int32)


_BUCKET_THR = tuple(int(np.argmax(_bucket_np(np.arange(2 * REL_MAX_DIST)) >= b)) for b in range(1, NUM_BUCKETS))


def _dot(a, b):
    return jnp.dot(a.astype(BF16), b.astype(BF16), preferred_element_type=F32)


def _dot_nt(a, b):
    return lax.dot_general(a.astype(BF16), b.astype(BF16), (((1,), (1,)), ((), ())), preferred_element_type=F32)


def _dot_tn(a, b):
    return lax.dot_general(a.astype(BF16), b.astype(BF16), (((0,), (0,)), ((), ())), preferred_element_type=F32)


def _dot_hi(a, b):
    return jnp.dot(a, b, preferred_element_type=F32, precision=lax.Precision.HIGHEST)


def _dot_hi_nt(a, b):
    return lax.dot_general(a, b, (((1,), (1,)), ((), ())), preferred_element_type=F32,
                           precision=lax.Precision.HIGHEST)


def _silu(x):
    return x * jax.nn.sigmoid(x)


def _params(*sem):
    return pltpu.CompilerParams(dimension_semantics=sem, vmem_limit_bytes=VMEM_LIMIT_BYTES)


def _in_proj_kernel(x_ref, g_ref, w_ref, o_ref, xn_ref):
    @pl.when(pl.program_id(1) == 0)
    def _():
        x = x_ref[...]
        y = x * lax.rsqrt(jnp.mean(x * x, axis=-1, keepdims=True) + EPS)
        xn_ref[...] = (y * g_ref[...]).astype(BF16)

    o_ref[...] = jnp.dot(xn_ref[...], w_ref[...], preferred_element_type=F32)


def _in_proj(x, g_all, w_all, layer, tm):
    m = x.shape[0]
    tn = 512
    return pl.pallas_call(
        _in_proj_kernel,
        grid=(m // tm, N_PAD // tn),
        in_specs=[pl.BlockSpec((tm, D_MODEL), lambda i, j: (i, 0)),
                  pl.BlockSpec((None, 1, D_MODEL), lambda i, j: (layer, 0, 0)),
                  pl.BlockSpec((None, D_MODEL, tn), lambda i, j: (layer, 0, j))],
        out_specs=pl.BlockSpec((tm, tn), lambda i, j: (i, j)),
        out_shape=jax.ShapeDtypeStruct((m, N_PAD), F32),
        scratch_shapes=[pltpu.VMEM((tm, D_MODEL), BF16)],
        compiler_params=_params("parallel", "arbitrary"),
        name="in_proj",
    )(x, g_all, w_all)


def _dn_prep_kernel(qkv_ref, sm_ref, conv0_ref, cw_ref, par_ref,
                    u_ref, w_ref, qd_ref, kd_ref, qk_ref, gl_ref, convn_ref, xs_ref, *, chunk, cs):
    tb = chunk * cs
    halo = SUBLANES

    @pl.when(pl.program_id(1) == 0)
    def _():
        xs_ref[0:halo, :] = conv0_ref[...]

    xs_ref[halo:halo + tb, :] = qkv_ref[...]
    base = halo - (CONV_W - 1)
    y = xs_ref[base:base + tb, :] * cw_ref[0:1, :]
    for j in range(1, CONV_W):
        y = y + xs_ref[base + j:base + j + tb, :] * cw_ref[j:j + 1, :]
    y = _silu(y)
    tail = xs_ref[tb:tb + halo, :]
    xs_ref[0:halo, :] = tail
    convn_ref[...] = tail

    sm = sm_ref[...]
    beta_all = jax.nn.sigmoid(sm)
    xg = sm + par_ref[1:2, :]
    softplus = jnp.maximum(xg, 0.0) + jnp.log1p(jnp.exp(-jnp.abs(xg)))
    g_all = -jnp.exp(par_ref[0:1, :]) * softplus
    row = lax.broadcasted_iota(jnp.int32, (tb, tb), 0)
    col = lax.broadcasted_iota(jnp.int32, (tb, tb), 1)
    in_chunk_lower = (row >= col) & (row // chunk == col // chunk)
    gc_all = _dot_hi(jnp.where(in_chunk_lower, 1.0, 0.0), g_all)
    pick = (lax.broadcasted_iota(jnp.int32, (DN_HEADS, LANES), 1)
            == lax.broadcasted_iota(jnp.int32, (DN_HEADS, LANES), 0) + DN_HEADS).astype(F32)
    gc_rows = _dot_hi_nt(pick, gc_all)
    for c in range(cs):
        gl_ref[c] = gc_all[(c + 1) * chunk - 1:(c + 1) * chunk, :]
    r64 = lax.broadcasted_iota(jnp.int32, (chunk, chunk), 0)
    c64 = lax.broadcasted_iota(jnp.int32, (chunk, chunk), 1)
    lower = r64 >= c64
    strict = r64 > c64

    lhs, kns, rhs, decays = [], [], [], []
    for h in range(DN_HEADS):
        hc = slice(h * DN_DV, (h + 1) * DN_DV)
        qh = y[:, h * DN_DK:(h + 1) * DN_DK]
        kh = y[:, DN_QK_W + h * DN_DK:DN_QK_W + (h + 1) * DN_DK]
        vh = y[:, 2 * DN_QK_W + h * DN_DV:2 * DN_QK_W + (h + 1) * DN_DV]
        qn = qh * lax.rsqrt(jnp.sum(qh * qh, axis=-1, keepdims=True) + EPS) * (DN_DK ** -0.5)
        kn = kh * lax.rsqrt(jnp.sum(kh * kh, axis=-1, keepdims=True) + EPS)
        beta = beta_all[:, h:h + 1]
        gc = gc_all[:, DN_HEADS + h:DN_HEADS + h + 1]
        egc = jnp.exp(gc)
        kb = kn * beta
        rhs_h = jnp.concatenate([vh * beta, kb * egc], axis=-1)
        qd_ref[:, hc] = (qn * egc).astype(qd_ref.dtype)
        for c in range(cs):
            rs = slice(c * chunk, (c + 1) * chunk)
            gcc = gc[rs]
            decays.append(jnp.exp(jnp.where(lower, gcc - gc_rows[h:h + 1, rs], NEG)))
            lhs.append(jnp.concatenate([kb[rs], qn[rs]], axis=0))
            kns.append(kn[rs])
            rhs.append(rhs_h[rs])
            kd_ref[rs, hc] = (kn[rs] * jnp.exp(gcc[chunk - 1:chunk, :] - gcc)).astype(kd_ref.dtype)
    decay = jnp.stack(decays)
    both = jnp.einsum('nid,njd->nij', jnp.stack(lhs).astype(BF16), jnp.stack(kns).astype(BF16),
                      preferred_element_type=F32)
    a = jnp.where(strict[None], both[:, :chunk] * decay, 0.0)
    qk = both[:, chunk:] * decay

    def bmm(p, q):
        return jnp.einsum('nij,njk->nik', p, q, preferred_element_type=F32)

    def split(v):
        hi = v.astype(BF16)
        return hi, (v - hi.astype(F32)).astype(BF16)

    def bmm3(p, q):
        (ph, pl_), (qh, ql) = p, q
        return bmm(ph, qh) + (bmm(ph, ql) + bmm(pl_, qh))

    x = jnp.stack(rhs)
    a_s = split(a)
    x = x - bmm3(a_s, split(x))
    pw = bmm3(a_s, a_s)
    x = x + bmm3(split(pw), split(x))
    for _ in range(int(math.log2(chunk)) - 2):
        pw = bmm(pw.astype(BF16), pw.astype(BF16))
        x = x + bmm(pw.astype(BF16), x.astype(BF16))
    for h in range(DN_HEADS):
        hc = slice(h * DN_DV, (h + 1) * DN_DV)
        for c in range(cs):
            rs = slice(c * chunk, (c + 1) * chunk)
            n = h * cs + c
            u_ref[rs, hc] = x[n, :, :DN_DV]
            w_ref[rs, hc] = x[n, :, DN_DV:].astype(w_ref.dtype)
            qk_ref[rs, h * chunk:(h + 1) * chunk] = qk[n]


def _dn_scan_kernel(u_ref, w_ref, qd_ref, kd_ref, qk_ref, gl_ref, z_ref, s0_ref, nw_ref,
                    o_ref, sout_ref, s_ref, *, chunk, cs):
    @pl.when(pl.program_id(1) == 0)
    def _():
        s_ref[...] = s0_ref[...]

    def heads(ref, rs, width):
        return jnp.stack([ref[rs, h * width:(h + 1) * width] for h in range(DN_HEADS)])

    def bdot(spec, p, q):
        return jnp.einsum(spec, p.astype(BF16), q.astype(BF16), preferred_element_type=F32)

    for c in range(cs):
        rs = slice(c * chunk, (c + 1) * chunk)
        decay_last = jnp.exp(gl_ref[c])
        decay_last = jnp.stack([decay_last[:, DN_HEADS + h:DN_HEADS + h + 1] for h in range(DN_HEADS)])
        s = s_ref[...]
        sb = s.astype(BF16)
        v_new = heads(u_ref, rs, DN_DV) - bdot('hcd,hde->hce', heads(w_ref, rs, DN_DK), sb)
        o = (bdot('hcd,hde->hce', heads(qd_ref, rs, DN_DK), sb)
             + bdot('hcs,hse->hce', heads(qk_ref, rs, chunk), v_new))
        s_ref[...] = s * decay_last + bdot('hcd,hce->hde', heads(kd_ref, rs, DN_DK), v_new)
        on = o * lax.rsqrt(jnp.mean(o * o, axis=-1, keepdims=True) + EPS) * nw_ref[...]
        for h in range(DN_HEADS):
            hc = slice(h * DN_DV, (h + 1) * DN_DV)
            o_ref[rs, hc] = on[h] * _silu(z_ref[rs, hc])

    @pl.when(pl.program_id(1) == pl.num_programs(1) - 1)
    def _():
        sout_ref[...] = s_ref[...]


def _deltanet(proj, nb, seq, chunk, conv0_all, s0_all, state_layer, cw_all, par_all, nw_all, layer):
    m = nb * seq
    nchunks = seq // chunk
    width = DN_HEADS * DN_DV
    hs = (DN_HEADS, DN_DK, DN_DV)
    cs_a = 2 if nchunks % 2 == 0 else 1
    na = nchunks // cs_a
    ta = cs_a * chunk
    op_dtype = BF16 if chunk % (2 * SUBLANES) == 0 else F32
    u, w, qd, kd, qk, gl, conv_n = pl.pallas_call(
        functools.partial(_dn_prep_kernel, chunk=chunk, cs=cs_a),
        grid=(nb, na),
        in_specs=[pl.BlockSpec((ta, DN_CONV_CH), lambda b, c: (b * na + c, C_QKV // DN_CONV_CH)),
                  pl.BlockSpec((ta, LANES), lambda b, c: (b * na + c, C_SM // LANES)),
                  pl.BlockSpec((None, None, SUBLANES, DN_CONV_CH), lambda b, c: (state_layer, b, 0, 0)),
                  pl.BlockSpec((None, CONV_W, DN_CONV_CH), lambda b, c: (layer, 0, 0)),
                  pl.BlockSpec((None, 2, LANES), lambda b, c: (layer, 0, 0))],
        out_specs=[pl.BlockSpec((ta, width), lambda b, c: (b * na + c, 0)),
                   pl.BlockSpec((ta, width), lambda b, c: (b * na + c, 0)),
                   pl.BlockSpec((ta, width), lambda b, c: (b * na + c, 0)),
                   pl.BlockSpec((ta, width), lambda b, c: (b * na + c, 0)),
                   pl.BlockSpec((ta, DN_HEADS * chunk), lambda b, c: (b * na + c, 0)),
                   pl.BlockSpec((cs_a, 1, LANES), lambda b, c: (b * na + c, 0, 0)),
                   pl.BlockSpec((None, SUBLANES, DN_CONV_CH), lambda b, c: (b, 0, 0))],
        out_shape=[jax.ShapeDtypeStruct((m, width), F32),
                   jax.ShapeDtypeStruct((m, width), op_dtype),
                   jax.ShapeDtypeStruct((m, width), op_dtype),
                   jax.ShapeDtypeStruct((m, width), op_dtype),
                   jax.ShapeDtypeStruct((m, DN_HEADS * chunk), F32),
                   jax.ShapeDtypeStruct((nb * nchunks, 1, LANES), F32),
                   jax.ShapeDtypeStruct((nb, SUBLANES, DN_CONV_CH), F32)],
        scratch_shapes=[pltpu.VMEM((ta + SUBLANES, DN_CONV_CH), F32)],
        compiler_params=_params("parallel", "arbitrary"),
        name="dn_prep",
    )(proj, proj, conv0_all, cw_all, par_all)

    cs_b = 4 if nchunks % 4 == 0 else 1
    nbk = nchunks // cs_b
    tbk = cs_b * chunk
    row_spec = pl.BlockSpec((tbk, width), lambda b, c: (b * nbk + c, 0))
    o, s_n = pl.pallas_call(
        functools.partial(_dn_scan_kernel, chunk=chunk, cs=cs_b),
        grid=(nb, nbk),
        in_specs=[row_spec, row_spec, row_spec, row_spec,
                  pl.BlockSpec((tbk, DN_HEADS * chunk), lambda b, c: (b * nbk + c, 0)),
                  pl.BlockSpec((cs_b, 1, LANES), lambda b, c: (b * nbk + c, 0, 0)),
                  pl.BlockSpec((tbk, width), lambda b, c: (b * nbk + c, C_Z // width)),
                  pl.BlockSpec((None, None) + hs, lambda b, c: (state_layer, b, 0, 0, 0)),
                  pl.BlockSpec((None, 1, DN_DV), lambda b, c: (layer, 0, 0))],
        out_specs=[row_spec, pl.BlockSpec((None,) + hs, lambda b, c: (b, 0, 0, 0))],
        out_shape=[jax.ShapeDtypeStruct((m, width), F32), jax.ShapeDtypeStruct((nb,) + hs, F32)],
        scratch_shapes=[pltpu.VMEM(hs, F32)],
        compiler_params=_params("parallel", "arbitrary"),
        name="dn_scan",
    )(u, w, qd, kd, qk, gl, proj, s0_all, nw_all)
    return o, conv_n, s_n


def _pool_kernel(x_ref, pw_ref, o_ref):
    rows = x_ref.shape[0]
    x = x_ref[...].reshape(rows // NSA_BLOCK, NSA_BLOCK, 2 * NSA_KV_W)
    o_ref[...] = jnp.sum(x * pw_ref[...][None], axis=1)


def _pool_prompt(proj, nb, seq, pw_all, layer):
    rows = min(seq, 1024)
    nr = seq // rows
    width = 2 * NSA_KV_W
    return pl.pallas_call(
        _pool_kernel,
        grid=(nb, nr),
        in_specs=[pl.BlockSpec((rows, width), lambda b, r: (b * nr + r, C_CMP // width)),
                  pl.BlockSpec((None, NSA_BLOCK, width), lambda b, r: (layer, 0, 0))],
        out_specs=pl.BlockSpec((None, rows // NSA_BLOCK, width), lambda b, r: (b, r, 0)),
        out_shape=jax.ShapeDtypeStruct((nb, seq // NSA_BLOCK, width), F32),
        compiler_params=_params("parallel", "parallel"),
        name="pool_prompt",
    )(proj, pw_all)


def _paged_pool_kernel(pt_ref, *refs, pages):
    del pt_ref
    page_refs = refs[:pages]
    pw_ref = refs[pages]
    o_ref = refs[pages + 1]
    rows = PAGE_SIZE * 4
    per_block = rows // 2 // SUBLANES
    for k in range(pages):
        prod = page_refs[k][...] * pw_ref[...]
        y = jnp.sum(prod.reshape(2, per_block, SUBLANES, NSA_HD), axis=1)
        y = y[:, 0:4, :] + y[:, 4:8, :]
        for s in range(4):
            o_ref[2 * k:2 * k + 2, s * NSA_HD:(s + 1) * NSA_HD] = y[:, s, :]


def _pool_paged(cache4, page_table, pw_all, layer, pages):
    nb, n_pages = page_table.shape
    steps = n_pages // pages
    rows = PAGE_SIZE * 4

    def page_spec(k):
        return pl.BlockSpec((None, None, rows, NSA_HD),
                            lambda b, s, pt: (layer, pt[b, s * pages + k], 0, 0))

    grid_spec = pltpu.PrefetchScalarGridSpec(
        num_scalar_prefetch=1,
        grid=(nb, steps),
        in_specs=[page_spec(k) for k in range(pages)]
        + [pl.BlockSpec((None, rows, NSA_HD), lambda b, s, pt: (layer, 0, 0))],
        out_specs=pl.BlockSpec((None, 2 * pages, 4 * NSA_HD), lambda b, s, pt: (b, s, 0)),
    )
    return pl.pallas_call(
        functools.partial(_paged_pool_kernel, pages=pages),
        grid_spec=grid_spec,
        out_shape=jax.ShapeDtypeStruct((nb, 2 * n_pages, 4 * NSA_HD), F32),
        compiler_params=_params("parallel", "arbitrary"),
        name="pool_paged",
    )(page_table, *([cache4] * pages), pw_all)


def _cmp_mlp_kernel(p_ref, w1_ref, w2_ref, o_ref):
    for s in range(4):
        kv = s // NSA_KV_HEADS
        h = _silu(_dot_hi(p_ref[:, s * NSA_HD:(s + 1) * NSA_HD], w1_ref[kv]))
        o_ref[:, s * NSA_HD:(s + 1) * NSA_HD] = _dot_hi(h, w2_ref[kv])


def _cmp_mlp(pooled, w1_all, w2_all, layer):
    nb, n, width = pooled.shape
    wspec = pl.BlockSpec((None, 2, NSA_HD, NSA_HD), lambda b: (layer, 0, 0, 0))
    return pl.pallas_call(
        _cmp_mlp_kernel,
        grid=(nb,),
        in_specs=[pl.BlockSpec((None, n, width), lambda b: (b, 0, 0)), wspec, wspec],
        out_specs=pl.BlockSpec((None, n, width), lambda b: (b, 0, 0)),
        out_shape=jax.ShapeDtypeStruct((nb, n, width), F32),
        compiler_params=_params("parallel"),
        name="cmp_mlp",
    )(pooled, w1_all, w2_all)


def _select_blocks(score, st_ref, nbp):
    st_ref[...] = score.T
    halves = [st_ref[g * nbp:(g + 1) * nbp, :] for g in range(NSA_KV_HEADS)]
    bi = lax.broadcasted_iota(jnp.int32, halves[0].shape, 0)

    def body(j, ranks):
        out = []
        for g in range(NSA_KV_HEADS):
            other = st_ref[pl.ds(g * nbp + j, 1), :]
            beats = (other > halves[g]) | ((other == halves[g]) & (j < bi))
            out.append(ranks[g] + jnp.where(beats, 1, 0))
        return tuple(out)

    ranks = lax.fori_loop(0, nbp, body, tuple(jnp.zeros(bi.shape, jnp.int32) for _ in halves))
    keep = [jnp.where((ranks[g] < NSA_TOPN) & (halves[g] >= 0.0), 1.0, 0.0) for g in range(NSA_KV_HEADS)]
    return jnp.concatenate(keep, axis=0).T


def _cmp_attn_kernel(q_ref, ckv_ref, bias_ref, o_ref, sc_ref, *scratch, pos0, tq, nb, nbs, nbp, select):
    qi = pl.program_id(1)
    t = pos0 + qi * tq + lax.broadcasted_iota(jnp.int32, (tq, 1), 0)
    blk = lax.broadcasted_iota(jnp.int32, (1, nb), 1)
    valid = t >= blk * NSA_BLOCK + (NSA_BLOCK - 1)
    bi = lax.broadcasted_iota(jnp.int32, (1, nbp), 1)
    cur = t // NSA_BLOCK
    forced = (bi == 0) | (bi == cur) | (bi == cur - 1)
    causal = bi <= cur
    scores = []
    for g in range(NSA_KV_HEADS):
        ck = ckv_ref[:, g * NSA_HD:(g + 1) * NSA_HD]
        cv = ckv_ref[:, (NSA_KV_HEADS + g) * NSA_HD:(NSA_KV_HEADS + g + 1) * NSA_HD]
        imp = jnp.zeros((tq, nb), F32)
        for r in range(NSA_GROUP):
            h = g * NSA_GROUP + r
            logits = _dot_hi_nt(q_ref[:, h * NSA_HD:(h + 1) * NSA_HD], ck) * SCALE + bias_ref[h]
            logits = jnp.where(valid, logits, NEG)
            mx = jnp.max(logits, axis=-1, keepdims=True)
            p = jnp.where(valid, jnp.exp(logits - mx), 0.0)
            p = p / jnp.maximum(jnp.sum(p, axis=-1, keepdims=True), 1e-30)
            o_ref[:, h * NSA_HD:(h + 1) * NSA_HD] = _dot_hi(p, cv)
            imp = imp + p
        if nbp > nb:
            imp = jnp.concatenate([imp, jnp.zeros((tq, nbp - nb), F32)], axis=-1)
        score = jnp.where(causal, jnp.where(forced, NSA_GROUP + 1.0, imp), -1.0)
        scores.append(jnp.where(bi < nbs, score, -2.0))
    score = jnp.concatenate(scores, axis=-1)
    sc_ref[...] = _select_blocks(score, scratch[0], nbp) if select else score


def _cmp_attn(proj, ckv, bias_cmp, nb_batch, seq, pos0, tq, nbs, nbp, select):
    nq = seq // tq
    nb = ckv.shape[1]
    m = nb_batch * seq
    bias_tq = bias_cmp.shape[1] // nq
    return pl.pallas_call(
        functools.partial(_cmp_attn_kernel, pos0=pos0, tq=tq, nb=nb, nbs=nbs, nbp=nbp, select=select),
        scratch_shapes=[pltpu.VMEM((NSA_KV_HEADS * nbp, tq), F32)] if select else [],
        grid=(nb_batch, nq),
        in_specs=[pl.BlockSpec((tq, NSA_WIDTH), lambda b, i: (b * nq + i, C_NQ // NSA_WIDTH)),
                  pl.BlockSpec((None, nb, 2 * NSA_KV_W), lambda b, i: (b, 0, 0)),
                  pl.BlockSpec((NSA_HEADS, bias_tq, nb), lambda b, i: (0, i, 0))],
        out_specs=[pl.BlockSpec((tq, NSA_WIDTH), lambda b, i: (b * nq + i, 0)),
                   pl.BlockSpec((tq, NSA_KV_HEADS * nbp), lambda b, i: (b * nq + i, 0))],
        out_shape=[jax.ShapeDtypeStruct((m, NSA_WIDTH), F32),
                   jax.ShapeDtypeStruct((m, NSA_KV_HEADS * nbp), F32)],
        compiler_params=_params("parallel", "parallel"),
        name="cmp_attn",
    )(proj, ckv, bias_cmp)


def _topk_kernel(s_ref, sel_ref, idx_ref, val_ref, *, nbs):
    s = s_ref[...]
    bi = lax.broadcasted_iota(jnp.int32, s.shape, 0)

    def body(j, rank):
        other = s_ref[pl.ds(j, 1), :]
        beats = (other > s) | ((other == s) & (j < bi))
        return rank + beats.astype(jnp.int32)

    rank = lax.fori_loop(0, nbs, body, jnp.zeros(s.shape, jnp.int32))
    ok = s >= 0.0
    sel_ref[...] = jnp.where((rank < NSA_TOPN) & ok, 1.0, 0.0)
    for r in range(NSA_TOPN):
        hit = rank == r
        idx_ref[r:r + 1, :] = jnp.sum(jnp.where(hit, bi, 0), axis=0, keepdims=True)
        val_ref[r:r + 1, :] = jnp.sum(jnp.where(hit & ok, 1, 0), axis=0, keepdims=True)


def _topk(score_t, nbs):
    rows, cols = score_t.shape
    tc = min(cols, 512)
    return pl.pallas_call(
        functools.partial(_topk_kernel, nbs=nbs),
        grid=(cols // tc,),
        in_specs=[pl.BlockSpec((rows, tc), lambda i: (0, i))],
        out_specs=[pl.BlockSpec((rows, tc), lambda i: (0, i)),
                   pl.BlockSpec((NSA_TOPN, tc), lambda i: (0, i)),
                   pl.BlockSpec((NSA_TOPN, tc), lambda i: (0, i))],
        out_shape=[jax.ShapeDtypeStruct((rows, cols), F32),
                   jax.ShapeDtypeStruct((NSA_TOPN, cols), jnp.int32),
                   jax.ShapeDtypeStruct((NSA_TOPN, cols), jnp.int32)],
        compiler_params=_params("parallel"),
        name="topk",
    )(score_t)


PQ = 128
PK = 512


def _softmax_rows(s, ok):
    mx = jnp.max(s, axis=-1, keepdims=True)
    p = jnp.where(ok, jnp.exp(s - mx), 0.0)
    return p, jnp.sum(p, axis=-1, keepdims=True)


def _nsa_prompt_kernel(q_ref, ks_ref, vs_ref, kw_ref, vw_ref, sel_ref, bt_ref, ocmp_ref, z_ref, gate_ref,
                       o_ref, s_ref, p_ref, sw_ref, pw_ref, m_ref, l_ref, acc_ref, *, nq, wt):
    qi = pl.program_id(2)
    rows = NSA_GROUP * PQ
    q4 = jnp.concatenate([q_ref[:, r * NSA_HD:(r + 1) * NSA_HD] for r in range(NSA_GROUP)], axis=0)
    q4 = (q4 * (SCALE * LOG2E)).astype(BF16)
    t = qi * PQ + lax.broadcasted_iota(jnp.int32, (PQ, 1), 0)
    nsel = sel_ref.shape[1]
    blk0 = pl.program_id(1) * (nsel // NSA_KV_HEADS)
    sub = PK // PQ
    unselected = jnp.where(sel_ref[...] > 0.5, 0.0, NEG).astype(BF16)
    qx = jnp.concatenate([q4, jnp.concatenate([unselected] * NSA_GROUP, axis=0)], axis=1)

    m_ref[...] = jnp.full((rows, PQ), NEG, F32)
    l_ref[...] = jnp.zeros((rows, PQ), F32)
    acc_ref[...] = jnp.zeros((rows, NSA_HD), F32)

    def key_tile(kt, diagonal):
        k0 = pl.multiple_of(kt * PK, PK)
        member = (lax.broadcasted_iota(jnp.int32, (PK, nsel), 1)
                  == blk0 + kt * (PK // NSA_BLOCK) + lax.broadcasted_iota(jnp.int32, (PK, nsel), 0) // NSA_BLOCK)
        kx = jnp.concatenate([ks_ref[pl.ds(k0, PK), :].astype(BF16), jnp.where(member, 1.0, 0.0).astype(BF16)],
                             axis=1)
        s_ref[...] = lax.dot_general(qx, kx, (((1,), (1,)), ((), ())), preferred_element_type=F32)
        if diagonal:
            future = jnp.where(k0 + lax.broadcasted_iota(jnp.int32, (1, PK), 1) <= t, 0.0, NEG)
        for r in range(NSA_GROUP):
            rs = slice(r * PQ, (r + 1) * PQ)
            zmax = None
            for c in range(sub):
                cs = slice(c * PQ, (c + 1) * PQ)
                d = jnp.clip(qi - (kt * sub + c), 0, N_BIAS_TILES - 1)
                z = s_ref[rs, cs] + bt_ref[r, d]
                if diagonal:
                    z = z + future[:, cs]
                s_ref[rs, cs] = z
                zmax = z if zmax is None else jnp.maximum(zmax, z)
            m_old = m_ref[rs]
            mx = jnp.maximum(m_old, jnp.broadcast_to(jnp.max(zmax, axis=-1, keepdims=True), (PQ, PQ)))
            alpha = jnp.exp2(m_old - mx)
            psum = None
            for c in range(sub):
                cs = slice(c * PQ, (c + 1) * PQ)
                p = jnp.exp2(s_ref[rs, cs] - mx)
                p_ref[rs, cs] = p.astype(BF16)
                psum = p if psum is None else psum + p
            l_ref[rs] = alpha * l_ref[rs] + psum
            acc_ref[rs] = alpha * acc_ref[rs]
            m_ref[rs] = mx
        acc_ref[...] += jnp.dot(p_ref[...], vs_ref[pl.ds(k0, PK), :].astype(BF16), preferred_element_type=F32)

    last = (qi * PQ) // PK

    def body(kt, carry):
        key_tile(kt, False)
        return carry

    lax.fori_loop(0, last, body, 0)
    key_tile(last, True)
    o_slc = acc_ref[...] / jnp.maximum(jnp.sum(l_ref[...], axis=-1, keepdims=True), 1e-30)

    wk = wt * PQ
    kst = jnp.clip(qi - (wt - 1), 0, nq - wt)
    k0 = pl.multiple_of(kst * PQ, PQ)
    sw_ref[...] = lax.dot_general(q4, kw_ref[pl.ds(k0, wk), :].astype(BF16), (((1,), (1,)), ((), ())),
                                  preferred_element_type=F32)
    dist = t - (k0 + lax.broadcasted_iota(jnp.int32, (1, wk), 1))
    outside = jnp.where((dist >= 0) & (dist <= WINDOW), 0.0, NEG)
    dens = []
    for r in range(NSA_GROUP):
        rs = slice(r * PQ, (r + 1) * PQ)
        zmax = None
        for c in range(wt):
            cs = slice(c * PQ, (c + 1) * PQ)
            d = jnp.clip(qi - (kst + c), 0, N_BIAS_TILES - 1)
            z = sw_ref[rs, cs] + bt_ref[r, d] + outside[:, cs]
            sw_ref[rs, cs] = z
            zmax = z if zmax is None else jnp.maximum(zmax, z)
        mx = jnp.broadcast_to(jnp.max(zmax, axis=-1, keepdims=True), (PQ, PQ))
        psum = None
        for c in range(wt):
            cs = slice(c * PQ, (c + 1) * PQ)
            p = jnp.exp2(sw_ref[rs, cs] - mx)
            pw_ref[rs, cs] = p.astype(BF16)
            psum = p if psum is None else psum + p
        dens.append(jnp.sum(psum, axis=-1, keepdims=True))
    pv = jnp.dot(pw_ref[...], vw_ref[pl.ds(k0, wk), :].astype(BF16), preferred_element_type=F32)
    o_win = [pv[r * PQ:(r + 1) * PQ] / jnp.maximum(dens[r], 1e-30) for r in range(NSA_GROUP)]

    gates = jax.nn.sigmoid(gate_ref[...])
    for r in range(NSA_GROUP):
        sl = slice(r * NSA_HD, (r + 1) * NSA_HD)
        rs = slice(r * PQ, (r + 1) * PQ)
        o = (gates[:, 3 * r:3 * r + 1] * ocmp_ref[:, sl] + gates[:, 3 * r + 1:3 * r + 2] * o_slc[rs]
             + gates[:, 3 * r + 2:3 * r + 3] * o_win[r])
        o_ref[:, sl] = o * _silu(z_ref[:, sl])


def _nsa_prompt(proj, sel, bias_tiles, o_cmp, nb_batch, seq):
    nq = seq // PQ
    wt = min(WINDOW // PQ + 1, nq)
    gw = NSA_GROUP * NSA_HD
    rows = NSA_GROUP * PQ
    nsel = sel.shape[-1]

    def kv_spec(col):
        return pl.BlockSpec((seq, NSA_HD), lambda b, g, i: (b, col // NSA_HD + g))

    return pl.pallas_call(
        functools.partial(_nsa_prompt_kernel, nq=nq, wt=wt),
        grid=(nb_batch, NSA_KV_HEADS, nq),
        in_specs=[pl.BlockSpec((PQ, gw), lambda b, g, i: (b * nq + i, C_NQ // gw + g)),
                  kv_spec(C_SLC), kv_spec(C_SLC + NSA_KV_W), kv_spec(C_WIN), kv_spec(C_WIN + NSA_KV_W),
                  pl.BlockSpec((PQ, nsel), lambda b, g, i: (b * nq + i, 0)),
                  pl.BlockSpec((NSA_GROUP, N_BIAS_TILES, PQ, PQ), lambda b, g, i: (g, 0, 0, 0)),
                  pl.BlockSpec((PQ, gw), lambda b, g, i: (b * nq + i, g)),
                  pl.BlockSpec((PQ, gw), lambda b, g, i: (b * nq + i, C_NZ // gw + g)),
                  pl.BlockSpec((PQ, LANES), lambda b, g, i: (b * nq + i, C_G0 // LANES + g))],
        out_specs=pl.BlockSpec((PQ, gw), lambda b, g, i: (b * nq + i, g)),
        out_shape=jax.ShapeDtypeStruct((nb_batch * seq, NSA_WIDTH), F32),
        scratch_shapes=[pltpu.VMEM((rows, PK), F32), pltpu.VMEM((rows, PK), BF16),
                        pltpu.VMEM((rows, wt * PQ), F32), pltpu.VMEM((rows, wt * PQ), BF16),
                        pltpu.VMEM((rows, PQ), F32), pltpu.VMEM((rows, PQ), F32), pltpu.VMEM((rows, NSA_HD), F32)],
        compiler_params=_params("parallel", "parallel", "arbitrary"),
        name="nsa_prompt",
    )(proj, proj, proj, proj, proj, sel, bias_tiles, o_cmp, proj, proj)


def _bias_by_threshold(dist, tcols):
    bias = jnp.broadcast_to(tcols[:, 0:1], (tcols.shape[0], dist.shape[-1]))
    for b in range(1, NUM_BUCKETS):
        bias = jnp.where(dist >= _BUCKET_THR[b - 1], tcols[:, b:b + 1], bias)
    return bias


SLOT_CHUNKS = NSA_TOPN + 2


def _slc_sample_kernel(pt_ref, idx_ref, val_ref, *refs, past, lq):
    del pt_ref
    n_in = NSA_KV_HEADS * NSA_TOPN
    page_refs = refs[:n_in]
    q_ref, knew_ref, tcol_ref, o_ref, kc_ref, vc_ref = refs[n_in:]
    b = pl.program_id(0)
    qi = pl.program_id(1)
    t = past + qi
    n_past_blocks = past // NSA_BLOCK
    nk = SLOT_CHUNKS * NSA_BLOCK
    lane = lax.broadcasted_iota(jnp.int32, (1, nk), 1)
    slot = lane // NSA_BLOCK
    this_q = lax.broadcasted_iota(jnp.int32, (lq, 1), 0) == qi
    for g in range(NSA_KV_HEADS):
        base = ((b * lq + qi) * NSA_KV_HEADS + g) * NSA_TOPN
        blk_of_lane = jnp.where(slot == NSA_TOPN, n_past_blocks, n_past_blocks + 1)
        use_of_lane = jnp.where(slot == NSA_TOPN, 1, 0)
        for j in range(NSA_TOPN):
            n_j = idx_ref[base + j]
            use = jnp.where((val_ref[base + j] > 0) & (n_j < n_past_blocks), 1, 0)
            blk_of_lane = jnp.where(slot == j, n_j, blk_of_lane)
            use_of_lane = jnp.where(slot == j, use, use_of_lane)
            page = page_refs[g * NSA_TOPN + j]
            kc_ref[j * NSA_BLOCK:(j + 1) * NSA_BLOCK, :] = page[pl.ds(g, NSA_BLOCK, stride=4), :]
            vc_ref[j * NSA_BLOCK:(j + 1) * NSA_BLOCK, :] = page[pl.ds(NSA_KV_HEADS + g, NSA_BLOCK, stride=4), :]
        new0 = NSA_TOPN * NSA_BLOCK
        kc_ref[new0:, :] = jnp.zeros((nk - new0, NSA_HD), F32)
        vc_ref[new0:, :] = jnp.zeros((nk - new0, NSA_HD), F32)
        kc_ref[new0:new0 + lq, :] = knew_ref[:, g * NSA_HD:(g + 1) * NSA_HD]
        vc_ref[new0:new0 + lq, :] = knew_ref[:, (NSA_KV_HEADS + g) * NSA_HD:(NSA_KV_HEADS + g + 1) * NSA_HD]
        q4 = jnp.concatenate(
            [jnp.sum(jnp.where(this_q, q_ref[:, (g * NSA_GROUP + r) * NSA_HD:(g * NSA_GROUP + r + 1) * NSA_HD], 0.0),
                     axis=0, keepdims=True)
             for r in range(NSA_GROUP)] + [jnp.zeros((SUBLANES - NSA_GROUP, NSA_HD), F32)], axis=0)
        dist = t - (blk_of_lane * NSA_BLOCK + lane % NSA_BLOCK)
        ok = (use_of_lane > 0) & (dist >= 0)
        s = _dot_nt(q4, kc_ref[...]) * SCALE + _bias_by_threshold(dist, tcol_ref[g])
        s = jnp.where(ok, s, NEG)
        p, den = _softmax_rows(s, ok)
        o = _dot(p, vc_ref[...]) / jnp.maximum(den, 1e-30)
        for r in range(NSA_GROUP):
            h = g * NSA_GROUP + r
            o_ref[:, h * NSA_HD:(h + 1) * NSA_HD] = o[r:r + 1, :]


def _slc_sample(proj, cache4, page_table, idx_flat, val_flat, tcols, layer, nb_batch, lq, past):
    rows = NSA_BLOCK * 4
    n_past_blocks = past // NSA_BLOCK

    def page_spec(g, j):
        def index_map(b, i, pt, idx, val):
            n = jnp.minimum(idx[((b * lq + i) * NSA_KV_HEADS + g) * NSA_TOPN + j], n_past_blocks - 1)
            return (layer, pt[b, n // 2], n % 2, 0)
        return pl.BlockSpec((None, None, rows, NSA_HD), index_map)

    grid_spec = pltpu.PrefetchScalarGridSpec(
        num_scalar_prefetch=3,
        grid=(nb_batch, lq),
        in_specs=[page_spec(g, j) for g in range(NSA_KV_HEADS) for j in range(NSA_TOPN)]
        + [pl.BlockSpec((lq, NSA_WIDTH), lambda b, i, *_: (b, C_NQ // NSA_WIDTH)),
           pl.BlockSpec((lq, 2 * NSA_KV_W), lambda b, i, *_: (b, C_SLC // (2 * NSA_KV_W))),
           pl.BlockSpec((NSA_KV_HEADS, SUBLANES, NUM_BUCKETS), lambda b, i, *_: (0, 0, 0))],
        out_specs=pl.BlockSpec((None, 1, NSA_WIDTH), lambda b, i, *_: (b * lq + i, 0, 0)),
        scratch_shapes=[pltpu.VMEM((SLOT_CHUNKS * NSA_BLOCK, NSA_HD), F32),
                        pltpu.VMEM((SLOT_CHUNKS * NSA_BLOCK, NSA_HD), F32)],
    )
    out = pl.pallas_call(
        functools.partial(_slc_sample_kernel, past=past, lq=lq),
        grid_spec=grid_spec,
        out_shape=jax.ShapeDtypeStruct((nb_batch * lq, 1, NSA_WIDTH), F32),
        compiler_params=_params("parallel", "arbitrary"),
        name="slc_sample",
    )(page_table, idx_flat, val_flat, *([cache4] * (NSA_KV_HEADS * NSA_TOPN)), proj, proj, tcols)
    return out.reshape(nb_batch * lq, NSA_WIDTH)


def _win_sample_kernel(q_ref, wpast_ref, wnew_ref, tcol_ref, ocmp_ref, oslc_ref, z_ref, g0_ref, g1_ref,
                       o_ref, kc_ref, vc_ref, *, past, lq, wlen):
    rows = NSA_GROUP * lq
    nk = kc_ref.shape[0]
    lane = lax.broadcasted_iota(jnp.int32, (1, nk), 1)
    kpos = past - wlen + lane
    t = past + lax.broadcasted_iota(jnp.int32, (rows, 1), 0) % lq
    dist = t - kpos
    ok = (dist >= 0) & (dist <= WINDOW) & (kpos >= 0) & (lane < wlen + lq)
    gate_refs = (g0_ref, g1_ref)
    for g in range(NSA_KV_HEADS):
        kc_ref[0:wlen, :] = wpast_ref[pl.ds(g, wlen, stride=4), :]
        vc_ref[0:wlen, :] = wpast_ref[pl.ds(NSA_KV_HEADS + g, wlen, stride=4), :]
        kc_ref[wlen:, :] = jnp.zeros((nk - wlen, NSA_HD), F32)
        vc_ref[wlen:, :] = jnp.zeros((nk - wlen, NSA_HD), F32)
        kc_ref[wlen:wlen + lq, :] = wnew_ref[:, g * NSA_HD:(g + 1) * NSA_HD]
        vc_ref[wlen:wlen + lq, :] = wnew_ref[:, (NSA_KV_HEADS + g) * NSA_HD:(NSA_KV_HEADS + g + 1) * NSA_HD]
        q4 = jnp.concatenate([q_ref[:, (g * NSA_GROUP + r) * NSA_HD:(g * NSA_GROUP + r + 1) * NSA_HD]
                              for r in range(NSA_GROUP)], axis=0)
        s = _dot_nt(q4, kc_ref[...]) * SCALE + _bias_by_threshold(dist, tcol_ref[g])
        s = jnp.where(ok, s, NEG)
        p, den = _softmax_rows(s, ok)
        o_win = _dot(p, vc_ref[...]) / jnp.maximum(den, 1e-30)
        gates = jax.nn.sigmoid(gate_refs[g][...])
        for r in range(NSA_GROUP):
            h = g * NSA_GROUP + r
            sl = slice(h * NSA_HD, (h + 1) * NSA_HD)
            o = (gates[:, 3 * r:3 * r + 1] * ocmp_ref[:, sl] + gates[:, 3 * r + 1:3 * r + 2] * oslc_ref[:, sl]
                 + gates[:, 3 * r + 2:3 * r + 3] * o_win[r * lq:(r + 1) * lq])
            o_ref[:, sl] = o * _silu(z_ref[:, sl])


def _win_sample(proj, win4, tcols_rows, o_cmp, o_slc, layer, nb_batch, lq, past):
    wlen = win4.shape[2] // 4
    nk = -(-(wlen + lq) // LANES) * LANES
    row_spec = pl.BlockSpec((lq, NSA_WIDTH), lambda b: (b, 0))
    return pl.pallas_call(
        functools.partial(_win_sample_kernel, past=past, lq=lq, wlen=wlen),
        grid=(nb_batch,),
        in_specs=[pl.BlockSpec((lq, NSA_WIDTH), lambda b: (b, C_NQ // NSA_WIDTH)),
                  pl.BlockSpec((None, None, wlen * 4, NSA_HD), lambda b: (layer, b, 0, 0)),
                  pl.BlockSpec((lq, 2 * NSA_KV_W), lambda b: (b, C_WIN // (2 * NSA_KV_W))),
                  pl.BlockSpec((NSA_KV_HEADS, NSA_GROUP * lq, NUM_BUCKETS), lambda b: (0, 0, 0)),
                  row_spec, row_spec,
                  pl.BlockSpec((lq, NSA_WIDTH), lambda b: (b, C_NZ // NSA_WIDTH)),
                  pl.BlockSpec((lq, LANES), lambda b: (b, C_G0 // LANES)),
                  pl.BlockSpec((lq, LANES), lambda b: (b, C_G1 // LANES))],
        out_specs=row_spec,
        out_shape=jax.ShapeDtypeStruct((nb_batch * lq, NSA_WIDTH), F32),
        scratch_shapes=[pltpu.VMEM((nk, NSA_HD), F32), pltpu.VMEM((nk, NSA_HD), F32)],
        compiler_params=_params("parallel"),
        name="win_sample",
    )(proj, win4, proj, tcols_rows, o_cmp, o_slc, proj, proj, proj)


def _out_kernel(dn_ref, nsa_ref, x_ref, p_ref, wo_ref, gpost_ref, wpp_ref, gple_ref, wpg_ref, o_ref):
    half = dn_ref.shape[1]
    mixed = (jnp.dot(dn_ref[...].astype(BF16), wo_ref[0:half, :], preferred_element_type=F32)
             + jnp.dot(nsa_ref[...].astype(BF16), wo_ref[half:, :], preferred_element_type=F32))
    y = mixed * lax.rsqrt(jnp.mean(mixed * mixed, axis=-1, keepdims=True) + EPS) * gpost_ref[...]
    x1 = x_ref[...] + y
    e = jnp.dot(p_ref[...].astype(BF16), wpp_ref[...], preferred_element_type=F32)
    e = e * lax.rsqrt(jnp.mean(e * e, axis=-1, keepdims=True) + EPS) * gple_ref[...]
    gate = jax.nn.sigmoid(jnp.dot(x1.astype(BF16), wpg_ref[...], preferred_element_type=F32))
    o_ref[...] = x1 + gate * e


def _out_proj(dn, nsa, x, p_all, wo_all, gpost_all, wpp_all, gple_all, wpg_all, layer, tm):
    m = x.shape[0]
    once = pl.Buffered(1)

    def wspec(k):
        return pl.BlockSpec((None, k, D_MODEL), lambda i: (layer, 0, 0), pipeline_mode=once)

    return pl.pallas_call(
        _out_kernel,
        grid=(m // tm,),
        in_specs=[pl.BlockSpec((tm, dn.shape[1]), lambda i: (i, 0)),
                  pl.BlockSpec((tm, nsa.shape[1]), lambda i: (i, 0)),
                  pl.BlockSpec((tm, D_MODEL), lambda i: (i, 0)),
                  pl.BlockSpec((None, tm, PLE_DIM), lambda i: (layer, i, 0)),
                  wspec(D_MODEL), wspec(1), wspec(PLE_DIM), wspec(1), wspec(D_MODEL)],
        out_specs=pl.BlockSpec((tm, D_MODEL), lambda i: (i, 0)),
        out_shape=jax.ShapeDtypeStruct((m, D_MODEL), F32),
        compiler_params=_params("parallel"),
        name="out_proj",
    )(dn, nsa, x, p_all, wo_all, gpost_all, wpp_all, gple_all, wpg_all)


def _relayout_w_in(w_in):
    depth = w_in.shape[0]

    def zeros(n):
        return jnp.zeros((depth, D_MODEL, n), w_in.dtype)

    half = 3 * NSA_GROUP
    parts = [w_in[:, :, 0:_O_BETA], w_in[:, :, _O_NQ:_O_GATE],
             w_in[:, :, _O_BETA:_O_NQ], zeros(LANES - 2 * DN_HEADS),
             w_in[:, :, _O_GATE:_O_GATE + half], zeros(LANES - half),
             w_in[:, :, _O_GATE + half:_O_GATE + 2 * half], zeros(LANES - half),
             zeros(N_PAD - C_G1 - LANES)]
    return jnp.concatenate(parts, axis=-1).astype(BF16)


def _bias_expand_kernel(t_ref, bucket_ref, o_ref):
    bucket = bucket_ref[...]
    for h in range(NSA_HEADS):
        acc = jnp.full(bucket.shape, t_ref[0, h], F32)
        for b in range(1, NUM_BUCKETS):
            acc = jnp.where(bucket >= b, t_ref[b, h], acc)
        o_ref[h] = acc


def _bias_expand(table, bucket):
    rows, cols = bucket.shape
    tr = max(t for t in range(SUBLANES, min(rows, 512) + 1, SUBLANES) if rows % t == 0)
    return pl.pallas_call(
        _bias_expand_kernel,
        grid=(rows // tr,),
        in_specs=[pl.BlockSpec(memory_space=pltpu.SMEM), pl.BlockSpec((tr, cols), lambda i: (i, 0))],
        out_specs=pl.BlockSpec((NSA_HEADS, tr, cols), lambda i: (0, i, 0)),
        out_shape=jax.ShapeDtypeStruct((NSA_HEADS, rows, cols), F32),
        compiler_params=_params("parallel"),
        name="bias_expand",
    )(table, jnp.asarray(bucket))


def _bias_tiles(table):
    i = np.arange(PQ)
    d = np.arange(N_BIAS_TILES)[:, None, None] * PQ + i[None, :, None] - i[None, None, :]
    tiles = _bias_expand(table * LOG2E, _bucket_np(d).reshape(N_BIAS_TILES * PQ, PQ))
    return tiles.reshape(NSA_HEADS, N_BIAS_TILES, PQ, PQ)


def _bias_cmp(table, pos0, lq, nb):
    dist = (pos0 + np.arange(lq))[:, None] - (np.arange(nb) * NSA_BLOCK + NSA_BLOCK - 1)[None, :]
    return _bias_expand(table, _bucket_np(dist))


def kernel(x_prompt, x_sample, state_dn_S, state_dn_conv, cache_win_kv, cache_cmp_kv, cache_slc_kv, page_table,
           p_prompt, p_sample, rel_bias_table, w_in, w_out, g_pre, g_post, dn_conv_w, dn_A_log, dn_dt_bias,
           dn_norm_w, cmp_pos_w, cmp_w1, cmp_w2, w_ple_proj, g_ple, w_ple_gate):
    depth = w_in.shape[0]
    bp, seq, _ = x_prompt.shape
    bs, lq, _ = x_sample.shape
    n_pages = page_table.shape[1]
    past = n_pages * PAGE_SIZE
    n_pool = cache_cmp_kv.shape[1]
    wlen = cache_win_kv.shape[2]
    mp, ms = bp * seq, bs * lq
    assert seq % PK == 0 and seq % DN_CHUNK == 0 and lq == SUBLANES and past % NSA_BLOCK == 0
    kv_row = (2, NSA_KV_HEADS, NSA_HD)

    w_in_r = _relayout_w_in(w_in)
    w_out_b = w_out.astype(BF16)
    wpp_b = w_ple_proj.astype(BF16)
    wpg_b = w_ple_gate.astype(BF16)
    g_pre3 = g_pre[:, None, :]
    g_post3 = g_post[:, None, :]
    g_ple3 = g_ple[:, None, :]
    nw3 = dn_norm_w[:, None, :]
    lane_pad = ((0, 0), (DN_HEADS, LANES - 2 * DN_HEADS))
    dn_par = jnp.stack([jnp.pad(dn_A_log, lane_pad), jnp.pad(dn_dt_bias, lane_pad)], axis=1)
    pw_prompt = jnp.broadcast_to(cmp_pos_w[:, :, None, :, None], (depth, 2, NSA_KV_HEADS, NSA_BLOCK, NSA_HD))
    pw_prompt = jnp.transpose(pw_prompt, (0, 3, 1, 2, 4)).reshape(depth, NSA_BLOCK, 2 * NSA_KV_W)
    pw_page = jnp.broadcast_to(cmp_pos_w[:, None, None, :, :, None],
                               (depth, PAGE_SIZE // NSA_BLOCK, NSA_KV_HEADS, 2, NSA_BLOCK, NSA_HD))
    pw_page = jnp.transpose(pw_page, (0, 1, 4, 3, 2, 5)).reshape(depth, PAGE_SIZE * 4, NSA_HD)
    table = rel_bias_table.astype(F32)
    bias_tiles = _bias_tiles(table)
    nb_p = seq // NSA_BLOCK
    nb_s = past // NSA_BLOCK
    nbs_s = -(-(past + lq) // NSA_BLOCK)
    nbp_s = -(-nbs_s // LANES) * LANES
    bias_cmp_p = _bias_cmp(table, 0, seq, nb_p)
    bias_cmp_s = _bias_cmp(table, past, lq, nb_s)
    tcols = jnp.pad(table.T.reshape(NSA_KV_HEADS, NSA_GROUP, NUM_BUCKETS),
                    ((0, 0), (0, SUBLANES - NSA_GROUP), (0, 0)))
    tcols_rows = jnp.repeat(table.T.reshape(NSA_KV_HEADS, NSA_GROUP, NUM_BUCKETS), lq, axis=1)

    conv0_s = jnp.pad(state_dn_conv, ((0, 0), (0, 0), (SUBLANES - (CONV_W - 1), 0), (0, 0)))
    conv0_p = jnp.zeros((1, bp, SUBLANES, DN_CONV_CH), F32)
    s0_p = jnp.zeros((1, bp, DN_HEADS, DN_DK, DN_DV), F32)
    cmp4 = cache_cmp_kv.reshape(depth, n_pool, PAGE_SIZE * 4, NSA_HD)
    slc4 = cache_slc_kv.reshape(depth, n_pool, PAGE_SIZE * 4, NSA_HD)
    win4 = cache_win_kv.reshape(depth, bs, wlen * 4, NSA_HD)
    p_prompt3 = p_prompt.reshape(depth, mp, PLE_DIM)
    p_sample3 = p_sample.reshape(depth, ms, PLE_DIM)

    xp = x_prompt.reshape(mp, D_MODEL)
    xs = x_sample.reshape(ms, D_MODEL)
    outs_p = [[] for _ in range(5)]
    outs_s = [[] for _ in range(5)]
    tq_p = min(256, seq)
    for i in range(depth):
        proj = _in_proj(xp, g_pre3, w_in_r, i, min(1024, mp))
        dn_o, conv_n, s_n = _deltanet(proj, bp, seq, DN_CHUNK, conv0_p, s0_p, 0, dn_conv_w, dn_par, nw3, i)
        ckv = _cmp_mlp(_pool_prompt(proj, bp, seq, pw_prompt, i), cmp_w1, cmp_w2, i)
        o_cmp, sel = _cmp_attn(proj, ckv, bias_cmp_p, bp, seq, 0, tq_p, nb_p, nb_p, True)
        nsa_o = _nsa_prompt(proj, sel, bias_tiles, o_cmp, bp, seq)
        xp = _out_proj(dn_o, nsa_o, xp, p_prompt3, w_out_b, g_post3, wpp_b, g_ple3, wpg_b, i, min(256, mp))
        kv_all = proj[:, C_CMP:C_WIN + 2 * NSA_KV_W].reshape(bp, seq, 3, *kv_row)
        outs_p[0].append(s_n)
        outs_p[1].append(conv_n[:, SUBLANES - (CONV_W - 1):])
        outs_p[2].append(kv_all[:, seq - min(WINDOW, seq):, 2])
        outs_p[3].append(kv_all[:, :, 0])
        outs_p[4].append(kv_all[:, :, 1])

        proj = _in_proj(xs, g_pre3, w_in_r, i, ms)
        dn_o, conv_n, s_n = _deltanet(proj, bs, lq, lq, conv0_s, state_dn_S, i, dn_conv_w, dn_par, nw3, i)
        pooled = _pool_paged(cmp4, page_table, pw_page, i, min(16, n_pages))
        ckv = _cmp_mlp(pooled, cmp_w1, cmp_w2, i)
        o_cmp, score = _cmp_attn(proj, ckv, bias_cmp_s, bs, lq, past, lq, nbs_s, nbp_s, False)
        score_t = jnp.transpose(score.reshape(ms * NSA_KV_HEADS, nbp_s))
        _, idx_t, val_t = _topk(score_t, nbs_s)
        o_slc = _slc_sample(proj, slc4, page_table, idx_t.T.reshape(-1), val_t.T.reshape(-1), tcols, i,
                            bs, lq, past)
        nsa_o = _win_sample(proj, win4, tcols_rows, o_cmp, o_slc, i, bs, lq, past)
        xs = _out_proj(dn_o, nsa_o, xs, p_sample3, w_out_b, g_post3, wpp_b, g_ple3, wpg_b, i, ms)
        kv_all = proj[:, C_CMP:C_WIN + 2 * NSA_KV_W].reshape(bs, lq, 3, *kv_row)
        win_all = jnp.concatenate([cache_win_kv[i], kv_all[:, :, 2]], axis=1)
        outs_s[0].append(s_n)
        outs_s[1].append(conv_n[:, SUBLANES - (CONV_W - 1):])
        outs_s[2].append(win_all[:, win_all.shape[1] - min(WINDOW, past + lq):])
        outs_s[3].append(kv_all[:, :, 0])
        outs_s[4].append(kv_all[:, :, 1])

    return ((xp.reshape(bp, seq, D_MODEL), xs.reshape(bs, lq, D_MODEL))
            + tuple(jnp.stack(o) for o in outs_p) + tuple(jnp.stack(o) for o in outs_s))
```

```python
import functools
import math

import numpy as np
import jax
import jax.numpy as jnp
from jax import lax
from jax.experimental import pallas as pl
from jax.experimental.pallas import tpu as pltpu

F32 = jnp.float32
BF16 = jnp.bfloat16

D_MODEL = 2048
DN_HEADS = 8
DN_DK = 128
DN_DV = 128
DN_QK_W = DN_HEADS * DN_DK
DN_CONV_CH = 2 * DN_QK_W + DN_HEADS * DN_DV
CONV_W = 4
DN_CHUNK = 64
NSA_HEADS = 8
NSA_KV_HEADS = 2
NSA_GROUP = 4
NSA_HD = 128
NSA_WIDTH = NSA_HEADS * NSA_HD
NSA_KV_W = NSA_KV_HEADS * NSA_HD
NSA_BLOCK = 64
NSA_TOPN = 16
WINDOW = 512
SCALE = NSA_HD ** -0.5
NUM_BUCKETS = 32
REL_MAX_DIST = 1024
PLE_DIM = 256
PAGE_SIZE = 128
EPS = 1e-6
NEG = -1e30
LOG2E = math.log2(math.e)

VMEM_LIMIT_BYTES = 56 * 1024 * 1024
LANES = 128
SUBLANES = 8

C_QKV = 0
C_Z = 3072
C_NQ = 4096
C_NZ = 5120
C_CMP = 6144
C_SLC = 6656
C_WIN = 7168
C_SM = 7680
C_G0 = 7808
C_G1 = 7936
N_PAD = 8192
_O_BETA = DN_CONV_CH + DN_HEADS * DN_DV
_O_NQ = _O_BETA + 2 * DN_HEADS
_O_GATE = _O_NQ + 2 * NSA_WIDTH + 3 * 2 * NSA_KV_W
N_BIAS_TILES = 9


def _bucket_np(dist):
    dist = np.maximum(np.asarray(dist, np.int64), 0)
    exact = NUM_BUCKETS // 2
    scaled = np.log(np.maximum(dist, 1).astype(np.float64) / exact) / math.log(REL_MAX_DIST / exact)
    large = np.minimum(exact + (scaled * (NUM_BUCKETS - exact)).astype(np.int64), NUM_BUCKETS - 1)
    return np.where(dist < exact, dist, large).astype(np.int32)


_BUCKET_THR = tuple(int(np.argmax(_bucket_np(np.arange(2 * REL_MAX_DIST)) >= b)) for b in range(1, NUM_BUCKETS))


def _dot(a, b):
    return jnp.dot(a.astype(BF16), b.astype(BF16), preferred_element_type=F32)


def _dot_nt(a, b):
    return lax.dot_general(a.astype(BF16), b.astype(BF16), (((1,), (1,)), ((), ())), preferred_element_type=F32)


def _dot_tn(a, b):
    return lax.dot_general(a.astype(BF16), b.astype(BF16), (((0,), (0,)), ((), ())), preferred_element_type=F32)


def _dot_hi(a, b):
    return jnp.dot(a, b, preferred_element_type=F32, precision=lax.Precision.HIGHEST)


def _dot_hi_nt(a, b):
    return lax.dot_general(a, b, (((1,), (1,)), ((), ())), preferred_element_type=F32,
                           precision=lax.Precision.HIGHEST)


def _silu(x):
    return x * jax.nn.sigmoid(x)


def _params(*sem):
    return pltpu.CompilerParams(dimension_semantics=sem, vmem_limit_bytes=VMEM_LIMIT_BYTES)


def _in_proj_kernel(x_ref, g_ref, w_ref, o_ref, xn_ref):
    @pl.when(pl.program_id(1) == 0)
    def _():
        x = x_ref[...]
        y = x * lax.rsqrt(jnp.mean(x * x, axis=-1, keepdims=True) + EPS)
        xn_ref[...] = (y * g_ref[...]).astype(BF16)

    o_ref[...] = jnp.dot(xn_ref[...], w_ref[...], preferred_element_type=F32)


def _in_proj(x, g_all, w_all, layer, tm):
    m = x.shape[0]
    tn = 1024
    return pl.pallas_call(
        _in_proj_kernel,
        grid=(m // tm, N_PAD // tn),
        in_specs=[pl.BlockSpec((tm, D_MODEL), lambda i, j: (i, 0)),
                  pl.BlockSpec((None, 1, D_MODEL), lambda i, j: (layer, 0, 0)),
                  pl.BlockSpec((None, D_MODEL, tn), lambda i, j: (layer, 0, j))],
        out_specs=pl.BlockSpec((tm, tn), lambda i, j: (i, j)),
        out_shape=jax.ShapeDtypeStruct((m, N_PAD), F32),
        scratch_shapes=[pltpu.VMEM((tm, D_MODEL), BF16)],
        compiler_params=_params("parallel", "arbitrary"),
        name="in_proj",
    )(x, g_all, w_all)


def _dn_prep_kernel(qkv_ref, sm_ref, conv0_ref, cw_ref, par_ref,
                    u_ref, w_ref, qd_ref, kd_ref, qk_ref, gl_ref, convn_ref, xs_ref, *, chunk, cs):
    tb = chunk * cs
    halo = SUBLANES

    @pl.when(pl.program_id(1) == 0)
    def _():
        xs_ref[0:halo, :] = conv0_ref[...]

    xs_ref[halo:halo + tb, :] = qkv_ref[...]
    base = halo - (CONV_W - 1)
    y = xs_ref[base:base + tb, :] * cw_ref[0:1, :]
    for j in range(1, CONV_W):
        y = y + xs_ref[base + j:base + j + tb, :] * cw_ref[j:j + 1, :]
    y = _silu(y)
    tail = xs_ref[tb:tb + halo, :]
    xs_ref[0:halo, :] = tail
    convn_ref[...] = tail

    sm = sm_ref[...]
    beta_all = jax.nn.sigmoid(sm)
    xg = sm + par_ref[1:2, :]
    softplus = jnp.maximum(xg, 0.0) + jnp.log1p(jnp.exp(-jnp.abs(xg)))
    g_all = -jnp.exp(par_ref[0:1, :]) * softplus
    row = lax.broadcasted_iota(jnp.int32, (tb, tb), 0)
    col = lax.broadcasted_iota(jnp.int32, (tb, tb), 1)
    in_chunk_lower = (row >= col) & (row // chunk == col // chunk)
    gc_all = _dot_hi(jnp.where(in_chunk_lower, 1.0, 0.0), g_all)
    pick = (lax.broadcasted_iota(jnp.int32, (DN_HEADS, LANES), 1)
            == lax.broadcasted_iota(jnp.int32, (DN_HEADS, LANES), 0) + DN_HEADS).astype(F32)
    gc_rows = _dot_hi_nt(pick, gc_all)
    for c in range(cs):
        gl_ref[c] = gc_all[(c + 1) * chunk - 1:(c + 1) * chunk, :]
    r64 = lax.broadcasted_iota(jnp.int32, (chunk, chunk), 0)
    c64 = lax.broadcasted_iota(jnp.int32, (chunk, chunk), 1)
    lower = r64 >= c64
    strict = r64 > c64

    lhs, kns, rhs, decays = [], [], [], []
    for h in range(DN_HEADS):
        hc = slice(h * DN_DV, (h + 1) * DN_DV)
        qh = y[:, h * DN_DK:(h + 1) * DN_DK]
        kh = y[:, DN_QK_W + h * DN_DK:DN_QK_W + (h + 1) * DN_DK]
        vh = y[:, 2 * DN_QK_W + h * DN_DV:2 * DN_QK_W + (h + 1) * DN_DV]
        qn = qh * lax.rsqrt(jnp.sum(qh * qh, axis=-1, keepdims=True) + EPS) * (DN_DK ** -0.5)
        kn = kh * lax.rsqrt(jnp.sum(kh * kh, axis=-1, keepdims=True) + EPS)
        beta = beta_all[:, h:h + 1]
        gc = gc_all[:, DN_HEADS + h:DN_HEADS + h + 1]
        egc = jnp.exp(gc)
        kb = kn * beta
        rhs_h = jnp.concatenate([vh * beta, kb * egc], axis=-1)
        qd_ref[:, hc] = (qn * egc).astype(qd_ref.dtype)
        for c in range(cs):
            rs = slice(c * chunk, (c + 1) * chunk)
            gcc = gc[rs]
            decays.append(jnp.exp(jnp.where(lower, gcc - gc_rows[h:h + 1, rs], NEG)))
            lhs.append(jnp.concatenate([kb[rs], qn[rs]], axis=0))
            kns.append(kn[rs])
            rhs.append(rhs_h[rs])
            kd_ref[rs, hc] = (kn[rs] * jnp.exp(gcc[chunk - 1:chunk, :] - gcc)).astype(kd_ref.dtype)
    decay = jnp.stack(decays)
    both = jnp.einsum('nid,njd->nij', jnp.stack(lhs).astype(BF16), jnp.stack(kns).astype(BF16),
                      preferred_element_type=F32)
    a = jnp.where(strict[None], both[:, :chunk] * decay, 0.0)
    qk = both[:, chunk:] * decay

    def bmm(p, q):
        return jnp.einsum('nij,njk->nik', p, q, preferred_element_type=F32)

    def split(v):
        hi = v.astype(BF16)
        return hi, (v - hi.astype(F32)).astype(BF16)

    def bmm3(p, q):
        (ph, pl_), (qh, ql) = p, q
        return bmm(ph, qh) + (bmm(ph, ql) + bmm(pl_, qh))

    x = jnp.stack(rhs)
    a_s = split(a)
    x = x - bmm3(a_s, split(x))
    pw = bmm3(a_s, a_s)
    x = x + bmm3(split(pw), split(x))
    for _ in range(int(math.log2(chunk)) - 2):
        pw = bmm(pw.astype(BF16), pw.astype(BF16))
        x = x + bmm(pw.astype(BF16), x.astype(BF16))
    for h in range(DN_HEADS):
        hc = slice(h * DN_DV, (h + 1) * DN_DV)
        for c in range(cs):
            rs = slice(c * chunk, (c + 1) * chunk)
            n = h * cs + c
            u_ref[rs, hc] = x[n, :, :DN_DV]
            w_ref[rs, hc] = x[n, :, DN_DV:].astype(w_ref.dtype)
            qk_ref[rs, h * chunk:(h + 1) * chunk] = qk[n]


def _dn_scan_kernel(u_ref, w_ref, qd_ref, kd_ref, qk_ref, gl_ref, z_ref, s0_ref, nw_ref,
                    o_ref, sout_ref, s_ref, *, chunk, cs):
    @pl.when(pl.program_id(1) == 0)
    def _():
        s_ref[...] = s0_ref[...]

    def heads(ref, rs, width):
        return jnp.stack([ref[rs, h * width:(h + 1) * width] for h in range(DN_HEADS)])

    def bdot(spec, p, q):
        return jnp.einsum(spec, p.astype(BF16), q.astype(BF16), preferred_element_type=F32)

    for c in range(cs):
        rs = slice(c * chunk, (c + 1) * chunk)
        decay_last = jnp.exp(gl_ref[c])
        decay_last = jnp.stack([decay_last[:, DN_HEADS + h:DN_HEADS + h + 1] for h in range(DN_HEADS)])
        s = s_ref[...]
        sb = s.astype(BF16)
        v_new = heads(u_ref, rs, DN_DV) - bdot('hcd,hde->hce', heads(w_ref, rs, DN_DK), sb)
        o = (bdot('hcd,hde->hce', heads(qd_ref, rs, DN_DK), sb)
             + bdot('hcs,hse->hce', heads(qk_ref, rs, chunk), v_new))
        s_ref[...] = s * decay_last + bdot('hcd,hce->hde', heads(kd_ref, rs, DN_DK), v_new)
        on = o * lax.rsqrt(jnp.mean(o * o, axis=-1, keepdims=True) + EPS) * nw_ref[...]
        for h in range(DN_HEADS):
            hc = slice(h * DN_DV, (h + 1) * DN_DV)
            o_ref[rs, hc] = on[h] * _silu(z_ref[rs, hc])

    @pl.when(pl.program_id(1) == pl.num_programs(1) - 1)
    def _():
        sout_ref[...] = s_ref[...]


def _deltanet(proj, nb, seq, chunk, conv0_all, s0_all, state_layer, cw_all, par_all, nw_all, layer):
    m = nb * seq
    nchunks = seq // chunk
    width = DN_HEADS * DN_DV
    hs = (DN_HEADS, DN_DK, DN_DV)
    cs_a = 2 if nchunks % 2 == 0 else 1
    na = nchunks // cs_a
    ta = cs_a * chunk
    op_dtype = BF16 if chunk % (2 * SUBLANES) == 0 else F32
    u, w, qd, kd, qk, gl, conv_n = pl.pallas_call(
        functools.partial(_dn_prep_kernel, chunk=chunk, cs=cs_a),
        grid=(nb, na),
        in_specs=[pl.BlockSpec((ta, DN_CONV_CH), lambda b, c: (b * na + c, C_QKV // DN_CONV_CH)),
                  pl.BlockSpec((ta, LANES), lambda b, c: (b * na + c, C_SM // LANES)),
                  pl.BlockSpec((None, None, SUBLANES, DN_CONV_CH), lambda b, c: (state_layer, b, 0, 0)),
                  pl.BlockSpec((None, CONV_W, DN_CONV_CH), lambda b, c: (layer, 0, 0)),
                  pl.BlockSpec((None, 2, LANES), lambda b, c: (layer, 0, 0))],
        out_specs=[pl.BlockSpec((ta, width), lambda b, c: (b * na + c, 0)),
                   pl.BlockSpec((ta, width), lambda b, c: (b * na + c, 0)),
                   pl.BlockSpec((ta, width), lambda b, c: (b * na + c, 0)),
                   pl.BlockSpec((ta, width), lambda b, c: (b * na + c, 0)),
                   pl.BlockSpec((ta, DN_HEADS * chunk), lambda b, c: (b * na + c, 0)),
                   pl.BlockSpec((cs_a, 1, LANES), lambda b, c: (b * na + c, 0, 0)),
                   pl.BlockSpec((None, SUBLANES, DN_CONV_CH), lambda b, c: (b, 0, 0))],
        out_shape=[jax.ShapeDtypeStruct((m, width), F32),
                   jax.ShapeDtypeStruct((m, width), op_dtype),
                   jax.ShapeDtypeStruct((m, width), op_dtype),
                   jax.ShapeDtypeStruct((m, width), op_dtype),
                   jax.ShapeDtypeStruct((m, DN_HEADS * chunk), F32),
                   jax.ShapeDtypeStruct((nb * nchunks, 1, LANES), F32),
                   jax.ShapeDtypeStruct((nb, SUBLANES, DN_CONV_CH), F32)],
        scratch_shapes=[pltpu.VMEM((ta + SUBLANES, DN_CONV_CH), F32)],
        compiler_params=_params("parallel", "arbitrary"),
        name="dn_prep",
    )(proj, proj, conv0_all, cw_all, par_all)

    cs_b = 4 if nchunks % 4 == 0 else 1
    nbk = nchunks // cs_b
    tbk = cs_b * chunk
    row_spec = pl.BlockSpec((tbk, width), lambda b, c: (b * nbk + c, 0))
    o, s_n = pl.pallas_call(
        functools.partial(_dn_scan_kernel, chunk=chunk, cs=cs_b),
        grid=(nb, nbk),
        in_specs=[row_spec, row_spec, row_spec, row_spec,
                  pl.BlockSpec((tbk, DN_HEADS * chunk), lambda b, c: (b * nbk + c, 0)),
                  pl.BlockSpec((cs_b, 1, LANES), lambda b, c: (b * nbk + c, 0, 0)),
                  pl.BlockSpec((tbk, width), lambda b, c: (b * nbk + c, C_Z // width)),
                  pl.BlockSpec((None, None) + hs, lambda b, c: (state_layer, b, 0, 0, 0)),
                  pl.BlockSpec((None, 1, DN_DV), lambda b, c: (layer, 0, 0))],
        out_specs=[row_spec, pl.BlockSpec((None,) + hs, lambda b, c: (b, 0, 0, 0))],
        out_shape=[jax.ShapeDtypeStruct((m, width), F32), jax.ShapeDtypeStruct((nb,) + hs, F32)],
        scratch_shapes=[pltpu.VMEM(hs, F32)],
        compiler_params=_params("parallel", "arbitrary"),
        name="dn_scan",
    )(u, w, qd, kd, qk, gl, proj, s0_all, nw_all)
    return o, conv_n, s_n


def _pool_kernel(x_ref, pw_ref, o_ref):
    rows = x_ref.shape[0]
    x = x_ref[...].reshape(rows // NSA_BLOCK, NSA_BLOCK, 2 * NSA_KV_W)
    o_ref[...] = jnp.sum(x * pw_ref[...][None], axis=1)


def _pool_prompt(proj, nb, seq, pw_all, layer):
    rows = min(seq, 1024)
    nr = seq // rows
    width = 2 * NSA_KV_W
    return pl.pallas_call(
        _pool_kernel,
        grid=(nb, nr),
        in_specs=[pl.BlockSpec((rows, width), lambda b, r: (b * nr + r, C_CMP // width)),
                  pl.BlockSpec((None, NSA_BLOCK, width), lambda b, r: (layer, 0, 0))],
        out_specs=pl.BlockSpec((None, rows // NSA_BLOCK, width), lambda b, r: (b, r, 0)),
        out_shape=jax.ShapeDtypeStruct((nb, seq // NSA_BLOCK, width), F32),
        compiler_params=_params("parallel", "parallel"),
        name="pool_prompt",
    )(proj, pw_all)


def _paged_pool_kernel(pt_ref, *refs, pages):
    del pt_ref
    page_refs = refs[:pages]
    pw_ref = refs[pages]
    o_ref = refs[pages + 1]
    rows = PAGE_SIZE * 4
    per_block = rows // 2 // SUBLANES
    for k in range(pages):
        prod = page_refs[k][...] * pw_ref[...]
        y = jnp.sum(prod.reshape(2, per_block, SUBLANES, NSA_HD), axis=1)
        y = y[:, 0:4, :] + y[:, 4:8, :]
        for s in range(4):
            o_ref[2 * k:2 * k + 2, s * NSA_HD:(s + 1) * NSA_HD] = y[:, s, :]


def _pool_paged(cache4, page_table, pw_all, layer, pages):
    nb, n_pages = page_table.shape
    steps = n_pages // pages
    rows = PAGE_SIZE * 4

    def page_spec(k):
        return pl.BlockSpec((None, None, rows, NSA_HD),
                            lambda b, s, pt: (layer, pt[b, s * pages + k], 0, 0))

    grid_spec = pltpu.PrefetchScalarGridSpec(
        num_scalar_prefetch=1,
        grid=(nb, steps),
        in_specs=[page_spec(k) for k in range(pages)]
        + [pl.BlockSpec((None, rows, NSA_HD), lambda b, s, pt: (layer, 0, 0))],
        out_specs=pl.BlockSpec((None, 2 * pages, 4 * NSA_HD), lambda b, s, pt: (b, s, 0)),
    )
    return pl.pallas_call(
        functools.partial(_paged_pool_kernel, pages=pages),
        grid_spec=grid_spec,
        out_shape=jax.ShapeDtypeStruct((nb, 2 * n_pages, 4 * NSA_HD), F32),
        compiler_params=_params("parallel", "arbitrary"),
        name="pool_paged",
    )(page_table, *([cache4] * pages), pw_all)


def _cmp_mlp_kernel(p_ref, w1_ref, w2_ref, o_ref):
    for s in range(4):
        kv = s // NSA_KV_HEADS
        h = _silu(_dot_hi(p_ref[:, s * NSA_HD:(s + 1) * NSA_HD], w1_ref[kv]))
        o_ref[:, s * NSA_HD:(s + 1) * NSA_HD] = _dot_hi(h, w2_ref[kv])


def _cmp_mlp(pooled, w1_all, w2_all, layer):
    nb, n, width = pooled.shape
    wspec = pl.BlockSpec((None, 2, NSA_HD, NSA_HD), lambda b: (layer, 0, 0, 0))
    return pl.pallas_call(
        _cmp_mlp_kernel,
        grid=(nb,),
        in_specs=[pl.BlockSpec((None, n, width), lambda b: (b, 0, 0)), wspec, wspec],
        out_specs=pl.BlockSpec((None, n, width), lambda b: (b, 0, 0)),
        out_shape=jax.ShapeDtypeStruct((nb, n, width), F32),
        compiler_params=_params("parallel"),
        name="cmp_mlp",
    )(pooled, w1_all, w2_all)


def _select_blocks(score, st_ref, nbp):
    st_ref[...] = score.T
    halves = [st_ref[g * nbp:(g + 1) * nbp, :] for g in range(NSA_KV_HEADS)]
    bi = lax.broadcasted_iota(jnp.int32, halves[0].shape, 0)

    def body(j, ranks):
        tie = jnp.where(j < bi, 1, 0)
        out = []
        for g in range(NSA_KV_HEADS):
            other = st_ref[pl.ds(g * nbp + j, 1), :]
            out.append(ranks[g] + jnp.where(other > halves[g], 1, jnp.where(other == halves[g], tie, 0)))
        return tuple(out)

    ranks = lax.fori_loop(0, nbp, body, tuple(jnp.zeros(bi.shape, jnp.int32) for _ in halves))
    keep = [jnp.where((ranks[g] < NSA_TOPN) & (halves[g] >= 0.0), 1.0, 0.0) for g in range(NSA_KV_HEADS)]
    return jnp.concatenate(keep, axis=0).T


def _cmp_attn_kernel(q_ref, ckv_ref, bias_ref, o_ref, sc_ref, *scratch, pos0, tq, nb, nbs, nbp, select):
    qi = pl.program_id(1)
    t = pos0 + qi * tq + lax.broadcasted_iota(jnp.int32, (tq, 1), 0)
    blk = lax.broadcasted_iota(jnp.int32, (1, nb), 1)
    valid = t >= blk * NSA_BLOCK + (NSA_BLOCK - 1)
    bi = lax.broadcasted_iota(jnp.int32, (1, nbp), 1)
    cur = t // NSA_BLOCK
    forced = (bi == 0) | (bi == cur) | (bi == cur - 1)
    causal = bi <= cur
    scores = []
    for g in range(NSA_KV_HEADS):
        ck = ckv_ref[:, g * NSA_HD:(g + 1) * NSA_HD]
        cv = ckv_ref[:, (NSA_KV_HEADS + g) * NSA_HD:(NSA_KV_HEADS + g + 1) * NSA_HD]
        imp = jnp.zeros((tq, nb), F32)
        for r in range(NSA_GROUP):
            h = g * NSA_GROUP + r
            logits = _dot_nt(q_ref[:, h * NSA_HD:(h + 1) * NSA_HD], ck) * SCALE + bias_ref[h]
            logits = jnp.where(valid, logits, NEG)
            mx = jnp.max(logits, axis=-1, keepdims=True)
            p = jnp.where(valid, jnp.exp(logits - mx), 0.0)
            p = p / jnp.maximum(jnp.sum(p, axis=-1, keepdims=True), 1e-30)
            o_ref[:, h * NSA_HD:(h + 1) * NSA_HD] = _dot(p, cv)
            imp = imp + p
        if nbp > nb:
            imp = jnp.concatenate([imp, jnp.zeros((tq, nbp - nb), F32)], axis=-1)
        score = jnp.where(causal, jnp.where(forced, NSA_GROUP + 1.0, imp), -1.0)
        scores.append(jnp.where(bi < nbs, score, -2.0))
    score = jnp.concatenate(scores, axis=-1)
    sc_ref[...] = _select_blocks(score, scratch[0], nbp) if select else score


def _cmp_attn(proj, ckv, bias_cmp, nb_batch, seq, pos0, tq, nbs, nbp, select):
    nq = seq // tq
    nb = ckv.shape[1]
    m = nb_batch * seq
    bias_tq = bias_cmp.shape[1] // nq
    return pl.pallas_call(
        functools.partial(_cmp_attn_kernel, pos0=pos0, tq=tq, nb=nb, nbs=nbs, nbp=nbp, select=select),
        scratch_shapes=[pltpu.VMEM((NSA_KV_HEADS * nbp, tq), F32)] if select else [],
        grid=(nb_batch, nq),
        in_specs=[pl.BlockSpec((tq, NSA_WIDTH), lambda b, i: (b * nq + i, C_NQ // NSA_WIDTH)),
                  pl.BlockSpec((None, nb, 2 * NSA_KV_W), lambda b, i: (b, 0, 0)),
                  pl.BlockSpec((NSA_HEADS, bias_tq, nb), lambda b, i: (0, i, 0))],
        out_specs=[pl.BlockSpec((tq, NSA_WIDTH), lambda b, i: (b * nq + i, 0)),
                   pl.BlockSpec((tq, NSA_KV_HEADS * nbp), lambda b, i: (b * nq + i, 0))],
        out_shape=[jax.ShapeDtypeStruct((m, NSA_WIDTH), F32),
                   jax.ShapeDtypeStruct((m, NSA_KV_HEADS * nbp), F32)],
        compiler_params=_params("parallel", "parallel"),
        name="cmp_attn",
    )(proj, ckv, bias_cmp)


def _topk_kernel(s_ref, sel_ref, idx_ref, val_ref, *, nbs):
    s = s_ref[...]
    bi = lax.broadcasted_iota(jnp.int32, s.shape, 0)

    def body(j, rank):
        other = s_ref[pl.ds(j, 1), :]
        beats = (other > s) | ((other == s) & (j < bi))
        return rank + beats.astype(jnp.int32)

    rank = lax.fori_loop(0, nbs, body, jnp.zeros(s.shape, jnp.int32))
    ok = s >= 0.0
    sel_ref[...] = jnp.where((rank < NSA_TOPN) & ok, 1.0, 0.0)
    for r in range(NSA_TOPN):
        hit = rank == r
        idx_ref[r:r + 1, :] = jnp.sum(jnp.where(hit, bi, 0), axis=0, keepdims=True)
        val_ref[r:r + 1, :] = jnp.sum(jnp.where(hit & ok, 1, 0), axis=0, keepdims=True)


def _topk(score_t, nbs):
    rows, cols = score_t.shape
    tc = min(cols, 512)
    return pl.pallas_call(
        functools.partial(_topk_kernel, nbs=nbs),
        grid=(cols // tc,),
        in_specs=[pl.BlockSpec((rows, tc), lambda i: (0, i))],
        out_specs=[pl.BlockSpec((rows, tc), lambda i: (0, i)),
                   pl.BlockSpec((NSA_TOPN, tc), lambda i: (0, i)),
                   pl.BlockSpec((NSA_TOPN, tc), lambda i: (0, i))],
        out_shape=[jax.ShapeDtypeStruct((rows, cols), F32),
                   jax.ShapeDtypeStruct((NSA_TOPN, cols), jnp.int32),
                   jax.ShapeDtypeStruct((NSA_TOPN, cols), jnp.int32)],
        compiler_params=_params("parallel"),
        name="topk",
    )(score_t)


PQ = 128
PK = 512


def _softmax_rows(s, ok):
    mx = jnp.max(s, axis=-1, keepdims=True)
    p = jnp.where(ok, jnp.exp(s - mx), 0.0)
    return p, jnp.sum(p, axis=-1, keepdims=True)


def _nsa_prompt_kernel(q_ref, ks_ref, vs_ref, kw_ref, vw_ref, sel_ref, bt_ref, ocmp_ref, z_ref, gate_ref,
                       o_ref, s_ref, p_ref, sw_ref, pw_ref, m_ref, l_ref, acc_ref, *, nq, wt):
    qi = pl.program_id(2)
    rows = NSA_GROUP * PQ
    q4 = jnp.concatenate([q_ref[:, r * NSA_HD:(r + 1) * NSA_HD] for r in range(NSA_GROUP)], axis=0)
    q4 = (q4 * (SCALE * LOG2E)).astype(BF16)
    t = qi * PQ + lax.broadcasted_iota(jnp.int32, (PQ, 1), 0)
    nsel = sel_ref.shape[1]
    blk0 = pl.program_id(1) * (nsel // NSA_KV_HEADS)
    sub = PK // PQ
    unselected = jnp.where(sel_ref[...] > 0.5, 0.0, NEG).astype(BF16)
    qx = jnp.concatenate([q4, jnp.concatenate([unselected] * NSA_GROUP, axis=0)], axis=1)

    m_ref[...] = jnp.full((rows, PQ), NEG, F32)
    l_ref[...] = jnp.zeros((rows, PQ), F32)
    acc_ref[...] = jnp.zeros((rows, NSA_HD), F32)

    def key_tile(kt, diagonal):
        k0 = pl.multiple_of(kt * PK, PK)
        member = (lax.broadcasted_iota(jnp.int32, (PK, nsel), 1)
                  == blk0 + kt * (PK // NSA_BLOCK) + lax.broadcasted_iota(jnp.int32, (PK, nsel), 0) // NSA_BLOCK)
        kx = jnp.concatenate([ks_ref[pl.ds(k0, PK), :].astype(BF16), jnp.where(member, 1.0, 0.0).astype(BF16)],
                             axis=1)
        s_ref[...] = lax.dot_general(qx, kx, (((1,), (1,)), ((), ())), preferred_element_type=F32)
        if diagonal:
            future = jnp.where(k0 + lax.broadcasted_iota(jnp.int32, (1, PK), 1) <= t, 0.0, NEG)
        for r in range(NSA_GROUP):
            rs = slice(r * PQ, (r + 1) * PQ)
            zmax = None
            for c in range(sub):
                cs = slice(c * PQ, (c + 1) * PQ)
                d = jnp.clip(qi - (kt * sub + c), 0, N_BIAS_TILES - 1)
                z = s_ref[rs, cs] + bt_ref[r, d]
                if diagonal:
                    z = z + future[:, cs]
                s_ref[rs, cs] = z
                zmax = z if zmax is None else jnp.maximum(zmax, z)
            m_old = m_ref[rs]
            mx = jnp.maximum(m_old, jnp.broadcast_to(jnp.max(zmax, axis=-1, keepdims=True), (PQ, PQ)))
            alpha = jnp.exp2(m_old - mx)
            psum = None
            for c in range(sub):
                cs = slice(c * PQ, (c + 1) * PQ)
                p = jnp.exp2(s_ref[rs, cs] - mx)
                p_ref[rs, cs] = p.astype(BF16)
                psum = p if psum is None else psum + p
            l_ref[rs] = alpha * l_ref[rs] + psum
            acc_ref[rs] = alpha * acc_ref[rs]
            m_ref[rs] = mx
        acc_ref[...] += jnp.dot(p_ref[...], vs_ref[pl.ds(k0, PK), :].astype(BF16), preferred_element_type=F32)

    last = (qi * PQ) // PK

    def body(kt, carry):
        key_tile(kt, False)
        return carry

    lax.fori_loop(0, last, body, 0)
    key_tile(last, True)
    o_slc = acc_ref[...] / jnp.maximum(jnp.sum(l_ref[...], axis=-1, keepdims=True), 1e-30)

    wk = wt * PQ
    kst = jnp.clip(qi - (wt - 1), 0, nq - wt)
    k0 = pl.multiple_of(kst * PQ, PQ)
    sw_ref[...] = lax.dot_general(q4, kw_ref[pl.ds(k0, wk), :].astype(BF16), (((1,), (1,)), ((), ())),
                                  preferred_element_type=F32)
    dist = t - (k0 + lax.broadcasted_iota(jnp.int32, (1, wk), 1))
    outside = jnp.where((dist >= 0) & (dist <= WINDOW), 0.0, NEG)
    dens = []
    for r in range(NSA_GROUP):
        rs = slice(r * PQ, (r + 1) * PQ)
        zmax = None
        for c in range(wt):
            cs = slice(c * PQ, (c + 1) * PQ)
            d = jnp.clip(qi - (kst + c), 0, N_BIAS_TILES - 1)
            z = sw_ref[rs, cs] + bt_ref[r, d] + outside[:, cs]
            sw_ref[rs, cs] = z
            zmax = z if zmax is None else jnp.maximum(zmax, z)
        mx = jnp.broadcast_to(jnp.max(zmax, axis=-1, keepdims=True), (PQ, PQ))
        psum = None
        for c in range(wt):
            cs = slice(c * PQ, (c + 1) * PQ)
            p = jnp.exp2(sw_ref[rs, cs] - mx)
            pw_ref[rs, cs] = p.astype(BF16)
            psum = p if psum is None else psum + p
        dens.append(jnp.sum(psum, axis=-1, keepdims=True))
    pv = jnp.dot(pw_ref[...], vw_ref[pl.ds(k0, wk), :].astype(BF16), preferred_element_type=F32)
    o_win = [pv[r * PQ:(r + 1) * PQ] / jnp.maximum(dens[r], 1e-30) for r in range(NSA_GROUP)]

    gates = jax.nn.sigmoid(gate_ref[...])
    for r in range(NSA_GROUP):
        sl = slice(r * NSA_HD, (r + 1) * NSA_HD)
        rs = slice(r * PQ, (r + 1) * PQ)
        o = (gates[:, 3 * r:3 * r + 1] * ocmp_ref[:, sl] + gates[:, 3 * r + 1:3 * r + 2] * o_slc[rs]
             + gates[:, 3 * r + 2:3 * r + 3] * o_win[r])
        o_ref[:, sl] = o * _silu(z_ref[:, sl])


def _nsa_prompt(proj, sel, bias_tiles, o_cmp, nb_batch, seq):
    nq = seq // PQ
    wt = min(WINDOW // PQ + 1, nq)
    gw = NSA_GROUP * NSA_HD
    rows = NSA_GROUP * PQ
    nsel = sel.shape[-1]

    def kv_spec(col):
        return pl.BlockSpec((seq, NSA_HD), lambda b, g, i: (b, col // NSA_HD + g))

    return pl.pallas_call(
        functools.partial(_nsa_prompt_kernel, nq=nq, wt=wt),
        grid=(nb_batch, NSA_KV_HEADS, nq),
        in_specs=[pl.BlockSpec((PQ, gw), lambda b, g, i: (b * nq + i, C_NQ // gw + g)),
                  kv_spec(C_SLC), kv_spec(C_SLC + NSA_KV_W), kv_spec(C_WIN), kv_spec(C_WIN + NSA_KV_W),
                  pl.BlockSpec((PQ, nsel), lambda b, g, i: (b * nq + i, 0)),
                  pl.BlockSpec((NSA_GROUP, N_BIAS_TILES, PQ, PQ), lambda b, g, i: (g, 0, 0, 0)),
                  pl.BlockSpec((PQ, gw), lambda b, g, i: (b * nq + i, g)),
                  pl.BlockSpec((PQ, gw), lambda b, g, i: (b * nq + i, C_NZ // gw + g)),
                  pl.BlockSpec((PQ, LANES), lambda b, g, i: (b * nq + i, C_G0 // LANES + g))],
        out_specs=pl.BlockSpec((PQ, gw), lambda b, g, i: (b * nq + i, g)),
        out_shape=jax.ShapeDtypeStruct((nb_batch * seq, NSA_WIDTH), F32),
        scratch_shapes=[pltpu.VMEM((rows, PK), F32), pltpu.VMEM((rows, PK), BF16),
                        pltpu.VMEM((rows, wt * PQ), F32), pltpu.VMEM((rows, wt * PQ), BF16),
                        pltpu.VMEM((rows, PQ), F32), pltpu.VMEM((rows, PQ), F32), pltpu.VMEM((rows, NSA_HD), F32)],
        compiler_params=_params("parallel", "parallel", "arbitrary"),
        name="nsa_prompt",
    )(proj, proj, proj, proj, proj, sel, bias_tiles, o_cmp, proj, proj)


def _bias_by_threshold(dist, tcols):
    bias = jnp.broadcast_to(tcols[:, 0:1], (tcols.shape[0], dist.shape[-1]))
    for b in range(1, NUM_BUCKETS):
        bias = jnp.where(dist >= _BUCKET_THR[b - 1], tcols[:, b:b + 1], bias)
    return bias


SLOT_CHUNKS = NSA_TOPN + 2


def _slc_sample_kernel(pt_ref, idx_ref, val_ref, *refs, past, lq):
    del pt_ref
    n_in = NSA_KV_HEADS * NSA_TOPN
    page_refs = refs[:n_in]
    q_ref, knew_ref, tcol_ref, o_ref, kc_ref, vc_ref = refs[n_in:]
    b = pl.program_id(0)
    qi = pl.program_id(1)
    t = past + qi
    n_past_blocks = past // NSA_BLOCK
    nk = SLOT_CHUNKS * NSA_BLOCK
    lane = lax.broadcasted_iota(jnp.int32, (1, nk), 1)
    slot = lane // NSA_BLOCK
    this_q = lax.broadcasted_iota(jnp.int32, (lq, 1), 0) == qi
    for g in range(NSA_KV_HEADS):
        base = ((b * lq + qi) * NSA_KV_HEADS + g) * NSA_TOPN
        blk_of_lane = jnp.where(slot == NSA_TOPN, n_past_blocks, n_past_blocks + 1)
        use_of_lane = jnp.where(slot == NSA_TOPN, 1, 0)
        for j in range(NSA_TOPN):
            n_j = idx_ref[base + j]
            use = jnp.where((val_ref[base + j] > 0) & (n_j < n_past_blocks), 1, 0)
            blk_of_lane = jnp.where(slot == j, n_j, blk_of_lane)
            use_of_lane = jnp.where(slot == j, use, use_of_lane)
            page = page_refs[g * NSA_TOPN + j]
            kc_ref[j * NSA_BLOCK:(j + 1) * NSA_BLOCK, :] = page[pl.ds(g, NSA_BLOCK, stride=4), :]
            vc_ref[j * NSA_BLOCK:(j + 1) * NSA_BLOCK, :] = page[pl.ds(NSA_KV_HEADS + g, NSA_BLOCK, stride=4), :]
        new0 = NSA_TOPN * NSA_BLOCK
        kc_ref[new0:, :] = jnp.zeros((nk - new0, NSA_HD), F32)
        vc_ref[new0:, :] = jnp.zeros((nk - new0, NSA_HD), F32)
        kc_ref[new0:new0 + lq, :] = knew_ref[:, g * NSA_HD:(g + 1) * NSA_HD]
        vc_ref[new0:new0 + lq, :] = knew_ref[:, (NSA_KV_HEADS + g) * NSA_HD:(NSA_KV_HEADS + g + 1) * NSA_HD]
        q4 = jnp.concatenate(
            [jnp.sum(jnp.where(this_q, q_ref[:, (g * NSA_GROUP + r) * NSA_HD:(g * NSA_GROUP + r + 1) * NSA_HD], 0.0),
                     axis=0, keepdims=True)
             for r in range(NSA_GROUP)] + [jnp.zeros((SUBLANES - NSA_GROUP, NSA_HD), F32)], axis=0)
        dist = t - (blk_of_lane * NSA_BLOCK + lane % NSA_BLOCK)
        ok = (use_of_lane > 0) & (dist >= 0)
        s = _dot_nt(q4, kc_ref[...]) * SCALE + _bias_by_threshold(dist, tcol_ref[g])
        s = jnp.where(ok, s, NEG)
        p, den = _softmax_rows(s, ok)
        o = _dot(p, vc_ref[...]) / jnp.maximum(den, 1e-30)
        for r in range(NSA_GROUP):
            h = g * NSA_GROUP + r
            o_ref[:, h * NSA_HD:(h + 1) * NSA_HD] = o[r:r + 1, :]


def _slc_sample(proj, cache4, page_table, idx_flat, val_flat, tcols, layer, nb_batch, lq, past):
    rows = NSA_BLOCK * 4
    n_past_blocks = past // NSA_BLOCK

    def page_spec(g, j):
        def index_map(b, i, pt, idx, val):
            n = jnp.minimum(idx[((b * lq + i) * NSA_KV_HEADS + g) * NSA_TOPN + j], n_past_blocks - 1)
            return (layer, pt[b, n // 2], n % 2, 0)
        return pl.BlockSpec((None, None, rows, NSA_HD), index_map)

    grid_spec = pltpu.PrefetchScalarGridSpec(
        num_scalar_prefetch=3,
        grid=(nb_batch, lq),
        in_specs=[page_spec(g, j) for g in range(NSA_KV_HEADS) for j in range(NSA_TOPN)]
        + [pl.BlockSpec((lq, NSA_WIDTH), lambda b, i, *_: (b, C_NQ // NSA_WIDTH)),
           pl.BlockSpec((lq, 2 * NSA_KV_W), lambda b, i, *_: (b, C_SLC // (2 * NSA_KV_W))),
           pl.BlockSpec((NSA_KV_HEADS, SUBLANES, NUM_BUCKETS), lambda b, i, *_: (0, 0, 0))],
        out_specs=pl.BlockSpec((None, 1, NSA_WIDTH), lambda b, i, *_: (b * lq + i, 0, 0)),
        scratch_shapes=[pltpu.VMEM((SLOT_CHUNKS * NSA_BLOCK, NSA_HD), F32),
                        pltpu.VMEM((SLOT_CHUNKS * NSA_BLOCK, NSA_HD), F32)],
    )
    out = pl.pallas_call(
        functools.partial(_slc_sample_kernel, past=past, lq=lq),
        grid_spec=grid_spec,
        out_shape=jax.ShapeDtypeStruct((nb_batch * lq, 1, NSA_WIDTH), F32),
        compiler_params=_params("parallel", "arbitrary"),
        name="slc_sample",
    )(page_table, idx_flat, val_flat, *([cache4] * (NSA_KV_HEADS * NSA_TOPN)), proj, proj, tcols)
    return out.reshape(nb_batch * lq, NSA_WIDTH)


def _win_sample_kernel(q_ref, wpast_ref, wnew_ref, tcol_ref, ocmp_ref, oslc_ref, z_ref, g0_ref, g1_ref,
                       o_ref, kc_ref, vc_ref, *, past, lq, wlen):
    rows = NSA_GROUP * lq
    nk = kc_ref.shape[0]
    lane = lax.broadcasted_iota(jnp.int32, (1, nk), 1)
    kpos = past - wlen + lane
    t = past + lax.broadcasted_iota(jnp.int32, (rows, 1), 0) % lq
    dist = t - kpos
    ok = (dist >= 0) & (dist <= WINDOW) & (kpos >= 0) & (lane < wlen + lq)
    gate_refs = (g0_ref, g1_ref)
    for g in range(NSA_KV_HEADS):
        kc_ref[0:wlen, :] = wpast_ref[pl.ds(g, wlen, stride=4), :]
        vc_ref[0:wlen, :] = wpast_ref[pl.ds(NSA_KV_HEADS + g, wlen, stride=4), :]
        kc_ref[wlen:, :] = jnp.zeros((nk - wlen, NSA_HD), F32)
        vc_ref[wlen:, :] = jnp.zeros((nk - wlen, NSA_HD), F32)
        kc_ref[wlen:wlen + lq, :] = wnew_ref[:, g * NSA_HD:(g + 1) * NSA_HD]
        vc_ref[wlen:wlen + lq, :] = wnew_ref[:, (NSA_KV_HEADS + g) * NSA_HD:(NSA_KV_HEADS + g + 1) * NSA_HD]
        q4 = jnp.concatenate([q_ref[:, (g * NSA_GROUP + r) * NSA_HD:(g * NSA_GROUP + r + 1) * NSA_HD]
                              for r in range(NSA_GROUP)], axis=0)
        s = _dot_nt(q4, kc_ref[...]) * SCALE + _bias_by_threshold(dist, tcol_ref[g])
        s = jnp.where(ok, s, NEG)
        p, den = _softmax_rows(s, ok)
        o_win = _dot(p, vc_ref[...]) / jnp.maximum(den, 1e-30)
        gates = jax.nn.sigmoid(gate_refs[g][...])
        for r in range(NSA_GROUP):
            h = g * NSA_GROUP + r
            sl = slice(h * NSA_HD, (h + 1) * NSA_HD)
            o = (gates[:, 3 * r:3 * r + 1] * ocmp_ref[:, sl] + gates[:, 3 * r + 1:3 * r + 2] * oslc_ref[:, sl]
                 + gates[:, 3 * r + 2:3 * r + 3] * o_win[r * lq:(r + 1) * lq])
            o_ref[:, sl] = o * _silu(z_ref[:, sl])


def _win_sample(proj, win4, tcols_rows, o_cmp, o_slc, layer, nb_batch, lq, past):
    wlen = win4.shape[2] // 4
    nk = -(-(wlen + lq) // LANES) * LANES
    row_spec = pl.BlockSpec((lq, NSA_WIDTH), lambda b: (b, 0))
    return pl.pallas_call(
        functools.partial(_win_sample_kernel, past=past, lq=lq, wlen=wlen),
        grid=(nb_batch,),
        in_specs=[pl.BlockSpec((lq, NSA_WIDTH), lambda b: (b, C_NQ // NSA_WIDTH)),
                  pl.BlockSpec((None, None, wlen * 4, NSA_HD), lambda b: (layer, b, 0, 0)),
                  pl.BlockSpec((lq, 2 * NSA_KV_W), lambda b: (b, C_WIN // (2 * NSA_KV_W))),
                  pl.BlockSpec((NSA_KV_HEADS, NSA_GROUP * lq, NUM_BUCKETS), lambda b: (0, 0, 0)),
                  row_spec, row_spec,
                  pl.BlockSpec((lq, NSA_WIDTH), lambda b: (b, C_NZ // NSA_WIDTH)),
                  pl.BlockSpec((lq, LANES), lambda b: (b, C_G0 // LANES)),
                  pl.BlockSpec((lq, LANES), lambda b: (b, C_G1 // LANES))],
        out_specs=row_spec,
        out_shape=jax.ShapeDtypeStruct((nb_batch * lq, NSA_WIDTH), F32),
        scratch_shapes=[pltpu.VMEM((nk, NSA_HD), F32), pltpu.VMEM((nk, NSA_HD), F32)],
        compiler_params=_params("parallel"),
        name="win_sample",
    )(proj, win4, proj, tcols_rows, o_cmp, o_slc, proj, proj, proj)


def _out_kernel(dn_ref, nsa_ref, x_ref, p_ref, wo_ref, gpost_ref, wpp_ref, gple_ref, wpg_ref, o_ref):
    half = dn_ref.shape[1]
    mixed = (jnp.dot(dn_ref[...].astype(BF16), wo_ref[0:half, :], preferred_element_type=F32)
             + jnp.dot(nsa_ref[...].astype(BF16), wo_ref[half:, :], preferred_element_type=F32))
    y = mixed * lax.rsqrt(jnp.mean(mixed * mixed, axis=-1, keepdims=True) + EPS) * gpost_ref[...]
    x1 = x_ref[...] + y
    e = jnp.dot(p_ref[...].astype(BF16), wpp_ref[...], preferred_element_type=F32)
    e = e * lax.rsqrt(jnp.mean(e * e, axis=-1, keepdims=True) + EPS) * gple_ref[...]
    gate = jax.nn.sigmoid(jnp.dot(x1.astype(BF16), wpg_ref[...], preferred_element_type=F32))
    o_ref[...] = x1 + gate * e


def _out_proj(dn, nsa, x, p_all, wo_all, gpost_all, wpp_all, gple_all, wpg_all, layer, tm):
    m = x.shape[0]
    once = pl.Buffered(1)

    def wspec(k):
        return pl.BlockSpec((None, k, D_MODEL), lambda i: (layer, 0, 0), pipeline_mode=once)

    return pl.pallas_call(
        _out_kernel,
        grid=(m // tm,),
        in_specs=[pl.BlockSpec((tm, dn.shape[1]), lambda i: (i, 0)),
                  pl.BlockSpec((tm, nsa.shape[1]), lambda i: (i, 0)),
                  pl.BlockSpec((tm, D_MODEL), lambda i: (i, 0)),
                  pl.BlockSpec((None, tm, PLE_DIM), lambda i: (layer, i, 0)),
                  wspec(D_MODEL), wspec(1), wspec(PLE_DIM), wspec(1), wspec(D_MODEL)],
        out_specs=pl.BlockSpec((tm, D_MODEL), lambda i: (i, 0)),
        out_shape=jax.ShapeDtypeStruct((m, D_MODEL), F32),
        compiler_params=_params("parallel"),
        name="out_proj",
    )(dn, nsa, x, p_all, wo_all, gpost_all, wpp_all, gple_all, wpg_all)


def _kv_rows_kernel(x_ref, cmp_ref, slc_ref, win_ref):
    tm = x_ref.shape[0]
    width = 2 * NSA_KV_W

    def rows(o_ref, seg):
        for s in range(4):
            o_ref[pl.ds(s, tm, stride=4), :] = x_ref[:, seg * width + s * NSA_HD:seg * width + (s + 1) * NSA_HD]

    rows(cmp_ref, 0)
    rows(slc_ref, 1)

    @pl.when(pl.program_id(1) == pl.num_programs(1) - 1)
    def _():
        rows(win_ref, 2)


def _kv_rows(proj, nb, seq):
    tm = min(WINDOW, seq)
    nr = seq // tm
    width = 3 * 2 * NSA_KV_W
    return pl.pallas_call(
        _kv_rows_kernel,
        grid=(nb, nr),
        in_specs=[pl.BlockSpec((tm, width), lambda b, r: (b * nr + r, C_CMP // width))],
        out_specs=[pl.BlockSpec((tm * 4, NSA_HD), lambda b, r: (b * nr + r, 0)),
                   pl.BlockSpec((tm * 4, NSA_HD), lambda b, r: (b * nr + r, 0)),
                   pl.BlockSpec((None, tm * 4, NSA_HD), lambda b, r: (b, 0, 0))],
        out_shape=[jax.ShapeDtypeStruct((nb * seq * 4, NSA_HD), F32),
                   jax.ShapeDtypeStruct((nb * seq * 4, NSA_HD), F32),
                   jax.ShapeDtypeStruct((nb, tm * 4, NSA_HD), F32)],
        compiler_params=_params("parallel", "arbitrary"),
        name="kv_rows",
    )(proj)


def _relayout_w_in(w_in):
    depth = w_in.shape[0]

    def zeros(n):
        return jnp.zeros((depth, D_MODEL, n), w_in.dtype)

    half = 3 * NSA_GROUP
    parts = [w_in[:, :, 0:_O_BETA], w_in[:, :, _O_NQ:_O_GATE],
             w_in[:, :, _O_BETA:_O_NQ], zeros(LANES - 2 * DN_HEADS),
             w_in[:, :, _O_GATE:_O_GATE + half], zeros(LANES - half),
             w_in[:, :, _O_GATE + half:_O_GATE + 2 * half], zeros(LANES - half),
             zeros(N_PAD - C_G1 - LANES)]
    return jnp.concatenate(parts, axis=-1).astype(BF16)


def _bias_expand_kernel(t_ref, bucket_ref, o_ref):
    bucket = bucket_ref[...]
    for h in range(NSA_HEADS):
        acc = jnp.full(bucket.shape, t_ref[0, h], F32)
        for b in range(1, NUM_BUCKETS):
            acc = jnp.where(bucket >= b, t_ref[b, h], acc)
        o_ref[h] = acc


def _bias_expand(table, bucket):
    rows, cols = bucket.shape
    tr = max(t for t in range(SUBLANES, min(rows, 512) + 1, SUBLANES) if rows % t == 0)
    return pl.pallas_call(
        _bias_expand_kernel,
        grid=(rows // tr,),
        in_specs=[pl.BlockSpec(memory_space=pltpu.SMEM), pl.BlockSpec((tr, cols), lambda i: (i, 0))],
        out_specs=pl.BlockSpec((NSA_HEADS, tr, cols), lambda i: (0, i, 0)),
        out_shape=jax.ShapeDtypeStruct((NSA_HEADS, rows, cols), F32),
        compiler_params=_params("parallel"),
        name="bias_expand",
    )(table, jnp.asarray(bucket))


def _bias_tiles(table):
    i = np.arange(PQ)
    d = np.arange(N_BIAS_TILES)[:, None, None] * PQ + i[None, :, None] - i[None, None, :]
    tiles = _bias_expand(table * LOG2E, _bucket_np(d).reshape(N_BIAS_TILES * PQ, PQ))
    return tiles.reshape(NSA_HEADS, N_BIAS_TILES, PQ, PQ)


def _bias_cmp(table, pos0, lq, nb):
    dist = (pos0 + np.arange(lq))[:, None] - (np.arange(nb) * NSA_BLOCK + NSA_BLOCK - 1)[None, :]
    return _bias_expand(table, _bucket_np(dist))


def kernel(x_prompt, x_sample, state_dn_S, state_dn_conv, cache_win_kv, cache_cmp_kv, cache_slc_kv, page_table,
           p_prompt, p_sample, rel_bias_table, w_in, w_out, g_pre, g_post, dn_conv_w, dn_A_log, dn_dt_bias,
           dn_norm_w, cmp_pos_w, cmp_w1, cmp_w2, w_ple_proj, g_ple, w_ple_gate):
    depth = w_in.shape[0]
    bp, seq, _ = x_prompt.shape
    bs, lq, _ = x_sample.shape
    n_pages = page_table.shape[1]
    past = n_pages * PAGE_SIZE
    n_pool = cache_cmp_kv.shape[1]
    wlen = cache_win_kv.shape[2]
    mp, ms = bp * seq, bs * lq
    assert seq % PK == 0 and seq % DN_CHUNK == 0 and lq == SUBLANES and past % NSA_BLOCK == 0
    kv_row = (2, NSA_KV_HEADS, NSA_HD)

    w_in_r = _relayout_w_in(w_in)
    w_out_b = w_out.astype(BF16)
    wpp_b = w_ple_proj.astype(BF16)
    wpg_b = w_ple_gate.astype(BF16)
    g_pre3 = g_pre[:, None, :]
    g_post3 = g_post[:, None, :]
    g_ple3 = g_ple[:, None, :]
    nw3 = dn_norm_w[:, None, :]
    lane_pad = ((0, 0), (DN_HEADS, LANES - 2 * DN_HEADS))
    dn_par = jnp.stack([jnp.pad(dn_A_log, lane_pad), jnp.pad(dn_dt_bias, lane_pad)], axis=1)
    pw_prompt = jnp.broadcast_to(cmp_pos_w[:, :, None, :, None], (depth, 2, NSA_KV_HEADS, NSA_BLOCK, NSA_HD))
    pw_prompt = jnp.transpose(pw_prompt, (0, 3, 1, 2, 4)).reshape(depth, NSA_BLOCK, 2 * NSA_KV_W)
    pw_page = jnp.broadcast_to(cmp_pos_w[:, None, None, :, :, None],
                               (depth, PAGE_SIZE // NSA_BLOCK, NSA_KV_HEADS, 2, NSA_BLOCK, NSA_HD))
    pw_page = jnp.transpose(pw_page, (0, 1, 4, 3, 2, 5)).reshape(depth, PAGE_SIZE * 4, NSA_HD)
    table = rel_bias_table.astype(F32)
    bias_tiles = _bias_tiles(table)
    nb_p = seq // NSA_BLOCK
    nb_s = past // NSA_BLOCK
    nbs_s = -(-(past + lq) // NSA_BLOCK)
    nbp_s = -(-nbs_s // LANES) * LANES
    bias_cmp_p = _bias_cmp(table, 0, seq, nb_p)
    bias_cmp_s = _bias_cmp(table, past, lq, nb_s)
    tcols = jnp.pad(table.T.reshape(NSA_KV_HEADS, NSA_GROUP, NUM_BUCKETS),
                    ((0, 0), (0, SUBLANES - NSA_GROUP), (0, 0)))
    tcols_rows = jnp.repeat(table.T.reshape(NSA_KV_HEADS, NSA_GROUP, NUM_BUCKETS), lq, axis=1)

    conv0_s = jnp.pad(state_dn_conv, ((0, 0), (0, 0), (SUBLANES - (CONV_W - 1), 0), (0, 0)))
    conv0_p = jnp.zeros((1, bp, SUBLANES, DN_CONV_CH), F32)
    s0_p = jnp.zeros((1, bp, DN_HEADS, DN_DK, DN_DV), F32)
    cmp4 = cache_cmp_kv.reshape(depth, n_pool, PAGE_SIZE * 4, NSA_HD)
    slc4 = cache_slc_kv.reshape(depth, n_pool, PAGE_SIZE * 4, NSA_HD)
    win4 = cache_win_kv.reshape(depth, bs, wlen * 4, NSA_HD)
    p_prompt3 = p_prompt.reshape(depth, mp, PLE_DIM)
    p_sample3 = p_sample.reshape(depth, ms, PLE_DIM)

    xp = x_prompt.reshape(mp, D_MODEL)
    xs = x_sample.reshape(ms, D_MODEL)
    outs_p = [[] for _ in range(2)]
    outs_s = [[] for _ in range(5)]
    kv_rows_p = [[] for _ in range(3)]
    tq_p = min(256, seq)
    for i in range(depth):
        proj = _in_proj(xp, g_pre3, w_in_r, i, min(1024, mp))
        dn_o, conv_n, s_n = _deltanet(proj, bp, seq, DN_CHUNK, conv0_p, s0_p, 0, dn_conv_w, dn_par, nw3, i)
        ckv = _cmp_mlp(_pool_prompt(proj, bp, seq, pw_prompt, i), cmp_w1, cmp_w2, i)
        o_cmp, sel = _cmp_attn(proj, ckv, bias_cmp_p, bp, seq, 0, tq_p, nb_p, nb_p, True)
        nsa_o = _nsa_prompt(proj, sel, bias_tiles, o_cmp, bp, seq)
        xp = _out_proj(dn_o, nsa_o, xp, p_prompt3, w_out_b, g_post3, wpp_b, g_ple3, wpg_b, i, min(256, mp))
        for dst, rows in zip(kv_rows_p, _kv_rows(proj, bp, seq)):
            dst.append(rows)
        outs_p[0].append(s_n)
        outs_p[1].append(conv_n[:, SUBLANES - (CONV_W - 1):])

        proj = _in_proj(xs, g_pre3, w_in_r, i, ms)
        dn_o, conv_n, s_n = _deltanet(proj, bs, lq, lq, conv0_s, state_dn_S, i, dn_conv_w, dn_par, nw3, i)
        pooled = _pool_paged(cmp4, page_table, pw_page, i, min(16, n_pages))
        ckv = _cmp_mlp(pooled, cmp_w1, cmp_w2, i)
        o_cmp, score = _cmp_attn(proj, ckv, bias_cmp_s, bs, lq, past, lq, nbs_s, nbp_s, False)
        score_t = jnp.transpose(score.reshape(ms * NSA_KV_HEADS, nbp_s))
        _, idx_t, val_t = _topk(score_t, nbs_s)
        o_slc = _slc_sample(proj, slc4, page_table, idx_t.T.reshape(-1), val_t.T.reshape(-1), tcols, i,
                            bs, lq, past)
        nsa_o = _win_sample(proj, win4, tcols_rows, o_cmp, o_slc, i, bs, lq, past)
        xs = _out_proj(dn_o, nsa_o, xs, p_sample3, w_out_b, g_post3, wpp_b, g_ple3, wpg_b, i, ms)
        kv_all = proj[:, C_CMP:C_WIN + 2 * NSA_KV_W].reshape(bs, lq, 3, *kv_row)
        win_all = jnp.concatenate([cache_win_kv[i], kv_all[:, :, 2]], axis=1)
        outs_s[0].append(s_n)
        outs_s[1].append(conv_n[:, SUBLANES - (CONV_W - 1):])
        outs_s[2].append(win_all[:, win_all.shape[1] - min(WINDOW, past + lq):])
        outs_s[3].append(kv_all[:, :, 0])
        outs_s[4].append(kv_all[:, :, 1])

    p_cmp, p_slc, p_win = (jnp.stack(o) for o in kv_rows_p)
    return ((xp.reshape(bp, seq, D_MODEL), xs.reshape(bs, lq, D_MODEL), jnp.stack(outs_p[0]), jnp.stack(outs_p[1]),
             p_win.reshape(depth, bp, min(WINDOW, seq), *kv_row), p_cmp.reshape(depth, bp, seq, *kv_row),
             p_slc.reshape(depth, bp, seq, *kv_row)) + tuple(jnp.stack(o) for o in outs_s))
```

```python
import functools
import math

import numpy as np
import jax
import jax.numpy as jnp
from jax import lax
from jax.experimental import pallas as pl
from jax.experimental.pallas import tpu as pltpu

F32 = jnp.float32
BF16 = jnp.bfloat16

D_MODEL = 2048
DN_HEADS = 8
DN_DK = 128
DN_DV = 128
DN_QK_W = DN_HEADS * DN_DK
DN_CONV_CH = 2 * DN_QK_W + DN_HEADS * DN_DV
CONV_W = 4
DN_CHUNK = 64
NSA_HEADS = 8
NSA_KV_HEADS = 2
NSA_GROUP = 4
NSA_HD = 128
NSA_WIDTH = NSA_HEADS * NSA_HD
NSA_KV_W = NSA_KV_HEADS * NSA_HD
NSA_BLOCK = 64
NSA_TOPN = 16
WINDOW = 512
SCALE = NSA_HD ** -0.5
NUM_BUCKETS = 32
REL_MAX_DIST = 1024
PLE_DIM = 256
PAGE_SIZE = 128
EPS = 1e-6
NEG = -1e30
LOG2E = math.log2(math.e)

VMEM_LIMIT_BYTES = 56 * 1024 * 1024
LANES = 128
SUBLANES = 8

C_QKV = 0
C_Z = 3072
C_NQ = 4096
C_NZ = 5120
C_CMP = 6144
C_SLC = 6656
C_WIN = 7168
C_SM = 7680
C_G0 = 7808
C_G1 = 7936
N_PAD = 8192
_O_BETA = DN_CONV_CH + DN_HEADS * DN_DV
_O_NQ = _O_BETA + 2 * DN_HEADS
_O_GATE = _O_NQ + 2 * NSA_WIDTH + 3 * 2 * NSA_KV_W
N_BIAS_TILES = 9


def _bucket_np(dist):
    dist = np.maximum(np.asarray(dist, np.int64), 0)
    exact = NUM_BUCKETS // 2
    scaled = np.log(np.maximum(dist, 1).astype(np.float64) / exact) / math.log(REL_MAX_DIST / exact)
    large = np.minimum(exact + (scaled * (NUM_BUCKETS - exact)).astype(np.int64), NUM_BUCKETS - 1)
    return np.where(dist < exact, dist, large).astype(np.int32)


_BUCKET_THR = tuple(int(np.argmax(_bucket_np(np.arange(2 * REL_MAX_DIST)) >= b)) for b in range(1, NUM_BUCKETS))


def _dot(a, b):
    return jnp.dot(a.astype(BF16), b.astype(BF16), preferred_element_type=F32)


def _dot_nt(a, b):
    return lax.dot_general(a.astype(BF16), b.astype(BF16), (((1,), (1,)), ((), ())), preferred_element_type=F32)


def _dot_tn(a, b):
    return lax.dot_general(a.astype(BF16), b.astype(BF16), (((0,), (0,)), ((), ())), preferred_element_type=F32)


def _dot_hi(a, b):
    return jnp.dot(a, b, preferred_element_type=F32, precision=lax.Precision.HIGHEST)


def _dot_hi_nt(a, b):
    return lax.dot_general(a, b, (((1,), (1,)), ((), ())), preferred_element_type=F32,
                           precision=lax.Precision.HIGHEST)


def _silu(x):
    return x * jax.nn.sigmoid(x)


def _params(*sem):
    return pltpu.CompilerParams(dimension_semantics=sem, vmem_limit_bytes=VMEM_LIMIT_BYTES)


def _in_proj_kernel(x_ref, g_ref, w_ref, o_ref, xn_ref):
    @pl.when(pl.program_id(1) == 0)
    def _():
        x = x_ref[...]
        y = x * lax.rsqrt(jnp.mean(x * x, axis=-1, keepdims=True) + EPS)
        xn_ref[...] = (y * g_ref[...]).astype(BF16)

    o_ref[...] = jnp.dot(xn_ref[...], w_ref[...], preferred_element_type=F32)


def _in_proj(x, g_all, w_all, layer, tm):
    m = x.shape[0]
    tn = 1024
    return pl.pallas_call(
        _in_proj_kernel,
        grid=(m // tm, N_PAD // tn),
        in_specs=[pl.BlockSpec((tm, D_MODEL), lambda i, j: (i, 0)),
                  pl.BlockSpec((None, 1, D_MODEL), lambda i, j: (layer, 0, 0)),
                  pl.BlockSpec((None, D_MODEL, tn), lambda i, j: (layer, 0, j))],
        out_specs=pl.BlockSpec((tm, tn), lambda i, j: (i, j)),
        out_shape=jax.ShapeDtypeStruct((m, N_PAD), F32),
        scratch_shapes=[pltpu.VMEM((tm, D_MODEL), BF16)],
        compiler_params=_params("parallel", "arbitrary"),
        name="in_proj",
    )(x, g_all, w_all)


def _dn_prep_kernel(qkv_ref, sm_ref, conv0_ref, cw_ref, par_ref,
                    u_ref, w_ref, qd_ref, kd_ref, qk_ref, gl_ref, convn_ref, xs_ref, *, chunk, cs):
    tb = chunk * cs
    halo = SUBLANES

    @pl.when(pl.program_id(1) == 0)
    def _():
        xs_ref[0:halo, :] = conv0_ref[...]

    xs_ref[halo:halo + tb, :] = qkv_ref[...]
    base = halo - (CONV_W - 1)
    y = xs_ref[base:base + tb, :] * cw_ref[0:1, :]
    for j in range(1, CONV_W):
        y = y + xs_ref[base + j:base + j + tb, :] * cw_ref[j:j + 1, :]
    y = _silu(y)
    tail = xs_ref[tb:tb + halo, :]
    xs_ref[0:halo, :] = tail
    convn_ref[...] = tail

    sm = sm_ref[...]
    beta_all = jax.nn.sigmoid(sm)
    xg = sm + par_ref[1:2, :]
    softplus = jnp.maximum(xg, 0.0) + jnp.log1p(jnp.exp(-jnp.abs(xg)))
    g_all = -jnp.exp(par_ref[0:1, :]) * softplus
    row = lax.broadcasted_iota(jnp.int32, (tb, tb), 0)
    col = lax.broadcasted_iota(jnp.int32, (tb, tb), 1)
    in_chunk_lower = (row >= col) & (row // chunk == col // chunk)
    gc_all = _dot_hi(jnp.where(in_chunk_lower, 1.0, 0.0), g_all)
    pick = (lax.broadcasted_iota(jnp.int32, (DN_HEADS, LANES), 1)
            == lax.broadcasted_iota(jnp.int32, (DN_HEADS, LANES), 0) + DN_HEADS).astype(F32)
    gc_rows = _dot_hi_nt(pick, gc_all)
    for c in range(cs):
        gl_ref[c] = gc_all[(c + 1) * chunk - 1:(c + 1) * chunk, :]
    r64 = lax.broadcasted_iota(jnp.int32, (chunk, chunk), 0)
    c64 = lax.broadcasted_iota(jnp.int32, (chunk, chunk), 1)
    lower = r64 >= c64
    strict = r64 > c64

    lhs, kns, rhs, decays = [], [], [], []
    for h in range(DN_HEADS):
        hc = slice(h * DN_DV, (h + 1) * DN_DV)
        qh = y[:, h * DN_DK:(h + 1) * DN_DK]
        kh = y[:, DN_QK_W + h * DN_DK:DN_QK_W + (h + 1) * DN_DK]
        vh = y[:, 2 * DN_QK_W + h * DN_DV:2 * DN_QK_W + (h + 1) * DN_DV]
        qn = qh * lax.rsqrt(jnp.sum(qh * qh, axis=-1, keepdims=True) + EPS) * (DN_DK ** -0.5)
        kn = kh * lax.rsqrt(jnp.sum(kh * kh, axis=-1, keepdims=True) + EPS)
        beta = beta_all[:, h:h + 1]
        gc = gc_all[:, DN_HEADS + h:DN_HEADS + h + 1]
        egc = jnp.exp(gc)
        kb = kn * beta
        rhs_h = jnp.concatenate([vh * beta, kb * egc], axis=-1)
        qd_ref[:, hc] = (qn * egc).astype(qd_ref.dtype)
        for c in range(cs):
            rs = slice(c * chunk, (c + 1) * chunk)
            gcc = gc[rs]
            decays.append(jnp.exp(jnp.where(lower, gcc - gc_rows[h:h + 1, rs], NEG)))
            lhs.append(jnp.concatenate([kb[rs], qn[rs]], axis=0))
            kns.append(kn[rs])
            rhs.append(rhs_h[rs])
            kd_ref[rs, hc] = (kn[rs] * jnp.exp(gcc[chunk - 1:chunk, :] - gcc)).astype(kd_ref.dtype)
    decay = jnp.stack(decays)
    both = jnp.einsum('nid,njd->nij', jnp.stack(lhs).astype(BF16), jnp.stack(kns).astype(BF16),
                      preferred_element_type=F32)
    a = jnp.where(strict[None], both[:, :chunk] * decay, 0.0)
    qk = both[:, chunk:] * decay

    def bmm(p, q):
        return jnp.einsum('nij,njk->nik', p, q, preferred_element_type=F32)

    def split(v):
        hi = v.astype(BF16)
        return hi, (v - hi.astype(F32)).astype(BF16)

    def bmm3(p, q):
        (ph, pl_), (qh, ql) = p, q
        return bmm(ph, qh) + (bmm(ph, ql) + bmm(pl_, qh))

    pw_s = split(a)
    inv = jnp.where(r64 == c64, 1.0, 0.0)[None] - a
    for _ in range(int(math.log2(chunk)) - 1):
        pw_s = split(bmm3(pw_s, pw_s))
        inv = inv + bmm3(pw_s, split(inv))
    x = bmm3(split(inv), split(jnp.stack(rhs)))
    for h in range(DN_HEADS):
        hc = slice(h * DN_DV, (h + 1) * DN_DV)
        for c in range(cs):
            rs = slice(c * chunk, (c + 1) * chunk)
            n = h * cs + c
            u_ref[rs, hc] = x[n, :, :DN_DV]
            w_ref[rs, hc] = x[n, :, DN_DV:].astype(w_ref.dtype)
            qk_ref[rs, h * chunk:(h + 1) * chunk] = qk[n]


def _dn_scan_kernel(u_ref, w_ref, qd_ref, kd_ref, qk_ref, gl_ref, z_ref, s0_ref, nw_ref,
                    o_ref, sout_ref, s_ref, *, chunk, cs):
    @pl.when(pl.program_id(1) == 0)
    def _():
        s_ref[...] = s0_ref[...]

    def heads(ref, rs, width):
        return jnp.stack([ref[rs, h * width:(h + 1) * width] for h in range(DN_HEADS)])

    def bdot(spec, p, q):
        return jnp.einsum(spec, p.astype(BF16), q.astype(BF16), preferred_element_type=F32)

    for c in range(cs):
        rs = slice(c * chunk, (c + 1) * chunk)
        decay_last = jnp.exp(gl_ref[c])
        decay_last = jnp.stack([decay_last[:, DN_HEADS + h:DN_HEADS + h + 1] for h in range(DN_HEADS)])
        s = s_ref[...]
        sb = s.astype(BF16)
        v_new = heads(u_ref, rs, DN_DV) - bdot('hcd,hde->hce', heads(w_ref, rs, DN_DK), sb)
        o = (bdot('hcd,hde->hce', heads(qd_ref, rs, DN_DK), sb)
             + bdot('hcs,hse->hce', heads(qk_ref, rs, chunk), v_new))
        s_ref[...] = s * decay_last + bdot('hcd,hce->hde', heads(kd_ref, rs, DN_DK), v_new)
        on = o * lax.rsqrt(jnp.mean(o * o, axis=-1, keepdims=True) + EPS) * nw_ref[...]
        for h in range(DN_HEADS):
            hc = slice(h * DN_DV, (h + 1) * DN_DV)
            o_ref[rs, hc] = on[h] * _silu(z_ref[rs, hc])

    @pl.when(pl.program_id(1) == pl.num_programs(1) - 1)
    def _():
        sout_ref[...] = s_ref[...]


def _deltanet(proj, nb, seq, chunk, conv0_all, s0_all, state_layer, cw_all, par_all, nw_all, layer):
    m = nb * seq
    nchunks = seq // chunk
    width = DN_HEADS * DN_DV
    hs = (DN_HEADS, DN_DK, DN_DV)
    cs_a = 2 if nchunks % 2 == 0 else 1
    na = nchunks // cs_a
    ta = cs_a * chunk
    op_dtype = BF16 if chunk % (2 * SUBLANES) == 0 else F32
    u, w, qd, kd, qk, gl, conv_n = pl.pallas_call(
        functools.partial(_dn_prep_kernel, chunk=chunk, cs=cs_a),
        grid=(nb, na),
        in_specs=[pl.BlockSpec((ta, DN_CONV_CH), lambda b, c: (b * na + c, C_QKV // DN_CONV_CH)),
                  pl.BlockSpec((ta, LANES), lambda b, c: (b * na + c, C_SM // LANES)),
                  pl.BlockSpec((None, None, SUBLANES, DN_CONV_CH), lambda b, c: (state_layer, b, 0, 0)),
                  pl.BlockSpec((None, CONV_W, DN_CONV_CH), lambda b, c: (layer, 0, 0)),
                  pl.BlockSpec((None, 2, LANES), lambda b, c: (layer, 0, 0))],
        out_specs=[pl.BlockSpec((ta, width), lambda b, c: (b * na + c, 0)),
                   pl.BlockSpec((ta, width), lambda b, c: (b * na + c, 0)),
                   pl.BlockSpec((ta, width), lambda b, c: (b * na + c, 0)),
                   pl.BlockSpec((ta, width), lambda b, c: (b * na + c, 0)),
                   pl.BlockSpec((ta, DN_HEADS * chunk), lambda b, c: (b * na + c, 0)),
                   pl.BlockSpec((cs_a, 1, LANES), lambda b, c: (b * na + c, 0, 0)),
                   pl.BlockSpec((None, SUBLANES, DN_CONV_CH), lambda b, c: (b, 0, 0))],
        out_shape=[jax.ShapeDtypeStruct((m, width), F32),
                   jax.ShapeDtypeStruct((m, width), op_dtype),
                   jax.ShapeDtypeStruct((m, width), op_dtype),
                   jax.ShapeDtypeStruct((m, width), op_dtype),
                   jax.ShapeDtypeStruct((m, DN_HEADS * chunk), F32),
                   jax.ShapeDtypeStruct((nb * nchunks, 1, LANES), F32),
                   jax.ShapeDtypeStruct((nb, SUBLANES, DN_CONV_CH), F32)],
        scratch_shapes=[pltpu.VMEM((ta + SUBLANES, DN_CONV_CH), F32)],
        compiler_params=_params("parallel", "arbitrary"),
        name="dn_prep",
    )(proj, proj, conv0_all, cw_all, par_all)

    cs_b = 4 if nchunks % 4 == 0 else 1
    nbk = nchunks // cs_b
    tbk = cs_b * chunk
    row_spec = pl.BlockSpec((tbk, width), lambda b, c: (b * nbk + c, 0))
    o, s_n = pl.pallas_call(
        functools.partial(_dn_scan_kernel, chunk=chunk, cs=cs_b),
        grid=(nb, nbk),
        in_specs=[row_spec, row_spec, row_spec, row_spec,
                  pl.BlockSpec((tbk, DN_HEADS * chunk), lambda b, c: (b * nbk + c, 0)),
                  pl.BlockSpec((cs_b, 1, LANES), lambda b, c: (b * nbk + c, 0, 0)),
                  pl.BlockSpec((tbk, width), lambda b, c: (b * nbk + c, C_Z // width)),
                  pl.BlockSpec((None, None) + hs, lambda b, c: (state_layer, b, 0, 0, 0)),
                  pl.BlockSpec((None, 1, DN_DV), lambda b, c: (layer, 0, 0))],
        out_specs=[row_spec, pl.BlockSpec((None,) + hs, lambda b, c: (b, 0, 0, 0))],
        out_shape=[jax.ShapeDtypeStruct((m, width), F32), jax.ShapeDtypeStruct((nb,) + hs, F32)],
        scratch_shapes=[pltpu.VMEM(hs, F32)],
        compiler_params=_params("parallel", "arbitrary"),
        name="dn_scan",
    )(u, w, qd, kd, qk, gl, proj, s0_all, nw_all)
    return o, conv_n, s_n


def _pool_kernel(x_ref, pw_ref, o_ref):
    rows = x_ref.shape[0]
    x = x_ref[...].reshape(rows // NSA_BLOCK, NSA_BLOCK, 2 * NSA_KV_W)
    o_ref[...] = jnp.sum(x * pw_ref[...][None], axis=1)


def _pool_prompt(proj, nb, seq, pw_all, layer):
    rows = min(seq, 1024)
    nr = seq // rows
    width = 2 * NSA_KV_W
    return pl.pallas_call(
        _pool_kernel,
        grid=(nb, nr),
        in_specs=[pl.BlockSpec((rows, width), lambda b, r: (b * nr + r, C_CMP // width)),
                  pl.BlockSpec((None, NSA_BLOCK, width), lambda b, r: (layer, 0, 0))],
        out_specs=pl.BlockSpec((None, rows // NSA_BLOCK, width), lambda b, r: (b, r, 0)),
        out_shape=jax.ShapeDtypeStruct((nb, seq // NSA_BLOCK, width), F32),
        compiler_params=_params("parallel", "parallel"),
        name="pool_prompt",
    )(proj, pw_all)


def _paged_pool_kernel(pt_ref, *refs, pages):
    del pt_ref
    page_refs = refs[:pages]
    pw_ref = refs[pages]
    o_ref = refs[pages + 1]
    rows = PAGE_SIZE * 4
    per_block = rows // 2 // SUBLANES
    for k in range(pages):
        prod = page_refs[k][...] * pw_ref[...]
        y = jnp.sum(prod.reshape(2, per_block, SUBLANES, NSA_HD), axis=1)
        y = y[:, 0:4, :] + y[:, 4:8, :]
        for s in range(4):
            o_ref[2 * k:2 * k + 2, s * NSA_HD:(s + 1) * NSA_HD] = y[:, s, :]


def _pool_paged(cache4, page_table, pw_all, layer, pages):
    nb, n_pages = page_table.shape
    steps = n_pages // pages
    rows = PAGE_SIZE * 4

    def page_spec(k):
        return pl.BlockSpec((None, None, rows, NSA_HD),
                            lambda b, s, pt: (layer, pt[b, s * pages + k], 0, 0))

    grid_spec = pltpu.PrefetchScalarGridSpec(
        num_scalar_prefetch=1,
        grid=(nb, steps),
        in_specs=[page_spec(k) for k in range(pages)]
        + [pl.BlockSpec((None, rows, NSA_HD), lambda b, s, pt: (layer, 0, 0))],
        out_specs=pl.BlockSpec((None, 2 * pages, 4 * NSA_HD), lambda b, s, pt: (b, s, 0)),
    )
    return pl.pallas_call(
        functools.partial(_paged_pool_kernel, pages=pages),
        grid_spec=grid_spec,
        out_shape=jax.ShapeDtypeStruct((nb, 2 * n_pages, 4 * NSA_HD), F32),
        compiler_params=_params("parallel", "arbitrary"),
        name="pool_paged",
    )(page_table, *([cache4] * pages), pw_all)


def _cmp_mlp_kernel(p_ref, w1_ref, w2_ref, o_ref):
    for s in range(4):
        kv = s // NSA_KV_HEADS
        h = _silu(_dot_hi(p_ref[:, s * NSA_HD:(s + 1) * NSA_HD], w1_ref[kv]))
        o_ref[:, s * NSA_HD:(s + 1) * NSA_HD] = _dot_hi(h, w2_ref[kv])


def _cmp_mlp(pooled, w1_all, w2_all, layer):
    nb, n, width = pooled.shape
    wspec = pl.BlockSpec((None, 2, NSA_HD, NSA_HD), lambda b: (layer, 0, 0, 0))
    return pl.pallas_call(
        _cmp_mlp_kernel,
        grid=(nb,),
        in_specs=[pl.BlockSpec((None, n, width), lambda b: (b, 0, 0)), wspec, wspec],
        out_specs=pl.BlockSpec((None, n, width), lambda b: (b, 0, 0)),
        out_shape=jax.ShapeDtypeStruct((nb, n, width), F32),
        compiler_params=_params("parallel"),
        name="cmp_mlp",
    )(pooled, w1_all, w2_all)


def _select_blocks(score, st_ref, nbp):
    st_ref[...] = score.T
    halves = [st_ref[g * nbp:(g + 1) * nbp, :] for g in range(NSA_KV_HEADS)]
    bi = lax.broadcasted_iota(jnp.int32, halves[0].shape, 0)

    def body(j, ranks):
        tie = jnp.where(j < bi, 1, 0)
        out = []
        for g in range(NSA_KV_HEADS):
            other = st_ref[pl.ds(g * nbp + j, 1), :]
            out.append(ranks[g] + jnp.where(other > halves[g], 1, jnp.where(other == halves[g], tie, 0)))
        return tuple(out)

    ranks = lax.fori_loop(0, nbp, body, tuple(jnp.zeros(bi.shape, jnp.int32) for _ in halves))
    keep = [jnp.where((ranks[g] < NSA_TOPN) & (halves[g] >= 0.0), 1.0, 0.0) for g in range(NSA_KV_HEADS)]
    return jnp.concatenate(keep, axis=0).T


def _cmp_attn_kernel(q_ref, ckv_ref, bias_ref, o_ref, sc_ref, *scratch, pos0, tq, nb, nbs, nbp, select):
    qi = pl.program_id(1)
    t = pos0 + qi * tq + lax.broadcasted_iota(jnp.int32, (tq, 1), 0)
    blk = lax.broadcasted_iota(jnp.int32, (1, nb), 1)
    valid = t >= blk * NSA_BLOCK + (NSA_BLOCK - 1)
    bi = lax.broadcasted_iota(jnp.int32, (1, nbp), 1)
    cur = t // NSA_BLOCK
    forced = (bi == 0) | (bi == cur) | (bi == cur - 1)
    causal = bi <= cur
    scores = []
    for g in range(NSA_KV_HEADS):
        ck = ckv_ref[:, g * NSA_HD:(g + 1) * NSA_HD]
        cv = ckv_ref[:, (NSA_KV_HEADS + g) * NSA_HD:(NSA_KV_HEADS + g + 1) * NSA_HD]
        imp = jnp.zeros((tq, nb), F32)
        for r in range(NSA_GROUP):
            h = g * NSA_GROUP + r
            logits = _dot_nt(q_ref[:, h * NSA_HD:(h + 1) * NSA_HD], ck) * SCALE + bias_ref[h]
            logits = jnp.where(valid, logits, NEG)
            mx = jnp.max(logits, axis=-1, keepdims=True)
            p = jnp.where(valid, jnp.exp(logits - mx), 0.0)
            p = p / jnp.maximum(jnp.sum(p, axis=-1, keepdims=True), 1e-30)
            o_ref[:, h * NSA_HD:(h + 1) * NSA_HD] = _dot(p, cv)
            imp = imp + p
        if nbp > nb:
            imp = jnp.concatenate([imp, jnp.zeros((tq, nbp - nb), F32)], axis=-1)
        score = jnp.where(causal, jnp.where(forced, NSA_GROUP + 1.0, imp), -1.0)
        scores.append(jnp.where(bi < nbs, score, -2.0))
    score = jnp.concatenate(scores, axis=-1)
    sc_ref[...] = _select_blocks(score, scratch[0], nbp) if select else score


def _cmp_attn(proj, ckv, bias_cmp, nb_batch, seq, pos0, tq, nbs, nbp, select):
    nq = seq // tq
    nb = ckv.shape[1]
    m = nb_batch * seq
    bias_tq = bias_cmp.shape[1] // nq
    return pl.pallas_call(
        functools.partial(_cmp_attn_kernel, pos0=pos0, tq=tq, nb=nb, nbs=nbs, nbp=nbp, select=select),
        scratch_shapes=[pltpu.VMEM((NSA_KV_HEADS * nbp, tq), F32)] if select else [],
        grid=(nb_batch, nq),
        in_specs=[pl.BlockSpec((tq, NSA_WIDTH), lambda b, i: (b * nq + i, C_NQ // NSA_WIDTH)),
                  pl.BlockSpec((None, nb, 2 * NSA_KV_W), lambda b, i: (b, 0, 0)),
                  pl.BlockSpec((NSA_HEADS, bias_tq, nb), lambda b, i: (0, i, 0))],
        out_specs=[pl.BlockSpec((tq, NSA_WIDTH), lambda b, i: (b * nq + i, 0)),
                   pl.BlockSpec((tq, NSA_KV_HEADS * nbp), lambda b, i: (b * nq + i, 0))],
        out_shape=[jax.ShapeDtypeStruct((m, NSA_WIDTH), F32),
                   jax.ShapeDtypeStruct((m, NSA_KV_HEADS * nbp), F32)],
        compiler_params=_params("parallel", "parallel"),
        name="cmp_attn",
    )(proj, ckv, bias_cmp)


def _topk_kernel(s_ref, sel_ref, idx_ref, val_ref, *, nbs):
    s = s_ref[...]
    bi = lax.broadcasted_iota(jnp.int32, s.shape, 0)

    def body(j, rank):
        other = s_ref[pl.ds(j, 1), :]
        beats = (other > s) | ((other == s) & (j < bi))
        return rank + beats.astype(jnp.int32)

    rank = lax.fori_loop(0, nbs, body, jnp.zeros(s.shape, jnp.int32))
    ok = s >= 0.0
    sel_ref[...] = jnp.where((rank < NSA_TOPN) & ok, 1.0, 0.0)
    for r in range(NSA_TOPN):
        hit = rank == r
        idx_ref[r:r + 1, :] = jnp.sum(jnp.where(hit, bi, 0), axis=0, keepdims=True)
        val_ref[r:r + 1, :] = jnp.sum(jnp.where(hit & ok, 1, 0), axis=0, keepdims=True)


def _topk(score_t, nbs):
    rows, cols = score_t.shape
    tc = min(cols, 512)
    return pl.pallas_call(
        functools.partial(_topk_kernel, nbs=nbs),
        grid=(cols // tc,),
        in_specs=[pl.BlockSpec((rows, tc), lambda i: (0, i))],
        out_specs=[pl.BlockSpec((rows, tc), lambda i: (0, i)),
                   pl.BlockSpec((NSA_TOPN, tc), lambda i: (0, i)),
                   pl.BlockSpec((NSA_TOPN, tc), lambda i: (0, i))],
        out_shape=[jax.ShapeDtypeStruct((rows, cols), F32),
                   jax.ShapeDtypeStruct((NSA_TOPN, cols), jnp.int32),
                   jax.ShapeDtypeStruct((NSA_TOPN, cols), jnp.int32)],
        compiler_params=_params("parallel"),
        name="topk",
    )(score_t)


PQ = 128
PK = 512


def _softmax_rows(s, ok):
    mx = jnp.max(s, axis=-1, keepdims=True)
    p = jnp.where(ok, jnp.exp(s - mx), 0.0)
    return p, jnp.sum(p, axis=-1, keepdims=True)


def _nsa_prompt_kernel(q_ref, ks_ref, vs_ref, kw_ref, vw_ref, sel_ref, bt_ref, ocmp_ref, z_ref, gate_ref,
                       o_ref, s2_ref, p2_ref, sw_ref, pw_ref, m_ref, l_ref, acc_ref, *, nq, wt):
    qi = pl.program_id(2)
    rows = NSA_GROUP * PQ
    q4 = jnp.concatenate([q_ref[:, r * NSA_HD:(r + 1) * NSA_HD] for r in range(NSA_GROUP)], axis=0)
    q4 = (q4 * (SCALE * LOG2E)).astype(BF16)
    t = qi * PQ + lax.broadcasted_iota(jnp.int32, (PQ, 1), 0)
    nsel = sel_ref.shape[1]
    blk0 = pl.program_id(1) * (nsel // NSA_KV_HEADS)
    sub = PK // PQ
    unselected = jnp.where(sel_ref[...] > 0.5, 0.0, NEG).astype(BF16)
    qx = jnp.concatenate([q4, jnp.concatenate([unselected] * NSA_GROUP, axis=0)], axis=1)

    m_ref[...] = jnp.full((rows, PQ), NEG, F32)
    l_ref[...] = jnp.zeros((rows, PQ), F32)
    acc_ref[...] = jnp.zeros((rows, NSA_HD), F32)

    def key_tile(kt, diagonal, slot):
        s_ref, p_ref = s2_ref.at[slot], p2_ref.at[slot]
        k0 = pl.multiple_of(kt * PK, PK)
        member = (lax.broadcasted_iota(jnp.int32, (PK, nsel), 1)
                  == blk0 + kt * (PK // NSA_BLOCK) + lax.broadcasted_iota(jnp.int32, (PK, nsel), 0) // NSA_BLOCK)
        kx = jnp.concatenate([ks_ref[pl.ds(k0, PK), :].astype(BF16), jnp.where(member, 1.0, 0.0).astype(BF16)],
                             axis=1)
        s_ref[...] = lax.dot_general(qx, kx, (((1,), (1,)), ((), ())), preferred_element_type=F32)
        if diagonal:
            future = jnp.where(k0 + lax.broadcasted_iota(jnp.int32, (1, PK), 1) <= t, 0.0, NEG)
        for r in range(NSA_GROUP):
            rs = slice(r * PQ, (r + 1) * PQ)
            zmax = None
            for c in range(sub):
                cs = slice(c * PQ, (c + 1) * PQ)
                d = jnp.clip(qi - (kt * sub + c), 0, N_BIAS_TILES - 1)
                z = s_ref[rs, cs] + bt_ref[r, d]
                if diagonal:
                    z = z + future[:, cs]
                s_ref[rs, cs] = z
                zmax = z if zmax is None else jnp.maximum(zmax, z)
            m_old = m_ref[rs]
            mx = jnp.maximum(m_old, jnp.broadcast_to(jnp.max(zmax, axis=-1, keepdims=True), (PQ, PQ)))
            alpha = jnp.exp2(m_old - mx)
            psum = None
            for c in range(sub):
                cs = slice(c * PQ, (c + 1) * PQ)
                p = jnp.exp2(s_ref[rs, cs] - mx)
                p_ref[rs, cs] = p.astype(BF16)
                psum = p if psum is None else psum + p
            l_ref[rs] = alpha * l_ref[rs] + psum
            acc_ref[rs] = alpha * acc_ref[rs]
            m_ref[rs] = mx
        acc_ref[...] += jnp.dot(p_ref[...], vs_ref[pl.ds(k0, PK), :].astype(BF16), preferred_element_type=F32)

    last = (qi * PQ) // PK

    def body(j, carry):
        key_tile(2 * j, False, 0)
        key_tile(2 * j + 1, False, 1)
        return carry

    lax.fori_loop(0, last // 2, body, 0)

    @pl.when(last % 2 == 1)
    def _():
        key_tile(last - 1, False, 0)

    key_tile(last, True, 1)
    o_slc = acc_ref[...] / jnp.maximum(jnp.sum(l_ref[...], axis=-1, keepdims=True), 1e-30)

    wk = wt * PQ
    kst = jnp.clip(qi - (wt - 1), 0, nq - wt)
    k0 = pl.multiple_of(kst * PQ, PQ)
    sw_ref[...] = lax.dot_general(q4, kw_ref[pl.ds(k0, wk), :].astype(BF16), (((1,), (1,)), ((), ())),
                                  preferred_element_type=F32)
    dist = t - (k0 + lax.broadcasted_iota(jnp.int32, (1, wk), 1))
    outside = jnp.where((dist >= 0) & (dist <= WINDOW), 0.0, NEG)
    dens = []
    for r in range(NSA_GROUP):
        rs = slice(r * PQ, (r + 1) * PQ)
        zmax = None
        for c in range(wt):
            cs = slice(c * PQ, (c + 1) * PQ)
            d = jnp.clip(qi - (kst + c), 0, N_BIAS_TILES - 1)
            z = sw_ref[rs, cs] + bt_ref[r, d] + outside[:, cs]
            sw_ref[rs, cs] = z
            zmax = z if zmax is None else jnp.maximum(zmax, z)
        mx = jnp.broadcast_to(jnp.max(zmax, axis=-1, keepdims=True), (PQ, PQ))
        psum = None
        for c in range(wt):
            cs = slice(c * PQ, (c + 1) * PQ)
            p = jnp.exp2(sw_ref[rs, cs] - mx)
            pw_ref[rs, cs] = p.astype(BF16)
            psum = p if psum is None else psum + p
        dens.append(jnp.sum(psum, axis=-1, keepdims=True))
    pv = jnp.dot(pw_ref[...], vw_ref[pl.ds(k0, wk), :].astype(BF16), preferred_element_type=F32)
    o_win = [pv[r * PQ:(r + 1) * PQ] / jnp.maximum(dens[r], 1e-30) for r in range(NSA_GROUP)]

    gates = jax.nn.sigmoid(gate_ref[...])
    for r in range(NSA_GROUP):
        sl = slice(r * NSA_HD, (r + 1) * NSA_HD)
        rs = slice(r * PQ, (r + 1) * PQ)
        o = (gates[:, 3 * r:3 * r + 1] * ocmp_ref[:, sl] + gates[:, 3 * r + 1:3 * r + 2] * o_slc[rs]
             + gates[:, 3 * r + 2:3 * r + 3] * o_win[r])
        o_ref[:, sl] = o * _silu(z_ref[:, sl])


def _nsa_prompt(proj, sel, bias_tiles, o_cmp, nb_batch, seq):
    nq = seq // PQ
    wt = min(WINDOW // PQ + 1, nq)
    gw = NSA_GROUP * NSA_HD
    rows = NSA_GROUP * PQ
    nsel = sel.shape[-1]

    def kv_spec(col):
        return pl.BlockSpec((seq, NSA_HD), lambda b, g, i: (b, col // NSA_HD + g))

    return pl.pallas_call(
        functools.partial(_nsa_prompt_kernel, nq=nq, wt=wt),
        grid=(nb_batch, NSA_KV_HEADS, nq),
        in_specs=[pl.BlockSpec((PQ, gw), lambda b, g, i: (b * nq + i, C_NQ // gw + g)),
                  kv_spec(C_SLC), kv_spec(C_SLC + NSA_KV_W), kv_spec(C_WIN), kv_spec(C_WIN + NSA_KV_W),
                  pl.BlockSpec((PQ, nsel), lambda b, g, i: (b * nq + i, 0)),
                  pl.BlockSpec((NSA_GROUP, N_BIAS_TILES, PQ, PQ), lambda b, g, i: (g, 0, 0, 0)),
                  pl.BlockSpec((PQ, gw), lambda b, g, i: (b * nq + i, g)),
                  pl.BlockSpec((PQ, gw), lambda b, g, i: (b * nq + i, C_NZ // gw + g)),
                  pl.BlockSpec((PQ, LANES), lambda b, g, i: (b * nq + i, C_G0 // LANES + g))],
        out_specs=pl.BlockSpec((PQ, gw), lambda b, g, i: (b * nq + i, g)),
        out_shape=jax.ShapeDtypeStruct((nb_batch * seq, NSA_WIDTH), F32),
        scratch_shapes=[pltpu.VMEM((2, rows, PK), F32), pltpu.VMEM((2, rows, PK), BF16),
                        pltpu.VMEM((rows, wt * PQ), F32), pltpu.VMEM((rows, wt * PQ), BF16),
                        pltpu.VMEM((rows, PQ), F32), pltpu.VMEM((rows, PQ), F32), pltpu.VMEM((rows, NSA_HD), F32)],
        compiler_params=_params("parallel", "parallel", "arbitrary"),
        name="nsa_prompt",
    )(proj, proj, proj, proj, proj, sel, bias_tiles, o_cmp, proj, proj)


def _bias_by_threshold(dist, tcols):
    bias = jnp.broadcast_to(tcols[:, 0:1], (tcols.shape[0], dist.shape[-1]))
    for b in range(1, NUM_BUCKETS):
        bias = jnp.where(dist >= _BUCKET_THR[b - 1], tcols[:, b:b + 1], bias)
    return bias


SLOT_CHUNKS = NSA_TOPN + 2


def _slc_sample_kernel(pt_ref, idx_ref, val_ref, *refs, past, lq):
    del pt_ref
    n_in = NSA_KV_HEADS * NSA_TOPN
    page_refs = refs[:n_in]
    q_ref, knew_ref, tcol_ref, o_ref, kc_ref, vc_ref = refs[n_in:]
    b = pl.program_id(0)
    qi = pl.program_id(1)
    t = past + qi
    n_past_blocks = past // NSA_BLOCK
    nk = SLOT_CHUNKS * NSA_BLOCK
    lane = lax.broadcasted_iota(jnp.int32, (1, nk), 1)
    slot = lane // NSA_BLOCK
    this_q = lax.broadcasted_iota(jnp.int32, (lq, 1), 0) == qi
    for g in range(NSA_KV_HEADS):
        base = ((b * lq + qi) * NSA_KV_HEADS + g) * NSA_TOPN
        blk_of_lane = jnp.where(slot == NSA_TOPN, n_past_blocks, n_past_blocks + 1)
        use_of_lane = jnp.where(slot == NSA_TOPN, 1, 0)
        for j in range(NSA_TOPN):
            n_j = idx_ref[base + j]
            use = jnp.where((val_ref[base + j] > 0) & (n_j < n_past_blocks), 1, 0)
            blk_of_lane = jnp.where(slot == j, n_j, blk_of_lane)
            use_of_lane = jnp.where(slot == j, use, use_of_lane)
            page = page_refs[g * NSA_TOPN + j]
            kc_ref[j * NSA_BLOCK:(j + 1) * NSA_BLOCK, :] = page[pl.ds(g, NSA_BLOCK, stride=4), :]
            vc_ref[j * NSA_BLOCK:(j + 1) * NSA_BLOCK, :] = page[pl.ds(NSA_KV_HEADS + g, NSA_BLOCK, stride=4), :]
        new0 = NSA_TOPN * NSA_BLOCK
        kc_ref[new0:, :] = jnp.zeros((nk - new0, NSA_HD), F32)
        vc_ref[new0:, :] = jnp.zeros((nk - new0, NSA_HD), F32)
        kc_ref[new0:new0 + lq, :] = knew_ref[:, g * NSA_HD:(g + 1) * NSA_HD]
        vc_ref[new0:new0 + lq, :] = knew_ref[:, (NSA_KV_HEADS + g) * NSA_HD:(NSA_KV_HEADS + g + 1) * NSA_HD]
        q4 = jnp.concatenate(
            [jnp.sum(jnp.where(this_q, q_ref[:, (g * NSA_GROUP + r) * NSA_HD:(g * NSA_GROUP + r + 1) * NSA_HD], 0.0),
                     axis=0, keepdims=True)
             for r in range(NSA_GROUP)] + [jnp.zeros((SUBLANES - NSA_GROUP, NSA_HD), F32)], axis=0)
        dist = t - (blk_of_lane * NSA_BLOCK + lane % NSA_BLOCK)
        ok = (use_of_lane > 0) & (dist >= 0)
        s = _dot_nt(q4, kc_ref[...]) * SCALE + _bias_by_threshold(dist, tcol_ref[g])
        s = jnp.where(ok, s, NEG)
        p, den = _softmax_rows(s, ok)
        o = _dot(p, vc_ref[...]) / jnp.maximum(den, 1e-30)
        for r in range(NSA_GROUP):
            h = g * NSA_GROUP + r
            o_ref[:, h * NSA_HD:(h + 1) * NSA_HD] = o[r:r + 1, :]


def _slc_sample(proj, cache4, page_table, idx_flat, val_flat, tcols, layer, nb_batch, lq, past):
    rows = NSA_BLOCK * 4
    n_past_blocks = past // NSA_BLOCK

    def page_spec(g, j):
        def index_map(b, i, pt, idx, val):
            n = jnp.minimum(idx[((b * lq + i) * NSA_KV_HEADS + g) * NSA_TOPN + j], n_past_blocks - 1)
            return (layer, pt[b, n // 2], n % 2, 0)
        return pl.BlockSpec((None, None, rows, NSA_HD), index_map)

    grid_spec = pltpu.PrefetchScalarGridSpec(
        num_scalar_prefetch=3,
        grid=(nb_batch, lq),
        in_specs=[page_spec(g, j) for g in range(NSA_KV_HEADS) for j in range(NSA_TOPN)]
        + [pl.BlockSpec((lq, NSA_WIDTH), lambda b, i, *_: (b, C_NQ // NSA_WIDTH)),
           pl.BlockSpec((lq, 2 * NSA_KV_W), lambda b, i, *_: (b, C_SLC // (2 * NSA_KV_W))),
           pl.BlockSpec((NSA_KV_HEADS, SUBLANES, NUM_BUCKETS), lambda b, i, *_: (0, 0, 0))],
        out_specs=pl.BlockSpec((None, 1, NSA_WIDTH), lambda b, i, *_: (b * lq + i, 0, 0)),
        scratch_shapes=[pltpu.VMEM((SLOT_CHUNKS * NSA_BLOCK, NSA_HD), F32),
                        pltpu.VMEM((SLOT_CHUNKS * NSA_BLOCK, NSA_HD), F32)],
    )
    out = pl.pallas_call(
        functools.partial(_slc_sample_kernel, past=past, lq=lq),
        grid_spec=grid_spec,
        out_shape=jax.ShapeDtypeStruct((nb_batch * lq, 1, NSA_WIDTH), F32),
        compiler_params=_params("parallel", "arbitrary"),
        name="slc_sample",
    )(page_table, idx_flat, val_flat, *([cache4] * (NSA_KV_HEADS * NSA_TOPN)), proj, proj, tcols)
    return out.reshape(nb_batch * lq, NSA_WIDTH)


def _win_sample_kernel(q_ref, wpast_ref, wnew_ref, tcol_ref, ocmp_ref, oslc_ref, z_ref, g0_ref, g1_ref,
                       o_ref, kc_ref, vc_ref, *, past, lq, wlen):
    rows = NSA_GROUP * lq
    nk = kc_ref.shape[0]
    lane = lax.broadcasted_iota(jnp.int32, (1, nk), 1)
    kpos = past - wlen + lane
    t = past + lax.broadcasted_iota(jnp.int32, (rows, 1), 0) % lq
    dist = t - kpos
    ok = (dist >= 0) & (dist <= WINDOW) & (kpos >= 0) & (lane < wlen + lq)
    gate_refs = (g0_ref, g1_ref)
    for g in range(NSA_KV_HEADS):
        kc_ref[0:wlen, :] = wpast_ref[pl.ds(g, wlen, stride=4), :]
        vc_ref[0:wlen, :] = wpast_ref[pl.ds(NSA_KV_HEADS + g, wlen, stride=4), :]
        kc_ref[wlen:, :] = jnp.zeros((nk - wlen, NSA_HD), F32)
        vc_ref[wlen:, :] = jnp.zeros((nk - wlen, NSA_HD), F32)
        kc_ref[wlen:wlen + lq, :] = wnew_ref[:, g * NSA_HD:(g + 1) * NSA_HD]
        vc_ref[wlen:wlen + lq, :] = wnew_ref[:, (NSA_KV_HEADS + g) * NSA_HD:(NSA_KV_HEADS + g + 1) * NSA_HD]
        q4 = jnp.concatenate([q_ref[:, (g * NSA_GROUP + r) * NSA_HD:(g * NSA_GROUP + r + 1) * NSA_HD]
                              for r in range(NSA_GROUP)], axis=0)
        s = _dot_nt(q4, kc_ref[...]) * SCALE + _bias_by_threshold(dist, tcol_ref[g])
        s = jnp.where(ok, s, NEG)
        p, den = _softmax_rows(s, ok)
        o_win = _dot(p, vc_ref[...]) / jnp.maximum(den, 1e-30)
        gates = jax.nn.sigmoid(gate_refs[g][...])
        for r in range(NSA_GROUP):
            h = g * NSA_GROUP + r
            sl = slice(h * NSA_HD, (h + 1) * NSA_HD)
            o = (gates[:, 3 * r:3 * r + 1] * ocmp_ref[:, sl] + gates[:, 3 * r + 1:3 * r + 2] * oslc_ref[:, sl]
                 + gates[:, 3 * r + 2:3 * r + 3] * o_win[r * lq:(r + 1) * lq])
            o_ref[:, sl] = o * _silu(z_ref[:, sl])


def _win_sample(proj, win4, tcols_rows, o_cmp, o_slc, layer, nb_batch, lq, past):
    wlen = win4.shape[2] // 4
    nk = -(-(wlen + lq) // LANES) * LANES
    row_spec = pl.BlockSpec((lq, NSA_WIDTH), lambda b: (b, 0))
    return pl.pallas_call(
        functools.partial(_win_sample_kernel, past=past, lq=lq, wlen=wlen),
        grid=(nb_batch,),
        in_specs=[pl.BlockSpec((lq, NSA_WIDTH), lambda b: (b, C_NQ // NSA_WIDTH)),
                  pl.BlockSpec((None, None, wlen * 4, NSA_HD), lambda b: (layer, b, 0, 0)),
                  pl.BlockSpec((lq, 2 * NSA_KV_W), lambda b: (b, C_WIN // (2 * NSA_KV_W))),
                  pl.BlockSpec((NSA_KV_HEADS, NSA_GROUP * lq, NUM_BUCKETS), lambda b: (0, 0, 0)),
                  row_spec, row_spec,
                  pl.BlockSpec((lq, NSA_WIDTH), lambda b: (b, C_NZ // NSA_WIDTH)),
                  pl.BlockSpec((lq, LANES), lambda b: (b, C_G0 // LANES)),
                  pl.BlockSpec((lq, LANES), lambda b: (b, C_G1 // LANES))],
        out_specs=row_spec,
        out_shape=jax.ShapeDtypeStruct((nb_batch * lq, NSA_WIDTH), F32),
        scratch_shapes=[pltpu.VMEM((nk, NSA_HD), F32), pltpu.VMEM((nk, NSA_HD), F32)],
        compiler_params=_params("parallel"),
        name="win_sample",
    )(proj, win4, proj, tcols_rows, o_cmp, o_slc, proj, proj, proj)


def _out_kernel(dn_ref, nsa_ref, x_ref, p_ref, wo_ref, gpost_ref, wpp_ref, gple_ref, wpg_ref, o_ref):
    half = dn_ref.shape[1]
    mixed = (jnp.dot(dn_ref[...].astype(BF16), wo_ref[0:half, :], preferred_element_type=F32)
             + jnp.dot(nsa_ref[...].astype(BF16), wo_ref[half:, :], preferred_element_type=F32))
    y = mixed * lax.rsqrt(jnp.mean(mixed * mixed, axis=-1, keepdims=True) + EPS) * gpost_ref[...]
    x1 = x_ref[...] + y
    e = jnp.dot(p_ref[...].astype(BF16), wpp_ref[...], preferred_element_type=F32)
    e = e * lax.rsqrt(jnp.mean(e * e, axis=-1, keepdims=True) + EPS) * gple_ref[...]
    gate = jax.nn.sigmoid(jnp.dot(x1.astype(BF16), wpg_ref[...], preferred_element_type=F32))
    o_ref[...] = x1 + gate * e


def _out_proj(dn, nsa, x, p_all, wo_all, gpost_all, wpp_all, gple_all, wpg_all, layer, tm):
    m = x.shape[0]
    once = pl.Buffered(1)

    def wspec(k):
        return pl.BlockSpec((None, k, D_MODEL), lambda i: (layer, 0, 0), pipeline_mode=once)

    return pl.pallas_call(
        _out_kernel,
        grid=(m // tm,),
        in_specs=[pl.BlockSpec((tm, dn.shape[1]), lambda i: (i, 0)),
                  pl.BlockSpec((tm, nsa.shape[1]), lambda i: (i, 0)),
                  pl.BlockSpec((tm, D_MODEL), lambda i: (i, 0)),
                  pl.BlockSpec((None, tm, PLE_DIM), lambda i: (layer, i, 0)),
                  wspec(D_MODEL), wspec(1), wspec(PLE_DIM), wspec(1), wspec(D_MODEL)],
        out_specs=pl.BlockSpec((tm, D_MODEL), lambda i: (i, 0)),
        out_shape=jax.ShapeDtypeStruct((m, D_MODEL), F32),
        compiler_params=_params("parallel"),
        name="out_proj",
    )(dn, nsa, x, p_all, wo_all, gpost_all, wpp_all, gple_all, wpg_all)


def _kv_rows_kernel(x_ref, cmp_ref, slc_ref, win_ref):
    tm = x_ref.shape[0]
    width = 2 * NSA_KV_W

    def rows(o_ref, seg):
        for s in range(4):
            o_ref[pl.ds(s, tm, stride=4), :] = x_ref[:, seg * width + s * NSA_HD:seg * width + (s + 1) * NSA_HD]

    rows(cmp_ref, 0)
    rows(slc_ref, 1)

    @pl.when(pl.program_id(1) == pl.num_programs(1) - 1)
    def _():
        rows(win_ref, 2)


def _kv_rows(proj, nb, seq):
    tm = min(WINDOW, seq)
    nr = seq // tm
    width = 3 * 2 * NSA_KV_W
    return pl.pallas_call(
        _kv_rows_kernel,
        grid=(nb, nr),
        in_specs=[pl.BlockSpec((tm, width), lambda b, r: (b * nr + r, C_CMP // width))],
        out_specs=[pl.BlockSpec((tm * 4, NSA_HD), lambda b, r: (b * nr + r, 0)),
                   pl.BlockSpec((tm * 4, NSA_HD), lambda b, r: (b * nr + r, 0)),
                   pl.BlockSpec((None, tm * 4, NSA_HD), lambda b, r: (b, 0, 0))],
        out_shape=[jax.ShapeDtypeStruct((nb * seq * 4, NSA_HD), F32),
                   jax.ShapeDtypeStruct((nb * seq * 4, NSA_HD), F32),
                   jax.ShapeDtypeStruct((nb, tm * 4, NSA_HD), F32)],
        compiler_params=_params("parallel", "arbitrary"),
        name="kv_rows",
    )(proj)


def _relayout_w_in_kernel(w_ref, o_ref):
    rows = w_ref.shape[0]
    half = 3 * NSA_GROUP

    def put(dst, src, width):
        o_ref[:, dst:dst + width] = w_ref[:, src:src + width].astype(BF16)

    def clear(dst, width):
        o_ref[:, dst:dst + width] = jnp.zeros((rows, width), BF16)

    put(C_QKV, 0, _O_BETA)
    put(C_NQ, _O_NQ, _O_GATE - _O_NQ)
    put(C_SM, _O_BETA, 2 * DN_HEADS)
    clear(C_SM + 2 * DN_HEADS, LANES - 2 * DN_HEADS)
    put(C_G0, _O_GATE, half)
    clear(C_G0 + half, LANES - half)
    put(C_G1, _O_GATE + half, half)
    clear(C_G1 + half, N_PAD - C_G1 - half)


def _relayout_w_in(w_in):
    depth, k, n = w_in.shape
    tk = 256
    return pl.pallas_call(
        _relayout_w_in_kernel,
        grid=(depth, k // tk),
        in_specs=[pl.BlockSpec((None, tk, n), lambda d, i: (d, i, 0))],
        out_specs=pl.BlockSpec((None, tk, N_PAD), lambda d, i: (d, i, 0)),
        out_shape=jax.ShapeDtypeStruct((depth, k, N_PAD), BF16),
        compiler_params=_params("parallel", "parallel"),
        name="relayout_w_in",
    )(w_in)


def _bias_expand_kernel(t_ref, bucket_ref, o_ref):
    bucket = bucket_ref[...]
    for h in range(NSA_HEADS):
        acc = jnp.full(bucket.shape, t_ref[0, h], F32)
        for b in range(1, NUM_BUCKETS):
            acc = jnp.where(bucket >= b, t_ref[b, h], acc)
        o_ref[h] = acc


def _bias_expand(table, bucket):
    rows, cols = bucket.shape
    tr = max(t for t in range(SUBLANES, min(rows, 512) + 1, SUBLANES) if rows % t == 0)
    return pl.pallas_call(
        _bias_expand_kernel,
        grid=(rows // tr,),
        in_specs=[pl.BlockSpec(memory_space=pltpu.SMEM), pl.BlockSpec((tr, cols), lambda i: (i, 0))],
        out_specs=pl.BlockSpec((NSA_HEADS, tr, cols), lambda i: (0, i, 0)),
        out_shape=jax.ShapeDtypeStruct((NSA_HEADS, rows, cols), F32),
        compiler_params=_params("parallel"),
        name="bias_expand",
    )(table, jnp.asarray(bucket))


def _bias_tiles(table):
    i = np.arange(PQ)
    d = np.arange(N_BIAS_TILES)[:, None, None] * PQ + i[None, :, None] - i[None, None, :]
    tiles = _bias_expand(table * LOG2E, _bucket_np(d).reshape(N_BIAS_TILES * PQ, PQ))
    return tiles.reshape(NSA_HEADS, N_BIAS_TILES, PQ, PQ)


def _bias_cmp(table, pos0, lq, nb):
    dist = (pos0 + np.arange(lq))[:, None] - (np.arange(nb) * NSA_BLOCK + NSA_BLOCK - 1)[None, :]
    return _bias_expand(table, _bucket_np(dist))


def kernel(x_prompt, x_sample, state_dn_S, state_dn_conv, cache_win_kv, cache_cmp_kv, cache_slc_kv, page_table,
           p_prompt, p_sample, rel_bias_table, w_in, w_out, g_pre, g_post, dn_conv_w, dn_A_log, dn_dt_bias,
           dn_norm_w, cmp_pos_w, cmp_w1, cmp_w2, w_ple_proj, g_ple, w_ple_gate):
    depth = w_in.shape[0]
    bp, seq, _ = x_prompt.shape
    bs, lq, _ = x_sample.shape
    n_pages = page_table.shape[1]
    past = n_pages * PAGE_SIZE
    n_pool = cache_cmp_kv.shape[1]
    wlen = cache_win_kv.shape[2]
    mp, ms = bp * seq, bs * lq
    assert seq % PK == 0 and seq % DN_CHUNK == 0 and lq == SUBLANES and past % NSA_BLOCK == 0
    kv_row = (2, NSA_KV_HEADS, NSA_HD)

    w_in_r = _relayout_w_in(w_in)
    w_out_b = w_out.astype(BF16)
    wpp_b = w_ple_proj.astype(BF16)
    wpg_b = w_ple_gate.astype(BF16)
    g_pre3 = g_pre[:, None, :]
    g_post3 = g_post[:, None, :]
    g_ple3 = g_ple[:, None, :]
    nw3 = dn_norm_w[:, None, :]
    lane_pad = ((0, 0), (DN_HEADS, LANES - 2 * DN_HEADS))
    dn_par = jnp.stack([jnp.pad(dn_A_log, lane_pad), jnp.pad(dn_dt_bias, lane_pad)], axis=1)
    pw_prompt = jnp.broadcast_to(cmp_pos_w[:, :, None, :, None], (depth, 2, NSA_KV_HEADS, NSA_BLOCK, NSA_HD))
    pw_prompt = jnp.transpose(pw_prompt, (0, 3, 1, 2, 4)).reshape(depth, NSA_BLOCK, 2 * NSA_KV_W)
    pw_page = jnp.broadcast_to(cmp_pos_w[:, None, None, :, :, None],
                               (depth, PAGE_SIZE // NSA_BLOCK, NSA_KV_HEADS, 2, NSA_BLOCK, NSA_HD))
    pw_page = jnp.transpose(pw_page, (0, 1, 4, 3, 2, 5)).reshape(depth, PAGE_SIZE * 4, NSA_HD)
    table = rel_bias_table.astype(F32)
    bias_tiles = _bias_tiles(table)
    nb_p = seq // NSA_BLOCK
    nb_s = past // NSA_BLOCK
    nbs_s = -(-(past + lq) // NSA_BLOCK)
    nbp_s = -(-nbs_s // LANES) * LANES
    bias_cmp_p = _bias_cmp(table, 0, seq, nb_p)
    bias_cmp_s = _bias_cmp(table, past, lq, nb_s)
    tcols = jnp.pad(table.T.reshape(NSA_KV_HEADS, NSA_GROUP, NUM_BUCKETS),
                    ((0, 0), (0, SUBLANES - NSA_GROUP), (0, 0)))
    tcols_rows = jnp.repeat(table.T.reshape(NSA_KV_HEADS, NSA_GROUP, NUM_BUCKETS), lq, axis=1)

    conv0_s = jnp.pad(state_dn_conv, ((0, 0), (0, 0), (SUBLANES - (CONV_W - 1), 0), (0, 0)))
    conv0_p = jnp.zeros((1, bp, SUBLANES, DN_CONV_CH), F32)
    s0_p = jnp.zeros((1, bp, DN_HEADS, DN_DK, DN_DV), F32)
    cmp4 = cache_cmp_kv.reshape(depth, n_pool, PAGE_SIZE * 4, NSA_HD)
    slc4 = cache_slc_kv.reshape(depth, n_pool, PAGE_SIZE * 4, NSA_HD)
    win4 = cache_win_kv.reshape(depth, bs, wlen * 4, NSA_HD)
    p_prompt3 = p_prompt.reshape(depth, mp, PLE_DIM)
    p_sample3 = p_sample.reshape(depth, ms, PLE_DIM)

    xp = x_prompt.reshape(mp, D_MODEL)
    xs = x_sample.reshape(ms, D_MODEL)
    outs_p = [[] for _ in range(2)]
    outs_s = [[] for _ in range(5)]
    kv_rows_p = [[] for _ in range(3)]
    tq_p = min(256, seq)
    for i in range(depth):
        proj = _in_proj(xp, g_pre3, w_in_r, i, min(1024, mp))
        dn_o, conv_n, s_n = _deltanet(proj, bp, seq, DN_CHUNK, conv0_p, s0_p, 0, dn_conv_w, dn_par, nw3, i)
        ckv = _cmp_mlp(_pool_prompt(proj, bp, seq, pw_prompt, i), cmp_w1, cmp_w2, i)
        o_cmp, sel = _cmp_attn(proj, ckv, bias_cmp_p, bp, seq, 0, tq_p, nb_p, nb_p, True)
        nsa_o = _nsa_prompt(proj, sel, bias_tiles, o_cmp, bp, seq)
        xp = _out_proj(dn_o, nsa_o, xp, p_prompt3, w_out_b, g_post3, wpp_b, g_ple3, wpg_b, i, min(256, mp))
        for dst, rows in zip(kv_rows_p, _kv_rows(proj, bp, seq)):
            dst.append(rows)
        outs_p[0].append(s_n)
        outs_p[1].append(conv_n[:, SUBLANES - (CONV_W - 1):])

        proj = _in_proj(xs, g_pre3, w_in_r, i, ms)
        dn_o, conv_n, s_n = _deltanet(proj, bs, lq, lq, conv0_s, state_dn_S, i, dn_conv_w, dn_par, nw3, i)
        pooled = _pool_paged(cmp4, page_table, pw_page, i, min(16, n_pages))
        ckv = _cmp_mlp(pooled, cmp_w1, cmp_w2, i)
        o_cmp, score = _cmp_attn(proj, ckv, bias_cmp_s, bs, lq, past, lq, nbs_s, nbp_s, False)
        score_t = jnp.transpose(score.reshape(ms * NSA_KV_HEADS, nbp_s))
        _, idx_t, val_t = _topk(score_t, nbs_s)
        o_slc = _slc_sample(proj, slc4, page_table, idx_t.T.reshape(-1), val_t.T.reshape(-1), tcols, i,
                            bs, lq, past)
        nsa_o = _win_sample(proj, win4, tcols_rows, o_cmp, o_slc, i, bs, lq, past)
        xs = _out_proj(dn_o, nsa_o, xs, p_sample3, w_out_b, g_post3, wpp_b, g_ple3, wpg_b, i, ms)
        kv_all = proj[:, C_CMP:C_WIN + 2 * NSA_KV_W].reshape(bs, lq, 3, *kv_row)
        win_all = jnp.concatenate([cache_win_kv[i], kv_all[:, :, 2]], axis=1)
        outs_s[0].append(s_n)
        outs_s[1].append(conv_n[:, SUBLANES - (CONV_W - 1):])
        outs_s[2].append(win_all[:, win_all.shape[1] - min(WINDOW, past + lq):])
        outs_s[3].append(kv_all[:, :, 0])
        outs_s[4].append(kv_all[:, :, 1])

    p_cmp, p_slc, p_win = (jnp.stack(o) for o in kv_rows_p)
    return ((xp.reshape(bp, seq, D_MODEL), xs.reshape(bs, lq, D_MODEL), jnp.stack(outs_p[0]), jnp.stack(outs_p[1]),
             p_win.reshape(depth, bp, min(WINDOW, seq), *kv_row), p_cmp.reshape(depth, bp, seq, *kv_row),
             p_slc.reshape(depth, bp, seq, *kv_row)) + tuple(jnp.stack(o) for o in outs_s))
```

```python
import functools
import math

import numpy as np
import jax
import jax.numpy as jnp
from jax import lax
from jax.experimental import pallas as pl
from jax.experimental.pallas import tpu as pltpu

F32 = jnp.float32
BF16 = jnp.bfloat16

D_MODEL = 2048
DN_HEADS = 8
DN_DK = 128
DN_DV = 128
DN_QK_W = DN_HEADS * DN_DK
DN_CONV_CH = 2 * DN_QK_W + DN_HEADS * DN_DV
CONV_W = 4
DN_CHUNK = 64
NSA_HEADS = 8
NSA_KV_HEADS = 2
NSA_GROUP = 4
NSA_HD = 128
NSA_WIDTH = NSA_HEADS * NSA_HD
NSA_KV_W = NSA_KV_HEADS * NSA_HD
NSA_BLOCK = 64
NSA_TOPN = 16
WINDOW = 512
SCALE = NSA_HD ** -0.5
NUM_BUCKETS = 32
REL_MAX_DIST = 1024
PLE_DIM = 256
PAGE_SIZE = 128
EPS = 1e-6
NEG = -1e30
LOG2E = math.log2(math.e)

VMEM_LIMIT_BYTES = 56 * 1024 * 1024
LANES = 128
SUBLANES = 8

C_QKV = 0
C_Z = 3072
C_NQ = 4096
C_NZ = 5120
C_CMP = 6144
C_SLC = 6656
C_WIN = 7168
C_SM = 7680
C_G0 = 7808
C_G1 = 7936
N_PAD = 8192
_O_BETA = DN_CONV_CH + DN_HEADS * DN_DV
_O_NQ = _O_BETA + 2 * DN_HEADS
_O_GATE = _O_NQ + 2 * NSA_WIDTH + 3 * 2 * NSA_KV_W
N_BIAS_TILES = 9


def _bucket_np(dist):
    dist = np.maximum(np.asarray(dist, np.int64), 0)
    exact = NUM_BUCKETS // 2
    scaled = np.log(np.maximum(dist, 1).astype(np.float64) / exact) / math.log(REL_MAX_DIST / exact)
    large = np.minimum(exact + (scaled * (NUM_BUCKETS - exact)).astype(np.int64), NUM_BUCKETS - 1)
    return np.where(dist < exact, dist, large).astype(np.int32)


_BUCKET_THR = tuple(int(np.argmax(_bucket_np(np.arange(2 * REL_MAX_DIST)) >= b)) for b in range(1, NUM_BUCKETS))


def _dot(a, b):
    return jnp.dot(a.astype(BF16), b.astype(BF16), preferred_element_type=F32)


def _dot_nt(a, b):
    return lax.dot_general(a.astype(BF16), b.astype(BF16), (((1,), (1,)), ((), ())), preferred_element_type=F32)


def _dot_tn(a, b):
    return lax.dot_general(a.astype(BF16), b.astype(BF16), (((0,), (0,)), ((), ())), preferred_element_type=F32)


def _dot_hi(a, b):
    return jnp.dot(a, b, preferred_element_type=F32, precision=lax.Precision.HIGHEST)


def _dot_hi_nt(a, b):
    return lax.dot_general(a, b, (((1,), (1,)), ((), ())), preferred_element_type=F32,
                           precision=lax.Precision.HIGHEST)


def _silu(x):
    return x * jax.nn.sigmoid(x)


def _params(*sem):
    return pltpu.CompilerParams(dimension_semantics=sem, vmem_limit_bytes=VMEM_LIMIT_BYTES)


def _in_proj_kernel(x_ref, g_ref, w_ref, o_ref, xn_ref):
    @pl.when(pl.program_id(1) == 0)
    def _():
        x = x_ref[...]
        y = x * lax.rsqrt(jnp.mean(x * x, axis=-1, keepdims=True) + EPS)
        xn_ref[...] = (y * g_ref[...]).astype(BF16)

    o_ref[...] = jnp.dot(xn_ref[...], w_ref[...], preferred_element_type=F32)


def _in_proj(x, g_all, w_all, layer, tm):
    m = x.shape[0]
    tn = 1024
    return pl.pallas_call(
        _in_proj_kernel,
        grid=(m // tm, N_PAD // tn),
        in_specs=[pl.BlockSpec((tm, D_MODEL), lambda i, j: (i, 0)),
                  pl.BlockSpec((None, 1, D_MODEL), lambda i, j: (layer, 0, 0)),
                  pl.BlockSpec((None, D_MODEL, tn), lambda i, j: (layer, 0, j))],
        out_specs=pl.BlockSpec((tm, tn), lambda i, j: (i, j)),
        out_shape=jax.ShapeDtypeStruct((m, N_PAD), F32),
        scratch_shapes=[pltpu.VMEM((tm, D_MODEL), BF16)],
        compiler_params=_params("parallel", "arbitrary"),
        name="in_proj",
    )(x, g_all, w_all)


def _dn_prep_kernel(qkv_ref, sm_ref, conv0_ref, cw_ref, par_ref,
                    u_ref, w_ref, qd_ref, kd_ref, qk_ref, gl_ref, convn_ref, xs_ref, *, chunk, cs):
    tb = chunk * cs
    halo = SUBLANES

    @pl.when(pl.program_id(1) == 0)
    def _():
        xs_ref[0:halo, :] = conv0_ref[...]

    xs_ref[halo:halo + tb, :] = qkv_ref[...]
    base = halo - (CONV_W - 1)
    y = xs_ref[base:base + tb, :] * cw_ref[0:1, :]
    for j in range(1, CONV_W):
        y = y + xs_ref[base + j:base + j + tb, :] * cw_ref[j:j + 1, :]
    y = _silu(y)
    tail = xs_ref[tb:tb + halo, :]
    xs_ref[0:halo, :] = tail
    convn_ref[...] = tail

    sm = sm_ref[...]
    beta_all = jax.nn.sigmoid(sm)
    xg = sm + par_ref[1:2, :]
    softplus = jnp.maximum(xg, 0.0) + jnp.log1p(jnp.exp(-jnp.abs(xg)))
    g_all = -jnp.exp(par_ref[0:1, :]) * softplus
    row = lax.broadcasted_iota(jnp.int32, (tb, tb), 0)
    col = lax.broadcasted_iota(jnp.int32, (tb, tb), 1)
    in_chunk_lower = (row >= col) & (row // chunk == col // chunk)
    gc_all = _dot_hi(jnp.where(in_chunk_lower, 1.0, 0.0), g_all)
    pick = (lax.broadcasted_iota(jnp.int32, (DN_HEADS, LANES), 1)
            == lax.broadcasted_iota(jnp.int32, (DN_HEADS, LANES), 0) + DN_HEADS).astype(F32)
    gc_rows = _dot_hi_nt(pick, gc_all)
    for c in range(cs):
        gl_ref[c] = gc_all[(c + 1) * chunk - 1:(c + 1) * chunk, :]
    r64 = lax.broadcasted_iota(jnp.int32, (chunk, chunk), 0)
    c64 = lax.broadcasted_iota(jnp.int32, (chunk, chunk), 1)
    lower = r64 >= c64
    strict = r64 > c64

    lhs, kns, rhs, decays = [], [], [], []
    for h in range(DN_HEADS):
        hc = slice(h * DN_DV, (h + 1) * DN_DV)
        qh = y[:, h * DN_DK:(h + 1) * DN_DK]
        kh = y[:, DN_QK_W + h * DN_DK:DN_QK_W + (h + 1) * DN_DK]
        vh = y[:, 2 * DN_QK_W + h * DN_DV:2 * DN_QK_W + (h + 1) * DN_DV]
        qn = qh * lax.rsqrt(jnp.sum(qh * qh, axis=-1, keepdims=True) + EPS) * (DN_DK ** -0.5)
        kn = kh * lax.rsqrt(jnp.sum(kh * kh, axis=-1, keepdims=True) + EPS)
        beta = beta_all[:, h:h + 1]
        gc = gc_all[:, DN_HEADS + h:DN_HEADS + h + 1]
        egc = jnp.exp(gc)
        kb = kn * beta
        rhs_h = jnp.concatenate([vh * beta, kb * egc], axis=-1)
        qd_ref[:, hc] = (qn * egc).astype(qd_ref.dtype)
        for c in range(cs):
            rs = slice(c * chunk, (c + 1) * chunk)
            gcc = gc[rs]
            decays.append(jnp.exp(jnp.where(lower, gcc - gc_rows[h:h + 1, rs], NEG)))
            lhs.append(jnp.concatenate([kb[rs], qn[rs]], axis=0))
            kns.append(kn[rs])
            rhs.append(rhs_h[rs])
            kd_ref[rs, hc] = (kn[rs] * jnp.exp(gcc[chunk - 1:chunk, :] - gcc)).astype(kd_ref.dtype)
    decay = jnp.stack(decays)
    both = jnp.einsum('nid,njd->nij', jnp.stack(lhs).astype(BF16), jnp.stack(kns).astype(BF16),
                      preferred_element_type=F32)
    a = jnp.where(strict[None], both[:, :chunk] * decay, 0.0)
    qk = both[:, chunk:] * decay

    def bmm(p, q):
        return jnp.einsum('nij,njk->nik', p, q, preferred_element_type=F32)

    def split(v):
        hi = v.astype(BF16)
        return hi, (v - hi.astype(F32)).astype(BF16)

    def bmm3(p, q):
        (ph, pl_), (qh, ql) = p, q
        return bmm(ph, qh) + (bmm(ph, ql) + bmm(pl_, qh))

    pw_s = split(a)
    inv = jnp.where(r64 == c64, 1.0, 0.0)[None] - a
    for _ in range(int(math.log2(chunk)) - 1):
        pw_s = split(bmm3(pw_s, pw_s))
        inv = inv + bmm3(pw_s, split(inv))
    x = bmm3(split(inv), split(jnp.stack(rhs)))
    for h in range(DN_HEADS):
        hc = slice(h * DN_DV, (h + 1) * DN_DV)
        for c in range(cs):
            rs = slice(c * chunk, (c + 1) * chunk)
            n = h * cs + c
            u_ref[rs, hc] = x[n, :, :DN_DV]
            w_ref[rs, hc] = x[n, :, DN_DV:].astype(w_ref.dtype)
            qk_ref[rs, h * chunk:(h + 1) * chunk] = qk[n]


def _dn_scan_kernel(u_ref, w_ref, qd_ref, kd_ref, qk_ref, gl_ref, z_ref, s0_ref, nw_ref,
                    o_ref, sout_ref, s_ref, *, chunk, cs):
    @pl.when(pl.program_id(1) == 0)
    def _():
        s_ref[...] = s0_ref[...]

    def heads(ref, rs, width):
        return jnp.stack([ref[rs, h * width:(h + 1) * width] for h in range(DN_HEADS)])

    def bdot(spec, p, q):
        return jnp.einsum(spec, p.astype(BF16), q.astype(BF16), preferred_element_type=F32)

    for c in range(cs):
        rs = slice(c * chunk, (c + 1) * chunk)
        decay_last = jnp.exp(gl_ref[c])
        decay_last = jnp.stack([decay_last[:, DN_HEADS + h:DN_HEADS + h + 1] for h in range(DN_HEADS)])
        s = s_ref[...]
        sb = s.astype(BF16)
        v_new = heads(u_ref, rs, DN_DV) - bdot('hcd,hde->hce', heads(w_ref, rs, DN_DK), sb)
        o = (bdot('hcd,hde->hce', heads(qd_ref, rs, DN_DK), sb)
             + bdot('hcs,hse->hce', heads(qk_ref, rs, chunk), v_new))
        s_ref[...] = s * decay_last + bdot('hcd,hce->hde', heads(kd_ref, rs, DN_DK), v_new)
        on = o * lax.rsqrt(jnp.mean(o * o, axis=-1, keepdims=True) + EPS) * nw_ref[...]
        for h in range(DN_HEADS):
            hc = slice(h * DN_DV, (h + 1) * DN_DV)
            o_ref[rs, hc] = on[h] * _silu(z_ref[rs, hc])

    @pl.when(pl.program_id(1) == pl.num_programs(1) - 1)
    def _():
        sout_ref[...] = s_ref[...]


def _deltanet(proj, nb, seq, chunk, conv0_all, s0_all, state_layer, cw_all, par_all, nw_all, layer):
    m = nb * seq
    nchunks = seq // chunk
    width = DN_HEADS * DN_DV
    hs = (DN_HEADS, DN_DK, DN_DV)
    cs_a = 2 if nchunks % 2 == 0 else 1
    na = nchunks // cs_a
    ta = cs_a * chunk
    op_dtype = BF16 if chunk % (2 * SUBLANES) == 0 else F32
    u, w, qd, kd, qk, gl, conv_n = pl.pallas_call(
        functools.partial(_dn_prep_kernel, chunk=chunk, cs=cs_a),
        grid=(nb, na),
        in_specs=[pl.BlockSpec((ta, DN_CONV_CH), lambda b, c: (b * na + c, C_QKV // DN_CONV_CH)),
                  pl.BlockSpec((ta, LANES), lambda b, c: (b * na + c, C_SM // LANES)),
                  pl.BlockSpec((None, None, SUBLANES, DN_CONV_CH), lambda b, c: (state_layer, b, 0, 0)),
                  pl.BlockSpec((None, CONV_W, DN_CONV_CH), lambda b, c: (layer, 0, 0)),
                  pl.BlockSpec((None, 2, LANES), lambda b, c: (layer, 0, 0))],
        out_specs=[pl.BlockSpec((ta, width), lambda b, c: (b * na + c, 0)),
                   pl.BlockSpec((ta, width), lambda b, c: (b * na + c, 0)),
                   pl.BlockSpec((ta, width), lambda b, c: (b * na + c, 0)),
                   pl.BlockSpec((ta, width), lambda b, c: (b * na + c, 0)),
                   pl.BlockSpec((ta, DN_HEADS * chunk), lambda b, c: (b * na + c, 0)),
                   pl.BlockSpec((cs_a, 1, LANES), lambda b, c: (b * na + c, 0, 0)),
                   pl.BlockSpec((None, SUBLANES, DN_CONV_CH), lambda b, c: (b, 0, 0))],
        out_shape=[jax.ShapeDtypeStruct((m, width), F32),
                   jax.ShapeDtypeStruct((m, width), op_dtype),
                   jax.ShapeDtypeStruct((m, width), op_dtype),
                   jax.ShapeDtypeStruct((m, width), op_dtype),
                   jax.ShapeDtypeStruct((m, DN_HEADS * chunk), F32),
                   jax.ShapeDtypeStruct((nb * nchunks, 1, LANES), F32),
                   jax.ShapeDtypeStruct((nb, SUBLANES, DN_CONV_CH), F32)],
        scratch_shapes=[pltpu.VMEM((ta + SUBLANES, DN_CONV_CH), F32)],
        compiler_params=_params("parallel", "arbitrary"),
        name="dn_prep",
    )(proj, proj, conv0_all, cw_all, par_all)

    cs_b = 4 if nchunks % 4 == 0 else 1
    nbk = nchunks // cs_b
    tbk = cs_b * chunk
    row_spec = pl.BlockSpec((tbk, width), lambda b, c: (b * nbk + c, 0))
    o, s_n = pl.pallas_call(
        functools.partial(_dn_scan_kernel, chunk=chunk, cs=cs_b),
        grid=(nb, nbk),
        in_specs=[row_spec, row_spec, row_spec, row_spec,
                  pl.BlockSpec((tbk, DN_HEADS * chunk), lambda b, c: (b * nbk + c, 0)),
                  pl.BlockSpec((cs_b, 1, LANES), lambda b, c: (b * nbk + c, 0, 0)),
                  pl.BlockSpec((tbk, width), lambda b, c: (b * nbk + c, C_Z // width)),
                  pl.BlockSpec((None, None) + hs, lambda b, c: (state_layer, b, 0, 0, 0)),
                  pl.BlockSpec((None, 1, DN_DV), lambda b, c: (layer, 0, 0))],
        out_specs=[row_spec, pl.BlockSpec((None,) + hs, lambda b, c: (b, 0, 0, 0))],
        out_shape=[jax.ShapeDtypeStruct((m, width), F32), jax.ShapeDtypeStruct((nb,) + hs, F32)],
        scratch_shapes=[pltpu.VMEM(hs, F32)],
        compiler_params=_params("parallel", "arbitrary"),
        name="dn_scan",
    )(u, w, qd, kd, qk, gl, proj, s0_all, nw_all)
    return o, conv_n, s_n


def _pool_kernel(x_ref, pw_ref, o_ref):
    rows = x_ref.shape[0]
    x = x_ref[...].reshape(rows // NSA_BLOCK, NSA_BLOCK, 2 * NSA_KV_W)
    o_ref[...] = jnp.sum(x * pw_ref[...][None], axis=1)


def _pool_prompt(proj, nb, seq, pw_all, layer):
    rows = min(seq, 1024)
    nr = seq // rows
    width = 2 * NSA_KV_W
    return pl.pallas_call(
        _pool_kernel,
        grid=(nb, nr),
        in_specs=[pl.BlockSpec((rows, width), lambda b, r: (b * nr + r, C_CMP // width)),
                  pl.BlockSpec((None, NSA_BLOCK, width), lambda b, r: (layer, 0, 0))],
        out_specs=pl.BlockSpec((None, rows // NSA_BLOCK, width), lambda b, r: (b, r, 0)),
        out_shape=jax.ShapeDtypeStruct((nb, seq // NSA_BLOCK, width), F32),
        compiler_params=_params("parallel", "parallel"),
        name="pool_prompt",
    )(proj, pw_all)


def _paged_pool_kernel(pt_ref, *refs, pages):
    del pt_ref
    page_refs = refs[:pages]
    pw_ref = refs[pages]
    o_ref = refs[pages + 1]
    rows = PAGE_SIZE * 4
    per_block = rows // 2 // SUBLANES
    for k in range(pages):
        prod = page_refs[k][...] * pw_ref[...]
        y = jnp.sum(prod.reshape(2, per_block, SUBLANES, NSA_HD), axis=1)
        y = y[:, 0:4, :] + y[:, 4:8, :]
        for s in range(4):
            o_ref[2 * k:2 * k + 2, s * NSA_HD:(s + 1) * NSA_HD] = y[:, s, :]


def _pool_paged(cache4, page_table, pw_all, layer, pages):
    nb, n_pages = page_table.shape
    steps = n_pages // pages
    rows = PAGE_SIZE * 4

    def page_spec(k):
        return pl.BlockSpec((None, None, rows, NSA_HD),
                            lambda b, s, pt: (layer, pt[b, s * pages + k], 0, 0))

    grid_spec = pltpu.PrefetchScalarGridSpec(
        num_scalar_prefetch=1,
        grid=(nb, steps),
        in_specs=[page_spec(k) for k in range(pages)]
        + [pl.BlockSpec((None, rows, NSA_HD), lambda b, s, pt: (layer, 0, 0))],
        out_specs=pl.BlockSpec((None, 2 * pages, 4 * NSA_HD), lambda b, s, pt: (b, s, 0)),
    )
    return pl.pallas_call(
        functools.partial(_paged_pool_kernel, pages=pages),
        grid_spec=grid_spec,
        out_shape=jax.ShapeDtypeStruct((nb, 2 * n_pages, 4 * NSA_HD), F32),
        compiler_params=_params("parallel", "arbitrary"),
        name="pool_paged",
    )(page_table, *([cache4] * pages), pw_all)


def _cmp_mlp_kernel(p_ref, w1_ref, w2_ref, o_ref):
    for s in range(4):
        kv = s // NSA_KV_HEADS
        h = _silu(_dot_hi(p_ref[:, s * NSA_HD:(s + 1) * NSA_HD], w1_ref[kv]))
        o_ref[:, s * NSA_HD:(s + 1) * NSA_HD] = _dot_hi(h, w2_ref[kv])


def _cmp_mlp(pooled, w1_all, w2_all, layer):
    nb, n, width = pooled.shape
    wspec = pl.BlockSpec((None, 2, NSA_HD, NSA_HD), lambda b: (layer, 0, 0, 0))
    return pl.pallas_call(
        _cmp_mlp_kernel,
        grid=(nb,),
        in_specs=[pl.BlockSpec((None, n, width), lambda b: (b, 0, 0)), wspec, wspec],
        out_specs=pl.BlockSpec((None, n, width), lambda b: (b, 0, 0)),
        out_shape=jax.ShapeDtypeStruct((nb, n, width), F32),
        compiler_params=_params("parallel"),
        name="cmp_mlp",
    )(pooled, w1_all, w2_all)


def _select_blocks(score, st_ref, nbp, n_causal):
    st_ref[...] = score.T
    halves = [st_ref[g * nbp:(g + 1) * nbp, :] for g in range(NSA_KV_HEADS)]
    bi = lax.broadcasted_iota(jnp.int32, halves[0].shape, 0)

    def body(j, ranks):
        tie = jnp.where(j < bi, 1, 0)
        out = []
        for g in range(NSA_KV_HEADS):
            other = st_ref[pl.ds(g * nbp + j, 1), :]
            out.append(ranks[g] + jnp.where(other > halves[g], 1, jnp.where(other == halves[g], tie, 0)))
        return tuple(out)

    ranks = lax.fori_loop(0, n_causal, body, tuple(jnp.zeros(bi.shape, jnp.int32) for _ in halves))
    keep = [jnp.where((ranks[g] < NSA_TOPN) & (halves[g] >= 0.0), 1.0, 0.0) for g in range(NSA_KV_HEADS)]
    return jnp.concatenate(keep, axis=0).T


def _cmp_attn_kernel(q_ref, ckv_ref, bias_ref, o_ref, sc_ref, *scratch, pos0, tq, nb, nbs, nbp, select):
    qi = pl.program_id(1)
    t = pos0 + qi * tq + lax.broadcasted_iota(jnp.int32, (tq, 1), 0)
    blk = lax.broadcasted_iota(jnp.int32, (1, nb), 1)
    valid = t >= blk * NSA_BLOCK + (NSA_BLOCK - 1)
    bi = lax.broadcasted_iota(jnp.int32, (1, nbp), 1)
    cur = t // NSA_BLOCK
    forced = (bi == 0) | (bi == cur) | (bi == cur - 1)
    causal = bi <= cur
    scores = []
    for g in range(NSA_KV_HEADS):
        ck = ckv_ref[:, g * NSA_HD:(g + 1) * NSA_HD]
        cv = ckv_ref[:, (NSA_KV_HEADS + g) * NSA_HD:(NSA_KV_HEADS + g + 1) * NSA_HD]
        imp = jnp.zeros((tq, nb), F32)
        for r in range(NSA_GROUP):
            h = g * NSA_GROUP + r
            logits = _dot_nt(q_ref[:, h * NSA_HD:(h + 1) * NSA_HD], ck) * SCALE + bias_ref[h]
            logits = jnp.where(valid, logits, NEG)
            mx = jnp.max(logits, axis=-1, keepdims=True)
            p = jnp.where(valid, jnp.exp(logits - mx), 0.0)
            p = p / jnp.maximum(jnp.sum(p, axis=-1, keepdims=True), 1e-30)
            o_ref[:, h * NSA_HD:(h + 1) * NSA_HD] = _dot(p, cv)
            imp = imp + p
        if nbp > nb:
            imp = jnp.concatenate([imp, jnp.zeros((tq, nbp - nb), F32)], axis=-1)
        score = jnp.where(causal, jnp.where(forced, NSA_GROUP + 1.0, imp), -1.0)
        scores.append(jnp.where(bi < nbs, score, -2.0))
    score = jnp.concatenate(scores, axis=-1)
    if select:
        n_causal = jnp.minimum((pos0 + (qi + 1) * tq - 1) // NSA_BLOCK + 1, nbs)
        sc_ref[...] = _select_blocks(score, scratch[0], nbp, n_causal)
    else:
        sc_ref[...] = score


def _cmp_attn(proj, ckv, bias_cmp, nb_batch, seq, pos0, tq, nbs, nbp, select):
    nq = seq // tq
    nb = ckv.shape[1]
    m = nb_batch * seq
    bias_tq = bias_cmp.shape[1] // nq
    return pl.pallas_call(
        functools.partial(_cmp_attn_kernel, pos0=pos0, tq=tq, nb=nb, nbs=nbs, nbp=nbp, select=select),
        scratch_shapes=[pltpu.VMEM((NSA_KV_HEADS * nbp, tq), F32)] if select else [],
        grid=(nb_batch, nq),
        in_specs=[pl.BlockSpec((tq, NSA_WIDTH), lambda b, i: (b * nq + i, C_NQ // NSA_WIDTH)),
                  pl.BlockSpec((None, nb, 2 * NSA_KV_W), lambda b, i: (b, 0, 0)),
                  pl.BlockSpec((NSA_HEADS, bias_tq, nb), lambda b, i: (0, i, 0))],
        out_specs=[pl.BlockSpec((tq, NSA_WIDTH), lambda b, i: (b * nq + i, 0)),
                   pl.BlockSpec((tq, NSA_KV_HEADS * nbp), lambda b, i: (b * nq + i, 0))],
        out_shape=[jax.ShapeDtypeStruct((m, NSA_WIDTH), F32),
                   jax.ShapeDtypeStruct((m, NSA_KV_HEADS * nbp), F32)],
        compiler_params=_params("parallel", "parallel"),
        name="cmp_attn",
    )(proj, ckv, bias_cmp)


def _topk_kernel(s_ref, sel_ref, idx_ref, val_ref, *, nbs):
    s = s_ref[...]
    bi = lax.broadcasted_iota(jnp.int32, s.shape, 0)

    def body(j, rank):
        other = s_ref[pl.ds(j, 1), :]
        beats = (other > s) | ((other == s) & (j < bi))
        return rank + beats.astype(jnp.int32)

    rank = lax.fori_loop(0, nbs, body, jnp.zeros(s.shape, jnp.int32))
    ok = s >= 0.0
    sel_ref[...] = jnp.where((rank < NSA_TOPN) & ok, 1.0, 0.0)
    for r in range(NSA_TOPN):
        hit = rank == r
        idx_ref[r:r + 1, :] = jnp.sum(jnp.where(hit, bi, 0), axis=0, keepdims=True)
        val_ref[r:r + 1, :] = jnp.sum(jnp.where(hit & ok, 1, 0), axis=0, keepdims=True)


def _topk(score_t, nbs):
    rows, cols = score_t.shape
    tc = min(cols, 512)
    return pl.pallas_call(
        functools.partial(_topk_kernel, nbs=nbs),
        grid=(cols // tc,),
        in_specs=[pl.BlockSpec((rows, tc), lambda i: (0, i))],
        out_specs=[pl.BlockSpec((rows, tc), lambda i: (0, i)),
                   pl.BlockSpec((NSA_TOPN, tc), lambda i: (0, i)),
                   pl.BlockSpec((NSA_TOPN, tc), lambda i: (0, i))],
        out_shape=[jax.ShapeDtypeStruct((rows, cols), F32),
                   jax.ShapeDtypeStruct((NSA_TOPN, cols), jnp.int32),
                   jax.ShapeDtypeStruct((NSA_TOPN, cols), jnp.int32)],
        compiler_params=_params("parallel"),
        name="topk",
    )(score_t)


PQ = 128
PK = 512


def _softmax_rows(s, ok):
    mx = jnp.max(s, axis=-1, keepdims=True)
    p = jnp.where(ok, jnp.exp(s - mx), 0.0)
    return p, jnp.sum(p, axis=-1, keepdims=True)


def _nsa_prompt_kernel(q_ref, ks_ref, vs_ref, kw_ref, vw_ref, sel_ref, bt_ref, ocmp_ref, z_ref, gate_ref,
                       o_ref, s2_ref, p2_ref, sw_ref, pw_ref, m_ref, l_ref, acc_ref, *, nq, wt):
    qi = pl.program_id(2)
    rows = NSA_GROUP * PQ
    q4 = jnp.concatenate([q_ref[:, r * NSA_HD:(r + 1) * NSA_HD] for r in range(NSA_GROUP)], axis=0)
    q4 = (q4 * (SCALE * LOG2E)).astype(BF16)
    t = qi * PQ + lax.broadcasted_iota(jnp.int32, (PQ, 1), 0)
    nsel = sel_ref.shape[1]
    blk0 = pl.program_id(1) * (nsel // NSA_KV_HEADS)
    sub = PK // PQ
    unselected = jnp.where(sel_ref[...] > 0.5, 0.0, NEG).astype(BF16)
    qx = jnp.concatenate([q4, jnp.concatenate([unselected] * NSA_GROUP, axis=0)], axis=1)

    m_ref[...] = jnp.full((rows, PQ), NEG, F32)
    l_ref[...] = jnp.zeros((rows, PQ), F32)
    acc_ref[...] = jnp.zeros((rows, NSA_HD), F32)

    def key_tile(kt, diagonal, slot):
        s_ref, p_ref = s2_ref.at[slot], p2_ref.at[slot]
        k0 = pl.multiple_of(kt * PK, PK)
        member = (lax.broadcasted_iota(jnp.int32, (PK, nsel), 1)
                  == blk0 + kt * (PK // NSA_BLOCK) + lax.broadcasted_iota(jnp.int32, (PK, nsel), 0) // NSA_BLOCK)
        kx = jnp.concatenate([ks_ref[pl.ds(k0, PK), :].astype(BF16), jnp.where(member, 1.0, 0.0).astype(BF16)],
                             axis=1)
        s_ref[...] = lax.dot_general(qx, kx, (((1,), (1,)), ((), ())), preferred_element_type=F32)
        if diagonal:
            future = jnp.where(k0 + lax.broadcasted_iota(jnp.int32, (1, PK), 1) <= t, 0.0, NEG)
        for r in range(NSA_GROUP):
            rs = slice(r * PQ, (r + 1) * PQ)
            zmax = None
            for c in range(sub):
                cs = slice(c * PQ, (c + 1) * PQ)
                d = jnp.clip(qi - (kt * sub + c), 0, N_BIAS_TILES - 1)
                z = s_ref[rs, cs] + bt_ref[r, d]
                if diagonal:
                    z = z + future[:, cs]
                s_ref[rs, cs] = z
                zmax = z if zmax is None else jnp.maximum(zmax, z)
            m_old = m_ref[rs]
            mx = jnp.maximum(m_old, jnp.broadcast_to(jnp.max(zmax, axis=-1, keepdims=True), (PQ, PQ)))
            alpha = jnp.exp2(m_old - mx)
            psum = None
            for c in range(sub):
                cs = slice(c * PQ, (c + 1) * PQ)
                p = jnp.exp2(s_ref[rs, cs] - mx)
                p_ref[rs, cs] = p.astype(BF16)
                psum = p if psum is None else psum + p
            l_ref[rs] = alpha * l_ref[rs] + psum
            acc_ref[rs] = alpha * acc_ref[rs]
            m_ref[rs] = mx
        acc_ref[...] += jnp.dot(p_ref[...], vs_ref[pl.ds(k0, PK), :].astype(BF16), preferred_element_type=F32)

    last = (qi * PQ) // PK

    def body(j, carry):
        key_tile(2 * j, False, 0)
        key_tile(2 * j + 1, False, 1)
        return carry

    lax.fori_loop(0, last // 2, body, 0)

    @pl.when(last % 2 == 1)
    def _():
        key_tile(last - 1, False, 0)

    key_tile(last, True, 1)
    o_slc = acc_ref[...] / jnp.maximum(jnp.sum(l_ref[...], axis=-1, keepdims=True), 1e-30)

    wk = wt * PQ
    kst = jnp.clip(qi - (wt - 1), 0, nq - wt)
    k0 = pl.multiple_of(kst * PQ, PQ)
    sw_ref[...] = lax.dot_general(q4, kw_ref[pl.ds(k0, wk), :].astype(BF16), (((1,), (1,)), ((), ())),
                                  preferred_element_type=F32)
    dist = t - (k0 + lax.broadcasted_iota(jnp.int32, (1, wk), 1))
    outside = jnp.where((dist >= 0) & (dist <= WINDOW), 0.0, NEG)
    dens = []
    for r in range(NSA_GROUP):
        rs = slice(r * PQ, (r + 1) * PQ)
        zmax = None
        for c in range(wt):
            cs = slice(c * PQ, (c + 1) * PQ)
            d = jnp.clip(qi - (kst + c), 0, N_BIAS_TILES - 1)
            z = sw_ref[rs, cs] + bt_ref[r, d] + outside[:, cs]
            sw_ref[rs, cs] = z
            zmax = z if zmax is None else jnp.maximum(zmax, z)
        mx = jnp.broadcast_to(jnp.max(zmax, axis=-1, keepdims=True), (PQ, PQ))
        psum = None
        for c in range(wt):
            cs = slice(c * PQ, (c + 1) * PQ)
            p = jnp.exp2(sw_ref[rs, cs] - mx)
            pw_ref[rs, cs] = p.astype(BF16)
            psum = p if psum is None else psum + p
        dens.append(jnp.sum(psum, axis=-1, keepdims=True))
    pv = jnp.dot(pw_ref[...], vw_ref[pl.ds(k0, wk), :].astype(BF16), preferred_element_type=F32)
    o_win = [pv[r * PQ:(r + 1) * PQ] / jnp.maximum(dens[r], 1e-30) for r in range(NSA_GROUP)]

    gates = jax.nn.sigmoid(gate_ref[...])
    for r in range(NSA_GROUP):
        sl = slice(r * NSA_HD, (r + 1) * NSA_HD)
        rs = slice(r * PQ, (r + 1) * PQ)
        o = (gates[:, 3 * r:3 * r + 1] * ocmp_ref[:, sl] + gates[:, 3 * r + 1:3 * r + 2] * o_slc[rs]
             + gates[:, 3 * r + 2:3 * r + 3] * o_win[r])
        o_ref[:, sl] = o * _silu(z_ref[:, sl])


def _nsa_prompt(proj, sel, bias_tiles, o_cmp, nb_batch, seq):
    nq = seq // PQ
    wt = min(WINDOW // PQ + 1, nq)
    gw = NSA_GROUP * NSA_HD
    rows = NSA_GROUP * PQ
    nsel = sel.shape[-1]

    def kv_spec(col):
        return pl.BlockSpec((seq, NSA_HD), lambda b, g, i: (b, col // NSA_HD + g))

    return pl.pallas_call(
        functools.partial(_nsa_prompt_kernel, nq=nq, wt=wt),
        grid=(nb_batch, NSA_KV_HEADS, nq),
        in_specs=[pl.BlockSpec((PQ, gw), lambda b, g, i: (b * nq + i, C_NQ // gw + g)),
                  kv_spec(C_SLC), kv_spec(C_SLC + NSA_KV_W), kv_spec(C_WIN), kv_spec(C_WIN + NSA_KV_W),
                  pl.BlockSpec((PQ, nsel), lambda b, g, i: (b * nq + i, 0)),
                  pl.BlockSpec((NSA_GROUP, N_BIAS_TILES, PQ, PQ), lambda b, g, i: (g, 0, 0, 0)),
                  pl.BlockSpec((PQ, gw), lambda b, g, i: (b * nq + i, g)),
                  pl.BlockSpec((PQ, gw), lambda b, g, i: (b * nq + i, C_NZ // gw + g)),
                  pl.BlockSpec((PQ, LANES), lambda b, g, i: (b * nq + i, C_G0 // LANES + g))],
        out_specs=pl.BlockSpec((PQ, gw), lambda b, g, i: (b * nq + i, g)),
        out_shape=jax.ShapeDtypeStruct((nb_batch * seq, NSA_WIDTH), F32),
        scratch_shapes=[pltpu.VMEM((2, rows, PK), F32), pltpu.VMEM((2, rows, PK), BF16),
                        pltpu.VMEM((rows, wt * PQ), F32), pltpu.VMEM((rows, wt * PQ), BF16),
                        pltpu.VMEM((rows, PQ), F32), pltpu.VMEM((rows, PQ), F32), pltpu.VMEM((rows, NSA_HD), F32)],
        compiler_params=_params("parallel", "parallel", "arbitrary"),
        name="nsa_prompt",
    )(proj, proj, proj, proj, proj, sel, bias_tiles, o_cmp, proj, proj)


def _bias_by_threshold(dist, tcols):
    bias = jnp.broadcast_to(tcols[:, 0:1], (tcols.shape[0], dist.shape[-1]))
    for b in range(1, NUM_BUCKETS):
        bias = jnp.where(dist >= _BUCKET_THR[b - 1], tcols[:, b:b + 1], bias)
    return bias


SLOT_CHUNKS = NSA_TOPN + 2


def _slc_sample_kernel(pt_ref, idx_ref, val_ref, *refs, past, lq):
    del pt_ref
    n_in = NSA_KV_HEADS * NSA_TOPN
    page_refs = refs[:n_in]
    q_ref, knew_ref, tcol_ref, o_ref, kc_ref, vc_ref = refs[n_in:]
    b = pl.program_id(0)
    qi = pl.program_id(1)
    t = past + qi
    n_past_blocks = past // NSA_BLOCK
    nk = SLOT_CHUNKS * NSA_BLOCK
    lane = lax.broadcasted_iota(jnp.int32, (1, nk), 1)
    slot = lane // NSA_BLOCK
    this_q = lax.broadcasted_iota(jnp.int32, (lq, 1), 0) == qi
    for g in range(NSA_KV_HEADS):
        base = ((b * lq + qi) * NSA_KV_HEADS + g) * NSA_TOPN
        blk_of_lane = jnp.where(slot == NSA_TOPN, n_past_blocks, n_past_blocks + 1)
        use_of_lane = jnp.where(slot == NSA_TOPN, 1, 0)
        for j in range(NSA_TOPN):
            n_j = idx_ref[base + j]
            use = jnp.where((val_ref[base + j] > 0) & (n_j < n_past_blocks), 1, 0)
            blk_of_lane = jnp.where(slot == j, n_j, blk_of_lane)
            use_of_lane = jnp.where(slot == j, use, use_of_lane)
            page = page_refs[g * NSA_TOPN + j]
            kc_ref[j * NSA_BLOCK:(j + 1) * NSA_BLOCK, :] = page[pl.ds(g, NSA_BLOCK, stride=4), :]
            vc_ref[j * NSA_BLOCK:(j + 1) * NSA_BLOCK, :] = page[pl.ds(NSA_KV_HEADS + g, NSA_BLOCK, stride=4), :]
        new0 = NSA_TOPN * NSA_BLOCK
        kc_ref[new0:, :] = jnp.zeros((nk - new0, NSA_HD), F32)
        vc_ref[new0:, :] = jnp.zeros((nk - new0, NSA_HD), F32)
        kc_ref[new0:new0 + lq, :] = knew_ref[:, g * NSA_HD:(g + 1) * NSA_HD]
        vc_ref[new0:new0 + lq, :] = knew_ref[:, (NSA_KV_HEADS + g) * NSA_HD:(NSA_KV_HEADS + g + 1) * NSA_HD]
        q4 = jnp.concatenate(
            [jnp.sum(jnp.where(this_q, q_ref[:, (g * NSA_GROUP + r) * NSA_HD:(g * NSA_GROUP + r + 1) * NSA_HD], 0.0),
                     axis=0, keepdims=True)
             for r in range(NSA_GROUP)] + [jnp.zeros((SUBLANES - NSA_GROUP, NSA_HD), F32)], axis=0)
        dist = t - (blk_of_lane * NSA_BLOCK + lane % NSA_BLOCK)
        ok = (use_of_lane > 0) & (dist >= 0)
        s = _dot_nt(q4, kc_ref[...]) * SCALE + _bias_by_threshold(dist, tcol_ref[g])
        s = jnp.where(ok, s, NEG)
        p, den = _softmax_rows(s, ok)
        o = _dot(p, vc_ref[...]) / jnp.maximum(den, 1e-30)
        for r in range(NSA_GROUP):
            h = g * NSA_GROUP + r
            o_ref[:, h * NSA_HD:(h + 1) * NSA_HD] = o[r:r + 1, :]


def _slc_sample(proj, cache4, page_table, idx_flat, val_flat, tcols, layer, nb_batch, lq, past):
    rows = NSA_BLOCK * 4
    n_past_blocks = past // NSA_BLOCK

    def page_spec(g, j):
        def index_map(b, i, pt, idx, val):
            n = jnp.minimum(idx[((b * lq + i) * NSA_KV_HEADS + g) * NSA_TOPN + j], n_past_blocks - 1)
            return (layer, pt[b, n // 2], n % 2, 0)
        return pl.BlockSpec((None, None, rows, NSA_HD), index_map)

    grid_spec = pltpu.PrefetchScalarGridSpec(
        num_scalar_prefetch=3,
        grid=(nb_batch, lq),
        in_specs=[page_spec(g, j) for g in range(NSA_KV_HEADS) for j in range(NSA_TOPN)]
        + [pl.BlockSpec((lq, NSA_WIDTH), lambda b, i, *_: (b, C_NQ // NSA_WIDTH)),
           pl.BlockSpec((lq, 2 * NSA_KV_W), lambda b, i, *_: (b, C_SLC // (2 * NSA_KV_W))),
           pl.BlockSpec((NSA_KV_HEADS, SUBLANES, NUM_BUCKETS), lambda b, i, *_: (0, 0, 0))],
        out_specs=pl.BlockSpec((None, 1, NSA_WIDTH), lambda b, i, *_: (b * lq + i, 0, 0)),
        scratch_shapes=[pltpu.VMEM((SLOT_CHUNKS * NSA_BLOCK, NSA_HD), F32),
                        pltpu.VMEM((SLOT_CHUNKS * NSA_BLOCK, NSA_HD), F32)],
    )
    out = pl.pallas_call(
        functools.partial(_slc_sample_kernel, past=past, lq=lq),
        grid_spec=grid_spec,
        out_shape=jax.ShapeDtypeStruct((nb_batch * lq, 1, NSA_WIDTH), F32),
        compiler_params=_params("parallel", "arbitrary"),
        name="slc_sample",
    )(page_table, idx_flat, val_flat, *([cache4] * (NSA_KV_HEADS * NSA_TOPN)), proj, proj, tcols)
    return out.reshape(nb_batch * lq, NSA_WIDTH)


def _win_sample_kernel(q_ref, wpast_ref, wnew_ref, tcol_ref, ocmp_ref, oslc_ref, z_ref, g0_ref, g1_ref,
                       o_ref, kc_ref, vc_ref, *, past, lq, wlen):
    rows = NSA_GROUP * lq
    nk = kc_ref.shape[0]
    lane = lax.broadcasted_iota(jnp.int32, (1, nk), 1)
    kpos = past - wlen + lane
    t = past + lax.broadcasted_iota(jnp.int32, (rows, 1), 0) % lq
    dist = t - kpos
    ok = (dist >= 0) & (dist <= WINDOW) & (kpos >= 0) & (lane < wlen + lq)
    gate_refs = (g0_ref, g1_ref)
    for g in range(NSA_KV_HEADS):
        kc_ref[0:wlen, :] = wpast_ref[pl.ds(g, wlen, stride=4), :]
        vc_ref[0:wlen, :] = wpast_ref[pl.ds(NSA_KV_HEADS + g, wlen, stride=4), :]
        kc_ref[wlen:, :] = jnp.zeros((nk - wlen, NSA_HD), F32)
        vc_ref[wlen:, :] = jnp.zeros((nk - wlen, NSA_HD), F32)
        kc_ref[wlen:wlen + lq, :] = wnew_ref[:, g * NSA_HD:(g + 1) * NSA_HD]
        vc_ref[wlen:wlen + lq, :] = wnew_ref[:, (NSA_KV_HEADS + g) * NSA_HD:(NSA_KV_HEADS + g + 1) * NSA_HD]
        q4 = jnp.concatenate([q_ref[:, (g * NSA_GROUP + r) * NSA_HD:(g * NSA_GROUP + r + 1) * NSA_HD]
                              for r in range(NSA_GROUP)], axis=0)
        s = _dot_nt(q4, kc_ref[...]) * SCALE + _bias_by_threshold(dist, tcol_ref[g])
        s = jnp.where(ok, s, NEG)
        p, den = _softmax_rows(s, ok)
        o_win = _dot(p, vc_ref[...]) / jnp.maximum(den, 1e-30)
        gates = jax.nn.sigmoid(gate_refs[g][...])
        for r in range(NSA_GROUP):
            h = g * NSA_GROUP + r
            sl = slice(h * NSA_HD, (h + 1) * NSA_HD)
            o = (gates[:, 3 * r:3 * r + 1] * ocmp_ref[:, sl] + gates[:, 3 * r + 1:3 * r + 2] * oslc_ref[:, sl]
                 + gates[:, 3 * r + 2:3 * r + 3] * o_win[r * lq:(r + 1) * lq])
            o_ref[:, sl] = o * _silu(z_ref[:, sl])


def _win_sample(proj, win4, tcols_rows, o_cmp, o_slc, layer, nb_batch, lq, past):
    wlen = win4.shape[2] // 4
    nk = -(-(wlen + lq) // LANES) * LANES
    row_spec = pl.BlockSpec((lq, NSA_WIDTH), lambda b: (b, 0))
    return pl.pallas_call(
        functools.partial(_win_sample_kernel, past=past, lq=lq, wlen=wlen),
        grid=(nb_batch,),
        in_specs=[pl.BlockSpec((lq, NSA_WIDTH), lambda b: (b, C_NQ // NSA_WIDTH)),
                  pl.BlockSpec((None, None, wlen * 4, NSA_HD), lambda b: (layer, b, 0, 0)),
                  pl.BlockSpec((lq, 2 * NSA_KV_W), lambda b: (b, C_WIN // (2 * NSA_KV_W))),
                  pl.BlockSpec((NSA_KV_HEADS, NSA_GROUP * lq, NUM_BUCKETS), lambda b: (0, 0, 0)),
                  row_spec, row_spec,
                  pl.BlockSpec((lq, NSA_WIDTH), lambda b: (b, C_NZ // NSA_WIDTH)),
                  pl.BlockSpec((lq, LANES), lambda b: (b, C_G0 // LANES)),
                  pl.BlockSpec((lq, LANES), lambda b: (b, C_G1 // LANES))],
        out_specs=row_spec,
        out_shape=jax.ShapeDtypeStruct((nb_batch * lq, NSA_WIDTH), F32),
        scratch_shapes=[pltpu.VMEM((nk, NSA_HD), F32), pltpu.VMEM((nk, NSA_HD), F32)],
        compiler_params=_params("parallel"),
        name="win_sample",
    )(proj, win4, proj, tcols_rows, o_cmp, o_slc, proj, proj, proj)


def _out_kernel(dn_ref, nsa_ref, x_ref, p_ref, wo_ref, gpost_ref, wpp_ref, gple_ref, wpg_ref, o_ref):
    half = dn_ref.shape[1]
    mixed = (jnp.dot(dn_ref[...].astype(BF16), wo_ref[0:half, :], preferred_element_type=F32)
             + jnp.dot(nsa_ref[...].astype(BF16), wo_ref[half:, :], preferred_element_type=F32))
    y = mixed * lax.rsqrt(jnp.mean(mixed * mixed, axis=-1, keepdims=True) + EPS) * gpost_ref[...]
    x1 = x_ref[...] + y
    e = jnp.dot(p_ref[...].astype(BF16), wpp_ref[...], preferred_element_type=F32)
    e = e * lax.rsqrt(jnp.mean(e * e, axis=-1, keepdims=True) + EPS) * gple_ref[...]
    gate = jax.nn.sigmoid(jnp.dot(x1.astype(BF16), wpg_ref[...], preferred_element_type=F32))
    o_ref[...] = x1 + gate * e


def _out_proj(dn, nsa, x, p_all, wo_all, gpost_all, wpp_all, gple_all, wpg_all, layer, tm):
    m = x.shape[0]
    once = pl.Buffered(1)

    def wspec(k):
        return pl.BlockSpec((None, k, D_MODEL), lambda i: (layer, 0, 0), pipeline_mode=once)

    return pl.pallas_call(
        _out_kernel,
        grid=(m // tm,),
        in_specs=[pl.BlockSpec((tm, dn.shape[1]), lambda i: (i, 0)),
                  pl.BlockSpec((tm, nsa.shape[1]), lambda i: (i, 0)),
                  pl.BlockSpec((tm, D_MODEL), lambda i: (i, 0)),
                  pl.BlockSpec((None, tm, PLE_DIM), lambda i: (layer, i, 0)),
                  wspec(D_MODEL), wspec(1), wspec(PLE_DIM), wspec(1), wspec(D_MODEL)],
        out_specs=pl.BlockSpec((tm, D_MODEL), lambda i: (i, 0)),
        out_shape=jax.ShapeDtypeStruct((m, D_MODEL), F32),
        compiler_params=_params("parallel"),
        name="out_proj",
    )(dn, nsa, x, p_all, wo_all, gpost_all, wpp_all, gple_all, wpg_all)


def _kv_rows_kernel(x_ref, cmp_ref, slc_ref, win_ref):
    tm = x_ref.shape[0]
    width = 2 * NSA_KV_W

    def rows(o_ref, seg):
        for s in range(4):
            o_ref[pl.ds(s, tm, stride=4), :] = x_ref[:, seg * width + s * NSA_HD:seg * width + (s + 1) * NSA_HD]

    rows(cmp_ref, 0)
    rows(slc_ref, 1)

    @pl.when(pl.program_id(1) == pl.num_programs(1) - 1)
    def _():
        rows(win_ref, 2)


def _kv_rows(proj, nb, seq):
    tm = min(WINDOW, seq)
    nr = seq // tm
    width = 3 * 2 * NSA_KV_W
    return pl.pallas_call(
        _kv_rows_kernel,
        grid=(nb, nr),
        in_specs=[pl.BlockSpec((tm, width), lambda b, r: (b * nr + r, C_CMP // width))],
        out_specs=[pl.BlockSpec((tm * 4, NSA_HD), lambda b, r: (b * nr + r, 0)),
                   pl.BlockSpec((tm * 4, NSA_HD), lambda b, r: (b * nr + r, 0)),
                   pl.BlockSpec((None, tm * 4, NSA_HD), lambda b, r: (b, 0, 0))],
        out_shape=[jax.ShapeDtypeStruct((nb * seq * 4, NSA_HD), F32),
                   jax.ShapeDtypeStruct((nb * seq * 4, NSA_HD), F32),
                   jax.ShapeDtypeStruct((nb, tm * 4, NSA_HD), F32)],
        compiler_params=_params("parallel", "arbitrary"),
        name="kv_rows",
    )(proj)


W_TILE = 512


def _relayout_w_in_kernel(wt_ref, tail_ref, o_ref):
    is_tail = pl.program_id(1) == pl.num_programs(1) - 1

    @pl.when(jnp.logical_not(is_tail))
    def _():
        o_ref[...] = wt_ref[0].T.astype(BF16)

    @pl.when(is_tail)
    def _():
        o_ref[...] = tail_ref[...].T.astype(BF16)


def _relayout_w_in(w_in):
    depth, k, _ = w_in.shape
    wt = jnp.swapaxes(w_in, 1, 2)
    half = 3 * NSA_GROUP

    def zeros(n):
        return jnp.zeros((depth, n, k), w_in.dtype)

    tail = jnp.concatenate([wt[:, _O_BETA:_O_NQ], zeros(LANES - 2 * DN_HEADS),
                            wt[:, _O_GATE:_O_GATE + half], zeros(LANES - half),
                            wt[:, _O_GATE + half:_O_GATE + 2 * half], zeros(N_PAD - C_G1 - half)], axis=1)
    n_tiles = N_PAD // W_TILE
    moved = C_NQ // W_TILE

    def src_row(d, j):
        j = jnp.minimum(j, n_tiles - 2)
        return (d, pl.multiple_of(j * W_TILE + jnp.where(j >= moved, _O_NQ - C_NQ, 0), SUBLANES), 0)

    return pl.pallas_call(
        _relayout_w_in_kernel,
        grid=(depth, n_tiles),
        in_specs=[pl.BlockSpec((pl.Element(1), pl.Element(W_TILE), pl.Element(k)), src_row),
                  pl.BlockSpec((None, W_TILE, k), lambda d, j: (d, 0, 0))],
        out_specs=pl.BlockSpec((None, k, W_TILE), lambda d, j: (d, 0, j)),
        out_shape=jax.ShapeDtypeStruct((depth, k, N_PAD), BF16),
        compiler_params=_params("parallel", "arbitrary"),
        name="relayout_w_in",
    )(wt, tail)


def _bias_expand_kernel(t_ref, bucket_ref, o_ref):
    bucket = bucket_ref[...]
    for h in range(NSA_HEADS):
        acc = jnp.full(bucket.shape, t_ref[0, h], F32)
        for b in range(1, NUM_BUCKETS):
            acc = jnp.where(bucket >= b, t_ref[b, h], acc)
        o_ref[h] = acc


def _bias_expand(table, bucket):
    rows, cols = bucket.shape
    tr = max(t for t in range(SUBLANES, min(rows, 512) + 1, SUBLANES) if rows % t == 0)
    return pl.pallas_call(
        _bias_expand_kernel,
        grid=(rows // tr,),
        in_specs=[pl.BlockSpec(memory_space=pltpu.SMEM), pl.BlockSpec((tr, cols), lambda i: (i, 0))],
        out_specs=pl.BlockSpec((NSA_HEADS, tr, cols), lambda i: (0, i, 0)),
        out_shape=jax.ShapeDtypeStruct((NSA_HEADS, rows, cols), F32),
        compiler_params=_params("parallel"),
        name="bias_expand",
    )(table, jnp.asarray(bucket))


def _bias_tiles(table):
    i = np.arange(PQ)
    d = np.arange(N_BIAS_TILES)[:, None, None] * PQ + i[None, :, None] - i[None, None, :]
    tiles = _bias_expand(table * LOG2E, _bucket_np(d).reshape(N_BIAS_TILES * PQ, PQ))
    return tiles.reshape(NSA_HEADS, N_BIAS_TILES, PQ, PQ)


def _bias_cmp(table, pos0, lq, nb):
    dist = (pos0 + np.arange(lq))[:, None] - (np.arange(nb) * NSA_BLOCK + NSA_BLOCK - 1)[None, :]
    return _bias_expand(table, _bucket_np(dist))


def kernel(x_prompt, x_sample, state_dn_S, state_dn_conv, cache_win_kv, cache_cmp_kv, cache_slc_kv, page_table,
           p_prompt, p_sample, rel_bias_table, w_in, w_out, g_pre, g_post, dn_conv_w, dn_A_log, dn_dt_bias,
           dn_norm_w, cmp_pos_w, cmp_w1, cmp_w2, w_ple_proj, g_ple, w_ple_gate):
    depth = w_in.shape[0]
    bp, seq, _ = x_prompt.shape
    bs, lq, _ = x_sample.shape
    n_pages = page_table.shape[1]
    past = n_pages * PAGE_SIZE
    n_pool = cache_cmp_kv.shape[1]
    wlen = cache_win_kv.shape[2]
    mp, ms = bp * seq, bs * lq
    assert seq % PK == 0 and seq % DN_CHUNK == 0 and lq == SUBLANES and past % NSA_BLOCK == 0
    kv_row = (2, NSA_KV_HEADS, NSA_HD)

    w_in_r = _relayout_w_in(w_in)
    w_out_b = w_out.astype(BF16)
    wpp_b = w_ple_proj.astype(BF16)
    wpg_b = w_ple_gate.astype(BF16)
    g_pre3 = g_pre[:, None, :]
    g_post3 = g_post[:, None, :]
    g_ple3 = g_ple[:, None, :]
    nw3 = dn_norm_w[:, None, :]
    lane_pad = ((0, 0), (DN_HEADS, LANES - 2 * DN_HEADS))
    dn_par = jnp.stack([jnp.pad(dn_A_log, lane_pad), jnp.pad(dn_dt_bias, lane_pad)], axis=1)
    pw_prompt = jnp.broadcast_to(cmp_pos_w[:, :, None, :, None], (depth, 2, NSA_KV_HEADS, NSA_BLOCK, NSA_HD))
    pw_prompt = jnp.transpose(pw_prompt, (0, 3, 1, 2, 4)).reshape(depth, NSA_BLOCK, 2 * NSA_KV_W)
    pw_page = jnp.broadcast_to(cmp_pos_w[:, None, None, :, :, None],
                               (depth, PAGE_SIZE // NSA_BLOCK, NSA_KV_HEADS, 2, NSA_BLOCK, NSA_HD))
    pw_page = jnp.transpose(pw_page, (0, 1, 4, 3, 2, 5)).reshape(depth, PAGE_SIZE * 4, NSA_HD)
    table = rel_bias_table.astype(F32)
    bias_tiles = _bias_tiles(table)
    nb_p = seq // NSA_BLOCK
    nb_s = past // NSA_BLOCK
    nbs_s = -(-(past + lq) // NSA_BLOCK)
    nbp_s = -(-nbs_s // LANES) * LANES
    bias_cmp_p = _bias_cmp(table, 0, seq, nb_p)
    bias_cmp_s = _bias_cmp(table, past, lq, nb_s)
    tcols = jnp.pad(table.T.reshape(NSA_KV_HEADS, NSA_GROUP, NUM_BUCKETS),
                    ((0, 0), (0, SUBLANES - NSA_GROUP), (0, 0)))
    tcols_rows = jnp.repeat(table.T.reshape(NSA_KV_HEADS, NSA_GROUP, NUM_BUCKETS), lq, axis=1)

    conv0_s = jnp.pad(state_dn_conv, ((0, 0), (0, 0), (SUBLANES - (CONV_W - 1), 0), (0, 0)))
    conv0_p = jnp.zeros((1, bp, SUBLANES, DN_CONV_CH), F32)
    s0_p = jnp.zeros((1, bp, DN_HEADS, DN_DK, DN_DV), F32)
    cmp4 = cache_cmp_kv.reshape(depth, n_pool, PAGE_SIZE * 4, NSA_HD)
    slc4 = cache_slc_kv.reshape(depth, n_pool, PAGE_SIZE * 4, NSA_HD)
    win4 = cache_win_kv.reshape(depth, bs, wlen * 4, NSA_HD)
    p_prompt3 = p_prompt.reshape(depth, mp, PLE_DIM)
    p_sample3 = p_sample.reshape(depth, ms, PLE_DIM)

    xp = x_prompt.reshape(mp, D_MODEL)
    xs = x_sample.reshape(ms, D_MODEL)
    outs_p = [[] for _ in range(2)]
    outs_s = [[] for _ in range(5)]
    kv_rows_p = [[] for _ in range(3)]
    tq_p = min(256, seq)
    for i in range(depth):
        proj = _in_proj(xp, g_pre3, w_in_r, i, min(1024, mp))
        dn_o, conv_n, s_n = _deltanet(proj, bp, seq, DN_CHUNK, conv0_p, s0_p, 0, dn_conv_w, dn_par, nw3, i)
        ckv = _cmp_mlp(_pool_prompt(proj, bp, seq, pw_prompt, i), cmp_w1, cmp_w2, i)
        o_cmp, sel = _cmp_attn(proj, ckv, bias_cmp_p, bp, seq, 0, tq_p, nb_p, nb_p, True)
        nsa_o = _nsa_prompt(proj, sel, bias_tiles, o_cmp, bp, seq)
        xp = _out_proj(dn_o, nsa_o, xp, p_prompt3, w_out_b, g_post3, wpp_b, g_ple3, wpg_b, i, min(256, mp))
        for dst, rows in zip(kv_rows_p, _kv_rows(proj, bp, seq)):
            dst.append(rows)
        outs_p[0].append(s_n)
        outs_p[1].append(conv_n[:, SUBLANES - (CONV_W - 1):])

        proj = _in_proj(xs, g_pre3, w_in_r, i, ms)
        dn_o, conv_n, s_n = _deltanet(proj, bs, lq, lq, conv0_s, state_dn_S, i, dn_conv_w, dn_par, nw3, i)
        pooled = _pool_paged(cmp4, page_table, pw_page, i, min(16, n_pages))
        ckv = _cmp_mlp(pooled, cmp_w1, cmp_w2, i)
        o_cmp, score = _cmp_attn(proj, ckv, bias_cmp_s, bs, lq, past, lq, nbs_s, nbp_s, False)
        score_t = jnp.transpose(score.reshape(ms * NSA_KV_HEADS, nbp_s))
        _, idx_t, val_t = _topk(score_t, nbs_s)
        o_slc = _slc_sample(proj, slc4, page_table, idx_t.T.reshape(-1), val_t.T.reshape(-1), tcols, i,
                            bs, lq, past)
        nsa_o = _win_sample(proj, win4, tcols_rows, o_cmp, o_slc, i, bs, lq, past)
        xs = _out_proj(dn_o, nsa_o, xs, p_sample3, w_out_b, g_post3, wpp_b, g_ple3, wpg_b, i, ms)
        kv_all = proj[:, C_CMP:C_WIN + 2 * NSA_KV_W].reshape(bs, lq, 3, *kv_row)
        win_all = jnp.concatenate([cache_win_kv[i], kv_all[:, :, 2]], axis=1)
        outs_s[0].append(s_n)
        outs_s[1].append(conv_n[:, SUBLANES - (CONV_W - 1):])
        outs_s[2].append(win_all[:, win_all.shape[1] - min(WINDOW, past + lq):])
        outs_s[3].append(kv_all[:, :, 0])
        outs_s[4].append(kv_all[:, :, 1])

    p_cmp, p_slc, p_win = (jnp.stack(o) for o in kv_rows_p)
    return ((xp.reshape(bp, seq, D_MODEL), xs.reshape(bs, lq, D_MODEL), jnp.stack(outs_p[0]), jnp.stack(outs_p[1]),
             p_win.reshape(depth, bp, min(WINDOW, seq), *kv_row), p_cmp.reshape(depth, bp, seq, *kv_row),
             p_slc.reshape(depth, bp, seq, *kv_row)) + tuple(jnp.stack(o) for o in outs_s))
```

```python
import functools
import math

import numpy as np
import jax
import jax.numpy as jnp
from jax import lax
from jax.experimental import pallas as pl
from jax.experimental.pallas import tpu as pltpu

F32 = jnp.float32
BF16 = jnp.bfloat16

D_MODEL = 2048
DN_HEADS = 8
DN_DK = 128
DN_DV = 128
DN_QK_W = DN_HEADS * DN_DK
DN_CONV_CH = 2 * DN_QK_W + DN_HEADS * DN_DV
CONV_W = 4
DN_CHUNK = 64
NSA_HEADS = 8
NSA_KV_HEADS = 2
NSA_GROUP = 4
NSA_HD = 128
NSA_WIDTH = NSA_HEADS * NSA_HD
NSA_KV_W = NSA_KV_HEADS * NSA_HD
NSA_BLOCK = 64
NSA_TOPN = 16
WINDOW = 512
SCALE = NSA_HD ** -0.5
NUM_BUCKETS = 32
REL_MAX_DIST = 1024
PLE_DIM = 256
PAGE_SIZE = 128
EPS = 1e-6
NEG = -1e30
LOG2E = math.log2(math.e)

VMEM_LIMIT_BYTES = 56 * 1024 * 1024
LANES = 128
SUBLANES = 8

C_QKV = 0
C_Z = 3072
C_NQ = 4096
C_NZ = 5120
C_CMP = 6144
C_SLC = 6656
C_WIN = 7168
C_SM = 7680
C_G0 = 7808
C_G1 = 7936
N_PAD = 8192
_O_BETA = DN_CONV_CH + DN_HEADS * DN_DV
_O_NQ = _O_BETA + 2 * DN_HEADS
_O_GATE = _O_NQ + 2 * NSA_WIDTH + 3 * 2 * NSA_KV_W
N_BIAS_TILES = 9


def _bucket_np(dist):
    dist = np.maximum(np.asarray(dist, np.int64), 0)
    exact = NUM_BUCKETS // 2
    scaled = np.log(np.maximum(dist, 1).astype(np.float64) / exact) / math.log(REL_MAX_DIST / exact)
    large = np.minimum(exact + (scaled * (NUM_BUCKETS - exact)).astype(np.int64), NUM_BUCKETS - 1)
    return np.where(dist < exact, dist, large).astype(np.int32)


_BUCKET_THR = tuple(int(np.argmax(_bucket_np(np.arange(2 * REL_MAX_DIST)) >= b)) for b in range(1, NUM_BUCKETS))


def _dot(a, b):
    return jnp.dot(a.astype(BF16), b.astype(BF16), preferred_element_type=F32)


def _dot_nt(a, b):
    return lax.dot_general(a.astype(BF16), b.astype(BF16), (((1,), (1,)), ((), ())), preferred_element_type=F32)


def _dot_tn(a, b):
    return lax.dot_general(a.astype(BF16), b.astype(BF16), (((0,), (0,)), ((), ())), preferred_element_type=F32)


def _dot_hi(a, b):
    return jnp.dot(a, b, preferred_element_type=F32, precision=lax.Precision.HIGHEST)


def _dot_hi_nt(a, b):
    return lax.dot_general(a, b, (((1,), (1,)), ((), ())), preferred_element_type=F32,
                           precision=lax.Precision.HIGHEST)


def _silu(x):
    return x * jax.nn.sigmoid(x)


def _params(*sem):
    return pltpu.CompilerParams(dimension_semantics=sem, vmem_limit_bytes=VMEM_LIMIT_BYTES)


def _in_proj_kernel(x_ref, g_ref, w_ref, o_ref, xn_ref):
    @pl.when(pl.program_id(1) == 0)
    def _():
        x = x_ref[...]
        y = x * lax.rsqrt(jnp.mean(x * x, axis=-1, keepdims=True) + EPS)
        xn_ref[...] = (y * g_ref[...]).astype(BF16)

    o_ref[...] = jnp.dot(xn_ref[...], w_ref[...], preferred_element_type=F32)


def _in_proj(x, g_all, w_all, layer, tm):
    m = x.shape[0]
    tn = 1024
    return pl.pallas_call(
        _in_proj_kernel,
        grid=(m // tm, N_PAD // tn),
        in_specs=[pl.BlockSpec((tm, D_MODEL), lambda i, j: (i, 0)),
                  pl.BlockSpec((None, 1, D_MODEL), lambda i, j: (layer, 0, 0)),
                  pl.BlockSpec((None, D_MODEL, tn), lambda i, j: (layer, 0, j))],
        out_specs=pl.BlockSpec((tm, tn), lambda i, j: (i, j)),
        out_shape=jax.ShapeDtypeStruct((m, N_PAD), F32),
        scratch_shapes=[pltpu.VMEM((tm, D_MODEL), BF16)],
        compiler_params=_params("parallel", "arbitrary"),
        name="in_proj",
    )(x, g_all, w_all)


def _dn_prep_kernel(qkv_ref, sm_ref, conv0_ref, cw_ref, par_ref,
                    u_ref, w_ref, qd_ref, kd_ref, qk_ref, gl_ref, convn_ref, xs_ref, *, chunk, cs):
    tb = chunk * cs
    halo = SUBLANES

    @pl.when(pl.program_id(1) == 0)
    def _():
        xs_ref[0:halo, :] = conv0_ref[...]

    xs_ref[halo:halo + tb, :] = qkv_ref[...]
    base = halo - (CONV_W - 1)
    y = xs_ref[base:base + tb, :] * cw_ref[0:1, :]
    for j in range(1, CONV_W):
        y = y + xs_ref[base + j:base + j + tb, :] * cw_ref[j:j + 1, :]
    y = _silu(y)
    tail = xs_ref[tb:tb + halo, :]
    xs_ref[0:halo, :] = tail
    convn_ref[...] = tail

    sm = sm_ref[...]
    beta_all = jax.nn.sigmoid(sm)
    xg = sm + par_ref[1:2, :]
    softplus = jnp.maximum(xg, 0.0) + jnp.log1p(jnp.exp(-jnp.abs(xg)))
    g_all = -jnp.exp(par_ref[0:1, :]) * softplus
    row = lax.broadcasted_iota(jnp.int32, (tb, tb), 0)
    col = lax.broadcasted_iota(jnp.int32, (tb, tb), 1)
    in_chunk_lower = (row >= col) & (row // chunk == col // chunk)
    gc_all = _dot_hi(jnp.where(in_chunk_lower, 1.0, 0.0), g_all)
    pick = (lax.broadcasted_iota(jnp.int32, (DN_HEADS, LANES), 1)
            == lax.broadcasted_iota(jnp.int32, (DN_HEADS, LANES), 0) + DN_HEADS).astype(F32)
    gc_rows = _dot_hi_nt(pick, gc_all)
    for c in range(cs):
        gl_ref[c] = gc_all[(c + 1) * chunk - 1:(c + 1) * chunk, :]
    r64 = lax.broadcasted_iota(jnp.int32, (chunk, chunk), 0)
    c64 = lax.broadcasted_iota(jnp.int32, (chunk, chunk), 1)
    lower = r64 >= c64
    strict = r64 > c64

    lhs, kns, rhs, decays = [], [], [], []
    for h in range(DN_HEADS):
        hc = slice(h * DN_DV, (h + 1) * DN_DV)
        qh = y[:, h * DN_DK:(h + 1) * DN_DK]
        kh = y[:, DN_QK_W + h * DN_DK:DN_QK_W + (h + 1) * DN_DK]
        vh = y[:, 2 * DN_QK_W + h * DN_DV:2 * DN_QK_W + (h + 1) * DN_DV]
        qn = qh * lax.rsqrt(jnp.sum(qh * qh, axis=-1, keepdims=True) + EPS) * (DN_DK ** -0.5)
        kn = kh * lax.rsqrt(jnp.sum(kh * kh, axis=-1, keepdims=True) + EPS)
        beta = beta_all[:, h:h + 1]
        gc = gc_all[:, DN_HEADS + h:DN_HEADS + h + 1]
        egc = jnp.exp(gc)
        kb = kn * beta
        rhs_h = jnp.concatenate([vh * beta, kb * egc], axis=-1)
        qd_ref[:, hc] = (qn * egc).astype(qd_ref.dtype)
        for c in range(cs):
            rs = slice(c * chunk, (c + 1) * chunk)
            gcc = gc[rs]
            decays.append(jnp.exp(jnp.where(lower, gcc - gc_rows[h:h + 1, rs], NEG)))
            lhs.append(jnp.concatenate([kb[rs], qn[rs]], axis=0))
            kns.append(kn[rs])
            rhs.append(rhs_h[rs])
            kd_ref[rs, hc] = (kn[rs] * jnp.exp(gcc[chunk - 1:chunk, :] - gcc)).astype(kd_ref.dtype)
    decay = jnp.stack(decays)
    both = jnp.einsum('nid,njd->nij', jnp.stack(lhs).astype(BF16), jnp.stack(kns).astype(BF16),
                      preferred_element_type=F32)
    a = jnp.where(strict[None], both[:, :chunk] * decay, 0.0)
    qk = both[:, chunk:] * decay

    def bmm(p, q):
        return jnp.einsum('nij,njk->nik', p, q, preferred_element_type=F32)

    def split(v):
        hi = v.astype(BF16)
        return hi, (v - hi.astype(F32)).astype(BF16)

    def bmm3(p, q):
        (ph, pl_), (qh, ql) = p, q
        return bmm(ph, qh) + (bmm(ph, ql) + bmm(pl_, qh))

    pw_s = split(a)
    inv = jnp.where(r64 == c64, 1.0, 0.0)[None] - a
    for _ in range(int(math.log2(chunk)) - 1):
        pw_s = split(bmm3(pw_s, pw_s))
        inv = inv + bmm3(pw_s, split(inv))
    x = bmm3(split(inv), split(jnp.stack(rhs)))
    for h in range(DN_HEADS):
        hc = slice(h * DN_DV, (h + 1) * DN_DV)
        for c in range(cs):
            rs = slice(c * chunk, (c + 1) * chunk)
            n = h * cs + c
            u_ref[rs, hc] = x[n, :, :DN_DV]
            w_ref[rs, hc] = x[n, :, DN_DV:].astype(w_ref.dtype)
            qk_ref[rs, h * chunk:(h + 1) * chunk] = qk[n]


def _dn_scan_kernel(u_ref, w_ref, qd_ref, kd_ref, qk_ref, gl_ref, z_ref, s0_ref, nw_ref,
                    o_ref, sout_ref, s_ref, *, chunk, cs):
    @pl.when(pl.program_id(1) == 0)
    def _():
        s_ref[...] = s0_ref[...]

    def heads(ref, rs, width):
        return jnp.stack([ref[rs, h * width:(h + 1) * width] for h in range(DN_HEADS)])

    def bdot(spec, p, q):
        return jnp.einsum(spec, p.astype(BF16), q.astype(BF16), preferred_element_type=F32)

    for c in range(cs):
        rs = slice(c * chunk, (c + 1) * chunk)
        decay_last = jnp.exp(gl_ref[c])
        decay_last = jnp.stack([decay_last[:, DN_HEADS + h:DN_HEADS + h + 1] for h in range(DN_HEADS)])
        s = s_ref[...]
        sb = s.astype(BF16)
        v_new = heads(u_ref, rs, DN_DV) - bdot('hcd,hde->hce', heads(w_ref, rs, DN_DK), sb)
        o = (bdot('hcd,hde->hce', heads(qd_ref, rs, DN_DK), sb)
             + bdot('hcs,hse->hce', heads(qk_ref, rs, chunk), v_new))
        s_ref[...] = s * decay_last + bdot('hcd,hce->hde', heads(kd_ref, rs, DN_DK), v_new)
        on = o * lax.rsqrt(jnp.mean(o * o, axis=-1, keepdims=True) + EPS) * nw_ref[...]
        for h in range(DN_HEADS):
            hc = slice(h * DN_DV, (h + 1) * DN_DV)
            o_ref[rs, hc] = on[h] * _silu(z_ref[rs, hc])

    @pl.when(pl.program_id(1) == pl.num_programs(1) - 1)
    def _():
        sout_ref[...] = s_ref[...]


def _deltanet(proj, nb, seq, chunk, conv0_all, s0_all, state_layer, cw_all, par_all, nw_all, layer):
    m = nb * seq
    nchunks = seq // chunk
    width = DN_HEADS * DN_DV
    hs = (DN_HEADS, DN_DK, DN_DV)
    cs_a = 2 if nchunks % 2 == 0 else 1
    na = nchunks // cs_a
    ta = cs_a * chunk
    op_dtype = BF16 if chunk % (2 * SUBLANES) == 0 else F32
    u, w, qd, kd, qk, gl, conv_n = pl.pallas_call(
        functools.partial(_dn_prep_kernel, chunk=chunk, cs=cs_a),
        grid=(nb, na),
        in_specs=[pl.BlockSpec((ta, DN_CONV_CH), lambda b, c: (b * na + c, C_QKV // DN_CONV_CH)),
                  pl.BlockSpec((ta, LANES), lambda b, c: (b * na + c, C_SM // LANES)),
                  pl.BlockSpec((None, None, SUBLANES, DN_CONV_CH), lambda b, c: (state_layer, b, 0, 0)),
                  pl.BlockSpec((None, CONV_W, DN_CONV_CH), lambda b, c: (layer, 0, 0)),
                  pl.BlockSpec((None, 2, LANES), lambda b, c: (layer, 0, 0))],
        out_specs=[pl.BlockSpec((ta, width), lambda b, c: (b * na + c, 0)),
                   pl.BlockSpec((ta, width), lambda b, c: (b * na + c, 0)),
                   pl.BlockSpec((ta, width), lambda b, c: (b * na + c, 0)),
                   pl.BlockSpec((ta, width), lambda b, c: (b * na + c, 0)),
                   pl.BlockSpec((ta, DN_HEADS * chunk), lambda b, c: (b * na + c, 0)),
                   pl.BlockSpec((cs_a, 1, LANES), lambda b, c: (b * na + c, 0, 0)),
                   pl.BlockSpec((None, SUBLANES, DN_CONV_CH), lambda b, c: (b, 0, 0))],
        out_shape=[jax.ShapeDtypeStruct((m, width), F32),
                   jax.ShapeDtypeStruct((m, width), op_dtype),
                   jax.ShapeDtypeStruct((m, width), op_dtype),
                   jax.ShapeDtypeStruct((m, width), op_dtype),
                   jax.ShapeDtypeStruct((m, DN_HEADS * chunk), F32),
                   jax.ShapeDtypeStruct((nb * nchunks, 1, LANES), F32),
                   jax.ShapeDtypeStruct((nb, SUBLANES, DN_CONV_CH), F32)],
        scratch_shapes=[pltpu.VMEM((ta + SUBLANES, DN_CONV_CH), F32)],
        compiler_params=_params("parallel", "arbitrary"),
        name="dn_prep",
    )(proj, proj, conv0_all, cw_all, par_all)

    cs_b = 4 if nchunks % 4 == 0 else 1
    nbk = nchunks // cs_b
    tbk = cs_b * chunk
    row_spec = pl.BlockSpec((tbk, width), lambda b, c: (b * nbk + c, 0))
    o, s_n = pl.pallas_call(
        functools.partial(_dn_scan_kernel, chunk=chunk, cs=cs_b),
        grid=(nb, nbk),
        in_specs=[row_spec, row_spec, row_spec, row_spec,
                  pl.BlockSpec((tbk, DN_HEADS * chunk), lambda b, c: (b * nbk + c, 0)),
                  pl.BlockSpec((cs_b, 1, LANES), lambda b, c: (b * nbk + c, 0, 0)),
                  pl.BlockSpec((tbk, width), lambda b, c: (b * nbk + c, C_Z // width)),
                  pl.BlockSpec((None, None) + hs, lambda b, c: (state_layer, b, 0, 0, 0)),
                  pl.BlockSpec((None, 1, DN_DV), lambda b, c: (layer, 0, 0))],
        out_specs=[row_spec, pl.BlockSpec((None,) + hs, lambda b, c: (b, 0, 0, 0))],
        out_shape=[jax.ShapeDtypeStruct((m, width), F32), jax.ShapeDtypeStruct((nb,) + hs, F32)],
        scratch_shapes=[pltpu.VMEM(hs, F32)],
        compiler_params=_params("parallel", "arbitrary"),
        name="dn_scan",
    )(u, w, qd, kd, qk, gl, proj, s0_all, nw_all)
    return o, conv_n, s_n


def _pool_kernel(x_ref, pw_ref, o_ref):
    rows = x_ref.shape[0]
    x = x_ref[...].reshape(rows // NSA_BLOCK, NSA_BLOCK, 2 * NSA_KV_W)
    o_ref[...] = jnp.sum(x * pw_ref[...][None], axis=1)


def _pool_prompt(proj, nb, seq, pw_all, layer):
    rows = min(seq, 1024)
    nr = seq // rows
    width = 2 * NSA_KV_W
    return pl.pallas_call(
        _pool_kernel,
        grid=(nb, nr),
        in_specs=[pl.BlockSpec((rows, width), lambda b, r: (b * nr + r, C_CMP // width)),
                  pl.BlockSpec((None, NSA_BLOCK, width), lambda b, r: (layer, 0, 0))],
        out_specs=pl.BlockSpec((None, rows // NSA_BLOCK, width), lambda b, r: (b, r, 0)),
        out_shape=jax.ShapeDtypeStruct((nb, seq // NSA_BLOCK, width), F32),
        compiler_params=_params("parallel", "parallel"),
        name="pool_prompt",
    )(proj, pw_all)


def _paged_pool_kernel(pt_ref, *refs, pages):
    del pt_ref
    page_refs = refs[:pages]
    pw_ref = refs[pages]
    o_ref = refs[pages + 1]
    rows = PAGE_SIZE * 4
    per_block = rows // 2 // SUBLANES
    for k in range(pages):
        prod = page_refs[k][...] * pw_ref[...]
        y = jnp.sum(prod.reshape(2, per_block, SUBLANES, NSA_HD), axis=1)
        y = y[:, 0:4, :] + y[:, 4:8, :]
        for s in range(4):
            o_ref[2 * k:2 * k + 2, s * NSA_HD:(s + 1) * NSA_HD] = y[:, s, :]


def _pool_paged(cache4, page_table, pw_all, layer, pages):
    nb, n_pages = page_table.shape
    steps = n_pages // pages
    rows = PAGE_SIZE * 4

    def page_spec(k):
        return pl.BlockSpec((None, None, rows, NSA_HD),
                            lambda b, s, pt: (layer, pt[b, s * pages + k], 0, 0))

    grid_spec = pltpu.PrefetchScalarGridSpec(
        num_scalar_prefetch=1,
        grid=(nb, steps),
        in_specs=[page_spec(k) for k in range(pages)]
        + [pl.BlockSpec((None, rows, NSA_HD), lambda b, s, pt: (layer, 0, 0))],
        out_specs=pl.BlockSpec((None, 2 * pages, 4 * NSA_HD), lambda b, s, pt: (b, s, 0)),
    )
    return pl.pallas_call(
        functools.partial(_paged_pool_kernel, pages=pages),
        grid_spec=grid_spec,
        out_shape=jax.ShapeDtypeStruct((nb, 2 * n_pages, 4 * NSA_HD), F32),
        compiler_params=_params("parallel", "arbitrary"),
        name="pool_paged",
    )(page_table, *([cache4] * pages), pw_all)


def _cmp_mlp_kernel(p_ref, w1_ref, w2_ref, o_ref):
    for s in range(4):
        kv = s // NSA_KV_HEADS
        h = _silu(_dot_hi(p_ref[:, s * NSA_HD:(s + 1) * NSA_HD], w1_ref[kv]))
        o_ref[:, s * NSA_HD:(s + 1) * NSA_HD] = _dot_hi(h, w2_ref[kv])


def _cmp_mlp(pooled, w1_all, w2_all, layer):
    nb, n, width = pooled.shape
    wspec = pl.BlockSpec((None, 2, NSA_HD, NSA_HD), lambda b: (layer, 0, 0, 0))
    return pl.pallas_call(
        _cmp_mlp_kernel,
        grid=(nb,),
        in_specs=[pl.BlockSpec((None, n, width), lambda b: (b, 0, 0)), wspec, wspec],
        out_specs=pl.BlockSpec((None, n, width), lambda b: (b, 0, 0)),
        out_shape=jax.ShapeDtypeStruct((nb, n, width), F32),
        compiler_params=_params("parallel"),
        name="cmp_mlp",
    )(pooled, w1_all, w2_all)


def _select_blocks(score, st_ref, nbp, n_causal):
    st_ref[...] = score.T
    halves = [st_ref[g * nbp:(g + 1) * nbp, :] for g in range(NSA_KV_HEADS)]
    bi = lax.broadcasted_iota(jnp.int32, halves[0].shape, 0)

    def body(j, ranks):
        tie = jnp.where(j < bi, 1, 0)
        out = []
        for g in range(NSA_KV_HEADS):
            other = st_ref[pl.ds(g * nbp + j, 1), :]
            out.append(ranks[g] + jnp.where(other > halves[g], 1, jnp.where(other == halves[g], tie, 0)))
        return tuple(out)

    ranks = lax.fori_loop(0, n_causal, body, tuple(jnp.zeros(bi.shape, jnp.int32) for _ in halves))
    keep = [jnp.where((ranks[g] < NSA_TOPN) & (halves[g] >= 0.0), 1.0, 0.0) for g in range(NSA_KV_HEADS)]
    return jnp.concatenate(keep, axis=0).T


def _cmp_attn_kernel(q_ref, ckv_ref, bias_ref, o_ref, sc_ref, *scratch, pos0, tq, nb, nbs, nbp, select):
    qi = pl.program_id(1)
    t = pos0 + qi * tq + lax.broadcasted_iota(jnp.int32, (tq, 1), 0)
    blk = lax.broadcasted_iota(jnp.int32, (1, nb), 1)
    valid = t >= blk * NSA_BLOCK + (NSA_BLOCK - 1)
    bi = lax.broadcasted_iota(jnp.int32, (1, nbp), 1)
    cur = t // NSA_BLOCK
    forced = (bi == 0) | (bi == cur) | (bi == cur - 1)
    causal = bi <= cur
    scores = []
    for g in range(NSA_KV_HEADS):
        ck = ckv_ref[:, g * NSA_HD:(g + 1) * NSA_HD]
        cv = ckv_ref[:, (NSA_KV_HEADS + g) * NSA_HD:(NSA_KV_HEADS + g + 1) * NSA_HD]
        imp = jnp.zeros((tq, nb), F32)
        for r in range(NSA_GROUP):
            h = g * NSA_GROUP + r
            logits = _dot_nt(q_ref[:, h * NSA_HD:(h + 1) * NSA_HD], ck) * SCALE + bias_ref[h]
            logits = jnp.where(valid, logits, NEG)
            mx = jnp.max(logits, axis=-1, keepdims=True)
            p = jnp.where(valid, jnp.exp(logits - mx), 0.0)
            p = p / jnp.maximum(jnp.sum(p, axis=-1, keepdims=True), 1e-30)
            o_ref[:, h * NSA_HD:(h + 1) * NSA_HD] = _dot(p, cv)
            imp = imp + p
        if nbp > nb:
            imp = jnp.concatenate([imp, jnp.zeros((tq, nbp - nb), F32)], axis=-1)
        score = jnp.where(causal, jnp.where(forced, NSA_GROUP + 1.0, imp), -1.0)
        scores.append(jnp.where(bi < nbs, score, -2.0))
    score = jnp.concatenate(scores, axis=-1)
    if select:
        n_causal = jnp.minimum((pos0 + (qi + 1) * tq - 1) // NSA_BLOCK + 1, nbs)
        sc_ref[...] = _select_blocks(score, scratch[0], nbp, n_causal)
    else:
        sc_ref[...] = score


def _cmp_attn(proj, ckv, bias_cmp, nb_batch, seq, pos0, tq, nbs, nbp, select):
    nq = seq // tq
    nb = ckv.shape[1]
    m = nb_batch * seq
    bias_tq = bias_cmp.shape[1] // nq
    return pl.pallas_call(
        functools.partial(_cmp_attn_kernel, pos0=pos0, tq=tq, nb=nb, nbs=nbs, nbp=nbp, select=select),
        scratch_shapes=[pltpu.VMEM((NSA_KV_HEADS * nbp, tq), F32)] if select else [],
        grid=(nb_batch, nq),
        in_specs=[pl.BlockSpec((tq, NSA_WIDTH), lambda b, i: (b * nq + i, C_NQ // NSA_WIDTH)),
                  pl.BlockSpec((None, nb, 2 * NSA_KV_W), lambda b, i: (b, 0, 0)),
                  pl.BlockSpec((NSA_HEADS, bias_tq, nb), lambda b, i: (0, i, 0))],
        out_specs=[pl.BlockSpec((tq, NSA_WIDTH), lambda b, i: (b * nq + i, 0)),
                   pl.BlockSpec((tq, NSA_KV_HEADS * nbp), lambda b, i: (b * nq + i, 0))],
        out_shape=[jax.ShapeDtypeStruct((m, NSA_WIDTH), F32),
                   jax.ShapeDtypeStruct((m, NSA_KV_HEADS * nbp), F32)],
        compiler_params=_params("parallel", "parallel"),
        name="cmp_attn",
    )(proj, ckv, bias_cmp)


def _topk_kernel(s_ref, sel_ref, idx_ref, val_ref, *, nbs):
    s = s_ref[...]
    bi = lax.broadcasted_iota(jnp.int32, s.shape, 0)

    def body(j, rank):
        other = s_ref[pl.ds(j, 1), :]
        beats = (other > s) | ((other == s) & (j < bi))
        return rank + beats.astype(jnp.int32)

    rank = lax.fori_loop(0, nbs, body, jnp.zeros(s.shape, jnp.int32))
    ok = s >= 0.0
    sel_ref[...] = jnp.where((rank < NSA_TOPN) & ok, 1.0, 0.0)
    for r in range(NSA_TOPN):
        hit = rank == r
        idx_ref[r:r + 1, :] = jnp.sum(jnp.where(hit, bi, 0), axis=0, keepdims=True)
        val_ref[r:r + 1, :] = jnp.sum(jnp.where(hit & ok, 1, 0), axis=0, keepdims=True)


def _topk(score_t, nbs):
    rows, cols = score_t.shape
    tc = min(cols, 512)
    return pl.pallas_call(
        functools.partial(_topk_kernel, nbs=nbs),
        grid=(cols // tc,),
        in_specs=[pl.BlockSpec((rows, tc), lambda i: (0, i))],
        out_specs=[pl.BlockSpec((rows, tc), lambda i: (0, i)),
                   pl.BlockSpec((NSA_TOPN, tc), lambda i: (0, i)),
                   pl.BlockSpec((NSA_TOPN, tc), lambda i: (0, i))],
        out_shape=[jax.ShapeDtypeStruct((rows, cols), F32),
                   jax.ShapeDtypeStruct((NSA_TOPN, cols), jnp.int32),
                   jax.ShapeDtypeStruct((NSA_TOPN, cols), jnp.int32)],
        compiler_params=_params("parallel"),
        name="topk",
    )(score_t)


PQ = 256
PK = 512
BT = LANES


def _softmax_rows(s, ok):
    mx = jnp.max(s, axis=-1, keepdims=True)
    p = jnp.where(ok, jnp.exp(s - mx), 0.0)
    return p, jnp.sum(p, axis=-1, keepdims=True)


def _nsa_prompt_kernel(q_ref, ks_ref, vs_ref, kw_ref, vw_ref, sel_ref, bt_ref, ocmp_ref, z_ref, gate_ref,
                       o_ref, s2_ref, p2_ref, sw_ref, pw_ref, m_ref, l_ref, acc_ref, *, n_bt, wsub):
    qi = pl.program_id(2)
    rows = NSA_GROUP * PQ
    qsub = PQ // BT
    ksub = PK // BT
    q4 = jnp.concatenate([q_ref[:, r * NSA_HD:(r + 1) * NSA_HD] for r in range(NSA_GROUP)], axis=0)
    q4 = (q4 * (SCALE * LOG2E)).astype(BF16)
    t = qi * PQ + lax.broadcasted_iota(jnp.int32, (PQ, 1), 0)
    nsel = sel_ref.shape[1]
    blk0 = pl.program_id(1) * (nsel // NSA_KV_HEADS)
    unselected = jnp.where(sel_ref[...] > 0.5, 0.0, NEG).astype(BF16)
    qx = jnp.concatenate([q4, jnp.concatenate([unselected] * NSA_GROUP, axis=0)], axis=1)

    def row_groups():
        for r in range(NSA_GROUP):
            for a in range(qsub):
                yield slice(r * PQ + a * BT, r * PQ + (a + 1) * BT), r, a

    def softmax_block(s_ref, p_ref, rs, r, tile_of, extra, width, m_old):
        zmax = None
        for c in range(width):
            cs = slice(c * BT, (c + 1) * BT)
            z = s_ref[rs, cs] + bt_ref[r, jnp.clip(tile_of(c), 0, N_BIAS_TILES - 1)]
            if extra is not None:
                z = z + extra[:, cs]
            s_ref[rs, cs] = z
            zmax = z if zmax is None else jnp.maximum(zmax, z)
        mx = jnp.broadcast_to(jnp.max(zmax, axis=-1, keepdims=True), (BT, BT))
        if m_old is not None:
            mx = jnp.maximum(m_old, mx)
        psum = None
        for c in range(width):
            cs = slice(c * BT, (c + 1) * BT)
            p = jnp.exp2(s_ref[rs, cs] - mx)
            p_ref[rs, cs] = p.astype(BF16)
            psum = p if psum is None else psum + p
        return mx, psum

    m_ref[...] = jnp.full((rows, BT), NEG, F32)
    l_ref[...] = jnp.zeros((rows, BT), F32)
    acc_ref[...] = jnp.zeros((rows, NSA_HD), F32)

    def key_tile(kt, diagonal, slot):
        s_ref, p_ref = s2_ref.at[slot], p2_ref.at[slot]
        k0 = pl.multiple_of(kt * PK, PK)
        member = (lax.broadcasted_iota(jnp.int32, (PK, nsel), 1)
                  == blk0 + kt * (PK // NSA_BLOCK) + lax.broadcasted_iota(jnp.int32, (PK, nsel), 0) // NSA_BLOCK)
        kx = jnp.concatenate([ks_ref[pl.ds(k0, PK), :].astype(BF16), jnp.where(member, 1.0, 0.0).astype(BF16)],
                             axis=1)
        s_ref[...] = lax.dot_general(qx, kx, (((1,), (1,)), ((), ())), preferred_element_type=F32)
        future = None
        if diagonal:
            future = jnp.where(k0 + lax.broadcasted_iota(jnp.int32, (1, PK), 1) <= t, 0.0, NEG)
        for rs, r, a in row_groups():
            m_old = m_ref[rs]
            mx, psum = softmax_block(s_ref, p_ref, rs, r, lambda c: qi * qsub + a - (kt * ksub + c),
                                     None if future is None else future[a * BT:(a + 1) * BT], ksub, m_old)
            alpha = jnp.exp2(m_old - mx)
            l_ref[rs] = alpha * l_ref[rs] + psum
            acc_ref[rs] = alpha * acc_ref[rs]
            m_ref[rs] = mx
        acc_ref[...] += jnp.dot(p_ref[...], vs_ref[pl.ds(k0, PK), :].astype(BF16), preferred_element_type=F32)

    last = (qi * PQ) // PK

    def body(j, carry):
        key_tile(2 * j, False, 0)
        key_tile(2 * j + 1, False, 1)
        return carry

    lax.fori_loop(0, last // 2, body, 0)

    @pl.when(last % 2 == 1)
    def _():
        key_tile(last - 1, False, 0)

    key_tile(last, True, 1)
    o_slc = acc_ref[...] / jnp.maximum(jnp.sum(l_ref[...], axis=-1, keepdims=True), 1e-30)

    wk = wsub * BT
    kst = jnp.clip(qi * qsub - WINDOW // BT, 0, n_bt - wsub)
    k0 = pl.multiple_of(kst * BT, BT)
    sw_ref[...] = lax.dot_general(q4, kw_ref[pl.ds(k0, wk), :].astype(BF16), (((1,), (1,)), ((), ())),
                                  preferred_element_type=F32)
    dist = t - (k0 + lax.broadcasted_iota(jnp.int32, (1, wk), 1))
    outside = jnp.where((dist >= 0) & (dist <= WINDOW), 0.0, NEG)
    dens = []
    for rs, r, a in row_groups():
        _, psum = softmax_block(sw_ref, pw_ref, rs, r, lambda c: qi * qsub + a - (kst + c),
                                outside[a * BT:(a + 1) * BT], wsub, None)
        dens.append(jnp.sum(psum, axis=-1, keepdims=True))
    pv = jnp.dot(pw_ref[...], vw_ref[pl.ds(k0, wk), :].astype(BF16), preferred_element_type=F32)
    o_win = pv / jnp.maximum(jnp.concatenate(dens, axis=0), 1e-30)

    gates = jax.nn.sigmoid(gate_ref[...])
    for r in range(NSA_GROUP):
        sl = slice(r * NSA_HD, (r + 1) * NSA_HD)
        rs = slice(r * PQ, (r + 1) * PQ)
        o = (gates[:, 3 * r:3 * r + 1] * ocmp_ref[:, sl] + gates[:, 3 * r + 1:3 * r + 2] * o_slc[rs]
             + gates[:, 3 * r + 2:3 * r + 3] * o_win[rs])
        o_ref[:, sl] = o * _silu(z_ref[:, sl])


def _nsa_prompt(proj, sel, bias_tiles, o_cmp, nb_batch, seq):
    nq = seq // PQ
    n_bt = seq // BT
    wsub = min((WINDOW + PQ) // BT, n_bt)
    gw = NSA_GROUP * NSA_HD
    rows = NSA_GROUP * PQ
    nsel = sel.shape[-1]

    def kv_spec(col):
        return pl.BlockSpec((seq, NSA_HD), lambda b, g, i: (b, col // NSA_HD + g))

    return pl.pallas_call(
        functools.partial(_nsa_prompt_kernel, n_bt=n_bt, wsub=wsub),
        grid=(nb_batch, NSA_KV_HEADS, nq),
        in_specs=[pl.BlockSpec((PQ, gw), lambda b, g, i: (b * nq + i, C_NQ // gw + g)),
                  kv_spec(C_SLC), kv_spec(C_SLC + NSA_KV_W), kv_spec(C_WIN), kv_spec(C_WIN + NSA_KV_W),
                  pl.BlockSpec((PQ, nsel), lambda b, g, i: (b * nq + i, 0)),
                  pl.BlockSpec((NSA_GROUP, N_BIAS_TILES, BT, BT), lambda b, g, i: (g, 0, 0, 0)),
                  pl.BlockSpec((PQ, gw), lambda b, g, i: (b * nq + i, g)),
                  pl.BlockSpec((PQ, gw), lambda b, g, i: (b * nq + i, C_NZ // gw + g)),
                  pl.BlockSpec((PQ, LANES), lambda b, g, i: (b * nq + i, C_G0 // LANES + g))],
        out_specs=pl.BlockSpec((PQ, gw), lambda b, g, i: (b * nq + i, g)),
        out_shape=jax.ShapeDtypeStruct((nb_batch * seq, NSA_WIDTH), F32),
        scratch_shapes=[pltpu.VMEM((2, rows, PK), F32), pltpu.VMEM((2, rows, PK), BF16),
                        pltpu.VMEM((rows, wsub * BT), F32), pltpu.VMEM((rows, wsub * BT), BF16),
                        pltpu.VMEM((rows, BT), F32), pltpu.VMEM((rows, BT), F32), pltpu.VMEM((rows, NSA_HD), F32)],
        compiler_params=_params("parallel", "parallel", "arbitrary"),
        name="nsa_prompt",
    )(proj, proj, proj, proj, proj, sel, bias_tiles, o_cmp, proj, proj)


def _bias_by_threshold(dist, tcols):
    bias = jnp.broadcast_to(tcols[:, 0:1], (tcols.shape[0], dist.shape[-1]))
    for b in range(1, NUM_BUCKETS):
        bias = jnp.where(dist >= _BUCKET_THR[b - 1], tcols[:, b:b + 1], bias)
    return bias


SLOT_CHUNKS = NSA_TOPN + 2


def _slc_sample_kernel(pt_ref, idx_ref, val_ref, *refs, past, lq):
    del pt_ref
    n_in = NSA_KV_HEADS * NSA_TOPN
    page_refs = refs[:n_in]
    q_ref, knew_ref, tcol_ref, o_ref, kc_ref, vc_ref = refs[n_in:]
    b = pl.program_id(0)
    qi = pl.program_id(1)
    t = past + qi
    n_past_blocks = past // NSA_BLOCK
    nk = SLOT_CHUNKS * NSA_BLOCK
    lane = lax.broadcasted_iota(jnp.int32, (1, nk), 1)
    slot = lane // NSA_BLOCK
    this_q = lax.broadcasted_iota(jnp.int32, (lq, 1), 0) == qi
    for g in range(NSA_KV_HEADS):
        base = ((b * lq + qi) * NSA_KV_HEADS + g) * NSA_TOPN
        blk_of_lane = jnp.where(slot == NSA_TOPN, n_past_blocks, n_past_blocks + 1)
        use_of_lane = jnp.where(slot == NSA_TOPN, 1, 0)
        for j in range(NSA_TOPN):
            n_j = idx_ref[base + j]
            use = jnp.where((val_ref[base + j] > 0) & (n_j < n_past_blocks), 1, 0)
            blk_of_lane = jnp.where(slot == j, n_j, blk_of_lane)
            use_of_lane = jnp.where(slot == j, use, use_of_lane)
            page = page_refs[g * NSA_TOPN + j]
            kc_ref[j * NSA_BLOCK:(j + 1) * NSA_BLOCK, :] = page[pl.ds(g, NSA_BLOCK, stride=4), :]
            vc_ref[j * NSA_BLOCK:(j + 1) * NSA_BLOCK, :] = page[pl.ds(NSA_KV_HEADS + g, NSA_BLOCK, stride=4), :]
        new0 = NSA_TOPN * NSA_BLOCK
        kc_ref[new0:, :] = jnp.zeros((nk - new0, NSA_HD), F32)
        vc_ref[new0:, :] = jnp.zeros((nk - new0, NSA_HD), F32)
        kc_ref[new0:new0 + lq, :] = knew_ref[:, g * NSA_HD:(g + 1) * NSA_HD]
        vc_ref[new0:new0 + lq, :] = knew_ref[:, (NSA_KV_HEADS + g) * NSA_HD:(NSA_KV_HEADS + g + 1) * NSA_HD]
        q4 = jnp.concatenate(
            [jnp.sum(jnp.where(this_q, q_ref[:, (g * NSA_GROUP + r) * NSA_HD:(g * NSA_GROUP + r + 1) * NSA_HD], 0.0),
                     axis=0, keepdims=True)
             for r in range(NSA_GROUP)] + [jnp.zeros((SUBLANES - NSA_GROUP, NSA_HD), F32)], axis=0)
        dist = t - (blk_of_lane * NSA_BLOCK + lane % NSA_BLOCK)
        ok = (use_of_lane > 0) & (dist >= 0)
        s = _dot_nt(q4, kc_ref[...]) * SCALE + _bias_by_threshold(dist, tcol_ref[g])
        s = jnp.where(ok, s, NEG)
        p, den = _softmax_rows(s, ok)
        o = _dot(p, vc_ref[...]) / jnp.maximum(den, 1e-30)
        for r in range(NSA_GROUP):
            h = g * NSA_GROUP + r
            o_ref[:, h * NSA_HD:(h + 1) * NSA_HD] = o[r:r + 1, :]


def _slc_sample(proj, cache4, page_table, idx_flat, val_flat, tcols, layer, nb_batch, lq, past):
    rows = NSA_BLOCK * 4
    n_past_blocks = past // NSA_BLOCK

    n_sel = NSA_KV_HEADS * NSA_TOPN
    blocks = jnp.minimum(idx_flat.reshape(nb_batch, lq * n_sel), n_past_blocks - 1)
    pages = jnp.take_along_axis(page_table, blocks // 2, axis=1)
    loc = jnp.concatenate([pages.reshape(nb_batch * lq, n_sel), (blocks % 2).reshape(nb_batch * lq, n_sel)], axis=1)

    def page_spec(g, j):
        def index_map(b, i, loc, idx, val):
            row = b * lq + i
            return (layer, loc[row, g * NSA_TOPN + j], loc[row, n_sel + g * NSA_TOPN + j], 0)
        return pl.BlockSpec((None, None, rows, NSA_HD), index_map)

    grid_spec = pltpu.PrefetchScalarGridSpec(
        num_scalar_prefetch=3,
        grid=(nb_batch, lq),
        in_specs=[page_spec(g, j) for g in range(NSA_KV_HEADS) for j in range(NSA_TOPN)]
        + [pl.BlockSpec((lq, NSA_WIDTH), lambda b, i, *_: (b, C_NQ // NSA_WIDTH)),
           pl.BlockSpec((lq, 2 * NSA_KV_W), lambda b, i, *_: (b, C_SLC // (2 * NSA_KV_W))),
           pl.BlockSpec((NSA_KV_HEADS, SUBLANES, NUM_BUCKETS), lambda b, i, *_: (0, 0, 0))],
        out_specs=pl.BlockSpec((None, 1, NSA_WIDTH), lambda b, i, *_: (b * lq + i, 0, 0)),
        scratch_shapes=[pltpu.VMEM((SLOT_CHUNKS * NSA_BLOCK, NSA_HD), F32),
                        pltpu.VMEM((SLOT_CHUNKS * NSA_BLOCK, NSA_HD), F32)],
    )
    out = pl.pallas_call(
        functools.partial(_slc_sample_kernel, past=past, lq=lq),
        grid_spec=grid_spec,
        out_shape=jax.ShapeDtypeStruct((nb_batch * lq, 1, NSA_WIDTH), F32),
        compiler_params=_params("parallel", "arbitrary"),
        name="slc_sample",
    )(loc, idx_flat, val_flat, *([cache4] * n_sel), proj, proj, tcols)
    return out.reshape(nb_batch * lq, NSA_WIDTH)


def _win_sample_kernel(q_ref, wpast_ref, wnew_ref, tcol_ref, ocmp_ref, oslc_ref, z_ref, g0_ref, g1_ref,
                       o_ref, kc_ref, vc_ref, *, past, lq, wlen):
    rows = NSA_GROUP * lq
    nk = kc_ref.shape[0]
    lane = lax.broadcasted_iota(jnp.int32, (1, nk), 1)
    kpos = past - wlen + lane
    t = past + lax.broadcasted_iota(jnp.int32, (rows, 1), 0) % lq
    dist = t - kpos
    ok = (dist >= 0) & (dist <= WINDOW) & (kpos >= 0) & (lane < wlen + lq)
    gate_refs = (g0_ref, g1_ref)
    for g in range(NSA_KV_HEADS):
        kc_ref[0:wlen, :] = wpast_ref[pl.ds(g, wlen, stride=4), :]
        vc_ref[0:wlen, :] = wpast_ref[pl.ds(NSA_KV_HEADS + g, wlen, stride=4), :]
        kc_ref[wlen:, :] = jnp.zeros((nk - wlen, NSA_HD), F32)
        vc_ref[wlen:, :] = jnp.zeros((nk - wlen, NSA_HD), F32)
        kc_ref[wlen:wlen + lq, :] = wnew_ref[:, g * NSA_HD:(g + 1) * NSA_HD]
        vc_ref[wlen:wlen + lq, :] = wnew_ref[:, (NSA_KV_HEADS + g) * NSA_HD:(NSA_KV_HEADS + g + 1) * NSA_HD]
        q4 = jnp.concatenate([q_ref[:, (g * NSA_GROUP + r) * NSA_HD:(g * NSA_GROUP + r + 1) * NSA_HD]
                              for r in range(NSA_GROUP)], axis=0)
        s = _dot_nt(q4, kc_ref[...]) * SCALE + _bias_by_threshold(dist, tcol_ref[g])
        s = jnp.where(ok, s, NEG)
        p, den = _softmax_rows(s, ok)
        o_win = _dot(p, vc_ref[...]) / jnp.maximum(den, 1e-30)
        gates = jax.nn.sigmoid(gate_refs[g][...])
        for r in range(NSA_GROUP):
            h = g * NSA_GROUP + r
            sl = slice(h * NSA_HD, (h + 1) * NSA_HD)
            o = (gates[:, 3 * r:3 * r + 1] * ocmp_ref[:, sl] + gates[:, 3 * r + 1:3 * r + 2] * oslc_ref[:, sl]
                 + gates[:, 3 * r + 2:3 * r + 3] * o_win[r * lq:(r + 1) * lq])
            o_ref[:, sl] = o * _silu(z_ref[:, sl])


def _win_sample(proj, win4, tcols_rows, o_cmp, o_slc, layer, nb_batch, lq, past):
    wlen = win4.shape[2] // 4
    nk = -(-(wlen + lq) // LANES) * LANES
    row_spec = pl.BlockSpec((lq, NSA_WIDTH), lambda b: (b, 0))
    return pl.pallas_call(
        functools.partial(_win_sample_kernel, past=past, lq=lq, wlen=wlen),
        grid=(nb_batch,),
        in_specs=[pl.BlockSpec((lq, NSA_WIDTH), lambda b: (b, C_NQ // NSA_WIDTH)),
                  pl.BlockSpec((None, None, wlen * 4, NSA_HD), lambda b: (layer, b, 0, 0)),
                  pl.BlockSpec((lq, 2 * NSA_KV_W), lambda b: (b, C_WIN // (2 * NSA_KV_W))),
                  pl.BlockSpec((NSA_KV_HEADS, NSA_GROUP * lq, NUM_BUCKETS), lambda b: (0, 0, 0)),
                  row_spec, row_spec,
                  pl.BlockSpec((lq, NSA_WIDTH), lambda b: (b, C_NZ // NSA_WIDTH)),
                  pl.BlockSpec((lq, LANES), lambda b: (b, C_G0 // LANES)),
                  pl.BlockSpec((lq, LANES), lambda b: (b, C_G1 // LANES))],
        out_specs=row_spec,
        out_shape=jax.ShapeDtypeStruct((nb_batch * lq, NSA_WIDTH), F32),
        scratch_shapes=[pltpu.VMEM((nk, NSA_HD), F32), pltpu.VMEM((nk, NSA_HD), F32)],
        compiler_params=_params("parallel"),
        name="win_sample",
    )(proj, win4, proj, tcols_rows, o_cmp, o_slc, proj, proj, proj)


def _out_kernel(dn_ref, nsa_ref, x_ref, p_ref, wo_ref, gpost_ref, wpp_ref, gple_ref, wpg_ref, o_ref):
    half = dn_ref.shape[1]
    mixed = (jnp.dot(dn_ref[...].astype(BF16), wo_ref[0:half, :], preferred_element_type=F32)
             + jnp.dot(nsa_ref[...].astype(BF16), wo_ref[half:, :], preferred_element_type=F32))
    y = mixed * lax.rsqrt(jnp.mean(mixed * mixed, axis=-1, keepdims=True) + EPS) * gpost_ref[...]
    x1 = x_ref[...] + y
    e = jnp.dot(p_ref[...].astype(BF16), wpp_ref[...], preferred_element_type=F32)
    e = e * lax.rsqrt(jnp.mean(e * e, axis=-1, keepdims=True) + EPS) * gple_ref[...]
    gate = jax.nn.sigmoid(jnp.dot(x1.astype(BF16), wpg_ref[...], preferred_element_type=F32))
    o_ref[...] = x1 + gate * e


def _out_proj(dn, nsa, x, p_all, wo_all, gpost_all, wpp_all, gple_all, wpg_all, layer, tm):
    m = x.shape[0]
    once = pl.Buffered(1)

    def wspec(k):
        return pl.BlockSpec((None, k, D_MODEL), lambda i: (layer, 0, 0), pipeline_mode=once)

    return pl.pallas_call(
        _out_kernel,
        grid=(m // tm,),
        in_specs=[pl.BlockSpec((tm, dn.shape[1]), lambda i: (i, 0)),
                  pl.BlockSpec((tm, nsa.shape[1]), lambda i: (i, 0)),
                  pl.BlockSpec((tm, D_MODEL), lambda i: (i, 0)),
                  pl.BlockSpec((None, tm, PLE_DIM), lambda i: (layer, i, 0)),
                  wspec(D_MODEL), wspec(1), wspec(PLE_DIM), wspec(1), wspec(D_MODEL)],
        out_specs=pl.BlockSpec((tm, D_MODEL), lambda i: (i, 0)),
        out_shape=jax.ShapeDtypeStruct((m, D_MODEL), F32),
        compiler_params=_params("parallel"),
        name="out_proj",
    )(dn, nsa, x, p_all, wo_all, gpost_all, wpp_all, gple_all, wpg_all)


def _kv_rows_kernel(x_ref, cmp_ref, slc_ref, win_ref):
    tm = x_ref.shape[0]
    width = 2 * NSA_KV_W

    def rows(o_ref, seg):
        for s in range(4):
            o_ref[pl.ds(s, tm, stride=4), :] = x_ref[:, seg * width + s * NSA_HD:seg * width + (s + 1) * NSA_HD]

    rows(cmp_ref, 0)
    rows(slc_ref, 1)

    @pl.when(pl.program_id(1) == pl.num_programs(1) - 1)
    def _():
        rows(win_ref, 2)


def _kv_rows(proj, nb, seq):
    tm = min(WINDOW, seq)
    nr = seq // tm
    width = 3 * 2 * NSA_KV_W
    return pl.pallas_call(
        _kv_rows_kernel,
        grid=(nb, nr),
        in_specs=[pl.BlockSpec((tm, width), lambda b, r: (b * nr + r, C_CMP // width))],
        out_specs=[pl.BlockSpec((tm * 4, NSA_HD), lambda b, r: (b * nr + r, 0)),
                   pl.BlockSpec((tm * 4, NSA_HD), lambda b, r: (b * nr + r, 0)),
                   pl.BlockSpec((None, tm * 4, NSA_HD), lambda b, r: (b, 0, 0))],
        out_shape=[jax.ShapeDtypeStruct((nb * seq * 4, NSA_HD), F32),
                   jax.ShapeDtypeStruct((nb * seq * 4, NSA_HD), F32),
                   jax.ShapeDtypeStruct((nb, tm * 4, NSA_HD), F32)],
        compiler_params=_params("parallel", "arbitrary"),
        name="kv_rows",
    )(proj)


W_TILE = 512


def _relayout_w_in_kernel(wt_ref, sm_ref, gate_ref, o_ref, tail_ref):
    is_tail = pl.program_id(1) == pl.num_programs(1) - 1

    @pl.when(jnp.logical_not(is_tail))
    def _():
        o_ref[...] = wt_ref[0].T.astype(BF16)

    @pl.when(is_tail)
    def _():
        half = 3 * NSA_GROUP
        tail_ref[...] = jnp.zeros(tail_ref.shape, F32)
        tail_ref[0:2 * DN_HEADS, :] = sm_ref[0]
        tail_ref[C_G0 - C_SM:C_G0 - C_SM + half, :] = gate_ref[0, 0:half, :]
        tail_ref[C_G1 - C_SM:C_G1 - C_SM + half, :] = gate_ref[0, half:2 * half, :]
        o_ref[...] = tail_ref[...].T.astype(BF16)


def _relayout_w_in(w_in):
    depth, k, _ = w_in.shape
    wt = jnp.swapaxes(w_in, 1, 2)
    n_gate = 2 * 3 * NSA_GROUP
    n_tiles = N_PAD // W_TILE
    moved = C_NQ // W_TILE

    def src_row(d, j):
        j = jnp.minimum(j, n_tiles - 2)
        return (d, pl.multiple_of(j * W_TILE + jnp.where(j >= moved, _O_NQ - C_NQ, 0), SUBLANES), 0)

    return pl.pallas_call(
        _relayout_w_in_kernel,
        grid=(depth, n_tiles),
        in_specs=[pl.BlockSpec((pl.Element(1), pl.Element(W_TILE), pl.Element(k)), src_row),
                  pl.BlockSpec((pl.Element(1), pl.Element(2 * DN_HEADS), pl.Element(k)), lambda d, j: (d, _O_BETA, 0)),
                  pl.BlockSpec((pl.Element(1), pl.Element(n_gate), pl.Element(k)), lambda d, j: (d, _O_GATE, 0))],
        out_specs=pl.BlockSpec((None, k, W_TILE), lambda d, j: (d, 0, j)),
        out_shape=jax.ShapeDtypeStruct((depth, k, N_PAD), BF16),
        scratch_shapes=[pltpu.VMEM((W_TILE, k), F32)],
        compiler_params=_params("parallel", "arbitrary"),
        name="relayout_w_in",
    )(wt, wt, wt)


def _bias_expand_kernel(t_ref, bucket_ref, o_ref):
    bucket = bucket_ref[...]
    for h in range(NSA_HEADS):
        acc = jnp.full(bucket.shape, t_ref[0, h], F32)
        for b in range(1, NUM_BUCKETS):
            acc = jnp.where(bucket >= b, t_ref[b, h], acc)
        o_ref[h] = acc


def _bias_expand(table, bucket):
    rows, cols = bucket.shape
    tr = max(t for t in range(SUBLANES, min(rows, 512) + 1, SUBLANES) if rows % t == 0)
    return pl.pallas_call(
        _bias_expand_kernel,
        grid=(rows // tr,),
        in_specs=[pl.BlockSpec(memory_space=pltpu.SMEM), pl.BlockSpec((tr, cols), lambda i: (i, 0))],
        out_specs=pl.BlockSpec((NSA_HEADS, tr, cols), lambda i: (0, i, 0)),
        out_shape=jax.ShapeDtypeStruct((NSA_HEADS, rows, cols), F32),
        compiler_params=_params("parallel"),
        name="bias_expand",
    )(table, jnp.asarray(bucket))


def _bias_tiles(table):
    i = np.arange(BT)
    d = np.arange(N_BIAS_TILES)[:, None, None] * BT + i[None, :, None] - i[None, None, :]
    tiles = _bias_expand(table * LOG2E, _bucket_np(d).reshape(N_BIAS_TILES * BT, BT))
    return tiles.reshape(NSA_HEADS, N_BIAS_TILES, BT, BT)


def _bias_cmp(table, pos0, lq, nb):
    dist = (pos0 + np.arange(lq))[:, None] - (np.arange(nb) * NSA_BLOCK + NSA_BLOCK - 1)[None, :]
    return _bias_expand(table, _bucket_np(dist))


def kernel(x_prompt, x_sample, state_dn_S, state_dn_conv, cache_win_kv, cache_cmp_kv, cache_slc_kv, page_table,
           p_prompt, p_sample, rel_bias_table, w_in, w_out, g_pre, g_post, dn_conv_w, dn_A_log, dn_dt_bias,
           dn_norm_w, cmp_pos_w, cmp_w1, cmp_w2, w_ple_proj, g_ple, w_ple_gate):
    depth = w_in.shape[0]
    bp, seq, _ = x_prompt.shape
    bs, lq, _ = x_sample.shape
    n_pages = page_table.shape[1]
    past = n_pages * PAGE_SIZE
    n_pool = cache_cmp_kv.shape[1]
    wlen = cache_win_kv.shape[2]
    mp, ms = bp * seq, bs * lq
    assert seq % PK == 0 and seq % DN_CHUNK == 0 and lq == SUBLANES and past % NSA_BLOCK == 0
    kv_row = (2, NSA_KV_HEADS, NSA_HD)

    w_in_r = _relayout_w_in(w_in)
    w_out_b = w_out.astype(BF16)
    wpp_b = w_ple_proj.astype(BF16)
    wpg_b = w_ple_gate.astype(BF16)
    g_pre3 = g_pre[:, None, :]
    g_post3 = g_post[:, None, :]
    g_ple3 = g_ple[:, None, :]
    nw3 = dn_norm_w[:, None, :]
    lane_pad = ((0, 0), (DN_HEADS, LANES - 2 * DN_HEADS))
    dn_par = jnp.stack([jnp.pad(dn_A_log, lane_pad), jnp.pad(dn_dt_bias, lane_pad)], axis=1)
    pw_prompt = jnp.broadcast_to(cmp_pos_w[:, :, None, :, None], (depth, 2, NSA_KV_HEADS, NSA_BLOCK, NSA_HD))
    pw_prompt = jnp.transpose(pw_prompt, (0, 3, 1, 2, 4)).reshape(depth, NSA_BLOCK, 2 * NSA_KV_W)
    pw_page = jnp.broadcast_to(cmp_pos_w[:, None, None, :, :, None],
                               (depth, PAGE_SIZE // NSA_BLOCK, NSA_KV_HEADS, 2, NSA_BLOCK, NSA_HD))
    pw_page = jnp.transpose(pw_page, (0, 1, 4, 3, 2, 5)).reshape(depth, PAGE_SIZE * 4, NSA_HD)
    table = rel_bias_table.astype(F32)
    bias_tiles = _bias_tiles(table)
    nb_p = seq // NSA_BLOCK
    nb_s = past // NSA_BLOCK
    nbs_s = -(-(past + lq) // NSA_BLOCK)
    nbp_s = -(-nbs_s // LANES) * LANES
    bias_cmp_p = _bias_cmp(table, 0, seq, nb_p)
    bias_cmp_s = _bias_cmp(table, past, lq, nb_s)
    tcols = jnp.pad(table.T.reshape(NSA_KV_HEADS, NSA_GROUP, NUM_BUCKETS),
                    ((0, 0), (0, SUBLANES - NSA_GROUP), (0, 0)))
    tcols_rows = jnp.repeat(table.T.reshape(NSA_KV_HEADS, NSA_GROUP, NUM_BUCKETS), lq, axis=1)

    conv0_s = jnp.pad(state_dn_conv, ((0, 0), (0, 0), (SUBLANES - (CONV_W - 1), 0), (0, 0)))
    conv0_p = jnp.zeros((1, bp, SUBLANES, DN_CONV_CH), F32)
    s0_p = jnp.zeros((1, bp, DN_HEADS, DN_DK, DN_DV), F32)
    cmp4 = cache_cmp_kv.reshape(depth, n_pool, PAGE_SIZE * 4, NSA_HD)
    slc4 = cache_slc_kv.reshape(depth, n_pool, PAGE_SIZE * 4, NSA_HD)
    win4 = cache_win_kv.reshape(depth, bs, wlen * 4, NSA_HD)
    p_prompt3 = p_prompt.reshape(depth, mp, PLE_DIM)
    p_sample3 = p_sample.reshape(depth, ms, PLE_DIM)

    xp = x_prompt.reshape(mp, D_MODEL)
    xs = x_sample.reshape(ms, D_MODEL)
    outs_p = [[] for _ in range(2)]
    outs_s = [[] for _ in range(5)]
    kv_rows_p = [[] for _ in range(3)]
    tq_p = min(256, seq)
    for i in range(depth):
        proj = _in_proj(xp, g_pre3, w_in_r, i, min(1024, mp))
        dn_o, conv_n, s_n = _deltanet(proj, bp, seq, DN_CHUNK, conv0_p, s0_p, 0, dn_conv_w, dn_par, nw3, i)
        ckv = _cmp_mlp(_pool_prompt(proj, bp, seq, pw_prompt, i), cmp_w1, cmp_w2, i)
        o_cmp, sel = _cmp_attn(proj, ckv, bias_cmp_p, bp, seq, 0, tq_p, nb_p, nb_p, True)
        nsa_o = _nsa_prompt(proj, sel, bias_tiles, o_cmp, bp, seq)
        xp = _out_proj(dn_o, nsa_o, xp, p_prompt3, w_out_b, g_post3, wpp_b, g_ple3, wpg_b, i, min(256, mp))
        for dst, rows in zip(kv_rows_p, _kv_rows(proj, bp, seq)):
            dst.append(rows)
        outs_p[0].append(s_n)
        outs_p[1].append(conv_n[:, SUBLANES - (CONV_W - 1):])

        proj = _in_proj(xs, g_pre3, w_in_r, i, ms)
        dn_o, conv_n, s_n = _deltanet(proj, bs, lq, lq, conv0_s, state_dn_S, i, dn_conv_w, dn_par, nw3, i)
        pooled = _pool_paged(cmp4, page_table, pw_page, i, min(16, n_pages))
        ckv = _cmp_mlp(pooled, cmp_w1, cmp_w2, i)
        o_cmp, score = _cmp_attn(proj, ckv, bias_cmp_s, bs, lq, past, lq, nbs_s, nbp_s, False)
        score_t = jnp.transpose(score.reshape(ms * NSA_KV_HEADS, nbp_s))
        _, idx_t, val_t = _topk(score_t, nbs_s)
        o_slc = _slc_sample(proj, slc4, page_table, idx_t.T.reshape(-1), val_t.T.reshape(-1), tcols, i,
                            bs, lq, past)
        nsa_o = _win_sample(proj, win4, tcols_rows, o_cmp, o_slc, i, bs, lq, past)
        xs = _out_proj(dn_o, nsa_o, xs, p_sample3, w_out_b, g_post3, wpp_b, g_ple3, wpg_b, i, ms)
        kv_all = proj[:, C_CMP:C_WIN + 2 * NSA_KV_W].reshape(bs, lq, 3, *kv_row)
        win_all = jnp.concatenate([cache_win_kv[i], kv_all[:, :, 2]], axis=1)
        outs_s[0].append(s_n)
        outs_s[1].append(conv_n[:, SUBLANES - (CONV_W - 1):])
        outs_s[2].append(win_all[:, win_all.shape[1] - min(WINDOW, past + lq):])
        outs_s[3].append(kv_all[:, :, 0])
        outs_s[4].append(kv_all[:, :, 1])

    p_cmp, p_slc, p_win = (jnp.stack(o) for o in kv_rows_p)
    return ((xp.reshape(bp, seq, D_MODEL), xs.reshape(bs, lq, D_MODEL), jnp.stack(outs_p[0]), jnp.stack(outs_p[1]),
             p_win.reshape(depth, bp, min(WINDOW, seq), *kv_row), p_cmp.reshape(depth, bp, seq, *kv_row),
             p_slc.reshape(depth, bp, seq, *kv_row)) + tuple(jnp.stack(o) for o in outs_s))
```

```python
import functools
import math

import numpy as np
import jax
import jax.numpy as jnp
from jax import lax
from jax.experimental import pallas as pl
from jax.experimental.pallas import tpu as pltpu

F32 = jnp.float32
BF16 = jnp.bfloat16

D_MODEL = 2048
DN_HEADS = 8
DN_DK = 128
DN_DV = 128
DN_QK_W = DN_HEADS * DN_DK
DN_CONV_CH = 2 * DN_QK_W + DN_HEADS * DN_DV
CONV_W = 4
DN_CHUNK = 64
NSA_HEADS = 8
NSA_KV_HEADS = 2
NSA_GROUP = 4
NSA_HD = 128
NSA_WIDTH = NSA_HEADS * NSA_HD
NSA_KV_W = NSA_KV_HEADS * NSA_HD
NSA_BLOCK = 64
NSA_TOPN = 16
WINDOW = 512
SCALE = NSA_HD ** -0.5
NUM_BUCKETS = 32
REL_MAX_DIST = 1024
PLE_DIM = 256
PAGE_SIZE = 128
EPS = 1e-6
NEG = -1e30
LOG2E = math.log2(math.e)

VMEM_LIMIT_BYTES = 56 * 1024 * 1024
LANES = 128
SUBLANES = 8

C_QKV = 0
C_Z = 3072
C_NQ = 4096
C_NZ = 5120
C_CMP = 6144
C_SLC = 6656
C_WIN = 7168
C_SM = 7680
C_G0 = 7808
C_G1 = 7936
N_PAD = 8192
_O_BETA = DN_CONV_CH + DN_HEADS * DN_DV
_O_NQ = _O_BETA + 2 * DN_HEADS
_O_GATE = _O_NQ + 2 * NSA_WIDTH + 3 * 2 * NSA_KV_W
N_BIAS_TILES = 9


def _bucket_np(dist):
    dist = np.maximum(np.asarray(dist, np.int64), 0)
    exact = NUM_BUCKETS // 2
    scaled = np.log(np.maximum(dist, 1).astype(np.float64) / exact) / math.log(REL_MAX_DIST / exact)
    large = np.minimum(exact + (scaled * (NUM_BUCKETS - exact)).astype(np.int64), NUM_BUCKETS - 1)
    return np.where(dist < exact, dist, large).astype(np.int32)


_BUCKET_THR = tuple(int(np.argmax(_bucket_np(np.arange(2 * REL_MAX_DIST)) >= b)) for b in range(1, NUM_BUCKETS))


def _dot(a, b):
    return jnp.dot(a.astype(BF16), b.astype(BF16), preferred_element_type=F32)


def _dot_nt(a, b):
    return lax.dot_general(a.astype(BF16), b.astype(BF16), (((1,), (1,)), ((), ())), preferred_element_type=F32)


def _dot_tn(a, b):
    return lax.dot_general(a.astype(BF16), b.astype(BF16), (((0,), (0,)), ((), ())), preferred_element_type=F32)


def _dot_hi(a, b):
    return jnp.dot(a, b, preferred_element_type=F32, precision=lax.Precision.HIGHEST)


def _dot_hi_nt(a, b):
    return lax.dot_general(a, b, (((1,), (1,)), ((), ())), preferred_element_type=F32,
                           precision=lax.Precision.HIGHEST)


def _silu(x):
    return x * jax.nn.sigmoid(x)


def _params(*sem):
    return pltpu.CompilerParams(dimension_semantics=sem, vmem_limit_bytes=VMEM_LIMIT_BYTES)


def _in_proj_kernel(x_ref, g_ref, w_ref, o_ref, xn_ref):
    @pl.when(pl.program_id(1) == 0)
    def _():
        x = x_ref[...]
        y = x * lax.rsqrt(jnp.mean(x * x, axis=-1, keepdims=True) + EPS)
        xn_ref[...] = (y * g_ref[...]).astype(BF16)

    o_ref[...] = jnp.dot(xn_ref[...], w_ref[...], preferred_element_type=F32)


def _in_proj(x, g_all, w_all, layer, tm):
    m = x.shape[0]
    tn = 1024
    return pl.pallas_call(
        _in_proj_kernel,
        grid=(m // tm, N_PAD // tn),
        in_specs=[pl.BlockSpec((tm, D_MODEL), lambda i, j: (i, 0)),
                  pl.BlockSpec((None, 1, D_MODEL), lambda i, j: (layer, 0, 0)),
                  pl.BlockSpec((None, D_MODEL, tn), lambda i, j: (layer, 0, j))],
        out_specs=pl.BlockSpec((tm, tn), lambda i, j: (i, j)),
        out_shape=jax.ShapeDtypeStruct((m, N_PAD), F32),
        scratch_shapes=[pltpu.VMEM((tm, D_MODEL), BF16)],
        compiler_params=_params("parallel", "arbitrary"),
        name="in_proj",
    )(x, g_all, w_all)


def _dn_prep_kernel(qkv_ref, sm_ref, conv0_ref, cw_ref, par_ref,
                    u_ref, w_ref, qd_ref, kd_ref, qk_ref, gl_ref, convn_ref, xs_ref, *, chunk, cs):
    tb = chunk * cs
    halo = SUBLANES

    @pl.when(pl.program_id(1) == 0)
    def _():
        xs_ref[0:halo, :] = conv0_ref[...]

    xs_ref[halo:halo + tb, :] = qkv_ref[...]
    base = halo - (CONV_W - 1)
    y = xs_ref[base:base + tb, :] * cw_ref[0:1, :]
    for j in range(1, CONV_W):
        y = y + xs_ref[base + j:base + j + tb, :] * cw_ref[j:j + 1, :]
    y = _silu(y)
    tail = xs_ref[tb:tb + halo, :]
    xs_ref[0:halo, :] = tail
    convn_ref[...] = tail

    sm = sm_ref[...]
    beta_all = jax.nn.sigmoid(sm)
    xg = sm + par_ref[1:2, :]
    softplus = jnp.maximum(xg, 0.0) + jnp.log1p(jnp.exp(-jnp.abs(xg)))
    g_all = -jnp.exp(par_ref[0:1, :]) * softplus
    row = lax.broadcasted_iota(jnp.int32, (tb, tb), 0)
    col = lax.broadcasted_iota(jnp.int32, (tb, tb), 1)
    in_chunk_lower = (row >= col) & (row // chunk == col // chunk)
    gc_all = _dot_hi(jnp.where(in_chunk_lower, 1.0, 0.0), g_all)
    pick = (lax.broadcasted_iota(jnp.int32, (DN_HEADS, LANES), 1)
            == lax.broadcasted_iota(jnp.int32, (DN_HEADS, LANES), 0) + DN_HEADS).astype(F32)
    gc_rows = _dot_hi_nt(pick, gc_all)
    for c in range(cs):
        gl_ref[c] = gc_all[(c + 1) * chunk - 1:(c + 1) * chunk, :]
    r64 = lax.broadcasted_iota(jnp.int32, (chunk, chunk), 0)
    c64 = lax.broadcasted_iota(jnp.int32, (chunk, chunk), 1)
    lower = r64 >= c64
    strict = r64 > c64

    lhs, kns, rhs, decays = [], [], [], []
    for h in range(DN_HEADS):
        hc = slice(h * DN_DV, (h + 1) * DN_DV)
        qh = y[:, h * DN_DK:(h + 1) * DN_DK]
        kh = y[:, DN_QK_W + h * DN_DK:DN_QK_W + (h + 1) * DN_DK]
        vh = y[:, 2 * DN_QK_W + h * DN_DV:2 * DN_QK_W + (h + 1) * DN_DV]
        qn = qh * lax.rsqrt(jnp.sum(qh * qh, axis=-1, keepdims=True) + EPS) * (DN_DK ** -0.5)
        kn = kh * lax.rsqrt(jnp.sum(kh * kh, axis=-1, keepdims=True) + EPS)
        beta = beta_all[:, h:h + 1]
        gc = gc_all[:, DN_HEADS + h:DN_HEADS + h + 1]
        egc = jnp.exp(gc)
        kb = kn * beta
        rhs_h = jnp.concatenate([vh * beta, kb * egc], axis=-1)
        qd_ref[:, hc] = (qn * egc).astype(qd_ref.dtype)
        for c in range(cs):
            rs = slice(c * chunk, (c + 1) * chunk)
            gcc = gc[rs]
            decays.append(jnp.exp(jnp.where(lower, gcc - gc_rows[h:h + 1, rs], NEG)))
            lhs.append(jnp.concatenate([kb[rs], qn[rs]], axis=0))
            kns.append(kn[rs])
            rhs.append(rhs_h[rs])
            kd_ref[rs, hc] = (kn[rs] * jnp.exp(gcc[chunk - 1:chunk, :] - gcc)).astype(kd_ref.dtype)
    decay = jnp.stack(decays)
    both = jnp.einsum('nid,njd->nij', jnp.stack(lhs).astype(BF16), jnp.stack(kns).astype(BF16),
                      preferred_element_type=F32)
    a = jnp.where(strict[None], both[:, :chunk] * decay, 0.0)
    qk = both[:, chunk:] * decay

    def bmm(p, q):
        return jnp.einsum('nij,njk->nik', p, q, preferred_element_type=F32)

    def split(v):
        hi = v.astype(BF16)
        return hi, (v - hi.astype(F32)).astype(BF16)

    def bmm3(p, q):
        (ph, pl_), (qh, ql) = p, q
        return bmm(ph, qh) + (bmm(ph, ql) + bmm(pl_, qh))

    pw_s = split(a)
    inv = jnp.where(r64 == c64, 1.0, 0.0)[None] - a
    for _ in range(int(math.log2(chunk)) - 1):
        pw_s = split(bmm3(pw_s, pw_s))
        inv = inv + bmm3(pw_s, split(inv))
    x = bmm3(split(inv), split(jnp.stack(rhs)))
    for h in range(DN_HEADS):
        hc = slice(h * DN_DV, (h + 1) * DN_DV)
        for c in range(cs):
            rs = slice(c * chunk, (c + 1) * chunk)
            n = h * cs + c
            u_ref[rs, hc] = x[n, :, :DN_DV]
            w_ref[rs, hc] = x[n, :, DN_DV:].astype(w_ref.dtype)
            qk_ref[rs, h * chunk:(h + 1) * chunk] = qk[n]


def _dn_scan_kernel(u_ref, w_ref, qd_ref, kd_ref, qk_ref, gl_ref, z_ref, s0_ref, nw_ref,
                    o_ref, sout_ref, s_ref, *, chunk, cs):
    @pl.when(pl.program_id(1) == 0)
    def _():
        s_ref[...] = s0_ref[...]

    def heads(ref, rs, width):
        return jnp.stack([ref[rs, h * width:(h + 1) * width] for h in range(DN_HEADS)])

    def bdot(spec, p, q):
        return jnp.einsum(spec, p.astype(BF16), q.astype(BF16), preferred_element_type=F32)

    for c in range(cs):
        rs = slice(c * chunk, (c + 1) * chunk)
        decay_last = jnp.exp(gl_ref[c])
        decay_last = jnp.stack([decay_last[:, DN_HEADS + h:DN_HEADS + h + 1] for h in range(DN_HEADS)])
        s = s_ref[...]
        sb = s.astype(BF16)
        v_new = heads(u_ref, rs, DN_DV) - bdot('hcd,hde->hce', heads(w_ref, rs, DN_DK), sb)
        o = (bdot('hcd,hde->hce', heads(qd_ref, rs, DN_DK), sb)
             + bdot('hcs,hse->hce', heads(qk_ref, rs, chunk), v_new))
        s_ref[...] = s * decay_last + bdot('hcd,hce->hde', heads(kd_ref, rs, DN_DK), v_new)
        on = o * lax.rsqrt(jnp.mean(o * o, axis=-1, keepdims=True) + EPS) * nw_ref[...]
        for h in range(DN_HEADS):
            hc = slice(h * DN_DV, (h + 1) * DN_DV)
            o_ref[rs, hc] = on[h] * _silu(z_ref[rs, hc])

    @pl.when(pl.program_id(1) == pl.num_programs(1) - 1)
    def _():
        sout_ref[...] = s_ref[...]


def _deltanet(proj, nb, seq, chunk, conv0_all, s0_all, state_layer, cw_all, par_all, nw_all, layer):
    m = nb * seq
    nchunks = seq // chunk
    width = DN_HEADS * DN_DV
    hs = (DN_HEADS, DN_DK, DN_DV)
    cs_a = 4 if nchunks % 4 == 0 else 1
    na = nchunks // cs_a
    ta = cs_a * chunk
    op_dtype = BF16 if chunk % (2 * SUBLANES) == 0 else F32
    u, w, qd, kd, qk, gl, conv_n = pl.pallas_call(
        functools.partial(_dn_prep_kernel, chunk=chunk, cs=cs_a),
        grid=(nb, na),
        in_specs=[pl.BlockSpec((ta, DN_CONV_CH), lambda b, c: (b * na + c, C_QKV // DN_CONV_CH)),
                  pl.BlockSpec((ta, LANES), lambda b, c: (b * na + c, C_SM // LANES)),
                  pl.BlockSpec((None, None, SUBLANES, DN_CONV_CH), lambda b, c: (state_layer, b, 0, 0)),
                  pl.BlockSpec((None, CONV_W, DN_CONV_CH), lambda b, c: (layer, 0, 0)),
                  pl.BlockSpec((None, 2, LANES), lambda b, c: (layer, 0, 0))],
        out_specs=[pl.BlockSpec((ta, width), lambda b, c: (b * na + c, 0)),
                   pl.BlockSpec((ta, width), lambda b, c: (b * na + c, 0)),
                   pl.BlockSpec((ta, width), lambda b, c: (b * na + c, 0)),
                   pl.BlockSpec((ta, width), lambda b, c: (b * na + c, 0)),
                   pl.BlockSpec((ta, DN_HEADS * chunk), lambda b, c: (b * na + c, 0)),
                   pl.BlockSpec((cs_a, 1, LANES), lambda b, c: (b * na + c, 0, 0)),
                   pl.BlockSpec((None, SUBLANES, DN_CONV_CH), lambda b, c: (b, 0, 0))],
        out_shape=[jax.ShapeDtypeStruct((m, width), F32),
                   jax.ShapeDtypeStruct((m, width), op_dtype),
                   jax.ShapeDtypeStruct((m, width), op_dtype),
                   jax.ShapeDtypeStruct((m, width), op_dtype),
                   jax.ShapeDtypeStruct((m, DN_HEADS * chunk), F32),
                   jax.ShapeDtypeStruct((nb * nchunks, 1, LANES), F32),
                   jax.ShapeDtypeStruct((nb, SUBLANES, DN_CONV_CH), F32)],
        scratch_shapes=[pltpu.VMEM((ta + SUBLANES, DN_CONV_CH), F32)],
        compiler_params=_params("parallel", "arbitrary"),
        name="dn_prep",
    )(proj, proj, conv0_all, cw_all, par_all)

    cs_b = 4 if nchunks % 4 == 0 else 1
    nbk = nchunks // cs_b
    tbk = cs_b * chunk
    row_spec = pl.BlockSpec((tbk, width), lambda b, c: (b * nbk + c, 0))
    o, s_n = pl.pallas_call(
        functools.partial(_dn_scan_kernel, chunk=chunk, cs=cs_b),
        grid=(nb, nbk),
        in_specs=[row_spec, row_spec, row_spec, row_spec,
                  pl.BlockSpec((tbk, DN_HEADS * chunk), lambda b, c: (b * nbk + c, 0)),
                  pl.BlockSpec((cs_b, 1, LANES), lambda b, c: (b * nbk + c, 0, 0)),
                  pl.BlockSpec((tbk, width), lambda b, c: (b * nbk + c, C_Z // width)),
                  pl.BlockSpec((None, None) + hs, lambda b, c: (state_layer, b, 0, 0, 0)),
                  pl.BlockSpec((None, 1, DN_DV), lambda b, c: (layer, 0, 0))],
        out_specs=[row_spec, pl.BlockSpec((None,) + hs, lambda b, c: (b, 0, 0, 0))],
        out_shape=[jax.ShapeDtypeStruct((m, width), F32), jax.ShapeDtypeStruct((nb,) + hs, F32)],
        scratch_shapes=[pltpu.VMEM(hs, F32)],
        compiler_params=_params("parallel", "arbitrary"),
        name="dn_scan",
    )(u, w, qd, kd, qk, gl, proj, s0_all, nw_all)
    return o, conv_n, s_n


def _pool_kernel(x_ref, pw_ref, o_ref):
    rows = x_ref.shape[0]
    x = x_ref[...].reshape(rows // NSA_BLOCK, NSA_BLOCK, 2 * NSA_KV_W)
    o_ref[...] = jnp.sum(x * pw_ref[...][None], axis=1)


def _pool_prompt(proj, nb, seq, pw_all, layer):
    rows = min(seq, 1024)
    nr = seq // rows
    width = 2 * NSA_KV_W
    return pl.pallas_call(
        _pool_kernel,
        grid=(nb, nr),
        in_specs=[pl.BlockSpec((rows, width), lambda b, r: (b * nr + r, C_CMP // width)),
                  pl.BlockSpec((None, NSA_BLOCK, width), lambda b, r: (layer, 0, 0))],
        out_specs=pl.BlockSpec((None, rows // NSA_BLOCK, width), lambda b, r: (b, r, 0)),
        out_shape=jax.ShapeDtypeStruct((nb, seq // NSA_BLOCK, width), F32),
        compiler_params=_params("parallel", "parallel"),
        name="pool_prompt",
    )(proj, pw_all)


def _paged_pool_kernel(pt_ref, *refs, pages):
    del pt_ref
    page_refs = refs[:pages]
    pw_ref = refs[pages]
    o_ref = refs[pages + 1]
    rows = PAGE_SIZE * 4
    per_block = rows // 2 // SUBLANES
    for k in range(pages):
        prod = page_refs[k][...] * pw_ref[...]
        y = jnp.sum(prod.reshape(2, per_block, SUBLANES, NSA_HD), axis=1)
        y = y[:, 0:4, :] + y[:, 4:8, :]
        for s in range(4):
            o_ref[2 * k:2 * k + 2, s * NSA_HD:(s + 1) * NSA_HD] = y[:, s, :]


def _pool_paged(cache4, page_table, pw_all, layer, pages):
    nb, n_pages = page_table.shape
    steps = n_pages // pages
    rows = PAGE_SIZE * 4

    def page_spec(k):
        return pl.BlockSpec((None, None, rows, NSA_HD),
                            lambda b, s, pt: (layer, pt[b, s * pages + k], 0, 0))

    grid_spec = pltpu.PrefetchScalarGridSpec(
        num_scalar_prefetch=1,
        grid=(nb, steps),
        in_specs=[page_spec(k) for k in range(pages)]
        + [pl.BlockSpec((None, rows, NSA_HD), lambda b, s, pt: (layer, 0, 0))],
        out_specs=pl.BlockSpec((None, 2 * pages, 4 * NSA_HD), lambda b, s, pt: (b, s, 0)),
    )
    return pl.pallas_call(
        functools.partial(_paged_pool_kernel, pages=pages),
        grid_spec=grid_spec,
        out_shape=jax.ShapeDtypeStruct((nb, 2 * n_pages, 4 * NSA_HD), F32),
        compiler_params=_params("parallel", "arbitrary"),
        name="pool_paged",
    )(page_table, *([cache4] * pages), pw_all)


def _cmp_mlp_kernel(p_ref, w1_ref, w2_ref, o_ref):
    for s in range(4):
        kv = s // NSA_KV_HEADS
        h = _silu(_dot_hi(p_ref[:, s * NSA_HD:(s + 1) * NSA_HD], w1_ref[kv]))
        o_ref[:, s * NSA_HD:(s + 1) * NSA_HD] = _dot_hi(h, w2_ref[kv])


def _cmp_mlp(pooled, w1_all, w2_all, layer):
    nb, n, width = pooled.shape
    wspec = pl.BlockSpec((None, 2, NSA_HD, NSA_HD), lambda b: (layer, 0, 0, 0))
    return pl.pallas_call(
        _cmp_mlp_kernel,
        grid=(nb,),
        in_specs=[pl.BlockSpec((None, n, width), lambda b: (b, 0, 0)), wspec, wspec],
        out_specs=pl.BlockSpec((None, n, width), lambda b: (b, 0, 0)),
        out_shape=jax.ShapeDtypeStruct((nb, n, width), F32),
        compiler_params=_params("parallel"),
        name="cmp_mlp",
    )(pooled, w1_all, w2_all)


def _select_blocks(score, st_ref, nbp, n_causal):
    st_ref[...] = score.T
    halves = [st_ref[g * nbp:(g + 1) * nbp, :] for g in range(NSA_KV_HEADS)]
    bi = lax.broadcasted_iota(jnp.int32, halves[0].shape, 0)

    def body(j, ranks):
        tie = jnp.where(j < bi, 1, 0)
        out = []
        for g in range(NSA_KV_HEADS):
            other = st_ref[pl.ds(g * nbp + j, 1), :]
            out.append(ranks[g] + jnp.where(other > halves[g], 1, jnp.where(other == halves[g], tie, 0)))
        return tuple(out)

    ranks = lax.fori_loop(0, n_causal, body, tuple(jnp.zeros(bi.shape, jnp.int32) for _ in halves))
    keep = [jnp.where((ranks[g] < NSA_TOPN) & (halves[g] >= 0.0), 1.0, 0.0) for g in range(NSA_KV_HEADS)]
    return jnp.concatenate(keep, axis=0).T


def _cmp_attn_kernel(q_ref, ckv_ref, bias_ref, o_ref, sc_ref, *scratch, pos0, tq, nb, nbs, nbp, select):
    qi = pl.program_id(1)
    t = pos0 + qi * tq + lax.broadcasted_iota(jnp.int32, (tq, 1), 0)
    blk = lax.broadcasted_iota(jnp.int32, (1, nb), 1)
    valid = t >= blk * NSA_BLOCK + (NSA_BLOCK - 1)
    bi = lax.broadcasted_iota(jnp.int32, (1, nbp), 1)
    cur = t // NSA_BLOCK
    forced = (bi == 0) | (bi == cur) | (bi == cur - 1)
    causal = bi <= cur
    scores = []
    for g in range(NSA_KV_HEADS):
        ck = ckv_ref[:, g * NSA_HD:(g + 1) * NSA_HD]
        cv = ckv_ref[:, (NSA_KV_HEADS + g) * NSA_HD:(NSA_KV_HEADS + g + 1) * NSA_HD]
        imp = jnp.zeros((tq, nb), F32)
        for r in range(NSA_GROUP):
            h = g * NSA_GROUP + r
            logits = _dot_nt(q_ref[:, h * NSA_HD:(h + 1) * NSA_HD], ck) * SCALE + bias_ref[h]
            logits = jnp.where(valid, logits, NEG)
            mx = jnp.max(logits, axis=-1, keepdims=True)
            p = jnp.where(valid, jnp.exp(logits - mx), 0.0)
            p = p / jnp.maximum(jnp.sum(p, axis=-1, keepdims=True), 1e-30)
            o_ref[:, h * NSA_HD:(h + 1) * NSA_HD] = _dot(p, cv)
            imp = imp + p
        if nbp > nb:
            imp = jnp.concatenate([imp, jnp.zeros((tq, nbp - nb), F32)], axis=-1)
        score = jnp.where(causal, jnp.where(forced, NSA_GROUP + 1.0, imp), -1.0)
        scores.append(jnp.where(bi < nbs, score, -2.0))
    score = jnp.concatenate(scores, axis=-1)
    if select:
        n_causal = jnp.minimum((pos0 + (qi + 1) * tq - 1) // NSA_BLOCK + 1, nbs)
        sc_ref[...] = _select_blocks(score, scratch[0], nbp, n_causal)
    else:
        sc_ref[...] = score


def _cmp_attn(proj, ckv, bias_cmp, nb_batch, seq, pos0, tq, nbs, nbp, select):
    nq = seq // tq
    nb = ckv.shape[1]
    m = nb_batch * seq
    bias_tq = bias_cmp.shape[1] // nq
    return pl.pallas_call(
        functools.partial(_cmp_attn_kernel, pos0=pos0, tq=tq, nb=nb, nbs=nbs, nbp=nbp, select=select),
        scratch_shapes=[pltpu.VMEM((NSA_KV_HEADS * nbp, tq), F32)] if select else [],
        grid=(nb_batch, nq),
        in_specs=[pl.BlockSpec((tq, NSA_WIDTH), lambda b, i: (b * nq + i, C_NQ // NSA_WIDTH)),
                  pl.BlockSpec((None, nb, 2 * NSA_KV_W), lambda b, i: (b, 0, 0)),
                  pl.BlockSpec((NSA_HEADS, bias_tq, nb), lambda b, i: (0, i, 0))],
        out_specs=[pl.BlockSpec((tq, NSA_WIDTH), lambda b, i: (b * nq + i, 0)),
                   pl.BlockSpec((tq, NSA_KV_HEADS * nbp), lambda b, i: (b * nq + i, 0))],
        out_shape=[jax.ShapeDtypeStruct((m, NSA_WIDTH), F32),
                   jax.ShapeDtypeStruct((m, NSA_KV_HEADS * nbp), F32)],
        compiler_params=_params("parallel", "parallel"),
        name="cmp_attn",
    )(proj, ckv, bias_cmp)


def _topk_kernel(s_ref, sel_ref, idx_ref, val_ref, *, nbs):
    s = s_ref[...]
    bi = lax.broadcasted_iota(jnp.int32, s.shape, 0)

    def body(j, rank):
        other = s_ref[pl.ds(j, 1), :]
        beats = (other > s) | ((other == s) & (j < bi))
        return rank + beats.astype(jnp.int32)

    rank = lax.fori_loop(0, nbs, body, jnp.zeros(s.shape, jnp.int32))
    ok = s >= 0.0
    sel_ref[...] = jnp.where((rank < NSA_TOPN) & ok, 1.0, 0.0)
    for r in range(NSA_TOPN):
        hit = rank == r
        idx_ref[r:r + 1, :] = jnp.sum(jnp.where(hit, bi, 0), axis=0, keepdims=True)
        val_ref[r:r + 1, :] = jnp.sum(jnp.where(hit & ok, 1, 0), axis=0, keepdims=True)


def _topk(score_t, nbs):
    rows, cols = score_t.shape
    tc = min(cols, 512)
    return pl.pallas_call(
        functools.partial(_topk_kernel, nbs=nbs),
        grid=(cols // tc,),
        in_specs=[pl.BlockSpec((rows, tc), lambda i: (0, i))],
        out_specs=[pl.BlockSpec((rows, tc), lambda i: (0, i)),
                   pl.BlockSpec((NSA_TOPN, tc), lambda i: (0, i)),
                   pl.BlockSpec((NSA_TOPN, tc), lambda i: (0, i))],
        out_shape=[jax.ShapeDtypeStruct((rows, cols), F32),
                   jax.ShapeDtypeStruct((NSA_TOPN, cols), jnp.int32),
                   jax.ShapeDtypeStruct((NSA_TOPN, cols), jnp.int32)],
        compiler_params=_params("parallel"),
        name="topk",
    )(score_t)


PQ = 256
PK = 512
BT = LANES


def _softmax_rows(s, ok):
    mx = jnp.max(s, axis=-1, keepdims=True)
    p = jnp.where(ok, jnp.exp(s - mx), 0.0)
    return p, jnp.sum(p, axis=-1, keepdims=True)


def _nsa_prompt_kernel(q_ref, ks_ref, vs_ref, kw_ref, vw_ref, sel_ref, bt_ref, ocmp_ref, z_ref, gate_ref,
                       o_ref, s2_ref, p2_ref, sw_ref, pw_ref, m_ref, acc_ref, *, n_bt, wsub):
    qi = pl.program_id(2)
    rows = NSA_GROUP * PQ
    qsub = PQ // BT
    ksub = PK // BT
    q4 = jnp.concatenate([q_ref[:, r * NSA_HD:(r + 1) * NSA_HD] for r in range(NSA_GROUP)], axis=0)
    q4 = (q4 * (SCALE * LOG2E)).astype(BF16)
    t = qi * PQ + lax.broadcasted_iota(jnp.int32, (PQ, 1), 0)
    nsel = sel_ref.shape[1]
    blk0 = pl.program_id(1) * (nsel // NSA_KV_HEADS)
    unselected = jnp.where(sel_ref[...] > 0.5, 0.0, NEG).astype(BF16)
    qx = jnp.concatenate([q4, jnp.concatenate([unselected] * NSA_GROUP, axis=0)], axis=1)

    def row_groups():
        for r in range(NSA_GROUP):
            for a in range(qsub):
                yield slice(r * PQ + a * BT, r * PQ + (a + 1) * BT), r, a

    def softmax_block(s_ref, p_ref, rs, r, tile_of, extra, width, m_old):
        zmax = None
        for c in range(width):
            cs = slice(c * BT, (c + 1) * BT)
            z = s_ref[rs, cs] + bt_ref[r, jnp.clip(tile_of(c), 0, N_BIAS_TILES - 1)]
            if extra is not None:
                z = z + extra[:, cs]
            s_ref[rs, cs] = z
            zmax = z if zmax is None else jnp.maximum(zmax, z)
        mx = jnp.broadcast_to(jnp.max(zmax, axis=-1, keepdims=True), (BT, BT))
        if m_old is not None:
            mx = jnp.maximum(m_old, mx)
        for c in range(width):
            cs = slice(c * BT, (c + 1) * BT)
            p_ref[rs, cs] = jnp.exp2(s_ref[rs, cs] - mx).astype(BF16)
        return mx

    def with_ones(v):
        return jnp.concatenate([v.astype(BF16), jnp.ones(v.shape, BF16)], axis=1)

    m_ref[...] = jnp.full((rows, BT), NEG, F32)
    acc_ref[...] = jnp.zeros((rows, 2 * NSA_HD), F32)

    def key_tile(kt, diagonal, slot):
        s_ref, p_ref = s2_ref.at[slot], p2_ref.at[slot]
        k0 = pl.multiple_of(kt * PK, PK)
        member = (lax.broadcasted_iota(jnp.int32, (PK, nsel), 1)
                  == blk0 + kt * (PK // NSA_BLOCK) + lax.broadcasted_iota(jnp.int32, (PK, nsel), 0) // NSA_BLOCK)
        kx = jnp.concatenate([ks_ref[pl.ds(k0, PK), :].astype(BF16), jnp.where(member, 1.0, 0.0).astype(BF16)],
                             axis=1)
        s_ref[...] = lax.dot_general(qx, kx, (((1,), (1,)), ((), ())), preferred_element_type=F32)
        future = None
        if diagonal:
            future = jnp.where(k0 + lax.broadcasted_iota(jnp.int32, (1, PK), 1) <= t, 0.0, NEG)
        for rs, r, a in row_groups():
            m_old = m_ref[rs]
            mx = softmax_block(s_ref, p_ref, rs, r, lambda c: qi * qsub + a - (kt * ksub + c),
                               None if future is None else future[a * BT:(a + 1) * BT], ksub, m_old)
            alpha = jnp.exp2(m_old - mx)
            acc_ref[rs] = jnp.concatenate([alpha, alpha], axis=1) * acc_ref[rs]
            m_ref[rs] = mx
        acc_ref[...] += jnp.dot(p_ref[...], with_ones(vs_ref[pl.ds(k0, PK), :]), preferred_element_type=F32)

    last = (qi * PQ) // PK

    def body(j, carry):
        key_tile(2 * j, False, 0)
        key_tile(2 * j + 1, False, 1)
        return carry

    lax.fori_loop(0, last // 2, body, 0)

    @pl.when(last % 2 == 1)
    def _():
        key_tile(last - 1, False, 0)

    key_tile(last, True, 1)
    o_slc = acc_ref[:, :NSA_HD] / jnp.maximum(acc_ref[:, NSA_HD:], 1e-30)

    wk = wsub * BT
    kst = jnp.clip(qi * qsub - WINDOW // BT, 0, n_bt - wsub)
    k0 = pl.multiple_of(kst * BT, BT)
    sw_ref[...] = lax.dot_general(q4, kw_ref[pl.ds(k0, wk), :].astype(BF16), (((1,), (1,)), ((), ())),
                                  preferred_element_type=F32)
    dist = t - (k0 + lax.broadcasted_iota(jnp.int32, (1, wk), 1))
    outside = jnp.where((dist >= 0) & (dist <= WINDOW), 0.0, NEG)
    for rs, r, a in row_groups():
        softmax_block(sw_ref, pw_ref, rs, r, lambda c: qi * qsub + a - (kst + c),
                      outside[a * BT:(a + 1) * BT], wsub, None)
    pv = jnp.dot(pw_ref[...], with_ones(vw_ref[pl.ds(k0, wk), :]), preferred_element_type=F32)
    o_win = pv[:, :NSA_HD] / jnp.maximum(pv[:, NSA_HD:], 1e-30)

    gates = jax.nn.sigmoid(gate_ref[...])
    for r in range(NSA_GROUP):
        sl = slice(r * NSA_HD, (r + 1) * NSA_HD)
        rs = slice(r * PQ, (r + 1) * PQ)
        o = (gates[:, 3 * r:3 * r + 1] * ocmp_ref[:, sl] + gates[:, 3 * r + 1:3 * r + 2] * o_slc[rs]
             + gates[:, 3 * r + 2:3 * r + 3] * o_win[rs])
        o_ref[:, sl] = o * _silu(z_ref[:, sl])


def _nsa_prompt(proj, sel, bias_tiles, o_cmp, nb_batch, seq):
    nq = seq // PQ
    n_bt = seq // BT
    wsub = min((WINDOW + PQ) // BT, n_bt)
    gw = NSA_GROUP * NSA_HD
    rows = NSA_GROUP * PQ
    nsel = sel.shape[-1]

    def kv_spec(col):
        return pl.BlockSpec((seq, NSA_HD), lambda b, g, i: (b, col // NSA_HD + g))

    return pl.pallas_call(
        functools.partial(_nsa_prompt_kernel, n_bt=n_bt, wsub=wsub),
        grid=(nb_batch, NSA_KV_HEADS, nq),
        in_specs=[pl.BlockSpec((PQ, gw), lambda b, g, i: (b * nq + i, C_NQ // gw + g)),
                  kv_spec(C_SLC), kv_spec(C_SLC + NSA_KV_W), kv_spec(C_WIN), kv_spec(C_WIN + NSA_KV_W),
                  pl.BlockSpec((PQ, nsel), lambda b, g, i: (b * nq + i, 0)),
                  pl.BlockSpec((NSA_GROUP, N_BIAS_TILES, BT, BT), lambda b, g, i: (g, 0, 0, 0)),
                  pl.BlockSpec((PQ, gw), lambda b, g, i: (b * nq + i, g)),
                  pl.BlockSpec((PQ, gw), lambda b, g, i: (b * nq + i, C_NZ // gw + g)),
                  pl.BlockSpec((PQ, LANES), lambda b, g, i: (b * nq + i, C_G0 // LANES + g))],
        out_specs=pl.BlockSpec((PQ, gw), lambda b, g, i: (b * nq + i, g)),
        out_shape=jax.ShapeDtypeStruct((nb_batch * seq, NSA_WIDTH), F32),
        scratch_shapes=[pltpu.VMEM((2, rows, PK), F32), pltpu.VMEM((2, rows, PK), BF16),
                        pltpu.VMEM((rows, wsub * BT), F32), pltpu.VMEM((rows, wsub * BT), BF16),
                        pltpu.VMEM((rows, BT), F32), pltpu.VMEM((rows, 2 * NSA_HD), F32)],
        compiler_params=_params("parallel", "parallel", "arbitrary"),
        name="nsa_prompt",
    )(proj, proj, proj, proj, proj, sel, bias_tiles, o_cmp, proj, proj)


def _bias_by_threshold(dist, tcols):
    bias = jnp.broadcast_to(tcols[:, 0:1], (tcols.shape[0], dist.shape[-1]))
    for b in range(1, NUM_BUCKETS):
        bias = jnp.where(dist >= _BUCKET_THR[b - 1], tcols[:, b:b + 1], bias)
    return bias


SLOT_CHUNKS = NSA_TOPN + 2


def _slc_sample_kernel(pt_ref, idx_ref, val_ref, *refs, past, lq):
    del pt_ref
    n_in = NSA_KV_HEADS * NSA_TOPN
    page_refs = refs[:n_in]
    q_ref, knew_ref, tcol_ref, o_ref, kc_ref, vc_ref = refs[n_in:]
    b = pl.program_id(0)
    qi = pl.program_id(1)
    t = past + qi
    n_past_blocks = past // NSA_BLOCK
    nk = SLOT_CHUNKS * NSA_BLOCK
    lane = lax.broadcasted_iota(jnp.int32, (1, nk), 1)
    slot = lane // NSA_BLOCK
    this_q = lax.broadcasted_iota(jnp.int32, (lq, 1), 0) == qi
    for g in range(NSA_KV_HEADS):
        base = ((b * lq + qi) * NSA_KV_HEADS + g) * NSA_TOPN
        blk_of_lane = jnp.where(slot == NSA_TOPN, n_past_blocks, n_past_blocks + 1)
        use_of_lane = jnp.where(slot == NSA_TOPN, 1, 0)
        for j in range(NSA_TOPN):
            n_j = idx_ref[base + j]
            use = jnp.where((val_ref[base + j] > 0) & (n_j < n_past_blocks), 1, 0)
            blk_of_lane = jnp.where(slot == j, n_j, blk_of_lane)
            use_of_lane = jnp.where(slot == j, use, use_of_lane)
            page = page_refs[g * NSA_TOPN + j]
            kc_ref[j * NSA_BLOCK:(j + 1) * NSA_BLOCK, :] = page[pl.ds(g, NSA_BLOCK, stride=4), :]
            vc_ref[j * NSA_BLOCK:(j + 1) * NSA_BLOCK, :] = page[pl.ds(NSA_KV_HEADS + g, NSA_BLOCK, stride=4), :]
        new0 = NSA_TOPN * NSA_BLOCK
        kc_ref[new0:, :] = jnp.zeros((nk - new0, NSA_HD), F32)
        vc_ref[new0:, :] = jnp.zeros((nk - new0, NSA_HD), F32)
        kc_ref[new0:new0 + lq, :] = knew_ref[:, g * NSA_HD:(g + 1) * NSA_HD]
        vc_ref[new0:new0 + lq, :] = knew_ref[:, (NSA_KV_HEADS + g) * NSA_HD:(NSA_KV_HEADS + g + 1) * NSA_HD]
        q4 = jnp.concatenate(
            [jnp.sum(jnp.where(this_q, q_ref[:, (g * NSA_GROUP + r) * NSA_HD:(g * NSA_GROUP + r + 1) * NSA_HD], 0.0),
                     axis=0, keepdims=True)
             for r in range(NSA_GROUP)] + [jnp.zeros((SUBLANES - NSA_GROUP, NSA_HD), F32)], axis=0)
        dist = t - (blk_of_lane * NSA_BLOCK + lane % NSA_BLOCK)
        ok = (use_of_lane > 0) & (dist >= 0)
        s = _dot_nt(q4, kc_ref[...]) * SCALE + _bias_by_threshold(dist, tcol_ref[g])
        s = jnp.where(ok, s, NEG)
        p, den = _softmax_rows(s, ok)
        o = _dot(p, vc_ref[...]) / jnp.maximum(den, 1e-30)
        for r in range(NSA_GROUP):
            h = g * NSA_GROUP + r
            o_ref[:, h * NSA_HD:(h + 1) * NSA_HD] = o[r:r + 1, :]


def _slc_sample(proj, cache4, page_table, idx_flat, val_flat, tcols, layer, nb_batch, lq, past):
    rows = NSA_BLOCK * 4
    n_past_blocks = past // NSA_BLOCK

    n_sel = NSA_KV_HEADS * NSA_TOPN
    blocks = jnp.minimum(idx_flat.reshape(nb_batch, lq * n_sel), n_past_blocks - 1)
    pages = jnp.take_along_axis(page_table, blocks // 2, axis=1)
    loc = jnp.concatenate([pages.reshape(nb_batch * lq, n_sel), (blocks % 2).reshape(nb_batch * lq, n_sel)], axis=1)

    def page_spec(g, j):
        def index_map(b, i, loc, idx, val):
            row = b * lq + i
            return (layer, loc[row, g * NSA_TOPN + j], loc[row, n_sel + g * NSA_TOPN + j], 0)
        return pl.BlockSpec((None, None, rows, NSA_HD), index_map)

    grid_spec = pltpu.PrefetchScalarGridSpec(
        num_scalar_prefetch=3,
        grid=(nb_batch, lq),
        in_specs=[page_spec(g, j) for g in range(NSA_KV_HEADS) for j in range(NSA_TOPN)]
        + [pl.BlockSpec((lq, NSA_WIDTH), lambda b, i, *_: (b, C_NQ // NSA_WIDTH)),
           pl.BlockSpec((lq, 2 * NSA_KV_W), lambda b, i, *_: (b, C_SLC // (2 * NSA_KV_W))),
           pl.BlockSpec((NSA_KV_HEADS, SUBLANES, NUM_BUCKETS), lambda b, i, *_: (0, 0, 0))],
        out_specs=pl.BlockSpec((None, 1, NSA_WIDTH), lambda b, i, *_: (b * lq + i, 0, 0)),
        scratch_shapes=[pltpu.VMEM((SLOT_CHUNKS * NSA_BLOCK, NSA_HD), F32),
                        pltpu.VMEM((SLOT_CHUNKS * NSA_BLOCK, NSA_HD), F32)],
    )
    out = pl.pallas_call(
        functools.partial(_slc_sample_kernel, past=past, lq=lq),
        grid_spec=grid_spec,
        out_shape=jax.ShapeDtypeStruct((nb_batch * lq, 1, NSA_WIDTH), F32),
        compiler_params=_params("parallel", "arbitrary"),
        name="slc_sample",
    )(loc, idx_flat, val_flat, *([cache4] * n_sel), proj, proj, tcols)
    return out.reshape(nb_batch * lq, NSA_WIDTH)


def _win_sample_kernel(q_ref, wpast_ref, wnew_ref, tcol_ref, ocmp_ref, oslc_ref, z_ref, g0_ref, g1_ref,
                       o_ref, kc_ref, vc_ref, *, past, lq, wlen):
    rows = NSA_GROUP * lq
    nk = kc_ref.shape[0]
    lane = lax.broadcasted_iota(jnp.int32, (1, nk), 1)
    kpos = past - wlen + lane
    t = past + lax.broadcasted_iota(jnp.int32, (rows, 1), 0) % lq
    dist = t - kpos
    ok = (dist >= 0) & (dist <= WINDOW) & (kpos >= 0) & (lane < wlen + lq)
    gate_refs = (g0_ref, g1_ref)
    for g in range(NSA_KV_HEADS):
        kc_ref[0:wlen, :] = wpast_ref[pl.ds(g, wlen, stride=4), :]
        vc_ref[0:wlen, :] = wpast_ref[pl.ds(NSA_KV_HEADS + g, wlen, stride=4), :]
        kc_ref[wlen:, :] = jnp.zeros((nk - wlen, NSA_HD), F32)
        vc_ref[wlen:, :] = jnp.zeros((nk - wlen, NSA_HD), F32)
        kc_ref[wlen:wlen + lq, :] = wnew_ref[:, g * NSA_HD:(g + 1) * NSA_HD]
        vc_ref[wlen:wlen + lq, :] = wnew_ref[:, (NSA_KV_HEADS + g) * NSA_HD:(NSA_KV_HEADS + g + 1) * NSA_HD]
        q4 = jnp.concatenate([q_ref[:, (g * NSA_GROUP + r) * NSA_HD:(g * NSA_GROUP + r + 1) * NSA_HD]
                              for r in range(NSA_GROUP)], axis=0)
        s = _dot_nt(q4, kc_ref[...]) * SCALE + _bias_by_threshold(dist, tcol_ref[g])
        s = jnp.where(ok, s, NEG)
        p, den = _softmax_rows(s, ok)
        o_win = _dot(p, vc_ref[...]) / jnp.maximum(den, 1e-30)
        gates = jax.nn.sigmoid(gate_refs[g][...])
        for r in range(NSA_GROUP):
            h = g * NSA_GROUP + r
            sl = slice(h * NSA_HD, (h + 1) * NSA_HD)
            o = (gates[:, 3 * r:3 * r + 1] * ocmp_ref[:, sl] + gates[:, 3 * r + 1:3 * r + 2] * oslc_ref[:, sl]
                 + gates[:, 3 * r + 2:3 * r + 3] * o_win[r * lq:(r + 1) * lq])
            o_ref[:, sl] = o * _silu(z_ref[:, sl])


def _win_sample(proj, win4, tcols_rows, o_cmp, o_slc, layer, nb_batch, lq, past):
    wlen = win4.shape[2] // 4
    nk = -(-(wlen + lq) // LANES) * LANES
    row_spec = pl.BlockSpec((lq, NSA_WIDTH), lambda b: (b, 0))
    return pl.pallas_call(
        functools.partial(_win_sample_kernel, past=past, lq=lq, wlen=wlen),
        grid=(nb_batch,),
        in_specs=[pl.BlockSpec((lq, NSA_WIDTH), lambda b: (b, C_NQ // NSA_WIDTH)),
                  pl.BlockSpec((None, None, wlen * 4, NSA_HD), lambda b: (layer, b, 0, 0)),
                  pl.BlockSpec((lq, 2 * NSA_KV_W), lambda b: (b, C_WIN // (2 * NSA_KV_W))),
                  pl.BlockSpec((NSA_KV_HEADS, NSA_GROUP * lq, NUM_BUCKETS), lambda b: (0, 0, 0)),
                  row_spec, row_spec,
                  pl.BlockSpec((lq, NSA_WIDTH), lambda b: (b, C_NZ // NSA_WIDTH)),
                  pl.BlockSpec((lq, LANES), lambda b: (b, C_G0 // LANES)),
                  pl.BlockSpec((lq, LANES), lambda b: (b, C_G1 // LANES))],
        out_specs=row_spec,
        out_shape=jax.ShapeDtypeStruct((nb_batch * lq, NSA_WIDTH), F32),
        scratch_shapes=[pltpu.VMEM((nk, NSA_HD), F32), pltpu.VMEM((nk, NSA_HD), F32)],
        compiler_params=_params("parallel"),
        name="win_sample",
    )(proj, win4, proj, tcols_rows, o_cmp, o_slc, proj, proj, proj)


def _out_kernel(dn_ref, nsa_ref, x_ref, p_ref, wo_ref, gpost_ref, wpp_ref, gple_ref, wpg_ref, o_ref):
    half = dn_ref.shape[1]
    mixed = (jnp.dot(dn_ref[...].astype(BF16), wo_ref[0:half, :], preferred_element_type=F32)
             + jnp.dot(nsa_ref[...].astype(BF16), wo_ref[half:, :], preferred_element_type=F32))
    y = mixed * lax.rsqrt(jnp.mean(mixed * mixed, axis=-1, keepdims=True) + EPS) * gpost_ref[...]
    x1 = x_ref[...] + y
    e = jnp.dot(p_ref[...].astype(BF16), wpp_ref[...], preferred_element_type=F32)
    e = e * lax.rsqrt(jnp.mean(e * e, axis=-1, keepdims=True) + EPS) * gple_ref[...]
    gate = jax.nn.sigmoid(jnp.dot(x1.astype(BF16), wpg_ref[...], preferred_element_type=F32))
    o_ref[...] = x1 + gate * e


def _out_proj(dn, nsa, x, p_all, wo_all, gpost_all, wpp_all, gple_all, wpg_all, layer, tm):
    m = x.shape[0]
    once = pl.Buffered(1)

    def wspec(k):
        return pl.BlockSpec((None, k, D_MODEL), lambda i: (layer, 0, 0), pipeline_mode=once)

    return pl.pallas_call(
        _out_kernel,
        grid=(m // tm,),
        in_specs=[pl.BlockSpec((tm, dn.shape[1]), lambda i: (i, 0)),
                  pl.BlockSpec((tm, nsa.shape[1]), lambda i: (i, 0)),
                  pl.BlockSpec((tm, D_MODEL), lambda i: (i, 0)),
                  pl.BlockSpec((None, tm, PLE_DIM), lambda i: (layer, i, 0)),
                  wspec(D_MODEL), wspec(1), wspec(PLE_DIM), wspec(1), wspec(D_MODEL)],
        out_specs=pl.BlockSpec((tm, D_MODEL), lambda i: (i, 0)),
        out_shape=jax.ShapeDtypeStruct((m, D_MODEL), F32),
        compiler_params=_params("parallel"),
        name="out_proj",
    )(dn, nsa, x, p_all, wo_all, gpost_all, wpp_all, gple_all, wpg_all)


def _kv_rows_kernel(x_ref, cmp_ref, slc_ref, win_ref):
    tm = x_ref.shape[0]
    width = 2 * NSA_KV_W

    def rows(o_ref, seg):
        for s in range(4):
            o_ref[pl.ds(s, tm, stride=4), :] = x_ref[:, seg * width + s * NSA_HD:seg * width + (s + 1) * NSA_HD]

    rows(cmp_ref, 0)
    rows(slc_ref, 1)

    @pl.when(pl.program_id(1) == pl.num_programs(1) - 1)
    def _():
        rows(win_ref, 2)


def _kv_rows(proj, nb, seq):
    tm = min(WINDOW, seq)
    nr = seq // tm
    width = 3 * 2 * NSA_KV_W
    return pl.pallas_call(
        _kv_rows_kernel,
        grid=(nb, nr),
        in_specs=[pl.BlockSpec((tm, width), lambda b, r: (b * nr + r, C_CMP // width))],
        out_specs=[pl.BlockSpec((tm * 4, NSA_HD), lambda b, r: (b * nr + r, 0)),
                   pl.BlockSpec((tm * 4, NSA_HD), lambda b, r: (b * nr + r, 0)),
                   pl.BlockSpec((None, tm * 4, NSA_HD), lambda b, r: (b, 0, 0))],
        out_shape=[jax.ShapeDtypeStruct((nb * seq * 4, NSA_HD), F32),
                   jax.ShapeDtypeStruct((nb * seq * 4, NSA_HD), F32),
                   jax.ShapeDtypeStruct((nb, tm * 4, NSA_HD), F32)],
        compiler_params=_params("parallel", "arbitrary"),
        name="kv_rows",
    )(proj)


W_TILE = 512


def _relayout_w_in_kernel(wt_ref, sm_ref, gate_ref, o_ref, tail_ref):
    is_tail = pl.program_id(1) == pl.num_programs(1) - 1

    @pl.when(jnp.logical_not(is_tail))
    def _():
        o_ref[...] = wt_ref[0].T.astype(BF16)

    @pl.when(is_tail)
    def _():
        half = 3 * NSA_GROUP
        tail_ref[...] = jnp.zeros(tail_ref.shape, F32)
        tail_ref[0:2 * DN_HEADS, :] = sm_ref[0]
        tail_ref[C_G0 - C_SM:C_G0 - C_SM + half, :] = gate_ref[0, 0:half, :]
        tail_ref[C_G1 - C_SM:C_G1 - C_SM + half, :] = gate_ref[0, half:2 * half, :]
        o_ref[...] = tail_ref[...].T.astype(BF16)


def _relayout_w_in(w_in):
    depth, k, _ = w_in.shape
    wt = jnp.swapaxes(w_in, 1, 2)
    n_gate = 2 * 3 * NSA_GROUP
    n_tiles = N_PAD // W_TILE
    moved = C_NQ // W_TILE

    def src_row(d, j):
        j = jnp.minimum(j, n_tiles - 2)
        return (d, pl.multiple_of(j * W_TILE + jnp.where(j >= moved, _O_NQ - C_NQ, 0), SUBLANES), 0)

    return pl.pallas_call(
        _relayout_w_in_kernel,
        grid=(depth, n_tiles),
        in_specs=[pl.BlockSpec((pl.Element(1), pl.Element(W_TILE), pl.Element(k)), src_row),
                  pl.BlockSpec((pl.Element(1), pl.Element(2 * DN_HEADS), pl.Element(k)), lambda d, j: (d, _O_BETA, 0)),
                  pl.BlockSpec((pl.Element(1), pl.Element(n_gate), pl.Element(k)), lambda d, j: (d, _O_GATE, 0))],
        out_specs=pl.BlockSpec((None, k, W_TILE), lambda d, j: (d, 0, j)),
        out_shape=jax.ShapeDtypeStruct((depth, k, N_PAD), BF16),
        scratch_shapes=[pltpu.VMEM((W_TILE, k), F32)],
        compiler_params=_params("parallel", "arbitrary"),
        name="relayout_w_in",
    )(wt, wt, wt)


def _bias_expand_kernel(t_ref, bucket_ref, o_ref):
    bucket = bucket_ref[...]
    for h in range(NSA_HEADS):
        acc = jnp.full(bucket.shape, t_ref[0, h], F32)
        for b in range(1, NUM_BUCKETS):
            acc = jnp.where(bucket >= b, t_ref[b, h], acc)
        o_ref[h] = acc


def _bias_expand(table, bucket):
    rows, cols = bucket.shape
    tr = max(t for t in range(SUBLANES, min(rows, 512) + 1, SUBLANES) if rows % t == 0)
    return pl.pallas_call(
        _bias_expand_kernel,
        grid=(rows // tr,),
        in_specs=[pl.BlockSpec(memory_space=pltpu.SMEM), pl.BlockSpec((tr, cols), lambda i: (i, 0))],
        out_specs=pl.BlockSpec((NSA_HEADS, tr, cols), lambda i: (0, i, 0)),
        out_shape=jax.ShapeDtypeStruct((NSA_HEADS, rows, cols), F32),
        compiler_params=_params("parallel"),
        name="bias_expand",
    )(table, jnp.asarray(bucket))


def _bias_tiles(table):
    i = np.arange(BT)
    d = np.arange(N_BIAS_TILES)[:, None, None] * BT + i[None, :, None] - i[None, None, :]
    tiles = _bias_expand(table * LOG2E, _bucket_np(d).reshape(N_BIAS_TILES * BT, BT))
    return tiles.reshape(NSA_HEADS, N_BIAS_TILES, BT, BT)


def _bias_cmp(table, pos0, lq, nb):
    dist = (pos0 + np.arange(lq))[:, None] - (np.arange(nb) * NSA_BLOCK + NSA_BLOCK - 1)[None, :]
    return _bias_expand(table, _bucket_np(dist))


def kernel(x_prompt, x_sample, state_dn_S, state_dn_conv, cache_win_kv, cache_cmp_kv, cache_slc_kv, page_table,
           p_prompt, p_sample, rel_bias_table, w_in, w_out, g_pre, g_post, dn_conv_w, dn_A_log, dn_dt_bias,
           dn_norm_w, cmp_pos_w, cmp_w1, cmp_w2, w_ple_proj, g_ple, w_ple_gate):
    depth = w_in.shape[0]
    bp, seq, _ = x_prompt.shape
    bs, lq, _ = x_sample.shape
    n_pages = page_table.shape[1]
    past = n_pages * PAGE_SIZE
    n_pool = cache_cmp_kv.shape[1]
    wlen = cache_win_kv.shape[2]
    mp, ms = bp * seq, bs * lq
    assert seq % PK == 0 and seq % DN_CHUNK == 0 and lq == SUBLANES and past % NSA_BLOCK == 0
    kv_row = (2, NSA_KV_HEADS, NSA_HD)

    w_in_r = _relayout_w_in(w_in)
    w_out_b = w_out.astype(BF16)
    wpp_b = w_ple_proj.astype(BF16)
    wpg_b = w_ple_gate.astype(BF16)
    g_pre3 = g_pre[:, None, :]
    g_post3 = g_post[:, None, :]
    g_ple3 = g_ple[:, None, :]
    nw3 = dn_norm_w[:, None, :]
    lane_pad = ((0, 0), (DN_HEADS, LANES - 2 * DN_HEADS))
    dn_par = jnp.stack([jnp.pad(dn_A_log, lane_pad), jnp.pad(dn_dt_bias, lane_pad)], axis=1)
    pw_prompt = jnp.broadcast_to(cmp_pos_w[:, :, None, :, None], (depth, 2, NSA_KV_HEADS, NSA_BLOCK, NSA_HD))
    pw_prompt = jnp.transpose(pw_prompt, (0, 3, 1, 2, 4)).reshape(depth, NSA_BLOCK, 2 * NSA_KV_W)
    pw_page = jnp.broadcast_to(cmp_pos_w[:, None, None, :, :, None],
                               (depth, PAGE_SIZE // NSA_BLOCK, NSA_KV_HEADS, 2, NSA_BLOCK, NSA_HD))
    pw_page = jnp.transpose(pw_page, (0, 1, 4, 3, 2, 5)).reshape(depth, PAGE_SIZE * 4, NSA_HD)
    table = rel_bias_table.astype(F32)
    bias_tiles = _bias_tiles(table)
    nb_p = seq // NSA_BLOCK
    nb_s = past // NSA_BLOCK
    nbs_s = -(-(past + lq) // NSA_BLOCK)
    nbp_s = -(-nbs_s // LANES) * LANES
    bias_cmp_p = _bias_cmp(table, 0, seq, nb_p)
    bias_cmp_s = _bias_cmp(table, past, lq, nb_s)
    tcols = jnp.pad(table.T.reshape(NSA_KV_HEADS, NSA_GROUP, NUM_BUCKETS),
                    ((0, 0), (0, SUBLANES - NSA_GROUP), (0, 0)))
    tcols_rows = jnp.repeat(table.T.reshape(NSA_KV_HEADS, NSA_GROUP, NUM_BUCKETS), lq, axis=1)

    conv0_s = jnp.pad(state_dn_conv, ((0, 0), (0, 0), (SUBLANES - (CONV_W - 1), 0), (0, 0)))
    conv0_p = jnp.zeros((1, bp, SUBLANES, DN_CONV_CH), F32)
    s0_p = jnp.zeros((1, bp, DN_HEADS, DN_DK, DN_DV), F32)
    cmp4 = cache_cmp_kv.reshape(depth, n_pool, PAGE_SIZE * 4, NSA_HD)
    slc4 = cache_slc_kv.reshape(depth, n_pool, PAGE_SIZE * 4, NSA_HD)
    win4 = cache_win_kv.reshape(depth, bs, wlen * 4, NSA_HD)
    p_prompt3 = p_prompt.reshape(depth, mp, PLE_DIM)
    p_sample3 = p_sample.reshape(depth, ms, PLE_DIM)

    xp = x_prompt.reshape(mp, D_MODEL)
    xs = x_sample.reshape(ms, D_MODEL)
    outs_p = [[] for _ in range(2)]
    outs_s = [[] for _ in range(5)]
    kv_rows_p = [[] for _ in range(3)]
    tq_p = min(256, seq)
    for i in range(depth):
        proj = _in_proj(xp, g_pre3, w_in_r, i, min(1024, mp))
        dn_o, conv_n, s_n = _deltanet(proj, bp, seq, DN_CHUNK, conv0_p, s0_p, 0, dn_conv_w, dn_par, nw3, i)
        ckv = _cmp_mlp(_pool_prompt(proj, bp, seq, pw_prompt, i), cmp_w1, cmp_w2, i)
        o_cmp, sel = _cmp_attn(proj, ckv, bias_cmp_p, bp, seq, 0, tq_p, nb_p, nb_p, True)
        nsa_o = _nsa_prompt(proj, sel, bias_tiles, o_cmp, bp, seq)
        xp = _out_proj(dn_o, nsa_o, xp, p_prompt3, w_out_b, g_post3, wpp_b, g_ple3, wpg_b, i, min(256, mp))
        for dst, rows in zip(kv_rows_p, _kv_rows(proj, bp, seq)):
            dst.append(rows)
        outs_p[0].append(s_n)
        outs_p[1].append(conv_n[:, SUBLANES - (CONV_W - 1):])

        proj = _in_proj(xs, g_pre3, w_in_r, i, ms)
        dn_o, conv_n, s_n = _deltanet(proj, bs, lq, lq, conv0_s, state_dn_S, i, dn_conv_w, dn_par, nw3, i)
        pooled = _pool_paged(cmp4, page_table, pw_page, i, min(16, n_pages))
        ckv = _cmp_mlp(pooled, cmp_w1, cmp_w2, i)
        o_cmp, score = _cmp_attn(proj, ckv, bias_cmp_s, bs, lq, past, lq, nbs_s, nbp_s, False)
        score_t = jnp.transpose(score.reshape(ms * NSA_KV_HEADS, nbp_s))
        _, idx_t, val_t = _topk(score_t, nbs_s)
        o_slc = _slc_sample(proj, slc4, page_table, idx_t.T.reshape(-1), val_t.T.reshape(-1), tcols, i,
                            bs, lq, past)
        nsa_o = _win_sample(proj, win4, tcols_rows, o_cmp, o_slc, i, bs, lq, past)
        xs = _out_proj(dn_o, nsa_o, xs, p_sample3, w_out_b, g_post3, wpp_b, g_ple3, wpg_b, i, ms)
        kv_all = proj[:, C_CMP:C_WIN + 2 * NSA_KV_W].reshape(bs, lq, 3, *kv_row)
        win_all = jnp.concatenate([cache_win_kv[i], kv_all[:, :, 2]], axis=1)
        outs_s[0].append(s_n)
        outs_s[1].append(conv_n[:, SUBLANES - (CONV_W - 1):])
        outs_s[2].append(win_all[:, win_all.shape[1] - min(WINDOW, past + lq):])
        outs_s[3].append(kv_all[:, :, 0])
        outs_s[4].append(kv_all[:, :, 1])

    p_cmp, p_slc, p_win = (jnp.stack(o) for o in kv_rows_p)
    return ((xp.reshape(bp, seq, D_MODEL), xs.reshape(bs, lq, D_MODEL), jnp.stack(outs_p[0]), jnp.stack(outs_p[1]),
             p_win.reshape(depth, bp, min(WINDOW, seq), *kv_row), p_cmp.reshape(depth, bp, seq, *kv_row),
             p_slc.reshape(depth, bp, seq, *kv_row)) + tuple(jnp.stack(o) for o in outs_s))
```

```python
import functools
import math

import numpy as np
import jax
import jax.numpy as jnp
from jax import lax
from jax.experimental import pallas as pl
from jax.experimental.pallas import tpu as pltpu

F32 = jnp.float32
BF16 = jnp.bfloat16

D_MODEL = 2048
DN_HEADS = 8
DN_DK = 128
DN_DV = 128
DN_QK_W = DN_HEADS * DN_DK
DN_CONV_CH = 2 * DN_QK_W + DN_HEADS * DN_DV
CONV_W = 4
DN_CHUNK = 64
NSA_HEADS = 8
NSA_KV_HEADS = 2
NSA_GROUP = 4
NSA_HD = 128
NSA_WIDTH = NSA_HEADS * NSA_HD
NSA_KV_W = NSA_KV_HEADS * NSA_HD
NSA_BLOCK = 64
NSA_TOPN = 16
WINDOW = 512
SCALE = NSA_HD ** -0.5
NUM_BUCKETS = 32
REL_MAX_DIST = 1024
PLE_DIM = 256
PAGE_SIZE = 128
EPS = 1e-6
NEG = -1e30
LOG2E = math.log2(math.e)

VMEM_LIMIT_BYTES = 56 * 1024 * 1024
LANES = 128
SUBLANES = 8

C_QKV = 0
C_Z = 3072
C_NQ = 4096
C_NZ = 5120
C_CMP = 6144
C_SLC = 6656
C_WIN = 7168
C_SM = 7680
C_G0 = 7808
C_G1 = 7936
N_PAD = 8192
_O_BETA = DN_CONV_CH + DN_HEADS * DN_DV
_O_NQ = _O_BETA + 2 * DN_HEADS
_O_GATE = _O_NQ + 2 * NSA_WIDTH + 3 * 2 * NSA_KV_W
N_BIAS_TILES = 9


def _bucket_np(dist):
    dist = np.maximum(np.asarray(dist, np.int64), 0)
    exact = NUM_BUCKETS // 2
    scaled = np.log(np.maximum(dist, 1).astype(np.float64) / exact) / math.log(REL_MAX_DIST / exact)
    large = np.minimum(exact + (scaled * (NUM_BUCKETS - exact)).astype(np.int64), NUM_BUCKETS - 1)
    return np.where(dist < exact, dist, large).astype(np.int32)


_BUCKET_THR = tuple(int(np.argmax(_bucket_np(np.arange(2 * REL_MAX_DIST)) >= b)) for b in range(1, NUM_BUCKETS))


def _dot(a, b):
    return jnp.dot(a.astype(BF16), b.astype(BF16), preferred_element_type=F32)


def _dot_nt(a, b):
    return lax.dot_general(a.astype(BF16), b.astype(BF16), (((1,), (1,)), ((), ())), preferred_element_type=F32)


def _dot_tn(a, b):
    return lax.dot_general(a.astype(BF16), b.astype(BF16), (((0,), (0,)), ((), ())), preferred_element_type=F32)


def _dot_hi(a, b):
    return jnp.dot(a, b, preferred_element_type=F32, precision=lax.Precision.HIGHEST)


def _dot_hi_nt(a, b):
    return lax.dot_general(a, b, (((1,), (1,)), ((), ())), preferred_element_type=F32,
                           precision=lax.Precision.HIGHEST)


def _silu(x):
    return x * jax.nn.sigmoid(x)


def _params(*sem):
    return pltpu.CompilerParams(dimension_semantics=sem, vmem_limit_bytes=VMEM_LIMIT_BYTES)


def _in_proj_kernel(x_ref, g_ref, w_ref, o_ref, xn_ref):
    @pl.when(pl.program_id(1) == 0)
    def _():
        x = x_ref[...]
        y = x * lax.rsqrt(jnp.mean(x * x, axis=-1, keepdims=True) + EPS)
        xn_ref[...] = (y * g_ref[...]).astype(BF16)

    o_ref[...] = jnp.dot(xn_ref[...], w_ref[...], preferred_element_type=F32)


IN_TM = 1024
IN_TN = 1024
POOL_PAGES = 16


def _in_proj(x, g_all, w_all, layer, tm):
    m = x.shape[0]
    tn = IN_TN
    return pl.pallas_call(
        _in_proj_kernel,
        grid=(m // tm, N_PAD // tn),
        in_specs=[pl.BlockSpec((tm, D_MODEL), lambda i, j: (i, 0)),
                  pl.BlockSpec((None, 1, D_MODEL), lambda i, j: (layer, 0, 0)),
                  pl.BlockSpec((None, D_MODEL, tn), lambda i, j: (layer, 0, j))],
        out_specs=pl.BlockSpec((tm, tn), lambda i, j: (i, j)),
        out_shape=jax.ShapeDtypeStruct((m, N_PAD), F32),
        scratch_shapes=[pltpu.VMEM((tm, D_MODEL), BF16)],
        compiler_params=_params("parallel", "arbitrary"),
        name="in_proj",
    )(x, g_all, w_all)


def _dn_prep_kernel(qkv_ref, sm_ref, conv0_ref, cw_ref, par_ref,
                    u_ref, w_ref, qd_ref, kd_ref, qk_ref, gl_ref, convn_ref, xs_ref, *, chunk, cs):
    tb = chunk * cs
    halo = SUBLANES

    @pl.when(pl.program_id(1) == 0)
    def _():
        xs_ref[0:halo, :] = conv0_ref[...]

    xs_ref[halo:halo + tb, :] = qkv_ref[...]
    base = halo - (CONV_W - 1)
    y = xs_ref[base:base + tb, :] * cw_ref[0:1, :]
    for j in range(1, CONV_W):
        y = y + xs_ref[base + j:base + j + tb, :] * cw_ref[j:j + 1, :]
    y = _silu(y)
    tail = xs_ref[tb:tb + halo, :]
    xs_ref[0:halo, :] = tail
    convn_ref[...] = tail

    sm = sm_ref[...]
    beta_all = jax.nn.sigmoid(sm)
    xg = sm + par_ref[1:2, :]
    softplus = jnp.maximum(xg, 0.0) + jnp.log1p(jnp.exp(-jnp.abs(xg)))
    g_all = -jnp.exp(par_ref[0:1, :]) * softplus
    row = lax.broadcasted_iota(jnp.int32, (tb, tb), 0)
    col = lax.broadcasted_iota(jnp.int32, (tb, tb), 1)
    in_chunk_lower = (row >= col) & (row // chunk == col // chunk)
    gc_all = _dot_hi(jnp.where(in_chunk_lower, 1.0, 0.0), g_all)
    pick = (lax.broadcasted_iota(jnp.int32, (DN_HEADS, LANES), 1)
            == lax.broadcasted_iota(jnp.int32, (DN_HEADS, LANES), 0) + DN_HEADS).astype(F32)
    gc_rows = _dot_hi_nt(pick, gc_all)
    for c in range(cs):
        gl_ref[c] = gc_all[(c + 1) * chunk - 1:(c + 1) * chunk, :]
    r64 = lax.broadcasted_iota(jnp.int32, (chunk, chunk), 0)
    c64 = lax.broadcasted_iota(jnp.int32, (chunk, chunk), 1)
    lower = r64 >= c64
    strict = r64 > c64

    lhs, kns, rhs, decays = [], [], [], []
    for h in range(DN_HEADS):
        hc = slice(h * DN_DV, (h + 1) * DN_DV)
        qh = y[:, h * DN_DK:(h + 1) * DN_DK]
        kh = y[:, DN_QK_W + h * DN_DK:DN_QK_W + (h + 1) * DN_DK]
        vh = y[:, 2 * DN_QK_W + h * DN_DV:2 * DN_QK_W + (h + 1) * DN_DV]
        qn = qh * lax.rsqrt(jnp.sum(qh * qh, axis=-1, keepdims=True) + EPS) * (DN_DK ** -0.5)
        kn = kh * lax.rsqrt(jnp.sum(kh * kh, axis=-1, keepdims=True) + EPS)
        beta = beta_all[:, h:h + 1]
        gc = gc_all[:, DN_HEADS + h:DN_HEADS + h + 1]
        egc = jnp.exp(gc)
        kb = kn * beta
        rhs_h = jnp.concatenate([vh * beta, kb * egc], axis=-1)
        qd_ref[:, hc] = (qn * egc).astype(qd_ref.dtype)
        for c in range(cs):
            rs = slice(c * chunk, (c + 1) * chunk)
            gcc = gc[rs]
            decays.append(jnp.exp(jnp.where(lower, gcc - gc_rows[h:h + 1, rs], NEG)))
            lhs.append(jnp.concatenate([kb[rs], qn[rs]], axis=0))
            kns.append(kn[rs])
            rhs.append(rhs_h[rs])
            kd_ref[rs, hc] = (kn[rs] * jnp.exp(gcc[chunk - 1:chunk, :] - gcc)).astype(kd_ref.dtype)
    decay = jnp.stack(decays)
    both = jnp.einsum('nid,njd->nij', jnp.stack(lhs).astype(BF16), jnp.stack(kns).astype(BF16),
                      preferred_element_type=F32)
    a = jnp.where(strict[None], both[:, :chunk] * decay, 0.0)
    qk = both[:, chunk:] * decay

    def bmm(p, q):
        return jnp.einsum('nij,njk->nik', p, q, preferred_element_type=F32)

    def split(v):
        hi = v.astype(BF16)
        return hi, (v - hi.astype(F32)).astype(BF16)

    def bmm3(p, q):
        (ph, pl_), (qh, ql) = p, q
        return bmm(ph, qh) + (bmm(ph, ql) + bmm(pl_, qh))

    pw_s = split(a)
    inv = jnp.where(r64 == c64, 1.0, 0.0)[None] - a
    for _ in range(int(math.log2(chunk)) - 1):
        pw_s = split(bmm3(pw_s, pw_s))
        inv = inv + bmm3(pw_s, split(inv))
    x = bmm3(split(inv), split(jnp.stack(rhs)))
    for h in range(DN_HEADS):
        hc = slice(h * DN_DV, (h + 1) * DN_DV)
        for c in range(cs):
            rs = slice(c * chunk, (c + 1) * chunk)
            n = h * cs + c
            u_ref[rs, hc] = x[n, :, :DN_DV]
            w_ref[rs, hc] = x[n, :, DN_DV:].astype(w_ref.dtype)
            qk_ref[rs, h * chunk:(h + 1) * chunk] = qk[n]


def _dn_scan_kernel(u_ref, w_ref, qd_ref, kd_ref, qk_ref, gl_ref, z_ref, s0_ref, nw_ref,
                    o_ref, sout_ref, s_ref, *, chunk, cs):
    @pl.when(pl.program_id(1) == 0)
    def _():
        s_ref[...] = s0_ref[...]

    def heads(ref, rs, width):
        return jnp.stack([ref[rs, h * width:(h + 1) * width] for h in range(DN_HEADS)])

    def bdot(spec, p, q):
        return jnp.einsum(spec, p.astype(BF16), q.astype(BF16), preferred_element_type=F32)

    for c in range(cs):
        rs = slice(c * chunk, (c + 1) * chunk)
        decay_last = jnp.exp(gl_ref[c])
        decay_last = jnp.stack([decay_last[:, DN_HEADS + h:DN_HEADS + h + 1] for h in range(DN_HEADS)])
        s = s_ref[...]
        sb = s.astype(BF16)
        v_new = heads(u_ref, rs, DN_DV) - bdot('hcd,hde->hce', heads(w_ref, rs, DN_DK), sb)
        o = (bdot('hcd,hde->hce', heads(qd_ref, rs, DN_DK), sb)
             + bdot('hcs,hse->hce', heads(qk_ref, rs, chunk), v_new))
        s_ref[...] = s * decay_last + bdot('hcd,hce->hde', heads(kd_ref, rs, DN_DK), v_new)
        on = o * lax.rsqrt(jnp.mean(o * o, axis=-1, keepdims=True) + EPS) * nw_ref[...]
        for h in range(DN_HEADS):
            hc = slice(h * DN_DV, (h + 1) * DN_DV)
            o_ref[rs, hc] = on[h] * _silu(z_ref[rs, hc])

    @pl.when(pl.program_id(1) == pl.num_programs(1) - 1)
    def _():
        sout_ref[...] = s_ref[...]


def _deltanet(proj, nb, seq, chunk, conv0_all, s0_all, state_layer, cw_all, par_all, nw_all, layer):
    m = nb * seq
    nchunks = seq // chunk
    width = DN_HEADS * DN_DV
    hs = (DN_HEADS, DN_DK, DN_DV)
    cs_a = 4 if nchunks % 4 == 0 else 1
    na = nchunks // cs_a
    ta = cs_a * chunk
    op_dtype = BF16 if chunk % (2 * SUBLANES) == 0 else F32
    u, w, qd, kd, qk, gl, conv_n = pl.pallas_call(
        functools.partial(_dn_prep_kernel, chunk=chunk, cs=cs_a),
        grid=(nb, na),
        in_specs=[pl.BlockSpec((ta, DN_CONV_CH), lambda b, c: (b * na + c, C_QKV // DN_CONV_CH)),
                  pl.BlockSpec((ta, LANES), lambda b, c: (b * na + c, C_SM // LANES)),
                  pl.BlockSpec((None, None, SUBLANES, DN_CONV_CH), lambda b, c: (state_layer, b, 0, 0)),
                  pl.BlockSpec((None, CONV_W, DN_CONV_CH), lambda b, c: (layer, 0, 0)),
                  pl.BlockSpec((None, 2, LANES), lambda b, c: (layer, 0, 0))],
        out_specs=[pl.BlockSpec((ta, width), lambda b, c: (b * na + c, 0)),
                   pl.BlockSpec((ta, width), lambda b, c: (b * na + c, 0)),
                   pl.BlockSpec((ta, width), lambda b, c: (b * na + c, 0)),
                   pl.BlockSpec((ta, width), lambda b, c: (b * na + c, 0)),
                   pl.BlockSpec((ta, DN_HEADS * chunk), lambda b, c: (b * na + c, 0)),
                   pl.BlockSpec((cs_a, 1, LANES), lambda b, c: (b * na + c, 0, 0)),
                   pl.BlockSpec((None, SUBLANES, DN_CONV_CH), lambda b, c: (b, 0, 0))],
        out_shape=[jax.ShapeDtypeStruct((m, width), F32),
                   jax.ShapeDtypeStruct((m, width), op_dtype),
                   jax.ShapeDtypeStruct((m, width), op_dtype),
                   jax.ShapeDtypeStruct((m, width), op_dtype),
                   jax.ShapeDtypeStruct((m, DN_HEADS * chunk), F32),
                   jax.ShapeDtypeStruct((nb * nchunks, 1, LANES), F32),
                   jax.ShapeDtypeStruct((nb, SUBLANES, DN_CONV_CH), F32)],
        scratch_shapes=[pltpu.VMEM((ta + SUBLANES, DN_CONV_CH), F32)],
        compiler_params=_params("parallel", "arbitrary"),
        name="dn_prep",
    )(proj, proj, conv0_all, cw_all, par_all)

    cs_b = 4 if nchunks % 4 == 0 else 1
    nbk = nchunks // cs_b
    tbk = cs_b * chunk
    row_spec = pl.BlockSpec((tbk, width), lambda b, c: (b * nbk + c, 0))
    o, s_n = pl.pallas_call(
        functools.partial(_dn_scan_kernel, chunk=chunk, cs=cs_b),
        grid=(nb, nbk),
        in_specs=[row_spec, row_spec, row_spec, row_spec,
                  pl.BlockSpec((tbk, DN_HEADS * chunk), lambda b, c: (b * nbk + c, 0)),
                  pl.BlockSpec((cs_b, 1, LANES), lambda b, c: (b * nbk + c, 0, 0)),
                  pl.BlockSpec((tbk, width), lambda b, c: (b * nbk + c, C_Z // width)),
                  pl.BlockSpec((None, None) + hs, lambda b, c: (state_layer, b, 0, 0, 0)),
                  pl.BlockSpec((None, 1, DN_DV), lambda b, c: (layer, 0, 0))],
        out_specs=[row_spec, pl.BlockSpec((None,) + hs, lambda b, c: (b, 0, 0, 0))],
        out_shape=[jax.ShapeDtypeStruct((m, width), F32), jax.ShapeDtypeStruct((nb,) + hs, F32)],
        scratch_shapes=[pltpu.VMEM(hs, F32)],
        compiler_params=_params("parallel", "arbitrary"),
        name="dn_scan",
    )(u, w, qd, kd, qk, gl, proj, s0_all, nw_all)
    return o, conv_n, s_n


def _pool_kernel(x_ref, pw_ref, o_ref):
    rows = x_ref.shape[0]
    x = x_ref[...].reshape(rows // NSA_BLOCK, NSA_BLOCK, 2 * NSA_KV_W)
    o_ref[...] = jnp.sum(x * pw_ref[...][None], axis=1)


def _pool_prompt(proj, nb, seq, pw_all, layer):
    rows = min(seq, 1024)
    nr = seq // rows
    width = 2 * NSA_KV_W
    return pl.pallas_call(
        _pool_kernel,
        grid=(nb, nr),
        in_specs=[pl.BlockSpec((rows, width), lambda b, r: (b * nr + r, C_CMP // width)),
                  pl.BlockSpec((None, NSA_BLOCK, width), lambda b, r: (layer, 0, 0))],
        out_specs=pl.BlockSpec((None, rows // NSA_BLOCK, width), lambda b, r: (b, r, 0)),
        out_shape=jax.ShapeDtypeStruct((nb, seq // NSA_BLOCK, width), F32),
        compiler_params=_params("parallel", "parallel"),
        name="pool_prompt",
    )(proj, pw_all)


def _pool_pages(page_refs, pw_ref, o_ref):
    rows = PAGE_SIZE * 4
    per_block = rows // 2 // SUBLANES
    for k, page_ref in enumerate(page_refs):
        prod = page_ref[...] * pw_ref[...]
        y = jnp.sum(prod.reshape(2, per_block, SUBLANES, NSA_HD), axis=1)
        y = y[:, 0:4, :] + y[:, 4:8, :]
        for s in range(4):
            o_ref[2 * k:2 * k + 2, s * NSA_HD:(s + 1) * NSA_HD] = y[:, s, :]


def _paged_pool_kernel(pt_ref, *refs, pages):
    del pt_ref
    _pool_pages(refs[:pages], refs[pages], refs[pages + 1])


def _in_proj_pool_kernel(pt_ref, x_ref, g_ref, w_ref, *refs, pages):
    del pt_ref
    o_ref, pooled_ref, xn_ref = refs[pages + 1:]
    _in_proj_kernel(x_ref, g_ref, w_ref, o_ref, xn_ref)
    _pool_pages(refs[:pages], refs[pages], pooled_ref)


def _in_proj_pool(x, g_all, w_all, cache4, page_table, pw_all, layer, tm, tn, pages):
    m = x.shape[0]
    nb, n_pages = page_table.shape
    assert (m // tm, N_PAD // tn) == (nb, n_pages // pages)
    rows = PAGE_SIZE * 4

    def page_spec(k):
        return pl.BlockSpec((None, None, rows, NSA_HD), lambda i, j, pt: (layer, pt[i, j * pages + k], 0, 0))

    grid_spec = pltpu.PrefetchScalarGridSpec(
        num_scalar_prefetch=1,
        grid=(m // tm, N_PAD // tn),
        in_specs=[pl.BlockSpec((tm, D_MODEL), lambda i, j, pt: (i, 0)),
                  pl.BlockSpec((None, 1, D_MODEL), lambda i, j, pt: (layer, 0, 0)),
                  pl.BlockSpec((None, D_MODEL, tn), lambda i, j, pt: (layer, 0, j))]
        + [page_spec(k) for k in range(pages)]
        + [pl.BlockSpec((None, rows, NSA_HD), lambda i, j, pt: (layer, 0, 0))],
        out_specs=[pl.BlockSpec((tm, tn), lambda i, j, pt: (i, j)),
                   pl.BlockSpec((None, 2 * pages, 4 * NSA_HD), lambda i, j, pt: (i, j, 0))],
        scratch_shapes=[pltpu.VMEM((tm, D_MODEL), BF16)],
    )
    return pl.pallas_call(
        functools.partial(_in_proj_pool_kernel, pages=pages),
        grid_spec=grid_spec,
        out_shape=[jax.ShapeDtypeStruct((m, N_PAD), F32),
                   jax.ShapeDtypeStruct((nb, 2 * n_pages, 4 * NSA_HD), F32)],
        compiler_params=_params("parallel", "arbitrary"),
        name="in_proj_pool",
    )(page_table, x, g_all, w_all, *([cache4] * pages), pw_all)


def _pool_paged(cache4, page_table, pw_all, layer, pages):
    nb, n_pages = page_table.shape
    steps = n_pages // pages
    rows = PAGE_SIZE * 4

    def page_spec(k):
        return pl.BlockSpec((None, None, rows, NSA_HD),
                            lambda b, s, pt: (layer, pt[b, s * pages + k], 0, 0))

    grid_spec = pltpu.PrefetchScalarGridSpec(
        num_scalar_prefetch=1,
        grid=(nb, steps),
        in_specs=[page_spec(k) for k in range(pages)]
        + [pl.BlockSpec((None, rows, NSA_HD), lambda b, s, pt: (layer, 0, 0))],
        out_specs=pl.BlockSpec((None, 2 * pages, 4 * NSA_HD), lambda b, s, pt: (b, s, 0)),
    )
    return pl.pallas_call(
        functools.partial(_paged_pool_kernel, pages=pages),
        grid_spec=grid_spec,
        out_shape=jax.ShapeDtypeStruct((nb, 2 * n_pages, 4 * NSA_HD), F32),
        compiler_params=_params("parallel", "arbitrary"),
        name="pool_paged",
    )(page_table, *([cache4] * pages), pw_all)


def _cmp_mlp_kernel(p_ref, w1_ref, w2_ref, o_ref):
    for s in range(4):
        kv = s // NSA_KV_HEADS
        h = _silu(_dot_hi(p_ref[:, s * NSA_HD:(s + 1) * NSA_HD], w1_ref[kv]))
        o_ref[:, s * NSA_HD:(s + 1) * NSA_HD] = _dot_hi(h, w2_ref[kv])


def _cmp_mlp(pooled, w1_all, w2_all, layer):
    nb, n, width = pooled.shape
    wspec = pl.BlockSpec((None, 2, NSA_HD, NSA_HD), lambda b: (layer, 0, 0, 0))
    return pl.pallas_call(
        _cmp_mlp_kernel,
        grid=(nb,),
        in_specs=[pl.BlockSpec((None, n, width), lambda b: (b, 0, 0)), wspec, wspec],
        out_specs=pl.BlockSpec((None, n, width), lambda b: (b, 0, 0)),
        out_shape=jax.ShapeDtypeStruct((nb, n, width), F32),
        compiler_params=_params("parallel"),
        name="cmp_mlp",
    )(pooled, w1_all, w2_all)


def _select_blocks(score, st_ref, nbp, n_causal):
    st_ref[...] = score.T
    halves = [st_ref[g * nbp:(g + 1) * nbp, :] for g in range(NSA_KV_HEADS)]
    bi = lax.broadcasted_iota(jnp.int32, halves[0].shape, 0)

    def body(j, ranks):
        tie = jnp.where(j < bi, 1, 0)
        out = []
        for g in range(NSA_KV_HEADS):
            other = st_ref[pl.ds(g * nbp + j, 1), :]
            out.append(ranks[g] + jnp.where(other > halves[g], 1, jnp.where(other == halves[g], tie, 0)))
        return tuple(out)

    ranks = lax.fori_loop(0, n_causal, body, tuple(jnp.zeros(bi.shape, jnp.int32) for _ in halves))
    keep = [jnp.where((ranks[g] < NSA_TOPN) & (halves[g] >= 0.0), 1.0, 0.0) for g in range(NSA_KV_HEADS)]
    return jnp.concatenate(keep, axis=0).T


def _cmp_attn_kernel(q_ref, ckv_ref, bias_ref, o_ref, sc_ref, *scratch, pos0, tq, nb, nbs, nbp, select):
    qi = pl.program_id(1)
    t = pos0 + qi * tq + lax.broadcasted_iota(jnp.int32, (tq, 1), 0)
    blk = lax.broadcasted_iota(jnp.int32, (1, nb), 1)
    valid = t >= blk * NSA_BLOCK + (NSA_BLOCK - 1)
    bi = lax.broadcasted_iota(jnp.int32, (1, nbp), 1)
    cur = t // NSA_BLOCK
    forced = (bi == 0) | (bi == cur) | (bi == cur - 1)
    causal = bi <= cur
    scores = []
    for g in range(NSA_KV_HEADS):
        ck = ckv_ref[:, g * NSA_HD:(g + 1) * NSA_HD]
        cv = ckv_ref[:, (NSA_KV_HEADS + g) * NSA_HD:(NSA_KV_HEADS + g + 1) * NSA_HD]
        imp = jnp.zeros((tq, nb), F32)
        for r in range(NSA_GROUP):
            h = g * NSA_GROUP + r
            logits = _dot_nt(q_ref[:, h * NSA_HD:(h + 1) * NSA_HD], ck) * SCALE + bias_ref[h]
            logits = jnp.where(valid, logits, NEG)
            mx = jnp.max(logits, axis=-1, keepdims=True)
            p = jnp.where(valid, jnp.exp(logits - mx), 0.0)
            p = p / jnp.maximum(jnp.sum(p, axis=-1, keepdims=True), 1e-30)
            o_ref[:, h * NSA_HD:(h + 1) * NSA_HD] = _dot(p, cv)
            imp = imp + p
        if nbp > nb:
            imp = jnp.concatenate([imp, jnp.zeros((tq, nbp - nb), F32)], axis=-1)
        score = jnp.where(causal, jnp.where(forced, NSA_GROUP + 1.0, imp), -1.0)
        scores.append(jnp.where(bi < nbs, score, -2.0))
    score = jnp.concatenate(scores, axis=-1)
    if select:
        n_causal = jnp.minimum((pos0 + (qi + 1) * tq - 1) // NSA_BLOCK + 1, nbs)
        sc_ref[...] = _select_blocks(score, scratch[0], nbp, n_causal)
    else:
        sc_ref[...] = score


def _cmp_attn(proj, ckv, bias_cmp, nb_batch, seq, pos0, tq, nbs, nbp, select):
    nq = seq // tq
    nb = ckv.shape[1]
    m = nb_batch * seq
    bias_tq = bias_cmp.shape[1] // nq
    return pl.pallas_call(
        functools.partial(_cmp_attn_kernel, pos0=pos0, tq=tq, nb=nb, nbs=nbs, nbp=nbp, select=select),
        scratch_shapes=[pltpu.VMEM((NSA_KV_HEADS * nbp, tq), F32)] if select else [],
        grid=(nb_batch, nq),
        in_specs=[pl.BlockSpec((tq, NSA_WIDTH), lambda b, i: (b * nq + i, C_NQ // NSA_WIDTH)),
                  pl.BlockSpec((None, nb, 2 * NSA_KV_W), lambda b, i: (b, 0, 0)),
                  pl.BlockSpec((NSA_HEADS, bias_tq, nb), lambda b, i: (0, i, 0))],
        out_specs=[pl.BlockSpec((tq, NSA_WIDTH), lambda b, i: (b * nq + i, 0)),
                   pl.BlockSpec((tq, NSA_KV_HEADS * nbp), lambda b, i: (b * nq + i, 0))],
        out_shape=[jax.ShapeDtypeStruct((m, NSA_WIDTH), F32),
                   jax.ShapeDtypeStruct((m, NSA_KV_HEADS * nbp), F32)],
        compiler_params=_params("parallel", "parallel"),
        name="cmp_attn",
    )(proj, ckv, bias_cmp)


def _topk_kernel(s_ref, sel_ref, idx_ref, val_ref, *, nbs):
    s = s_ref[...]
    bi = lax.broadcasted_iota(jnp.int32, s.shape, 0)

    def body(j, rank):
        other = s_ref[pl.ds(j, 1), :]
        beats = (other > s) | ((other == s) & (j < bi))
        return rank + beats.astype(jnp.int32)

    rank = lax.fori_loop(0, nbs, body, jnp.zeros(s.shape, jnp.int32))
    ok = s >= 0.0
    sel_ref[...] = jnp.where((rank < NSA_TOPN) & ok, 1.0, 0.0)
    for r in range(NSA_TOPN):
        hit = rank == r
        idx_ref[r:r + 1, :] = jnp.sum(jnp.where(hit, bi, 0), axis=0, keepdims=True)
        val_ref[r:r + 1, :] = jnp.sum(jnp.where(hit & ok, 1, 0), axis=0, keepdims=True)


def _topk(score_t, nbs):
    rows, cols = score_t.shape
    tc = min(cols, 512)
    return pl.pallas_call(
        functools.partial(_topk_kernel, nbs=nbs),
        grid=(cols // tc,),
        in_specs=[pl.BlockSpec((rows, tc), lambda i: (0, i))],
        out_specs=[pl.BlockSpec((rows, tc), lambda i: (0, i)),
                   pl.BlockSpec((NSA_TOPN, tc), lambda i: (0, i)),
                   pl.BlockSpec((NSA_TOPN, tc), lambda i: (0, i))],
        out_shape=[jax.ShapeDtypeStruct((rows, cols), F32),
                   jax.ShapeDtypeStruct((NSA_TOPN, cols), jnp.int32),
                   jax.ShapeDtypeStruct((NSA_TOPN, cols), jnp.int32)],
        compiler_params=_params("parallel"),
        name="topk",
    )(score_t)


PQ = 256
PK = 512
BT = LANES


def _softmax_rows(s, ok):
    mx = jnp.max(s, axis=-1, keepdims=True)
    p = jnp.where(ok, jnp.exp(s - mx), 0.0)
    return p, jnp.sum(p, axis=-1, keepdims=True)


def _nsa_prompt_kernel(q_ref, ks_ref, vs_ref, kw_ref, vw_ref, sel_ref, bt_ref, ocmp_ref, z_ref, gate_ref,
                       o_ref, s2_ref, p2_ref, sw_ref, pw_ref, m_ref, acc_ref, *, n_bt, wsub):
    qi = pl.program_id(2)
    rows = NSA_GROUP * PQ
    qsub = PQ // BT
    ksub = PK // BT
    q4 = jnp.concatenate([q_ref[:, r * NSA_HD:(r + 1) * NSA_HD] for r in range(NSA_GROUP)], axis=0)
    q4 = (q4 * (SCALE * LOG2E)).astype(BF16)
    t = qi * PQ + lax.broadcasted_iota(jnp.int32, (PQ, 1), 0)
    nsel = sel_ref.shape[1]
    blk0 = pl.program_id(1) * (nsel // NSA_KV_HEADS)
    unselected = jnp.where(sel_ref[...] > 0.5, 0.0, NEG).astype(BF16)
    qx = jnp.concatenate([q4, jnp.concatenate([unselected] * NSA_GROUP, axis=0)], axis=1)

    def row_groups():
        for r in range(NSA_GROUP):
            for a in range(qsub):
                yield slice(r * PQ + a * BT, r * PQ + (a + 1) * BT), r, a

    def softmax_block(s_ref, p_ref, rs, r, tile_of, extra, width, m_old):
        zmax = None
        for c in range(width):
            cs = slice(c * BT, (c + 1) * BT)
            z = s_ref[rs, cs] + bt_ref[r, jnp.clip(tile_of(c), 0, N_BIAS_TILES - 1)]
            if extra is not None:
                z = z + extra[:, cs]
            s_ref[rs, cs] = z
            zmax = z if zmax is None else jnp.maximum(zmax, z)
        mx = jnp.broadcast_to(jnp.max(zmax, axis=-1, keepdims=True), (BT, BT))
        if m_old is not None:
            mx = jnp.maximum(m_old, mx)
        for c in range(width):
            cs = slice(c * BT, (c + 1) * BT)
            p_ref[rs, cs] = jnp.exp2(s_ref[rs, cs] - mx).astype(BF16)
        return mx

    def with_ones(v):
        return jnp.concatenate([v.astype(BF16), jnp.ones(v.shape, BF16)], axis=1)

    m_ref[...] = jnp.full((rows, BT), NEG, F32)
    acc_ref[...] = jnp.zeros((rows, 2 * NSA_HD), F32)

    def key_tile(kt, diagonal, slot):
        s_ref, p_ref = s2_ref.at[slot], p2_ref.at[slot]
        k0 = pl.multiple_of(kt * PK, PK)
        member = (lax.broadcasted_iota(jnp.int32, (PK, nsel), 1)
                  == blk0 + kt * (PK // NSA_BLOCK) + lax.broadcasted_iota(jnp.int32, (PK, nsel), 0) // NSA_BLOCK)
        kx = jnp.concatenate([ks_ref[pl.ds(k0, PK), :].astype(BF16), jnp.where(member, 1.0, 0.0).astype(BF16)],
                             axis=1)
        s_ref[...] = lax.dot_general(qx, kx, (((1,), (1,)), ((), ())), preferred_element_type=F32)
        future = None
        if diagonal:
            future = jnp.where(k0 + lax.broadcasted_iota(jnp.int32, (1, PK), 1) <= t, 0.0, NEG)
        for rs, r, a in row_groups():
            m_old = m_ref[rs]
            mx = softmax_block(s_ref, p_ref, rs, r, lambda c: qi * qsub + a - (kt * ksub + c),
                               None if future is None else future[a * BT:(a + 1) * BT], ksub, m_old)
            alpha = jnp.exp2(m_old - mx)
            acc_ref[rs] = jnp.concatenate([alpha, alpha], axis=1) * acc_ref[rs]
            m_ref[rs] = mx
        acc_ref[...] += jnp.dot(p_ref[...], with_ones(vs_ref[pl.ds(k0, PK), :]), preferred_element_type=F32)

    last = (qi * PQ) // PK

    def body(j, carry):
        key_tile(2 * j, False, 0)
        key_tile(2 * j + 1, False, 1)
        return carry

    lax.fori_loop(0, last // 2, body, 0)

    @pl.when(last % 2 == 1)
    def _():
        key_tile(last - 1, False, 0)

    key_tile(last, True, 1)
    o_slc = acc_ref[:, :NSA_HD] / jnp.maximum(acc_ref[:, NSA_HD:], 1e-30)

    wk = wsub * BT
    kst = jnp.clip(qi * qsub - WINDOW // BT, 0, n_bt - wsub)
    k0 = pl.multiple_of(kst * BT, BT)
    sw_ref[...] = lax.dot_general(q4, kw_ref[pl.ds(k0, wk), :].astype(BF16), (((1,), (1,)), ((), ())),
                                  preferred_element_type=F32)
    dist = t - (k0 + lax.broadcasted_iota(jnp.int32, (1, wk), 1))
    outside = jnp.where((dist >= 0) & (dist <= WINDOW), 0.0, NEG)
    for rs, r, a in row_groups():
        softmax_block(sw_ref, pw_ref, rs, r, lambda c: qi * qsub + a - (kst + c),
                      outside[a * BT:(a + 1) * BT], wsub, None)
    pv = jnp.dot(pw_ref[...], with_ones(vw_ref[pl.ds(k0, wk), :]), preferred_element_type=F32)
    o_win = pv[:, :NSA_HD] / jnp.maximum(pv[:, NSA_HD:], 1e-30)

    gates = jax.nn.sigmoid(gate_ref[...])
    for r in range(NSA_GROUP):
        sl = slice(r * NSA_HD, (r + 1) * NSA_HD)
        rs = slice(r * PQ, (r + 1) * PQ)
        o = (gates[:, 3 * r:3 * r + 1] * ocmp_ref[:, sl] + gates[:, 3 * r + 1:3 * r + 2] * o_slc[rs]
             + gates[:, 3 * r + 2:3 * r + 3] * o_win[rs])
        o_ref[:, sl] = o * _silu(z_ref[:, sl])


def _nsa_prompt(proj, sel, bias_tiles, o_cmp, nb_batch, seq):
    nq = seq // PQ
    n_bt = seq // BT
    wsub = min((WINDOW + PQ) // BT, n_bt)
    gw = NSA_GROUP * NSA_HD
    rows = NSA_GROUP * PQ
    nsel = sel.shape[-1]

    def kv_spec(col):
        return pl.BlockSpec((seq, NSA_HD), lambda b, g, i: (b, col // NSA_HD + g))

    return pl.pallas_call(
        functools.partial(_nsa_prompt_kernel, n_bt=n_bt, wsub=wsub),
        grid=(nb_batch, NSA_KV_HEADS, nq),
        in_specs=[pl.BlockSpec((PQ, gw), lambda b, g, i: (b * nq + i, C_NQ // gw + g)),
                  kv_spec(C_SLC), kv_spec(C_SLC + NSA_KV_W), kv_spec(C_WIN), kv_spec(C_WIN + NSA_KV_W),
                  pl.BlockSpec((PQ, nsel), lambda b, g, i: (b * nq + i, 0)),
                  pl.BlockSpec((NSA_GROUP, N_BIAS_TILES, BT, BT), lambda b, g, i: (g, 0, 0, 0)),
                  pl.BlockSpec((PQ, gw), lambda b, g, i: (b * nq + i, g)),
                  pl.BlockSpec((PQ, gw), lambda b, g, i: (b * nq + i, C_NZ // gw + g)),
                  pl.BlockSpec((PQ, LANES), lambda b, g, i: (b * nq + i, C_G0 // LANES + g))],
        out_specs=pl.BlockSpec((PQ, gw), lambda b, g, i: (b * nq + i, g)),
        out_shape=jax.ShapeDtypeStruct((nb_batch * seq, NSA_WIDTH), F32),
        scratch_shapes=[pltpu.VMEM((2, rows, PK), F32), pltpu.VMEM((2, rows, PK), BF16),
                        pltpu.VMEM((rows, wsub * BT), F32), pltpu.VMEM((rows, wsub * BT), BF16),
                        pltpu.VMEM((rows, BT), F32), pltpu.VMEM((rows, 2 * NSA_HD), F32)],
        compiler_params=_params("parallel", "parallel", "arbitrary"),
        name="nsa_prompt",
    )(proj, proj, proj, proj, proj, sel, bias_tiles, o_cmp, proj, proj)


def _bias_by_threshold(dist, tcols):
    bias = jnp.broadcast_to(tcols[:, 0:1], (tcols.shape[0], dist.shape[-1]))
    for b in range(1, NUM_BUCKETS):
        bias = jnp.where(dist >= _BUCKET_THR[b - 1], tcols[:, b:b + 1], bias)
    return bias


SLOT_CHUNKS = NSA_TOPN + 2


def _slc_sample_kernel(pt_ref, idx_ref, val_ref, *refs, past, lq):
    del pt_ref
    n_in = NSA_KV_HEADS * NSA_TOPN
    page_refs = refs[:n_in]
    q_ref, knew_ref, tcol_ref, o_ref, kc_ref, vc_ref = refs[n_in:]
    b = pl.program_id(0)
    qi = pl.program_id(1)
    t = past + qi
    n_past_blocks = past // NSA_BLOCK
    nk = SLOT_CHUNKS * NSA_BLOCK
    lane = lax.broadcasted_iota(jnp.int32, (1, nk), 1)
    slot = lane // NSA_BLOCK
    this_q = lax.broadcasted_iota(jnp.int32, (lq, 1), 0) == qi
    for g in range(NSA_KV_HEADS):
        base = ((b * lq + qi) * NSA_KV_HEADS + g) * NSA_TOPN
        blk_of_lane = jnp.where(slot == NSA_TOPN, n_past_blocks, n_past_blocks + 1)
        use_of_lane = jnp.where(slot == NSA_TOPN, 1, 0)
        for j in range(NSA_TOPN):
            n_j = idx_ref[base + j]
            use = jnp.where((val_ref[base + j] > 0) & (n_j < n_past_blocks), 1, 0)
            blk_of_lane = jnp.where(slot == j, n_j, blk_of_lane)
            use_of_lane = jnp.where(slot == j, use, use_of_lane)
            page = page_refs[g * NSA_TOPN + j]
            kc_ref[j * NSA_BLOCK:(j + 1) * NSA_BLOCK, :] = page[pl.ds(g, NSA_BLOCK, stride=4), :]
            vc_ref[j * NSA_BLOCK:(j + 1) * NSA_BLOCK, :] = page[pl.ds(NSA_KV_HEADS + g, NSA_BLOCK, stride=4), :]
        new0 = NSA_TOPN * NSA_BLOCK
        kc_ref[new0:, :] = jnp.zeros((nk - new0, NSA_HD), F32)
        vc_ref[new0:, :] = jnp.zeros((nk - new0, NSA_HD), F32)
        kc_ref[new0:new0 + lq, :] = knew_ref[:, g * NSA_HD:(g + 1) * NSA_HD]
        vc_ref[new0:new0 + lq, :] = knew_ref[:, (NSA_KV_HEADS + g) * NSA_HD:(NSA_KV_HEADS + g + 1) * NSA_HD]
        q4 = jnp.concatenate(
            [jnp.sum(jnp.where(this_q, q_ref[:, (g * NSA_GROUP + r) * NSA_HD:(g * NSA_GROUP + r + 1) * NSA_HD], 0.0),
                     axis=0, keepdims=True)
             for r in range(NSA_GROUP)] + [jnp.zeros((SUBLANES - NSA_GROUP, NSA_HD), F32)], axis=0)
        dist = t - (blk_of_lane * NSA_BLOCK + lane % NSA_BLOCK)
        ok = (use_of_lane > 0) & (dist >= 0)
        s = _dot_nt(q4, kc_ref[...]) * SCALE + _bias_by_threshold(dist, tcol_ref[g])
        s = jnp.where(ok, s, NEG)
        p, den = _softmax_rows(s, ok)
        o = _dot(p, vc_ref[...]) / jnp.maximum(den, 1e-30)
        for r in range(NSA_GROUP):
            h = g * NSA_GROUP + r
            o_ref[:, h * NSA_HD:(h + 1) * NSA_HD] = o[r:r + 1, :]


def _slc_sample(proj, cache4, page_table, idx_flat, val_flat, tcols, layer, nb_batch, lq, past):
    rows = NSA_BLOCK * 4
    n_past_blocks = past // NSA_BLOCK

    n_sel = NSA_KV_HEADS * NSA_TOPN
    blocks = jnp.minimum(idx_flat.reshape(nb_batch, lq * n_sel), n_past_blocks - 1)
    pages = jnp.take_along_axis(page_table, blocks // 2, axis=1)
    loc = jnp.concatenate([pages.reshape(nb_batch * lq, n_sel), (blocks % 2).reshape(nb_batch * lq, n_sel)], axis=1)

    def page_spec(g, j):
        def index_map(b, i, loc, idx, val):
            row = b * lq + i
            return (layer, loc[row, g * NSA_TOPN + j], loc[row, n_sel + g * NSA_TOPN + j], 0)
        return pl.BlockSpec((None, None, rows, NSA_HD), index_map)

    grid_spec = pltpu.PrefetchScalarGridSpec(
        num_scalar_prefetch=3,
        grid=(nb_batch, lq),
        in_specs=[page_spec(g, j) for g in range(NSA_KV_HEADS) for j in range(NSA_TOPN)]
        + [pl.BlockSpec((lq, NSA_WIDTH), lambda b, i, *_: (b, C_NQ // NSA_WIDTH)),
           pl.BlockSpec((lq, 2 * NSA_KV_W), lambda b, i, *_: (b, C_SLC // (2 * NSA_KV_W))),
           pl.BlockSpec((NSA_KV_HEADS, SUBLANES, NUM_BUCKETS), lambda b, i, *_: (0, 0, 0))],
        out_specs=pl.BlockSpec((None, 1, NSA_WIDTH), lambda b, i, *_: (b * lq + i, 0, 0)),
        scratch_shapes=[pltpu.VMEM((SLOT_CHUNKS * NSA_BLOCK, NSA_HD), F32),
                        pltpu.VMEM((SLOT_CHUNKS * NSA_BLOCK, NSA_HD), F32)],
    )
    out = pl.pallas_call(
        functools.partial(_slc_sample_kernel, past=past, lq=lq),
        grid_spec=grid_spec,
        out_shape=jax.ShapeDtypeStruct((nb_batch * lq, 1, NSA_WIDTH), F32),
        compiler_params=_params("parallel", "arbitrary"),
        name="slc_sample",
    )(loc, idx_flat, val_flat, *([cache4] * n_sel), proj, proj, tcols)
    return out.reshape(nb_batch * lq, NSA_WIDTH)


def _win_sample_kernel(q_ref, wpast_ref, wnew_ref, tcol_ref, ocmp_ref, oslc_ref, z_ref, g0_ref, g1_ref,
                       o_ref, kc_ref, vc_ref, *, past, lq, wlen):
    rows = NSA_GROUP * lq
    nk = kc_ref.shape[0]
    lane = lax.broadcasted_iota(jnp.int32, (1, nk), 1)
    kpos = past - wlen + lane
    t = past + lax.broadcasted_iota(jnp.int32, (rows, 1), 0) % lq
    dist = t - kpos
    ok = (dist >= 0) & (dist <= WINDOW) & (kpos >= 0) & (lane < wlen + lq)
    gate_refs = (g0_ref, g1_ref)
    for g in range(NSA_KV_HEADS):
        kc_ref[0:wlen, :] = wpast_ref[pl.ds(g, wlen, stride=4), :]
        vc_ref[0:wlen, :] = wpast_ref[pl.ds(NSA_KV_HEADS + g, wlen, stride=4), :]
        kc_ref[wlen:, :] = jnp.zeros((nk - wlen, NSA_HD), F32)
        vc_ref[wlen:, :] = jnp.zeros((nk - wlen, NSA_HD), F32)
        kc_ref[wlen:wlen + lq, :] = wnew_ref[:, g * NSA_HD:(g + 1) * NSA_HD]
        vc_ref[wlen:wlen + lq, :] = wnew_ref[:, (NSA_KV_HEADS + g) * NSA_HD:(NSA_KV_HEADS + g + 1) * NSA_HD]
        q4 = jnp.concatenate([q_ref[:, (g * NSA_GROUP + r) * NSA_HD:(g * NSA_GROUP + r + 1) * NSA_HD]
                              for r in range(NSA_GROUP)], axis=0)
        s = _dot_nt(q4, kc_ref[...]) * SCALE + _bias_by_threshold(dist, tcol_ref[g])
        s = jnp.where(ok, s, NEG)
        p, den = _softmax_rows(s, ok)
        o_win = _dot(p, vc_ref[...]) / jnp.maximum(den, 1e-30)
        gates = jax.nn.sigmoid(gate_refs[g][...])
        for r in range(NSA_GROUP):
            h = g * NSA_GROUP + r
            sl = slice(h * NSA_HD, (h + 1) * NSA_HD)
            o = (gates[:, 3 * r:3 * r + 1] * ocmp_ref[:, sl] + gates[:, 3 * r + 1:3 * r + 2] * oslc_ref[:, sl]
                 + gates[:, 3 * r + 2:3 * r + 3] * o_win[r * lq:(r + 1) * lq])
            o_ref[:, sl] = o * _silu(z_ref[:, sl])


def _win_sample(proj, win4, tcols_rows, o_cmp, o_slc, layer, nb_batch, lq, past):
    wlen = win4.shape[2] // 4
    nk = -(-(wlen + lq) // LANES) * LANES
    row_spec = pl.BlockSpec((lq, NSA_WIDTH), lambda b: (b, 0))
    return pl.pallas_call(
        functools.partial(_win_sample_kernel, past=past, lq=lq, wlen=wlen),
        grid=(nb_batch,),
        in_specs=[pl.BlockSpec((lq, NSA_WIDTH), lambda b: (b, C_NQ // NSA_WIDTH)),
                  pl.BlockSpec((None, None, wlen * 4, NSA_HD), lambda b: (layer, b, 0, 0)),
                  pl.BlockSpec((lq, 2 * NSA_KV_W), lambda b: (b, C_WIN // (2 * NSA_KV_W))),
                  pl.BlockSpec((NSA_KV_HEADS, NSA_GROUP * lq, NUM_BUCKETS), lambda b: (0, 0, 0)),
                  row_spec, row_spec,
                  pl.BlockSpec((lq, NSA_WIDTH), lambda b: (b, C_NZ // NSA_WIDTH)),
                  pl.BlockSpec((lq, LANES), lambda b: (b, C_G0 // LANES)),
                  pl.BlockSpec((lq, LANES), lambda b: (b, C_G1 // LANES))],
        out_specs=row_spec,
        out_shape=jax.ShapeDtypeStruct((nb_batch * lq, NSA_WIDTH), F32),
        scratch_shapes=[pltpu.VMEM((nk, NSA_HD), F32), pltpu.VMEM((nk, NSA_HD), F32)],
        compiler_params=_params("parallel"),
        name="win_sample",
    )(proj, win4, proj, tcols_rows, o_cmp, o_slc, proj, proj, proj)


def _out_kernel(dn_ref, nsa_ref, x_ref, p_ref, wo_ref, gpost_ref, wpp_ref, gple_ref, wpg_ref, o_ref):
    half = dn_ref.shape[1]
    mixed = (jnp.dot(dn_ref[...].astype(BF16), wo_ref[0:half, :], preferred_element_type=F32)
             + jnp.dot(nsa_ref[...].astype(BF16), wo_ref[half:, :], preferred_element_type=F32))
    y = mixed * lax.rsqrt(jnp.mean(mixed * mixed, axis=-1, keepdims=True) + EPS) * gpost_ref[...]
    x1 = x_ref[...] + y
    e = jnp.dot(p_ref[...].astype(BF16), wpp_ref[...], preferred_element_type=F32)
    e = e * lax.rsqrt(jnp.mean(e * e, axis=-1, keepdims=True) + EPS) * gple_ref[...]
    gate = jax.nn.sigmoid(jnp.dot(x1.astype(BF16), wpg_ref[...], preferred_element_type=F32))
    o_ref[...] = x1 + gate * e


def _out_proj(dn, nsa, x, p_all, wo_all, gpost_all, wpp_all, gple_all, wpg_all, layer, tm):
    m = x.shape[0]
    once = pl.Buffered(1)

    def wspec(k):
        return pl.BlockSpec((None, k, D_MODEL), lambda i: (layer, 0, 0), pipeline_mode=once)

    return pl.pallas_call(
        _out_kernel,
        grid=(m // tm,),
        in_specs=[pl.BlockSpec((tm, dn.shape[1]), lambda i: (i, 0)),
                  pl.BlockSpec((tm, nsa.shape[1]), lambda i: (i, 0)),
                  pl.BlockSpec((tm, D_MODEL), lambda i: (i, 0)),
                  pl.BlockSpec((None, tm, PLE_DIM), lambda i: (layer, i, 0)),
                  wspec(D_MODEL), wspec(1), wspec(PLE_DIM), wspec(1), wspec(D_MODEL)],
        out_specs=pl.BlockSpec((tm, D_MODEL), lambda i: (i, 0)),
        out_shape=jax.ShapeDtypeStruct((m, D_MODEL), F32),
        compiler_params=_params("parallel"),
        name="out_proj",
    )(dn, nsa, x, p_all, wo_all, gpost_all, wpp_all, gple_all, wpg_all)


def _kv_rows_kernel(*refs, depth):
    x_refs = refs[:depth]
    cmp_ref, slc_ref, win_ref = refs[depth:]
    tm = x_refs[0].shape[0]
    width = 2 * NSA_KV_W

    def rows(o_ref, x_ref, seg):
        for s in range(4):
            o_ref[pl.ds(s, tm, stride=4), :] = x_ref[:, seg * width + s * NSA_HD:seg * width + (s + 1) * NSA_HD]

    for layer in range(depth):
        @pl.when(pl.program_id(0) == layer)
        def _(x_ref=x_refs[layer]):
            rows(cmp_ref, x_ref, 0)
            rows(slc_ref, x_ref, 1)

            @pl.when(pl.program_id(2) == pl.num_programs(2) - 1)
            def _():
                rows(win_ref, x_ref, 2)


def _kv_rows(projs, nb, seq):
    depth = len(projs)
    tm = min(WINDOW, seq)
    nr = seq // tm
    width = 3 * 2 * NSA_KV_W

    def x_spec(layer):
        return pl.BlockSpec((tm, width), lambda d, b, r: (jnp.where(d == layer, b * nr + r, 0), C_CMP // width))

    return pl.pallas_call(
        functools.partial(_kv_rows_kernel, depth=depth),
        grid=(depth, nb, nr),
        in_specs=[x_spec(layer) for layer in range(depth)],
        out_specs=[pl.BlockSpec((None, tm * 4, NSA_HD), lambda d, b, r: (d, b * nr + r, 0)),
                   pl.BlockSpec((None, tm * 4, NSA_HD), lambda d, b, r: (d, b * nr + r, 0)),
                   pl.BlockSpec((None, None, tm * 4, NSA_HD), lambda d, b, r: (d, b, 0, 0))],
        out_shape=[jax.ShapeDtypeStruct((depth, nb * seq * 4, NSA_HD), F32),
                   jax.ShapeDtypeStruct((depth, nb * seq * 4, NSA_HD), F32),
                   jax.ShapeDtypeStruct((depth, nb, tm * 4, NSA_HD), F32)],
        compiler_params=_params("arbitrary", "arbitrary", "arbitrary"),
        name="kv_rows",
    )(*projs)


W_TILE = 512


def _relayout_w_in_kernel(wt_ref, sm_ref, gate_ref, o_ref, tail_ref):
    is_tail = pl.program_id(1) == pl.num_programs(1) - 1

    @pl.when(jnp.logical_not(is_tail))
    def _():
        o_ref[...] = wt_ref[0].T.astype(BF16)

    @pl.when(is_tail)
    def _():
        half = 3 * NSA_GROUP
        tail_ref[...] = jnp.zeros(tail_ref.shape, F32)
        tail_ref[0:2 * DN_HEADS, :] = sm_ref[0]
        tail_ref[C_G0 - C_SM:C_G0 - C_SM + half, :] = gate_ref[0, 0:half, :]
        tail_ref[C_G1 - C_SM:C_G1 - C_SM + half, :] = gate_ref[0, half:2 * half, :]
        o_ref[...] = tail_ref[...].T.astype(BF16)


def _relayout_w_in(w_in):
    depth, k, _ = w_in.shape
    wt = jnp.swapaxes(w_in, 1, 2)
    n_gate = 2 * 3 * NSA_GROUP
    n_tiles = N_PAD // W_TILE
    moved = C_NQ // W_TILE

    def src_row(d, j):
        j = jnp.minimum(j, n_tiles - 2)
        return (d, pl.multiple_of(j * W_TILE + jnp.where(j >= moved, _O_NQ - C_NQ, 0), SUBLANES), 0)

    return pl.pallas_call(
        _relayout_w_in_kernel,
        grid=(depth, n_tiles),
        in_specs=[pl.BlockSpec((pl.Element(1), pl.Element(W_TILE), pl.Element(k)), src_row),
                  pl.BlockSpec((pl.Element(1), pl.Element(2 * DN_HEADS), pl.Element(k)), lambda d, j: (d, _O_BETA, 0)),
                  pl.BlockSpec((pl.Element(1), pl.Element(n_gate), pl.Element(k)), lambda d, j: (d, _O_GATE, 0))],
        out_specs=pl.BlockSpec((None, k, W_TILE), lambda d, j: (d, 0, j)),
        out_shape=jax.ShapeDtypeStruct((depth, k, N_PAD), BF16),
        scratch_shapes=[pltpu.VMEM((W_TILE, k), F32)],
        compiler_params=_params("parallel", "arbitrary"),
        name="relayout_w_in",
    )(wt, wt, wt)


def _bias_expand_kernel(t_ref, bucket_ref, o_ref):
    bucket = bucket_ref[...]
    for h in range(NSA_HEADS):
        acc = jnp.full(bucket.shape, t_ref[0, h], F32)
        for b in range(1, NUM_BUCKETS):
            acc = jnp.where(bucket >= b, t_ref[b, h], acc)
        o_ref[h] = acc


def _bias_expand(table, bucket):
    rows, cols = bucket.shape
    tr = max(t for t in range(SUBLANES, min(rows, 512) + 1, SUBLANES) if rows % t == 0)
    return pl.pallas_call(
        _bias_expand_kernel,
        grid=(rows // tr,),
        in_specs=[pl.BlockSpec(memory_space=pltpu.SMEM), pl.BlockSpec((tr, cols), lambda i: (i, 0))],
        out_specs=pl.BlockSpec((NSA_HEADS, tr, cols), lambda i: (0, i, 0)),
        out_shape=jax.ShapeDtypeStruct((NSA_HEADS, rows, cols), F32),
        compiler_params=_params("parallel"),
        name="bias_expand",
    )(table, jnp.asarray(bucket))


def _bias_tiles(table):
    i = np.arange(BT)
    d = np.arange(N_BIAS_TILES)[:, None, None] * BT + i[None, :, None] - i[None, None, :]
    tiles = _bias_expand(table * LOG2E, _bucket_np(d).reshape(N_BIAS_TILES * BT, BT))
    return tiles.reshape(NSA_HEADS, N_BIAS_TILES, BT, BT)


def _bias_cmp(table, pos0, lq, nb):
    dist = (pos0 + np.arange(lq))[:, None] - (np.arange(nb) * NSA_BLOCK + NSA_BLOCK - 1)[None, :]
    return _bias_expand(table, _bucket_np(dist))


def kernel(x_prompt, x_sample, state_dn_S, state_dn_conv, cache_win_kv, cache_cmp_kv, cache_slc_kv, page_table,
           p_prompt, p_sample, rel_bias_table, w_in, w_out, g_pre, g_post, dn_conv_w, dn_A_log, dn_dt_bias,
           dn_norm_w, cmp_pos_w, cmp_w1, cmp_w2, w_ple_proj, g_ple, w_ple_gate):
    depth = w_in.shape[0]
    bp, seq, _ = x_prompt.shape
    bs, lq, _ = x_sample.shape
    n_pages = page_table.shape[1]
    past = n_pages * PAGE_SIZE
    n_pool = cache_cmp_kv.shape[1]
    wlen = cache_win_kv.shape[2]
    mp, ms = bp * seq, bs * lq
    assert seq % PK == 0 and seq % DN_CHUNK == 0 and lq == SUBLANES and past % NSA_BLOCK == 0
    kv_row = (2, NSA_KV_HEADS, NSA_HD)

    w_in_r = _relayout_w_in(w_in)
    w_out_b = w_out.astype(BF16)
    wpp_b = w_ple_proj.astype(BF16)
    wpg_b = w_ple_gate.astype(BF16)
    g_pre3 = g_pre[:, None, :]
    g_post3 = g_post[:, None, :]
    g_ple3 = g_ple[:, None, :]
    nw3 = dn_norm_w[:, None, :]
    lane_pad = ((0, 0), (DN_HEADS, LANES - 2 * DN_HEADS))
    dn_par = jnp.stack([jnp.pad(dn_A_log, lane_pad), jnp.pad(dn_dt_bias, lane_pad)], axis=1)
    pw_prompt = jnp.broadcast_to(cmp_pos_w[:, :, None, :, None], (depth, 2, NSA_KV_HEADS, NSA_BLOCK, NSA_HD))
    pw_prompt = jnp.transpose(pw_prompt, (0, 3, 1, 2, 4)).reshape(depth, NSA_BLOCK, 2 * NSA_KV_W)
    pw_page = jnp.broadcast_to(cmp_pos_w[:, None, None, :, :, None],
                               (depth, PAGE_SIZE // NSA_BLOCK, NSA_KV_HEADS, 2, NSA_BLOCK, NSA_HD))
    pw_page = jnp.transpose(pw_page, (0, 1, 4, 3, 2, 5)).reshape(depth, PAGE_SIZE * 4, NSA_HD)
    table = rel_bias_table.astype(F32)
    bias_tiles = _bias_tiles(table)
    nb_p = seq // NSA_BLOCK
    nb_s = past // NSA_BLOCK
    nbs_s = -(-(past + lq) // NSA_BLOCK)
    nbp_s = -(-nbs_s // LANES) * LANES
    bias_cmp_p = _bias_cmp(table, 0, seq, nb_p)
    bias_cmp_s = _bias_cmp(table, past, lq, nb_s)
    tcols = jnp.pad(table.T.reshape(NSA_KV_HEADS, NSA_GROUP, NUM_BUCKETS),
                    ((0, 0), (0, SUBLANES - NSA_GROUP), (0, 0)))
    tcols_rows = jnp.repeat(table.T.reshape(NSA_KV_HEADS, NSA_GROUP, NUM_BUCKETS), lq, axis=1)

    conv0_s = jnp.pad(state_dn_conv, ((0, 0), (0, 0), (SUBLANES - (CONV_W - 1), 0), (0, 0)))
    conv0_p = jnp.zeros((1, bp, SUBLANES, DN_CONV_CH), F32)
    s0_p = jnp.zeros((1, bp, DN_HEADS, DN_DK, DN_DV), F32)
    cmp4 = cache_cmp_kv.reshape(depth, n_pool, PAGE_SIZE * 4, NSA_HD)
    slc4 = cache_slc_kv.reshape(depth, n_pool, PAGE_SIZE * 4, NSA_HD)
    win4 = cache_win_kv.reshape(depth, bs, wlen * 4, NSA_HD)
    p_prompt3 = p_prompt.reshape(depth, mp, PLE_DIM)
    p_sample3 = p_sample.reshape(depth, ms, PLE_DIM)

    xp = x_prompt.reshape(mp, D_MODEL)
    xs = x_sample.reshape(ms, D_MODEL)
    outs_p = [[] for _ in range(2)]
    outs_s = [[] for _ in range(5)]
    projs_p = []
    tq_p = min(256, seq)
    tm_p = min(IN_TM, mp)
    pages = min(POOL_PAGES, n_pages)
    fuse_pool = (mp // tm_p, N_PAD // IN_TN) == (bs, n_pages // pages)
    for i in range(depth):
        if fuse_pool:
            proj, pooled = _in_proj_pool(xp, g_pre3, w_in_r, cmp4, page_table, pw_page, i, tm_p, IN_TN, pages)
        else:
            proj = _in_proj(xp, g_pre3, w_in_r, i, tm_p)
            pooled = _pool_paged(cmp4, page_table, pw_page, i, pages)
        dn_o, conv_n, s_n = _deltanet(proj, bp, seq, DN_CHUNK, conv0_p, s0_p, 0, dn_conv_w, dn_par, nw3, i)
        ckv = _cmp_mlp(_pool_prompt(proj, bp, seq, pw_prompt, i), cmp_w1, cmp_w2, i)
        o_cmp, sel = _cmp_attn(proj, ckv, bias_cmp_p, bp, seq, 0, tq_p, nb_p, nb_p, True)
        nsa_o = _nsa_prompt(proj, sel, bias_tiles, o_cmp, bp, seq)
        xp = _out_proj(dn_o, nsa_o, xp, p_prompt3, w_out_b, g_post3, wpp_b, g_ple3, wpg_b, i, min(256, mp))
        projs_p.append(proj)
        outs_p[0].append(s_n)
        outs_p[1].append(conv_n[:, SUBLANES - (CONV_W - 1):])

        proj = _in_proj(xs, g_pre3, w_in_r, i, ms)
        dn_o, conv_n, s_n = _deltanet(proj, bs, lq, lq, conv0_s, state_dn_S, i, dn_conv_w, dn_par, nw3, i)
        ckv = _cmp_mlp(pooled, cmp_w1, cmp_w2, i)
        o_cmp, score = _cmp_attn(proj, ckv, bias_cmp_s, bs, lq, past, lq, nbs_s, nbp_s, False)
        score_t = jnp.transpose(score.reshape(ms * NSA_KV_HEADS, nbp_s))
        _, idx_t, val_t = _topk(score_t, nbs_s)
        o_slc = _slc_sample(proj, slc4, page_table, idx_t.T.reshape(-1), val_t.T.reshape(-1), tcols, i,
                            bs, lq, past)
        nsa_o = _win_sample(proj, win4, tcols_rows, o_cmp, o_slc, i, bs, lq, past)
        xs = _out_proj(dn_o, nsa_o, xs, p_sample3, w_out_b, g_post3, wpp_b, g_ple3, wpg_b, i, ms)
        kv_all = proj[:, C_CMP:C_WIN + 2 * NSA_KV_W].reshape(bs, lq, 3, *kv_row)
        win_all = jnp.concatenate([cache_win_kv[i], kv_all[:, :, 2]], axis=1)
        outs_s[0].append(s_n)
        outs_s[1].append(conv_n[:, SUBLANES - (CONV_W - 1):])
        outs_s[2].append(win_all[:, win_all.shape[1] - min(WINDOW, past + lq):])
        outs_s[3].append(kv_all[:, :, 0])
        outs_s[4].append(kv_all[:, :, 1])

    p_cmp, p_slc, p_win = _kv_rows(projs_p, bp, seq)
    return ((xp.reshape(bp, seq, D_MODEL), xs.reshape(bs, lq, D_MODEL), jnp.stack(outs_p[0]), jnp.stack(outs_p[1]),
             p_win.reshape(depth, bp, min(WINDOW, seq), *kv_row), p_cmp.reshape(depth, bp, seq, *kv_row),
             p_slc.reshape(depth, bp, seq, *kv_row)) + tuple(jnp.stack(o) for o in outs_s))
```

```python
import functools
import math

import numpy as np
import jax
import jax.numpy as jnp
from jax import lax
from jax.experimental import pallas as pl
from jax.experimental.pallas import tpu as pltpu

F32 = jnp.float32
BF16 = jnp.bfloat16

D_MODEL = 2048
DN_HEADS = 8
DN_DK = 128
DN_DV = 128
DN_QK_W = DN_HEADS * DN_DK
DN_CONV_CH = 2 * DN_QK_W + DN_HEADS * DN_DV
CONV_W = 4
DN_CHUNK = 64
NSA_HEADS = 8
NSA_KV_HEADS = 2
NSA_GROUP = 4
NSA_HD = 128
NSA_WIDTH = NSA_HEADS * NSA_HD
NSA_KV_W = NSA_KV_HEADS * NSA_HD
NSA_BLOCK = 64
NSA_TOPN = 16
WINDOW = 512
SCALE = NSA_HD ** -0.5
NUM_BUCKETS = 32
REL_MAX_DIST = 1024
PLE_DIM = 256
PAGE_SIZE = 128
EPS = 1e-6
NEG = -1e30
LOG2E = math.log2(math.e)

VMEM_LIMIT_BYTES = 56 * 1024 * 1024
LANES = 128
SUBLANES = 8

C_QKV = 0
C_Z = 3072
C_NQ = 4096
C_NZ = 5120
C_CMP = 6144
C_SLC = 6656
C_WIN = 7168
C_SM = 7680
C_G0 = 7808
C_G1 = 7936
N_PAD = 8192
_O_BETA = DN_CONV_CH + DN_HEADS * DN_DV
_O_NQ = _O_BETA + 2 * DN_HEADS
_O_GATE = _O_NQ + 2 * NSA_WIDTH + 3 * 2 * NSA_KV_W
N_BIAS_TILES = 9


def _bucket_np(dist):
    dist = np.maximum(np.asarray(dist, np.int64), 0)
    exact = NUM_BUCKETS // 2
    scaled = np.log(np.maximum(dist, 1).astype(np.float64) / exact) / math.log(REL_MAX_DIST / exact)
    large = np.minimum(exact + (scaled * (NUM_BUCKETS - exact)).astype(np.int64), NUM_BUCKETS - 1)
    return np.where(dist < exact, dist, large).astype(np.int32)


_BUCKET_THR = tuple(int(np.argmax(_bucket_np(np.arange(2 * REL_MAX_DIST)) >= b)) for b in range(1, NUM_BUCKETS))


def _dot(a, b):
    return jnp.dot(a.astype(BF16), b.astype(BF16), preferred_element_type=F32)


def _dot_nt(a, b):
    return lax.dot_general(a.astype(BF16), b.astype(BF16), (((1,), (1,)), ((), ())), preferred_element_type=F32)


def _dot_tn(a, b):
    return lax.dot_general(a.astype(BF16), b.astype(BF16), (((0,), (0,)), ((), ())), preferred_element_type=F32)


def _dot_hi(a, b):
    return jnp.dot(a, b, preferred_element_type=F32, precision=lax.Precision.HIGHEST)


def _dot_hi_nt(a, b):
    return lax.dot_general(a, b, (((1,), (1,)), ((), ())), preferred_element_type=F32,
                           precision=lax.Precision.HIGHEST)


def _silu(x):
    return x * jax.nn.sigmoid(x)


def _params(*sem):
    return pltpu.CompilerParams(dimension_semantics=sem, vmem_limit_bytes=VMEM_LIMIT_BYTES)


def _in_proj_kernel(x_ref, g_ref, w_ref, o_ref, xn_ref):
    @pl.when(pl.program_id(1) == 0)
    def _():
        x = x_ref[...]
        y = x * lax.rsqrt(jnp.mean(x * x, axis=-1, keepdims=True) + EPS)
        xn_ref[...] = (y * g_ref[...]).astype(BF16)

    o_ref[...] = jnp.dot(xn_ref[...], w_ref[...], preferred_element_type=F32)


IN_TM = 1024
IN_TN = 1024
POOL_PAGES = 16


def _in_proj(x, g_all, w_all, layer, tm):
    m = x.shape[0]
    tn = IN_TN
    return pl.pallas_call(
        _in_proj_kernel,
        grid=(m // tm, N_PAD // tn),
        in_specs=[pl.BlockSpec((tm, D_MODEL), lambda i, j: (i, 0)),
                  pl.BlockSpec((None, 1, D_MODEL), lambda i, j: (layer, 0, 0)),
                  pl.BlockSpec((None, D_MODEL, tn), lambda i, j: (layer, 0, j))],
        out_specs=pl.BlockSpec((tm, tn), lambda i, j: (i, j)),
        out_shape=jax.ShapeDtypeStruct((m, N_PAD), F32),
        scratch_shapes=[pltpu.VMEM((tm, D_MODEL), BF16)],
        compiler_params=_params("parallel", "arbitrary"),
        name="in_proj",
    )(x, g_all, w_all)


def _dn_prep_kernel(qkv_ref, sm_ref, conv0_ref, cw_ref, par_ref,
                    u_ref, w_ref, qd_ref, kd_ref, qk_ref, gl_ref, convn_ref, xs_ref, *, chunk, cs):
    tb = chunk * cs
    halo = SUBLANES

    @pl.when(pl.program_id(1) == 0)
    def _():
        xs_ref[0:halo, :] = conv0_ref[...]

    xs_ref[halo:halo + tb, :] = qkv_ref[...]
    base = halo - (CONV_W - 1)
    y = xs_ref[base:base + tb, :] * cw_ref[0:1, :]
    for j in range(1, CONV_W):
        y = y + xs_ref[base + j:base + j + tb, :] * cw_ref[j:j + 1, :]
    y = _silu(y)
    tail = xs_ref[tb:tb + halo, :]
    xs_ref[0:halo, :] = tail
    convn_ref[...] = tail

    sm = sm_ref[...]
    beta_all = jax.nn.sigmoid(sm)
    xg = sm + par_ref[1:2, :]
    softplus = jnp.maximum(xg, 0.0) + jnp.log1p(jnp.exp(-jnp.abs(xg)))
    g_all = -jnp.exp(par_ref[0:1, :]) * softplus
    row = lax.broadcasted_iota(jnp.int32, (tb, tb), 0)
    col = lax.broadcasted_iota(jnp.int32, (tb, tb), 1)
    in_chunk_lower = (row >= col) & (row // chunk == col // chunk)
    gc_all = _dot_hi(jnp.where(in_chunk_lower, 1.0, 0.0), g_all)
    pick = (lax.broadcasted_iota(jnp.int32, (DN_HEADS, LANES), 1)
            == lax.broadcasted_iota(jnp.int32, (DN_HEADS, LANES), 0) + DN_HEADS).astype(F32)
    gc_rows = _dot_hi_nt(pick, gc_all)
    for c in range(cs):
        gl_ref[c] = gc_all[(c + 1) * chunk - 1:(c + 1) * chunk, :]
    r64 = lax.broadcasted_iota(jnp.int32, (chunk, chunk), 0)
    c64 = lax.broadcasted_iota(jnp.int32, (chunk, chunk), 1)
    lower = r64 >= c64
    strict = r64 > c64

    lhs, kns, rhs, decays = [], [], [], []
    for h in range(DN_HEADS):
        hc = slice(h * DN_DV, (h + 1) * DN_DV)
        qh = y[:, h * DN_DK:(h + 1) * DN_DK]
        kh = y[:, DN_QK_W + h * DN_DK:DN_QK_W + (h + 1) * DN_DK]
        vh = y[:, 2 * DN_QK_W + h * DN_DV:2 * DN_QK_W + (h + 1) * DN_DV]
        qn = qh * lax.rsqrt(jnp.sum(qh * qh, axis=-1, keepdims=True) + EPS) * (DN_DK ** -0.5)
        kn = kh * lax.rsqrt(jnp.sum(kh * kh, axis=-1, keepdims=True) + EPS)
        beta = beta_all[:, h:h + 1]
        gc = gc_all[:, DN_HEADS + h:DN_HEADS + h + 1]
        egc = jnp.exp(gc)
        kb = kn * beta
        rhs_h = jnp.concatenate([vh * beta, kb * egc], axis=-1)
        qd_ref[:, hc] = (qn * egc).astype(qd_ref.dtype)
        for c in range(cs):
            rs = slice(c * chunk, (c + 1) * chunk)
            gcc = gc[rs]
            decays.append(jnp.exp(jnp.where(lower, gcc - gc_rows[h:h + 1, rs], NEG)))
            lhs.append(jnp.concatenate([kb[rs], qn[rs]], axis=0))
            kns.append(kn[rs])
            rhs.append(rhs_h[rs])
            kd_ref[rs, hc] = (kn[rs] * jnp.exp(gcc[chunk - 1:chunk, :] - gcc)).astype(kd_ref.dtype)
    decay = jnp.stack(decays)
    both = jnp.einsum('nid,njd->nij', jnp.stack(lhs).astype(BF16), jnp.stack(kns).astype(BF16),
                      preferred_element_type=F32)
    a = jnp.where(strict[None], both[:, :chunk] * decay, 0.0)
    qk = both[:, chunk:] * decay

    def bmm(p, q):
        return jnp.einsum('nij,njk->nik', p, q, preferred_element_type=F32)

    def split(v):
        hi = v.astype(BF16)
        return hi, (v - hi.astype(F32)).astype(BF16)

    def bmm3(p, q):
        (ph, pl_), (qh, ql) = p, q
        return bmm(ph, qh) + (bmm(ph, ql) + bmm(pl_, qh))

    pw_s = split(a)
    inv = jnp.where(r64 == c64, 1.0, 0.0)[None] - a
    for _ in range(int(math.log2(chunk)) - 1):
        pw_s = split(bmm3(pw_s, pw_s))
        inv = inv + bmm3(pw_s, split(inv))
    x = bmm3(split(inv), split(jnp.stack(rhs)))
    for h in range(DN_HEADS):
        hc = slice(h * DN_DV, (h + 1) * DN_DV)
        for c in range(cs):
            rs = slice(c * chunk, (c + 1) * chunk)
            n = h * cs + c
            u_ref[rs, hc] = x[n, :, :DN_DV]
            w_ref[rs, hc] = x[n, :, DN_DV:].astype(w_ref.dtype)
            qk_ref[rs, h * chunk:(h + 1) * chunk] = qk[n]


def _dn_scan_kernel(u_ref, w_ref, qd_ref, kd_ref, qk_ref, gl_ref, z_ref, s0_ref, nw_ref,
                    o_ref, sout_ref, s_ref, *, chunk, cs):
    @pl.when(pl.program_id(1) == 0)
    def _():
        s_ref[...] = s0_ref[...]

    def heads(ref, rs, width):
        return jnp.stack([ref[rs, h * width:(h + 1) * width] for h in range(DN_HEADS)])

    def bdot(spec, p, q):
        return jnp.einsum(spec, p.astype(BF16), q.astype(BF16), preferred_element_type=F32)

    for c in range(cs):
        rs = slice(c * chunk, (c + 1) * chunk)
        decay_last = jnp.exp(gl_ref[c])
        decay_last = jnp.stack([decay_last[:, DN_HEADS + h:DN_HEADS + h + 1] for h in range(DN_HEADS)])
        s = s_ref[...]
        sb = s.astype(BF16)
        v_new = heads(u_ref, rs, DN_DV) - bdot('hcd,hde->hce', heads(w_ref, rs, DN_DK), sb)
        o = (bdot('hcd,hde->hce', heads(qd_ref, rs, DN_DK), sb)
             + bdot('hcs,hse->hce', heads(qk_ref, rs, chunk), v_new))
        s_ref[...] = s * decay_last + bdot('hcd,hce->hde', heads(kd_ref, rs, DN_DK), v_new)
        on = o * lax.rsqrt(jnp.mean(o * o, axis=-1, keepdims=True) + EPS) * nw_ref[...]
        for h in range(DN_HEADS):
            hc = slice(h * DN_DV, (h + 1) * DN_DV)
            o_ref[rs, hc] = on[h] * _silu(z_ref[rs, hc])

    @pl.when(pl.program_id(1) == pl.num_programs(1) - 1)
    def _():
        sout_ref[...] = s_ref[...]


def _deltanet(proj, nb, seq, chunk, conv0_all, s0_all, state_layer, cw_all, par_all, nw_all, layer):
    m = nb * seq
    nchunks = seq // chunk
    width = DN_HEADS * DN_DV
    hs = (DN_HEADS, DN_DK, DN_DV)
    cs_a = 4 if nchunks % 4 == 0 else 1
    na = nchunks // cs_a
    ta = cs_a * chunk
    op_dtype = BF16 if chunk % (2 * SUBLANES) == 0 else F32
    u, w, qd, kd, qk, gl, conv_n = pl.pallas_call(
        functools.partial(_dn_prep_kernel, chunk=chunk, cs=cs_a),
        grid=(nb, na),
        in_specs=[pl.BlockSpec((ta, DN_CONV_CH), lambda b, c: (b * na + c, C_QKV // DN_CONV_CH)),
                  pl.BlockSpec((ta, LANES), lambda b, c: (b * na + c, C_SM // LANES)),
                  pl.BlockSpec((None, None, SUBLANES, DN_CONV_CH), lambda b, c: (state_layer, b, 0, 0)),
                  pl.BlockSpec((None, CONV_W, DN_CONV_CH), lambda b, c: (layer, 0, 0)),
                  pl.BlockSpec((None, 2, LANES), lambda b, c: (layer, 0, 0))],
        out_specs=[pl.BlockSpec((ta, width), lambda b, c: (b * na + c, 0)),
                   pl.BlockSpec((ta, width), lambda b, c: (b * na + c, 0)),
                   pl.BlockSpec((ta, width), lambda b, c: (b * na + c, 0)),
                   pl.BlockSpec((ta, width), lambda b, c: (b * na + c, 0)),
                   pl.BlockSpec((ta, DN_HEADS * chunk), lambda b, c: (b * na + c, 0)),
                   pl.BlockSpec((cs_a, 1, LANES), lambda b, c: (b * na + c, 0, 0)),
                   pl.BlockSpec((None, SUBLANES, DN_CONV_CH), lambda b, c: (b, 0, 0))],
        out_shape=[jax.ShapeDtypeStruct((m, width), F32),
                   jax.ShapeDtypeStruct((m, width), op_dtype),
                   jax.ShapeDtypeStruct((m, width), op_dtype),
                   jax.ShapeDtypeStruct((m, width), op_dtype),
                   jax.ShapeDtypeStruct((m, DN_HEADS * chunk), F32),
                   jax.ShapeDtypeStruct((nb * nchunks, 1, LANES), F32),
                   jax.ShapeDtypeStruct((nb, SUBLANES, DN_CONV_CH), F32)],
        scratch_shapes=[pltpu.VMEM((ta + SUBLANES, DN_CONV_CH), F32)],
        compiler_params=_params("parallel", "arbitrary"),
        name="dn_prep",
    )(proj, proj, conv0_all, cw_all, par_all)

    cs_b = 8 if nchunks % 8 == 0 else 1
    nbk = nchunks // cs_b
    tbk = cs_b * chunk
    row_spec = pl.BlockSpec((tbk, width), lambda b, c: (b * nbk + c, 0))
    o, s_n = pl.pallas_call(
        functools.partial(_dn_scan_kernel, chunk=chunk, cs=cs_b),
        grid=(nb, nbk),
        in_specs=[row_spec, row_spec, row_spec, row_spec,
                  pl.BlockSpec((tbk, DN_HEADS * chunk), lambda b, c: (b * nbk + c, 0)),
                  pl.BlockSpec((cs_b, 1, LANES), lambda b, c: (b * nbk + c, 0, 0)),
                  pl.BlockSpec((tbk, width), lambda b, c: (b * nbk + c, C_Z // width)),
                  pl.BlockSpec((None, None) + hs, lambda b, c: (state_layer, b, 0, 0, 0)),
                  pl.BlockSpec((None, 1, DN_DV), lambda b, c: (layer, 0, 0))],
        out_specs=[row_spec, pl.BlockSpec((None,) + hs, lambda b, c: (b, 0, 0, 0))],
        out_shape=[jax.ShapeDtypeStruct((m, width), F32), jax.ShapeDtypeStruct((nb,) + hs, F32)],
        scratch_shapes=[pltpu.VMEM(hs, F32)],
        compiler_params=_params("parallel", "arbitrary"),
        name="dn_scan",
    )(u, w, qd, kd, qk, gl, proj, s0_all, nw_all)
    return o, conv_n, s_n


def _pool_kernel(x_ref, pw_ref, o_ref):
    rows = x_ref.shape[0]
    x = x_ref[...].reshape(rows // NSA_BLOCK, NSA_BLOCK, 2 * NSA_KV_W)
    o_ref[...] = jnp.sum(x * pw_ref[...][None], axis=1)


def _pool_prompt(proj, nb, seq, pw_all, layer):
    rows = min(seq, 1024)
    nr = seq // rows
    width = 2 * NSA_KV_W
    return pl.pallas_call(
        _pool_kernel,
        grid=(nb, nr),
        in_specs=[pl.BlockSpec((rows, width), lambda b, r: (b * nr + r, C_CMP // width)),
                  pl.BlockSpec((None, NSA_BLOCK, width), lambda b, r: (layer, 0, 0))],
        out_specs=pl.BlockSpec((None, rows // NSA_BLOCK, width), lambda b, r: (b, r, 0)),
        out_shape=jax.ShapeDtypeStruct((nb, seq // NSA_BLOCK, width), F32),
        compiler_params=_params("parallel", "parallel"),
        name="pool_prompt",
    )(proj, pw_all)


def _pool_pages(page_refs, pw_ref, o_ref):
    rows = PAGE_SIZE * 4
    per_block = rows // 2 // SUBLANES
    for k, page_ref in enumerate(page_refs):
        prod = page_ref[...] * pw_ref[...]
        y = jnp.sum(prod.reshape(2, per_block, SUBLANES, NSA_HD), axis=1)
        y = y[:, 0:4, :] + y[:, 4:8, :]
        for s in range(4):
            o_ref[2 * k:2 * k + 2, s * NSA_HD:(s + 1) * NSA_HD] = y[:, s, :]


def _paged_pool_kernel(pt_ref, *refs, pages):
    del pt_ref
    _pool_pages(refs[:pages], refs[pages], refs[pages + 1])


def _in_proj_pool_kernel(pt_ref, x_ref, g_ref, w_ref, *refs, pages):
    del pt_ref
    o_ref, pooled_ref, xn_ref = refs[pages + 1:]
    _in_proj_kernel(x_ref, g_ref, w_ref, o_ref, xn_ref)
    _pool_pages(refs[:pages], refs[pages], pooled_ref)


def _in_proj_pool(x, g_all, w_all, cache4, page_table, pw_all, layer, tm, tn, pages):
    m = x.shape[0]
    nb, n_pages = page_table.shape
    assert (m // tm, N_PAD // tn) == (nb, n_pages // pages)
    rows = PAGE_SIZE * 4

    def page_spec(k):
        return pl.BlockSpec((None, None, rows, NSA_HD), lambda i, j, pt: (layer, pt[i, j * pages + k], 0, 0))

    grid_spec = pltpu.PrefetchScalarGridSpec(
        num_scalar_prefetch=1,
        grid=(m // tm, N_PAD // tn),
        in_specs=[pl.BlockSpec((tm, D_MODEL), lambda i, j, pt: (i, 0)),
                  pl.BlockSpec((None, 1, D_MODEL), lambda i, j, pt: (layer, 0, 0)),
                  pl.BlockSpec((None, D_MODEL, tn), lambda i, j, pt: (layer, 0, j))]
        + [page_spec(k) for k in range(pages)]
        + [pl.BlockSpec((None, rows, NSA_HD), lambda i, j, pt: (layer, 0, 0))],
        out_specs=[pl.BlockSpec((tm, tn), lambda i, j, pt: (i, j)),
                   pl.BlockSpec((None, 2 * pages, 4 * NSA_HD), lambda i, j, pt: (i, j, 0))],
        scratch_shapes=[pltpu.VMEM((tm, D_MODEL), BF16)],
    )
    return pl.pallas_call(
        functools.partial(_in_proj_pool_kernel, pages=pages),
        grid_spec=grid_spec,
        out_shape=[jax.ShapeDtypeStruct((m, N_PAD), F32),
                   jax.ShapeDtypeStruct((nb, 2 * n_pages, 4 * NSA_HD), F32)],
        compiler_params=_params("parallel", "arbitrary"),
        name="in_proj_pool",
    )(page_table, x, g_all, w_all, *([cache4] * pages), pw_all)


def _pool_paged(cache4, page_table, pw_all, layer, pages):
    nb, n_pages = page_table.shape
    steps = n_pages // pages
    rows = PAGE_SIZE * 4

    def page_spec(k):
        return pl.BlockSpec((None, None, rows, NSA_HD),
                            lambda b, s, pt: (layer, pt[b, s * pages + k], 0, 0))

    grid_spec = pltpu.PrefetchScalarGridSpec(
        num_scalar_prefetch=1,
        grid=(nb, steps),
        in_specs=[page_spec(k) for k in range(pages)]
        + [pl.BlockSpec((None, rows, NSA_HD), lambda b, s, pt: (layer, 0, 0))],
        out_specs=pl.BlockSpec((None, 2 * pages, 4 * NSA_HD), lambda b, s, pt: (b, s, 0)),
    )
    return pl.pallas_call(
        functools.partial(_paged_pool_kernel, pages=pages),
        grid_spec=grid_spec,
        out_shape=jax.ShapeDtypeStruct((nb, 2 * n_pages, 4 * NSA_HD), F32),
        compiler_params=_params("parallel", "arbitrary"),
        name="pool_paged",
    )(page_table, *([cache4] * pages), pw_all)


def _cmp_mlp_kernel(p_ref, w1_ref, w2_ref, o_ref):
    for s in range(4):
        kv = s // NSA_KV_HEADS
        h = _silu(_dot_hi(p_ref[:, s * NSA_HD:(s + 1) * NSA_HD], w1_ref[kv]))
        o_ref[:, s * NSA_HD:(s + 1) * NSA_HD] = _dot_hi(h, w2_ref[kv])


def _cmp_mlp(pooled, w1_all, w2_all, layer):
    nb, n, width = pooled.shape
    wspec = pl.BlockSpec((None, 2, NSA_HD, NSA_HD), lambda b: (layer, 0, 0, 0))
    return pl.pallas_call(
        _cmp_mlp_kernel,
        grid=(nb,),
        in_specs=[pl.BlockSpec((None, n, width), lambda b: (b, 0, 0)), wspec, wspec],
        out_specs=pl.BlockSpec((None, n, width), lambda b: (b, 0, 0)),
        out_shape=jax.ShapeDtypeStruct((nb, n, width), F32),
        compiler_params=_params("parallel"),
        name="cmp_mlp",
    )(pooled, w1_all, w2_all)


def _select_blocks(score, st_ref, nbp, n_causal):
    st_ref[...] = score.T
    halves = [st_ref[g * nbp:(g + 1) * nbp, :] for g in range(NSA_KV_HEADS)]
    bi = lax.broadcasted_iota(jnp.int32, halves[0].shape, 0)

    def body(j, ranks):
        tie = jnp.where(j < bi, 1, 0)
        out = []
        for g in range(NSA_KV_HEADS):
            other = st_ref[pl.ds(g * nbp + j, 1), :]
            out.append(ranks[g] + jnp.where(other > halves[g], 1, jnp.where(other == halves[g], tie, 0)))
        return tuple(out)

    ranks = lax.fori_loop(0, n_causal, body, tuple(jnp.zeros(bi.shape, jnp.int32) for _ in halves))
    keep = [jnp.where((ranks[g] < NSA_TOPN) & (halves[g] >= 0.0), 1.0, 0.0) for g in range(NSA_KV_HEADS)]
    return jnp.concatenate(keep, axis=0).T


def _cmp_attn_kernel(q_ref, ckv_ref, bias_ref, o_ref, sc_ref, *scratch, pos0, tq, nb, nbs, nbp, select):
    qi = pl.program_id(1)
    t = pos0 + qi * tq + lax.broadcasted_iota(jnp.int32, (tq, 1), 0)
    blk = lax.broadcasted_iota(jnp.int32, (1, nb), 1)
    valid = t >= blk * NSA_BLOCK + (NSA_BLOCK - 1)
    bi = lax.broadcasted_iota(jnp.int32, (1, nbp), 1)
    cur = t // NSA_BLOCK
    forced = (bi == 0) | (bi == cur) | (bi == cur - 1)
    causal = bi <= cur
    scores = []
    for g in range(NSA_KV_HEADS):
        ck = ckv_ref[:, g * NSA_HD:(g + 1) * NSA_HD]
        cv = ckv_ref[:, (NSA_KV_HEADS + g) * NSA_HD:(NSA_KV_HEADS + g + 1) * NSA_HD]
        imp = jnp.zeros((tq, nb), F32)
        for r in range(NSA_GROUP):
            h = g * NSA_GROUP + r
            logits = _dot_nt(q_ref[:, h * NSA_HD:(h + 1) * NSA_HD], ck) * SCALE + bias_ref[h]
            logits = jnp.where(valid, logits, NEG)
            mx = jnp.max(logits, axis=-1, keepdims=True)
            p = jnp.where(valid, jnp.exp(logits - mx), 0.0)
            p = p / jnp.maximum(jnp.sum(p, axis=-1, keepdims=True), 1e-30)
            o_ref[:, h * NSA_HD:(h + 1) * NSA_HD] = _dot(p, cv)
            imp = imp + p
        if nbp > nb:
            imp = jnp.concatenate([imp, jnp.zeros((tq, nbp - nb), F32)], axis=-1)
        score = jnp.where(causal, jnp.where(forced, NSA_GROUP + 1.0, imp), -1.0)
        scores.append(jnp.where(bi < nbs, score, -2.0))
    score = jnp.concatenate(scores, axis=-1)
    if select:
        n_causal = jnp.minimum((pos0 + (qi + 1) * tq - 1) // NSA_BLOCK + 1, nbs)
        sc_ref[...] = _select_blocks(score, scratch[0], nbp, n_causal)
    else:
        sc_ref[...] = score


def _cmp_attn(proj, ckv, bias_cmp, nb_batch, seq, pos0, tq, nbs, nbp, select):
    nq = seq // tq
    nb = ckv.shape[1]
    m = nb_batch * seq
    bias_tq = bias_cmp.shape[1] // nq
    return pl.pallas_call(
        functools.partial(_cmp_attn_kernel, pos0=pos0, tq=tq, nb=nb, nbs=nbs, nbp=nbp, select=select),
        scratch_shapes=[pltpu.VMEM((NSA_KV_HEADS * nbp, tq), F32)] if select else [],
        grid=(nb_batch, nq),
        in_specs=[pl.BlockSpec((tq, NSA_WIDTH), lambda b, i: (b * nq + i, C_NQ // NSA_WIDTH)),
                  pl.BlockSpec((None, nb, 2 * NSA_KV_W), lambda b, i: (b, 0, 0)),
                  pl.BlockSpec((NSA_HEADS, bias_tq, nb), lambda b, i: (0, i, 0))],
        out_specs=[pl.BlockSpec((tq, NSA_WIDTH), lambda b, i: (b * nq + i, 0)),
                   pl.BlockSpec((tq, NSA_KV_HEADS * nbp), lambda b, i: (b * nq + i, 0))],
        out_shape=[jax.ShapeDtypeStruct((m, NSA_WIDTH), F32),
                   jax.ShapeDtypeStruct((m, NSA_KV_HEADS * nbp), F32)],
        compiler_params=_params("parallel", "parallel"),
        name="cmp_attn",
    )(proj, ckv, bias_cmp)


def _topk_kernel(s_ref, sel_ref, idx_ref, val_ref, *, nbs):
    s = s_ref[...]
    bi = lax.broadcasted_iota(jnp.int32, s.shape, 0)

    def body(j, rank):
        other = s_ref[pl.ds(j, 1), :]
        beats = (other > s) | ((other == s) & (j < bi))
        return rank + beats.astype(jnp.int32)

    rank = lax.fori_loop(0, nbs, body, jnp.zeros(s.shape, jnp.int32))
    ok = s >= 0.0
    sel_ref[...] = jnp.where((rank < NSA_TOPN) & ok, 1.0, 0.0)
    for r in range(NSA_TOPN):
        hit = rank == r
        idx_ref[r:r + 1, :] = jnp.sum(jnp.where(hit, bi, 0), axis=0, keepdims=True)
        val_ref[r:r + 1, :] = jnp.sum(jnp.where(hit & ok, 1, 0), axis=0, keepdims=True)


def _topk(score_t, nbs):
    rows, cols = score_t.shape
    tc = min(cols, 512)
    return pl.pallas_call(
        functools.partial(_topk_kernel, nbs=nbs),
        grid=(cols // tc,),
        in_specs=[pl.BlockSpec((rows, tc), lambda i: (0, i))],
        out_specs=[pl.BlockSpec((rows, tc), lambda i: (0, i)),
                   pl.BlockSpec((NSA_TOPN, tc), lambda i: (0, i)),
                   pl.BlockSpec((NSA_TOPN, tc), lambda i: (0, i))],
        out_shape=[jax.ShapeDtypeStruct((rows, cols), F32),
                   jax.ShapeDtypeStruct((NSA_TOPN, cols), jnp.int32),
                   jax.ShapeDtypeStruct((NSA_TOPN, cols), jnp.int32)],
        compiler_params=_params("parallel"),
        name="topk",
    )(score_t)


PQ = 256
PK = 512
BT = LANES


def _softmax_rows(s, ok):
    mx = jnp.max(s, axis=-1, keepdims=True)
    p = jnp.where(ok, jnp.exp(s - mx), 0.0)
    return p, jnp.sum(p, axis=-1, keepdims=True)


def _nsa_prompt_kernel(q_ref, ks_ref, vs_ref, kw_ref, vw_ref, sel_ref, bt_ref, ocmp_ref, z_ref, gate_ref,
                       o_ref, s2_ref, p2_ref, sw_ref, pw_ref, m_ref, acc_ref, *, n_bt, wsub):
    qi = pl.program_id(2)
    rows = NSA_GROUP * PQ
    qsub = PQ // BT
    ksub = PK // BT
    q4 = jnp.concatenate([q_ref[:, r * NSA_HD:(r + 1) * NSA_HD] for r in range(NSA_GROUP)], axis=0)
    q4 = (q4 * (SCALE * LOG2E)).astype(BF16)
    t = qi * PQ + lax.broadcasted_iota(jnp.int32, (PQ, 1), 0)
    nsel = sel_ref.shape[1]
    blk0 = pl.program_id(1) * (nsel // NSA_KV_HEADS)
    unselected = jnp.where(sel_ref[...] > 0.5, 0.0, NEG).astype(BF16)
    qx = jnp.concatenate([q4, jnp.concatenate([unselected] * NSA_GROUP, axis=0)], axis=1)

    def row_groups():
        for r in range(NSA_GROUP):
            for a in range(qsub):
                yield slice(r * PQ + a * BT, r * PQ + (a + 1) * BT), r, a

    def softmax_block(s_ref, p_ref, rs, r, tile_of, extra, width, m_old, live=None, edge=None):
        live = range(width) if live is None else live
        edge = live if edge is None else edge
        zmax = None
        for c in live:
            cs = slice(c * BT, (c + 1) * BT)
            z = s_ref[rs, cs] + bt_ref[r, jnp.clip(tile_of(c), 0, N_BIAS_TILES - 1)]
            if extra is not None and c in edge:
                z = z + extra[:, cs]
            s_ref[rs, cs] = z
            zmax = z if zmax is None else jnp.maximum(zmax, z)
        mx = jnp.broadcast_to(jnp.max(zmax, axis=-1, keepdims=True), (BT, BT))
        if m_old is not None:
            mx = jnp.maximum(m_old, mx)
        for c in range(width):
            cs = slice(c * BT, (c + 1) * BT)
            if c in live:
                p_ref[rs, cs] = jnp.exp2(s_ref[rs, cs] - mx).astype(BF16)
            else:
                p_ref[rs, cs] = jnp.zeros((BT, BT), BF16)
        return mx

    def with_ones(v):
        return jnp.concatenate([v.astype(BF16), jnp.ones(v.shape, BF16)], axis=1)

    m_ref[...] = jnp.full((rows, BT), NEG, F32)
    acc_ref[...] = jnp.zeros((rows, 2 * NSA_HD), F32)

    def key_tile(kt, diagonal, slot, qoff=0):
        s_ref, p_ref = s2_ref.at[slot], p2_ref.at[slot]
        k0 = pl.multiple_of(kt * PK, PK)
        member = (lax.broadcasted_iota(jnp.int32, (PK, nsel), 1)
                  == blk0 + kt * (PK // NSA_BLOCK) + lax.broadcasted_iota(jnp.int32, (PK, nsel), 0) // NSA_BLOCK)
        kx = jnp.concatenate([ks_ref[pl.ds(k0, PK), :].astype(BF16), jnp.where(member, 1.0, 0.0).astype(BF16)],
                             axis=1)
        s_ref[...] = lax.dot_general(qx, kx, (((1,), (1,)), ((), ())), preferred_element_type=F32)
        future = None
        if diagonal:
            future = jnp.where(k0 + lax.broadcasted_iota(jnp.int32, (1, PK), 1) <= t, 0.0, NEG)
        for rs, r, a in row_groups():
            m_old = m_ref[rs]
            tile_of = lambda c: qi * qsub + a - (kt * ksub + c)
            if diagonal:
                own = qoff + a
                mx = softmax_block(s_ref, p_ref, rs, r, tile_of, future[a * BT:(a + 1) * BT], ksub, m_old,
                                   live=range(own + 1), edge=(own,))
            else:
                mx = softmax_block(s_ref, p_ref, rs, r, tile_of, None, ksub, m_old)
            alpha = jnp.exp2(m_old - mx)
            acc_ref[rs] = jnp.concatenate([alpha, alpha], axis=1) * acc_ref[rs]
            m_ref[rs] = mx
        acc_ref[...] += jnp.dot(p_ref[...], with_ones(vs_ref[pl.ds(k0, PK), :]), preferred_element_type=F32)

    last = (qi * PQ) // PK

    def body(j, carry):
        key_tile(2 * j, False, 0)
        key_tile(2 * j + 1, False, 1)
        return carry

    lax.fori_loop(0, last // 2, body, 0)

    @pl.when(last % 2 == 1)
    def _():
        key_tile(last - 1, False, 0)

    for case in range(PK // PQ):
        @pl.when((qi * qsub) % ksub == case * qsub)
        def _(case=case):
            key_tile(last, True, 1, qoff=case * qsub)

    o_slc = acc_ref[:, :NSA_HD] / jnp.maximum(acc_ref[:, NSA_HD:], 1e-30)

    wk = wsub * BT
    kst = jnp.clip(qi * qsub - WINDOW // BT, 0, n_bt - wsub)
    k0 = pl.multiple_of(kst * BT, BT)
    sw_ref[...] = lax.dot_general(q4, kw_ref[pl.ds(k0, wk), :].astype(BF16), (((1,), (1,)), ((), ())),
                                  preferred_element_type=F32)
    dist = t - (k0 + lax.broadcasted_iota(jnp.int32, (1, wk), 1))
    outside = jnp.where((dist >= 0) & (dist <= WINDOW), 0.0, NEG)
    wband = WINDOW // BT

    def window_softmax(banded):
        for rs, r, a in row_groups():
            live = range(a, a + wband + 1) if banded else None
            edge = (a, a + wband) if banded else None
            softmax_block(sw_ref, pw_ref, rs, r, lambda c: qi * qsub + a - (kst + c),
                          outside[a * BT:(a + 1) * BT], wsub, None, live=live, edge=edge)

    if wsub == wband + qsub:
        pl.when(qi * qsub >= wband)(lambda: window_softmax(True))
        pl.when(qi * qsub < wband)(lambda: window_softmax(False))
    else:
        window_softmax(False)
    pv = jnp.dot(pw_ref[...], with_ones(vw_ref[pl.ds(k0, wk), :]), preferred_element_type=F32)
    o_win = pv[:, :NSA_HD] / jnp.maximum(pv[:, NSA_HD:], 1e-30)

    gates = jax.nn.sigmoid(gate_ref[...])
    for r in range(NSA_GROUP):
        sl = slice(r * NSA_HD, (r + 1) * NSA_HD)
        rs = slice(r * PQ, (r + 1) * PQ)
        o = (gates[:, 3 * r:3 * r + 1] * ocmp_ref[:, sl] + gates[:, 3 * r + 1:3 * r + 2] * o_slc[rs]
             + gates[:, 3 * r + 2:3 * r + 3] * o_win[rs])
        o_ref[:, sl] = o * _silu(z_ref[:, sl])


def _nsa_prompt(proj, sel, bias_tiles, o_cmp, nb_batch, seq):
    nq = seq // PQ
    n_bt = seq // BT
    wsub = min((WINDOW + PQ) // BT, n_bt)
    gw = NSA_GROUP * NSA_HD
    rows = NSA_GROUP * PQ
    nsel = sel.shape[-1]

    def kv_spec(col):
        return pl.BlockSpec((seq, NSA_HD), lambda b, g, i: (b, col // NSA_HD + g))

    return pl.pallas_call(
        functools.partial(_nsa_prompt_kernel, n_bt=n_bt, wsub=wsub),
        grid=(nb_batch, NSA_KV_HEADS, nq),
        in_specs=[pl.BlockSpec((PQ, gw), lambda b, g, i: (b * nq + i, C_NQ // gw + g)),
                  kv_spec(C_SLC), kv_spec(C_SLC + NSA_KV_W), kv_spec(C_WIN), kv_spec(C_WIN + NSA_KV_W),
                  pl.BlockSpec((PQ, nsel), lambda b, g, i: (b * nq + i, 0)),
                  pl.BlockSpec((NSA_GROUP, N_BIAS_TILES, BT, BT), lambda b, g, i: (g, 0, 0, 0)),
                  pl.BlockSpec((PQ, gw), lambda b, g, i: (b * nq + i, g)),
                  pl.BlockSpec((PQ, gw), lambda b, g, i: (b * nq + i, C_NZ // gw + g)),
                  pl.BlockSpec((PQ, LANES), lambda b, g, i: (b * nq + i, C_G0 // LANES + g))],
        out_specs=pl.BlockSpec((PQ, gw), lambda b, g, i: (b * nq + i, g)),
        out_shape=jax.ShapeDtypeStruct((nb_batch * seq, NSA_WIDTH), F32),
        scratch_shapes=[pltpu.VMEM((2, rows, PK), F32), pltpu.VMEM((2, rows, PK), BF16),
                        pltpu.VMEM((rows, wsub * BT), F32), pltpu.VMEM((rows, wsub * BT), BF16),
                        pltpu.VMEM((rows, BT), F32), pltpu.VMEM((rows, 2 * NSA_HD), F32)],
        compiler_params=_params("parallel", "parallel", "arbitrary"),
        name="nsa_prompt",
    )(proj, proj, proj, proj, proj, sel, bias_tiles, o_cmp, proj, proj)


def _bias_by_threshold(dist, tcols):
    bias = jnp.broadcast_to(tcols[:, 0:1], (tcols.shape[0], dist.shape[-1]))
    for b in range(1, NUM_BUCKETS):
        bias = jnp.where(dist >= _BUCKET_THR[b - 1], tcols[:, b:b + 1], bias)
    return bias


SLOT_CHUNKS = NSA_TOPN + 2


def _slc_sample_kernel(pt_ref, idx_ref, val_ref, *refs, past, lq):
    del pt_ref
    n_in = NSA_KV_HEADS * NSA_TOPN
    page_refs = refs[:n_in]
    q_ref, knew_ref, tcol_ref, o_ref, kc_ref, vc_ref = refs[n_in:]
    b = pl.program_id(0)
    qi = pl.program_id(1)
    t = past + qi
    n_past_blocks = past // NSA_BLOCK
    nk = SLOT_CHUNKS * NSA_BLOCK
    lane = lax.broadcasted_iota(jnp.int32, (1, nk), 1)
    slot = lane // NSA_BLOCK
    this_q = lax.broadcasted_iota(jnp.int32, (lq, 1), 0) == qi
    for g in range(NSA_KV_HEADS):
        base = ((b * lq + qi) * NSA_KV_HEADS + g) * NSA_TOPN
        blk_of_lane = jnp.where(slot == NSA_TOPN, n_past_blocks, n_past_blocks + 1)
        use_of_lane = jnp.where(slot == NSA_TOPN, 1, 0)
        for j in range(NSA_TOPN):
            n_j = idx_ref[base + j]
            use = jnp.where((val_ref[base + j] > 0) & (n_j < n_past_blocks), 1, 0)
            blk_of_lane = jnp.where(slot == j, n_j, blk_of_lane)
            use_of_lane = jnp.where(slot == j, use, use_of_lane)
            page = page_refs[g * NSA_TOPN + j]
            kc_ref[j * NSA_BLOCK:(j + 1) * NSA_BLOCK, :] = page[pl.ds(g, NSA_BLOCK, stride=4), :]
            vc_ref[j * NSA_BLOCK:(j + 1) * NSA_BLOCK, :] = page[pl.ds(NSA_KV_HEADS + g, NSA_BLOCK, stride=4), :]
        new0 = NSA_TOPN * NSA_BLOCK
        kc_ref[new0:, :] = jnp.zeros((nk - new0, NSA_HD), F32)
        vc_ref[new0:, :] = jnp.zeros((nk - new0, NSA_HD), F32)
        kc_ref[new0:new0 + lq, :] = knew_ref[:, g * NSA_HD:(g + 1) * NSA_HD]
        vc_ref[new0:new0 + lq, :] = knew_ref[:, (NSA_KV_HEADS + g) * NSA_HD:(NSA_KV_HEADS + g + 1) * NSA_HD]
        q4 = jnp.concatenate(
            [jnp.sum(jnp.where(this_q, q_ref[:, (g * NSA_GROUP + r) * NSA_HD:(g * NSA_GROUP + r + 1) * NSA_HD], 0.0),
                     axis=0, keepdims=True)
             for r in range(NSA_GROUP)] + [jnp.zeros((SUBLANES - NSA_GROUP, NSA_HD), F32)], axis=0)
        dist = t - (blk_of_lane * NSA_BLOCK + lane % NSA_BLOCK)
        ok = (use_of_lane > 0) & (dist >= 0)
        s = _dot_nt(q4, kc_ref[...]) * SCALE + _bias_by_threshold(dist, tcol_ref[g])
        s = jnp.where(ok, s, NEG)
        p, den = _softmax_rows(s, ok)
        o = _dot(p, vc_ref[...]) / jnp.maximum(den, 1e-30)
        for r in range(NSA_GROUP):
            h = g * NSA_GROUP + r
            o_ref[:, h * NSA_HD:(h + 1) * NSA_HD] = o[r:r + 1, :]


def _slc_sample(proj, cache4, page_table, idx_flat, val_flat, tcols, layer, nb_batch, lq, past):
    rows = NSA_BLOCK * 4
    n_past_blocks = past // NSA_BLOCK

    n_sel = NSA_KV_HEADS * NSA_TOPN
    blocks = jnp.minimum(idx_flat.reshape(nb_batch, lq * n_sel), n_past_blocks - 1)
    pages = jnp.take_along_axis(page_table, blocks // 2, axis=1)
    loc = jnp.concatenate([pages.reshape(nb_batch * lq, n_sel), (blocks % 2).reshape(nb_batch * lq, n_sel)], axis=1)

    def page_spec(g, j):
        def index_map(b, i, loc, idx, val):
            row = b * lq + i
            return (layer, loc[row, g * NSA_TOPN + j], loc[row, n_sel + g * NSA_TOPN + j], 0)
        return pl.BlockSpec((None, None, rows, NSA_HD), index_map)

    grid_spec = pltpu.PrefetchScalarGridSpec(
        num_scalar_prefetch=3,
        grid=(nb_batch, lq),
        in_specs=[page_spec(g, j) for g in range(NSA_KV_HEADS) for j in range(NSA_TOPN)]
        + [pl.BlockSpec((lq, NSA_WIDTH), lambda b, i, *_: (b, C_NQ // NSA_WIDTH)),
           pl.BlockSpec((lq, 2 * NSA_KV_W), lambda b, i, *_: (b, C_SLC // (2 * NSA_KV_W))),
           pl.BlockSpec((NSA_KV_HEADS, SUBLANES, NUM_BUCKETS), lambda b, i, *_: (0, 0, 0))],
        out_specs=pl.BlockSpec((None, 1, NSA_WIDTH), lambda b, i, *_: (b * lq + i, 0, 0)),
        scratch_shapes=[pltpu.VMEM((SLOT_CHUNKS * NSA_BLOCK, NSA_HD), F32),
                        pltpu.VMEM((SLOT_CHUNKS * NSA_BLOCK, NSA_HD), F32)],
    )
    out = pl.pallas_call(
        functools.partial(_slc_sample_kernel, past=past, lq=lq),
        grid_spec=grid_spec,
        out_shape=jax.ShapeDtypeStruct((nb_batch * lq, 1, NSA_WIDTH), F32),
        compiler_params=_params("parallel", "arbitrary"),
        name="slc_sample",
    )(loc, idx_flat, val_flat, *([cache4] * n_sel), proj, proj, tcols)
    return out.reshape(nb_batch * lq, NSA_WIDTH)


def _win_sample_kernel(q_ref, wpast_ref, wnew_ref, tcol_ref, ocmp_ref, oslc_ref, z_ref, g0_ref, g1_ref,
                       o_ref, kc_ref, vc_ref, *, past, lq, wlen):
    rows = NSA_GROUP * lq
    nk = kc_ref.shape[0]
    lane = lax.broadcasted_iota(jnp.int32, (1, nk), 1)
    kpos = past - wlen + lane
    t = past + lax.broadcasted_iota(jnp.int32, (rows, 1), 0) % lq
    dist = t - kpos
    ok = (dist >= 0) & (dist <= WINDOW) & (kpos >= 0) & (lane < wlen + lq)
    gate_refs = (g0_ref, g1_ref)
    for g in range(NSA_KV_HEADS):
        kc_ref[0:wlen, :] = wpast_ref[pl.ds(g, wlen, stride=4), :]
        vc_ref[0:wlen, :] = wpast_ref[pl.ds(NSA_KV_HEADS + g, wlen, stride=4), :]
        kc_ref[wlen:, :] = jnp.zeros((nk - wlen, NSA_HD), F32)
        vc_ref[wlen:, :] = jnp.zeros((nk - wlen, NSA_HD), F32)
        kc_ref[wlen:wlen + lq, :] = wnew_ref[:, g * NSA_HD:(g + 1) * NSA_HD]
        vc_ref[wlen:wlen + lq, :] = wnew_ref[:, (NSA_KV_HEADS + g) * NSA_HD:(NSA_KV_HEADS + g + 1) * NSA_HD]
        q4 = jnp.concatenate([q_ref[:, (g * NSA_GROUP + r) * NSA_HD:(g * NSA_GROUP + r + 1) * NSA_HD]
                              for r in range(NSA_GROUP)], axis=0)
        s = _dot_nt(q4, kc_ref[...]) * SCALE + _bias_by_threshold(dist, tcol_ref[g])
        s = jnp.where(ok, s, NEG)
        p, den = _softmax_rows(s, ok)
        o_win = _dot(p, vc_ref[...]) / jnp.maximum(den, 1e-30)
        gates = jax.nn.sigmoid(gate_refs[g][...])
        for r in range(NSA_GROUP):
            h = g * NSA_GROUP + r
            sl = slice(h * NSA_HD, (h + 1) * NSA_HD)
            o = (gates[:, 3 * r:3 * r + 1] * ocmp_ref[:, sl] + gates[:, 3 * r + 1:3 * r + 2] * oslc_ref[:, sl]
                 + gates[:, 3 * r + 2:3 * r + 3] * o_win[r * lq:(r + 1) * lq])
            o_ref[:, sl] = o * _silu(z_ref[:, sl])


def _win_sample(proj, win4, tcols_rows, o_cmp, o_slc, layer, nb_batch, lq, past):
    wlen = win4.shape[2] // 4
    nk = -(-(wlen + lq) // LANES) * LANES
    row_spec = pl.BlockSpec((lq, NSA_WIDTH), lambda b: (b, 0))
    return pl.pallas_call(
        functools.partial(_win_sample_kernel, past=past, lq=lq, wlen=wlen),
        grid=(nb_batch,),
        in_specs=[pl.BlockSpec((lq, NSA_WIDTH), lambda b: (b, C_NQ // NSA_WIDTH)),
                  pl.BlockSpec((None, None, wlen * 4, NSA_HD), lambda b: (layer, b, 0, 0)),
                  pl.BlockSpec((lq, 2 * NSA_KV_W), lambda b: (b, C_WIN // (2 * NSA_KV_W))),
                  pl.BlockSpec((NSA_KV_HEADS, NSA_GROUP * lq, NUM_BUCKETS), lambda b: (0, 0, 0)),
                  row_spec, row_spec,
                  pl.BlockSpec((lq, NSA_WIDTH), lambda b: (b, C_NZ // NSA_WIDTH)),
                  pl.BlockSpec((lq, LANES), lambda b: (b, C_G0 // LANES)),
                  pl.BlockSpec((lq, LANES), lambda b: (b, C_G1 // LANES))],
        out_specs=row_spec,
        out_shape=jax.ShapeDtypeStruct((nb_batch * lq, NSA_WIDTH), F32),
        scratch_shapes=[pltpu.VMEM((nk, NSA_HD), F32), pltpu.VMEM((nk, NSA_HD), F32)],
        compiler_params=_params("parallel"),
        name="win_sample",
    )(proj, win4, proj, tcols_rows, o_cmp, o_slc, proj, proj, proj)


def _out_kernel(dn_ref, nsa_ref, x_ref, p_ref, wo_ref, gpost_ref, wpp_ref, gple_ref, wpg_ref, o_ref):
    half = dn_ref.shape[1]
    mixed = (jnp.dot(dn_ref[...].astype(BF16), wo_ref[0:half, :], preferred_element_type=F32)
             + jnp.dot(nsa_ref[...].astype(BF16), wo_ref[half:, :], preferred_element_type=F32))
    y = mixed * lax.rsqrt(jnp.mean(mixed * mixed, axis=-1, keepdims=True) + EPS) * gpost_ref[...]
    x1 = x_ref[...] + y
    e = jnp.dot(p_ref[...].astype(BF16), wpp_ref[...], preferred_element_type=F32)
    e = e * lax.rsqrt(jnp.mean(e * e, axis=-1, keepdims=True) + EPS) * gple_ref[...]
    gate = jax.nn.sigmoid(jnp.dot(x1.astype(BF16), wpg_ref[...], preferred_element_type=F32))
    o_ref[...] = x1 + gate * e


def _out_proj(dn, nsa, x, p_all, wo_all, gpost_all, wpp_all, gple_all, wpg_all, layer, tm):
    m = x.shape[0]
    once = pl.Buffered(1)

    def wspec(k):
        return pl.BlockSpec((None, k, D_MODEL), lambda i: (layer, 0, 0), pipeline_mode=once)

    return pl.pallas_call(
        _out_kernel,
        grid=(m // tm,),
        in_specs=[pl.BlockSpec((tm, dn.shape[1]), lambda i: (i, 0)),
                  pl.BlockSpec((tm, nsa.shape[1]), lambda i: (i, 0)),
                  pl.BlockSpec((tm, D_MODEL), lambda i: (i, 0)),
                  pl.BlockSpec((None, tm, PLE_DIM), lambda i: (layer, i, 0)),
                  wspec(D_MODEL), wspec(1), wspec(PLE_DIM), wspec(1), wspec(D_MODEL)],
        out_specs=pl.BlockSpec((tm, D_MODEL), lambda i: (i, 0)),
        out_shape=jax.ShapeDtypeStruct((m, D_MODEL), F32),
        compiler_params=_params("parallel"),
        name="out_proj",
    )(dn, nsa, x, p_all, wo_all, gpost_all, wpp_all, gple_all, wpg_all)


def _kv_rows_kernel(*refs, depth):
    x_refs = refs[:depth]
    cmp_ref, slc_ref, win_ref = refs[depth:]
    tm = x_refs[0].shape[0]
    width = 2 * NSA_KV_W

    def rows(o_ref, x_ref, seg):
        for s in range(4):
            o_ref[pl.ds(s, tm, stride=4), :] = x_ref[:, seg * width + s * NSA_HD:seg * width + (s + 1) * NSA_HD]

    for layer in range(depth):
        @pl.when(pl.program_id(0) == layer)
        def _(x_ref=x_refs[layer]):
            rows(cmp_ref, x_ref, 0)
            rows(slc_ref, x_ref, 1)

            @pl.when(pl.program_id(2) == pl.num_programs(2) - 1)
            def _():
                rows(win_ref, x_ref, 2)


def _kv_rows(projs, nb, seq):
    depth = len(projs)
    tm = min(WINDOW, seq)
    nr = seq // tm
    width = 3 * 2 * NSA_KV_W

    def x_spec(layer):
        return pl.BlockSpec((tm, width), lambda d, b, r: (jnp.where(d == layer, b * nr + r, 0), C_CMP // width))

    return pl.pallas_call(
        functools.partial(_kv_rows_kernel, depth=depth),
        grid=(depth, nb, nr),
        in_specs=[x_spec(layer) for layer in range(depth)],
        out_specs=[pl.BlockSpec((None, tm * 4, NSA_HD), lambda d, b, r: (d, b * nr + r, 0)),
                   pl.BlockSpec((None, tm * 4, NSA_HD), lambda d, b, r: (d, b * nr + r, 0)),
                   pl.BlockSpec((None, None, tm * 4, NSA_HD), lambda d, b, r: (d, b, 0, 0))],
        out_shape=[jax.ShapeDtypeStruct((depth, nb * seq * 4, NSA_HD), F32),
                   jax.ShapeDtypeStruct((depth, nb * seq * 4, NSA_HD), F32),
                   jax.ShapeDtypeStruct((depth, nb, tm * 4, NSA_HD), F32)],
        compiler_params=_params("arbitrary", "arbitrary", "arbitrary"),
        name="kv_rows",
    )(*projs)


W_TILE = 512


def _relayout_w_in_kernel(wt_ref, sm_ref, gate_ref, o_ref, tail_ref):
    is_tail = pl.program_id(1) == pl.num_programs(1) - 1

    @pl.when(jnp.logical_not(is_tail))
    def _():
        o_ref[...] = wt_ref[0].T.astype(BF16)

    @pl.when(is_tail)
    def _():
        half = 3 * NSA_GROUP
        tail_ref[...] = jnp.zeros(tail_ref.shape, F32)
        tail_ref[0:2 * DN_HEADS, :] = sm_ref[0]
        tail_ref[C_G0 - C_SM:C_G0 - C_SM + half, :] = gate_ref[0, 0:half, :]
        tail_ref[C_G1 - C_SM:C_G1 - C_SM + half, :] = gate_ref[0, half:2 * half, :]
        o_ref[...] = tail_ref[...].T.astype(BF16)


def _relayout_w_in(w_in):
    depth, k, _ = w_in.shape
    wt = jnp.swapaxes(w_in, 1, 2)
    n_gate = 2 * 3 * NSA_GROUP
    n_tiles = N_PAD // W_TILE
    moved = C_NQ // W_TILE

    def src_row(d, j):
        j = jnp.minimum(j, n_tiles - 2)
        return (d, pl.multiple_of(j * W_TILE + jnp.where(j >= moved, _O_NQ - C_NQ, 0), SUBLANES), 0)

    return pl.pallas_call(
        _relayout_w_in_kernel,
        grid=(depth, n_tiles),
        in_specs=[pl.BlockSpec((pl.Element(1), pl.Element(W_TILE), pl.Element(k)), src_row),
                  pl.BlockSpec((pl.Element(1), pl.Element(2 * DN_HEADS), pl.Element(k)), lambda d, j: (d, _O_BETA, 0)),
                  pl.BlockSpec((pl.Element(1), pl.Element(n_gate), pl.Element(k)), lambda d, j: (d, _O_GATE, 0))],
        out_specs=pl.BlockSpec((None, k, W_TILE), lambda d, j: (d, 0, j)),
        out_shape=jax.ShapeDtypeStruct((depth, k, N_PAD), BF16),
        scratch_shapes=[pltpu.VMEM((W_TILE, k), F32)],
        compiler_params=_params("parallel", "arbitrary"),
        name="relayout_w_in",
    )(wt, wt, wt)


def _bias_expand_kernel(t_ref, bucket_ref, o_ref):
    bucket = bucket_ref[...]
    for h in range(NSA_HEADS):
        acc = jnp.full(bucket.shape, t_ref[0, h], F32)
        for b in range(1, NUM_BUCKETS):
            acc = jnp.where(bucket >= b, t_ref[b, h], acc)
        o_ref[h] = acc


def _bias_expand(table, bucket):
    rows, cols = bucket.shape
    tr = max(t for t in range(SUBLANES, min(rows, 512) + 1, SUBLANES) if rows % t == 0)
    return pl.pallas_call(
        _bias_expand_kernel,
        grid=(rows // tr,),
        in_specs=[pl.BlockSpec(memory_space=pltpu.SMEM), pl.BlockSpec((tr, cols), lambda i: (i, 0))],
        out_specs=pl.BlockSpec((NSA_HEADS, tr, cols), lambda i: (0, i, 0)),
        out_shape=jax.ShapeDtypeStruct((NSA_HEADS, rows, cols), F32),
        compiler_params=_params("parallel"),
        name="bias_expand",
    )(table, jnp.asarray(bucket))


def _bias_tiles(table):
    i = np.arange(BT)
    d = np.arange(N_BIAS_TILES)[:, None, None] * BT + i[None, :, None] - i[None, None, :]
    tiles = _bias_expand(table * LOG2E, _bucket_np(d).reshape(N_BIAS_TILES * BT, BT))
    return tiles.reshape(NSA_HEADS, N_BIAS_TILES, BT, BT)


def _bias_cmp(table, pos0, lq, nb):
    dist = (pos0 + np.arange(lq))[:, None] - (np.arange(nb) * NSA_BLOCK + NSA_BLOCK - 1)[None, :]
    return _bias_expand(table, _bucket_np(dist))


def kernel(x_prompt, x_sample, state_dn_S, state_dn_conv, cache_win_kv, cache_cmp_kv, cache_slc_kv, page_table,
           p_prompt, p_sample, rel_bias_table, w_in, w_out, g_pre, g_post, dn_conv_w, dn_A_log, dn_dt_bias,
           dn_norm_w, cmp_pos_w, cmp_w1, cmp_w2, w_ple_proj, g_ple, w_ple_gate):
    depth = w_in.shape[0]
    bp, seq, _ = x_prompt.shape
    bs, lq, _ = x_sample.shape
    n_pages = page_table.shape[1]
    past = n_pages * PAGE_SIZE
    n_pool = cache_cmp_kv.shape[1]
    wlen = cache_win_kv.shape[2]
    mp, ms = bp * seq, bs * lq
    assert seq % PK == 0 and seq % DN_CHUNK == 0 and lq == SUBLANES and past % NSA_BLOCK == 0
    kv_row = (2, NSA_KV_HEADS, NSA_HD)

    w_in_r = _relayout_w_in(w_in)
    w_out_b = w_out.astype(BF16)
    wpp_b = w_ple_proj.astype(BF16)
    wpg_b = w_ple_gate.astype(BF16)
    g_pre3 = g_pre[:, None, :]
    g_post3 = g_post[:, None, :]
    g_ple3 = g_ple[:, None, :]
    nw3 = dn_norm_w[:, None, :]
    lane_pad = ((0, 0), (DN_HEADS, LANES - 2 * DN_HEADS))
    dn_par = jnp.stack([jnp.pad(dn_A_log, lane_pad), jnp.pad(dn_dt_bias, lane_pad)], axis=1)
    pw_prompt = jnp.broadcast_to(cmp_pos_w[:, :, None, :, None], (depth, 2, NSA_KV_HEADS, NSA_BLOCK, NSA_HD))
    pw_prompt = jnp.transpose(pw_prompt, (0, 3, 1, 2, 4)).reshape(depth, NSA_BLOCK, 2 * NSA_KV_W)
    pw_page = jnp.broadcast_to(cmp_pos_w[:, None, None, :, :, None],
                               (depth, PAGE_SIZE // NSA_BLOCK, NSA_KV_HEADS, 2, NSA_BLOCK, NSA_HD))
    pw_page = jnp.transpose(pw_page, (0, 1, 4, 3, 2, 5)).reshape(depth, PAGE_SIZE * 4, NSA_HD)
    table = rel_bias_table.astype(F32)
    bias_tiles = _bias_tiles(table)
    nb_p = seq // NSA_BLOCK
    nb_s = past // NSA_BLOCK
    nbs_s = -(-(past + lq) // NSA_BLOCK)
    nbp_s = -(-nbs_s // LANES) * LANES
    bias_cmp_p = _bias_cmp(table, 0, seq, nb_p)
    bias_cmp_s = _bias_cmp(table, past, lq, nb_s)
    tcols = jnp.pad(table.T.reshape(NSA_KV_HEADS, NSA_GROUP, NUM_BUCKETS),
                    ((0, 0), (0, SUBLANES - NSA_GROUP), (0, 0)))
    tcols_rows = jnp.repeat(table.T.reshape(NSA_KV_HEADS, NSA_GROUP, NUM_BUCKETS), lq, axis=1)

    conv0_s = jnp.pad(state_dn_conv, ((0, 0), (0, 0), (SUBLANES - (CONV_W - 1), 0), (0, 0)))
    conv0_p = jnp.zeros((1, bp, SUBLANES, DN_CONV_CH), F32)
    s0_p = jnp.zeros((1, bp, DN_HEADS, DN_DK, DN_DV), F32)
    cmp4 = cache_cmp_kv.reshape(depth, n_pool, PAGE_SIZE * 4, NSA_HD)
    slc4 = cache_slc_kv.reshape(depth, n_pool, PAGE_SIZE * 4, NSA_HD)
    win4 = cache_win_kv.reshape(depth, bs, wlen * 4, NSA_HD)
    p_prompt3 = p_prompt.reshape(depth, mp, PLE_DIM)
    p_sample3 = p_sample.reshape(depth, ms, PLE_DIM)

    xp = x_prompt.reshape(mp, D_MODEL)
    xs = x_sample.reshape(ms, D_MODEL)
    outs_p = [[] for _ in range(2)]
    outs_s = [[] for _ in range(5)]
    projs_p = []
    tq_p = min(256, seq)
    tm_p = min(IN_TM, mp)
    pages = min(POOL_PAGES, n_pages)
    fuse_pool = (mp // tm_p, N_PAD // IN_TN) == (bs, n_pages // pages)
    for i in range(depth):
        if fuse_pool:
            proj, pooled = _in_proj_pool(xp, g_pre3, w_in_r, cmp4, page_table, pw_page, i, tm_p, IN_TN, pages)
        else:
            proj = _in_proj(xp, g_pre3, w_in_r, i, tm_p)
            pooled = _pool_paged(cmp4, page_table, pw_page, i, pages)
        dn_o, conv_n, s_n = _deltanet(proj, bp, seq, DN_CHUNK, conv0_p, s0_p, 0, dn_conv_w, dn_par, nw3, i)
        ckv = _cmp_mlp(_pool_prompt(proj, bp, seq, pw_prompt, i), cmp_w1, cmp_w2, i)
        o_cmp, sel = _cmp_attn(proj, ckv, bias_cmp_p, bp, seq, 0, tq_p, nb_p, nb_p, True)
        nsa_o = _nsa_prompt(proj, sel, bias_tiles, o_cmp, bp, seq)
        xp = _out_proj(dn_o, nsa_o, xp, p_prompt3, w_out_b, g_post3, wpp_b, g_ple3, wpg_b, i, min(256, mp))
        projs_p.append(proj)
        outs_p[0].append(s_n)
        outs_p[1].append(conv_n[:, SUBLANES - (CONV_W - 1):])

        proj = _in_proj(xs, g_pre3, w_in_r, i, ms)
        dn_o, conv_n, s_n = _deltanet(proj, bs, lq, lq, conv0_s, state_dn_S, i, dn_conv_w, dn_par, nw3, i)
        ckv = _cmp_mlp(pooled, cmp_w1, cmp_w2, i)
        o_cmp, score = _cmp_attn(proj, ckv, bias_cmp_s, bs, lq, past, lq, nbs_s, nbp_s, False)
        score_t = jnp.transpose(score.reshape(ms * NSA_KV_HEADS, nbp_s))
        _, idx_t, val_t = _topk(score_t, nbs_s)
        o_slc = _slc_sample(proj, slc4, page_table, idx_t.T.reshape(-1), val_t.T.reshape(-1), tcols, i,
                            bs, lq, past)
        nsa_o = _win_sample(proj, win4, tcols_rows, o_cmp, o_slc, i, bs, lq, past)
        xs = _out_proj(dn_o, nsa_o, xs, p_sample3, w_out_b, g_post3, wpp_b, g_ple3, wpg_b, i, ms)
        kv_all = proj[:, C_CMP:C_WIN + 2 * NSA_KV_W].reshape(bs, lq, 3, *kv_row)
        win_all = jnp.concatenate([cache_win_kv[i], kv_all[:, :, 2]], axis=1)
        outs_s[0].append(s_n)
        outs_s[1].append(conv_n[:, SUBLANES - (CONV_W - 1):])
        outs_s[2].append(win_all[:, win_all.shape[1] - min(WINDOW, past + lq):])
        outs_s[3].append(kv_all[:, :, 0])
        outs_s[4].append(kv_all[:, :, 1])

    p_cmp, p_slc, p_win = _kv_rows(projs_p, bp, seq)
    return ((xp.reshape(bp, seq, D_MODEL), xs.reshape(bs, lq, D_MODEL), jnp.stack(outs_p[0]), jnp.stack(outs_p[1]),
             p_win.reshape(depth, bp, min(WINDOW, seq), *kv_row), p_cmp.reshape(depth, bp, seq, *kv_row),
             p_slc.reshape(depth, bp, seq, *kv_row)) + tuple(jnp.stack(o) for o in outs_s))
```

```python
import functools
import math

import numpy as np
import jax
import jax.numpy as jnp
from jax import lax
from jax.experimental import pallas as pl
from jax.experimental.pallas import tpu as pltpu

F32 = jnp.float32
BF16 = jnp.bfloat16

D_MODEL = 2048
DN_HEADS = 8
DN_DK = 128
DN_DV = 128
DN_QK_W = DN_HEADS * DN_DK
DN_CONV_CH = 2 * DN_QK_W + DN_HEADS * DN_DV
CONV_W = 4
DN_CHUNK = 64
NSA_HEADS = 8
NSA_KV_HEADS = 2
NSA_GROUP = 4
NSA_HD = 128
NSA_WIDTH = NSA_HEADS * NSA_HD
NSA_KV_W = NSA_KV_HEADS * NSA_HD
NSA_BLOCK = 64
NSA_TOPN = 16
WINDOW = 512
SCALE = NSA_HD ** -0.5
NUM_BUCKETS = 32
REL_MAX_DIST = 1024
PLE_DIM = 256
PAGE_SIZE = 128
EPS = 1e-6
NEG = -1e30
LOG2E = math.log2(math.e)

VMEM_LIMIT_BYTES = 56 * 1024 * 1024
LANES = 128
SUBLANES = 8

C_QKV = 0
C_Z = 3072
C_NQ = 4096
C_NZ = 5120
C_CMP = 6144
C_SLC = 6656
C_WIN = 7168
C_SM = 7680
C_G0 = 7808
C_G1 = 7936
N_PAD = 8192
_O_BETA = DN_CONV_CH + DN_HEADS * DN_DV
_O_NQ = _O_BETA + 2 * DN_HEADS
_O_GATE = _O_NQ + 2 * NSA_WIDTH + 3 * 2 * NSA_KV_W
N_BIAS_TILES = 9


def _bucket_np(dist):
    dist = np.maximum(np.asarray(dist, np.int64), 0)
    exact = NUM_BUCKETS // 2
    scaled = np.log(np.maximum(dist, 1).astype(np.float64) / exact) / math.log(REL_MAX_DIST / exact)
    large = np.minimum(exact + (scaled * (NUM_BUCKETS - exact)).astype(np.int64), NUM_BUCKETS - 1)
    return np.where(dist < exact, dist, large).astype(np.int32)


_BUCKET_THR = tuple(int(np.argmax(_bucket_np(np.arange(2 * REL_MAX_DIST)) >= b)) for b in range(1, NUM_BUCKETS))


def _dot(a, b):
    return jnp.dot(a.astype(BF16), b.astype(BF16), preferred_element_type=F32)


def _dot_nt(a, b):
    return lax.dot_general(a.astype(BF16), b.astype(BF16), (((1,), (1,)), ((), ())), preferred_element_type=F32)


def _dot_tn(a, b):
    return lax.dot_general(a.astype(BF16), b.astype(BF16), (((0,), (0,)), ((), ())), preferred_element_type=F32)


def _dot_hi(a, b):
    return jnp.dot(a, b, preferred_element_type=F32, precision=lax.Precision.HIGHEST)


def _dot_hi_nt(a, b):
    return lax.dot_general(a, b, (((1,), (1,)), ((), ())), preferred_element_type=F32,
                           precision=lax.Precision.HIGHEST)


def _silu(x):
    return x * jax.nn.sigmoid(x)


def _params(*sem):
    return pltpu.CompilerParams(dimension_semantics=sem, vmem_limit_bytes=VMEM_LIMIT_BYTES)


def _in_proj_kernel(x_ref, g_ref, w_ref, o_ref, xn_ref):
    @pl.when(pl.program_id(1) == 0)
    def _():
        x = x_ref[...]
        y = x * lax.rsqrt(jnp.mean(x * x, axis=-1, keepdims=True) + EPS)
        xn_ref[...] = (y * g_ref[...]).astype(BF16)

    o_ref[...] = jnp.dot(xn_ref[...], w_ref[...], preferred_element_type=F32)


IN_TM = 1024
IN_TN = 1024
POOL_PAGES = 16


def _in_proj(x, g_all, w_all, layer, tm):
    m = x.shape[0]
    tn = IN_TN
    return pl.pallas_call(
        _in_proj_kernel,
        grid=(m // tm, N_PAD // tn),
        in_specs=[pl.BlockSpec((tm, D_MODEL), lambda i, j: (i, 0)),
                  pl.BlockSpec((None, 1, D_MODEL), lambda i, j: (layer, 0, 0)),
                  pl.BlockSpec((None, D_MODEL, tn), lambda i, j: (layer, 0, j))],
        out_specs=pl.BlockSpec((tm, tn), lambda i, j: (i, j)),
        out_shape=jax.ShapeDtypeStruct((m, N_PAD), F32),
        scratch_shapes=[pltpu.VMEM((tm, D_MODEL), BF16)],
        compiler_params=_params("parallel", "arbitrary"),
        name="in_proj",
    )(x, g_all, w_all)


def _dn_prep_kernel(qkv_ref, sm_ref, conv0_ref, cw_ref, par_ref,
                    u_ref, w_ref, qd_ref, kd_ref, qk_ref, gl_ref, convn_ref, xs_ref, *, chunk, cs):
    tb = chunk * cs
    halo = SUBLANES

    @pl.when(pl.program_id(1) == 0)
    def _():
        xs_ref[0:halo, :] = conv0_ref[...]

    xs_ref[halo:halo + tb, :] = qkv_ref[...]
    base = halo - (CONV_W - 1)
    y = xs_ref[base:base + tb, :] * cw_ref[0:1, :]
    for j in range(1, CONV_W):
        y = y + xs_ref[base + j:base + j + tb, :] * cw_ref[j:j + 1, :]
    y = _silu(y)
    tail = xs_ref[tb:tb + halo, :]
    xs_ref[0:halo, :] = tail
    convn_ref[...] = tail

    sm = sm_ref[...]
    beta_all = jax.nn.sigmoid(sm)
    xg = sm + par_ref[1:2, :]
    softplus = jnp.maximum(xg, 0.0) + jnp.log1p(jnp.exp(-jnp.abs(xg)))
    g_all = -jnp.exp(par_ref[0:1, :]) * softplus
    row = lax.broadcasted_iota(jnp.int32, (tb, tb), 0)
    col = lax.broadcasted_iota(jnp.int32, (tb, tb), 1)
    in_chunk_lower = (row >= col) & (row // chunk == col // chunk)
    gc_all = _dot_hi(jnp.where(in_chunk_lower, 1.0, 0.0), g_all)
    pick = (lax.broadcasted_iota(jnp.int32, (DN_HEADS, LANES), 1)
            == lax.broadcasted_iota(jnp.int32, (DN_HEADS, LANES), 0) + DN_HEADS).astype(F32)
    gc_rows = _dot_hi_nt(pick, gc_all)
    for c in range(cs):
        gl_ref[c] = gc_all[(c + 1) * chunk - 1:(c + 1) * chunk, :]
    r64 = lax.broadcasted_iota(jnp.int32, (chunk, chunk), 0)
    c64 = lax.broadcasted_iota(jnp.int32, (chunk, chunk), 1)
    lower = r64 >= c64
    strict = r64 > c64

    lhs, kns, rhs, decays = [], [], [], []
    for h in range(DN_HEADS):
        hc = slice(h * DN_DV, (h + 1) * DN_DV)
        qh = y[:, h * DN_DK:(h + 1) * DN_DK]
        kh = y[:, DN_QK_W + h * DN_DK:DN_QK_W + (h + 1) * DN_DK]
        vh = y[:, 2 * DN_QK_W + h * DN_DV:2 * DN_QK_W + (h + 1) * DN_DV]
        qn = qh * lax.rsqrt(jnp.sum(qh * qh, axis=-1, keepdims=True) + EPS) * (DN_DK ** -0.5)
        kn = kh * lax.rsqrt(jnp.sum(kh * kh, axis=-1, keepdims=True) + EPS)
        beta = beta_all[:, h:h + 1]
        gc = gc_all[:, DN_HEADS + h:DN_HEADS + h + 1]
        egc = jnp.exp(gc)
        kb = kn * beta
        rhs_h = jnp.concatenate([vh * beta, kb * egc], axis=-1)
        qd_ref[:, hc] = (qn * egc).astype(qd_ref.dtype)
        for c in range(cs):
            rs = slice(c * chunk, (c + 1) * chunk)
            gcc = gc[rs]
            decays.append(jnp.exp(jnp.where(lower, gcc - gc_rows[h:h + 1, rs], NEG)))
            lhs.append(jnp.concatenate([kb[rs], qn[rs]], axis=0))
            kns.append(kn[rs])
            rhs.append(rhs_h[rs])
            kd_ref[rs, hc] = (kn[rs] * jnp.exp(gcc[chunk - 1:chunk, :] - gcc)).astype(kd_ref.dtype)
    decay = jnp.stack(decays)
    both = jnp.einsum('nid,njd->nij', jnp.stack(lhs).astype(BF16), jnp.stack(kns).astype(BF16),
                      preferred_element_type=F32)
    a = jnp.where(strict[None], both[:, :chunk] * decay, 0.0)
    qk = both[:, chunk:] * decay

    def bmm(p, q):
        return jnp.einsum('nij,njk->nik', p, q, preferred_element_type=F32)

    def split(v):
        hi = v.astype(BF16)
        return hi, (v - hi.astype(F32)).astype(BF16)

    def bmm3(p, q):
        (ph, pl_), (qh, ql) = p, q
        return bmm(ph, qh) + (bmm(ph, ql) + bmm(pl_, qh))

    pw_s = split(a)
    inv = jnp.where(r64 == c64, 1.0, 0.0)[None] - a
    for _ in range(int(math.log2(chunk)) - 1):
        pw_s = split(bmm3(pw_s, pw_s))
        inv = inv + bmm3(pw_s, split(inv))
    x = bmm3(split(inv), split(jnp.stack(rhs)))
    for h in range(DN_HEADS):
        hc = slice(h * DN_DV, (h + 1) * DN_DV)
        for c in range(cs):
            rs = slice(c * chunk, (c + 1) * chunk)
            n = h * cs + c
            u_ref[rs, hc] = x[n, :, :DN_DV]
            w_ref[rs, hc] = x[n, :, DN_DV:].astype(w_ref.dtype)
            qk_ref[rs, h * chunk:(h + 1) * chunk] = qk[n]


def _dn_scan_kernel(u_ref, w_ref, qd_ref, kd_ref, qk_ref, gl_ref, z_ref, s0_ref, nw_ref,
                    o_ref, sout_ref, s_ref, *, chunk, cs):
    @pl.when(pl.program_id(1) == 0)
    def _():
        s_ref[...] = s0_ref[...]

    def heads(ref, rs, width):
        return jnp.stack([ref[rs, h * width:(h + 1) * width] for h in range(DN_HEADS)])

    def bdot(spec, p, q):
        return jnp.einsum(spec, p.astype(BF16), q.astype(BF16), preferred_element_type=F32)

    for c in range(cs):
        rs = slice(c * chunk, (c + 1) * chunk)
        decay_last = jnp.exp(gl_ref[c])
        decay_last = jnp.stack([decay_last[:, DN_HEADS + h:DN_HEADS + h + 1] for h in range(DN_HEADS)])
        s = s_ref[...]
        sb = s.astype(BF16)
        v_new = heads(u_ref, rs, DN_DV) - bdot('hcd,hde->hce', heads(w_ref, rs, DN_DK), sb)
        o = (bdot('hcd,hde->hce', heads(qd_ref, rs, DN_DK), sb)
             + bdot('hcs,hse->hce', heads(qk_ref, rs, chunk), v_new))
        s_ref[...] = s * decay_last + bdot('hcd,hce->hde', heads(kd_ref, rs, DN_DK), v_new)
        on = o * lax.rsqrt(jnp.mean(o * o, axis=-1, keepdims=True) + EPS) * nw_ref[...]
        for h in range(DN_HEADS):
            hc = slice(h * DN_DV, (h + 1) * DN_DV)
            o_ref[rs, hc] = on[h] * _silu(z_ref[rs, hc])

    @pl.when(pl.program_id(1) == pl.num_programs(1) - 1)
    def _():
        sout_ref[...] = s_ref[...]


def _deltanet(proj, nb, seq, chunk, conv0_all, s0_all, state_layer, cw_all, par_all, nw_all, layer):
    m = nb * seq
    nchunks = seq // chunk
    width = DN_HEADS * DN_DV
    hs = (DN_HEADS, DN_DK, DN_DV)
    cs_a = 4 if nchunks % 4 == 0 else 1
    na = nchunks // cs_a
    ta = cs_a * chunk
    op_dtype = BF16 if chunk % (2 * SUBLANES) == 0 else F32
    u, w, qd, kd, qk, gl, conv_n = pl.pallas_call(
        functools.partial(_dn_prep_kernel, chunk=chunk, cs=cs_a),
        grid=(nb, na),
        in_specs=[pl.BlockSpec((ta, DN_CONV_CH), lambda b, c: (b * na + c, C_QKV // DN_CONV_CH)),
                  pl.BlockSpec((ta, LANES), lambda b, c: (b * na + c, C_SM // LANES)),
                  pl.BlockSpec((None, None, SUBLANES, DN_CONV_CH), lambda b, c: (state_layer, b, 0, 0)),
                  pl.BlockSpec((None, CONV_W, DN_CONV_CH), lambda b, c: (layer, 0, 0)),
                  pl.BlockSpec((None, 2, LANES), lambda b, c: (layer, 0, 0))],
        out_specs=[pl.BlockSpec((ta, width), lambda b, c: (b * na + c, 0)),
                   pl.BlockSpec((ta, width), lambda b, c: (b * na + c, 0)),
                   pl.BlockSpec((ta, width), lambda b, c: (b * na + c, 0)),
                   pl.BlockSpec((ta, width), lambda b, c: (b * na + c, 0)),
                   pl.BlockSpec((ta, DN_HEADS * chunk), lambda b, c: (b * na + c, 0)),
                   pl.BlockSpec((cs_a, 1, LANES), lambda b, c: (b * na + c, 0, 0)),
                   pl.BlockSpec((None, SUBLANES, DN_CONV_CH), lambda b, c: (b, 0, 0))],
        out_shape=[jax.ShapeDtypeStruct((m, width), F32),
                   jax.ShapeDtypeStruct((m, width), op_dtype),
                   jax.ShapeDtypeStruct((m, width), op_dtype),
                   jax.ShapeDtypeStruct((m, width), op_dtype),
                   jax.ShapeDtypeStruct((m, DN_HEADS * chunk), F32),
                   jax.ShapeDtypeStruct((nb * nchunks, 1, LANES), F32),
                   jax.ShapeDtypeStruct((nb, SUBLANES, DN_CONV_CH), F32)],
        scratch_shapes=[pltpu.VMEM((ta + SUBLANES, DN_CONV_CH), F32)],
        compiler_params=_params("parallel", "arbitrary"),
        name="dn_prep",
    )(proj, proj, conv0_all, cw_all, par_all)

    cs_b = 8 if nchunks % 8 == 0 else 1
    nbk = nchunks // cs_b
    tbk = cs_b * chunk
    row_spec = pl.BlockSpec((tbk, width), lambda b, c: (b * nbk + c, 0))
    o, s_n = pl.pallas_call(
        functools.partial(_dn_scan_kernel, chunk=chunk, cs=cs_b),
        grid=(nb, nbk),
        in_specs=[row_spec, row_spec, row_spec, row_spec,
                  pl.BlockSpec((tbk, DN_HEADS * chunk), lambda b, c: (b * nbk + c, 0)),
                  pl.BlockSpec((cs_b, 1, LANES), lambda b, c: (b * nbk + c, 0, 0)),
                  pl.BlockSpec((tbk, width), lambda b, c: (b * nbk + c, C_Z // width)),
                  pl.BlockSpec((None, None) + hs, lambda b, c: (state_layer, b, 0, 0, 0)),
                  pl.BlockSpec((None, 1, DN_DV), lambda b, c: (layer, 0, 0))],
        out_specs=[row_spec, pl.BlockSpec((None,) + hs, lambda b, c: (b, 0, 0, 0))],
        out_shape=[jax.ShapeDtypeStruct((m, width), F32), jax.ShapeDtypeStruct((nb,) + hs, F32)],
        scratch_shapes=[pltpu.VMEM(hs, F32)],
        compiler_params=_params("parallel", "arbitrary"),
        name="dn_scan",
    )(u, w, qd, kd, qk, gl, proj, s0_all, nw_all)
    return o, conv_n, s_n


def _pool_kernel(x_ref, pw_ref, o_ref):
    rows = x_ref.shape[0]
    x = x_ref[...].reshape(rows // NSA_BLOCK, NSA_BLOCK, 2 * NSA_KV_W)
    o_ref[...] = jnp.sum(x * pw_ref[...][None], axis=1)


def _pool_prompt(proj, nb, seq, pw_all, layer):
    rows = min(seq, 1024)
    nr = seq // rows
    width = 2 * NSA_KV_W
    return pl.pallas_call(
        _pool_kernel,
        grid=(nb, nr),
        in_specs=[pl.BlockSpec((rows, width), lambda b, r: (b * nr + r, C_CMP // width)),
                  pl.BlockSpec((None, NSA_BLOCK, width), lambda b, r: (layer, 0, 0))],
        out_specs=pl.BlockSpec((None, rows // NSA_BLOCK, width), lambda b, r: (b, r, 0)),
        out_shape=jax.ShapeDtypeStruct((nb, seq // NSA_BLOCK, width), F32),
        compiler_params=_params("parallel", "parallel"),
        name="pool_prompt",
    )(proj, pw_all)


def _pool_pages(page_refs, pw_ref, o_ref):
    rows = PAGE_SIZE * 4
    per_block = rows // 2 // SUBLANES
    for k, page_ref in enumerate(page_refs):
        prod = page_ref[...] * pw_ref[...]
        y = jnp.sum(prod.reshape(2, per_block, SUBLANES, NSA_HD), axis=1)
        y = y[:, 0:4, :] + y[:, 4:8, :]
        for s in range(4):
            o_ref[2 * k:2 * k + 2, s * NSA_HD:(s + 1) * NSA_HD] = y[:, s, :]


def _paged_pool_kernel(pt_ref, *refs, pages):
    del pt_ref
    _pool_pages(refs[:pages], refs[pages], refs[pages + 1])


def _in_proj_pool_kernel(pt_ref, x_ref, g_ref, w_ref, *refs, pages):
    del pt_ref
    o_ref, pooled_ref, xn_ref = refs[pages + 1:]
    _in_proj_kernel(x_ref, g_ref, w_ref, o_ref, xn_ref)
    _pool_pages(refs[:pages], refs[pages], pooled_ref)


def _in_proj_pool(x, g_all, w_all, cache4, page_table, pw_all, layer, tm, tn, pages):
    m = x.shape[0]
    nb, n_pages = page_table.shape
    assert (m // tm, N_PAD // tn) == (nb, n_pages // pages)
    rows = PAGE_SIZE * 4

    def page_spec(k):
        return pl.BlockSpec((None, None, rows, NSA_HD), lambda i, j, pt: (layer, pt[i, j * pages + k], 0, 0))

    grid_spec = pltpu.PrefetchScalarGridSpec(
        num_scalar_prefetch=1,
        grid=(m // tm, N_PAD // tn),
        in_specs=[pl.BlockSpec((tm, D_MODEL), lambda i, j, pt: (i, 0)),
                  pl.BlockSpec((None, 1, D_MODEL), lambda i, j, pt: (layer, 0, 0)),
                  pl.BlockSpec((None, D_MODEL, tn), lambda i, j, pt: (layer, 0, j))]
        + [page_spec(k) for k in range(pages)]
        + [pl.BlockSpec((None, rows, NSA_HD), lambda i, j, pt: (layer, 0, 0))],
        out_specs=[pl.BlockSpec((tm, tn), lambda i, j, pt: (i, j)),
                   pl.BlockSpec((None, 2 * pages, 4 * NSA_HD), lambda i, j, pt: (i, j, 0))],
        scratch_shapes=[pltpu.VMEM((tm, D_MODEL), BF16)],
    )
    return pl.pallas_call(
        functools.partial(_in_proj_pool_kernel, pages=pages),
        grid_spec=grid_spec,
        out_shape=[jax.ShapeDtypeStruct((m, N_PAD), F32),
                   jax.ShapeDtypeStruct((nb, 2 * n_pages, 4 * NSA_HD), F32)],
        compiler_params=_params("parallel", "arbitrary"),
        name="in_proj_pool",
    )(page_table, x, g_all, w_all, *([cache4] * pages), pw_all)


def _pool_paged(cache4, page_table, pw_all, layer, pages):
    nb, n_pages = page_table.shape
    steps = n_pages // pages
    rows = PAGE_SIZE * 4

    def page_spec(k):
        return pl.BlockSpec((None, None, rows, NSA_HD),
                            lambda b, s, pt: (layer, pt[b, s * pages + k], 0, 0))

    grid_spec = pltpu.PrefetchScalarGridSpec(
        num_scalar_prefetch=1,
        grid=(nb, steps),
        in_specs=[page_spec(k) for k in range(pages)]
        + [pl.BlockSpec((None, rows, NSA_HD), lambda b, s, pt: (layer, 0, 0))],
        out_specs=pl.BlockSpec((None, 2 * pages, 4 * NSA_HD), lambda b, s, pt: (b, s, 0)),
    )
    return pl.pallas_call(
        functools.partial(_paged_pool_kernel, pages=pages),
        grid_spec=grid_spec,
        out_shape=jax.ShapeDtypeStruct((nb, 2 * n_pages, 4 * NSA_HD), F32),
        compiler_params=_params("parallel", "arbitrary"),
        name="pool_paged",
    )(page_table, *([cache4] * pages), pw_all)


def _cmp_mlp_kernel(p_ref, w1_ref, w2_ref, o_ref):
    for s in range(4):
        kv = s // NSA_KV_HEADS
        h = _silu(_dot_hi(p_ref[:, s * NSA_HD:(s + 1) * NSA_HD], w1_ref[kv]))
        o_ref[:, s * NSA_HD:(s + 1) * NSA_HD] = _dot_hi(h, w2_ref[kv])


def _cmp_mlp(pooled, w1_all, w2_all, layer):
    nb, n, width = pooled.shape
    wspec = pl.BlockSpec((None, 2, NSA_HD, NSA_HD), lambda b: (layer, 0, 0, 0))
    return pl.pallas_call(
        _cmp_mlp_kernel,
        grid=(nb,),
        in_specs=[pl.BlockSpec((None, n, width), lambda b: (b, 0, 0)), wspec, wspec],
        out_specs=pl.BlockSpec((None, n, width), lambda b: (b, 0, 0)),
        out_shape=jax.ShapeDtypeStruct((nb, n, width), F32),
        compiler_params=_params("parallel"),
        name="cmp_mlp",
    )(pooled, w1_all, w2_all)


def _select_blocks(score, st_ref, nbp, n_causal):
    st_ref[...] = score.T
    halves = [st_ref[g * nbp:(g + 1) * nbp, :] for g in range(NSA_KV_HEADS)]
    bi = lax.broadcasted_iota(jnp.int32, halves[0].shape, 0)

    def body(j, ranks):
        tie = jnp.where(j < bi, 1, 0)
        out = []
        for g in range(NSA_KV_HEADS):
            other = st_ref[pl.ds(g * nbp + j, 1), :]
            out.append(ranks[g] + jnp.where(other > halves[g], 1, jnp.where(other == halves[g], tie, 0)))
        return tuple(out)

    ranks = lax.fori_loop(0, n_causal, body, tuple(jnp.zeros(bi.shape, jnp.int32) for _ in halves))
    keep = [jnp.where((ranks[g] < NSA_TOPN) & (halves[g] >= 0.0), 1.0, 0.0) for g in range(NSA_KV_HEADS)]
    return jnp.concatenate(keep, axis=0).T


def _cmp_attn_kernel(q_ref, ckv_ref, bias_ref, o_ref, sc_ref, *scratch, pos0, tq, nb, nbs, nbp, select):
    qi = pl.program_id(1)
    t = pos0 + qi * tq + lax.broadcasted_iota(jnp.int32, (tq, 1), 0)
    blk = lax.broadcasted_iota(jnp.int32, (1, nb), 1)
    valid = t >= blk * NSA_BLOCK + (NSA_BLOCK - 1)
    bi = lax.broadcasted_iota(jnp.int32, (1, nbp), 1)
    cur = t // NSA_BLOCK
    forced = (bi == 0) | (bi == cur) | (bi == cur - 1)
    causal = bi <= cur
    scores = []
    for g in range(NSA_KV_HEADS):
        ck = ckv_ref[:, g * NSA_HD:(g + 1) * NSA_HD]
        cv = ckv_ref[:, (NSA_KV_HEADS + g) * NSA_HD:(NSA_KV_HEADS + g + 1) * NSA_HD]
        imp = jnp.zeros((tq, nb), F32)
        for r in range(NSA_GROUP):
            h = g * NSA_GROUP + r
            logits = _dot_nt(q_ref[:, h * NSA_HD:(h + 1) * NSA_HD], ck) * SCALE + bias_ref[h]
            logits = jnp.where(valid, logits, NEG)
            mx = jnp.max(logits, axis=-1, keepdims=True)
            p = jnp.where(valid, jnp.exp(logits - mx), 0.0)
            p = p / jnp.maximum(jnp.sum(p, axis=-1, keepdims=True), 1e-30)
            o_ref[:, h * NSA_HD:(h + 1) * NSA_HD] = _dot(p, cv)
            imp = imp + p
        if nbp > nb:
            imp = jnp.concatenate([imp, jnp.zeros((tq, nbp - nb), F32)], axis=-1)
        score = jnp.where(causal, jnp.where(forced, NSA_GROUP + 1.0, imp), -1.0)
        scores.append(jnp.where(bi < nbs, score, -2.0))
    score = jnp.concatenate(scores, axis=-1)
    if select:
        n_causal = jnp.minimum((pos0 + (qi + 1) * tq - 1) // NSA_BLOCK + 1, nbs)
        sc_ref[...] = _select_blocks(score, scratch[0], nbp, n_causal)
    else:
        sc_ref[...] = score


def _cmp_attn(proj, ckv, bias_cmp, nb_batch, seq, pos0, tq, nbs, nbp, select):
    nq = seq // tq
    nb = ckv.shape[1]
    m = nb_batch * seq
    bias_tq = bias_cmp.shape[1] // nq
    return pl.pallas_call(
        functools.partial(_cmp_attn_kernel, pos0=pos0, tq=tq, nb=nb, nbs=nbs, nbp=nbp, select=select),
        scratch_shapes=[pltpu.VMEM((NSA_KV_HEADS * nbp, tq), F32)] if select else [],
        grid=(nb_batch, nq),
        in_specs=[pl.BlockSpec((tq, NSA_WIDTH), lambda b, i: (b * nq + i, C_NQ // NSA_WIDTH)),
                  pl.BlockSpec((None, nb, 2 * NSA_KV_W), lambda b, i: (b, 0, 0)),
                  pl.BlockSpec((NSA_HEADS, bias_tq, nb), lambda b, i: (0, i, 0))],
        out_specs=[pl.BlockSpec((tq, NSA_WIDTH), lambda b, i: (b * nq + i, 0)),
                   pl.BlockSpec((tq, NSA_KV_HEADS * nbp), lambda b, i: (b * nq + i, 0))],
        out_shape=[jax.ShapeDtypeStruct((m, NSA_WIDTH), F32),
                   jax.ShapeDtypeStruct((m, NSA_KV_HEADS * nbp), F32)],
        compiler_params=_params("parallel", "parallel"),
        name="cmp_attn",
    )(proj, ckv, bias_cmp)


def _topk_kernel(s_ref, sel_ref, idx_ref, val_ref, *, nbs):
    s = s_ref[...]
    bi = lax.broadcasted_iota(jnp.int32, s.shape, 0)

    def body(j, rank):
        other = s_ref[pl.ds(j, 1), :]
        beats = (other > s) | ((other == s) & (j < bi))
        return rank + beats.astype(jnp.int32)

    rank = lax.fori_loop(0, nbs, body, jnp.zeros(s.shape, jnp.int32))
    ok = s >= 0.0
    sel_ref[...] = jnp.where((rank < NSA_TOPN) & ok, 1.0, 0.0)
    for r in range(NSA_TOPN):
        hit = rank == r
        idx_ref[r:r + 1, :] = jnp.sum(jnp.where(hit, bi, 0), axis=0, keepdims=True)
        val_ref[r:r + 1, :] = jnp.sum(jnp.where(hit & ok, 1, 0), axis=0, keepdims=True)


def _topk(score_t, nbs):
    rows, cols = score_t.shape
    tc = min(cols, 512)
    return pl.pallas_call(
        functools.partial(_topk_kernel, nbs=nbs),
        grid=(cols // tc,),
        in_specs=[pl.BlockSpec((rows, tc), lambda i: (0, i))],
        out_specs=[pl.BlockSpec((rows, tc), lambda i: (0, i)),
                   pl.BlockSpec((NSA_TOPN, tc), lambda i: (0, i)),
                   pl.BlockSpec((NSA_TOPN, tc), lambda i: (0, i))],
        out_shape=[jax.ShapeDtypeStruct((rows, cols), F32),
                   jax.ShapeDtypeStruct((NSA_TOPN, cols), jnp.int32),
                   jax.ShapeDtypeStruct((NSA_TOPN, cols), jnp.int32)],
        compiler_params=_params("parallel"),
        name="topk",
    )(score_t)


PQ = 256
PK = 512
BT = LANES


def _softmax_rows(s, ok):
    mx = jnp.max(s, axis=-1, keepdims=True)
    p = jnp.where(ok, jnp.exp(s - mx), 0.0)
    return p, jnp.sum(p, axis=-1, keepdims=True)


def _nsa_prompt_kernel(q_ref, ks_ref, vs_ref, kw_ref, vw_ref, sel_ref, bt_ref, ocmp_ref, z_ref, gate_ref,
                       o_ref, s2_ref, p2_ref, sw_ref, pw_ref, m_ref, acc_ref, *, n_bt, wsub):
    qi = pl.program_id(2)
    rows = NSA_GROUP * PQ
    qsub = PQ // BT
    ksub = PK // BT
    q4 = jnp.concatenate([q_ref[:, r * NSA_HD:(r + 1) * NSA_HD] for r in range(NSA_GROUP)], axis=0)
    q4 = (q4 * (SCALE * LOG2E)).astype(BF16)
    t = qi * PQ + lax.broadcasted_iota(jnp.int32, (PQ, 1), 0)
    nsel = sel_ref.shape[1]
    blk0 = pl.program_id(1) * (nsel // NSA_KV_HEADS)
    unselected = jnp.where(sel_ref[...] > 0.5, 0.0, NEG).astype(BF16)
    qx = jnp.concatenate([q4, jnp.concatenate([unselected] * NSA_GROUP, axis=0)], axis=1)

    def row_groups():
        for r in range(NSA_GROUP):
            for a in range(qsub):
                yield slice(r * PQ + a * BT, r * PQ + (a + 1) * BT), r, a

    def softmax_block(s_ref, p_ref, rs, r, tile_of, extra, width, m_old):
        zmax = None
        for c in range(width):
            cs = slice(c * BT, (c + 1) * BT)
            z = s_ref[rs, cs] + bt_ref[r, jnp.clip(tile_of(c), 0, N_BIAS_TILES - 1)]
            if extra is not None:
                z = z + extra[:, cs]
            s_ref[rs, cs] = z
            zmax = z if zmax is None else jnp.maximum(zmax, z)
        mx = jnp.broadcast_to(jnp.max(zmax, axis=-1, keepdims=True), (BT, BT))
        if m_old is not None:
            mx = jnp.maximum(m_old, mx)
        for c in range(width):
            cs = slice(c * BT, (c + 1) * BT)
            p_ref[rs, cs] = jnp.exp2(s_ref[rs, cs] - mx).astype(BF16)
        return mx

    def with_ones(v):
        return jnp.concatenate([v.astype(BF16), jnp.ones(v.shape, BF16)], axis=1)

    m_ref[...] = jnp.full((rows, BT), NEG, F32)
    acc_ref[...] = jnp.zeros((rows, 2 * NSA_HD), F32)

    def key_tile(kt, diagonal, slot):
        s_ref, p_ref = s2_ref.at[slot], p2_ref.at[slot]
        k0 = pl.multiple_of(kt * PK, PK)
        member = (lax.broadcasted_iota(jnp.int32, (PK, nsel), 1)
                  == blk0 + kt * (PK // NSA_BLOCK) + lax.broadcasted_iota(jnp.int32, (PK, nsel), 0) // NSA_BLOCK)
        kx = jnp.concatenate([ks_ref[pl.ds(k0, PK), :].astype(BF16), jnp.where(member, 1.0, 0.0).astype(BF16)],
                             axis=1)
        s_ref[...] = lax.dot_general(qx, kx, (((1,), (1,)), ((), ())), preferred_element_type=F32)
        future = None
        if diagonal:
            future = jnp.where(k0 + lax.broadcasted_iota(jnp.int32, (1, PK), 1) <= t, 0.0, NEG)
        for rs, r, a in row_groups():
            m_old = m_ref[rs]
            mx = softmax_block(s_ref, p_ref, rs, r, lambda c: qi * qsub + a - (kt * ksub + c),
                               None if future is None else future[a * BT:(a + 1) * BT], ksub, m_old)
            alpha = jnp.exp2(m_old - mx)
            acc_ref[rs] = jnp.concatenate([alpha, alpha], axis=1) * acc_ref[rs]
            m_ref[rs] = mx
        acc_ref[...] += jnp.dot(p_ref[...], with_ones(vs_ref[pl.ds(k0, PK), :]), preferred_element_type=F32)

    last = (qi * PQ) // PK

    def body(j, carry):
        key_tile(2 * j, False, 0)
        key_tile(2 * j + 1, False, 1)
        return carry

    lax.fori_loop(0, last // 2, body, 0)

    @pl.when(last % 2 == 1)
    def _():
        key_tile(last - 1, False, 0)

    key_tile(last, True, 1)
    o_slc = acc_ref[:, :NSA_HD] / jnp.maximum(acc_ref[:, NSA_HD:], 1e-30)

    wk = wsub * BT
    kst = jnp.clip(qi * qsub - WINDOW // BT, 0, n_bt - wsub)
    k0 = pl.multiple_of(kst * BT, BT)
    sw_ref[...] = lax.dot_general(q4, kw_ref[pl.ds(k0, wk), :].astype(BF16), (((1,), (1,)), ((), ())),
                                  preferred_element_type=F32)
    dist = t - (k0 + lax.broadcasted_iota(jnp.int32, (1, wk), 1))
    outside = jnp.where((dist >= 0) & (dist <= WINDOW), 0.0, NEG)
    for rs, r, a in row_groups():
        softmax_block(sw_ref, pw_ref, rs, r, lambda c: qi * qsub + a - (kst + c),
                      outside[a * BT:(a + 1) * BT], wsub, None)
    pv = jnp.dot(pw_ref[...], with_ones(vw_ref[pl.ds(k0, wk), :]), preferred_element_type=F32)
    o_win = pv[:, :NSA_HD] / jnp.maximum(pv[:, NSA_HD:], 1e-30)

    gates = jax.nn.sigmoid(gate_ref[...])
    for r in range(NSA_GROUP):
        sl = slice(r * NSA_HD, (r + 1) * NSA_HD)
        rs = slice(r * PQ, (r + 1) * PQ)
        o = (gates[:, 3 * r:3 * r + 1] * ocmp_ref[:, sl] + gates[:, 3 * r + 1:3 * r + 2] * o_slc[rs]
             + gates[:, 3 * r + 2:3 * r + 3] * o_win[rs])
        o_ref[:, sl] = o * _silu(z_ref[:, sl])


def _nsa_prompt(proj, sel, bias_tiles, o_cmp, nb_batch, seq):
    nq = seq // PQ
    n_bt = seq // BT
    wsub = min((WINDOW + PQ) // BT, n_bt)
    gw = NSA_GROUP * NSA_HD
    rows = NSA_GROUP * PQ
    nsel = sel.shape[-1]

    def kv_spec(col):
        return pl.BlockSpec((seq, NSA_HD), lambda b, g, i: (b, col // NSA_HD + g))

    return pl.pallas_call(
        functools.partial(_nsa_prompt_kernel, n_bt=n_bt, wsub=wsub),
        grid=(nb_batch, NSA_KV_HEADS, nq),
        in_specs=[pl.BlockSpec((PQ, gw), lambda b, g, i: (b * nq + i, C_NQ // gw + g)),
                  kv_spec(C_SLC), kv_spec(C_SLC + NSA_KV_W), kv_spec(C_WIN), kv_spec(C_WIN + NSA_KV_W),
                  pl.BlockSpec((PQ, nsel), lambda b, g, i: (b * nq + i, 0)),
                  pl.BlockSpec((NSA_GROUP, N_BIAS_TILES, BT, BT), lambda b, g, i: (g, 0, 0, 0)),
                  pl.BlockSpec((PQ, gw), lambda b, g, i: (b * nq + i, g)),
                  pl.BlockSpec((PQ, gw), lambda b, g, i: (b * nq + i, C_NZ // gw + g)),
                  pl.BlockSpec((PQ, LANES), lambda b, g, i: (b * nq + i, C_G0 // LANES + g))],
        out_specs=pl.BlockSpec((PQ, gw), lambda b, g, i: (b * nq + i, g)),
        out_shape=jax.ShapeDtypeStruct((nb_batch * seq, NSA_WIDTH), F32),
        scratch_shapes=[pltpu.VMEM((2, rows, PK), F32), pltpu.VMEM((2, rows, PK), BF16),
                        pltpu.VMEM((rows, wsub * BT), F32), pltpu.VMEM((rows, wsub * BT), BF16),
                        pltpu.VMEM((rows, BT), F32), pltpu.VMEM((rows, 2 * NSA_HD), F32)],
        compiler_params=_params("parallel", "parallel", "arbitrary"),
        name="nsa_prompt",
    )(proj, proj, proj, proj, proj, sel, bias_tiles, o_cmp, proj, proj)


def _bias_by_threshold(dist, tcols):
    bias = jnp.broadcast_to(tcols[:, 0:1], (tcols.shape[0], dist.shape[-1]))
    for b in range(1, NUM_BUCKETS):
        bias = jnp.where(dist >= _BUCKET_THR[b - 1], tcols[:, b:b + 1], bias)
    return bias


SLOT_CHUNKS = NSA_TOPN + 2


def _slc_sample_kernel(pt_ref, idx_ref, val_ref, *refs, past, lq):
    del pt_ref
    n_in = NSA_KV_HEADS * NSA_TOPN
    page_refs = refs[:n_in]
    q_ref, knew_ref, tcol_ref, o_ref, kc_ref, vc_ref = refs[n_in:]
    b = pl.program_id(0)
    qi = pl.program_id(1)
    t = past + qi
    n_past_blocks = past // NSA_BLOCK
    nk = SLOT_CHUNKS * NSA_BLOCK
    lane = lax.broadcasted_iota(jnp.int32, (1, nk), 1)
    slot = lane // NSA_BLOCK
    this_q = lax.broadcasted_iota(jnp.int32, (lq, 1), 0) == qi
    for g in range(NSA_KV_HEADS):
        base = ((b * lq + qi) * NSA_KV_HEADS + g) * NSA_TOPN
        blk_of_lane = jnp.where(slot == NSA_TOPN, n_past_blocks, n_past_blocks + 1)
        use_of_lane = jnp.where(slot == NSA_TOPN, 1, 0)
        for j in range(NSA_TOPN):
            n_j = idx_ref[base + j]
            use = jnp.where((val_ref[base + j] > 0) & (n_j < n_past_blocks), 1, 0)
            blk_of_lane = jnp.where(slot == j, n_j, blk_of_lane)
            use_of_lane = jnp.where(slot == j, use, use_of_lane)
            page = page_refs[g * NSA_TOPN + j]
            kc_ref[j * NSA_BLOCK:(j + 1) * NSA_BLOCK, :] = page[pl.ds(g, NSA_BLOCK, stride=4), :]
            vc_ref[j * NSA_BLOCK:(j + 1) * NSA_BLOCK, :] = page[pl.ds(NSA_KV_HEADS + g, NSA_BLOCK, stride=4), :]
        new0 = NSA_TOPN * NSA_BLOCK
        kc_ref[new0:, :] = jnp.zeros((nk - new0, NSA_HD), F32)
        vc_ref[new0:, :] = jnp.zeros((nk - new0, NSA_HD), F32)
        kc_ref[new0:new0 + lq, :] = knew_ref[:, g * NSA_HD:(g + 1) * NSA_HD]
        vc_ref[new0:new0 + lq, :] = knew_ref[:, (NSA_KV_HEADS + g) * NSA_HD:(NSA_KV_HEADS + g + 1) * NSA_HD]
        q4 = jnp.concatenate(
            [jnp.sum(jnp.where(this_q, q_ref[:, (g * NSA_GROUP + r) * NSA_HD:(g * NSA_GROUP + r + 1) * NSA_HD], 0.0),
                     axis=0, keepdims=True)
             for r in range(NSA_GROUP)] + [jnp.zeros((SUBLANES - NSA_GROUP, NSA_HD), F32)], axis=0)
        dist = t - (blk_of_lane * NSA_BLOCK + lane % NSA_BLOCK)
        ok = (use_of_lane > 0) & (dist >= 0)
        s = _dot_nt(q4, kc_ref[...]) * SCALE + _bias_by_threshold(dist, tcol_ref[g])
        s = jnp.where(ok, s, NEG)
        p, den = _softmax_rows(s, ok)
        o = _dot(p, vc_ref[...]) / jnp.maximum(den, 1e-30)
        for r in range(NSA_GROUP):
            h = g * NSA_GROUP + r
            o_ref[:, h * NSA_HD:(h + 1) * NSA_HD] = o[r:r + 1, :]


def _slc_sample(proj, cache4, page_table, idx_flat, val_flat, tcols, layer, nb_batch, lq, past):
    rows = NSA_BLOCK * 4
    n_past_blocks = past // NSA_BLOCK

    n_sel = NSA_KV_HEADS * NSA_TOPN
    blocks = jnp.minimum(idx_flat.reshape(nb_batch, lq * n_sel), n_past_blocks - 1)
    pages = jnp.take_along_axis(page_table, blocks // 2, axis=1)
    loc = jnp.concatenate([pages.reshape(nb_batch * lq, n_sel), (blocks % 2).reshape(nb_batch * lq, n_sel)], axis=1)

    def page_spec(g, j):
        def index_map(b, i, loc, idx, val):
            row = b * lq + i
            return (layer, loc[row, g * NSA_TOPN + j], loc[row, n_sel + g * NSA_TOPN + j], 0)
        return pl.BlockSpec((None, None, rows, NSA_HD), index_map)

    grid_spec = pltpu.PrefetchScalarGridSpec(
        num_scalar_prefetch=3,
        grid=(nb_batch, lq),
        in_specs=[page_spec(g, j) for g in range(NSA_KV_HEADS) for j in range(NSA_TOPN)]
        + [pl.BlockSpec((lq, NSA_WIDTH), lambda b, i, *_: (b, C_NQ // NSA_WIDTH)),
           pl.BlockSpec((lq, 2 * NSA_KV_W), lambda b, i, *_: (b, C_SLC // (2 * NSA_KV_W))),
           pl.BlockSpec((NSA_KV_HEADS, SUBLANES, NUM_BUCKETS), lambda b, i, *_: (0, 0, 0))],
        out_specs=pl.BlockSpec((None, 1, NSA_WIDTH), lambda b, i, *_: (b * lq + i, 0, 0)),
        scratch_shapes=[pltpu.VMEM((SLOT_CHUNKS * NSA_BLOCK, NSA_HD), F32),
                        pltpu.VMEM((SLOT_CHUNKS * NSA_BLOCK, NSA_HD), F32)],
    )
    out = pl.pallas_call(
        functools.partial(_slc_sample_kernel, past=past, lq=lq),
        grid_spec=grid_spec,
        out_shape=jax.ShapeDtypeStruct((nb_batch * lq, 1, NSA_WIDTH), F32),
        compiler_params=_params("parallel", "arbitrary"),
        name="slc_sample",
    )(loc, idx_flat, val_flat, *([cache4] * n_sel), proj, proj, tcols)
    return out.reshape(nb_batch * lq, NSA_WIDTH)


def _win_sample_kernel(q_ref, wpast_ref, wnew_ref, tcol_ref, ocmp_ref, oslc_ref, z_ref, g0_ref, g1_ref,
                       o_ref, kc_ref, vc_ref, *, past, lq, wlen):
    rows = NSA_GROUP * lq
    nk = kc_ref.shape[0]
    lane = lax.broadcasted_iota(jnp.int32, (1, nk), 1)
    kpos = past - wlen + lane
    t = past + lax.broadcasted_iota(jnp.int32, (rows, 1), 0) % lq
    dist = t - kpos
    ok = (dist >= 0) & (dist <= WINDOW) & (kpos >= 0) & (lane < wlen + lq)
    gate_refs = (g0_ref, g1_ref)
    for g in range(NSA_KV_HEADS):
        kc_ref[0:wlen, :] = wpast_ref[pl.ds(g, wlen, stride=4), :]
        vc_ref[0:wlen, :] = wpast_ref[pl.ds(NSA_KV_HEADS + g, wlen, stride=4), :]
        kc_ref[wlen:, :] = jnp.zeros((nk - wlen, NSA_HD), F32)
        vc_ref[wlen:, :] = jnp.zeros((nk - wlen, NSA_HD), F32)
        kc_ref[wlen:wlen + lq, :] = wnew_ref[:, g * NSA_HD:(g + 1) * NSA_HD]
        vc_ref[wlen:wlen + lq, :] = wnew_ref[:, (NSA_KV_HEADS + g) * NSA_HD:(NSA_KV_HEADS + g + 1) * NSA_HD]
        q4 = jnp.concatenate([q_ref[:, (g * NSA_GROUP + r) * NSA_HD:(g * NSA_GROUP + r + 1) * NSA_HD]
                              for r in range(NSA_GROUP)], axis=0)
        s = _dot_nt(q4, kc_ref[...]) * SCALE + _bias_by_threshold(dist, tcol_ref[g])
        s = jnp.where(ok, s, NEG)
        p, den = _softmax_rows(s, ok)
        o_win = _dot(p, vc_ref[...]) / jnp.maximum(den, 1e-30)
        gates = jax.nn.sigmoid(gate_refs[g][...])
        for r in range(NSA_GROUP):
            h = g * NSA_GROUP + r
            sl = slice(h * NSA_HD, (h + 1) * NSA_HD)
            o = (gates[:, 3 * r:3 * r + 1] * ocmp_ref[:, sl] + gates[:, 3 * r + 1:3 * r + 2] * oslc_ref[:, sl]
                 + gates[:, 3 * r + 2:3 * r + 3] * o_win[r * lq:(r + 1) * lq])
            o_ref[:, sl] = o * _silu(z_ref[:, sl])


def _win_sample(proj, win4, tcols_rows, o_cmp, o_slc, layer, nb_batch, lq, past):
    wlen = win4.shape[2] // 4
    nk = -(-(wlen + lq) // LANES) * LANES
    row_spec = pl.BlockSpec((lq, NSA_WIDTH), lambda b: (b, 0))
    return pl.pallas_call(
        functools.partial(_win_sample_kernel, past=past, lq=lq, wlen=wlen),
        grid=(nb_batch,),
        in_specs=[pl.BlockSpec((lq, NSA_WIDTH), lambda b: (b, C_NQ // NSA_WIDTH)),
                  pl.BlockSpec((None, None, wlen * 4, NSA_HD), lambda b: (layer, b, 0, 0)),
                  pl.BlockSpec((lq, 2 * NSA_KV_W), lambda b: (b, C_WIN // (2 * NSA_KV_W))),
                  pl.BlockSpec((NSA_KV_HEADS, NSA_GROUP * lq, NUM_BUCKETS), lambda b: (0, 0, 0)),
                  row_spec, row_spec,
                  pl.BlockSpec((lq, NSA_WIDTH), lambda b: (b, C_NZ // NSA_WIDTH)),
                  pl.BlockSpec((lq, LANES), lambda b: (b, C_G0 // LANES)),
                  pl.BlockSpec((lq, LANES), lambda b: (b, C_G1 // LANES))],
        out_specs=row_spec,
        out_shape=jax.ShapeDtypeStruct((nb_batch * lq, NSA_WIDTH), F32),
        scratch_shapes=[pltpu.VMEM((nk, NSA_HD), F32), pltpu.VMEM((nk, NSA_HD), F32)],
        compiler_params=_params("parallel"),
        name="win_sample",
    )(proj, win4, proj, tcols_rows, o_cmp, o_slc, proj, proj, proj)


def _out_kernel(dn_ref, nsa_ref, x_ref, p_ref, wo_ref, gpost_ref, wpp_ref, gple_ref, wpg_ref, o_ref):
    half = dn_ref.shape[1]
    mixed = (jnp.dot(dn_ref[...].astype(BF16), wo_ref[0:half, :], preferred_element_type=F32)
             + jnp.dot(nsa_ref[...].astype(BF16), wo_ref[half:, :], preferred_element_type=F32))
    y = mixed * lax.rsqrt(jnp.mean(mixed * mixed, axis=-1, keepdims=True) + EPS) * gpost_ref[...]
    x1 = x_ref[...] + y
    e = jnp.dot(p_ref[...].astype(BF16), wpp_ref[...], preferred_element_type=F32)
    e = e * lax.rsqrt(jnp.mean(e * e, axis=-1, keepdims=True) + EPS) * gple_ref[...]
    gate = jax.nn.sigmoid(jnp.dot(x1.astype(BF16), wpg_ref[...], preferred_element_type=F32))
    o_ref[...] = x1 + gate * e


def _out_proj(dn, nsa, x, p_all, wo_all, gpost_all, wpp_all, gple_all, wpg_all, layer, tm):
    m = x.shape[0]
    once = pl.Buffered(1)

    def wspec(k):
        return pl.BlockSpec((None, k, D_MODEL), lambda i: (layer, 0, 0), pipeline_mode=once)

    return pl.pallas_call(
        _out_kernel,
        grid=(m // tm,),
        in_specs=[pl.BlockSpec((tm, dn.shape[1]), lambda i: (i, 0)),
                  pl.BlockSpec((tm, nsa.shape[1]), lambda i: (i, 0)),
                  pl.BlockSpec((tm, D_MODEL), lambda i: (i, 0)),
                  pl.BlockSpec((None, tm, PLE_DIM), lambda i: (layer, i, 0)),
                  wspec(D_MODEL), wspec(1), wspec(PLE_DIM), wspec(1), wspec(D_MODEL)],
        out_specs=pl.BlockSpec((tm, D_MODEL), lambda i: (i, 0)),
        out_shape=jax.ShapeDtypeStruct((m, D_MODEL), F32),
        compiler_params=_params("parallel"),
        name="out_proj",
    )(dn, nsa, x, p_all, wo_all, gpost_all, wpp_all, gple_all, wpg_all)


def _kv_rows_kernel(*refs, depth):
    x_refs = refs[:depth]
    cmp_ref, slc_ref, win_ref = refs[depth:]
    tm = x_refs[0].shape[0]
    width = 2 * NSA_KV_W

    def rows(o_ref, x_ref, seg):
        for s in range(4):
            o_ref[pl.ds(s, tm, stride=4), :] = x_ref[:, seg * width + s * NSA_HD:seg * width + (s + 1) * NSA_HD]

    for layer in range(depth):
        @pl.when(pl.program_id(0) == layer)
        def _(x_ref=x_refs[layer]):
            rows(cmp_ref, x_ref, 0)
            rows(slc_ref, x_ref, 1)

            @pl.when(pl.program_id(2) == pl.num_programs(2) - 1)
            def _():
                rows(win_ref, x_ref, 2)


def _kv_rows(projs, nb, seq):
    depth = len(projs)
    tm = min(WINDOW, seq)
    nr = seq // tm
    width = 3 * 2 * NSA_KV_W

    def x_spec(layer):
        return pl.BlockSpec((tm, width), lambda d, b, r: (jnp.where(d == layer, b * nr + r, 0), C_CMP // width))

    return pl.pallas_call(
        functools.partial(_kv_rows_kernel, depth=depth),
        grid=(depth, nb, nr),
        in_specs=[x_spec(layer) for layer in range(depth)],
        out_specs=[pl.BlockSpec((None, tm * 4, NSA_HD), lambda d, b, r: (d, b * nr + r, 0)),
                   pl.BlockSpec((None, tm * 4, NSA_HD), lambda d, b, r: (d, b * nr + r, 0)),
                   pl.BlockSpec((None, None, tm * 4, NSA_HD), lambda d, b, r: (d, b, 0, 0))],
        out_shape=[jax.ShapeDtypeStruct((depth, nb * seq * 4, NSA_HD), F32),
                   jax.ShapeDtypeStruct((depth, nb * seq * 4, NSA_HD), F32),
                   jax.ShapeDtypeStruct((depth, nb, tm * 4, NSA_HD), F32)],
        compiler_params=_params("arbitrary", "arbitrary", "arbitrary"),
        name="kv_rows",
    )(*projs)


W_TILE = 512


def _relayout_w_in_kernel(wt_ref, sm_ref, gate_ref, o_ref, tail_ref):
    is_tail = pl.program_id(1) == pl.num_programs(1) - 1

    @pl.when(jnp.logical_not(is_tail))
    def _():
        o_ref[...] = wt_ref[0].T.astype(BF16)

    @pl.when(is_tail)
    def _():
        half = 3 * NSA_GROUP
        tail_ref[...] = jnp.zeros(tail_ref.shape, F32)
        tail_ref[0:2 * DN_HEADS, :] = sm_ref[0]
        tail_ref[C_G0 - C_SM:C_G0 - C_SM + half, :] = gate_ref[0, 0:half, :]
        tail_ref[C_G1 - C_SM:C_G1 - C_SM + half, :] = gate_ref[0, half:2 * half, :]
        o_ref[...] = tail_ref[...].T.astype(BF16)


def _relayout_w_in(w_in):
    depth, k, _ = w_in.shape
    wt = jnp.swapaxes(w_in, 1, 2)
    n_gate = 2 * 3 * NSA_GROUP
    n_tiles = N_PAD // W_TILE
    moved = C_NQ // W_TILE

    def src_row(d, j):
        j = jnp.minimum(j, n_tiles - 2)
        return (d, pl.multiple_of(j * W_TILE + jnp.where(j >= moved, _O_NQ - C_NQ, 0), SUBLANES), 0)

    return pl.pallas_call(
        _relayout_w_in_kernel,
        grid=(depth, n_tiles),
        in_specs=[pl.BlockSpec((pl.Element(1), pl.Element(W_TILE), pl.Element(k)), src_row),
                  pl.BlockSpec((pl.Element(1), pl.Element(2 * DN_HEADS), pl.Element(k)), lambda d, j: (d, _O_BETA, 0)),
                  pl.BlockSpec((pl.Element(1), pl.Element(n_gate), pl.Element(k)), lambda d, j: (d, _O_GATE, 0))],
        out_specs=pl.BlockSpec((None, k, W_TILE), lambda d, j: (d, 0, j)),
        out_shape=jax.ShapeDtypeStruct((depth, k, N_PAD), BF16),
        scratch_shapes=[pltpu.VMEM((W_TILE, k), F32)],
        compiler_params=_params("parallel", "arbitrary"),
        name="relayout_w_in",
    )(wt, wt, wt)


def _bias_expand_kernel(t_ref, bucket_ref, o_ref):
    bucket = bucket_ref[...]
    for h in range(NSA_HEADS):
        acc = jnp.full(bucket.shape, t_ref[0, h], F32)
        for b in range(1, NUM_BUCKETS):
            acc = jnp.where(bucket >= b, t_ref[b, h], acc)
        o_ref[h] = acc


def _bias_expand(table, bucket):
    rows, cols = bucket.shape
    tr = max(t for t in range(SUBLANES, min(rows, 512) + 1, SUBLANES) if rows % t == 0)
    return pl.pallas_call(
        _bias_expand_kernel,
        grid=(rows // tr,),
        in_specs=[pl.BlockSpec(memory_space=pltpu.SMEM), pl.BlockSpec((tr, cols), lambda i: (i, 0))],
        out_specs=pl.BlockSpec((NSA_HEADS, tr, cols), lambda i: (0, i, 0)),
        out_shape=jax.ShapeDtypeStruct((NSA_HEADS, rows, cols), F32),
        compiler_params=_params("parallel"),
        name="bias_expand",
    )(table, jnp.asarray(bucket))


def _bias_tiles(table):
    i = np.arange(BT)
    d = np.arange(N_BIAS_TILES)[:, None, None] * BT + i[None, :, None] - i[None, None, :]
    tiles = _bias_expand(table * LOG2E, _bucket_np(d).reshape(N_BIAS_TILES * BT, BT))
    return tiles.reshape(NSA_HEADS, N_BIAS_TILES, BT, BT)


def _bias_cmp(table, pos0, lq, nb):
    dist = (pos0 + np.arange(lq))[:, None] - (np.arange(nb) * NSA_BLOCK + NSA_BLOCK - 1)[None, :]
    return _bias_expand(table, _bucket_np(dist))


def kernel(x_prompt, x_sample, state_dn_S, state_dn_conv, cache_win_kv, cache_cmp_kv, cache_slc_kv, page_table,
           p_prompt, p_sample, rel_bias_table, w_in, w_out, g_pre, g_post, dn_conv_w, dn_A_log, dn_dt_bias,
           dn_norm_w, cmp_pos_w, cmp_w1, cmp_w2, w_ple_proj, g_ple, w_ple_gate):
    depth = w_in.shape[0]
    bp, seq, _ = x_prompt.shape
    bs, lq, _ = x_sample.shape
    n_pages = page_table.shape[1]
    past = n_pages * PAGE_SIZE
    n_pool = cache_cmp_kv.shape[1]
    wlen = cache_win_kv.shape[2]
    mp, ms = bp * seq, bs * lq
    assert seq % PK == 0 and seq % DN_CHUNK == 0 and lq == SUBLANES and past % NSA_BLOCK == 0
    kv_row = (2, NSA_KV_HEADS, NSA_HD)

    w_in_r = _relayout_w_in(w_in)
    w_out_b = w_out.astype(BF16)
    wpp_b = w_ple_proj.astype(BF16)
    wpg_b = w_ple_gate.astype(BF16)
    g_pre3 = g_pre[:, None, :]
    g_post3 = g_post[:, None, :]
    g_ple3 = g_ple[:, None, :]
    nw3 = dn_norm_w[:, None, :]
    lane_pad = ((0, 0), (DN_HEADS, LANES - 2 * DN_HEADS))
    dn_par = jnp.stack([jnp.pad(dn_A_log, lane_pad), jnp.pad(dn_dt_bias, lane_pad)], axis=1)
    pw_prompt = jnp.broadcast_to(cmp_pos_w[:, :, None, :, None], (depth, 2, NSA_KV_HEADS, NSA_BLOCK, NSA_HD))
    pw_prompt = jnp.transpose(pw_prompt, (0, 3, 1, 2, 4)).reshape(depth, NSA_BLOCK, 2 * NSA_KV_W)
    pw_page = jnp.broadcast_to(cmp_pos_w[:, None, None, :, :, None],
                               (depth, PAGE_SIZE // NSA_BLOCK, NSA_KV_HEADS, 2, NSA_BLOCK, NSA_HD))
    pw_page = jnp.transpose(pw_page, (0, 1, 4, 3, 2, 5)).reshape(depth, PAGE_SIZE * 4, NSA_HD)
    table = rel_bias_table.astype(F32)
    bias_tiles = _bias_tiles(table)
    nb_p = seq // NSA_BLOCK
    nb_s = past // NSA_BLOCK
    nbs_s = -(-(past + lq) // NSA_BLOCK)
    nbp_s = -(-nbs_s // LANES) * LANES
    bias_cmp_p = _bias_cmp(table, 0, seq, nb_p)
    bias_cmp_s = _bias_cmp(table, past, lq, nb_s)
    tcols = jnp.pad(table.T.reshape(NSA_KV_HEADS, NSA_GROUP, NUM_BUCKETS),
                    ((0, 0), (0, SUBLANES - NSA_GROUP), (0, 0)))
    tcols_rows = jnp.repeat(table.T.reshape(NSA_KV_HEADS, NSA_GROUP, NUM_BUCKETS), lq, axis=1)

    conv0_s = jnp.pad(state_dn_conv, ((0, 0), (0, 0), (SUBLANES - (CONV_W - 1), 0), (0, 0)))
    conv0_p = jnp.zeros((1, bp, SUBLANES, DN_CONV_CH), F32)
    s0_p = jnp.zeros((1, bp, DN_HEADS, DN_DK, DN_DV), F32)
    cmp4 = cache_cmp_kv.reshape(depth, n_pool, PAGE_SIZE * 4, NSA_HD)
    slc4 = cache_slc_kv.reshape(depth, n_pool, PAGE_SIZE * 4, NSA_HD)
    win4 = cache_win_kv.reshape(depth, bs, wlen * 4, NSA_HD)
    p_prompt3 = p_prompt.reshape(depth, mp, PLE_DIM)
    p_sample3 = p_sample.reshape(depth, ms, PLE_DIM)

    xp = x_prompt.reshape(mp, D_MODEL)
    xs = x_sample.reshape(ms, D_MODEL)
    outs_p = [[] for _ in range(2)]
    outs_s = [[] for _ in range(5)]
    projs_p = []
    tq_p = min(256, seq)
    tm_p = min(IN_TM, mp)
    pages = min(POOL_PAGES, n_pages)
    fuse_pool = (mp // tm_p, N_PAD // IN_TN) == (bs, n_pages // pages)
    for i in range(depth):
        if fuse_pool:
            proj, pooled = _in_proj_pool(xp, g_pre3, w_in_r, cmp4, page_table, pw_page, i, tm_p, IN_TN, pages)
        else:
            proj = _in_proj(xp, g_pre3, w_in_r, i, tm_p)
            pooled = _pool_paged(cmp4, page_table, pw_page, i, pages)
        dn_o, conv_n, s_n = _deltanet(proj, bp, seq, DN_CHUNK, conv0_p, s0_p, 0, dn_conv_w, dn_par, nw3, i)
        ckv = _cmp_mlp(_pool_prompt(proj, bp, seq, pw_prompt, i), cmp_w1, cmp_w2, i)
        o_cmp, sel = _cmp_attn(proj, ckv, bias_cmp_p, bp, seq, 0, tq_p, nb_p, nb_p, True)
        nsa_o = _nsa_prompt(proj, sel, bias_tiles, o_cmp, bp, seq)
        xp = _out_proj(dn_o, nsa_o, xp, p_prompt3, w_out_b, g_post3, wpp_b, g_ple3, wpg_b, i, min(256, mp))
        projs_p.append(proj)
        outs_p[0].append(s_n)
        outs_p[1].append(conv_n[:, SUBLANES - (CONV_W - 1):])

        proj = _in_proj(xs, g_pre3, w_in_r, i, ms)
        dn_o, conv_n, s_n = _deltanet(proj, bs, lq, lq, conv0_s, state_dn_S, i, dn_conv_w, dn_par, nw3, i)
        ckv = _cmp_mlp(pooled, cmp_w1, cmp_w2, i)
        o_cmp, score = _cmp_attn(proj, ckv, bias_cmp_s, bs, lq, past, lq, nbs_s, nbp_s, False)
        score_t = jnp.transpose(score.reshape(ms * NSA_KV_HEADS, nbp_s))
        _, idx_t, val_t = _topk(score_t, nbs_s)
        o_slc = _slc_sample(proj, slc4, page_table, idx_t.T.reshape(-1), val_t.T.reshape(-1), tcols, i,
                            bs, lq, past)
        nsa_o = _win_sample(proj, win4, tcols_rows, o_cmp, o_slc, i, bs, lq, past)
        xs = _out_proj(dn_o, nsa_o, xs, p_sample3, w_out_b, g_post3, wpp_b, g_ple3, wpg_b, i, ms)
        kv_all = proj[:, C_CMP:C_WIN + 2 * NSA_KV_W].reshape(bs, lq, 3, *kv_row)
        win_all = jnp.concatenate([cache_win_kv[i], kv_all[:, :, 2]], axis=1)
        outs_s[0].append(s_n)
        outs_s[1].append(conv_n[:, SUBLANES - (CONV_W - 1):])
        outs_s[2].append(win_all[:, win_all.shape[1] - min(WINDOW, past + lq):])
        outs_s[3].append(kv_all[:, :, 0])
        outs_s[4].append(kv_all[:, :, 1])

    p_cmp, p_slc, p_win = _kv_rows(projs_p, bp, seq)
    return ((xp.reshape(bp, seq, D_MODEL), xs.reshape(bs, lq, D_MODEL), jnp.stack(outs_p[0]), jnp.stack(outs_p[1]),
             p_win.reshape(depth, bp, min(WINDOW, seq), *kv_row), p_cmp.reshape(depth, bp, seq, *kv_row),
             p_slc.reshape(depth, bp, seq, *kv_row)) + tuple(jnp.stack(o) for o in outs_s))
```

```python
import functools
import math

import numpy as np
import jax
import jax.numpy as jnp
from jax import lax
from jax.experimental import pallas as pl
from jax.experimental.pallas import tpu as pltpu

F32 = jnp.float32
BF16 = jnp.bfloat16

D_MODEL = 2048
DN_HEADS = 8
DN_DK = 128
DN_DV = 128
DN_QK_W = DN_HEADS * DN_DK
DN_CONV_CH = 2 * DN_QK_W + DN_HEADS * DN_DV
CONV_W = 4
DN_CHUNK = 64
NSA_HEADS = 8
NSA_KV_HEADS = 2
NSA_GROUP = 4
NSA_HD = 128
NSA_WIDTH = NSA_HEADS * NSA_HD
NSA_KV_W = NSA_KV_HEADS * NSA_HD
NSA_BLOCK = 64
NSA_TOPN = 16
WINDOW = 512
SCALE = NSA_HD ** -0.5
NUM_BUCKETS = 32
REL_MAX_DIST = 1024
PLE_DIM = 256
PAGE_SIZE = 128
EPS = 1e-6
NEG = -1e30
LOG2E = math.log2(math.e)

VMEM_LIMIT_BYTES = 56 * 1024 * 1024
LANES = 128
SUBLANES = 8

C_QKV = 0
C_Z = 3072
C_NQ = 4096
C_NZ = 5120
C_CMP = 6144
C_SLC = 6656
C_WIN = 7168
C_SM = 7680
C_G0 = 7808
C_G1 = 7936
N_PAD = 8192
_O_BETA = DN_CONV_CH + DN_HEADS * DN_DV
_O_NQ = _O_BETA + 2 * DN_HEADS
_O_GATE = _O_NQ + 2 * NSA_WIDTH + 3 * 2 * NSA_KV_W
N_BIAS_TILES = 9


def _bucket_np(dist):
    dist = np.maximum(np.asarray(dist, np.int64), 0)
    exact = NUM_BUCKETS // 2
    scaled = np.log(np.maximum(dist, 1).astype(np.float64) / exact) / math.log(REL_MAX_DIST / exact)
    large = np.minimum(exact + (scaled * (NUM_BUCKETS - exact)).astype(np.int64), NUM_BUCKETS - 1)
    return np.where(dist < exact, dist, large).astype(np.int32)


_BUCKET_THR = tuple(int(np.argmax(_bucket_np(np.arange(2 * REL_MAX_DIST)) >= b)) for b in range(1, NUM_BUCKETS))


def _dot(a, b):
    return jnp.dot(a.astype(BF16), b.astype(BF16), preferred_element_type=F32)


def _dot_nt(a, b):
    return lax.dot_general(a.astype(BF16), b.astype(BF16), (((1,), (1,)), ((), ())), preferred_element_type=F32)


def _dot_hi(a, b):
    return jnp.dot(a, b, preferred_element_type=F32, precision=lax.Precision.HIGHEST)


def _dot_hi_nt(a, b):
    return lax.dot_general(a, b, (((1,), (1,)), ((), ())), preferred_element_type=F32,
                           precision=lax.Precision.HIGHEST)


def _silu(x):
    return x * jax.nn.sigmoid(x)


def _params(*sem):
    return pltpu.CompilerParams(dimension_semantics=sem, vmem_limit_bytes=VMEM_LIMIT_BYTES)


def _in_proj_kernel(x_ref, g_ref, w_ref, o_ref, xn_ref):
    @pl.when(pl.program_id(1) == 0)
    def _():
        x = x_ref[...]
        y = x * lax.rsqrt(jnp.mean(x * x, axis=-1, keepdims=True) + EPS)
        xn_ref[...] = (y * g_ref[...]).astype(BF16)

    o_ref[...] = jnp.dot(xn_ref[...], w_ref[...], preferred_element_type=F32)


IN_TM = 1024
IN_TN = 1024
POOL_PAGES = 16


def _in_proj(x, g_all, w_all, layer, tm):
    m = x.shape[0]
    tn = IN_TN
    return pl.pallas_call(
        _in_proj_kernel,
        grid=(m // tm, N_PAD // tn),
        in_specs=[pl.BlockSpec((tm, D_MODEL), lambda i, j: (i, 0)),
                  pl.BlockSpec((None, 1, D_MODEL), lambda i, j: (layer, 0, 0)),
                  pl.BlockSpec((None, D_MODEL, tn), lambda i, j: (layer, 0, j))],
        out_specs=pl.BlockSpec((tm, tn), lambda i, j: (i, j)),
        out_shape=jax.ShapeDtypeStruct((m, N_PAD), F32),
        scratch_shapes=[pltpu.VMEM((tm, D_MODEL), BF16)],
        compiler_params=_params("parallel", "arbitrary"),
        name="in_proj",
    )(x, g_all, w_all)


def _dn_prep_kernel(qkv_ref, sm_ref, conv0_ref, cw_ref, par_ref,
                    u_ref, w_ref, qd_ref, kd_ref, qk_ref, gl_ref, convn_ref, xs_ref, *, chunk, cs):
    tb = chunk * cs
    halo = SUBLANES

    @pl.when(pl.program_id(1) == 0)
    def _():
        xs_ref[0:halo, :] = conv0_ref[...]

    xs_ref[halo:halo + tb, :] = qkv_ref[...]
    base = halo - (CONV_W - 1)
    y = xs_ref[base:base + tb, :] * cw_ref[0:1, :]
    for j in range(1, CONV_W):
        y = y + xs_ref[base + j:base + j + tb, :] * cw_ref[j:j + 1, :]
    y = _silu(y)
    tail = xs_ref[tb:tb + halo, :]
    xs_ref[0:halo, :] = tail
    convn_ref[...] = tail

    sm = sm_ref[...]
    beta_all = jax.nn.sigmoid(sm)
    xg = sm + par_ref[1:2, :]
    softplus = jnp.maximum(xg, 0.0) + jnp.log1p(jnp.exp(-jnp.abs(xg)))
    g_all = -jnp.exp(par_ref[0:1, :]) * softplus
    row = lax.broadcasted_iota(jnp.int32, (tb, tb), 0)
    col = lax.broadcasted_iota(jnp.int32, (tb, tb), 1)
    in_chunk_lower = (row >= col) & (row // chunk == col // chunk)
    gc_all = _dot_hi(jnp.where(in_chunk_lower, 1.0, 0.0), g_all)
    pick = (lax.broadcasted_iota(jnp.int32, (DN_HEADS, LANES), 1)
            == lax.broadcasted_iota(jnp.int32, (DN_HEADS, LANES), 0) + DN_HEADS).astype(F32)
    gc_rows = _dot_hi_nt(pick, gc_all)
    for c in range(cs):
        gl_ref[c] = gc_all[(c + 1) * chunk - 1:(c + 1) * chunk, :]
    r64 = lax.broadcasted_iota(jnp.int32, (chunk, chunk), 0)
    c64 = lax.broadcasted_iota(jnp.int32, (chunk, chunk), 1)
    lower = r64 >= c64
    strict = r64 > c64

    lhs, kns, rhs, decays = [], [], [], []
    for h in range(DN_HEADS):
        hc = slice(h * DN_DV, (h + 1) * DN_DV)
        qh = y[:, h * DN_DK:(h + 1) * DN_DK]
        kh = y[:, DN_QK_W + h * DN_DK:DN_QK_W + (h + 1) * DN_DK]
        vh = y[:, 2 * DN_QK_W + h * DN_DV:2 * DN_QK_W + (h + 1) * DN_DV]
        qn = qh * lax.rsqrt(jnp.sum(qh * qh, axis=-1, keepdims=True) + EPS) * (DN_DK ** -0.5)
        kn = kh * lax.rsqrt(jnp.sum(kh * kh, axis=-1, keepdims=True) + EPS)
        beta = beta_all[:, h:h + 1]
        gc = gc_all[:, DN_HEADS + h:DN_HEADS + h + 1]
        egc = jnp.exp(gc)
        kb = kn * beta
        rhs_h = jnp.concatenate([vh * beta, kb * egc], axis=-1)
        qd_ref[:, hc] = (qn * egc).astype(qd_ref.dtype)
        for c in range(cs):
            rs = slice(c * chunk, (c + 1) * chunk)
            gcc = gc[rs]
            decays.append(jnp.exp(jnp.where(lower, gcc - gc_rows[h:h + 1, rs], NEG)))
            lhs.append(jnp.concatenate([kb[rs], qn[rs]], axis=0))
            kns.append(kn[rs])
            rhs.append(rhs_h[rs])
            kd_ref[rs, hc] = (kn[rs] * jnp.exp(gcc[chunk - 1:chunk, :] - gcc)).astype(kd_ref.dtype)
    decay = jnp.stack(decays)
    both = jnp.einsum('nid,njd->nij', jnp.stack(lhs).astype(BF16), jnp.stack(kns).astype(BF16),
                      preferred_element_type=F32)
    a = jnp.where(strict[None], both[:, :chunk] * decay, 0.0)
    qk = both[:, chunk:] * decay

    def bmm(p, q):
        return jnp.einsum('nij,njk->nik', p, q, preferred_element_type=F32)

    def split(v):
        hi = v.astype(BF16)
        return hi, (v - hi.astype(F32)).astype(BF16)

    def bmm3(p, q):
        (ph, pl_), (qh, ql) = p, q
        return bmm(ph, qh) + (bmm(ph, ql) + bmm(pl_, qh))

    pw_s = split(a)
    inv = jnp.where(r64 == c64, 1.0, 0.0)[None] - a
    for _ in range(int(math.log2(chunk)) - 1):
        pw_s = split(bmm3(pw_s, pw_s))
        inv = inv + bmm3(pw_s, split(inv))
    x = bmm3(split(inv), split(jnp.stack(rhs)))
    for h in range(DN_HEADS):
        hc = slice(h * DN_DV, (h + 1) * DN_DV)
        for c in range(cs):
            rs = slice(c * chunk, (c + 1) * chunk)
            n = h * cs + c
            u_ref[rs, hc] = x[n, :, :DN_DV]
            w_ref[rs, hc] = x[n, :, DN_DV:].astype(w_ref.dtype)
            qk_ref[rs, h * chunk:(h + 1) * chunk] = qk[n]


def _dn_scan_kernel(u_ref, w_ref, qd_ref, kd_ref, qk_ref, gl_ref, z_ref, s0_ref, nw_ref,
                    o_ref, sout_ref, s_ref, *, chunk, cs):
    @pl.when(pl.program_id(1) == 0)
    def _():
        s_ref[...] = s0_ref[...]

    def heads(ref, rs, width):
        return jnp.stack([ref[rs, h * width:(h + 1) * width] for h in range(DN_HEADS)])

    def bdot(spec, p, q):
        return jnp.einsum(spec, p.astype(BF16), q.astype(BF16), preferred_element_type=F32)

    for c in range(cs):
        rs = slice(c * chunk, (c + 1) * chunk)
        decay_last = jnp.exp(gl_ref[c])
        decay_last = jnp.stack([decay_last[:, DN_HEADS + h:DN_HEADS + h + 1] for h in range(DN_HEADS)])
        s = s_ref[...]
        sb = s.astype(BF16)
        v_new = heads(u_ref, rs, DN_DV) - bdot('hcd,hde->hce', heads(w_ref, rs, DN_DK), sb)
        o = (bdot('hcd,hde->hce', heads(qd_ref, rs, DN_DK), sb)
             + bdot('hcs,hse->hce', heads(qk_ref, rs, chunk), v_new))
        s_ref[...] = s * decay_last + bdot('hcd,hce->hde', heads(kd_ref, rs, DN_DK), v_new)
        on = o * lax.rsqrt(jnp.mean(o * o, axis=-1, keepdims=True) + EPS) * nw_ref[...]
        for h in range(DN_HEADS):
            hc = slice(h * DN_DV, (h + 1) * DN_DV)
            o_ref[rs, hc] = on[h] * _silu(z_ref[rs, hc])

    @pl.when(pl.program_id(1) == pl.num_programs(1) - 1)
    def _():
        sout_ref[...] = s_ref[...]


def _deltanet(proj, nb, seq, chunk, conv0_all, s0_all, state_layer, cw_all, par_all, nw_all, layer):
    m = nb * seq
    nchunks = seq // chunk
    width = DN_HEADS * DN_DV
    hs = (DN_HEADS, DN_DK, DN_DV)
    cs_a = 4 if nchunks % 4 == 0 else 1
    na = nchunks // cs_a
    ta = cs_a * chunk
    op_dtype = BF16 if chunk % (2 * SUBLANES) == 0 else F32
    u, w, qd, kd, qk, gl, conv_n = pl.pallas_call(
        functools.partial(_dn_prep_kernel, chunk=chunk, cs=cs_a),
        grid=(nb, na),
        in_specs=[pl.BlockSpec((ta, DN_CONV_CH), lambda b, c: (b * na + c, C_QKV // DN_CONV_CH)),
                  pl.BlockSpec((ta, LANES), lambda b, c: (b * na + c, C_SM // LANES)),
                  pl.BlockSpec((None, None, SUBLANES, DN_CONV_CH), lambda b, c: (state_layer, b, 0, 0)),
                  pl.BlockSpec((None, CONV_W, DN_CONV_CH), lambda b, c: (layer, 0, 0)),
                  pl.BlockSpec((None, 2, LANES), lambda b, c: (layer, 0, 0))],
        out_specs=[pl.BlockSpec((ta, width), lambda b, c: (b * na + c, 0)),
                   pl.BlockSpec((ta, width), lambda b, c: (b * na + c, 0)),
                   pl.BlockSpec((ta, width), lambda b, c: (b * na + c, 0)),
                   pl.BlockSpec((ta, width), lambda b, c: (b * na + c, 0)),
                   pl.BlockSpec((ta, DN_HEADS * chunk), lambda b, c: (b * na + c, 0)),
                   pl.BlockSpec((cs_a, 1, LANES), lambda b, c: (b * na + c, 0, 0)),
                   pl.BlockSpec((None, SUBLANES, DN_CONV_CH), lambda b, c: (b, 0, 0))],
        out_shape=[jax.ShapeDtypeStruct((m, width), F32),
                   jax.ShapeDtypeStruct((m, width), op_dtype),
                   jax.ShapeDtypeStruct((m, width), op_dtype),
                   jax.ShapeDtypeStruct((m, width), op_dtype),
                   jax.ShapeDtypeStruct((m, DN_HEADS * chunk), F32),
                   jax.ShapeDtypeStruct((nb * nchunks, 1, LANES), F32),
                   jax.ShapeDtypeStruct((nb, SUBLANES, DN_CONV_CH), F32)],
        scratch_shapes=[pltpu.VMEM((ta + SUBLANES, DN_CONV_CH), F32)],
        compiler_params=_params("parallel", "arbitrary"),
        name="dn_prep",
    )(proj, proj, conv0_all, cw_all, par_all)

    cs_b = 8 if nchunks % 8 == 0 else 1
    nbk = nchunks // cs_b
    tbk = cs_b * chunk
    row_spec = pl.BlockSpec((tbk, width), lambda b, c: (b * nbk + c, 0))
    o, s_n = pl.pallas_call(
        functools.partial(_dn_scan_kernel, chunk=chunk, cs=cs_b),
        grid=(nb, nbk),
        in_specs=[row_spec, row_spec, row_spec, row_spec,
                  pl.BlockSpec((tbk, DN_HEADS * chunk), lambda b, c: (b * nbk + c, 0)),
                  pl.BlockSpec((cs_b, 1, LANES), lambda b, c: (b * nbk + c, 0, 0)),
                  pl.BlockSpec((tbk, width), lambda b, c: (b * nbk + c, C_Z // width)),
                  pl.BlockSpec((None, None) + hs, lambda b, c: (state_layer, b, 0, 0, 0)),
                  pl.BlockSpec((None, 1, DN_DV), lambda b, c: (layer, 0, 0))],
        out_specs=[row_spec, pl.BlockSpec((None,) + hs, lambda b, c: (b, 0, 0, 0))],
        out_shape=[jax.ShapeDtypeStruct((m, width), F32), jax.ShapeDtypeStruct((nb,) + hs, F32)],
        scratch_shapes=[pltpu.VMEM(hs, F32)],
        compiler_params=_params("parallel", "arbitrary"),
        name="dn_scan",
    )(u, w, qd, kd, qk, gl, proj, s0_all, nw_all)
    return o, conv_n, s_n


def _pool_kernel(x_ref, pw_ref, o_ref):
    rows = x_ref.shape[0]
    x = x_ref[...].reshape(rows // NSA_BLOCK, NSA_BLOCK, 2 * NSA_KV_W)
    o_ref[...] = jnp.sum(x * pw_ref[...][None], axis=1)


def _pool_prompt(proj, nb, seq, pw_all, layer):
    rows = min(seq, 1024)
    nr = seq // rows
    width = 2 * NSA_KV_W
    return pl.pallas_call(
        _pool_kernel,
        grid=(nb, nr),
        in_specs=[pl.BlockSpec((rows, width), lambda b, r: (b * nr + r, C_CMP // width)),
                  pl.BlockSpec((None, NSA_BLOCK, width), lambda b, r: (layer, 0, 0))],
        out_specs=pl.BlockSpec((None, rows // NSA_BLOCK, width), lambda b, r: (b, r, 0)),
        out_shape=jax.ShapeDtypeStruct((nb, seq // NSA_BLOCK, width), F32),
        compiler_params=_params("parallel", "parallel"),
        name="pool_prompt",
    )(proj, pw_all)


def _pool_pages(page_refs, pw_ref, o_ref):
    rows = PAGE_SIZE * 4
    per_block = rows // 2 // SUBLANES
    for k, page_ref in enumerate(page_refs):
        prod = page_ref[...] * pw_ref[...]
        y = jnp.sum(prod.reshape(2, per_block, SUBLANES, NSA_HD), axis=1)
        y = y[:, 0:4, :] + y[:, 4:8, :]
        for s in range(4):
            o_ref[2 * k:2 * k + 2, s * NSA_HD:(s + 1) * NSA_HD] = y[:, s, :]


def _paged_pool_kernel(pt_ref, *refs, pages):
    del pt_ref
    _pool_pages(refs[:pages], refs[pages], refs[pages + 1])


def _in_proj_pool_kernel(pt_ref, x_ref, g_ref, w_ref, *refs, pages):
    del pt_ref
    o_ref, pooled_ref, xn_ref = refs[pages + 1:]
    _in_proj_kernel(x_ref, g_ref, w_ref, o_ref, xn_ref)
    _pool_pages(refs[:pages], refs[pages], pooled_ref)


def _in_proj_pool(x, g_all, w_all, cache4, page_table, pw_all, layer, tm, tn, pages):
    m = x.shape[0]
    nb, n_pages = page_table.shape
    assert (m // tm, N_PAD // tn) == (nb, n_pages // pages)
    rows = PAGE_SIZE * 4

    def page_spec(k):
        return pl.BlockSpec((None, None, rows, NSA_HD), lambda i, j, pt: (layer, pt[i, j * pages + k], 0, 0))

    grid_spec = pltpu.PrefetchScalarGridSpec(
        num_scalar_prefetch=1,
        grid=(m // tm, N_PAD // tn),
        in_specs=[pl.BlockSpec((tm, D_MODEL), lambda i, j, pt: (i, 0)),
                  pl.BlockSpec((None, 1, D_MODEL), lambda i, j, pt: (layer, 0, 0)),
                  pl.BlockSpec((None, D_MODEL, tn), lambda i, j, pt: (layer, 0, j))]
        + [page_spec(k) for k in range(pages)]
        + [pl.BlockSpec((None, rows, NSA_HD), lambda i, j, pt: (layer, 0, 0))],
        out_specs=[pl.BlockSpec((tm, tn), lambda i, j, pt: (i, j)),
                   pl.BlockSpec((None, 2 * pages, 4 * NSA_HD), lambda i, j, pt: (i, j, 0))],
        scratch_shapes=[pltpu.VMEM((tm, D_MODEL), BF16)],
    )
    return pl.pallas_call(
        functools.partial(_in_proj_pool_kernel, pages=pages),
        grid_spec=grid_spec,
        out_shape=[jax.ShapeDtypeStruct((m, N_PAD), F32),
                   jax.ShapeDtypeStruct((nb, 2 * n_pages, 4 * NSA_HD), F32)],
        compiler_params=_params("parallel", "arbitrary"),
        name="in_proj_pool",
    )(page_table, x, g_all, w_all, *([cache4] * pages), pw_all)


def _pool_paged(cache4, page_table, pw_all, layer, pages):
    nb, n_pages = page_table.shape
    steps = n_pages // pages
    rows = PAGE_SIZE * 4

    def page_spec(k):
        return pl.BlockSpec((None, None, rows, NSA_HD),
                            lambda b, s, pt: (layer, pt[b, s * pages + k], 0, 0))

    grid_spec = pltpu.PrefetchScalarGridSpec(
        num_scalar_prefetch=1,
        grid=(nb, steps),
        in_specs=[page_spec(k) for k in range(pages)]
        + [pl.BlockSpec((None, rows, NSA_HD), lambda b, s, pt: (layer, 0, 0))],
        out_specs=pl.BlockSpec((None, 2 * pages, 4 * NSA_HD), lambda b, s, pt: (b, s, 0)),
    )
    return pl.pallas_call(
        functools.partial(_paged_pool_kernel, pages=pages),
        grid_spec=grid_spec,
        out_shape=jax.ShapeDtypeStruct((nb, 2 * n_pages, 4 * NSA_HD), F32),
        compiler_params=_params("parallel", "arbitrary"),
        name="pool_paged",
    )(page_table, *([cache4] * pages), pw_all)


def _select_blocks(score, st_ref, nbp, n_causal):
    st_ref[...] = score.T
    halves = [st_ref[g * nbp:(g + 1) * nbp, :] for g in range(NSA_KV_HEADS)]
    bi = lax.broadcasted_iota(jnp.int32, halves[0].shape, 0)

    def body(j, ranks):
        tie = jnp.where(j < bi, 1, 0)
        out = []
        for g in range(NSA_KV_HEADS):
            other = st_ref[pl.ds(g * nbp + j, 1), :]
            out.append(ranks[g] + jnp.where(other > halves[g], 1, jnp.where(other == halves[g], tie, 0)))
        return tuple(out)

    ranks = lax.fori_loop(0, n_causal, body, tuple(jnp.zeros(bi.shape, jnp.int32) for _ in halves))
    keep = [jnp.where((ranks[g] < NSA_TOPN) & (halves[g] >= 0.0), 1.0, 0.0) for g in range(NSA_KV_HEADS)]
    return jnp.concatenate(keep, axis=0).T


def _cmp_attn_kernel(q_ref, pooled_ref, w1_ref, w2_ref, bias_ref, o_ref, sc_ref, ckv_ref, *scratch,
                     pos0, tq, nb, nbs, nbp, select):
    qi = pl.program_id(1)

    @pl.when(qi == 0)
    def _():
        for s in range(2 * NSA_KV_HEADS):
            kv = s // NSA_KV_HEADS
            h = _silu(_dot_hi(pooled_ref[:, s * NSA_HD:(s + 1) * NSA_HD], w1_ref[kv]))
            ckv_ref[:, s * NSA_HD:(s + 1) * NSA_HD] = _dot_hi(h, w2_ref[kv])

    t = pos0 + qi * tq + lax.broadcasted_iota(jnp.int32, (tq, 1), 0)
    blk = lax.broadcasted_iota(jnp.int32, (1, nb), 1)
    valid = t >= blk * NSA_BLOCK + (NSA_BLOCK - 1)
    bi = lax.broadcasted_iota(jnp.int32, (1, nbp), 1)
    cur = t // NSA_BLOCK
    forced = (bi == 0) | (bi == cur) | (bi == cur - 1)
    causal = bi <= cur
    scores = []
    for g in range(NSA_KV_HEADS):
        ck = ckv_ref[:, g * NSA_HD:(g + 1) * NSA_HD]
        cv = ckv_ref[:, (NSA_KV_HEADS + g) * NSA_HD:(NSA_KV_HEADS + g + 1) * NSA_HD]
        imp = jnp.zeros((tq, nb), F32)
        for r in range(NSA_GROUP):
            h = g * NSA_GROUP + r
            logits = _dot_nt(q_ref[:, h * NSA_HD:(h + 1) * NSA_HD], ck) * SCALE + bias_ref[h]
            logits = jnp.where(valid, logits, NEG)
            mx = jnp.max(logits, axis=-1, keepdims=True)
            p = jnp.where(valid, jnp.exp(logits - mx), 0.0)
            p = p / jnp.maximum(jnp.sum(p, axis=-1, keepdims=True), 1e-30)
            o_ref[:, h * NSA_HD:(h + 1) * NSA_HD] = _dot(p, cv)
            imp = imp + p
        if nbp > nb:
            imp = jnp.concatenate([imp, jnp.zeros((tq, nbp - nb), F32)], axis=-1)
        score = jnp.where(causal, jnp.where(forced, NSA_GROUP + 1.0, imp), -1.0)
        scores.append(jnp.where(bi < nbs, score, -2.0))
    score = jnp.concatenate(scores, axis=-1)
    if select:
        n_causal = jnp.minimum((pos0 + (qi + 1) * tq - 1) // NSA_BLOCK + 1, nbs)
        sc_ref[...] = _select_blocks(score, scratch[0], nbp, n_causal)
    else:
        sc_ref[...] = score


def _cmp_attn(proj, pooled, w1_all, w2_all, layer, bias_cmp, nb_batch, seq, pos0, tq, nbs, nbp, select):
    nq = seq // tq
    nb = pooled.shape[1]
    m = nb_batch * seq
    bias_tq = bias_cmp.shape[1] // nq
    wspec = pl.BlockSpec((None, 2, NSA_HD, NSA_HD), lambda b, i: (layer, 0, 0, 0))
    return pl.pallas_call(
        functools.partial(_cmp_attn_kernel, pos0=pos0, tq=tq, nb=nb, nbs=nbs, nbp=nbp, select=select),
        scratch_shapes=[pltpu.VMEM((nb, 2 * NSA_KV_W), F32)]
        + ([pltpu.VMEM((NSA_KV_HEADS * nbp, tq), F32)] if select else []),
        grid=(nb_batch, nq),
        in_specs=[pl.BlockSpec((tq, NSA_WIDTH), lambda b, i: (b * nq + i, C_NQ // NSA_WIDTH)),
                  pl.BlockSpec((None, nb, 2 * NSA_KV_W), lambda b, i: (b, 0, 0)), wspec, wspec,
                  pl.BlockSpec((NSA_HEADS, bias_tq, nb), lambda b, i: (0, i, 0))],
        out_specs=[pl.BlockSpec((tq, NSA_WIDTH), lambda b, i: (b * nq + i, 0)),
                   pl.BlockSpec((tq, NSA_KV_HEADS * nbp), lambda b, i: (b * nq + i, 0))],
        out_shape=[jax.ShapeDtypeStruct((m, NSA_WIDTH), F32),
                   jax.ShapeDtypeStruct((m, NSA_KV_HEADS * nbp), F32)],
        compiler_params=_params("parallel", "arbitrary"),
        name="cmp_attn",
    )(proj, pooled, w1_all, w2_all, bias_cmp)


def _topk_kernel(s_ref, sel_ref, idx_ref, val_ref, *, nbs):
    s = s_ref[...]
    bi = lax.broadcasted_iota(jnp.int32, s.shape, 0)

    def body(j, rank):
        other = s_ref[pl.ds(j, 1), :]
        beats = (other > s) | ((other == s) & (j < bi))
        return rank + beats.astype(jnp.int32)

    rank = lax.fori_loop(0, nbs, body, jnp.zeros(s.shape, jnp.int32))
    ok = s >= 0.0
    sel_ref[...] = jnp.where((rank < NSA_TOPN) & ok, 1.0, 0.0)
    for r in range(NSA_TOPN):
        hit = rank == r
        idx_ref[r:r + 1, :] = jnp.sum(jnp.where(hit, bi, 0), axis=0, keepdims=True)
        val_ref[r:r + 1, :] = jnp.sum(jnp.where(hit & ok, 1, 0), axis=0, keepdims=True)


def _topk(score_t, nbs):
    rows, cols = score_t.shape
    tc = min(cols, 512)
    return pl.pallas_call(
        functools.partial(_topk_kernel, nbs=nbs),
        grid=(cols // tc,),
        in_specs=[pl.BlockSpec((rows, tc), lambda i: (0, i))],
        out_specs=[pl.BlockSpec((rows, tc), lambda i: (0, i)),
                   pl.BlockSpec((NSA_TOPN, tc), lambda i: (0, i)),
                   pl.BlockSpec((NSA_TOPN, tc), lambda i: (0, i))],
        out_shape=[jax.ShapeDtypeStruct((rows, cols), F32),
                   jax.ShapeDtypeStruct((NSA_TOPN, cols), jnp.int32),
                   jax.ShapeDtypeStruct((NSA_TOPN, cols), jnp.int32)],
        compiler_params=_params("parallel"),
        name="topk",
    )(score_t)


PQ = 256
PK = 512
BT = LANES


def _softmax_rows(s, ok):
    mx = jnp.max(s, axis=-1, keepdims=True)
    p = jnp.where(ok, jnp.exp(s - mx), 0.0)
    return p, jnp.sum(p, axis=-1, keepdims=True)


def _nsa_prompt_kernel(q_ref, ks_ref, vs_ref, kw_ref, vw_ref, sel_ref, bt_ref, ocmp_ref, z_ref, gate_ref,
                       o_ref, s2_ref, p2_ref, sw_ref, pw_ref, m_ref, acc_ref, *, n_bt, wsub):
    qi = pl.program_id(2)
    rows = NSA_GROUP * PQ
    qsub = PQ // BT
    ksub = PK // BT
    q4 = jnp.concatenate([q_ref[:, r * NSA_HD:(r + 1) * NSA_HD] for r in range(NSA_GROUP)], axis=0)
    q4 = (q4 * (SCALE * LOG2E)).astype(BF16)
    t = qi * PQ + lax.broadcasted_iota(jnp.int32, (PQ, 1), 0)
    nsel = sel_ref.shape[1]
    blk0 = pl.program_id(1) * (nsel // NSA_KV_HEADS)
    unselected = jnp.where(sel_ref[...] > 0.5, 0.0, NEG).astype(BF16)
    qx = jnp.concatenate([q4, jnp.concatenate([unselected] * NSA_GROUP, axis=0)], axis=1)

    def row_groups():
        for r in range(NSA_GROUP):
            for a in range(qsub):
                yield slice(r * PQ + a * BT, r * PQ + (a + 1) * BT), r, a

    def softmax_block(s_ref, p_ref, rs, r, tile_of, extra, width, m_old):
        zmax = None
        for c in range(width):
            cs = slice(c * BT, (c + 1) * BT)
            z = s_ref[rs, cs] + bt_ref[r, jnp.clip(tile_of(c), 0, N_BIAS_TILES - 1)]
            if extra is not None:
                z = z + extra[:, cs]
            s_ref[rs, cs] = z
            zmax = z if zmax is None else jnp.maximum(zmax, z)
        mx = jnp.broadcast_to(jnp.max(zmax, axis=-1, keepdims=True), (BT, BT))
        if m_old is not None:
            mx = jnp.maximum(m_old, mx)
        for c in range(width):
            cs = slice(c * BT, (c + 1) * BT)
            p_ref[rs, cs] = jnp.exp2(s_ref[rs, cs] - mx).astype(BF16)
        return mx

    def with_ones(v):
        return jnp.concatenate([v.astype(BF16), jnp.ones(v.shape, BF16)], axis=1)

    m_ref[...] = jnp.full((rows, BT), NEG, F32)
    acc_ref[...] = jnp.zeros((rows, 2 * NSA_HD), F32)

    def key_tile(kt, diagonal, slot):
        s_ref, p_ref = s2_ref.at[slot], p2_ref.at[slot]
        k0 = pl.multiple_of(kt * PK, PK)
        member = (lax.broadcasted_iota(jnp.int32, (PK, nsel), 1)
                  == blk0 + kt * (PK // NSA_BLOCK) + lax.broadcasted_iota(jnp.int32, (PK, nsel), 0) // NSA_BLOCK)
        kx = jnp.concatenate([ks_ref[pl.ds(k0, PK), :].astype(BF16), jnp.where(member, 1.0, 0.0).astype(BF16)],
                             axis=1)
        s_ref[...] = lax.dot_general(qx, kx, (((1,), (1,)), ((), ())), preferred_element_type=F32)
        future = None
        if diagonal:
            future = jnp.where(k0 + lax.broadcasted_iota(jnp.int32, (1, PK), 1) <= t, 0.0, NEG)
        for rs, r, a in row_groups():
            m_old = m_ref[rs]
            mx = softmax_block(s_ref, p_ref, rs, r, lambda c: qi * qsub + a - (kt * ksub + c),
                               None if future is None else future[a * BT:(a + 1) * BT], ksub, m_old)
            alpha = jnp.exp2(m_old - mx)
            acc_ref[rs] = jnp.concatenate([alpha, alpha], axis=1) * acc_ref[rs]
            m_ref[rs] = mx
        acc_ref[...] += jnp.dot(p_ref[...], with_ones(vs_ref[pl.ds(k0, PK), :]), preferred_element_type=F32)

    last = (qi * PQ) // PK

    def body(j, carry):
        key_tile(2 * j, False, 0)
        key_tile(2 * j + 1, False, 1)
        return carry

    lax.fori_loop(0, last // 2, body, 0)

    @pl.when(last % 2 == 1)
    def _():
        key_tile(last - 1, False, 0)

    key_tile(last, True, 1)
    o_slc = acc_ref[:, :NSA_HD] / jnp.maximum(acc_ref[:, NSA_HD:], 1e-30)

    wk = wsub * BT
    kst = jnp.clip(qi * qsub - WINDOW // BT, 0, n_bt - wsub)
    k0 = pl.multiple_of(kst * BT, BT)
    sw_ref[...] = lax.dot_general(q4, kw_ref[pl.ds(k0, wk), :].astype(BF16), (((1,), (1,)), ((), ())),
                                  preferred_element_type=F32)
    dist = t - (k0 + lax.broadcasted_iota(jnp.int32, (1, wk), 1))
    outside = jnp.where((dist >= 0) & (dist <= WINDOW), 0.0, NEG)
    for rs, r, a in row_groups():
        softmax_block(sw_ref, pw_ref, rs, r, lambda c: qi * qsub + a - (kst + c),
                      outside[a * BT:(a + 1) * BT], wsub, None)
    pv = jnp.dot(pw_ref[...], with_ones(vw_ref[pl.ds(k0, wk), :]), preferred_element_type=F32)
    o_win = pv[:, :NSA_HD] / jnp.maximum(pv[:, NSA_HD:], 1e-30)

    gates = jax.nn.sigmoid(gate_ref[...])
    for r in range(NSA_GROUP):
        sl = slice(r * NSA_HD, (r + 1) * NSA_HD)
        rs = slice(r * PQ, (r + 1) * PQ)
        o = (gates[:, 3 * r:3 * r + 1] * ocmp_ref[:, sl] + gates[:, 3 * r + 1:3 * r + 2] * o_slc[rs]
             + gates[:, 3 * r + 2:3 * r + 3] * o_win[rs])
        o_ref[:, sl] = o * _silu(z_ref[:, sl])


def _nsa_prompt(proj, sel, bias_tiles, o_cmp, nb_batch, seq):
    nq = seq // PQ
    n_bt = seq // BT
    wsub = min((WINDOW + PQ) // BT, n_bt)
    gw = NSA_GROUP * NSA_HD
    rows = NSA_GROUP * PQ
    nsel = sel.shape[-1]

    def kv_spec(col):
        return pl.BlockSpec((seq, NSA_HD), lambda b, g, i: (b, col // NSA_HD + g))

    return pl.pallas_call(
        functools.partial(_nsa_prompt_kernel, n_bt=n_bt, wsub=wsub),
        grid=(nb_batch, NSA_KV_HEADS, nq),
        in_specs=[pl.BlockSpec((PQ, gw), lambda b, g, i: (b * nq + i, C_NQ // gw + g)),
                  kv_spec(C_SLC), kv_spec(C_SLC + NSA_KV_W), kv_spec(C_WIN), kv_spec(C_WIN + NSA_KV_W),
                  pl.BlockSpec((PQ, nsel), lambda b, g, i: (b * nq + i, 0)),
                  pl.BlockSpec((NSA_GROUP, N_BIAS_TILES, BT, BT), lambda b, g, i: (g, 0, 0, 0)),
                  pl.BlockSpec((PQ, gw), lambda b, g, i: (b * nq + i, g)),
                  pl.BlockSpec((PQ, gw), lambda b, g, i: (b * nq + i, C_NZ // gw + g)),
                  pl.BlockSpec((PQ, LANES), lambda b, g, i: (b * nq + i, C_G0 // LANES + g))],
        out_specs=pl.BlockSpec((PQ, gw), lambda b, g, i: (b * nq + i, g)),
        out_shape=jax.ShapeDtypeStruct((nb_batch * seq, NSA_WIDTH), F32),
        scratch_shapes=[pltpu.VMEM((2, rows, PK), F32), pltpu.VMEM((2, rows, PK), BF16),
                        pltpu.VMEM((rows, wsub * BT), F32), pltpu.VMEM((rows, wsub * BT), BF16),
                        pltpu.VMEM((rows, BT), F32), pltpu.VMEM((rows, 2 * NSA_HD), F32)],
        compiler_params=_params("parallel", "parallel", "arbitrary"),
        name="nsa_prompt",
    )(proj, proj, proj, proj, proj, sel, bias_tiles, o_cmp, proj, proj)


def _bias_by_threshold(dist, tcols):
    bias = jnp.broadcast_to(tcols[:, 0:1], (tcols.shape[0], dist.shape[-1]))
    for b in range(1, NUM_BUCKETS):
        bias = jnp.where(dist >= _BUCKET_THR[b - 1], tcols[:, b:b + 1], bias)
    return bias


SLOT_CHUNKS = NSA_TOPN + 2


def _slc_sample_kernel(loc_ref, idx_ref, val_ref, *refs, past, lq):
    del loc_ref
    n_in = NSA_KV_HEADS * NSA_TOPN
    page_refs = refs[:n_in]
    q_ref, knew_ref, tcol_ref, o_ref, kc_ref, vc_ref = refs[n_in:]
    b = pl.program_id(0)
    qi = pl.program_id(1)
    t = past + qi
    n_past_blocks = past // NSA_BLOCK
    nk = SLOT_CHUNKS * NSA_BLOCK
    lane = lax.broadcasted_iota(jnp.int32, (1, nk), 1)
    slot = lane // NSA_BLOCK
    this_q = lax.broadcasted_iota(jnp.int32, (lq, 1), 0) == qi
    for g in range(NSA_KV_HEADS):
        base = ((b * lq + qi) * NSA_KV_HEADS + g) * NSA_TOPN
        blk_of_lane = jnp.where(slot == NSA_TOPN, n_past_blocks, n_past_blocks + 1)
        use_of_lane = jnp.where(slot == NSA_TOPN, 1, 0)
        for j in range(NSA_TOPN):
            n_j = idx_ref[base + j]
            use = jnp.where((val_ref[base + j] > 0) & (n_j < n_past_blocks), 1, 0)
            blk_of_lane = jnp.where(slot == j, n_j, blk_of_lane)
            use_of_lane = jnp.where(slot == j, use, use_of_lane)
            page = page_refs[g * NSA_TOPN + j]
            kc_ref[j * NSA_BLOCK:(j + 1) * NSA_BLOCK, :] = page[pl.ds(g, NSA_BLOCK, stride=4), :]
            vc_ref[j * NSA_BLOCK:(j + 1) * NSA_BLOCK, :] = page[pl.ds(NSA_KV_HEADS + g, NSA_BLOCK, stride=4), :]
        new0 = NSA_TOPN * NSA_BLOCK
        kc_ref[new0:, :] = jnp.zeros((nk - new0, NSA_HD), F32)
        vc_ref[new0:, :] = jnp.zeros((nk - new0, NSA_HD), F32)
        kc_ref[new0:new0 + lq, :] = knew_ref[:, g * NSA_HD:(g + 1) * NSA_HD]
        vc_ref[new0:new0 + lq, :] = knew_ref[:, (NSA_KV_HEADS + g) * NSA_HD:(NSA_KV_HEADS + g + 1) * NSA_HD]
        q4 = jnp.concatenate(
            [jnp.sum(jnp.where(this_q, q_ref[:, (g * NSA_GROUP + r) * NSA_HD:(g * NSA_GROUP + r + 1) * NSA_HD], 0.0),
                     axis=0, keepdims=True)
             for r in range(NSA_GROUP)] + [jnp.zeros((SUBLANES - NSA_GROUP, NSA_HD), F32)], axis=0)
        dist = t - (blk_of_lane * NSA_BLOCK + lane % NSA_BLOCK)
        ok = (use_of_lane > 0) & (dist >= 0)
        s = _dot_nt(q4, kc_ref[...]) * SCALE + _bias_by_threshold(dist, tcol_ref[g])
        s = jnp.where(ok, s, NEG)
        p, den = _softmax_rows(s, ok)
        o = _dot(p, vc_ref[...]) / jnp.maximum(den, 1e-30)
        for r in range(NSA_GROUP):
            h = g * NSA_GROUP + r
            o_ref[:, h * NSA_HD:(h + 1) * NSA_HD] = o[r:r + 1, :]


def _slc_sample(proj, cache4, page_table, idx_flat, val_flat, tcols, layer, nb_batch, lq, past):
    rows = NSA_BLOCK * 4
    n_past_blocks = past // NSA_BLOCK

    n_sel = NSA_KV_HEADS * NSA_TOPN
    blocks = jnp.minimum(idx_flat.reshape(nb_batch, lq * n_sel), n_past_blocks - 1)
    pages = jnp.take_along_axis(page_table, blocks // 2, axis=1)
    loc = jnp.concatenate([pages.reshape(nb_batch * lq, n_sel), (blocks % 2).reshape(nb_batch * lq, n_sel)], axis=1)

    def page_spec(g, j):
        def index_map(b, i, loc, idx, val):
            row = b * lq + i
            return (layer, loc[row, g * NSA_TOPN + j], loc[row, n_sel + g * NSA_TOPN + j], 0)
        return pl.BlockSpec((None, None, rows, NSA_HD), index_map)

    grid_spec = pltpu.PrefetchScalarGridSpec(
        num_scalar_prefetch=3,
        grid=(nb_batch, lq),
        in_specs=[page_spec(g, j) for g in range(NSA_KV_HEADS) for j in range(NSA_TOPN)]
        + [pl.BlockSpec((lq, NSA_WIDTH), lambda b, i, *_: (b, C_NQ // NSA_WIDTH)),
           pl.BlockSpec((lq, 2 * NSA_KV_W), lambda b, i, *_: (b, C_SLC // (2 * NSA_KV_W))),
           pl.BlockSpec((NSA_KV_HEADS, SUBLANES, NUM_BUCKETS), lambda b, i, *_: (0, 0, 0))],
        out_specs=pl.BlockSpec((None, 1, NSA_WIDTH), lambda b, i, *_: (b * lq + i, 0, 0)),
        scratch_shapes=[pltpu.VMEM((SLOT_CHUNKS * NSA_BLOCK, NSA_HD), F32),
                        pltpu.VMEM((SLOT_CHUNKS * NSA_BLOCK, NSA_HD), F32)],
    )
    out = pl.pallas_call(
        functools.partial(_slc_sample_kernel, past=past, lq=lq),
        grid_spec=grid_spec,
        out_shape=jax.ShapeDtypeStruct((nb_batch * lq, 1, NSA_WIDTH), F32),
        compiler_params=_params("parallel", "arbitrary"),
        name="slc_sample",
    )(loc, idx_flat, val_flat, *([cache4] * n_sel), proj, proj, tcols)
    return out.reshape(nb_batch * lq, NSA_WIDTH)


def _win_sample_kernel(q_ref, wpast_ref, wnew_ref, tcol_ref, ocmp_ref, oslc_ref, z_ref, g0_ref, g1_ref,
                       o_ref, kc_ref, vc_ref, *, past, lq, wlen):
    rows = NSA_GROUP * lq
    nk = kc_ref.shape[0]
    lane = lax.broadcasted_iota(jnp.int32, (1, nk), 1)
    kpos = past - wlen + lane
    t = past + lax.broadcasted_iota(jnp.int32, (rows, 1), 0) % lq
    dist = t - kpos
    ok = (dist >= 0) & (dist <= WINDOW) & (kpos >= 0) & (lane < wlen + lq)
    gate_refs = (g0_ref, g1_ref)
    for g in range(NSA_KV_HEADS):
        kc_ref[0:wlen, :] = wpast_ref[pl.ds(g, wlen, stride=4), :]
        vc_ref[0:wlen, :] = wpast_ref[pl.ds(NSA_KV_HEADS + g, wlen, stride=4), :]
        kc_ref[wlen:, :] = jnp.zeros((nk - wlen, NSA_HD), F32)
        vc_ref[wlen:, :] = jnp.zeros((nk - wlen, NSA_HD), F32)
        kc_ref[wlen:wlen + lq, :] = wnew_ref[:, g * NSA_HD:(g + 1) * NSA_HD]
        vc_ref[wlen:wlen + lq, :] = wnew_ref[:, (NSA_KV_HEADS + g) * NSA_HD:(NSA_KV_HEADS + g + 1) * NSA_HD]
        q4 = jnp.concatenate([q_ref[:, (g * NSA_GROUP + r) * NSA_HD:(g * NSA_GROUP + r + 1) * NSA_HD]
                              for r in range(NSA_GROUP)], axis=0)
        s = _dot_nt(q4, kc_ref[...]) * SCALE + _bias_by_threshold(dist, tcol_ref[g])
        s = jnp.where(ok, s, NEG)
        p, den = _softmax_rows(s, ok)
        o_win = _dot(p, vc_ref[...]) / jnp.maximum(den, 1e-30)
        gates = jax.nn.sigmoid(gate_refs[g][...])
        for r in range(NSA_GROUP):
            h = g * NSA_GROUP + r
            sl = slice(h * NSA_HD, (h + 1) * NSA_HD)
            o = (gates[:, 3 * r:3 * r + 1] * ocmp_ref[:, sl] + gates[:, 3 * r + 1:3 * r + 2] * oslc_ref[:, sl]
                 + gates[:, 3 * r + 2:3 * r + 3] * o_win[r * lq:(r + 1) * lq])
            o_ref[:, sl] = o * _silu(z_ref[:, sl])


def _win_sample(proj, win4, tcols_rows, o_cmp, o_slc, layer, nb_batch, lq, past):
    wlen = win4.shape[2] // 4
    nk = -(-(wlen + lq) // LANES) * LANES
    row_spec = pl.BlockSpec((lq, NSA_WIDTH), lambda b: (b, 0))
    return pl.pallas_call(
        functools.partial(_win_sample_kernel, past=past, lq=lq, wlen=wlen),
        grid=(nb_batch,),
        in_specs=[pl.BlockSpec((lq, NSA_WIDTH), lambda b: (b, C_NQ // NSA_WIDTH)),
                  pl.BlockSpec((None, None, wlen * 4, NSA_HD), lambda b: (layer, b, 0, 0)),
                  pl.BlockSpec((lq, 2 * NSA_KV_W), lambda b: (b, C_WIN // (2 * NSA_KV_W))),
                  pl.BlockSpec((NSA_KV_HEADS, NSA_GROUP * lq, NUM_BUCKETS), lambda b: (0, 0, 0)),
                  row_spec, row_spec,
                  pl.BlockSpec((lq, NSA_WIDTH), lambda b: (b, C_NZ // NSA_WIDTH)),
                  pl.BlockSpec((lq, LANES), lambda b: (b, C_G0 // LANES)),
                  pl.BlockSpec((lq, LANES), lambda b: (b, C_G1 // LANES))],
        out_specs=row_spec,
        out_shape=jax.ShapeDtypeStruct((nb_batch * lq, NSA_WIDTH), F32),
        scratch_shapes=[pltpu.VMEM((nk, NSA_HD), F32), pltpu.VMEM((nk, NSA_HD), F32)],
        compiler_params=_params("parallel"),
        name="win_sample",
    )(proj, win4, proj, tcols_rows, o_cmp, o_slc, proj, proj, proj)


def _out_kernel(dn_ref, nsa_ref, x_ref, p_ref, wo_ref, gpost_ref, wpp_ref, gple_ref, wpg_ref, o_ref):
    half = dn_ref.shape[1]
    mixed = (jnp.dot(dn_ref[...].astype(BF16), wo_ref[0:half, :], preferred_element_type=F32)
             + jnp.dot(nsa_ref[...].astype(BF16), wo_ref[half:, :], preferred_element_type=F32))
    y = mixed * lax.rsqrt(jnp.mean(mixed * mixed, axis=-1, keepdims=True) + EPS) * gpost_ref[...]
    x1 = x_ref[...] + y
    e = jnp.dot(p_ref[...].astype(BF16), wpp_ref[...], preferred_element_type=F32)
    e = e * lax.rsqrt(jnp.mean(e * e, axis=-1, keepdims=True) + EPS) * gple_ref[...]
    gate = jax.nn.sigmoid(jnp.dot(x1.astype(BF16), wpg_ref[...], preferred_element_type=F32))
    o_ref[...] = x1 + gate * e


def _out_proj(dn, nsa, x, p_all, wo_all, gpost_all, wpp_all, gple_all, wpg_all, layer, tm):
    m = x.shape[0]
    once = pl.Buffered(1)

    def wspec(k):
        return pl.BlockSpec((None, k, D_MODEL), lambda i: (layer, 0, 0), pipeline_mode=once)

    return pl.pallas_call(
        _out_kernel,
        grid=(m // tm,),
        in_specs=[pl.BlockSpec((tm, dn.shape[1]), lambda i: (i, 0)),
                  pl.BlockSpec((tm, nsa.shape[1]), lambda i: (i, 0)),
                  pl.BlockSpec((tm, D_MODEL), lambda i: (i, 0)),
                  pl.BlockSpec((None, tm, PLE_DIM), lambda i: (layer, i, 0)),
                  wspec(D_MODEL), wspec(1), wspec(PLE_DIM), wspec(1), wspec(D_MODEL)],
        out_specs=pl.BlockSpec((tm, D_MODEL), lambda i: (i, 0)),
        out_shape=jax.ShapeDtypeStruct((m, D_MODEL), F32),
        compiler_params=_params("parallel"),
        name="out_proj",
    )(dn, nsa, x, p_all, wo_all, gpost_all, wpp_all, gple_all, wpg_all)


def _kv_rows_kernel(*refs, depth):
    x_refs = refs[:depth]
    cmp_ref, slc_ref, win_ref = refs[depth:]
    tm = x_refs[0].shape[0]
    width = 2 * NSA_KV_W

    def rows(o_ref, x_ref, seg):
        for s in range(4):
            o_ref[pl.ds(s, tm, stride=4), :] = x_ref[:, seg * width + s * NSA_HD:seg * width + (s + 1) * NSA_HD]

    for layer in range(depth):
        @pl.when(pl.program_id(0) == layer)
        def _(x_ref=x_refs[layer]):
            rows(cmp_ref, x_ref, 0)
            rows(slc_ref, x_ref, 1)

            @pl.when(pl.program_id(2) == pl.num_programs(2) - 1)
            def _():
                rows(win_ref, x_ref, 2)


def _kv_rows(projs, nb, seq):
    depth = len(projs)
    tm = min(WINDOW, seq)
    nr = seq // tm
    width = 3 * 2 * NSA_KV_W

    def x_spec(layer):
        return pl.BlockSpec((tm, width), lambda d, b, r: (jnp.where(d == layer, b * nr + r, 0), C_CMP // width))

    return pl.pallas_call(
        functools.partial(_kv_rows_kernel, depth=depth),
        grid=(depth, nb, nr),
        in_specs=[x_spec(layer) for layer in range(depth)],
        out_specs=[pl.BlockSpec((None, tm * 4, NSA_HD), lambda d, b, r: (d, b * nr + r, 0)),
                   pl.BlockSpec((None, tm * 4, NSA_HD), lambda d, b, r: (d, b * nr + r, 0)),
                   pl.BlockSpec((None, None, tm * 4, NSA_HD), lambda d, b, r: (d, b, 0, 0))],
        out_shape=[jax.ShapeDtypeStruct((depth, nb * seq * 4, NSA_HD), F32),
                   jax.ShapeDtypeStruct((depth, nb * seq * 4, NSA_HD), F32),
                   jax.ShapeDtypeStruct((depth, nb, tm * 4, NSA_HD), F32)],
        compiler_params=_params("arbitrary", "arbitrary", "arbitrary"),
        name="kv_rows",
    )(*projs)


W_TILE = 512


def _relayout_w_in_kernel(wt_ref, sm_ref, gate_ref, o_ref, tail_ref):
    is_tail = pl.program_id(1) == pl.num_programs(1) - 1

    @pl.when(jnp.logical_not(is_tail))
    def _():
        o_ref[...] = wt_ref[0].T.astype(BF16)

    @pl.when(is_tail)
    def _():
        half = 3 * NSA_GROUP
        tail_ref[...] = jnp.zeros(tail_ref.shape, F32)
        tail_ref[0:2 * DN_HEADS, :] = sm_ref[0]
        tail_ref[C_G0 - C_SM:C_G0 - C_SM + half, :] = gate_ref[0, 0:half, :]
        tail_ref[C_G1 - C_SM:C_G1 - C_SM + half, :] = gate_ref[0, half:2 * half, :]
        o_ref[...] = tail_ref[...].T.astype(BF16)


def _relayout_w_in(w_in):
    depth, k, _ = w_in.shape
    wt = jnp.swapaxes(w_in, 1, 2)
    n_gate = 2 * 3 * NSA_GROUP
    n_tiles = N_PAD // W_TILE
    moved = C_NQ // W_TILE

    def src_row(d, j):
        j = jnp.minimum(j, n_tiles - 2)
        return (d, pl.multiple_of(j * W_TILE + jnp.where(j >= moved, _O_NQ - C_NQ, 0), SUBLANES), 0)

    return pl.pallas_call(
        _relayout_w_in_kernel,
        grid=(depth, n_tiles),
        in_specs=[pl.BlockSpec((pl.Element(1), pl.Element(W_TILE), pl.Element(k)), src_row),
                  pl.BlockSpec((pl.Element(1), pl.Element(2 * DN_HEADS), pl.Element(k)), lambda d, j: (d, _O_BETA, 0)),
                  pl.BlockSpec((pl.Element(1), pl.Element(n_gate), pl.Element(k)), lambda d, j: (d, _O_GATE, 0))],
        out_specs=pl.BlockSpec((None, k, W_TILE), lambda d, j: (d, 0, j)),
        out_shape=jax.ShapeDtypeStruct((depth, k, N_PAD), BF16),
        scratch_shapes=[pltpu.VMEM((W_TILE, k), F32)],
        compiler_params=_params("parallel", "arbitrary"),
        name="relayout_w_in",
    )(wt, wt, wt)


def _bias_expand_kernel(t_ref, bucket_ref, o_ref):
    bucket = bucket_ref[...]
    for h in range(NSA_HEADS):
        acc = jnp.full(bucket.shape, t_ref[0, h], F32)
        for b in range(1, NUM_BUCKETS):
            acc = jnp.where(bucket >= b, t_ref[b, h], acc)
        o_ref[h] = acc


def _bias_expand(table, bucket):
    rows, cols = bucket.shape
    tr = max(t for t in range(SUBLANES, min(rows, 512) + 1, SUBLANES) if rows % t == 0)
    return pl.pallas_call(
        _bias_expand_kernel,
        grid=(rows // tr,),
        in_specs=[pl.BlockSpec(memory_space=pltpu.SMEM), pl.BlockSpec((tr, cols), lambda i: (i, 0))],
        out_specs=pl.BlockSpec((NSA_HEADS, tr, cols), lambda i: (0, i, 0)),
        out_shape=jax.ShapeDtypeStruct((NSA_HEADS, rows, cols), F32),
        compiler_params=_params("parallel"),
        name="bias_expand",
    )(table, jnp.asarray(bucket))


def _bias_tiles(table):
    i = np.arange(BT)
    d = np.arange(N_BIAS_TILES)[:, None, None] * BT + i[None, :, None] - i[None, None, :]
    tiles = _bias_expand(table * LOG2E, _bucket_np(d).reshape(N_BIAS_TILES * BT, BT))
    return tiles.reshape(NSA_HEADS, N_BIAS_TILES, BT, BT)


def _bias_cmp(table, pos0, lq, nb):
    dist = (pos0 + np.arange(lq))[:, None] - (np.arange(nb) * NSA_BLOCK + NSA_BLOCK - 1)[None, :]
    return _bias_expand(table, _bucket_np(dist))


def kernel(x_prompt, x_sample, state_dn_S, state_dn_conv, cache_win_kv, cache_cmp_kv, cache_slc_kv, page_table,
           p_prompt, p_sample, rel_bias_table, w_in, w_out, g_pre, g_post, dn_conv_w, dn_A_log, dn_dt_bias,
           dn_norm_w, cmp_pos_w, cmp_w1, cmp_w2, w_ple_proj, g_ple, w_ple_gate):
    depth = w_in.shape[0]
    bp, seq, _ = x_prompt.shape
    bs, lq, _ = x_sample.shape
    n_pages = page_table.shape[1]
    past = n_pages * PAGE_SIZE
    n_pool = cache_cmp_kv.shape[1]
    wlen = cache_win_kv.shape[2]
    mp, ms = bp * seq, bs * lq
    assert seq % PK == 0 and seq % DN_CHUNK == 0 and lq == SUBLANES and past % NSA_BLOCK == 0
    kv_row = (2, NSA_KV_HEADS, NSA_HD)

    w_in_r = _relayout_w_in(w_in)
    w_out_b = w_out.astype(BF16)
    wpp_b = w_ple_proj.astype(BF16)
    wpg_b = w_ple_gate.astype(BF16)
    g_pre3 = g_pre[:, None, :]
    g_post3 = g_post[:, None, :]
    g_ple3 = g_ple[:, None, :]
    nw3 = dn_norm_w[:, None, :]
    lane_pad = ((0, 0), (DN_HEADS, LANES - 2 * DN_HEADS))
    dn_par = jnp.stack([jnp.pad(dn_A_log, lane_pad), jnp.pad(dn_dt_bias, lane_pad)], axis=1)
    pw_prompt = jnp.broadcast_to(cmp_pos_w[:, :, None, :, None], (depth, 2, NSA_KV_HEADS, NSA_BLOCK, NSA_HD))
    pw_prompt = jnp.transpose(pw_prompt, (0, 3, 1, 2, 4)).reshape(depth, NSA_BLOCK, 2 * NSA_KV_W)
    pw_page = jnp.broadcast_to(cmp_pos_w[:, None, None, :, :, None],
                               (depth, PAGE_SIZE // NSA_BLOCK, NSA_KV_HEADS, 2, NSA_BLOCK, NSA_HD))
    pw_page = jnp.transpose(pw_page, (0, 1, 4, 3, 2, 5)).reshape(depth, PAGE_SIZE * 4, NSA_HD)
    table = rel_bias_table.astype(F32)
    bias_tiles = _bias_tiles(table)
    nb_p = seq // NSA_BLOCK
    nb_s = past // NSA_BLOCK
    nbs_s = -(-(past + lq) // NSA_BLOCK)
    nbp_s = -(-nbs_s // LANES) * LANES
    bias_cmp_p = _bias_cmp(table, 0, seq, nb_p)
    bias_cmp_s = _bias_cmp(table, past, lq, nb_s)
    tcols = jnp.pad(table.T.reshape(NSA_KV_HEADS, NSA_GROUP, NUM_BUCKETS),
                    ((0, 0), (0, SUBLANES - NSA_GROUP), (0, 0)))
    tcols_rows = jnp.repeat(table.T.reshape(NSA_KV_HEADS, NSA_GROUP, NUM_BUCKETS), lq, axis=1)

    conv0_s = jnp.pad(state_dn_conv, ((0, 0), (0, 0), (SUBLANES - (CONV_W - 1), 0), (0, 0)))
    conv0_p = jnp.zeros((1, bp, SUBLANES, DN_CONV_CH), F32)
    s0_p = jnp.zeros((1, bp, DN_HEADS, DN_DK, DN_DV), F32)
    cmp4 = cache_cmp_kv.reshape(depth, n_pool, PAGE_SIZE * 4, NSA_HD)
    slc4 = cache_slc_kv.reshape(depth, n_pool, PAGE_SIZE * 4, NSA_HD)
    win4 = cache_win_kv.reshape(depth, bs, wlen * 4, NSA_HD)
    p_prompt3 = p_prompt.reshape(depth, mp, PLE_DIM)
    p_sample3 = p_sample.reshape(depth, ms, PLE_DIM)

    xp = x_prompt.reshape(mp, D_MODEL)
    xs = x_sample.reshape(ms, D_MODEL)
    outs_p = [[] for _ in range(2)]
    outs_s = [[] for _ in range(5)]
    projs_p = []
    tq_p = min(256, seq)
    tm_p = min(IN_TM, mp)
    pages = min(POOL_PAGES, n_pages)
    fuse_pool = (mp // tm_p, N_PAD // IN_TN) == (bs, n_pages // pages)
    for i in range(depth):
        if fuse_pool:
            proj, pooled = _in_proj_pool(xp, g_pre3, w_in_r, cmp4, page_table, pw_page, i, tm_p, IN_TN, pages)
        else:
            proj = _in_proj(xp, g_pre3, w_in_r, i, tm_p)
            pooled = _pool_paged(cmp4, page_table, pw_page, i, pages)
        dn_o, conv_n, s_n = _deltanet(proj, bp, seq, DN_CHUNK, conv0_p, s0_p, 0, dn_conv_w, dn_par, nw3, i)
        o_cmp, sel = _cmp_attn(proj, _pool_prompt(proj, bp, seq, pw_prompt, i), cmp_w1, cmp_w2, i, bias_cmp_p,
                               bp, seq, 0, tq_p, nb_p, nb_p, True)
        nsa_o = _nsa_prompt(proj, sel, bias_tiles, o_cmp, bp, seq)
        xp = _out_proj(dn_o, nsa_o, xp, p_prompt3, w_out_b, g_post3, wpp_b, g_ple3, wpg_b, i, min(256, mp))
        projs_p.append(proj)
        outs_p[0].append(s_n)
        outs_p[1].append(conv_n[:, SUBLANES - (CONV_W - 1):])

        proj = _in_proj(xs, g_pre3, w_in_r, i, ms)
        dn_o, conv_n, s_n = _deltanet(proj, bs, lq, lq, conv0_s, state_dn_S, i, dn_conv_w, dn_par, nw3, i)
        o_cmp, score = _cmp_attn(proj, pooled, cmp_w1, cmp_w2, i, bias_cmp_s, bs, lq, past, lq, nbs_s, nbp_s, False)
        score_t = jnp.transpose(score.reshape(ms * NSA_KV_HEADS, nbp_s))
        _, idx_t, val_t = _topk(score_t, nbs_s)
        o_slc = _slc_sample(proj, slc4, page_table, idx_t.T.reshape(-1), val_t.T.reshape(-1), tcols, i,
                            bs, lq, past)
        nsa_o = _win_sample(proj, win4, tcols_rows, o_cmp, o_slc, i, bs, lq, past)
        xs = _out_proj(dn_o, nsa_o, xs, p_sample3, w_out_b, g_post3, wpp_b, g_ple3, wpg_b, i, ms)
        kv_all = proj[:, C_CMP:C_WIN + 2 * NSA_KV_W].reshape(bs, lq, 3, *kv_row)
        win_all = jnp.concatenate([cache_win_kv[i], kv_all[:, :, 2]], axis=1)
        outs_s[0].append(s_n)
        outs_s[1].append(conv_n[:, SUBLANES - (CONV_W - 1):])
        outs_s[2].append(win_all[:, win_all.shape[1] - min(WINDOW, past + lq):])
        outs_s[3].append(kv_all[:, :, 0])
        outs_s[4].append(kv_all[:, :, 1])

    p_cmp, p_slc, p_win = _kv_rows(projs_p, bp, seq)
    return ((xp.reshape(bp, seq, D_MODEL), xs.reshape(bs, lq, D_MODEL), jnp.stack(outs_p[0]), jnp.stack(outs_p[1]),
             p_win.reshape(depth, bp, min(WINDOW, seq), *kv_row), p_cmp.reshape(depth, bp, seq, *kv_row),
             p_slc.reshape(depth, bp, seq, *kv_row)) + tuple(jnp.stack(o) for o in outs_s))
```

```python
import functools
import math

import numpy as np
import jax
import jax.numpy as jnp
from jax import lax
from jax.experimental import pallas as pl
from jax.experimental.pallas import tpu as pltpu

F32 = jnp.float32
BF16 = jnp.bfloat16

D_MODEL = 2048
DN_HEADS = 8
DN_DK = 128
DN_DV = 128
DN_QK_W = DN_HEADS * DN_DK
DN_CONV_CH = 2 * DN_QK_W + DN_HEADS * DN_DV
CONV_W = 4
DN_CHUNK = 64
NSA_HEADS = 8
NSA_KV_HEADS = 2
NSA_GROUP = 4
NSA_HD = 128
NSA_WIDTH = NSA_HEADS * NSA_HD
NSA_KV_W = NSA_KV_HEADS * NSA_HD
NSA_BLOCK = 64
NSA_TOPN = 16
WINDOW = 512
SCALE = NSA_HD ** -0.5
NUM_BUCKETS = 32
REL_MAX_DIST = 1024
PLE_DIM = 256
PAGE_SIZE = 128
EPS = 1e-6
NEG = -1e30
LOG2E = math.log2(math.e)

VMEM_LIMIT_BYTES = 56 * 1024 * 1024
LANES = 128
SUBLANES = 8

C_QKV = 0
C_Z = 3072
C_NQ = 4096
C_NZ = 5120
C_CMP = 6144
C_SLC = 6656
C_WIN = 7168
C_SM = 7680
C_G0 = 7808
C_G1 = 7936
N_PAD = 8192
_O_BETA = DN_CONV_CH + DN_HEADS * DN_DV
_O_NQ = _O_BETA + 2 * DN_HEADS
_O_GATE = _O_NQ + 2 * NSA_WIDTH + 3 * 2 * NSA_KV_W
N_BIAS_TILES = 9


def _bucket_np(dist):
    dist = np.maximum(np.asarray(dist, np.int64), 0)
    exact = NUM_BUCKETS // 2
    scaled = np.log(np.maximum(dist, 1).astype(np.float64) / exact) / math.log(REL_MAX_DIST / exact)
    large = np.minimum(exact + (scaled * (NUM_BUCKETS - exact)).astype(np.int64), NUM_BUCKETS - 1)
    return np.where(dist < exact, dist, large).astype(np.int32)


_BUCKET_THR = tuple(int(np.argmax(_bucket_np(np.arange(2 * REL_MAX_DIST)) >= b)) for b in range(1, NUM_BUCKETS))


def _dot(a, b):
    return jnp.dot(a.astype(BF16), b.astype(BF16), preferred_element_type=F32)


def _dot_nt(a, b):
    return lax.dot_general(a.astype(BF16), b.astype(BF16), (((1,), (1,)), ((), ())), preferred_element_type=F32)


def _dot_hi(a, b):
    return jnp.dot(a, b, preferred_element_type=F32, precision=lax.Precision.HIGHEST)


def _dot_hi_nt(a, b):
    return lax.dot_general(a, b, (((1,), (1,)), ((), ())), preferred_element_type=F32,
                           precision=lax.Precision.HIGHEST)


def _silu(x):
    return x * jax.nn.sigmoid(x)


def _params(*sem):
    return pltpu.CompilerParams(dimension_semantics=sem, vmem_limit_bytes=VMEM_LIMIT_BYTES)


def _in_proj_kernel(x_ref, g_ref, w_ref, o_ref, xn_ref):
    @pl.when(pl.program_id(1) == 0)
    def _():
        x = x_ref[...]
        y = x * lax.rsqrt(jnp.mean(x * x, axis=-1, keepdims=True) + EPS)
        xn_ref[...] = (y * g_ref[...]).astype(BF16)

    o_ref[...] = jnp.dot(xn_ref[...], w_ref[...], preferred_element_type=F32)


IN_TM = 1024
IN_TN = 1024
POOL_PAGES = 16


def _in_proj(x, g_all, w_all, layer, tm):
    m = x.shape[0]
    tn = IN_TN
    return pl.pallas_call(
        _in_proj_kernel,
        grid=(m // tm, N_PAD // tn),
        in_specs=[pl.BlockSpec((tm, D_MODEL), lambda i, j: (i, 0)),
                  pl.BlockSpec((None, 1, D_MODEL), lambda i, j: (layer, 0, 0)),
                  pl.BlockSpec((None, D_MODEL, tn), lambda i, j: (layer, 0, j))],
        out_specs=pl.BlockSpec((tm, tn), lambda i, j: (i, j)),
        out_shape=jax.ShapeDtypeStruct((m, N_PAD), F32),
        scratch_shapes=[pltpu.VMEM((tm, D_MODEL), BF16)],
        compiler_params=_params("parallel", "arbitrary"),
        name="in_proj",
    )(x, g_all, w_all)


def _dn_prep_kernel(qkv_ref, sm_ref, conv0_ref, cw_ref, par_ref,
                    u_ref, w_ref, qd_ref, kd_ref, qk_ref, gl_ref, convn_ref, xs_ref, *, chunk, cs):
    tb = chunk * cs
    halo = SUBLANES

    @pl.when(pl.program_id(1) == 0)
    def _():
        xs_ref[0:halo, :] = conv0_ref[...]

    xs_ref[halo:halo + tb, :] = qkv_ref[...]
    base = halo - (CONV_W - 1)
    y = xs_ref[base:base + tb, :] * cw_ref[0:1, :]
    for j in range(1, CONV_W):
        y = y + xs_ref[base + j:base + j + tb, :] * cw_ref[j:j + 1, :]
    y = _silu(y)
    tail = xs_ref[tb:tb + halo, :]
    xs_ref[0:halo, :] = tail
    convn_ref[...] = tail

    sm = sm_ref[...]
    beta_all = jax.nn.sigmoid(sm)
    xg = sm + par_ref[1:2, :]
    softplus = jnp.maximum(xg, 0.0) + jnp.log1p(jnp.exp(-jnp.abs(xg)))
    g_all = -jnp.exp(par_ref[0:1, :]) * softplus
    row = lax.broadcasted_iota(jnp.int32, (tb, tb), 0)
    col = lax.broadcasted_iota(jnp.int32, (tb, tb), 1)
    in_chunk_lower = (row >= col) & (row // chunk == col // chunk)
    gc_all = _dot_hi(jnp.where(in_chunk_lower, 1.0, 0.0), g_all)
    pick = (lax.broadcasted_iota(jnp.int32, (DN_HEADS, LANES), 1)
            == lax.broadcasted_iota(jnp.int32, (DN_HEADS, LANES), 0) + DN_HEADS).astype(F32)
    gc_rows = _dot_hi_nt(pick, gc_all)
    for c in range(cs):
        gl_ref[c] = gc_all[(c + 1) * chunk - 1:(c + 1) * chunk, :]
    r64 = lax.broadcasted_iota(jnp.int32, (chunk, chunk), 0)
    c64 = lax.broadcasted_iota(jnp.int32, (chunk, chunk), 1)
    lower = r64 >= c64
    strict = r64 > c64

    lhs, kns, rhs, decays = [], [], [], []
    for h in range(DN_HEADS):
        hc = slice(h * DN_DV, (h + 1) * DN_DV)
        qh = y[:, h * DN_DK:(h + 1) * DN_DK]
        kh = y[:, DN_QK_W + h * DN_DK:DN_QK_W + (h + 1) * DN_DK]
        vh = y[:, 2 * DN_QK_W + h * DN_DV:2 * DN_QK_W + (h + 1) * DN_DV]
        qn = qh * lax.rsqrt(jnp.sum(qh * qh, axis=-1, keepdims=True) + EPS) * (DN_DK ** -0.5)
        kn = kh * lax.rsqrt(jnp.sum(kh * kh, axis=-1, keepdims=True) + EPS)
        beta = beta_all[:, h:h + 1]
        gc = gc_all[:, DN_HEADS + h:DN_HEADS + h + 1]
        egc = jnp.exp(gc)
        kb = kn * beta
        rhs_h = jnp.concatenate([vh * beta, kb * egc], axis=-1)
        qd_ref[:, hc] = (qn * egc).astype(qd_ref.dtype)
        for c in range(cs):
            rs = slice(c * chunk, (c + 1) * chunk)
            gcc = gc[rs]
            decays.append(jnp.exp(jnp.where(lower, gcc - gc_rows[h:h + 1, rs], NEG)))
            lhs.append(jnp.concatenate([kb[rs], qn[rs]], axis=0))
            kns.append(kn[rs])
            rhs.append(rhs_h[rs])
            kd_ref[rs, hc] = (kn[rs] * jnp.exp(gcc[chunk - 1:chunk, :] - gcc)).astype(kd_ref.dtype)
    decay = jnp.stack(decays)
    both = jnp.einsum('nid,njd->nij', jnp.stack(lhs).astype(BF16), jnp.stack(kns).astype(BF16),
                      preferred_element_type=F32)
    a = jnp.where(strict[None], both[:, :chunk] * decay, 0.0)
    qk = both[:, chunk:] * decay

    def bmm(p, q):
        return jnp.einsum('nij,njk->nik', p, q, preferred_element_type=F32)

    def split(v):
        hi = v.astype(BF16)
        return hi, (v - hi.astype(F32)).astype(BF16)

    def bmm3(p, q):
        (ph, pl_), (qh, ql) = p, q
        return bmm(ph, qh) + (bmm(ph, ql) + bmm(pl_, qh))

    pw_s = split(a)
    inv = jnp.where(r64 == c64, 1.0, 0.0)[None] - a
    for _ in range(int(math.log2(chunk)) - 1):
        pw_s = split(bmm3(pw_s, pw_s))
        inv = inv + bmm3(pw_s, split(inv))
    x = bmm3(split(inv), split(jnp.stack(rhs)))
    for h in range(DN_HEADS):
        hc = slice(h * DN_DV, (h + 1) * DN_DV)
        for c in range(cs):
            rs = slice(c * chunk, (c + 1) * chunk)
            n = h * cs + c
            u_ref[rs, hc] = x[n, :, :DN_DV]
            w_ref[rs, hc] = x[n, :, DN_DV:].astype(w_ref.dtype)
            qk_ref[rs, h * chunk:(h + 1) * chunk] = qk[n]


def _dn_scan_kernel(u_ref, w_ref, qd_ref, kd_ref, qk_ref, gl_ref, z_ref, s0_ref, nw_ref,
                    o_ref, sout_ref, s_ref, *, chunk, cs):
    @pl.when(pl.program_id(1) == 0)
    def _():
        s_ref[...] = s0_ref[...]

    def heads(ref, rs, width):
        return jnp.stack([ref[rs, h * width:(h + 1) * width] for h in range(DN_HEADS)])

    def bdot(spec, p, q):
        return jnp.einsum(spec, p.astype(BF16), q.astype(BF16), preferred_element_type=F32)

    for c in range(cs):
        rs = slice(c * chunk, (c + 1) * chunk)
        decay_last = jnp.exp(gl_ref[c])
        decay_last = jnp.stack([decay_last[:, DN_HEADS + h:DN_HEADS + h + 1] for h in range(DN_HEADS)])
        s = s_ref[...]
        sb = s.astype(BF16)
        v_new = heads(u_ref, rs, DN_DV) - bdot('hcd,hde->hce', heads(w_ref, rs, DN_DK), sb)
        o = (bdot('hcd,hde->hce', heads(qd_ref, rs, DN_DK), sb)
             + bdot('hcs,hse->hce', heads(qk_ref, rs, chunk), v_new))
        s_ref[...] = s * decay_last + bdot('hcd,hce->hde', heads(kd_ref, rs, DN_DK), v_new)
        on = o * lax.rsqrt(jnp.mean(o * o, axis=-1, keepdims=True) + EPS) * nw_ref[...]
        for h in range(DN_HEADS):
            hc = slice(h * DN_DV, (h + 1) * DN_DV)
            o_ref[rs, hc] = on[h] * _silu(z_ref[rs, hc])

    @pl.when(pl.program_id(1) == pl.num_programs(1) - 1)
    def _():
        sout_ref[...] = s_ref[...]


def _deltanet(proj, nb, seq, chunk, conv0_all, s0_all, state_layer, cw_all, par_all, nw_all, layer):
    m = nb * seq
    nchunks = seq // chunk
    width = DN_HEADS * DN_DV
    hs = (DN_HEADS, DN_DK, DN_DV)
    cs_a = 4 if nchunks % 4 == 0 else 1
    na = nchunks // cs_a
    ta = cs_a * chunk
    op_dtype = BF16 if chunk % (2 * SUBLANES) == 0 else F32
    u, w, qd, kd, qk, gl, conv_n = pl.pallas_call(
        functools.partial(_dn_prep_kernel, chunk=chunk, cs=cs_a),
        grid=(nb, na),
        in_specs=[pl.BlockSpec((ta, DN_CONV_CH), lambda b, c: (b * na + c, C_QKV // DN_CONV_CH)),
                  pl.BlockSpec((ta, LANES), lambda b, c: (b * na + c, C_SM // LANES)),
                  pl.BlockSpec((None, None, SUBLANES, DN_CONV_CH), lambda b, c: (state_layer, b, 0, 0)),
                  pl.BlockSpec((None, CONV_W, DN_CONV_CH), lambda b, c: (layer, 0, 0)),
                  pl.BlockSpec((None, 2, LANES), lambda b, c: (layer, 0, 0))],
        out_specs=[pl.BlockSpec((ta, width), lambda b, c: (b * na + c, 0)),
                   pl.BlockSpec((ta, width), lambda b, c: (b * na + c, 0)),
                   pl.BlockSpec((ta, width), lambda b, c: (b * na + c, 0)),
                   pl.BlockSpec((ta, width), lambda b, c: (b * na + c, 0)),
                   pl.BlockSpec((ta, DN_HEADS * chunk), lambda b, c: (b * na + c, 0)),
                   pl.BlockSpec((cs_a, 1, LANES), lambda b, c: (b * na + c, 0, 0)),
                   pl.BlockSpec((None, SUBLANES, DN_CONV_CH), lambda b, c: (b, 0, 0))],
        out_shape=[jax.ShapeDtypeStruct((m, width), F32),
                   jax.ShapeDtypeStruct((m, width), op_dtype),
                   jax.ShapeDtypeStruct((m, width), op_dtype),
                   jax.ShapeDtypeStruct((m, width), op_dtype),
                   jax.ShapeDtypeStruct((m, DN_HEADS * chunk), F32),
                   jax.ShapeDtypeStruct((nb * nchunks, 1, LANES), F32),
                   jax.ShapeDtypeStruct((nb, SUBLANES, DN_CONV_CH), F32)],
        scratch_shapes=[pltpu.VMEM((ta + SUBLANES, DN_CONV_CH), F32)],
        compiler_params=_params("parallel", "arbitrary"),
        name="dn_prep",
    )(proj, proj, conv0_all, cw_all, par_all)

    cs_b = 8 if nchunks % 8 == 0 else 1
    nbk = nchunks // cs_b
    tbk = cs_b * chunk
    row_spec = pl.BlockSpec((tbk, width), lambda b, c: (b * nbk + c, 0))
    o, s_n = pl.pallas_call(
        functools.partial(_dn_scan_kernel, chunk=chunk, cs=cs_b),
        grid=(nb, nbk),
        in_specs=[row_spec, row_spec, row_spec, row_spec,
                  pl.BlockSpec((tbk, DN_HEADS * chunk), lambda b, c: (b * nbk + c, 0)),
                  pl.BlockSpec((cs_b, 1, LANES), lambda b, c: (b * nbk + c, 0, 0)),
                  pl.BlockSpec((tbk, width), lambda b, c: (b * nbk + c, C_Z // width)),
                  pl.BlockSpec((None, None) + hs, lambda b, c: (state_layer, b, 0, 0, 0)),
                  pl.BlockSpec((None, 1, DN_DV), lambda b, c: (layer, 0, 0))],
        out_specs=[row_spec, pl.BlockSpec((None,) + hs, lambda b, c: (b, 0, 0, 0))],
        out_shape=[jax.ShapeDtypeStruct((m, width), F32), jax.ShapeDtypeStruct((nb,) + hs, F32)],
        scratch_shapes=[pltpu.VMEM(hs, F32)],
        compiler_params=_params("parallel", "arbitrary"),
        name="dn_scan",
    )(u, w, qd, kd, qk, gl, proj, s0_all, nw_all)
    return o, conv_n, s_n


def _pool_kernel(x_ref, pw_ref, o_ref):
    rows = x_ref.shape[0]
    x = x_ref[...].reshape(rows // NSA_BLOCK, NSA_BLOCK, 2 * NSA_KV_W)
    o_ref[...] = jnp.sum(x * pw_ref[...][None], axis=1)


def _pool_prompt(proj, nb, seq, pw_all, layer):
    rows = min(seq, 1024)
    nr = seq // rows
    width = 2 * NSA_KV_W
    return pl.pallas_call(
        _pool_kernel,
        grid=(nb, nr),
        in_specs=[pl.BlockSpec((rows, width), lambda b, r: (b * nr + r, C_CMP // width)),
                  pl.BlockSpec((None, NSA_BLOCK, width), lambda b, r: (layer, 0, 0))],
        out_specs=pl.BlockSpec((None, rows // NSA_BLOCK, width), lambda b, r: (b, r, 0)),
        out_shape=jax.ShapeDtypeStruct((nb, seq // NSA_BLOCK, width), F32),
        compiler_params=_params("parallel", "parallel"),
        name="pool_prompt",
    )(proj, pw_all)


def _pool_pages(page_refs, pw_ref, o_ref):
    rows = PAGE_SIZE * 4
    per_block = rows // 2 // SUBLANES
    for k, page_ref in enumerate(page_refs):
        prod = page_ref[...] * pw_ref[...]
        y = jnp.sum(prod.reshape(2, per_block, SUBLANES, NSA_HD), axis=1)
        y = y[:, 0:4, :] + y[:, 4:8, :]
        for s in range(4):
            o_ref[2 * k:2 * k + 2, s * NSA_HD:(s + 1) * NSA_HD] = y[:, s, :]


def _paged_pool_kernel(pt_ref, *refs, pages):
    del pt_ref
    _pool_pages(refs[:pages], refs[pages], refs[pages + 1])


def _in_proj_pool_kernel(pt_ref, x_ref, g_ref, w_ref, *refs, pages):
    del pt_ref
    o_ref, pooled_ref, xn_ref = refs[pages + 1:]
    _in_proj_kernel(x_ref, g_ref, w_ref, o_ref, xn_ref)
    _pool_pages(refs[:pages], refs[pages], pooled_ref)


def _in_proj_pool(x, g_all, w_all, cache4, page_table, pw_all, layer, tm, tn, pages):
    m = x.shape[0]
    nb, n_pages = page_table.shape
    assert (m // tm, N_PAD // tn) == (nb, n_pages // pages)
    rows = PAGE_SIZE * 4

    def page_spec(k):
        return pl.BlockSpec((None, None, rows, NSA_HD), lambda i, j, pt: (layer, pt[i, j * pages + k], 0, 0))

    grid_spec = pltpu.PrefetchScalarGridSpec(
        num_scalar_prefetch=1,
        grid=(m // tm, N_PAD // tn),
        in_specs=[pl.BlockSpec((tm, D_MODEL), lambda i, j, pt: (i, 0)),
                  pl.BlockSpec((None, 1, D_MODEL), lambda i, j, pt: (layer, 0, 0)),
                  pl.BlockSpec((None, D_MODEL, tn), lambda i, j, pt: (layer, 0, j))]
        + [page_spec(k) for k in range(pages)]
        + [pl.BlockSpec((None, rows, NSA_HD), lambda i, j, pt: (layer, 0, 0))],
        out_specs=[pl.BlockSpec((tm, tn), lambda i, j, pt: (i, j)),
                   pl.BlockSpec((None, 2 * pages, 4 * NSA_HD), lambda i, j, pt: (i, j, 0))],
        scratch_shapes=[pltpu.VMEM((tm, D_MODEL), BF16)],
    )
    return pl.pallas_call(
        functools.partial(_in_proj_pool_kernel, pages=pages),
        grid_spec=grid_spec,
        out_shape=[jax.ShapeDtypeStruct((m, N_PAD), F32),
                   jax.ShapeDtypeStruct((nb, 2 * n_pages, 4 * NSA_HD), F32)],
        compiler_params=_params("parallel", "arbitrary"),
        name="in_proj_pool",
    )(page_table, x, g_all, w_all, *([cache4] * pages), pw_all)


def _pool_paged(cache4, page_table, pw_all, layer, pages):
    nb, n_pages = page_table.shape
    steps = n_pages // pages
    rows = PAGE_SIZE * 4

    def page_spec(k):
        return pl.BlockSpec((None, None, rows, NSA_HD),
                            lambda b, s, pt: (layer, pt[b, s * pages + k], 0, 0))

    grid_spec = pltpu.PrefetchScalarGridSpec(
        num_scalar_prefetch=1,
        grid=(nb, steps),
        in_specs=[page_spec(k) for k in range(pages)]
        + [pl.BlockSpec((None, rows, NSA_HD), lambda b, s, pt: (layer, 0, 0))],
        out_specs=pl.BlockSpec((None, 2 * pages, 4 * NSA_HD), lambda b, s, pt: (b, s, 0)),
    )
    return pl.pallas_call(
        functools.partial(_paged_pool_kernel, pages=pages),
        grid_spec=grid_spec,
        out_shape=jax.ShapeDtypeStruct((nb, 2 * n_pages, 4 * NSA_HD), F32),
        compiler_params=_params("parallel", "arbitrary"),
        name="pool_paged",
    )(page_table, *([cache4] * pages), pw_all)


def _select_blocks(score, st_ref, nbp, n_causal):
    st_ref[...] = score.T
    halves = [st_ref[g * nbp:(g + 1) * nbp, :] for g in range(NSA_KV_HEADS)]
    bi = lax.broadcasted_iota(jnp.int32, halves[0].shape, 0)

    def body(j, ranks):
        tie = jnp.where(j < bi, 1, 0)
        out = []
        for g in range(NSA_KV_HEADS):
            other = st_ref[pl.ds(g * nbp + j, 1), :]
            out.append(ranks[g] + jnp.where(other > halves[g], 1, jnp.where(other == halves[g], tie, 0)))
        return tuple(out)

    ranks = lax.fori_loop(0, n_causal, body, tuple(jnp.zeros(bi.shape, jnp.int32) for _ in halves))
    keep = [jnp.where((ranks[g] < NSA_TOPN) & (halves[g] >= 0.0), 1.0, 0.0) for g in range(NSA_KV_HEADS)]
    return jnp.concatenate(keep, axis=0).T


def _cmp_attn_kernel(q_ref, pooled_ref, w1_ref, w2_ref, bias_ref, o_ref, sc_ref, ckv_ref, *scratch,
                     pos0, tq, nb, nbs, nbp, select):
    qi = pl.program_id(1)

    @pl.when(qi == 0)
    def _():
        for s in range(2 * NSA_KV_HEADS):
            kv = s // NSA_KV_HEADS
            h = _silu(_dot_hi(pooled_ref[:, s * NSA_HD:(s + 1) * NSA_HD], w1_ref[kv]))
            ckv_ref[:, s * NSA_HD:(s + 1) * NSA_HD] = _dot_hi(h, w2_ref[kv])

    t = pos0 + qi * tq + lax.broadcasted_iota(jnp.int32, (tq, 1), 0)
    blk = lax.broadcasted_iota(jnp.int32, (1, nb), 1)
    valid = t >= blk * NSA_BLOCK + (NSA_BLOCK - 1)
    bi = lax.broadcasted_iota(jnp.int32, (1, nbp), 1)
    cur = t // NSA_BLOCK
    forced = (bi == 0) | (bi == cur) | (bi == cur - 1)
    causal = bi <= cur
    scores = []
    for g in range(NSA_KV_HEADS):
        ck = ckv_ref[:, g * NSA_HD:(g + 1) * NSA_HD]
        cv = ckv_ref[:, (NSA_KV_HEADS + g) * NSA_HD:(NSA_KV_HEADS + g + 1) * NSA_HD]
        imp = jnp.zeros((tq, nb), F32)
        for r in range(NSA_GROUP):
            h = g * NSA_GROUP + r
            logits = _dot_nt(q_ref[:, h * NSA_HD:(h + 1) * NSA_HD], ck) * SCALE + bias_ref[h]
            logits = jnp.where(valid, logits, NEG)
            mx = jnp.max(logits, axis=-1, keepdims=True)
            p = jnp.where(valid, jnp.exp(logits - mx), 0.0)
            p = p / jnp.maximum(jnp.sum(p, axis=-1, keepdims=True), 1e-30)
            o_ref[:, h * NSA_HD:(h + 1) * NSA_HD] = _dot(p, cv)
            imp = imp + p
        if nbp > nb:
            imp = jnp.concatenate([imp, jnp.zeros((tq, nbp - nb), F32)], axis=-1)
        score = jnp.where(causal, jnp.where(forced, NSA_GROUP + 1.0, imp), -1.0)
        scores.append(jnp.where(bi < nbs, score, -2.0))
    score = jnp.concatenate(scores, axis=-1)
    if select:
        n_causal = jnp.minimum((pos0 + (qi + 1) * tq - 1) // NSA_BLOCK + 1, nbs)
        sc_ref[...] = _select_blocks(score, scratch[0], nbp, n_causal)
    else:
        sc_ref[...] = score


def _cmp_attn(proj, pooled, w1_all, w2_all, layer, bias_cmp, nb_batch, seq, pos0, tq, nbs, nbp, select):
    nq = seq // tq
    nb = pooled.shape[1]
    m = nb_batch * seq
    bias_tq = bias_cmp.shape[1] // nq
    wspec = pl.BlockSpec((None, 2, NSA_HD, NSA_HD), lambda b, i: (layer, 0, 0, 0))
    return pl.pallas_call(
        functools.partial(_cmp_attn_kernel, pos0=pos0, tq=tq, nb=nb, nbs=nbs, nbp=nbp, select=select),
        scratch_shapes=[pltpu.VMEM((nb, 2 * NSA_KV_W), F32)]
        + ([pltpu.VMEM((NSA_KV_HEADS * nbp, tq), F32)] if select else []),
        grid=(nb_batch, nq),
        in_specs=[pl.BlockSpec((tq, NSA_WIDTH), lambda b, i: (b * nq + i, C_NQ // NSA_WIDTH)),
                  pl.BlockSpec((None, nb, 2 * NSA_KV_W), lambda b, i: (b, 0, 0)), wspec, wspec,
                  pl.BlockSpec((NSA_HEADS, bias_tq, nb), lambda b, i: (0, i, 0))],
        out_specs=[pl.BlockSpec((tq, NSA_WIDTH), lambda b, i: (b * nq + i, 0)),
                   pl.BlockSpec((tq, NSA_KV_HEADS * nbp), lambda b, i: (b * nq + i, 0))],
        out_shape=[jax.ShapeDtypeStruct((m, NSA_WIDTH), F32),
                   jax.ShapeDtypeStruct((m, NSA_KV_HEADS * nbp), F32)],
        compiler_params=_params("parallel", "arbitrary"),
        name="cmp_attn",
    )(proj, pooled, w1_all, w2_all, bias_cmp)


def _topk_kernel(s_ref, sel_ref, idx_ref, val_ref, *, nbs):
    s = s_ref[...]
    bi = lax.broadcasted_iota(jnp.int32, s.shape, 0)

    def body(j, rank):
        other = s_ref[pl.ds(j, 1), :]
        beats = (other > s) | ((other == s) & (j < bi))
        return rank + beats.astype(jnp.int32)

    rank = lax.fori_loop(0, nbs, body, jnp.zeros(s.shape, jnp.int32))
    ok = s >= 0.0
    sel_ref[...] = jnp.where((rank < NSA_TOPN) & ok, 1.0, 0.0)
    for r in range(NSA_TOPN):
        hit = rank == r
        idx_ref[r:r + 1, :] = jnp.sum(jnp.where(hit, bi, 0), axis=0, keepdims=True)
        val_ref[r:r + 1, :] = jnp.sum(jnp.where(hit & ok, 1, 0), axis=0, keepdims=True)


def _topk(score_t, nbs):
    rows, cols = score_t.shape
    tc = min(cols, 512)
    return pl.pallas_call(
        functools.partial(_topk_kernel, nbs=nbs),
        grid=(cols // tc,),
        in_specs=[pl.BlockSpec((rows, tc), lambda i: (0, i))],
        out_specs=[pl.BlockSpec((rows, tc), lambda i: (0, i)),
                   pl.BlockSpec((NSA_TOPN, tc), lambda i: (0, i)),
                   pl.BlockSpec((NSA_TOPN, tc), lambda i: (0, i))],
        out_shape=[jax.ShapeDtypeStruct((rows, cols), F32),
                   jax.ShapeDtypeStruct((NSA_TOPN, cols), jnp.int32),
                   jax.ShapeDtypeStruct((NSA_TOPN, cols), jnp.int32)],
        compiler_params=_params("parallel"),
        name="topk",
    )(score_t)


PQ = 256
PK = 512
BT = LANES


def _softmax_rows(s, ok):
    mx = jnp.max(s, axis=-1, keepdims=True)
    p = jnp.where(ok, jnp.exp(s - mx), 0.0)
    return p, jnp.sum(p, axis=-1, keepdims=True)


def _nsa_prompt_kernel(q_ref, ks_ref, vs_ref, kw_ref, vw_ref, sel_ref, bt_ref, ocmp_ref, z_ref, gate_ref,
                       o_ref, s2_ref, p2_ref, sw_ref, pw_ref, m_ref, acc_ref, *, n_bt, wsub):
    qi = pl.program_id(2)
    rows = NSA_GROUP * PQ
    qsub = PQ // BT
    ksub = PK // BT
    q4 = jnp.concatenate([q_ref[:, r * NSA_HD:(r + 1) * NSA_HD] for r in range(NSA_GROUP)], axis=0)
    q4 = (q4 * (SCALE * LOG2E)).astype(BF16)
    t = qi * PQ + lax.broadcasted_iota(jnp.int32, (PQ, 1), 0)
    nsel = sel_ref.shape[1]
    blk0 = pl.program_id(1) * (nsel // NSA_KV_HEADS)
    unselected = jnp.where(sel_ref[...] > 0.5, 0.0, NEG).astype(BF16)
    qx = jnp.concatenate([q4, jnp.concatenate([unselected] * NSA_GROUP, axis=0)], axis=1)

    def row_groups():
        for r in range(NSA_GROUP):
            for a in range(qsub):
                yield slice(r * PQ + a * BT, r * PQ + (a + 1) * BT), r, a

    def softmax_block(s_ref, p_ref, rs, r, tile_of, extra, width, m_old):
        zmax = None
        for c in range(width):
            cs = slice(c * BT, (c + 1) * BT)
            z = s_ref[rs, cs] + bt_ref[r, jnp.clip(tile_of(c), 0, N_BIAS_TILES - 1)]
            if extra is not None:
                z = z + extra[:, cs]
            s_ref[rs, cs] = z
            zmax = z if zmax is None else jnp.maximum(zmax, z)
        mx = jnp.broadcast_to(jnp.max(zmax, axis=-1, keepdims=True), (BT, BT))
        if m_old is not None:
            mx = jnp.maximum(m_old, mx)
        for c in range(width):
            cs = slice(c * BT, (c + 1) * BT)
            p_ref[rs, cs] = jnp.exp2(s_ref[rs, cs] - mx).astype(BF16)
        return mx

    def with_ones(v):
        return jnp.concatenate([v.astype(BF16), jnp.ones(v.shape, BF16)], axis=1)

    m_ref[...] = jnp.full((rows, BT), NEG, F32)
    acc_ref[...] = jnp.zeros((rows, 2 * NSA_HD), F32)

    def key_tile(kt, diagonal, slot):
        s_ref, p_ref = s2_ref.at[slot], p2_ref.at[slot]
        k0 = pl.multiple_of(kt * PK, PK)
        member = (lax.broadcasted_iota(jnp.int32, (PK, nsel), 1)
                  == blk0 + kt * (PK // NSA_BLOCK) + lax.broadcasted_iota(jnp.int32, (PK, nsel), 0) // NSA_BLOCK)
        kx = jnp.concatenate([ks_ref[pl.ds(k0, PK), :].astype(BF16), jnp.where(member, 1.0, 0.0).astype(BF16)],
                             axis=1)
        s_ref[...] = lax.dot_general(qx, kx, (((1,), (1,)), ((), ())), preferred_element_type=F32)
        future = None
        if diagonal:
            future = jnp.where(k0 + lax.broadcasted_iota(jnp.int32, (1, PK), 1) <= t, 0.0, NEG)
        for rs, r, a in row_groups():
            m_old = m_ref[rs]
            mx = softmax_block(s_ref, p_ref, rs, r, lambda c: qi * qsub + a - (kt * ksub + c),
                               None if future is None else future[a * BT:(a + 1) * BT], ksub, m_old)
            alpha = jnp.exp2(m_old - mx)
            acc_ref[rs] = jnp.concatenate([alpha, alpha], axis=1) * acc_ref[rs]
            m_ref[rs] = mx
        acc_ref[...] += jnp.dot(p_ref[...], with_ones(vs_ref[pl.ds(k0, PK), :]), preferred_element_type=F32)

    last = (qi * PQ) // PK

    def body(j, carry):
        key_tile(2 * j, False, 0)
        key_tile(2 * j + 1, False, 1)
        return carry

    lax.fori_loop(0, last // 2, body, 0)

    @pl.when(last % 2 == 1)
    def _():
        key_tile(last - 1, False, 0)

    key_tile(last, True, 1)
    o_slc = acc_ref[:, :NSA_HD] / jnp.maximum(acc_ref[:, NSA_HD:], 1e-30)

    wk = wsub * BT
    kst = jnp.clip(qi * qsub - WINDOW // BT, 0, n_bt - wsub)
    k0 = pl.multiple_of(kst * BT, BT)
    sw_ref[...] = lax.dot_general(q4, kw_ref[pl.ds(k0, wk), :].astype(BF16), (((1,), (1,)), ((), ())),
                                  preferred_element_type=F32)
    dist = t - (k0 + lax.broadcasted_iota(jnp.int32, (1, wk), 1))
    outside = jnp.where((dist >= 0) & (dist <= WINDOW), 0.0, NEG)
    for rs, r, a in row_groups():
        softmax_block(sw_ref, pw_ref, rs, r, lambda c: qi * qsub + a - (kst + c),
                      outside[a * BT:(a + 1) * BT], wsub, None)
    pv = jnp.dot(pw_ref[...], with_ones(vw_ref[pl.ds(k0, wk), :]), preferred_element_type=F32)
    o_win = pv[:, :NSA_HD] / jnp.maximum(pv[:, NSA_HD:], 1e-30)

    gates = jax.nn.sigmoid(gate_ref[...])
    for r in range(NSA_GROUP):
        sl = slice(r * NSA_HD, (r + 1) * NSA_HD)
        rs = slice(r * PQ, (r + 1) * PQ)
        o = (gates[:, 3 * r:3 * r + 1] * ocmp_ref[:, sl] + gates[:, 3 * r + 1:3 * r + 2] * o_slc[rs]
             + gates[:, 3 * r + 2:3 * r + 3] * o_win[rs])
        o_ref[:, sl] = o * _silu(z_ref[:, sl])


def _nsa_prompt(proj, sel, bias_tiles, o_cmp, nb_batch, seq):
    nq = seq // PQ
    n_bt = seq // BT
    wsub = min((WINDOW + PQ) // BT, n_bt)
    gw = NSA_GROUP * NSA_HD
    rows = NSA_GROUP * PQ
    nsel = sel.shape[-1]

    def kv_spec(col):
        return pl.BlockSpec((seq, NSA_HD), lambda b, g, i: (b, col // NSA_HD + g))

    return pl.pallas_call(
        functools.partial(_nsa_prompt_kernel, n_bt=n_bt, wsub=wsub),
        grid=(nb_batch, NSA_KV_HEADS, nq),
        in_specs=[pl.BlockSpec((PQ, gw), lambda b, g, i: (b * nq + i, C_NQ // gw + g)),
                  kv_spec(C_SLC), kv_spec(C_SLC + NSA_KV_W), kv_spec(C_WIN), kv_spec(C_WIN + NSA_KV_W),
                  pl.BlockSpec((PQ, nsel), lambda b, g, i: (b * nq + i, 0)),
                  pl.BlockSpec((NSA_GROUP, N_BIAS_TILES, BT, BT), lambda b, g, i: (g, 0, 0, 0)),
                  pl.BlockSpec((PQ, gw), lambda b, g, i: (b * nq + i, g)),
                  pl.BlockSpec((PQ, gw), lambda b, g, i: (b * nq + i, C_NZ // gw + g)),
                  pl.BlockSpec((PQ, LANES), lambda b, g, i: (b * nq + i, C_G0 // LANES + g))],
        out_specs=pl.BlockSpec((PQ, gw), lambda b, g, i: (b * nq + i, g)),
        out_shape=jax.ShapeDtypeStruct((nb_batch * seq, NSA_WIDTH), F32),
        scratch_shapes=[pltpu.VMEM((2, rows, PK), F32), pltpu.VMEM((2, rows, PK), BF16),
                        pltpu.VMEM((rows, wsub * BT), F32), pltpu.VMEM((rows, wsub * BT), BF16),
                        pltpu.VMEM((rows, BT), F32), pltpu.VMEM((rows, 2 * NSA_HD), F32)],
        compiler_params=_params("parallel", "parallel", "arbitrary"),
        name="nsa_prompt",
    )(proj, proj, proj, proj, proj, sel, bias_tiles, o_cmp, proj, proj)


def _bias_by_threshold(dist, tcols):
    bias = jnp.broadcast_to(tcols[:, 0:1], (tcols.shape[0], dist.shape[-1]))
    for b in range(1, NUM_BUCKETS):
        bias = jnp.where(dist >= _BUCKET_THR[b - 1], tcols[:, b:b + 1], bias)
    return bias


SLOT_CHUNKS = NSA_TOPN + 2


def _slc_sample_kernel(loc_ref, idx_ref, val_ref, *refs, past, lq):
    del loc_ref
    n_in = NSA_KV_HEADS * NSA_TOPN
    page_refs = refs[:n_in]
    q_ref, knew_ref, tcol_ref, o_ref, kc_ref, vc_ref = refs[n_in:]
    b = pl.program_id(0)
    qi = pl.program_id(1)
    t = past + qi
    n_past_blocks = past // NSA_BLOCK
    nk = SLOT_CHUNKS * NSA_BLOCK
    lane = lax.broadcasted_iota(jnp.int32, (1, nk), 1)
    slot = lane // NSA_BLOCK
    this_q = lax.broadcasted_iota(jnp.int32, (lq, 1), 0) == qi
    for g in range(NSA_KV_HEADS):
        base = ((b * lq + qi) * NSA_KV_HEADS + g) * NSA_TOPN
        blk_of_lane = jnp.where(slot == NSA_TOPN, n_past_blocks, n_past_blocks + 1)
        use_of_lane = jnp.where(slot == NSA_TOPN, 1, 0)
        for j in range(NSA_TOPN):
            n_j = idx_ref[base + j]
            use = jnp.where((val_ref[base + j] > 0) & (n_j < n_past_blocks), 1, 0)
            blk_of_lane = jnp.where(slot == j, n_j, blk_of_lane)
            use_of_lane = jnp.where(slot == j, use, use_of_lane)
            page = page_refs[g * NSA_TOPN + j]
            kc_ref[j * NSA_BLOCK:(j + 1) * NSA_BLOCK, :] = page[pl.ds(g, NSA_BLOCK, stride=4), :]
            vc_ref[j * NSA_BLOCK:(j + 1) * NSA_BLOCK, :] = page[pl.ds(NSA_KV_HEADS + g, NSA_BLOCK, stride=4), :]
        new0 = NSA_TOPN * NSA_BLOCK
        kc_ref[new0:, :] = jnp.zeros((nk - new0, NSA_HD), F32)
        vc_ref[new0:, :] = jnp.zeros((nk - new0, NSA_HD), F32)
        kc_ref[new0:new0 + lq, :] = knew_ref[:, g * NSA_HD:(g + 1) * NSA_HD]
        vc_ref[new0:new0 + lq, :] = knew_ref[:, (NSA_KV_HEADS + g) * NSA_HD:(NSA_KV_HEADS + g + 1) * NSA_HD]
        q4 = jnp.concatenate(
            [jnp.sum(jnp.where(this_q, q_ref[:, (g * NSA_GROUP + r) * NSA_HD:(g * NSA_GROUP + r + 1) * NSA_HD], 0.0),
                     axis=0, keepdims=True)
             for r in range(NSA_GROUP)] + [jnp.zeros((SUBLANES - NSA_GROUP, NSA_HD), F32)], axis=0)
        dist = t - (blk_of_lane * NSA_BLOCK + lane % NSA_BLOCK)
        ok = (use_of_lane > 0) & (dist >= 0)
        s = _dot_nt(q4, kc_ref[...]) * SCALE + _bias_by_threshold(dist, tcol_ref[g])
        s = jnp.where(ok, s, NEG)
        p, den = _softmax_rows(s, ok)
        o = _dot(p, vc_ref[...]) / jnp.maximum(den, 1e-30)
        for r in range(NSA_GROUP):
            h = g * NSA_GROUP + r
            o_ref[:, h * NSA_HD:(h + 1) * NSA_HD] = o[r:r + 1, :]


def _slc_sample(proj, cache4, page_table, idx_flat, val_flat, tcols, layer, nb_batch, lq, past):
    rows = NSA_BLOCK * 4
    n_past_blocks = past // NSA_BLOCK

    n_sel = NSA_KV_HEADS * NSA_TOPN
    blocks = jnp.minimum(idx_flat.reshape(nb_batch, lq * n_sel), n_past_blocks - 1)
    pages = jnp.take_along_axis(page_table, blocks // 2, axis=1)
    loc = jnp.concatenate([pages.reshape(nb_batch * lq, n_sel), (blocks % 2).reshape(nb_batch * lq, n_sel)], axis=1)

    def page_spec(g, j):
        def index_map(b, i, loc, idx, val):
            row = b * lq + i
            return (layer, loc[row, g * NSA_TOPN + j], loc[row, n_sel + g * NSA_TOPN + j], 0)
        return pl.BlockSpec((None, None, rows, NSA_HD), index_map)

    grid_spec = pltpu.PrefetchScalarGridSpec(
        num_scalar_prefetch=3,
        grid=(nb_batch, lq),
        in_specs=[page_spec(g, j) for g in range(NSA_KV_HEADS) for j in range(NSA_TOPN)]
        + [pl.BlockSpec((lq, NSA_WIDTH), lambda b, i, *_: (b, C_NQ // NSA_WIDTH)),
           pl.BlockSpec((lq, 2 * NSA_KV_W), lambda b, i, *_: (b, C_SLC // (2 * NSA_KV_W))),
           pl.BlockSpec((NSA_KV_HEADS, SUBLANES, NUM_BUCKETS), lambda b, i, *_: (0, 0, 0))],
        out_specs=pl.BlockSpec((None, 1, NSA_WIDTH), lambda b, i, *_: (b * lq + i, 0, 0)),
        scratch_shapes=[pltpu.VMEM((SLOT_CHUNKS * NSA_BLOCK, NSA_HD), F32),
                        pltpu.VMEM((SLOT_CHUNKS * NSA_BLOCK, NSA_HD), F32)],
    )
    out = pl.pallas_call(
        functools.partial(_slc_sample_kernel, past=past, lq=lq),
        grid_spec=grid_spec,
        out_shape=jax.ShapeDtypeStruct((nb_batch * lq, 1, NSA_WIDTH), F32),
        compiler_params=_params("parallel", "arbitrary"),
        name="slc_sample",
    )(loc, idx_flat, val_flat, *([cache4] * n_sel), proj, proj, tcols)
    return out.reshape(nb_batch * lq, NSA_WIDTH)


def _win_sample_kernel(q_ref, wpast_ref, wnew_ref, tcol_ref, ocmp_ref, oslc_ref, z_ref, g0_ref, g1_ref,
                       o_ref, kc_ref, vc_ref, *, past, lq, wlen):
    rows = NSA_GROUP * lq
    nk = kc_ref.shape[0]
    lane = lax.broadcasted_iota(jnp.int32, (1, nk), 1)
    kpos = past - wlen + lane
    t = past + lax.broadcasted_iota(jnp.int32, (rows, 1), 0) % lq
    dist = t - kpos
    ok = (dist >= 0) & (dist <= WINDOW) & (kpos >= 0) & (lane < wlen + lq)
    gate_refs = (g0_ref, g1_ref)
    for g in range(NSA_KV_HEADS):
        kc_ref[0:wlen, :] = wpast_ref[pl.ds(g, wlen, stride=4), :]
        vc_ref[0:wlen, :] = wpast_ref[pl.ds(NSA_KV_HEADS + g, wlen, stride=4), :]
        kc_ref[wlen:, :] = jnp.zeros((nk - wlen, NSA_HD), F32)
        vc_ref[wlen:, :] = jnp.zeros((nk - wlen, NSA_HD), F32)
        kc_ref[wlen:wlen + lq, :] = wnew_ref[:, g * NSA_HD:(g + 1) * NSA_HD]
        vc_ref[wlen:wlen + lq, :] = wnew_ref[:, (NSA_KV_HEADS + g) * NSA_HD:(NSA_KV_HEADS + g + 1) * NSA_HD]
        q4 = jnp.concatenate([q_ref[:, (g * NSA_GROUP + r) * NSA_HD:(g * NSA_GROUP + r + 1) * NSA_HD]
                              for r in range(NSA_GROUP)], axis=0)
        s = _dot_nt(q4, kc_ref[...]) * SCALE + _bias_by_threshold(dist, tcol_ref[g])
        s = jnp.where(ok, s, NEG)
        p, den = _softmax_rows(s, ok)
        o_win = _dot(p, vc_ref[...]) / jnp.maximum(den, 1e-30)
        gates = jax.nn.sigmoid(gate_refs[g][...])
        for r in range(NSA_GROUP):
            h = g * NSA_GROUP + r
            sl = slice(h * NSA_HD, (h + 1) * NSA_HD)
            o = (gates[:, 3 * r:3 * r + 1] * ocmp_ref[:, sl] + gates[:, 3 * r + 1:3 * r + 2] * oslc_ref[:, sl]
                 + gates[:, 3 * r + 2:3 * r + 3] * o_win[r * lq:(r + 1) * lq])
            o_ref[:, sl] = o * _silu(z_ref[:, sl])


def _win_sample(proj, win4, tcols_rows, o_cmp, o_slc, layer, nb_batch, lq, past):
    wlen = win4.shape[2] // 4
    nk = -(-(wlen + lq) // LANES) * LANES
    row_spec = pl.BlockSpec((lq, NSA_WIDTH), lambda b: (b, 0))
    return pl.pallas_call(
        functools.partial(_win_sample_kernel, past=past, lq=lq, wlen=wlen),
        grid=(nb_batch,),
        in_specs=[pl.BlockSpec((lq, NSA_WIDTH), lambda b: (b, C_NQ // NSA_WIDTH)),
                  pl.BlockSpec((None, None, wlen * 4, NSA_HD), lambda b: (layer, b, 0, 0)),
                  pl.BlockSpec((lq, 2 * NSA_KV_W), lambda b: (b, C_WIN // (2 * NSA_KV_W))),
                  pl.BlockSpec((NSA_KV_HEADS, NSA_GROUP * lq, NUM_BUCKETS), lambda b: (0, 0, 0)),
                  row_spec, row_spec,
                  pl.BlockSpec((lq, NSA_WIDTH), lambda b: (b, C_NZ // NSA_WIDTH)),
                  pl.BlockSpec((lq, LANES), lambda b: (b, C_G0 // LANES)),
                  pl.BlockSpec((lq, LANES), lambda b: (b, C_G1 // LANES))],
        out_specs=row_spec,
        out_shape=jax.ShapeDtypeStruct((nb_batch * lq, NSA_WIDTH), F32),
        scratch_shapes=[pltpu.VMEM((nk, NSA_HD), F32), pltpu.VMEM((nk, NSA_HD), F32)],
        compiler_params=_params("parallel"),
        name="win_sample",
    )(proj, win4, proj, tcols_rows, o_cmp, o_slc, proj, proj, proj)


def _out_kernel(dn_ref, nsa_ref, x_ref, p_ref, wo_ref, gpost_ref, wpp_ref, gple_ref, wpg_ref, o_ref):
    half = dn_ref.shape[1]
    mixed = (jnp.dot(dn_ref[...].astype(BF16), wo_ref[0:half, :], preferred_element_type=F32)
             + jnp.dot(nsa_ref[...].astype(BF16), wo_ref[half:, :], preferred_element_type=F32))
    y = mixed * lax.rsqrt(jnp.mean(mixed * mixed, axis=-1, keepdims=True) + EPS) * gpost_ref[...]
    x1 = x_ref[...] + y
    e = jnp.dot(p_ref[...].astype(BF16), wpp_ref[...], preferred_element_type=F32)
    e = e * lax.rsqrt(jnp.mean(e * e, axis=-1, keepdims=True) + EPS) * gple_ref[...]
    gate = jax.nn.sigmoid(jnp.dot(x1.astype(BF16), wpg_ref[...], preferred_element_type=F32))
    o_ref[...] = x1 + gate * e


def _out_proj(dn, nsa, x, p_all, wo_all, gpost_all, wpp_all, gple_all, wpg_all, layer, tm):
    m = x.shape[0]
    once = pl.Buffered(1)

    def wspec(k):
        return pl.BlockSpec((None, k, D_MODEL), lambda i: (layer, 0, 0), pipeline_mode=once)

    return pl.pallas_call(
        _out_kernel,
        grid=(m // tm,),
        in_specs=[pl.BlockSpec((tm, dn.shape[1]), lambda i: (i, 0)),
                  pl.BlockSpec((tm, nsa.shape[1]), lambda i: (i, 0)),
                  pl.BlockSpec((tm, D_MODEL), lambda i: (i, 0)),
                  pl.BlockSpec((None, tm, PLE_DIM), lambda i: (layer, i, 0)),
                  wspec(D_MODEL), wspec(1), wspec(PLE_DIM), wspec(1), wspec(D_MODEL)],
        out_specs=pl.BlockSpec((tm, D_MODEL), lambda i: (i, 0)),
        out_shape=jax.ShapeDtypeStruct((m, D_MODEL), F32),
        compiler_params=_params("parallel"),
        name="out_proj",
    )(dn, nsa, x, p_all, wo_all, gpost_all, wpp_all, gple_all, wpg_all)


def _kv_rows_kernel(*refs, depth):
    x_refs = refs[:depth]
    cmp_ref, slc_ref, win_ref = refs[depth:]
    tm = x_refs[0].shape[0]
    width = 2 * NSA_KV_W

    def rows(o_ref, x_ref, seg):
        for s in range(4):
            o_ref[pl.ds(s, tm, stride=4), :] = x_ref[:, seg * width + s * NSA_HD:seg * width + (s + 1) * NSA_HD]

    for layer in range(depth):
        @pl.when(pl.program_id(0) == layer)
        def _(x_ref=x_refs[layer]):
            rows(cmp_ref, x_ref, 0)
            rows(slc_ref, x_ref, 1)

            @pl.when(pl.program_id(2) == pl.num_programs(2) - 1)
            def _():
                rows(win_ref, x_ref, 2)


def _kv_rows(projs, nb, seq):
    depth = len(projs)
    tm = min(WINDOW, seq)
    nr = seq // tm
    width = 3 * 2 * NSA_KV_W

    def x_spec(layer):
        return pl.BlockSpec((tm, width), lambda d, b, r: (jnp.where(d == layer, b * nr + r, 0), C_CMP // width))

    return pl.pallas_call(
        functools.partial(_kv_rows_kernel, depth=depth),
        grid=(depth, nb, nr),
        in_specs=[x_spec(layer) for layer in range(depth)],
        out_specs=[pl.BlockSpec((None, tm * 4, NSA_HD), lambda d, b, r: (d, b * nr + r, 0)),
                   pl.BlockSpec((None, tm * 4, NSA_HD), lambda d, b, r: (d, b * nr + r, 0)),
                   pl.BlockSpec((None, None, tm * 4, NSA_HD), lambda d, b, r: (d, b, 0, 0))],
        out_shape=[jax.ShapeDtypeStruct((depth, nb * seq * 4, NSA_HD), F32),
                   jax.ShapeDtypeStruct((depth, nb * seq * 4, NSA_HD), F32),
                   jax.ShapeDtypeStruct((depth, nb, tm * 4, NSA_HD), F32)],
        compiler_params=_params("arbitrary", "arbitrary", "arbitrary"),
        name="kv_rows",
    )(*projs)


def _win_state_kernel(*refs, depth, lq, keep):
    cache_ref = refs[0]
    x_refs = refs[1:1 + depth]
    o_ref = refs[1 + depth]
    total = cache_ref.shape[0]
    o_ref[0:keep * 4, :] = cache_ref[total - keep * 4:total, :]
    for layer in range(depth):
        @pl.when(pl.program_id(0) == layer)
        def _(x_ref=x_refs[layer]):
            for s in range(4):
                o_ref[pl.ds(keep * 4 + s, lq, stride=4), :] = x_ref[:, s * NSA_HD:(s + 1) * NSA_HD]


def _win_state(win4, projs, nb, lq):
    depth = len(projs)
    rows = win4.shape[2]
    wlen = rows // 4
    keep = wlen - lq
    width = 2 * NSA_KV_W

    def x_spec(layer):
        return pl.BlockSpec((lq, width), lambda d, b: (jnp.where(d == layer, b, 0), C_WIN // width))

    return pl.pallas_call(
        functools.partial(_win_state_kernel, depth=depth, lq=lq, keep=keep),
        grid=(depth, nb),
        in_specs=[pl.BlockSpec((None, None, rows, NSA_HD), lambda d, b: (d, b, 0, 0))]
        + [x_spec(layer) for layer in range(depth)],
        out_specs=pl.BlockSpec((None, None, rows, NSA_HD), lambda d, b: (d, b, 0, 0)),
        out_shape=jax.ShapeDtypeStruct((depth, nb, rows, NSA_HD), F32),
        compiler_params=_params("arbitrary", "arbitrary"),
        name="win_state",
    )(win4, *projs)


W_TILE = 512


def _relayout_w_in_kernel(wt_ref, sm_ref, gate_ref, o_ref, tail_ref):
    is_tail = pl.program_id(1) == pl.num_programs(1) - 1

    @pl.when(jnp.logical_not(is_tail))
    def _():
        o_ref[...] = wt_ref[0].T.astype(BF16)

    @pl.when(is_tail)
    def _():
        half = 3 * NSA_GROUP
        tail_ref[...] = jnp.zeros(tail_ref.shape, F32)
        tail_ref[0:2 * DN_HEADS, :] = sm_ref[0]
        tail_ref[C_G0 - C_SM:C_G0 - C_SM + half, :] = gate_ref[0, 0:half, :]
        tail_ref[C_G1 - C_SM:C_G1 - C_SM + half, :] = gate_ref[0, half:2 * half, :]
        o_ref[...] = tail_ref[...].T.astype(BF16)


def _relayout_w_in(w_in):
    depth, k, _ = w_in.shape
    wt = jnp.swapaxes(w_in, 1, 2)
    n_gate = 2 * 3 * NSA_GROUP
    n_tiles = N_PAD // W_TILE
    moved = C_NQ // W_TILE

    def src_row(d, j):
        j = jnp.minimum(j, n_tiles - 2)
        return (d, pl.multiple_of(j * W_TILE + jnp.where(j >= moved, _O_NQ - C_NQ, 0), SUBLANES), 0)

    return pl.pallas_call(
        _relayout_w_in_kernel,
        grid=(depth, n_tiles),
        in_specs=[pl.BlockSpec((pl.Element(1), pl.Element(W_TILE), pl.Element(k)), src_row),
                  pl.BlockSpec((pl.Element(1), pl.Element(2 * DN_HEADS), pl.Element(k)), lambda d, j: (d, _O_BETA, 0)),
                  pl.BlockSpec((pl.Element(1), pl.Element(n_gate), pl.Element(k)), lambda d, j: (d, _O_GATE, 0))],
        out_specs=pl.BlockSpec((None, k, W_TILE), lambda d, j: (d, 0, j)),
        out_shape=jax.ShapeDtypeStruct((depth, k, N_PAD), BF16),
        scratch_shapes=[pltpu.VMEM((W_TILE, k), F32)],
        compiler_params=_params("parallel", "arbitrary"),
        name="relayout_w_in",
    )(wt, wt, wt)


def _bias_expand_kernel(t_ref, bucket_ref, o_ref):
    bucket = bucket_ref[...]
    for h in range(NSA_HEADS):
        acc = jnp.full(bucket.shape, t_ref[0, h], F32)
        for b in range(1, NUM_BUCKETS):
            acc = jnp.where(bucket >= b, t_ref[b, h], acc)
        o_ref[h] = acc


def _bias_expand(table, bucket):
    rows, cols = bucket.shape
    tr = max(t for t in range(SUBLANES, min(rows, 512) + 1, SUBLANES) if rows % t == 0)
    return pl.pallas_call(
        _bias_expand_kernel,
        grid=(rows // tr,),
        in_specs=[pl.BlockSpec(memory_space=pltpu.SMEM), pl.BlockSpec((tr, cols), lambda i: (i, 0))],
        out_specs=pl.BlockSpec((NSA_HEADS, tr, cols), lambda i: (0, i, 0)),
        out_shape=jax.ShapeDtypeStruct((NSA_HEADS, rows, cols), F32),
        compiler_params=_params("parallel"),
        name="bias_expand",
    )(table, jnp.asarray(bucket))


def _bias_tiles(table):
    i = np.arange(BT)
    d = np.arange(N_BIAS_TILES)[:, None, None] * BT + i[None, :, None] - i[None, None, :]
    tiles = _bias_expand(table * LOG2E, _bucket_np(d).reshape(N_BIAS_TILES * BT, BT))
    return tiles.reshape(NSA_HEADS, N_BIAS_TILES, BT, BT)


def _bias_cmp(table, pos0, lq, nb):
    dist = (pos0 + np.arange(lq))[:, None] - (np.arange(nb) * NSA_BLOCK + NSA_BLOCK - 1)[None, :]
    return _bias_expand(table, _bucket_np(dist))


def kernel(x_prompt, x_sample, state_dn_S, state_dn_conv, cache_win_kv, cache_cmp_kv, cache_slc_kv, page_table,
           p_prompt, p_sample, rel_bias_table, w_in, w_out, g_pre, g_post, dn_conv_w, dn_A_log, dn_dt_bias,
           dn_norm_w, cmp_pos_w, cmp_w1, cmp_w2, w_ple_proj, g_ple, w_ple_gate):
    depth = w_in.shape[0]
    bp, seq, _ = x_prompt.shape
    bs, lq, _ = x_sample.shape
    n_pages = page_table.shape[1]
    past = n_pages * PAGE_SIZE
    n_pool = cache_cmp_kv.shape[1]
    wlen = cache_win_kv.shape[2]
    mp, ms = bp * seq, bs * lq
    assert seq % PK == 0 and seq % DN_CHUNK == 0 and lq == SUBLANES and past % NSA_BLOCK == 0
    assert wlen == WINDOW <= past
    kv_row = (2, NSA_KV_HEADS, NSA_HD)

    w_in_r = _relayout_w_in(w_in)
    w_out_b = w_out.astype(BF16)
    wpp_b = w_ple_proj.astype(BF16)
    wpg_b = w_ple_gate.astype(BF16)
    g_pre3 = g_pre[:, None, :]
    g_post3 = g_post[:, None, :]
    g_ple3 = g_ple[:, None, :]
    nw3 = dn_norm_w[:, None, :]
    lane_pad = ((0, 0), (DN_HEADS, LANES - 2 * DN_HEADS))
    dn_par = jnp.stack([jnp.pad(dn_A_log, lane_pad), jnp.pad(dn_dt_bias, lane_pad)], axis=1)
    pw_prompt = jnp.broadcast_to(cmp_pos_w[:, :, None, :, None], (depth, 2, NSA_KV_HEADS, NSA_BLOCK, NSA_HD))
    pw_prompt = jnp.transpose(pw_prompt, (0, 3, 1, 2, 4)).reshape(depth, NSA_BLOCK, 2 * NSA_KV_W)
    pw_page = jnp.broadcast_to(cmp_pos_w[:, None, None, :, :, None],
                               (depth, PAGE_SIZE // NSA_BLOCK, NSA_KV_HEADS, 2, NSA_BLOCK, NSA_HD))
    pw_page = jnp.transpose(pw_page, (0, 1, 4, 3, 2, 5)).reshape(depth, PAGE_SIZE * 4, NSA_HD)
    table = rel_bias_table.astype(F32)
    bias_tiles = _bias_tiles(table)
    nb_p = seq // NSA_BLOCK
    nb_s = past // NSA_BLOCK
    nbs_s = -(-(past + lq) // NSA_BLOCK)
    nbp_s = -(-nbs_s // LANES) * LANES
    bias_cmp_p = _bias_cmp(table, 0, seq, nb_p)
    bias_cmp_s = _bias_cmp(table, past, lq, nb_s)
    tcols = jnp.pad(table.T.reshape(NSA_KV_HEADS, NSA_GROUP, NUM_BUCKETS),
                    ((0, 0), (0, SUBLANES - NSA_GROUP), (0, 0)))
    tcols_rows = jnp.repeat(table.T.reshape(NSA_KV_HEADS, NSA_GROUP, NUM_BUCKETS), lq, axis=1)

    conv0_s = jnp.pad(state_dn_conv, ((0, 0), (0, 0), (SUBLANES - (CONV_W - 1), 0), (0, 0)))
    conv0_p = jnp.zeros((1, bp, SUBLANES, DN_CONV_CH), F32)
    s0_p = jnp.zeros((1, bp, DN_HEADS, DN_DK, DN_DV), F32)
    cmp4 = cache_cmp_kv.reshape(depth, n_pool, PAGE_SIZE * 4, NSA_HD)
    slc4 = cache_slc_kv.reshape(depth, n_pool, PAGE_SIZE * 4, NSA_HD)
    win4 = cache_win_kv.reshape(depth, bs, wlen * 4, NSA_HD)
    p_prompt3 = p_prompt.reshape(depth, mp, PLE_DIM)
    p_sample3 = p_sample.reshape(depth, ms, PLE_DIM)

    xp = x_prompt.reshape(mp, D_MODEL)
    xs = x_sample.reshape(ms, D_MODEL)
    outs_p = [[] for _ in range(2)]
    outs_s = [[] for _ in range(4)]
    projs_p, projs_s = [], []
    tq_p = min(256, seq)
    tm_p = min(IN_TM, mp)
    pages = min(POOL_PAGES, n_pages)
    fuse_pool = (mp // tm_p, N_PAD // IN_TN) == (bs, n_pages // pages)
    for i in range(depth):
        if fuse_pool:
            proj, pooled = _in_proj_pool(xp, g_pre3, w_in_r, cmp4, page_table, pw_page, i, tm_p, IN_TN, pages)
        else:
            proj = _in_proj(xp, g_pre3, w_in_r, i, tm_p)
            pooled = _pool_paged(cmp4, page_table, pw_page, i, pages)
        dn_o, conv_n, s_n = _deltanet(proj, bp, seq, DN_CHUNK, conv0_p, s0_p, 0, dn_conv_w, dn_par, nw3, i)
        o_cmp, sel = _cmp_attn(proj, _pool_prompt(proj, bp, seq, pw_prompt, i), cmp_w1, cmp_w2, i, bias_cmp_p,
                               bp, seq, 0, tq_p, nb_p, nb_p, True)
        nsa_o = _nsa_prompt(proj, sel, bias_tiles, o_cmp, bp, seq)
        xp = _out_proj(dn_o, nsa_o, xp, p_prompt3, w_out_b, g_post3, wpp_b, g_ple3, wpg_b, i, min(256, mp))
        projs_p.append(proj)
        outs_p[0].append(s_n)
        outs_p[1].append(conv_n[:, SUBLANES - (CONV_W - 1):])

        proj = _in_proj(xs, g_pre3, w_in_r, i, ms)
        dn_o, conv_n, s_n = _deltanet(proj, bs, lq, lq, conv0_s, state_dn_S, i, dn_conv_w, dn_par, nw3, i)
        o_cmp, score = _cmp_attn(proj, pooled, cmp_w1, cmp_w2, i, bias_cmp_s, bs, lq, past, lq, nbs_s, nbp_s, False)
        score_t = jnp.transpose(score.reshape(ms * NSA_KV_HEADS, nbp_s))
        _, idx_t, val_t = _topk(score_t, nbs_s)
        o_slc = _slc_sample(proj, slc4, page_table, idx_t.T.reshape(-1), val_t.T.reshape(-1), tcols, i,
                            bs, lq, past)
        nsa_o = _win_sample(proj, win4, tcols_rows, o_cmp, o_slc, i, bs, lq, past)
        xs = _out_proj(dn_o, nsa_o, xs, p_sample3, w_out_b, g_post3, wpp_b, g_ple3, wpg_b, i, ms)
        kv_all = proj[:, C_CMP:C_WIN].reshape(bs, lq, 2, *kv_row)
        projs_s.append(proj)
        outs_s[0].append(s_n)
        outs_s[1].append(conv_n[:, SUBLANES - (CONV_W - 1):])
        outs_s[2].append(kv_all[:, :, 0])
        outs_s[3].append(kv_all[:, :, 1])

    p_cmp, p_slc, p_win = _kv_rows(projs_p, bp, seq)
    s_win = _win_state(win4, projs_s, bs, lq).reshape(depth, bs, wlen, *kv_row)
    s_S, s_conv, s_cmp, s_slc = (jnp.stack(o) for o in outs_s)
    return (xp.reshape(bp, seq, D_MODEL), xs.reshape(bs, lq, D_MODEL), jnp.stack(outs_p[0]), jnp.stack(outs_p[1]),
            p_win.reshape(depth, bp, min(WINDOW, seq), *kv_row), p_cmp.reshape(depth, bp, seq, *kv_row),
            p_slc.reshape(depth, bp, seq, *kv_row), s_S, s_conv, s_win, s_cmp, s_slc)
```

```python
import functools
import math

import numpy as np
import jax
import jax.numpy as jnp
from jax import lax
from jax.experimental import pallas as pl
from jax.experimental.pallas import tpu as pltpu

F32 = jnp.float32
BF16 = jnp.bfloat16

D_MODEL = 2048
DN_HEADS = 8
DN_DK = 128
DN_DV = 128
DN_QK_W = DN_HEADS * DN_DK
DN_CONV_CH = 2 * DN_QK_W + DN_HEADS * DN_DV
CONV_W = 4
DN_CHUNK = 64
NSA_HEADS = 8
NSA_KV_HEADS = 2
NSA_GROUP = 4
NSA_HD = 128
NSA_WIDTH = NSA_HEADS * NSA_HD
NSA_KV_W = NSA_KV_HEADS * NSA_HD
NSA_BLOCK = 64
NSA_TOPN = 16
WINDOW = 512
SCALE = NSA_HD ** -0.5
NUM_BUCKETS = 32
REL_MAX_DIST = 1024
PLE_DIM = 256
PAGE_SIZE = 128
EPS = 1e-6
NEG = -1e30
LOG2E = math.log2(math.e)

VMEM_LIMIT_BYTES = 56 * 1024 * 1024
LANES = 128
SUBLANES = 8

C_QKV = 0
C_Z = 3072
C_NQ = 4096
C_NZ = 5120
C_CMP = 6144
C_SLC = 6656
C_WIN = 7168
C_SM = 7680
C_G0 = 7808
C_G1 = 7936
N_PAD = 8192
_O_BETA = DN_CONV_CH + DN_HEADS * DN_DV
_O_NQ = _O_BETA + 2 * DN_HEADS
_O_GATE = _O_NQ + 2 * NSA_WIDTH + 3 * 2 * NSA_KV_W
N_BIAS_TILES = 9


def _bucket_np(dist):
    dist = np.maximum(np.asarray(dist, np.int64), 0)
    exact = NUM_BUCKETS // 2
    scaled = np.log(np.maximum(dist, 1).astype(np.float64) / exact) / math.log(REL_MAX_DIST / exact)
    large = np.minimum(exact + (scaled * (NUM_BUCKETS - exact)).astype(np.int64), NUM_BUCKETS - 1)
    return np.where(dist < exact, dist, large).astype(np.int32)


_BUCKET_THR = tuple(int(np.argmax(_bucket_np(np.arange(2 * REL_MAX_DIST)) >= b)) for b in range(1, NUM_BUCKETS))


def _dot(a, b):
    return jnp.dot(a.astype(BF16), b.astype(BF16), preferred_element_type=F32)


def _dot_nt(a, b):
    return lax.dot_general(a.astype(BF16), b.astype(BF16), (((1,), (1,)), ((), ())), preferred_element_type=F32)


def _dot_hi(a, b):
    return jnp.dot(a, b, preferred_element_type=F32, precision=lax.Precision.HIGHEST)


def _dot_hi_nt(a, b):
    return lax.dot_general(a, b, (((1,), (1,)), ((), ())), preferred_element_type=F32,
                           precision=lax.Precision.HIGHEST)


def _silu(x):
    return x * jax.nn.sigmoid(x)


def _params(*sem):
    return pltpu.CompilerParams(dimension_semantics=sem, vmem_limit_bytes=VMEM_LIMIT_BYTES)


def _in_proj_kernel(x_ref, g_ref, w_ref, o_ref, xn_ref):
    @pl.when(pl.program_id(1) == 0)
    def _():
        x = x_ref[...]
        y = x * lax.rsqrt(jnp.mean(x * x, axis=-1, keepdims=True) + EPS)
        xn_ref[...] = (y * g_ref[...]).astype(BF16)

    o_ref[...] = jnp.dot(xn_ref[...], w_ref[...], preferred_element_type=F32)


IN_TM = 1024
IN_TN = 1024
POOL_PAGES = 16


def _in_proj(x, g_all, w_all, layer, tm):
    m = x.shape[0]
    tn = IN_TN
    return pl.pallas_call(
        _in_proj_kernel,
        grid=(m // tm, N_PAD // tn),
        in_specs=[pl.BlockSpec((tm, D_MODEL), lambda i, j: (i, 0)),
                  pl.BlockSpec((None, 1, D_MODEL), lambda i, j: (layer, 0, 0)),
                  pl.BlockSpec((None, D_MODEL, tn), lambda i, j: (layer, 0, j))],
        out_specs=pl.BlockSpec((tm, tn), lambda i, j: (i, j)),
        out_shape=jax.ShapeDtypeStruct((m, N_PAD), F32),
        scratch_shapes=[pltpu.VMEM((tm, D_MODEL), BF16)],
        compiler_params=_params("parallel", "arbitrary"),
        name="in_proj",
    )(x, g_all, w_all)


def _dn_prep_kernel(qkv_ref, sm_ref, conv0_ref, cw_ref, par_ref,
                    u_ref, w_ref, qd_ref, kd_ref, qk_ref, gl_ref, convn_ref, xs_ref, *, chunk, cs):
    tb = chunk * cs
    halo = SUBLANES

    @pl.when(pl.program_id(1) == 0)
    def _():
        xs_ref[0:halo, :] = conv0_ref[...]

    xs_ref[halo:halo + tb, :] = qkv_ref[...]
    base = halo - (CONV_W - 1)
    y = xs_ref[base:base + tb, :] * cw_ref[0:1, :]
    for j in range(1, CONV_W):
        y = y + xs_ref[base + j:base + j + tb, :] * cw_ref[j:j + 1, :]
    y = _silu(y)
    tail = xs_ref[tb:tb + halo, :]
    xs_ref[0:halo, :] = tail
    convn_ref[...] = tail

    sm = sm_ref[...]
    beta_all = jax.nn.sigmoid(sm)
    xg = sm + par_ref[1:2, :]
    softplus = jnp.maximum(xg, 0.0) + jnp.log1p(jnp.exp(-jnp.abs(xg)))
    g_all = -jnp.exp(par_ref[0:1, :]) * softplus
    row = lax.broadcasted_iota(jnp.int32, (tb, tb), 0)
    col = lax.broadcasted_iota(jnp.int32, (tb, tb), 1)
    in_chunk_lower = (row >= col) & (row // chunk == col // chunk)
    gc_all = _dot_hi(jnp.where(in_chunk_lower, 1.0, 0.0), g_all)
    pick = (lax.broadcasted_iota(jnp.int32, (DN_HEADS, LANES), 1)
            == lax.broadcasted_iota(jnp.int32, (DN_HEADS, LANES), 0) + DN_HEADS).astype(F32)
    gc_rows = _dot_hi_nt(pick, gc_all)
    for c in range(cs):
        gl_ref[c] = gc_all[(c + 1) * chunk - 1:(c + 1) * chunk, :]
    r64 = lax.broadcasted_iota(jnp.int32, (chunk, chunk), 0)
    c64 = lax.broadcasted_iota(jnp.int32, (chunk, chunk), 1)
    lower = r64 >= c64
    strict = r64 > c64

    lhs, kns, rhs, decays = [], [], [], []
    for h in range(DN_HEADS):
        hc = slice(h * DN_DV, (h + 1) * DN_DV)
        qh = y[:, h * DN_DK:(h + 1) * DN_DK]
        kh = y[:, DN_QK_W + h * DN_DK:DN_QK_W + (h + 1) * DN_DK]
        vh = y[:, 2 * DN_QK_W + h * DN_DV:2 * DN_QK_W + (h + 1) * DN_DV]
        qn = qh * lax.rsqrt(jnp.sum(qh * qh, axis=-1, keepdims=True) + EPS) * (DN_DK ** -0.5)
        kn = kh * lax.rsqrt(jnp.sum(kh * kh, axis=-1, keepdims=True) + EPS)
        beta = beta_all[:, h:h + 1]
        gc = gc_all[:, DN_HEADS + h:DN_HEADS + h + 1]
        egc = jnp.exp(gc)
        kb = kn * beta
        rhs_h = jnp.concatenate([vh * beta, kb * egc], axis=-1)
        qd_ref[:, hc] = (qn * egc).astype(qd_ref.dtype)
        for c in range(cs):
            rs = slice(c * chunk, (c + 1) * chunk)
            gcc = gc[rs]
            decays.append(jnp.exp(jnp.where(lower, gcc - gc_rows[h:h + 1, rs], NEG)))
            lhs.append(jnp.concatenate([kb[rs], qn[rs]], axis=0))
            kns.append(kn[rs])
            rhs.append(rhs_h[rs])
            kd_ref[rs, hc] = (kn[rs] * jnp.exp(gcc[chunk - 1:chunk, :] - gcc)).astype(kd_ref.dtype)
    decay = jnp.stack(decays)
    both = jnp.einsum('nid,njd->nij', jnp.stack(lhs).astype(BF16), jnp.stack(kns).astype(BF16),
                      preferred_element_type=F32)
    a = jnp.where(strict[None], both[:, :chunk] * decay, 0.0)
    qk = both[:, chunk:] * decay

    def bmm(p, q):
        return jnp.einsum('nij,njk->nik', p, q, preferred_element_type=F32)

    def split(v):
        hi = v.astype(BF16)
        return hi, (v - hi.astype(F32)).astype(BF16)

    def bmm3(p, q):
        (ph, pl_), (qh, ql) = p, q
        return bmm(ph, qh) + (bmm(ph, ql) + bmm(pl_, qh))

    pw_s = split(a)
    inv = jnp.where(r64 == c64, 1.0, 0.0)[None] - a
    for _ in range(int(math.log2(chunk)) - 1):
        pw_s = split(bmm3(pw_s, pw_s))
        inv = inv + bmm3(pw_s, split(inv))
    x = bmm3(split(inv), split(jnp.stack(rhs)))
    for h in range(DN_HEADS):
        hc = slice(h * DN_DV, (h + 1) * DN_DV)
        for c in range(cs):
            rs = slice(c * chunk, (c + 1) * chunk)
            n = h * cs + c
            u_ref[rs, hc] = x[n, :, :DN_DV]
            w_ref[rs, hc] = x[n, :, DN_DV:].astype(w_ref.dtype)
            qk_ref[rs, h * chunk:(h + 1) * chunk] = qk[n]


def _dn_scan_kernel(u_ref, w_ref, qd_ref, kd_ref, qk_ref, gl_ref, z_ref, s0_ref, nw_ref,
                    o_ref, sout_ref, s_ref, *, chunk, cs):
    @pl.when(pl.program_id(1) == 0)
    def _():
        s_ref[...] = s0_ref[...]

    def heads(ref, rs, width):
        return jnp.stack([ref[rs, h * width:(h + 1) * width] for h in range(DN_HEADS)])

    def bdot(spec, p, q):
        return jnp.einsum(spec, p.astype(BF16), q.astype(BF16), preferred_element_type=F32)

    for c in range(cs):
        rs = slice(c * chunk, (c + 1) * chunk)
        decay_last = jnp.exp(gl_ref[c])
        decay_last = jnp.stack([decay_last[:, DN_HEADS + h:DN_HEADS + h + 1] for h in range(DN_HEADS)])
        s = s_ref[...]
        sb = s.astype(BF16)
        v_new = heads(u_ref, rs, DN_DV) - bdot('hcd,hde->hce', heads(w_ref, rs, DN_DK), sb)
        o = (bdot('hcd,hde->hce', heads(qd_ref, rs, DN_DK), sb)
             + bdot('hcs,hse->hce', heads(qk_ref, rs, chunk), v_new))
        s_ref[...] = s * decay_last + bdot('hcd,hce->hde', heads(kd_ref, rs, DN_DK), v_new)
        on = o * lax.rsqrt(jnp.mean(o * o, axis=-1, keepdims=True) + EPS) * nw_ref[...]
        for h in range(DN_HEADS):
            hc = slice(h * DN_DV, (h + 1) * DN_DV)
            o_ref[rs, hc] = on[h] * _silu(z_ref[rs, hc])

    @pl.when(pl.program_id(1) == pl.num_programs(1) - 1)
    def _():
        sout_ref[...] = s_ref[...]


def _deltanet(proj, nb, seq, chunk, conv0_all, s0_all, state_layer, cw_all, par_all, nw_all, layer):
    m = nb * seq
    nchunks = seq // chunk
    width = DN_HEADS * DN_DV
    hs = (DN_HEADS, DN_DK, DN_DV)
    cs_a = 4 if nchunks % 4 == 0 else 1
    na = nchunks // cs_a
    ta = cs_a * chunk
    op_dtype = BF16 if chunk % (2 * SUBLANES) == 0 else F32
    u, w, qd, kd, qk, gl, conv_n = pl.pallas_call(
        functools.partial(_dn_prep_kernel, chunk=chunk, cs=cs_a),
        grid=(nb, na),
        in_specs=[pl.BlockSpec((ta, DN_CONV_CH), lambda b, c: (b * na + c, C_QKV // DN_CONV_CH)),
                  pl.BlockSpec((ta, LANES), lambda b, c: (b * na + c, C_SM // LANES)),
                  pl.BlockSpec((None, None, SUBLANES, DN_CONV_CH), lambda b, c: (state_layer, b, 0, 0)),
                  pl.BlockSpec((None, CONV_W, DN_CONV_CH), lambda b, c: (layer, 0, 0)),
                  pl.BlockSpec((None, 2, LANES), lambda b, c: (layer, 0, 0))],
        out_specs=[pl.BlockSpec((ta, width), lambda b, c: (b * na + c, 0)),
                   pl.BlockSpec((ta, width), lambda b, c: (b * na + c, 0)),
                   pl.BlockSpec((ta, width), lambda b, c: (b * na + c, 0)),
                   pl.BlockSpec((ta, width), lambda b, c: (b * na + c, 0)),
                   pl.BlockSpec((ta, DN_HEADS * chunk), lambda b, c: (b * na + c, 0)),
                   pl.BlockSpec((cs_a, 1, LANES), lambda b, c: (b * na + c, 0, 0)),
                   pl.BlockSpec((None, SUBLANES, DN_CONV_CH), lambda b, c: (b, 0, 0))],
        out_shape=[jax.ShapeDtypeStruct((m, width), F32),
                   jax.ShapeDtypeStruct((m, width), op_dtype),
                   jax.ShapeDtypeStruct((m, width), op_dtype),
                   jax.ShapeDtypeStruct((m, width), op_dtype),
                   jax.ShapeDtypeStruct((m, DN_HEADS * chunk), F32),
                   jax.ShapeDtypeStruct((nb * nchunks, 1, LANES), F32),
                   jax.ShapeDtypeStruct((nb, SUBLANES, DN_CONV_CH), F32)],
        scratch_shapes=[pltpu.VMEM((ta + SUBLANES, DN_CONV_CH), F32)],
        compiler_params=_params("parallel", "arbitrary"),
        name="dn_prep",
    )(proj, proj, conv0_all, cw_all, par_all)

    cs_b = 8 if nchunks % 8 == 0 else 1
    nbk = nchunks // cs_b
    tbk = cs_b * chunk
    row_spec = pl.BlockSpec((tbk, width), lambda b, c: (b * nbk + c, 0))
    o, s_n = pl.pallas_call(
        functools.partial(_dn_scan_kernel, chunk=chunk, cs=cs_b),
        grid=(nb, nbk),
        in_specs=[row_spec, row_spec, row_spec, row_spec,
                  pl.BlockSpec((tbk, DN_HEADS * chunk), lambda b, c: (b * nbk + c, 0)),
                  pl.BlockSpec((cs_b, 1, LANES), lambda b, c: (b * nbk + c, 0, 0)),
                  pl.BlockSpec((tbk, width), lambda b, c: (b * nbk + c, C_Z // width)),
                  pl.BlockSpec((None, None) + hs, lambda b, c: (state_layer, b, 0, 0, 0)),
                  pl.BlockSpec((None, 1, DN_DV), lambda b, c: (layer, 0, 0))],
        out_specs=[row_spec, pl.BlockSpec((None,) + hs, lambda b, c: (b, 0, 0, 0))],
        out_shape=[jax.ShapeDtypeStruct((m, width), F32), jax.ShapeDtypeStruct((nb,) + hs, F32)],
        scratch_shapes=[pltpu.VMEM(hs, F32)],
        compiler_params=_params("parallel", "arbitrary"),
        name="dn_scan",
    )(u, w, qd, kd, qk, gl, proj, s0_all, nw_all)
    return o, conv_n, s_n


def _pool_kernel(x_ref, pw_ref, o_ref):
    rows = x_ref.shape[0]
    x = x_ref[...].reshape(rows // NSA_BLOCK, NSA_BLOCK, 2 * NSA_KV_W)
    o_ref[...] = jnp.sum(x * pw_ref[...][None], axis=1)


def _pool_prompt(proj, nb, seq, pw_all, layer):
    rows = min(seq, 1024)
    nr = seq // rows
    width = 2 * NSA_KV_W
    return pl.pallas_call(
        _pool_kernel,
        grid=(nb, nr),
        in_specs=[pl.BlockSpec((rows, width), lambda b, r: (b * nr + r, C_CMP // width)),
                  pl.BlockSpec((None, NSA_BLOCK, width), lambda b, r: (layer, 0, 0))],
        out_specs=pl.BlockSpec((None, rows // NSA_BLOCK, width), lambda b, r: (b, r, 0)),
        out_shape=jax.ShapeDtypeStruct((nb, seq // NSA_BLOCK, width), F32),
        compiler_params=_params("parallel", "parallel"),
        name="pool_prompt",
    )(proj, pw_all)


def _pool_pages(page_refs, pw_ref, o_ref):
    rows = PAGE_SIZE * 4
    per_block = rows // 2 // SUBLANES
    for k, page_ref in enumerate(page_refs):
        prod = page_ref[...] * pw_ref[...]
        y = jnp.sum(prod.reshape(2, per_block, SUBLANES, NSA_HD), axis=1)
        y = y[:, 0:4, :] + y[:, 4:8, :]
        for s in range(4):
            o_ref[2 * k:2 * k + 2, s * NSA_HD:(s + 1) * NSA_HD] = y[:, s, :]


def _paged_pool_kernel(pt_ref, *refs, pages):
    del pt_ref
    _pool_pages(refs[:pages], refs[pages], refs[pages + 1])


def _in_proj_pool_kernel(pt_ref, x_ref, g_ref, w_ref, *refs, pages):
    del pt_ref
    o_ref, pooled_ref, xn_ref = refs[pages + 1:]
    _in_proj_kernel(x_ref, g_ref, w_ref, o_ref, xn_ref)
    _pool_pages(refs[:pages], refs[pages], pooled_ref)


def _in_proj_pool(x, g_all, w_all, cache4, page_table, pw_all, layer, tm, tn, pages):
    m = x.shape[0]
    nb, n_pages = page_table.shape
    assert (m // tm, N_PAD // tn) == (nb, n_pages // pages)
    rows = PAGE_SIZE * 4

    def page_spec(k):
        return pl.BlockSpec((None, None, rows, NSA_HD), lambda i, j, pt: (layer, pt[i, j * pages + k], 0, 0))

    grid_spec = pltpu.PrefetchScalarGridSpec(
        num_scalar_prefetch=1,
        grid=(m // tm, N_PAD // tn),
        in_specs=[pl.BlockSpec((tm, D_MODEL), lambda i, j, pt: (i, 0)),
                  pl.BlockSpec((None, 1, D_MODEL), lambda i, j, pt: (layer, 0, 0)),
                  pl.BlockSpec((None, D_MODEL, tn), lambda i, j, pt: (layer, 0, j))]
        + [page_spec(k) for k in range(pages)]
        + [pl.BlockSpec((None, rows, NSA_HD), lambda i, j, pt: (layer, 0, 0))],
        out_specs=[pl.BlockSpec((tm, tn), lambda i, j, pt: (i, j)),
                   pl.BlockSpec((None, 2 * pages, 4 * NSA_HD), lambda i, j, pt: (i, j, 0))],
        scratch_shapes=[pltpu.VMEM((tm, D_MODEL), BF16)],
    )
    return pl.pallas_call(
        functools.partial(_in_proj_pool_kernel, pages=pages),
        grid_spec=grid_spec,
        out_shape=[jax.ShapeDtypeStruct((m, N_PAD), F32),
                   jax.ShapeDtypeStruct((nb, 2 * n_pages, 4 * NSA_HD), F32)],
        compiler_params=_params("parallel", "arbitrary"),
        name="in_proj_pool",
    )(page_table, x, g_all, w_all, *([cache4] * pages), pw_all)


def _pool_paged(cache4, page_table, pw_all, layer, pages):
    nb, n_pages = page_table.shape
    steps = n_pages // pages
    rows = PAGE_SIZE * 4

    def page_spec(k):
        return pl.BlockSpec((None, None, rows, NSA_HD),
                            lambda b, s, pt: (layer, pt[b, s * pages + k], 0, 0))

    grid_spec = pltpu.PrefetchScalarGridSpec(
        num_scalar_prefetch=1,
        grid=(nb, steps),
        in_specs=[page_spec(k) for k in range(pages)]
        + [pl.BlockSpec((None, rows, NSA_HD), lambda b, s, pt: (layer, 0, 0))],
        out_specs=pl.BlockSpec((None, 2 * pages, 4 * NSA_HD), lambda b, s, pt: (b, s, 0)),
    )
    return pl.pallas_call(
        functools.partial(_paged_pool_kernel, pages=pages),
        grid_spec=grid_spec,
        out_shape=jax.ShapeDtypeStruct((nb, 2 * n_pages, 4 * NSA_HD), F32),
        compiler_params=_params("parallel", "arbitrary"),
        name="pool_paged",
    )(page_table, *([cache4] * pages), pw_all)


def _select_blocks(score, st_ref, nbp, n_causal):
    st_ref[...] = score.T
    halves = [st_ref[g * nbp:(g + 1) * nbp, :] for g in range(NSA_KV_HEADS)]
    bi = lax.broadcasted_iota(jnp.int32, halves[0].shape, 0)

    def body(j, ranks):
        tie = jnp.where(j < bi, 1, 0)
        out = []
        for g in range(NSA_KV_HEADS):
            other = st_ref[pl.ds(g * nbp + j, 1), :]
            out.append(ranks[g] + jnp.where(other > halves[g], 1, jnp.where(other == halves[g], tie, 0)))
        return tuple(out)

    ranks = lax.fori_loop(0, n_causal, body, tuple(jnp.zeros(bi.shape, jnp.int32) for _ in halves))
    keep = [jnp.where((ranks[g] < NSA_TOPN) & (halves[g] >= 0.0), 1.0, 0.0) for g in range(NSA_KV_HEADS)]
    return jnp.concatenate(keep, axis=0).T


def _cmp_attn_kernel(q_ref, pooled_ref, w1_ref, w2_ref, bias_ref, o_ref, sc_ref, ckv_ref, *scratch,
                     pos0, tq, nb, nbs, nbp, select):
    qi = pl.program_id(1)

    @pl.when(qi == 0)
    def _():
        for s in range(2 * NSA_KV_HEADS):
            kv = s // NSA_KV_HEADS
            h = _silu(_dot_hi(pooled_ref[:, s * NSA_HD:(s + 1) * NSA_HD], w1_ref[kv]))
            ckv_ref[:, s * NSA_HD:(s + 1) * NSA_HD] = _dot_hi(h, w2_ref[kv])

    t = pos0 + qi * tq + lax.broadcasted_iota(jnp.int32, (tq, 1), 0)
    blk = lax.broadcasted_iota(jnp.int32, (1, nb), 1)
    valid = t >= blk * NSA_BLOCK + (NSA_BLOCK - 1)
    bi = lax.broadcasted_iota(jnp.int32, (1, nbp), 1)
    cur = t // NSA_BLOCK
    forced = (bi == 0) | (bi == cur) | (bi == cur - 1)
    causal = bi <= cur
    scores = []
    for g in range(NSA_KV_HEADS):
        ck = ckv_ref[:, g * NSA_HD:(g + 1) * NSA_HD]
        cv = ckv_ref[:, (NSA_KV_HEADS + g) * NSA_HD:(NSA_KV_HEADS + g + 1) * NSA_HD]
        imp = jnp.zeros((tq, nb), F32)
        for r in range(NSA_GROUP):
            h = g * NSA_GROUP + r
            logits = _dot_nt(q_ref[:, h * NSA_HD:(h + 1) * NSA_HD], ck) * SCALE + bias_ref[h]
            logits = jnp.where(valid, logits, NEG)
            mx = jnp.max(logits, axis=-1, keepdims=True)
            p = jnp.where(valid, jnp.exp(logits - mx), 0.0)
            p = p / jnp.maximum(jnp.sum(p, axis=-1, keepdims=True), 1e-30)
            o_ref[:, h * NSA_HD:(h + 1) * NSA_HD] = _dot(p, cv)
            imp = imp + p
        if nbp > nb:
            imp = jnp.concatenate([imp, jnp.zeros((tq, nbp - nb), F32)], axis=-1)
        score = jnp.where(causal, jnp.where(forced, NSA_GROUP + 1.0, imp), -1.0)
        scores.append(jnp.where(bi < nbs, score, -2.0))
    score = jnp.concatenate(scores, axis=-1)
    if select:
        n_causal = jnp.minimum((pos0 + (qi + 1) * tq - 1) // NSA_BLOCK + 1, nbs)
        sc_ref[...] = _select_blocks(score, scratch[0], nbp, n_causal)
    else:
        sc_ref[...] = score


def _cmp_attn(proj, pooled, w1_all, w2_all, layer, bias_cmp, nb_batch, seq, pos0, tq, nbs, nbp, select):
    nq = seq // tq
    nb = pooled.shape[1]
    m = nb_batch * seq
    bias_tq = bias_cmp.shape[1] // nq
    wspec = pl.BlockSpec((None, 2, NSA_HD, NSA_HD), lambda b, i: (layer, 0, 0, 0))
    return pl.pallas_call(
        functools.partial(_cmp_attn_kernel, pos0=pos0, tq=tq, nb=nb, nbs=nbs, nbp=nbp, select=select),
        scratch_shapes=[pltpu.VMEM((nb, 2 * NSA_KV_W), F32)]
        + ([pltpu.VMEM((NSA_KV_HEADS * nbp, tq), F32)] if select else []),
        grid=(nb_batch, nq),
        in_specs=[pl.BlockSpec((tq, NSA_WIDTH), lambda b, i: (b * nq + i, C_NQ // NSA_WIDTH)),
                  pl.BlockSpec((None, nb, 2 * NSA_KV_W), lambda b, i: (b, 0, 0)), wspec, wspec,
                  pl.BlockSpec((NSA_HEADS, bias_tq, nb), lambda b, i: (0, i, 0))],
        out_specs=[pl.BlockSpec((tq, NSA_WIDTH), lambda b, i: (b * nq + i, 0)),
                   pl.BlockSpec((tq, NSA_KV_HEADS * nbp), lambda b, i: (b * nq + i, 0))],
        out_shape=[jax.ShapeDtypeStruct((m, NSA_WIDTH), F32),
                   jax.ShapeDtypeStruct((m, NSA_KV_HEADS * nbp), F32)],
        compiler_params=_params("parallel", "arbitrary"),
        name="cmp_attn",
    )(proj, pooled, w1_all, w2_all, bias_cmp)


def _topk_kernel(s_ref, sel_ref, idx_ref, val_ref, *, nbs):
    s = s_ref[...]
    bi = lax.broadcasted_iota(jnp.int32, s.shape, 0)

    def body(j, rank):
        other = s_ref[pl.ds(j, 1), :]
        beats = (other > s) | ((other == s) & (j < bi))
        return rank + beats.astype(jnp.int32)

    rank = lax.fori_loop(0, nbs, body, jnp.zeros(s.shape, jnp.int32))
    ok = s >= 0.0
    sel_ref[...] = jnp.where((rank < NSA_TOPN) & ok, 1.0, 0.0)
    for r in range(NSA_TOPN):
        hit = rank == r
        idx_ref[r:r + 1, :] = jnp.sum(jnp.where(hit, bi, 0), axis=0, keepdims=True)
        val_ref[r:r + 1, :] = jnp.sum(jnp.where(hit & ok, 1, 0), axis=0, keepdims=True)


def _topk(score_t, nbs):
    rows, cols = score_t.shape
    tc = min(cols, 512)
    return pl.pallas_call(
        functools.partial(_topk_kernel, nbs=nbs),
        grid=(cols // tc,),
        in_specs=[pl.BlockSpec((rows, tc), lambda i: (0, i))],
        out_specs=[pl.BlockSpec((rows, tc), lambda i: (0, i)),
                   pl.BlockSpec((NSA_TOPN, tc), lambda i: (0, i)),
                   pl.BlockSpec((NSA_TOPN, tc), lambda i: (0, i))],
        out_shape=[jax.ShapeDtypeStruct((rows, cols), F32),
                   jax.ShapeDtypeStruct((NSA_TOPN, cols), jnp.int32),
                   jax.ShapeDtypeStruct((NSA_TOPN, cols), jnp.int32)],
        compiler_params=_params("parallel"),
        name="topk",
    )(score_t)


PQ = 256
PK = 512
BT = LANES


def _softmax_rows(s, ok):
    mx = jnp.max(s, axis=-1, keepdims=True)
    p = jnp.where(ok, jnp.exp(s - mx), 0.0)
    return p, jnp.sum(p, axis=-1, keepdims=True)


def _nsa_prompt_kernel(q_ref, ks_ref, vs_ref, kw_ref, vw_ref, sel_ref, bt_ref, ocmp_ref, z_ref, gate_ref,
                       o_ref, s2_ref, p2_ref, sw_ref, pw_ref, m_ref, acc_ref, *, n_bt, wsub):
    qi = pl.program_id(2)
    rows = NSA_GROUP * PQ
    qsub = PQ // BT
    ksub = PK // BT
    q4 = jnp.concatenate([q_ref[:, r * NSA_HD:(r + 1) * NSA_HD] for r in range(NSA_GROUP)], axis=0)
    q4 = (q4 * (SCALE * LOG2E)).astype(BF16)
    t = qi * PQ + lax.broadcasted_iota(jnp.int32, (PQ, 1), 0)
    nsel = sel_ref.shape[1]
    blk0 = pl.program_id(1) * (nsel // NSA_KV_HEADS)
    unselected = jnp.where(sel_ref[...] > 0.5, 0.0, NEG).astype(BF16)
    qx = jnp.concatenate([q4, jnp.concatenate([unselected] * NSA_GROUP, axis=0)], axis=1)

    def row_groups():
        for r in range(NSA_GROUP):
            for a in range(qsub):
                yield slice(r * PQ + a * BT, r * PQ + (a + 1) * BT), r, a

    def softmax_block(s_ref, p_ref, rs, r, tile_of, extra, width, m_old):
        zmax = None
        for c in range(width):
            cs = slice(c * BT, (c + 1) * BT)
            z = s_ref[rs, cs] + bt_ref[r, jnp.clip(tile_of(c), 0, N_BIAS_TILES - 1)]
            if extra is not None:
                z = z + extra[:, cs]
            s_ref[rs, cs] = z
            zmax = z if zmax is None else jnp.maximum(zmax, z)
        mx = jnp.broadcast_to(jnp.max(zmax, axis=-1, keepdims=True), (BT, BT))
        if m_old is not None:
            mx = jnp.maximum(m_old, mx)
        for c in range(width):
            cs = slice(c * BT, (c + 1) * BT)
            p_ref[rs, cs] = jnp.exp2(s_ref[rs, cs] - mx).astype(BF16)
        return mx

    def with_ones(v):
        return jnp.concatenate([v.astype(BF16), jnp.ones(v.shape, BF16)], axis=1)

    m_ref[...] = jnp.full((rows, BT), NEG, F32)
    acc_ref[...] = jnp.zeros((rows, 2 * NSA_HD), F32)

    def key_tile(kt, diagonal, slot):
        s_ref, p_ref = s2_ref.at[slot], p2_ref.at[slot]
        k0 = pl.multiple_of(kt * PK, PK)
        member = (lax.broadcasted_iota(jnp.int32, (PK, nsel), 1)
                  == blk0 + kt * (PK // NSA_BLOCK) + lax.broadcasted_iota(jnp.int32, (PK, nsel), 0) // NSA_BLOCK)
        kx = jnp.concatenate([ks_ref[pl.ds(k0, PK), :].astype(BF16), jnp.where(member, 1.0, 0.0).astype(BF16)],
                             axis=1)
        s_ref[...] = lax.dot_general(qx, kx, (((1,), (1,)), ((), ())), preferred_element_type=F32)
        future = None
        if diagonal:
            future = jnp.where(k0 + lax.broadcasted_iota(jnp.int32, (1, PK), 1) <= t, 0.0, NEG)
        for rs, r, a in row_groups():
            m_old = m_ref[rs]
            mx = softmax_block(s_ref, p_ref, rs, r, lambda c: qi * qsub + a - (kt * ksub + c),
                               None if future is None else future[a * BT:(a + 1) * BT], ksub, m_old)
            alpha = jnp.exp2(m_old - mx)
            acc_ref[rs] = jnp.concatenate([alpha, alpha], axis=1) * acc_ref[rs]
            m_ref[rs] = mx
        acc_ref[...] += jnp.dot(p_ref[...], with_ones(vs_ref[pl.ds(k0, PK), :]), preferred_element_type=F32)

    last = (qi * PQ) // PK

    def body(j, carry):
        key_tile(2 * j, False, 0)
        key_tile(2 * j + 1, False, 1)
        return carry

    lax.fori_loop(0, last // 2, body, 0)

    @pl.when(last % 2 == 1)
    def _():
        key_tile(last - 1, False, 0)

    key_tile(last, True, 1)
    o_slc = acc_ref[:, :NSA_HD] / jnp.maximum(acc_ref[:, NSA_HD:], 1e-30)

    wk = wsub * BT
    kst = jnp.clip(qi * qsub - WINDOW // BT, 0, n_bt - wsub)
    k0 = pl.multiple_of(kst * BT, BT)
    sw_ref[...] = lax.dot_general(q4, kw_ref[pl.ds(k0, wk), :].astype(BF16), (((1,), (1,)), ((), ())),
                                  preferred_element_type=F32)
    dist = t - (k0 + lax.broadcasted_iota(jnp.int32, (1, wk), 1))
    outside = jnp.where((dist >= 0) & (dist <= WINDOW), 0.0, NEG)
    for rs, r, a in row_groups():
        softmax_block(sw_ref, pw_ref, rs, r, lambda c: qi * qsub + a - (kst + c),
                      outside[a * BT:(a + 1) * BT], wsub, None)
    pv = jnp.dot(pw_ref[...], with_ones(vw_ref[pl.ds(k0, wk), :]), preferred_element_type=F32)
    o_win = pv[:, :NSA_HD] / jnp.maximum(pv[:, NSA_HD:], 1e-30)

    gates = jax.nn.sigmoid(gate_ref[...])
    for r in range(NSA_GROUP):
        sl = slice(r * NSA_HD, (r + 1) * NSA_HD)
        rs = slice(r * PQ, (r + 1) * PQ)
        o = (gates[:, 3 * r:3 * r + 1] * ocmp_ref[:, sl] + gates[:, 3 * r + 1:3 * r + 2] * o_slc[rs]
             + gates[:, 3 * r + 2:3 * r + 3] * o_win[rs])
        o_ref[:, sl] = o * _silu(z_ref[:, sl])


def _nsa_prompt(proj, sel, bias_tiles, o_cmp, nb_batch, seq):
    nq = seq // PQ
    n_bt = seq // BT
    wsub = min((WINDOW + PQ) // BT, n_bt)
    gw = NSA_GROUP * NSA_HD
    rows = NSA_GROUP * PQ
    nsel = sel.shape[-1]

    def kv_spec(col):
        return pl.BlockSpec((seq, NSA_HD), lambda b, g, i: (b, col // NSA_HD + g))

    return pl.pallas_call(
        functools.partial(_nsa_prompt_kernel, n_bt=n_bt, wsub=wsub),
        grid=(nb_batch, NSA_KV_HEADS, nq),
        in_specs=[pl.BlockSpec((PQ, gw), lambda b, g, i: (b * nq + i, C_NQ // gw + g)),
                  kv_spec(C_SLC), kv_spec(C_SLC + NSA_KV_W), kv_spec(C_WIN), kv_spec(C_WIN + NSA_KV_W),
                  pl.BlockSpec((PQ, nsel), lambda b, g, i: (b * nq + i, 0)),
                  pl.BlockSpec((NSA_GROUP, N_BIAS_TILES, BT, BT), lambda b, g, i: (g, 0, 0, 0)),
                  pl.BlockSpec((PQ, gw), lambda b, g, i: (b * nq + i, g)),
                  pl.BlockSpec((PQ, gw), lambda b, g, i: (b * nq + i, C_NZ // gw + g)),
                  pl.BlockSpec((PQ, LANES), lambda b, g, i: (b * nq + i, C_G0 // LANES + g))],
        out_specs=pl.BlockSpec((PQ, gw), lambda b, g, i: (b * nq + i, g)),
        out_shape=jax.ShapeDtypeStruct((nb_batch * seq, NSA_WIDTH), F32),
        scratch_shapes=[pltpu.VMEM((2, rows, PK), F32), pltpu.VMEM((2, rows, PK), BF16),
                        pltpu.VMEM((rows, wsub * BT), F32), pltpu.VMEM((rows, wsub * BT), BF16),
                        pltpu.VMEM((rows, BT), F32), pltpu.VMEM((rows, 2 * NSA_HD), F32)],
        compiler_params=_params("parallel", "parallel", "arbitrary"),
        name="nsa_prompt",
    )(proj, proj, proj, proj, proj, sel, bias_tiles, o_cmp, proj, proj)


def _bias_by_threshold(dist, tcols):
    bias = jnp.broadcast_to(tcols[:, 0:1], (tcols.shape[0], dist.shape[-1]))
    for b in range(1, NUM_BUCKETS):
        bias = jnp.where(dist >= _BUCKET_THR[b - 1], tcols[:, b:b + 1], bias)
    return bias


SLOT_CHUNKS = NSA_TOPN + 2


def _slc_sample_kernel(loc_ref, idx_ref, val_ref, *refs, past, lq):
    del loc_ref
    n_in = NSA_KV_HEADS * NSA_TOPN
    page_refs = refs[:n_in]
    q_ref, knew_ref, tcol_ref, o_ref, kc_ref, vc_ref = refs[n_in:]
    b = pl.program_id(0)
    qi = pl.program_id(1)
    t = past + qi
    n_past_blocks = past // NSA_BLOCK
    nk = SLOT_CHUNKS * NSA_BLOCK
    lane = lax.broadcasted_iota(jnp.int32, (1, nk), 1)
    slot = lane // NSA_BLOCK
    this_q = lax.broadcasted_iota(jnp.int32, (lq, 1), 0) == qi
    for g in range(NSA_KV_HEADS):
        base = ((b * lq + qi) * NSA_KV_HEADS + g) * NSA_TOPN
        blk_of_lane = jnp.where(slot == NSA_TOPN, n_past_blocks, n_past_blocks + 1)
        use_of_lane = jnp.where(slot == NSA_TOPN, 1, 0)
        for j in range(NSA_TOPN):
            n_j = idx_ref[base + j]
            use = jnp.where((val_ref[base + j] > 0) & (n_j < n_past_blocks), 1, 0)
            blk_of_lane = jnp.where(slot == j, n_j, blk_of_lane)
            use_of_lane = jnp.where(slot == j, use, use_of_lane)
            page = page_refs[g * NSA_TOPN + j]
            kc_ref[j * NSA_BLOCK:(j + 1) * NSA_BLOCK, :] = page[pl.ds(g, NSA_BLOCK, stride=4), :]
            vc_ref[j * NSA_BLOCK:(j + 1) * NSA_BLOCK, :] = page[pl.ds(NSA_KV_HEADS + g, NSA_BLOCK, stride=4), :]
        new0 = NSA_TOPN * NSA_BLOCK
        kc_ref[new0:, :] = jnp.zeros((nk - new0, NSA_HD), F32)
        vc_ref[new0:, :] = jnp.zeros((nk - new0, NSA_HD), F32)
        kc_ref[new0:new0 + lq, :] = knew_ref[:, g * NSA_HD:(g + 1) * NSA_HD]
        vc_ref[new0:new0 + lq, :] = knew_ref[:, (NSA_KV_HEADS + g) * NSA_HD:(NSA_KV_HEADS + g + 1) * NSA_HD]
        q4 = jnp.concatenate(
            [jnp.sum(jnp.where(this_q, q_ref[:, (g * NSA_GROUP + r) * NSA_HD:(g * NSA_GROUP + r + 1) * NSA_HD], 0.0),
                     axis=0, keepdims=True)
             for r in range(NSA_GROUP)] + [jnp.zeros((SUBLANES - NSA_GROUP, NSA_HD), F32)], axis=0)
        dist = t - (blk_of_lane * NSA_BLOCK + lane % NSA_BLOCK)
        ok = (use_of_lane > 0) & (dist >= 0)
        s = _dot_nt(q4, kc_ref[...]) * SCALE + _bias_by_threshold(dist, tcol_ref[g])
        s = jnp.where(ok, s, NEG)
        p, den = _softmax_rows(s, ok)
        o = _dot(p, vc_ref[...]) / jnp.maximum(den, 1e-30)
        for r in range(NSA_GROUP):
            h = g * NSA_GROUP + r
            o_ref[:, h * NSA_HD:(h + 1) * NSA_HD] = o[r:r + 1, :]


def _slc_sample(proj, cache4, page_table, idx_flat, val_flat, tcols, layer, nb_batch, lq, past):
    rows = NSA_BLOCK * 4
    n_past_blocks = past // NSA_BLOCK

    n_sel = NSA_KV_HEADS * NSA_TOPN
    blocks = jnp.minimum(idx_flat.reshape(nb_batch, lq * n_sel), n_past_blocks - 1)
    hit = (blocks // 2)[:, :, None] == jnp.arange(page_table.shape[1], dtype=blocks.dtype)
    pages = jnp.sum(jnp.where(hit, page_table[:, None, :], 0), axis=-1)
    loc = jnp.concatenate([pages.reshape(nb_batch * lq, n_sel), (blocks % 2).reshape(nb_batch * lq, n_sel)], axis=1)

    def page_spec(g, j):
        def index_map(b, i, loc, idx, val):
            row = b * lq + i
            return (layer, loc[row, g * NSA_TOPN + j], loc[row, n_sel + g * NSA_TOPN + j], 0)
        return pl.BlockSpec((None, None, rows, NSA_HD), index_map)

    grid_spec = pltpu.PrefetchScalarGridSpec(
        num_scalar_prefetch=3,
        grid=(nb_batch, lq),
        in_specs=[page_spec(g, j) for g in range(NSA_KV_HEADS) for j in range(NSA_TOPN)]
        + [pl.BlockSpec((lq, NSA_WIDTH), lambda b, i, *_: (b, C_NQ // NSA_WIDTH)),
           pl.BlockSpec((lq, 2 * NSA_KV_W), lambda b, i, *_: (b, C_SLC // (2 * NSA_KV_W))),
           pl.BlockSpec((NSA_KV_HEADS, SUBLANES, NUM_BUCKETS), lambda b, i, *_: (0, 0, 0))],
        out_specs=pl.BlockSpec((None, 1, NSA_WIDTH), lambda b, i, *_: (b * lq + i, 0, 0)),
        scratch_shapes=[pltpu.VMEM((SLOT_CHUNKS * NSA_BLOCK, NSA_HD), F32),
                        pltpu.VMEM((SLOT_CHUNKS * NSA_BLOCK, NSA_HD), F32)],
    )
    out = pl.pallas_call(
        functools.partial(_slc_sample_kernel, past=past, lq=lq),
        grid_spec=grid_spec,
        out_shape=jax.ShapeDtypeStruct((nb_batch * lq, 1, NSA_WIDTH), F32),
        compiler_params=_params("parallel", "arbitrary"),
        name="slc_sample",
    )(loc, idx_flat, val_flat, *([cache4] * n_sel), proj, proj, tcols)
    return out.reshape(nb_batch * lq, NSA_WIDTH)


def _win_sample_kernel(q_ref, wpast_ref, wnew_ref, tcol_ref, ocmp_ref, oslc_ref, z_ref, g0_ref, g1_ref,
                       o_ref, kc_ref, vc_ref, *, past, lq, wlen):
    rows = NSA_GROUP * lq
    nk = kc_ref.shape[0]
    lane = lax.broadcasted_iota(jnp.int32, (1, nk), 1)
    kpos = past - wlen + lane
    t = past + lax.broadcasted_iota(jnp.int32, (rows, 1), 0) % lq
    dist = t - kpos
    ok = (dist >= 0) & (dist <= WINDOW) & (kpos >= 0) & (lane < wlen + lq)
    gate_refs = (g0_ref, g1_ref)
    for g in range(NSA_KV_HEADS):
        kc_ref[0:wlen, :] = wpast_ref[pl.ds(g, wlen, stride=4), :]
        vc_ref[0:wlen, :] = wpast_ref[pl.ds(NSA_KV_HEADS + g, wlen, stride=4), :]
        kc_ref[wlen:, :] = jnp.zeros((nk - wlen, NSA_HD), F32)
        vc_ref[wlen:, :] = jnp.zeros((nk - wlen, NSA_HD), F32)
        kc_ref[wlen:wlen + lq, :] = wnew_ref[:, g * NSA_HD:(g + 1) * NSA_HD]
        vc_ref[wlen:wlen + lq, :] = wnew_ref[:, (NSA_KV_HEADS + g) * NSA_HD:(NSA_KV_HEADS + g + 1) * NSA_HD]
        q4 = jnp.concatenate([q_ref[:, (g * NSA_GROUP + r) * NSA_HD:(g * NSA_GROUP + r + 1) * NSA_HD]
                              for r in range(NSA_GROUP)], axis=0)
        s = _dot_nt(q4, kc_ref[...]) * SCALE + _bias_by_threshold(dist, tcol_ref[g])
        s = jnp.where(ok, s, NEG)
        p, den = _softmax_rows(s, ok)
        o_win = _dot(p, vc_ref[...]) / jnp.maximum(den, 1e-30)
        gates = jax.nn.sigmoid(gate_refs[g][...])
        for r in range(NSA_GROUP):
            h = g * NSA_GROUP + r
            sl = slice(h * NSA_HD, (h + 1) * NSA_HD)
            o = (gates[:, 3 * r:3 * r + 1] * ocmp_ref[:, sl] + gates[:, 3 * r + 1:3 * r + 2] * oslc_ref[:, sl]
                 + gates[:, 3 * r + 2:3 * r + 3] * o_win[r * lq:(r + 1) * lq])
            o_ref[:, sl] = o * _silu(z_ref[:, sl])


def _win_sample(proj, win4, tcols_rows, o_cmp, o_slc, layer, nb_batch, lq, past):
    wlen = win4.shape[2] // 4
    nk = -(-(wlen + lq) // LANES) * LANES
    row_spec = pl.BlockSpec((lq, NSA_WIDTH), lambda b: (b, 0))
    return pl.pallas_call(
        functools.partial(_win_sample_kernel, past=past, lq=lq, wlen=wlen),
        grid=(nb_batch,),
        in_specs=[pl.BlockSpec((lq, NSA_WIDTH), lambda b: (b, C_NQ // NSA_WIDTH)),
                  pl.BlockSpec((None, None, wlen * 4, NSA_HD), lambda b: (layer, b, 0, 0)),
                  pl.BlockSpec((lq, 2 * NSA_KV_W), lambda b: (b, C_WIN // (2 * NSA_KV_W))),
                  pl.BlockSpec((NSA_KV_HEADS, NSA_GROUP * lq, NUM_BUCKETS), lambda b: (0, 0, 0)),
                  row_spec, row_spec,
                  pl.BlockSpec((lq, NSA_WIDTH), lambda b: (b, C_NZ // NSA_WIDTH)),
                  pl.BlockSpec((lq, LANES), lambda b: (b, C_G0 // LANES)),
                  pl.BlockSpec((lq, LANES), lambda b: (b, C_G1 // LANES))],
        out_specs=row_spec,
        out_shape=jax.ShapeDtypeStruct((nb_batch * lq, NSA_WIDTH), F32),
        scratch_shapes=[pltpu.VMEM((nk, NSA_HD), F32), pltpu.VMEM((nk, NSA_HD), F32)],
        compiler_params=_params("parallel"),
        name="win_sample",
    )(proj, win4, proj, tcols_rows, o_cmp, o_slc, proj, proj, proj)


def _out_kernel(dn_ref, nsa_ref, x_ref, p_ref, wo_ref, gpost_ref, wpp_ref, gple_ref, wpg_ref, o_ref):
    half = dn_ref.shape[1]
    mixed = (jnp.dot(dn_ref[...].astype(BF16), wo_ref[0:half, :], preferred_element_type=F32)
             + jnp.dot(nsa_ref[...].astype(BF16), wo_ref[half:, :], preferred_element_type=F32))
    y = mixed * lax.rsqrt(jnp.mean(mixed * mixed, axis=-1, keepdims=True) + EPS) * gpost_ref[...]
    x1 = x_ref[...] + y
    e = jnp.dot(p_ref[...].astype(BF16), wpp_ref[...], preferred_element_type=F32)
    e = e * lax.rsqrt(jnp.mean(e * e, axis=-1, keepdims=True) + EPS) * gple_ref[...]
    gate = jax.nn.sigmoid(jnp.dot(x1.astype(BF16), wpg_ref[...], preferred_element_type=F32))
    o_ref[...] = x1 + gate * e


def _out_proj(dn, nsa, x, p_all, wo_all, gpost_all, wpp_all, gple_all, wpg_all, layer, tm):
    m = x.shape[0]
    once = pl.Buffered(1)

    def wspec(k):
        return pl.BlockSpec((None, k, D_MODEL), lambda i: (layer, 0, 0), pipeline_mode=once)

    return pl.pallas_call(
        _out_kernel,
        grid=(m // tm,),
        in_specs=[pl.BlockSpec((tm, dn.shape[1]), lambda i: (i, 0)),
                  pl.BlockSpec((tm, nsa.shape[1]), lambda i: (i, 0)),
                  pl.BlockSpec((tm, D_MODEL), lambda i: (i, 0)),
                  pl.BlockSpec((None, tm, PLE_DIM), lambda i: (layer, i, 0)),
                  wspec(D_MODEL), wspec(1), wspec(PLE_DIM), wspec(1), wspec(D_MODEL)],
        out_specs=pl.BlockSpec((tm, D_MODEL), lambda i: (i, 0)),
        out_shape=jax.ShapeDtypeStruct((m, D_MODEL), F32),
        compiler_params=_params("parallel"),
        name="out_proj",
    )(dn, nsa, x, p_all, wo_all, gpost_all, wpp_all, gple_all, wpg_all)


def _kv_rows_kernel(*refs, depth):
    x_refs = refs[:depth]
    cmp_ref, slc_ref, win_ref = refs[depth:]
    tm = x_refs[0].shape[0]
    width = 2 * NSA_KV_W

    def rows(o_ref, x_ref, seg):
        for s in range(4):
            o_ref[pl.ds(s, tm, stride=4), :] = x_ref[:, seg * width + s * NSA_HD:seg * width + (s + 1) * NSA_HD]

    for layer in range(depth):
        @pl.when(pl.program_id(0) == layer)
        def _(x_ref=x_refs[layer]):
            rows(cmp_ref, x_ref, 0)
            rows(slc_ref, x_ref, 1)

            @pl.when(pl.program_id(2) == pl.num_programs(2) - 1)
            def _():
                rows(win_ref, x_ref, 2)


def _kv_rows(projs, nb, seq):
    depth = len(projs)
    tm = min(WINDOW, seq)
    nr = seq // tm
    width = 3 * 2 * NSA_KV_W

    def x_spec(layer):
        return pl.BlockSpec((tm, width), lambda d, b, r: (jnp.where(d == layer, b * nr + r, 0), C_CMP // width))

    return pl.pallas_call(
        functools.partial(_kv_rows_kernel, depth=depth),
        grid=(depth, nb, nr),
        in_specs=[x_spec(layer) for layer in range(depth)],
        out_specs=[pl.BlockSpec((None, tm * 4, NSA_HD), lambda d, b, r: (d, b * nr + r, 0)),
                   pl.BlockSpec((None, tm * 4, NSA_HD), lambda d, b, r: (d, b * nr + r, 0)),
                   pl.BlockSpec((None, None, tm * 4, NSA_HD), lambda d, b, r: (d, b, 0, 0))],
        out_shape=[jax.ShapeDtypeStruct((depth, nb * seq * 4, NSA_HD), F32),
                   jax.ShapeDtypeStruct((depth, nb * seq * 4, NSA_HD), F32),
                   jax.ShapeDtypeStruct((depth, nb, tm * 4, NSA_HD), F32)],
        compiler_params=_params("arbitrary", "arbitrary", "arbitrary"),
        name="kv_rows",
    )(*projs)


def _win_state_kernel(*refs, depth, lq, keep):
    cache_ref = refs[0]
    x_refs = refs[1:1 + depth]
    o_ref = refs[1 + depth]
    total = cache_ref.shape[0]
    o_ref[0:keep * 4, :] = cache_ref[total - keep * 4:total, :]
    for layer in range(depth):
        @pl.when(pl.program_id(0) == layer)
        def _(x_ref=x_refs[layer]):
            for s in range(4):
                o_ref[pl.ds(keep * 4 + s, lq, stride=4), :] = x_ref[:, s * NSA_HD:(s + 1) * NSA_HD]


def _win_state(win4, projs, nb, lq):
    depth = len(projs)
    rows = win4.shape[2]
    wlen = rows // 4
    keep = wlen - lq
    width = 2 * NSA_KV_W

    def x_spec(layer):
        return pl.BlockSpec((lq, width), lambda d, b: (jnp.where(d == layer, b, 0), C_WIN // width))

    return pl.pallas_call(
        functools.partial(_win_state_kernel, depth=depth, lq=lq, keep=keep),
        grid=(depth, nb),
        in_specs=[pl.BlockSpec((None, None, rows, NSA_HD), lambda d, b: (d, b, 0, 0))]
        + [x_spec(layer) for layer in range(depth)],
        out_specs=pl.BlockSpec((None, None, rows, NSA_HD), lambda d, b: (d, b, 0, 0)),
        out_shape=jax.ShapeDtypeStruct((depth, nb, rows, NSA_HD), F32),
        compiler_params=_params("arbitrary", "arbitrary"),
        name="win_state",
    )(win4, *projs)


W_TILE = 512


def _relayout_w_in_kernel(wt_ref, sm_ref, gate_ref, o_ref, tail_ref):
    is_tail = pl.program_id(1) == pl.num_programs(1) - 1

    @pl.when(jnp.logical_not(is_tail))
    def _():
        o_ref[...] = wt_ref[0].T.astype(BF16)

    @pl.when(is_tail)
    def _():
        half = 3 * NSA_GROUP
        tail_ref[...] = jnp.zeros(tail_ref.shape, F32)
        tail_ref[0:2 * DN_HEADS, :] = sm_ref[0]
        tail_ref[C_G0 - C_SM:C_G0 - C_SM + half, :] = gate_ref[0, 0:half, :]
        tail_ref[C_G1 - C_SM:C_G1 - C_SM + half, :] = gate_ref[0, half:2 * half, :]
        o_ref[...] = tail_ref[...].T.astype(BF16)


def _relayout_w_in(w_in):
    depth, k, _ = w_in.shape
    wt = jnp.swapaxes(w_in, 1, 2)
    n_gate = 2 * 3 * NSA_GROUP
    n_tiles = N_PAD // W_TILE
    moved = C_NQ // W_TILE

    def src_row(d, j):
        j = jnp.minimum(j, n_tiles - 2)
        return (d, pl.multiple_of(j * W_TILE + jnp.where(j >= moved, _O_NQ - C_NQ, 0), SUBLANES), 0)

    return pl.pallas_call(
        _relayout_w_in_kernel,
        grid=(depth, n_tiles),
        in_specs=[pl.BlockSpec((pl.Element(1), pl.Element(W_TILE), pl.Element(k)), src_row),
                  pl.BlockSpec((pl.Element(1), pl.Element(2 * DN_HEADS), pl.Element(k)), lambda d, j: (d, _O_BETA, 0)),
                  pl.BlockSpec((pl.Element(1), pl.Element(n_gate), pl.Element(k)), lambda d, j: (d, _O_GATE, 0))],
        out_specs=pl.BlockSpec((None, k, W_TILE), lambda d, j: (d, 0, j)),
        out_shape=jax.ShapeDtypeStruct((depth, k, N_PAD), BF16),
        scratch_shapes=[pltpu.VMEM((W_TILE, k), F32)],
        compiler_params=_params("parallel", "arbitrary"),
        name="relayout_w_in",
    )(wt, wt, wt)


def _bias_expand_kernel(t_ref, bucket_ref, o_ref):
    bucket = bucket_ref[...]
    for h in range(NSA_HEADS):
        acc = jnp.full(bucket.shape, t_ref[0, h], F32)
        for b in range(1, NUM_BUCKETS):
            acc = jnp.where(bucket >= b, t_ref[b, h], acc)
        o_ref[h] = acc


def _bias_expand(table, bucket):
    rows, cols = bucket.shape
    tr = max(t for t in range(SUBLANES, min(rows, 512) + 1, SUBLANES) if rows % t == 0)
    return pl.pallas_call(
        _bias_expand_kernel,
        grid=(rows // tr,),
        in_specs=[pl.BlockSpec(memory_space=pltpu.SMEM), pl.BlockSpec((tr, cols), lambda i: (i, 0))],
        out_specs=pl.BlockSpec((NSA_HEADS, tr, cols), lambda i: (0, i, 0)),
        out_shape=jax.ShapeDtypeStruct((NSA_HEADS, rows, cols), F32),
        compiler_params=_params("parallel"),
        name="bias_expand",
    )(table, jnp.asarray(bucket))


def _bias_tiles(table):
    i = np.arange(BT)
    d = np.arange(N_BIAS_TILES)[:, None, None] * BT + i[None, :, None] - i[None, None, :]
    tiles = _bias_expand(table * LOG2E, _bucket_np(d).reshape(N_BIAS_TILES * BT, BT))
    return tiles.reshape(NSA_HEADS, N_BIAS_TILES, BT, BT)


def _bias_cmp(table, pos0, lq, nb):
    dist = (pos0 + np.arange(lq))[:, None] - (np.arange(nb) * NSA_BLOCK + NSA_BLOCK - 1)[None, :]
    return _bias_expand(table, _bucket_np(dist))


def kernel(x_prompt, x_sample, state_dn_S, state_dn_conv, cache_win_kv, cache_cmp_kv, cache_slc_kv, page_table,
           p_prompt, p_sample, rel_bias_table, w_in, w_out, g_pre, g_post, dn_conv_w, dn_A_log, dn_dt_bias,
           dn_norm_w, cmp_pos_w, cmp_w1, cmp_w2, w_ple_proj, g_ple, w_ple_gate):
    depth = w_in.shape[0]
    bp, seq, _ = x_prompt.shape
    bs, lq, _ = x_sample.shape
    n_pages = page_table.shape[1]
    past = n_pages * PAGE_SIZE
    n_pool = cache_cmp_kv.shape[1]
    wlen = cache_win_kv.shape[2]
    mp, ms = bp * seq, bs * lq
    assert seq % PK == 0 and seq % DN_CHUNK == 0 and lq == SUBLANES and past % NSA_BLOCK == 0
    assert wlen == WINDOW <= past
    kv_row = (2, NSA_KV_HEADS, NSA_HD)

    w_in_r = _relayout_w_in(w_in)
    w_out_b = w_out.astype(BF16)
    wpp_b = w_ple_proj.astype(BF16)
    wpg_b = w_ple_gate.astype(BF16)
    g_pre3 = g_pre[:, None, :]
    g_post3 = g_post[:, None, :]
    g_ple3 = g_ple[:, None, :]
    nw3 = dn_norm_w[:, None, :]
    lane_pad = ((0, 0), (DN_HEADS, LANES - 2 * DN_HEADS))
    dn_par = jnp.stack([jnp.pad(dn_A_log, lane_pad), jnp.pad(dn_dt_bias, lane_pad)], axis=1)
    pw_prompt = jnp.broadcast_to(cmp_pos_w[:, :, None, :, None], (depth, 2, NSA_KV_HEADS, NSA_BLOCK, NSA_HD))
    pw_prompt = jnp.transpose(pw_prompt, (0, 3, 1, 2, 4)).reshape(depth, NSA_BLOCK, 2 * NSA_KV_W)
    pw_page = jnp.broadcast_to(cmp_pos_w[:, None, None, :, :, None],
                               (depth, PAGE_SIZE // NSA_BLOCK, NSA_KV_HEADS, 2, NSA_BLOCK, NSA_HD))
    pw_page = jnp.transpose(pw_page, (0, 1, 4, 3, 2, 5)).reshape(depth, PAGE_SIZE * 4, NSA_HD)
    table = rel_bias_table.astype(F32)
    bias_tiles = _bias_tiles(table)
    nb_p = seq // NSA_BLOCK
    nb_s = past // NSA_BLOCK
    nbs_s = -(-(past + lq) // NSA_BLOCK)
    nbp_s = -(-nbs_s // LANES) * LANES
    bias_cmp_p = _bias_cmp(table, 0, seq, nb_p)
    bias_cmp_s = _bias_cmp(table, past, lq, nb_s)
    tcols = jnp.pad(table.T.reshape(NSA_KV_HEADS, NSA_GROUP, NUM_BUCKETS),
                    ((0, 0), (0, SUBLANES - NSA_GROUP), (0, 0)))
    tcols_rows = jnp.repeat(table.T.reshape(NSA_KV_HEADS, NSA_GROUP, NUM_BUCKETS), lq, axis=1)

    conv0_s = jnp.pad(state_dn_conv, ((0, 0), (0, 0), (SUBLANES - (CONV_W - 1), 0), (0, 0)))
    conv0_p = jnp.zeros((1, bp, SUBLANES, DN_CONV_CH), F32)
    s0_p = jnp.zeros((1, bp, DN_HEADS, DN_DK, DN_DV), F32)
    cmp4 = cache_cmp_kv.reshape(depth, n_pool, PAGE_SIZE * 4, NSA_HD)
    slc4 = cache_slc_kv.reshape(depth, n_pool, PAGE_SIZE * 4, NSA_HD)
    win4 = cache_win_kv.reshape(depth, bs, wlen * 4, NSA_HD)
    p_prompt3 = p_prompt.reshape(depth, mp, PLE_DIM)
    p_sample3 = p_sample.reshape(depth, ms, PLE_DIM)

    xp = x_prompt.reshape(mp, D_MODEL)
    xs = x_sample.reshape(ms, D_MODEL)
    outs_p = [[] for _ in range(2)]
    outs_s = [[] for _ in range(4)]
    projs_p, projs_s = [], []
    tq_p = min(256, seq)
    tm_p = min(IN_TM, mp)
    pages = min(POOL_PAGES, n_pages)
    fuse_pool = (mp // tm_p, N_PAD // IN_TN) == (bs, n_pages // pages)
    for i in range(depth):
        if fuse_pool:
            proj, pooled = _in_proj_pool(xp, g_pre3, w_in_r, cmp4, page_table, pw_page, i, tm_p, IN_TN, pages)
        else:
            proj = _in_proj(xp, g_pre3, w_in_r, i, tm_p)
            pooled = _pool_paged(cmp4, page_table, pw_page, i, pages)
        dn_o, conv_n, s_n = _deltanet(proj, bp, seq, DN_CHUNK, conv0_p, s0_p, 0, dn_conv_w, dn_par, nw3, i)
        o_cmp, sel = _cmp_attn(proj, _pool_prompt(proj, bp, seq, pw_prompt, i), cmp_w1, cmp_w2, i, bias_cmp_p,
                               bp, seq, 0, tq_p, nb_p, nb_p, True)
        nsa_o = _nsa_prompt(proj, sel, bias_tiles, o_cmp, bp, seq)
        xp = _out_proj(dn_o, nsa_o, xp, p_prompt3, w_out_b, g_post3, wpp_b, g_ple3, wpg_b, i, min(256, mp))
        projs_p.append(proj)
        outs_p[0].append(s_n)
        outs_p[1].append(conv_n[:, SUBLANES - (CONV_W - 1):])

        proj = _in_proj(xs, g_pre3, w_in_r, i, ms)
        dn_o, conv_n, s_n = _deltanet(proj, bs, lq, lq, conv0_s, state_dn_S, i, dn_conv_w, dn_par, nw3, i)
        o_cmp, score = _cmp_attn(proj, pooled, cmp_w1, cmp_w2, i, bias_cmp_s, bs, lq, past, lq, nbs_s, nbp_s, False)
        score_t = jnp.transpose(score.reshape(ms * NSA_KV_HEADS, nbp_s))
        _, idx_t, val_t = _topk(score_t[:-(-nbs_s // SUBLANES) * SUBLANES], nbs_s)
        o_slc = _slc_sample(proj, slc4, page_table, idx_t.T.reshape(-1), val_t.T.reshape(-1), tcols, i,
                            bs, lq, past)
        nsa_o = _win_sample(proj, win4, tcols_rows, o_cmp, o_slc, i, bs, lq, past)
        xs = _out_proj(dn_o, nsa_o, xs, p_sample3, w_out_b, g_post3, wpp_b, g_ple3, wpg_b, i, ms)
        kv_all = proj[:, C_CMP:C_WIN].reshape(bs, lq, 2, *kv_row)
        projs_s.append(proj)
        outs_s[0].append(s_n)
        outs_s[1].append(conv_n[:, SUBLANES - (CONV_W - 1):])
        outs_s[2].append(kv_all[:, :, 0])
        outs_s[3].append(kv_all[:, :, 1])

    p_cmp, p_slc, p_win = _kv_rows(projs_p, bp, seq)
    s_win = _win_state(win4, projs_s, bs, lq).reshape(depth, bs, wlen, *kv_row)
    s_S, s_conv, s_cmp, s_slc = (jnp.stack(o) for o in outs_s)
    return (xp.reshape(bp, seq, D_MODEL), xs.reshape(bs, lq, D_MODEL), jnp.stack(outs_p[0]), jnp.stack(outs_p[1]),
            p_win.reshape(depth, bp, min(WINDOW, seq), *kv_row), p_cmp.reshape(depth, bp, seq, *kv_row),
            p_slc.reshape(depth, bp, seq, *kv_row), s_S, s_conv, s_win, s_cmp, s_slc)
```
